```python
import math
import jax, jax.numpy as jnp
from jax import lax
import numpy as np

D_MODEL = 1024
BATCH = 8
SEQ = 8192
DEPTH = 1

PLE_DIM = 256
C_RW = D_MODEL // 2
RW_HEAD = 64
RW_HEADS = C_RW // RW_HEAD
DECAY_LORA = 64
AAA_LORA = 64
GATE_LORA = 128
RW_COLS = 3 * C_RW + 2 * DECAY_LORA + 2 * AAA_LORA + GATE_LORA
RW_GN_EPS = RW_HEAD * 1e-5
C_DA = D_MODEL - C_RW
DA_HEADS = 4
DA_V_DIM = C_DA // DA_HEADS
DA_QK_DIM = DA_V_DIM // 2
DA_COLS = 3 * C_DA
IN_COLS = RW_COLS + DA_COLS
D_MIX = C_RW + C_DA
Q_BLOCK = 128
N_EXPERTS = 32
TOP_K = 4
D_FF = D_MODEL
SWIGLU_ALPHA = 1.702
SWIGLU_LIMIT = 7.0
EXPERT_BLOCK = 128
NORM_EPS = 1e-6

kernel_name = "hybrid_rwkv7_diffattn_moe_encoder"


def rmsnorm(x, g, eps=NORM_EPS):
    xf = x.astype(jnp.float32)
    y = xf * lax.rsqrt(jnp.mean(xf * xf, axis=-1, keepdims=True) + eps)
    return y * g.astype(jnp.float32)


def centred_shift(z):
    zp = jnp.pad(z, ((0, 0), (1, 1), (0, 0)))
    return 0.5 * (zp[:, :-2] + zp[:, 2:])


def rwkv7_bidirectional(z, mu, w0, w2, a0, a2, g2, k_k, k_a, r_k, ln_g, ln_b):
    B, T, _ = z.shape
    H, N = RW_HEADS, RW_HEAD
    z = z.astype(jnp.float32)
    z = z + (centred_shift(z) - z) * mu
    idx = [C_RW, 2 * C_RW, 3 * C_RW, 3 * C_RW + 2 * DECAY_LORA,
           3 * C_RW + 2 * DECAY_LORA + 2 * AAA_LORA]
    r, k, v, wd, ad, gd = jnp.split(z, idx, axis=-1)
    wd = wd.reshape(B, T, 2, DECAY_LORA)
    ad = ad.reshape(B, T, 2, AAA_LORA)
    w_log = -jax.nn.softplus(-(w0 + jnp.einsum('btdr,drc->btdc', jnp.tanh(wd), w2))) - 0.5
    decay = jnp.exp(-jnp.exp(w_log))
    a = jax.nn.sigmoid(a0 + jnp.einsum('btdr,drc->btdc', ad, a2))
    g = jax.nn.sigmoid(gd) @ g2
    kk = (k * k_k).reshape(B, T, H, N)
    kk = kk * lax.rsqrt(jnp.sum(kk * kk, axis=-1, keepdims=True) + 1e-12)
    kk = kk.reshape(B, T, C_RW)
    k_dir = k[:, :, None, :] * (1.0 + (a - 1.0) * k_a)
    b_dir = kk[:, :, None, :] * a

    def to_time_major(x_fwd, x_bwd):
        s = jnp.stack([x_fwd, jnp.flip(x_bwd, axis=1)], axis=0).astype(jnp.float32)
        return s.reshape(2, B, T, H, N).transpose(2, 0, 1, 3, 4)

    xs = (to_time_major(r, r),
          to_time_major(decay[:, :, 0], decay[:, :, 1]),
          to_time_major(k_dir[:, :, 0], k_dir[:, :, 1]),
          to_time_major(v, v),
          to_time_major(-kk, -kk),
          to_time_major(b_dir[:, :, 0], b_dir[:, :, 1]))

    def step(S, inp):
        r_t, w_t, k_t, v_t, a_t, b_t = inp
        sa = jnp.einsum('dbhvk,dbhk->dbhv', S, a_t)
        S = S * w_t[..., None, :] + sa[..., None] * b_t[..., None, :] + v_t[..., None] * k_t[..., None, :]
        y = jnp.einsum('dbhvk,dbhk->dbhv', S, r_t)
        return S, y

    S0 = jnp.zeros((2, B, H, N, N), jnp.float32)
    _, ys = lax.scan(step, S0, xs)
    y = ys[:, 0] + jnp.flip(ys[:, 1], axis=0)
    y = y.transpose(1, 0, 2, 3)
    mean = jnp.mean(y, axis=-1, keepdims=True)
    var = jnp.mean(jnp.square(y - mean), axis=-1, keepdims=True)
    y = (y - mean) * lax.rsqrt(var + RW_GN_EPS) * ln_g + ln_b
    k_bonus = jnp.mean(k_dir, axis=2).reshape(B, T, H, N)
    rh = r.reshape(B, T, H, N)
    vh = v.reshape(B, T, H, N)
    bonus = jnp.sum(rh * k_bonus * r_k, axis=-1, keepdims=True) * vh
    return (y + bonus).reshape(B, T, C_RW) * g


def diff_attention(zq, zk, zv, q_norm_g, k_norm_g, lq1, lk1, lq2, lk2, subln_g, lambda_init):
    B, T, _ = zq.shape
    H, D, E = DA_HEADS, DA_QK_DIM, DA_V_DIM
    q = rmsnorm(zq.reshape(B, T, H, 2, D), q_norm_g)
    k = rmsnorm(zk.reshape(B, T, H, 2, D), k_norm_g)
    v = zv.astype(jnp.float32).reshape(B, T, H, E)
    lq1, lk1, lq2, lk2 = (t.astype(jnp.float32) for t in (lq1, lk1, lq2, lk2))
    lam = jnp.exp(jnp.sum(lq1 * lk1)) - jnp.exp(jnp.sum(lq2 * lk2)) + lambda_init
    scale = 1.0 / math.sqrt(D)
    slopes = 2.0 ** (-8.0 * jnp.arange(1, H + 1, dtype=jnp.float32) / H)
    n_blocks = T // Q_BLOCK
    qb = q.reshape(B, n_blocks, Q_BLOCK, H, 2, D).transpose(1, 0, 2, 3, 4, 5)
    pos_k = jnp.arange(T, dtype=jnp.int32)

    def block(args):
        q_blk, bi = args
        pos_q = bi * Q_BLOCK + jnp.arange(Q_BLOCK, dtype=jnp.int32)
        dist = jnp.abs(pos_q[:, None] - pos_k[None, :]).astype(jnp.float32)
        s = jnp.einsum('bqhcd,bkhcd->bhcqk', q_blk, k) * scale
        s = s - slopes[:, None, None, None] * dist
        pm = jax.nn.softmax(s, axis=-1)
        attn = pm[:, :, 0] - lam * pm[:, :, 1]
        return jnp.einsum('bhqk,bkhe->bqhe', attn, v)

    o = lax.map(block, (qb, jnp.arange(n_blocks, dtype=jnp.int32)))
    o = o.transpose(1, 0, 2, 3, 4).reshape(B, T, H, E)
    o = rmsnorm(o, subln_g) * (1.0 - lambda_init)
    return o.reshape(B, T, H * E)


def routed_experts(xs, ids, gates, w1, b1, w2, b2):
    Tn = xs.shape[0]
    M = Tn * TOP_K
    e_flat = ids.reshape(M)
    g_flat = gates.reshape(M)
    order = jnp.argsort(e_flat)
    se = e_flat[order]
    stok = (order // TOP_K).astype(jnp.int32)
    sg = g_flat[order]
    counts = jnp.bincount(e_flat, length=N_EXPERTS)
    padded = (counts + EXPERT_BLOCK - 1) // EXPERT_BLOCK * EXPERT_BLOCK
    pend = jnp.cumsum(padded)
    pstart = pend - padded
    cstart = jnp.cumsum(counts) - counts
    dest = pstart[se] + jnp.arange(M, dtype=jnp.int32) - cstart[se]
    n_blk = (M + N_EXPERTS * (EXPERT_BLOCK - 1) + EXPERT_BLOCK - 1) // EXPERT_BLOCK
    P = n_blk * EXPERT_BLOCK
    row_tok = jnp.zeros((P,), jnp.int32).at[dest].set(stok)
    row_gate = jnp.zeros((P,), jnp.float32).at[dest].set(sg.astype(jnp.float32))
    blk_start = jnp.arange(n_blk, dtype=pend.dtype) * EXPERT_BLOCK
    blk_e = jnp.minimum(jnp.searchsorted(pend, blk_start, side='right'), N_EXPERTS - 1)
    xb = xs[row_tok].reshape(n_blk, EXPERT_BLOCK, xs.shape[-1])

    def expert_block(args):
        x_blk, e = args
        hcat = x_blk @ w1[e] + b1[e]
        glu = jnp.minimum(hcat[:, :D_FF], SWIGLU_LIMIT)
        lin = jnp.clip(hcat[:, D_FF:], -SWIGLU_LIMIT, SWIGLU_LIMIT)
        act = glu * jax.nn.sigmoid(SWIGLU_ALPHA * glu) * (lin + 1.0)
        return (act @ w2[e] + b2[e]).astype(jnp.float32)

    yb = lax.map(expert_block, (xb, blk_e)).reshape(P, -1)
    return jnp.zeros((Tn, yb.shape[-1]), jnp.float32).at[row_tok].add(yb * row_gate[:, None])


def moe(xn, w_router, b_router, w1, b1, w2, b2):
    logits = (xn @ w_router + b_router).astype(jnp.float32)
    vals, idx = lax.top_k(logits, TOP_K)
    gates = jax.nn.softmax(vals, axis=-1)

    def per_seq(args):
        xs, ids, gs = args
        return routed_experts(xs, ids, gs, w1, b1, w2, b2)

    return lax.map(per_seq, (xn, idx.astype(jnp.int32), gates))


def setup_inputs(seed: int = 0) -> dict:
    key = jax.random.key(seed)
    ks = iter(jax.random.split(key, 48))
    L = DEPTH
    f32 = jnp.float32

    def nrm(shape, scale):
        return jax.random.normal(next(ks), shape, f32) * scale

    def gain(shape):
        return 1.0 + nrm(shape, 0.02)

    return {
        "x": nrm((BATCH, SEQ, D_MODEL), 1.0),
        "p": nrm((DEPTH, BATCH, SEQ, PLE_DIM), 1.0),
        "norm_mix_g": gain((L, D_MODEL)),
        "w_in": nrm((L, D_MODEL, IN_COLS), D_MODEL ** -0.5),
        "rw_mu": jax.random.uniform(next(ks), (L, RW_COLS), f32, 0.0, 1.0),
        "rw_w0": jax.random.uniform(next(ks), (L, 2, C_RW), f32, -6.0, -1.0),
        "rw_w2": nrm((L, 2, DECAY_LORA, C_RW), 0.1 * DECAY_LORA ** -0.5),
        "rw_a0": nrm((L, 2, C_RW), 0.1),
        "rw_a2": nrm((L, 2, AAA_LORA, C_RW), 0.1 * AAA_LORA ** -0.5),
        "rw_g2": nrm((L, GATE_LORA, C_RW), GATE_LORA ** -0.5),
        "rw_k_k": 0.85 + nrm((L, C_RW), 0.02),
        "rw_k_a": gain((L, C_RW)),
        "rw_r_k": nrm((L, RW_HEADS, RW_HEAD), 0.1),
        "rw_ln_g": gain((L, RW_HEADS, RW_HEAD)),
        "rw_ln_b": nrm((L, RW_HEADS, RW_HEAD), 0.01),
        "da_q_norm_g": gain((L, DA_QK_DIM)),
        "da_k_norm_g": gain((L, DA_QK_DIM)),
        "da_lq1": nrm((L, DA_QK_DIM), 0.1),
        "da_lk1": nrm((L, DA_QK_DIM), 0.1),
        "da_lq2": nrm((L, DA_QK_DIM), 0.1),
        "da_lk2": nrm((L, DA_QK_DIM), 0.1),
        "da_subln_g": gain((L, DA_V_DIM)),
        "w_out": nrm((L, D_MIX, D_MODEL), D_MIX ** -0.5),
        "norm_ffn_g": gain((L, D_MODEL)),
        "w_router": nrm((L, D_MODEL, N_EXPERTS), D_MODEL ** -0.5),
        "b_router": nrm((L, N_EXPERTS), 0.01),
        "w1": nrm((L, N_EXPERTS, D_MODEL, 2 * D_FF), D_MODEL ** -0.5),
        "b1": nrm((L, N_EXPERTS, 2 * D_FF), 0.01),
        "w2": nrm((L, N_EXPERTS, D_FF, D_MODEL), D_FF ** -0.5),
        "b2": nrm((L, N_EXPERTS, D_MODEL), 0.01),
        "w_ple": nrm((L, PLE_DIM, D_MODEL), PLE_DIM ** -0.5),
        "ple_norm_g": gain((L, D_MODEL)),
        "w_ple_gate": nrm((L, D_MODEL, D_MODEL), D_MODEL ** -0.5),
    }


def reference(x, p, norm_mix_g, w_in, rw_mu, rw_w0, rw_w2, rw_a0, rw_a2, rw_g2,
              rw_k_k, rw_k_a, rw_r_k, rw_ln_g, rw_ln_b, da_q_norm_g, da_k_norm_g,
              da_lq1, da_lk1, da_lq2, da_lk2, da_subln_g, w_out, norm_ffn_g,
              w_router, b_router, w1, b1, w2, b2, w_ple, ple_norm_g, w_ple_gate):
    h = x.astype(jnp.float32)
    for i in range(DEPTH):
        lambda_init = 0.8 - 0.6 * math.exp(-0.3 * i)
        u = rmsnorm(h, norm_mix_g[i]).astype(x.dtype)
        z = u @ w_in[i]
        z_rw = z[..., :RW_COLS]
        z_q = z[..., RW_COLS:RW_COLS + C_DA]
        z_k = z[..., RW_COLS + C_DA:RW_COLS + 2 * C_DA]
        z_v = z[..., RW_COLS + 2 * C_DA:]
        o_rw = rwkv7_bidirectional(z_rw, rw_mu[i], rw_w0[i], rw_w2[i], rw_a0[i], rw_a2[i],
                                   rw_g2[i], rw_k_k[i], rw_k_a[i], rw_r_k[i],
                                   rw_ln_g[i], rw_ln_b[i])
        o_da = diff_attention(z_q, z_k, z_v, da_q_norm_g[i], da_k_norm_g[i],
                              da_lq1[i], da_lk1[i], da_lq2[i], da_lk2[i],
                              da_subln_g[i], lambda_init)
        o = jnp.concatenate([o_rw, o_da], axis=-1).astype(x.dtype)
        h = h + (o @ w_out[i]).astype(jnp.float32)
        m = rmsnorm(h, norm_ffn_g[i]).astype(x.dtype)
        h = h + moe(m, w_router[i], b_router[i], w1[i], b1[i], w2[i], b2[i])
        e = rmsnorm(p[i] @ w_ple[i], ple_norm_g[i])
        gate = jax.nn.sigmoid((h.astype(x.dtype) @ w_ple_gate[i]).astype(jnp.float32))
        h = h + e * gate
    return h.astype(x.dtype)
```

```python
import functools
import math

import jax
import jax.numpy as jnp
from jax import lax
from jax.experimental import pallas as pl
from jax.experimental.pallas import tpu as pltpu

F32 = jnp.float32
BF16 = jnp.bfloat16

V7X_LANES = 128
V7X_VMEM_BYTES = 64 * 1024 * 1024
VMEM_LIMIT = 52 * 1024 * 1024

NORM_EPS = 1e-6
RW_HEAD = 64
CHUNK = 64
HEAD_GROUP = 256
N_EXPERTS = 32
TOP_K = 4
SWIGLU_ALPHA = 1.702
SWIGLU_LIMIT = 7.0


def _cparams(sem, **kw):
    return pltpu.CompilerParams(dimension_semantics=sem, vmem_limit_bytes=VMEM_LIMIT, **kw)


def _dot(a, b):
    return jnp.dot(a, b, preferred_element_type=F32)


def _split3(x):
    hi = x.astype(BF16)
    r1 = x - hi.astype(F32)
    mid = r1.astype(BF16)
    lo = (r1 - mid.astype(F32)).astype(BF16)
    return hi, mid, lo


def _dot_exact_rhs(x, g_bf16):
    hi, mid, lo = _split3(x)
    return _dot(hi, g_bf16) + _dot(mid, g_bf16) + _dot(lo, g_bf16)


def _dot_exact_lhs(g_bf16, x):
    hi, mid, lo = _split3(x)
    return _dot(g_bf16, hi) + _dot(g_bf16, mid) + _dot(g_bf16, lo)


def _dot_f32(a, b):
    ah, am, al = _split3(a)
    bh, bm, bl = _split3(b)
    return (_dot(ah, bh) + (_dot(ah, bm) + _dot(am, bh))
            + (_dot(am, bm) + _dot(ah, bl) + _dot(al, bh)))


def _group_matrix(width, group, value):
    r = lax.broadcasted_iota(jnp.int32, (width, width), 0) // group
    c = lax.broadcasted_iota(jnp.int32, (width, width), 1) // group
    return jnp.where(r == c, value, 0.0).astype(BF16)


def _inproj_kernel(x_ref, g_ref, w_ref, gq_ref, gk_ref, gm_ref, zrw_ref, q_ref, k_ref, v_ref, *, rw_cols, c_da, scale):
    x = x_ref[...]
    ms = jnp.mean(x * x, axis=-1, keepdims=True)
    u = (x * lax.rsqrt(ms + NORM_EPS) * g_ref[...]).astype(BF16)
    z = _dot(u, w_ref[...])
    zrw_ref[...] = z[:, :rw_cols]
    zq = z[:, rw_cols:rw_cols + c_da]
    zk = z[:, rw_cols + c_da:rw_cols + 2 * c_da]
    zv = z[:, rw_cols + 2 * c_da:]
    gm = gm_ref[...]
    msq = _dot_exact_rhs(zq * zq, gm)
    msk = _dot_exact_rhs(zk * zk, gm)
    q_ref[...] = (zq * lax.rsqrt(msq + NORM_EPS) * (gq_ref[...] * scale)).astype(BF16)
    k_ref[...] = (zk * lax.rsqrt(msk + NORM_EPS) * gk_ref[...]).astype(BF16)
    v_ref[...] = zv.astype(BF16)


def _inproj(x2, g, w_bf, gq_t, gk_t, gm, rw_cols, c_da, scale, tm):
    n, d = x2.shape
    cols = w_bf.shape[1]
    full = lambda i: (0, 0)
    row = lambda i: (i, 0)
    return pl.pallas_call(
        functools.partial(_inproj_kernel, rw_cols=rw_cols, c_da=c_da, scale=scale),
        grid=(n // tm,),
        in_specs=[
            pl.BlockSpec((tm, d), row),
            pl.BlockSpec((1, d), full),
            pl.BlockSpec((d, cols), full),
            pl.BlockSpec((1, c_da), full),
            pl.BlockSpec((1, c_da), full),
            pl.BlockSpec((c_da, c_da), full),
        ],
        out_specs=[
            pl.BlockSpec((tm, rw_cols), row),
            pl.BlockSpec((tm, c_da), row),
            pl.BlockSpec((tm, c_da), row),
            pl.BlockSpec((tm, c_da), row),
        ],
        out_shape=[
            jax.ShapeDtypeStruct((n, rw_cols), F32),
            jax.ShapeDtypeStruct((n, c_da), BF16),
            jax.ShapeDtypeStruct((n, c_da), BF16),
            jax.ShapeDtypeStruct((n, c_da), BF16),
        ],
        compiler_params=_cparams(("parallel",)),
        name="inproj",
    )(x2, g, w_bf, gq_t, gk_t, gm)


def _rwkv_prep_kernel(z_ref, zp_ref, zn_ref, mu_ref, w2_ref, w0_ref, a2_ref, a0_ref, g2_ref, kk_ref, ka_ref,
                      rk_ref, gsum_ref,
                      r_ref, v_ref, nkk_ref, lw_ref, kd_ref, bd_ref, g_ref, bonus_ref, *, c_rw, lora):
    i = pl.program_id(1)
    nt = pl.num_programs(1)
    z = z_ref[...]
    tt = z.shape[0]
    rows = lax.broadcasted_iota(jnp.int32, z.shape, 0)
    prev_row = jnp.where(i == 0, 0.0, zp_ref[7:8, :])
    next_row = jnp.where(i == nt - 1, 0.0, zn_ref[0:1, :])
    zprev = jnp.where(rows == 0, prev_row, pltpu.roll(z, 1, 0))
    znext = jnp.where(rows == tt - 1, next_row, pltpu.roll(z, tt - 1, 0))
    z = z + (0.5 * (zprev + znext) - z) * mu_ref[...]

    r = z[:, 0:c_rw]
    k = z[:, c_rw:2 * c_rw]
    v = z[:, 2 * c_rw:3 * c_rw]
    o = 3 * c_rw
    wd = z[:, o:o + 2 * lora]
    ad = z[:, o + 2 * lora:o + 4 * lora]
    gd = z[:, o + 4 * lora:]

    xw = w0_ref[...] + _dot(jnp.tanh(wd).astype(BF16), w2_ref[...])
    xa = a0_ref[...] + _dot(ad.astype(BF16), a2_ref[...])
    lw = -math.exp(-0.5) * jax.nn.sigmoid(xw)
    rate = jax.nn.sigmoid(xa)
    g = _dot(jax.nn.sigmoid(gd).astype(BF16), g2_ref[...])

    gsum = gsum_ref[...]
    kkr = k * kk_ref[...]
    kk = kkr * lax.rsqrt(_dot_exact_rhs(kkr * kkr, gsum) + 1e-12)
    ka = ka_ref[...]
    kd0 = k * (1.0 + (rate[:, :c_rw] - 1.0) * ka)
    kd1 = k * (1.0 + (rate[:, c_rw:] - 1.0) * ka)
    bonus = _dot_exact_rhs(r * (0.5 * (kd0 + kd1)) * rk_ref[...], gsum) * v

    r_ref[...] = r
    v_ref[...] = v
    nkk_ref[...] = -kk
    lw_ref[0] = lw[:, :c_rw]
    lw_ref[1] = lw[:, c_rw:]
    kd_ref[0] = kd0
    kd_ref[1] = kd1
    bd_ref[0] = kk * rate[:, :c_rw]
    bd_ref[1] = kk * rate[:, c_rw:]
    g_ref[...] = g
    bonus_ref[...] = bonus


def _rwkv_prep(zrw3, mu, w2bd, w0f, a2bd, a0f, g2_bf, k_k, k_a, r_kf, gsum, c_rw, lora, tt):
    b, t, cols = zrw3.shape
    nt = t // tt
    hb = tt // 8
    nhb = t // 8
    full = lambda bi, i: (0, 0)
    tile = lambda bi, i: (bi, i, 0)
    dtile = lambda bi, i: (0, bi, i, 0)
    one = jax.ShapeDtypeStruct((b, t, c_rw), F32)
    two = jax.ShapeDtypeStruct((2, b, t, c_rw), F32)
    return pl.pallas_call(
        functools.partial(_rwkv_prep_kernel, c_rw=c_rw, lora=lora),
        grid=(b, nt),
        in_specs=[
            pl.BlockSpec((None, tt, cols), tile),
            pl.BlockSpec((None, 8, cols), lambda bi, i: (bi, jnp.maximum(i * hb - 1, 0), 0)),
            pl.BlockSpec((None, 8, cols), lambda bi, i: (bi, jnp.minimum((i + 1) * hb, nhb - 1), 0)),
            pl.BlockSpec((1, cols), full),
            pl.BlockSpec(w2bd.shape, full),
            pl.BlockSpec((1, 2 * c_rw), full),
            pl.BlockSpec(a2bd.shape, full),
            pl.BlockSpec((1, 2 * c_rw), full),
            pl.BlockSpec(g2_bf.shape, full),
            pl.BlockSpec((1, c_rw), full),
            pl.BlockSpec((1, c_rw), full),
            pl.BlockSpec((1, c_rw), full),
            pl.BlockSpec((c_rw, c_rw), full),
        ],
        out_specs=[
            pl.BlockSpec((None, tt, c_rw), tile),
            pl.BlockSpec((None, tt, c_rw), tile),
            pl.BlockSpec((None, tt, c_rw), tile),
            pl.BlockSpec((2, None, tt, c_rw), dtile),
            pl.BlockSpec((2, None, tt, c_rw), dtile),
            pl.BlockSpec((2, None, tt, c_rw), dtile),
            pl.BlockSpec((None, tt, c_rw), tile),
            pl.BlockSpec((None, tt, c_rw), tile),
        ],
        out_shape=[one, one, one, two, two, two, one, one],
        compiler_params=_cparams(("parallel", "parallel")),
        name="rwkv_prep",
    )(zrw3, zrw3, zrw3, mu, w2bd, w0f, a2bd, a0f, g2_bf, k_k, k_a, r_kf, gsum)


def _block_diag(y, mask):
    g = y.shape[1] // RW_HEAD
    return jnp.where(mask, jnp.concatenate([y] * g, axis=0), 0.0).astype(BF16)


def _bmm(x, y, mask):
    outs = []
    for s in range(0, y.shape[1], HEAD_GROUP):
        outs.append(_dot(x[:, s:s + HEAD_GROUP].astype(BF16), _block_diag(y[:, s:s + HEAD_GROUP], mask)))
    return jnp.concatenate(outs, axis=1)


def _bmm_nt(x, y, mask):
    outs = []
    for s in range(0, y.shape[1], HEAD_GROUP):
        bd = _block_diag(y[:, s:s + HEAD_GROUP], mask)
        outs.append(lax.dot_general(x[:, s:s + HEAD_GROUP].astype(BF16), bd, (((1,), (1,)), ((), ())),
                                    preferred_element_type=F32))
    return jnp.concatenate(outs, axis=1)


def _bmm_tn(x, y, mask):
    outs = []
    for s in range(0, y.shape[1], HEAD_GROUP):
        xt = x[:, s:s + HEAD_GROUP].T.astype(BF16)
        full = jnp.where(mask, _dot(xt, y[:, s:s + HEAD_GROUP].astype(BF16)), 0.0)
        o = full[0:RW_HEAD]
        for j in range(1, HEAD_GROUP // RW_HEAD):
            o = o + full[j * RW_HEAD:(j + 1) * RW_HEAD]
        outs.append(o)
    return jnp.concatenate(outs, axis=1)


def _scan_chunk(r, lw, k, v, a, b, state, d, bdmask):
    c, w = r.shape
    t_raw = lax.broadcasted_iota(jnp.int32, (c, w), 0)
    s_raw = lax.broadcasted_iota(jnp.int32, (c, w), 1) % RW_HEAD
    t = t_raw + d * (c - 1 - 2 * t_raw)
    s = s_raw + d * (c - 1 - 2 * s_raw)
    tt_raw = lax.broadcasted_iota(jnp.int32, (c, c), 0)
    ss_raw = lax.broadcasted_iota(jnp.int32, (c, c), 1)
    tt = tt_raw + d * (c - 1 - 2 * tt_raw)
    ss = ss_raw + d * (c - 1 - 2 * ss_raw)
    mcum = jnp.where(ss <= tt, 1.0, 0.0).astype(BF16)
    cum = _dot_exact_lhs(mcum, lw)
    tot = jnp.sum(lw, axis=0, keepdims=True)
    e_pos = jnp.exp(cum)
    e_neg = jnp.exp(-cum)
    e_tail = jnp.exp(tot - cum)
    rt = r * e_pos
    at = a * jnp.exp(cum - lw)
    bt = b * e_neg
    kt = k * e_neg
    bh = b * e_tail
    kh = k * e_tail

    strict = s < t
    incl = s <= t
    ar = jnp.concatenate([at, rt], axis=0)
    pb = _bmm_nt(ar, bt, bdmask)
    pk = _bmm_nt(ar, kt, bdmask)
    a_ab = jnp.where(strict, pb[:c], 0.0)
    a_rb = jnp.where(incl, pb[c:], 0.0)
    a_ak = jnp.where(strict, pk[:c], 0.0)
    a_rk = jnp.where(incl, pk[c:], 0.0)

    eye = jnp.where(s == t, 1.0, 0.0)
    t_hi = t // 2
    s_hi = s // 2
    tinv = eye + jnp.where((t_hi == s_hi) & (t > s), a_ab, 0.0)
    sz = 2
    while sz < c:
        off = (t // (2 * sz) == s // (2 * sz)) & ((t // sz) % 2 == 1) & ((s // sz) % 2 == 0)
        x = _bmm(jnp.where(off, a_ab, 0.0), tinv, bdmask)
        tinv = tinv + _bmm(tinv, x, bdmask)
        sz *= 2

    att = _bmm(tinv, at, bdmask)
    vt = _bmm(tinv, _bmm(a_ak, v, bdmask), bdmask)
    rh = rt + _bmm(a_rb, att, bdmask)
    y_in = _bmm(a_rb, vt, bdmask) + _bmm(a_rk, v, bdmask)
    mc = eye * jnp.exp(tot) + _bmm_tn(bh, att, bdmask)
    nc = _bmm_tn(jnp.concatenate([bh, kh], axis=0), jnp.concatenate([vt, v], axis=0), bdmask)
    y = _bmm(rh, state, bdmask) + y_in
    new_state = _bmm(mc, state, bdmask) + nc
    return y, new_state


def _rwkv_scan_kernel(r_ref, v_ref, a_ref, lw_ref, k_ref, b_ref, y_ref, state_ref, *, n_chunks):
    d = pl.program_id(0)

    @pl.when(pl.program_id(2) == 0)
    def _():
        state_ref[...] = jnp.zeros_like(state_ref)

    rr = lax.broadcasted_iota(jnp.int32, (HEAD_GROUP, HEAD_GROUP), 0) // RW_HEAD
    cc = lax.broadcasted_iota(jnp.int32, (HEAD_GROUP, HEAD_GROUP), 1) // RW_HEAD
    bdmask = rr == cc

    def body(j, carry):
        cj = j + d * (n_chunks - 1 - 2 * j)
        sl = pl.ds(pl.multiple_of(cj * CHUNK, CHUNK), CHUNK)
        y, st = _scan_chunk(r_ref[sl, :], lw_ref[sl, :], k_ref[sl, :], v_ref[sl, :], a_ref[sl, :], b_ref[sl, :],
                            state_ref[...], d, bdmask)
        y_ref[sl, :] = y
        state_ref[...] = st
        return carry

    lax.fori_loop(0, n_chunks, body, 0)


def _rwkv_scan(r, v, nkk, lw, kd, bd, tb):
    b, t, w = r.shape
    nb = t // tb
    shared = lambda d, bi, i: (bi, i + d * (nb - 1 - 2 * i), 0)
    perdir = lambda d, bi, i: (d, bi, i + d * (nb - 1 - 2 * i), 0)
    return pl.pallas_call(
        functools.partial(_rwkv_scan_kernel, n_chunks=tb // CHUNK),
        grid=(2, b, nb),
        in_specs=[
            pl.BlockSpec((None, tb, w), shared),
            pl.BlockSpec((None, tb, w), shared),
            pl.BlockSpec((None, tb, w), shared),
            pl.BlockSpec((None, None, tb, w), perdir),
            pl.BlockSpec((None, None, tb, w), perdir),
            pl.BlockSpec((None, None, tb, w), perdir),
        ],
        out_specs=pl.BlockSpec((None, None, tb, w), perdir),
        out_shape=jax.ShapeDtypeStruct((2, b, t, w), F32),
        scratch_shapes=[pltpu.VMEM((CHUNK, w), F32)],
        compiler_params=_cparams(("parallel", "parallel", "arbitrary")),
        name="rwkv_scan",
    )(r, v, nkk, lw, kd, bd)


def _attn_kernel(slope_ref, q_ref, k_ref, v_ref, lq1_ref, lk1_ref, lq2_ref, lk2_ref, sg_ref, o_ref,
                 m_ref, l_ref, acc_ref, *, qk_dim, lambda_init):
    h = pl.program_id(1)
    i = pl.program_id(2)
    j = pl.program_id(3)
    nk = pl.num_programs(3)
    tq = q_ref.shape[0]
    tk = k_ref.shape[0]

    @pl.when(j == 0)
    def _():
        m_ref[...] = jnp.full(m_ref.shape, -jnp.inf, F32)
        l_ref[...] = jnp.zeros_like(l_ref)
        acc_ref[...] = jnp.zeros_like(acc_ref)

    q = q_ref[...]
    k = k_ref[...]
    v = v_ref[...]
    lane = lax.broadcasted_iota(jnp.int32, q.shape, 1)
    zero = jnp.zeros_like(q)
    qi = lax.broadcasted_iota(jnp.int32, (tq, tk), 0) + i * tq
    kj = lax.broadcasted_iota(jnp.int32, (tq, tk), 1) + j * tk
    bias = slope_ref[h] * jnp.abs(qi - kj).astype(F32)
    for c in range(2):
        qc = jnp.where((lane < qk_dim) == (c == 0), q, zero)
        s = lax.dot_general(qc, k, (((1,), (1,)), ((), ())), preferred_element_type=F32) - bias
        m_old = m_ref[c]
        m_new = jnp.maximum(m_old, jnp.max(s, axis=-1, keepdims=True))
        alpha = jnp.exp(m_old - m_new)
        p = jnp.exp(s - m_new)
        l_ref[c] = alpha * l_ref[c] + jnp.sum(p, axis=-1, keepdims=True)
        acc_ref[c] = alpha * acc_ref[c] + _dot(p.astype(BF16), v)
        m_ref[c] = m_new

    @pl.when(j == nk - 1)
    def _():
        lam = (jnp.exp(jnp.sum(lq1_ref[...] * lk1_ref[...], axis=-1, keepdims=True))
               - jnp.exp(jnp.sum(lq2_ref[...] * lk2_ref[...], axis=-1, keepdims=True)) + lambda_init)
        o = acc_ref[0] / l_ref[0] - lam * (acc_ref[1] / l_ref[1])
        ms = jnp.mean(o * o, axis=-1, keepdims=True)
        o_ref[...] = o * lax.rsqrt(ms + NORM_EPS) * (sg_ref[...] * (1.0 - lambda_init))


def _attention(slopes, q3, k3, v3, lq1, lk1, lq2, lk2, subln_g, heads, v_dim, lambda_init, tq, tk):
    b, t, _ = q3.shape
    small = lambda s, bi, h, i, j: (0, 0)
    grid_spec = pltpu.PrefetchScalarGridSpec(
        num_scalar_prefetch=1,
        grid=(b, heads, t // tq, t // tk),
        in_specs=[
            pl.BlockSpec((None, tq, v_dim), lambda bi, h, i, j, s: (bi, i, h)),
            pl.BlockSpec((None, tk, v_dim), lambda bi, h, i, j, s: (bi, j, h)),
            pl.BlockSpec((None, tk, v_dim), lambda bi, h, i, j, s: (bi, j, h)),
            pl.BlockSpec(lq1.shape, lambda bi, h, i, j, s: (0, 0)),
            pl.BlockSpec(lk1.shape, lambda bi, h, i, j, s: (0, 0)),
            pl.BlockSpec(lq2.shape, lambda bi, h, i, j, s: (0, 0)),
            pl.BlockSpec(lk2.shape, lambda bi, h, i, j, s: (0, 0)),
            pl.BlockSpec(subln_g.shape, lambda bi, h, i, j, s: (0, 0)),
        ],
        out_specs=pl.BlockSpec((None, tq, v_dim), lambda bi, h, i, j, s: (bi, i, h)),
        scratch_shapes=[
            pltpu.VMEM((2, tq, 1), F32),
            pltpu.VMEM((2, tq, 1), F32),
            pltpu.VMEM((2, tq, v_dim), F32),
        ],
    )
    del small
    return pl.pallas_call(
        functools.partial(_attn_kernel, qk_dim=v_dim // 2, lambda_init=lambda_init),
        grid_spec=grid_spec,
        out_shape=jax.ShapeDtypeStruct((b, t, heads * v_dim), F32),
        compiler_params=_cparams(("parallel", "parallel", "parallel", "arbitrary")),
        name="attn",
    )(slopes, q3, k3, v3, lq1, lk1, lq2, lk2, subln_g)


def _post_kernel(yf_ref, yb_ref, bonus_ref, g_ref, oda_ref, x_ref, lng_ref, lnb_ref, gmean_ref, wo_rw_ref, wo_da_ref,
                 gffn_ref, wr_ref, br_ref, h_ref, m_ref, idx_ref, gate_ref, *, gn_eps):
    gmean = gmean_ref[...]
    y = yf_ref[...] + yb_ref[...]
    mean = _dot_exact_rhs(y, gmean)
    yc = y - mean
    var = _dot_exact_rhs(yc * yc, gmean)
    yn = yc * lax.rsqrt(var + gn_eps) * lng_ref[...] + lnb_ref[...]
    o_rw = (yn + bonus_ref[...]) * g_ref[...]
    h = x_ref[...] + _dot(o_rw.astype(BF16), wo_rw_ref[...]) + _dot(oda_ref[...].astype(BF16), wo_da_ref[...])
    h_ref[...] = h
    ms = jnp.mean(h * h, axis=-1, keepdims=True)
    m = h * lax.rsqrt(ms + NORM_EPS) * gffn_ref[...]
    m_ref[...] = m

    logits = _dot_f32(m, wr_ref[...]) + br_ref[...]
    ne = logits.shape[1]
    lane = lax.broadcasted_iota(jnp.int32, logits.shape, 1)
    kcol = lax.broadcasted_iota(jnp.int32, (logits.shape[0], TOP_K), 1)
    idx_all = jnp.zeros((logits.shape[0], TOP_K), jnp.int32)
    val_all = jnp.zeros((logits.shape[0], TOP_K), F32)
    work = logits
    for kk in range(TOP_K):
        mx = jnp.max(work, axis=-1, keepdims=True)
        sel = jnp.min(jnp.where(work == mx, lane, ne), axis=-1, keepdims=True)
        idx_all = jnp.where(kcol == kk, sel, idx_all)
        val_all = jnp.where(kcol == kk, mx, val_all)
        work = jnp.where(lane == sel, -jnp.inf, work)
    e = jnp.exp(val_all - jnp.max(val_all, axis=-1, keepdims=True))
    idx_ref[...] = idx_all
    gate_ref[...] = e / jnp.sum(e, axis=-1, keepdims=True)


def _post(y2, bonus, g, oda, x2, lng, lnb, gmean, wo_rw, wo_da, gffn, wr, br, gn_eps, tm):
    _, n, c = y2.shape
    d = x2.shape[1]
    ne = wr.shape[1]
    full = lambda i: (0, 0)
    row = lambda i: (i, 0)
    return pl.pallas_call(
        functools.partial(_post_kernel, gn_eps=gn_eps),
        grid=(n // tm,),
        in_specs=[
            pl.BlockSpec((None, tm, c), lambda i: (0, i, 0)),
            pl.BlockSpec((None, tm, c), lambda i: (1, i, 0)),
            pl.BlockSpec((tm, c), row),
            pl.BlockSpec((tm, c), row),
            pl.BlockSpec((tm, c), row),
            pl.BlockSpec((tm, d), row),
            pl.BlockSpec((1, c), full),
            pl.BlockSpec((1, c), full),
            pl.BlockSpec((c, c), full),
            pl.BlockSpec((c, d), full),
            pl.BlockSpec((c, d), full),
            pl.BlockSpec((1, d), full),
            pl.BlockSpec((d, ne), full),
            pl.BlockSpec((1, ne), full),
        ],
        out_specs=[
            pl.BlockSpec((tm, d), row),
            pl.BlockSpec((tm, d), row),
            pl.BlockSpec((tm, TOP_K), row),
            pl.BlockSpec((tm, TOP_K), row),
        ],
        out_shape=[
            jax.ShapeDtypeStruct((n, d), F32),
            jax.ShapeDtypeStruct((n, d), F32),
            jax.ShapeDtypeStruct((n, TOP_K), jnp.int32),
            jax.ShapeDtypeStruct((n, TOP_K), F32),
        ],
        compiler_params=_cparams(("parallel",)),
        name="post",
    )(y2, y2, bonus, g, oda, x2, lng, lnb, gmean, wo_rw, wo_da, gffn, wr, br)


def _rank_kernel(idx_ref, ltri_ref, rank_ref, count_ref, carry_ref):
    @pl.when(pl.program_id(0) == 0)
    def _():
        carry_ref[...] = jnp.zeros_like(carry_ref)

    idx = idx_ref[...]
    tr = idx.shape[0]
    lane = lax.broadcasted_iota(jnp.int32, (tr, N_EXPERTS), 1)
    hots = [jnp.where(lane == idx[:, kk:kk + 1], 1.0, 0.0) for kk in range(TOP_K)]
    mask = hots[0]
    for kk in range(1, TOP_K):
        mask = mask + hots[kk]
    before = _dot(ltri_ref[...], mask.astype(BF16)) + carry_ref[...]
    kcol = lax.broadcasted_iota(jnp.int32, (tr, TOP_K), 1)
    rank = jnp.zeros((tr, TOP_K), F32)
    for kk in range(TOP_K):
        rk = jnp.sum(hots[kk] * before, axis=-1, keepdims=True)
        rank = jnp.where(kcol == kk, rk, rank)
    rank_ref[...] = rank.astype(jnp.int32)
    carry_ref[...] = carry_ref[...] + jnp.sum(mask, axis=0, keepdims=True)
    count_ref[...] = carry_ref[...].astype(jnp.int32)


def _moe_rank(idx, tr):
    n = idx.shape[0]
    r = lax.broadcasted_iota(jnp.int32, (tr, tr), 0)
    c = lax.broadcasted_iota(jnp.int32, (tr, tr), 1)
    ltri = jnp.where(c < r, 1.0, 0.0).astype(BF16)
    return pl.pallas_call(
        _rank_kernel,
        grid=(n // tr,),
        in_specs=[pl.BlockSpec((tr, TOP_K), lambda i: (i, 0)), pl.BlockSpec((tr, tr), lambda i: (0, 0))],
        out_specs=[pl.BlockSpec((tr, TOP_K), lambda i: (i, 0)), pl.BlockSpec((1, N_EXPERTS), lambda i: (0, 0))],
        out_shape=[jax.ShapeDtypeStruct((n, TOP_K), jnp.int32), jax.ShapeDtypeStruct((1, N_EXPERTS), jnp.int32)],
        scratch_shapes=[pltpu.VMEM((1, N_EXPERTS), F32)],
        compiler_params=_cparams(("arbitrary",)),
        name="moe_rank",
    )(idx, ltri)


def _dispatch_kernel(offs_ref, idx_ref, rank_ref, m_ref, xs_ref, sem):
    td = m_ref.shape[0]

    def row_copy(t, kk):
        e = idx_ref[t * TOP_K + kk]
        dest = offs_ref[e] + rank_ref[t * TOP_K + kk]
        return pltpu.make_async_copy(m_ref.at[pl.ds(t, 1), :], xs_ref.at[pl.ds(dest, 1), :], sem)

    def issue(t, carry):
        for kk in range(TOP_K):
            row_copy(t, kk).start()
        return carry

    lax.fori_loop(0, td, issue, 0)

    def drain(t, carry):
        for kk in range(TOP_K):
            row_copy(t, kk).wait()
        return carry

    lax.fori_loop(0, td, drain, 0)


def _dispatch(offs, idx_flat, rank_flat, m, p_rows, td):
    n, d = m.shape
    grid_spec = pltpu.PrefetchScalarGridSpec(
        num_scalar_prefetch=1,
        grid=(n // td,),
        in_specs=[
            pl.BlockSpec((td * TOP_K,), lambda i, o: (i,), memory_space=pltpu.SMEM),
            pl.BlockSpec((td * TOP_K,), lambda i, o: (i,), memory_space=pltpu.SMEM),
            pl.BlockSpec((td, d), lambda i, o: (i, 0)),
        ],
        out_specs=pl.BlockSpec(memory_space=pl.ANY),
        scratch_shapes=[pltpu.SemaphoreType.DMA(())],
    )
    return pl.pallas_call(
        _dispatch_kernel,
        grid_spec=grid_spec,
        out_shape=jax.ShapeDtypeStruct((p_rows, d), F32),
        compiler_params=_cparams(("arbitrary",), has_side_effects=True),
        name="moe_dispatch",
    )(offs, idx_flat, rank_flat, m)


def _ffn_kernel(be_ref, nu_ref, x_ref, w1_ref, b1_ref, w2_ref, b2_ref, y_ref, *, d_ff):
    @pl.when(pl.program_id(0) < nu_ref[0])
    def _():
        x = x_ref[...].astype(BF16)
        hcat = _dot(x, w1_ref[...]) + b1_ref[...]
        glu = jnp.minimum(hcat[:, :d_ff], SWIGLU_LIMIT)
        lin = jnp.clip(hcat[:, d_ff:], -SWIGLU_LIMIT, SWIGLU_LIMIT)
        act = glu * jax.nn.sigmoid(SWIGLU_ALPHA * glu) * (lin + 1.0)
        y_ref[...] = _dot(act.astype(BF16), w2_ref[...]) + b2_ref[...]


def _moe_ffn(blk_e, n_used, xs, w1_bf, b1, w2_bf, b2, bm):
    p_rows, d = xs.shape
    ne, _, f2 = w1_bf.shape
    d_ff = f2 // 2
    grid_spec = pltpu.PrefetchScalarGridSpec(
        num_scalar_prefetch=2,
        grid=(p_rows // bm,),
        in_specs=[
            pl.BlockSpec((bm, d), lambda i, be, nu: (jnp.minimum(i, nu[0] - 1), 0)),
            pl.BlockSpec((None, d, f2), lambda i, be, nu: (be[i], 0, 0)),
            pl.BlockSpec((None, 1, f2), lambda i, be, nu: (be[i], 0, 0)),
            pl.BlockSpec((None, d_ff, d), lambda i, be, nu: (be[i], 0, 0)),
            pl.BlockSpec((None, 1, d), lambda i, be, nu: (be[i], 0, 0)),
        ],
        out_specs=pl.BlockSpec((bm, d), lambda i, be, nu: (i, 0)),
    )
    return pl.pallas_call(
        functools.partial(_ffn_kernel, d_ff=d_ff),
        grid_spec=grid_spec,
        out_shape=jax.ShapeDtypeStruct((p_rows, d), F32),
        compiler_params=_cparams(("arbitrary",)),
        name="moe_ffn",
    )(blk_e, n_used, xs, w1_bf, b1.reshape(ne, 1, f2), w2_bf, b2.reshape(ne, 1, d))


def _combine_kernel(offs_ref, idx_ref, rank_ref, h_ref, gate_ref, y_ref, o_ref, buf_ref, sem):
    tc = h_ref.shape[0]

    def row_copy(t, kk):
        e = idx_ref[t * TOP_K + kk]
        src = offs_ref[e] + rank_ref[t * TOP_K + kk]
        return pltpu.make_async_copy(y_ref.at[pl.ds(src, 1), :], buf_ref.at[kk, pl.ds(t, 1), :], sem)

    def issue(t, carry):
        for kk in range(TOP_K):
            row_copy(t, kk).start()
        return carry

    lax.fori_loop(0, tc, issue, 0)

    def drain(t, carry):
        for kk in range(TOP_K):
            row_copy(t, kk).wait()
        return carry

    lax.fori_loop(0, tc, drain, 0)

    gate = gate_ref[...]
    acc = h_ref[...]
    for kk in range(TOP_K):
        acc = acc + gate[:, kk:kk + 1] * buf_ref[kk]
    o_ref[...] = acc


def _combine(offs, idx_flat, rank_flat, h, gates, y, tc):
    n, d = h.shape
    grid_spec = pltpu.PrefetchScalarGridSpec(
        num_scalar_prefetch=1,
        grid=(n // tc,),
        in_specs=[
            pl.BlockSpec((tc * TOP_K,), lambda i, o: (i,), memory_space=pltpu.SMEM),
            pl.BlockSpec((tc * TOP_K,), lambda i, o: (i,), memory_space=pltpu.SMEM),
            pl.BlockSpec((tc, d), lambda i, o: (i, 0)),
            pl.BlockSpec((tc, TOP_K), lambda i, o: (i, 0)),
            pl.BlockSpec(memory_space=pl.ANY),
        ],
        out_specs=pl.BlockSpec((tc, d), lambda i, o: (i, 0)),
        scratch_shapes=[pltpu.VMEM((TOP_K, tc, d), F32), pltpu.SemaphoreType.DMA(())],
    )
    return pl.pallas_call(
        _combine_kernel,
        grid_spec=grid_spec,
        out_shape=jax.ShapeDtypeStruct((n, d), F32),
        compiler_params=_cparams(("arbitrary",)),
        name="moe_combine",
    )(offs, idx_flat, rank_flat, h, gates, y)


def _ple_kernel(h_ref, p_ref, wp_ref, gp_ref, wg_ref, o_ref):
    h = h_ref[...]
    e = _dot(p_ref[...].astype(BF16), wp_ref[...])
    ms = jnp.mean(e * e, axis=-1, keepdims=True)
    e = e * lax.rsqrt(ms + NORM_EPS) * gp_ref[...]
    gate = jax.nn.sigmoid(_dot(h.astype(BF16), wg_ref[...]))
    o_ref[...] = h + e * gate


def _ple(h, p2, wp_bf, gp, wg_bf, tm):
    n, d = h.shape
    pd = p2.shape[1]
    full = lambda i: (0, 0)
    row = lambda i: (i, 0)
    return pl.pallas_call(
        _ple_kernel,
        grid=(n // tm,),
        in_specs=[
            pl.BlockSpec((tm, d), row),
            pl.BlockSpec((tm, pd), row),
            pl.BlockSpec((pd, d), full),
            pl.BlockSpec((1, d), full),
            pl.BlockSpec((d, d), full),
        ],
        out_specs=pl.BlockSpec((tm, d), row),
        out_shape=jax.ShapeDtypeStruct((n, d), F32),
        compiler_params=_cparams(("parallel",)),
        name="ple",
    )(h, p2, wp_bf, gp, wg_bf)


def _tile(n, want):
    t = min(n, want)
    assert n % t == 0, (n, t)
    return t


def _layer(h3, p3, lambda_init, norm_mix_g, w_in, rw_mu, rw_w0, rw_w2, rw_a0, rw_a2, rw_g2, rw_k_k, rw_k_a, rw_r_k,
           rw_ln_g, rw_ln_b, da_q_norm_g, da_k_norm_g, da_lq1, da_lk1, da_lq2, da_lk2, da_subln_g, w_out,
           norm_ffn_g, w_router, b_router, w1, b1, w2, b2, w_ple, ple_norm_g, w_ple_gate):
    b, t, d = h3.shape
    n = b * t
    c_rw = rw_k_k.shape[0]
    lora = rw_w2.shape[1]
    rw_cols = rw_mu.shape[0]
    rw_heads = rw_r_k.shape[0]
    c_da = w_in.shape[1] - rw_cols
    assert c_da % 3 == 0
    c_da //= 3
    qk_dim = da_q_norm_g.shape[0]
    v_dim = da_subln_g.shape[0]
    da_heads = c_da // v_dim
    assert rw_heads * RW_HEAD == c_rw and c_rw % HEAD_GROUP == 0 and v_dim == V7X_LANES and 2 * qk_dim == v_dim
    assert 2 * lora == V7X_LANES and w_router.shape[1] == N_EXPERTS

    x2 = h3.reshape(n, d)
    row = lambda a: a.reshape(1, -1).astype(F32)

    gq_t = row(jnp.tile(da_q_norm_g, c_da // qk_dim))
    gk_t = row(jnp.tile(da_k_norm_g, c_da // qk_dim))
    gm_qk = _group_matrix(c_da, qk_dim, 1.0 / qk_dim)
    zrw, q, k, v = _inproj(x2, row(norm_mix_g), w_in.astype(BF16), gq_t, gk_t, gm_qk, rw_cols, c_da,
                           1.0 / math.sqrt(qk_dim), _tile(n, 256))

    zeros = jnp.zeros((lora, c_rw), F32)
    w2bd = jnp.concatenate([jnp.concatenate([rw_w2[0], zeros], axis=1),
                            jnp.concatenate([zeros, rw_w2[1]], axis=1)], axis=0).astype(BF16)
    a2bd = jnp.concatenate([jnp.concatenate([rw_a2[0], zeros], axis=1),
                            jnp.concatenate([zeros, rw_a2[1]], axis=1)], axis=0).astype(BF16)
    gsum = _group_matrix(c_rw, RW_HEAD, 1.0)
    r, vv, nkk, lw, kd, bd, g, bonus = _rwkv_prep(
        zrw.reshape(b, t, rw_cols), row(rw_mu), w2bd, row(rw_w0), a2bd, row(rw_a0), rw_g2.astype(BF16),
        row(rw_k_k), row(rw_k_a), row(rw_r_k), gsum, c_rw, lora, _tile(t, 256))
    y = _rwkv_scan(r, vv, nkk, lw, kd, bd, _tile(t, 256))

    slopes = 2.0 ** (-8.0 * jnp.arange(1, da_heads + 1, dtype=F32) / da_heads)
    ta = _tile(t, 512)
    o_da = _attention(slopes, q.reshape(b, t, c_da), k.reshape(b, t, c_da), v.reshape(b, t, c_da),
                      row(da_lq1), row(da_lk1), row(da_lq2), row(da_lk2), row(da_subln_g),
                      da_heads, v_dim, lambda_init, ta, ta)

    gmean = _group_matrix(c_rw, RW_HEAD, 1.0 / RW_HEAD)
    w_out_bf = w_out.astype(BF16)
    h1, m, idx, gates = _post(
        y.reshape(2, n, c_rw), bonus.reshape(n, c_rw), g.reshape(n, c_rw), o_da.reshape(n, c_da), x2,
        row(rw_ln_g), row(rw_ln_b), gmean, w_out_bf[:c_rw], w_out_bf[c_rw:], row(norm_ffn_g),
        w_router.astype(F32), row(b_router), RW_HEAD * 1e-5, _tile(n, 256))

    bm = 256
    rank, counts = _moe_rank(idx, _tile(n, 512))
    counts = counts.reshape(N_EXPERTS)
    padded = (counts + bm - 1) // bm * bm
    pend = jnp.cumsum(padded)
    offs = (pend - padded).astype(jnp.int32)
    n_blk = (n * TOP_K) // bm + N_EXPERTS
    blk_start = jnp.arange(n_blk, dtype=jnp.int32) * bm
    blk_e = jnp.minimum(jnp.sum(blk_start[:, None] >= pend[None, :], axis=1), N_EXPERTS - 1).astype(jnp.int32)
    n_used = (pend[-1] // bm).astype(jnp.int32).reshape(1)
    idx_flat = idx.reshape(n * TOP_K)
    rank_flat = rank.reshape(n * TOP_K)
    xs = _dispatch(offs, idx_flat, rank_flat, m, n_blk * bm, _tile(n, 256))
    ys = _moe_ffn(blk_e, n_used, xs, w1.astype(BF16), b1, w2.astype(BF16), b2, bm)
    h2 = _combine(offs, idx_flat, rank_flat, h1, gates, ys, _tile(n, 256))

    out = _ple(h2, p3.reshape(n, -1), w_ple.astype(BF16), row(ple_norm_g), w_ple_gate.astype(BF16), _tile(n, 512))
    return out.reshape(b, t, d)


def kernel(x, p, norm_mix_g, w_in, rw_mu, rw_w0, rw_w2, rw_a0, rw_a2, rw_g2, rw_k_k, rw_k_a, rw_r_k, rw_ln_g,
           rw_ln_b, da_q_norm_g, da_k_norm_g, da_lq1, da_lk1, da_lq2, da_lk2, da_subln_g, w_out, norm_ffn_g,
           w_router, b_router, w1, b1, w2, b2, w_ple, ple_norm_g, w_ple_gate):
    h = x.astype(F32)
    params = (norm_mix_g, w_in, rw_mu, rw_w0, rw_w2, rw_a0, rw_a2, rw_g2, rw_k_k, rw_k_a, rw_r_k, rw_ln_g, rw_ln_b,
              da_q_norm_g, da_k_norm_g, da_lq1, da_lk1, da_lq2, da_lk2, da_subln_g, w_out, norm_ffn_g,
              w_router, b_router, w1, b1, w2, b2, w_ple, ple_norm_g, w_ple_gate)
    for i in range(p.shape[0]):
        lambda_init = 0.8 - 0.6 * math.exp(-0.3 * i)
        h = _layer(h, p[i], lambda_init, *(a[i] for a in params))
    return h.astype(x.dtype)
```

```python
import functools
import math

import jax
import jax.numpy as jnp
from jax import lax
from jax.experimental import pallas as pl
from jax.experimental.pallas import tpu as pltpu

F32 = jnp.float32
BF16 = jnp.bfloat16

V7X_LANES = 128
V7X_VMEM_BYTES = 64 * 1024 * 1024
VMEM_LIMIT = 52 * 1024 * 1024

NORM_EPS = 1e-6
RW_HEAD = 64
CHUNK = 64
HEAD_GROUP = 256
N_EXPERTS = 32
TOP_K = 4
SWIGLU_ALPHA = 1.702
SWIGLU_LIMIT = 7.0


def _cparams(sem, **kw):
    return pltpu.CompilerParams(dimension_semantics=sem, vmem_limit_bytes=VMEM_LIMIT, **kw)


def _dot(a, b):
    return jnp.dot(a, b, preferred_element_type=F32)


def _split3(x):
    hi = x.astype(BF16)
    r1 = x - hi.astype(F32)
    mid = r1.astype(BF16)
    lo = (r1 - mid.astype(F32)).astype(BF16)
    return hi, mid, lo


def _dot_exact_rhs(x, g_bf16):
    hi, mid, lo = _split3(x)
    return _dot(hi, g_bf16) + _dot(mid, g_bf16) + _dot(lo, g_bf16)


def _dot_exact_lhs(g_bf16, x):
    hi, mid, lo = _split3(x)
    return _dot(g_bf16, hi) + _dot(g_bf16, mid) + _dot(g_bf16, lo)


def _dot_f32(a, b):
    ah, am, al = _split3(a)
    bh, bm, bl = _split3(b)
    return (_dot(ah, bh) + (_dot(ah, bm) + _dot(am, bh))
            + (_dot(am, bm) + _dot(ah, bl) + _dot(al, bh)))


def _group_matrix(width, group, value):
    r = lax.broadcasted_iota(jnp.int32, (width, width), 0) // group
    c = lax.broadcasted_iota(jnp.int32, (width, width), 1) // group
    return jnp.where(r == c, value, 0.0).astype(BF16)


def _inproj_kernel(x_ref, g_ref, w_ref, gq_ref, gk_ref, gm_ref, zrw_ref, q_ref, k_ref, v_ref, *, rw_cols, c_da, scale):
    x = x_ref[...]
    ms = jnp.mean(x * x, axis=-1, keepdims=True)
    u = (x * lax.rsqrt(ms + NORM_EPS) * g_ref[...]).astype(BF16)
    z = _dot(u, w_ref[...])
    zrw_ref[...] = z[:, :rw_cols]
    zq = z[:, rw_cols:rw_cols + c_da]
    zk = z[:, rw_cols + c_da:rw_cols + 2 * c_da]
    zv = z[:, rw_cols + 2 * c_da:]
    gm = gm_ref[...]
    msq = _dot_exact_rhs(zq * zq, gm)
    msk = _dot_exact_rhs(zk * zk, gm)
    q_ref[...] = (zq * lax.rsqrt(msq + NORM_EPS) * (gq_ref[...] * scale)).astype(BF16)
    k_ref[...] = (zk * lax.rsqrt(msk + NORM_EPS) * gk_ref[...]).astype(BF16)
    v_ref[...] = zv.astype(BF16)


def _inproj(x2, g, w_bf, gq_t, gk_t, gm, rw_cols, c_da, scale, tm):
    n, d = x2.shape
    cols = w_bf.shape[1]
    full = lambda i: (0, 0)
    row = lambda i: (i, 0)
    return pl.pallas_call(
        functools.partial(_inproj_kernel, rw_cols=rw_cols, c_da=c_da, scale=scale),
        grid=(n // tm,),
        in_specs=[
            pl.BlockSpec((tm, d), row),
            pl.BlockSpec((1, d), full),
            pl.BlockSpec((d, cols), full),
            pl.BlockSpec((1, c_da), full),
            pl.BlockSpec((1, c_da), full),
            pl.BlockSpec((c_da, c_da), full),
        ],
        out_specs=[
            pl.BlockSpec((tm, rw_cols), row),
            pl.BlockSpec((tm, c_da), row),
            pl.BlockSpec((tm, c_da), row),
            pl.BlockSpec((tm, c_da), row),
        ],
        out_shape=[
            jax.ShapeDtypeStruct((n, rw_cols), F32),
            jax.ShapeDtypeStruct((n, c_da), BF16),
            jax.ShapeDtypeStruct((n, c_da), BF16),
            jax.ShapeDtypeStruct((n, c_da), BF16),
        ],
        compiler_params=_cparams(("parallel",)),
        name="inproj",
    )(x2, g, w_bf, gq_t, gk_t, gm)


def _rwkv_prep_kernel(z_ref, zp_ref, zn_ref, mu_ref, w2_ref, w0_ref, a2_ref, a0_ref, g2_ref, kk_ref, ka_ref,
                      rk_ref, gsum_ref,
                      r_ref, v_ref, nkk_ref, lw_ref, kd_ref, bd_ref, g_ref, bonus_ref, *, c_rw, lora):
    i = pl.program_id(1)
    nt = pl.num_programs(1)
    z = z_ref[...]
    tt = z.shape[0]
    rows = lax.broadcasted_iota(jnp.int32, z.shape, 0)
    prev_row = jnp.where(i == 0, 0.0, zp_ref[7:8, :])
    next_row = jnp.where(i == nt - 1, 0.0, zn_ref[0:1, :])
    zprev = jnp.where(rows == 0, prev_row, pltpu.roll(z, 1, 0))
    znext = jnp.where(rows == tt - 1, next_row, pltpu.roll(z, tt - 1, 0))
    z = z + (0.5 * (zprev + znext) - z) * mu_ref[...]

    r = z[:, 0:c_rw]
    k = z[:, c_rw:2 * c_rw]
    v = z[:, 2 * c_rw:3 * c_rw]
    o = 3 * c_rw
    wd = z[:, o:o + 2 * lora]
    ad = z[:, o + 2 * lora:o + 4 * lora]
    gd = z[:, o + 4 * lora:]

    xw = w0_ref[...] + _dot(jnp.tanh(wd).astype(BF16), w2_ref[...])
    xa = a0_ref[...] + _dot(ad.astype(BF16), a2_ref[...])
    lw = -math.exp(-0.5) * jax.nn.sigmoid(xw)
    rate = jax.nn.sigmoid(xa)
    g = _dot(jax.nn.sigmoid(gd).astype(BF16), g2_ref[...])

    gsum = gsum_ref[...]
    kkr = k * kk_ref[...]
    kk = kkr * lax.rsqrt(_dot_exact_rhs(kkr * kkr, gsum) + 1e-12)
    ka = ka_ref[...]
    kd0 = k * (1.0 + (rate[:, :c_rw] - 1.0) * ka)
    kd1 = k * (1.0 + (rate[:, c_rw:] - 1.0) * ka)
    bonus = _dot_exact_rhs(r * (0.5 * (kd0 + kd1)) * rk_ref[...], gsum) * v

    r_ref[...] = r
    v_ref[...] = v
    nkk_ref[...] = -kk
    lw_ref[0] = lw[:, :c_rw]
    lw_ref[1] = lw[:, c_rw:]
    kd_ref[0] = kd0
    kd_ref[1] = kd1
    bd_ref[0] = kk * rate[:, :c_rw]
    bd_ref[1] = kk * rate[:, c_rw:]
    g_ref[...] = g
    bonus_ref[...] = bonus


def _rwkv_prep(zrw3, mu, w2bd, w0f, a2bd, a0f, g2_bf, k_k, k_a, r_kf, gsum, c_rw, lora, tt):
    b, t, cols = zrw3.shape
    nt = t // tt
    hb = tt // 8
    nhb = t // 8
    full = lambda bi, i: (0, 0)
    tile = lambda bi, i: (bi, i, 0)
    dtile = lambda bi, i: (0, bi, i, 0)
    one = jax.ShapeDtypeStruct((b, t, c_rw), F32)
    two = jax.ShapeDtypeStruct((2, b, t, c_rw), F32)
    return pl.pallas_call(
        functools.partial(_rwkv_prep_kernel, c_rw=c_rw, lora=lora),
        grid=(b, nt),
        in_specs=[
            pl.BlockSpec((None, tt, cols), tile),
            pl.BlockSpec((None, 8, cols), lambda bi, i: (bi, jnp.maximum(i * hb - 1, 0), 0)),
            pl.BlockSpec((None, 8, cols), lambda bi, i: (bi, jnp.minimum((i + 1) * hb, nhb - 1), 0)),
            pl.BlockSpec((1, cols), full),
            pl.BlockSpec(w2bd.shape, full),
            pl.BlockSpec((1, 2 * c_rw), full),
            pl.BlockSpec(a2bd.shape, full),
            pl.BlockSpec((1, 2 * c_rw), full),
            pl.BlockSpec(g2_bf.shape, full),
            pl.BlockSpec((1, c_rw), full),
            pl.BlockSpec((1, c_rw), full),
            pl.BlockSpec((1, c_rw), full),
            pl.BlockSpec((c_rw, c_rw), full),
        ],
        out_specs=[
            pl.BlockSpec((None, tt, c_rw), tile),
            pl.BlockSpec((None, tt, c_rw), tile),
            pl.BlockSpec((None, tt, c_rw), tile),
            pl.BlockSpec((2, None, tt, c_rw), dtile),
            pl.BlockSpec((2, None, tt, c_rw), dtile),
            pl.BlockSpec((2, None, tt, c_rw), dtile),
            pl.BlockSpec((None, tt, c_rw), tile),
            pl.BlockSpec((None, tt, c_rw), tile),
        ],
        out_shape=[one, one, one, two, two, two, one, one],
        compiler_params=_cparams(("parallel", "parallel")),
        name="rwkv_prep",
    )(zrw3, zrw3, zrw3, mu, w2bd, w0f, a2bd, a0f, g2_bf, k_k, k_a, r_kf, gsum)


M_STRICT, M_INCL, M_EYE, M_PAIR, M_LEVEL = 0, 1, 2, 3, 4
N_LEVELS = 5
SCAN_SEQS = 2


def _scan_tables(w):
    c = CHUNK
    masks = []
    mcums = []
    for d in range(2):
        t = lax.broadcasted_iota(jnp.int32, (c, w), 0)
        s = lax.broadcasted_iota(jnp.int32, (c, w), 1) % RW_HEAD
        tt = lax.broadcasted_iota(jnp.int32, (c, c), 0)
        ss = lax.broadcasted_iota(jnp.int32, (c, c), 1)
        if d == 1:
            t, s, tt, ss = c - 1 - t, c - 1 - s, c - 1 - tt, c - 1 - ss
        rows = [s < t, s <= t, s == t, (t // 2 == s // 2) & (t > s)]
        sz = 2
        while sz < c:
            rows.append((t // (2 * sz) == s // (2 * sz)) & ((t // sz) % 2 == 1) & ((s // sz) % 2 == 0))
            sz *= 2
        assert len(rows) == M_LEVEL + N_LEVELS
        masks.append(jnp.stack([m.astype(F32) for m in rows]))
        mcums.append((ss <= tt).astype(BF16))
    return jnp.stack(masks), jnp.stack(mcums)


def _block_diag(y, bd16):
    g = y.shape[1] // RW_HEAD
    return jnp.concatenate([y.astype(BF16)] * g, axis=0) * bd16


def _bmm(x, y, bd16):
    outs = []
    for s in range(0, y.shape[1], HEAD_GROUP):
        outs.append(_dot(x[:, s:s + HEAD_GROUP].astype(BF16), _block_diag(y[:, s:s + HEAD_GROUP], bd16)))
    return jnp.concatenate(outs, axis=1)


def _bmm_nt(x, y, bd16):
    outs = []
    for s in range(0, y.shape[1], HEAD_GROUP):
        bd = _block_diag(y[:, s:s + HEAD_GROUP], bd16)
        outs.append(lax.dot_general(x[:, s:s + HEAD_GROUP].astype(BF16), bd, (((1,), (1,)), ((), ())),
                                    preferred_element_type=F32))
    return jnp.concatenate(outs, axis=1)


def _bmm_tn(x, y, bd32):
    outs = []
    for s in range(0, y.shape[1], HEAD_GROUP):
        xt = x[:, s:s + HEAD_GROUP].T.astype(BF16)
        full = _dot(xt, y[:, s:s + HEAD_GROUP].astype(BF16)) * bd32
        o = full[0:RW_HEAD]
        for j in range(1, HEAD_GROUP // RW_HEAD):
            o = o + full[j * RW_HEAD:(j + 1) * RW_HEAD]
        outs.append(o)
    return jnp.concatenate(outs, axis=1)


def _scan_chunks(r, lw, k, v, a, b, state, masks, mcum, bd16, bd32):
    c = r[0].shape[0]

    def each(f, *cols):
        return [f(*xs) for xs in zip(*cols)]

    cum = each(_dot_exact_lhs, mcum, lw)
    tot = each(lambda x: jnp.sum(x, axis=0, keepdims=True), lw)
    e_neg = each(lambda x: jnp.exp(-x), cum)
    e_tail = each(lambda t_, x: jnp.exp(t_ - x), tot, cum)
    rt = each(lambda x, cm: x * jnp.exp(cm), r, cum)
    at = each(lambda x, cm, l_: x * jnp.exp(cm - l_), a, cum, lw)
    bt = each(jnp.multiply, b, e_neg)
    kt = each(jnp.multiply, k, e_neg)
    bh = each(jnp.multiply, b, e_tail)
    kh = each(jnp.multiply, k, e_tail)

    ar = each(lambda x, y_: jnp.concatenate([x, y_], axis=0), at, rt)
    pb = each(lambda x, y_: _bmm_nt(x, y_, bd16), ar, bt)
    pk = each(lambda x, y_: _bmm_nt(x, y_, bd16), ar, kt)
    a_ab = each(lambda x, m: x[:c] * m[M_STRICT], pb, masks)
    a_rb = each(lambda x, m: x[c:] * m[M_INCL], pb, masks)
    a_ak = each(lambda x, m: x[:c] * m[M_STRICT], pk, masks)
    a_rk = each(lambda x, m: x[c:] * m[M_INCL], pk, masks)

    tinv = each(lambda x, m: m[M_EYE] + x * m[M_PAIR], a_ab, masks)
    for lvl in range(N_LEVELS):
        x = each(lambda aa, m, ti: _bmm(aa * m[M_LEVEL + lvl], ti, bd16), a_ab, masks, tinv)
        tinv = each(lambda ti, x_: ti + _bmm(ti, x_, bd16), tinv, x)

    bmm = lambda p, q: _bmm(p, q, bd16)
    att = each(bmm, tinv, at)
    w1 = each(bmm, a_ak, v)
    vt = each(bmm, tinv, w1)
    rh = each(lambda x, p, q: x + _bmm(p, q, bd16), rt, a_rb, att)
    y_in = each(lambda p, q, p2, q2: _bmm(p, q, bd16) + _bmm(p2, q2, bd16), a_rb, vt, a_rk, v)
    mc = each(lambda m, t_, p, q: m[M_EYE] * jnp.exp(t_) + _bmm_tn(p, q, bd32), masks, tot, bh, att)
    nc = each(lambda p, p2, q, q2: _bmm_tn(jnp.concatenate([p, p2], axis=0), jnp.concatenate([q, q2], axis=0), bd32),
              bh, kh, vt, v)
    y = each(lambda p, q, yi: _bmm(p, q, bd16) + yi, rh, state, y_in)
    new_state = each(lambda p, q, n_: _bmm(p, q, bd16) + n_, mc, state, nc)
    return y, new_state


def _rwkv_scan_kernel(rf_ref, vf_ref, af_ref, rb_ref, vb_ref, ab_ref, lwf_ref, kf_ref, bf_ref, lwb_ref, kb_ref,
                      bb_ref, mask_ref, mcum_ref, bd16_ref, bd32_ref, yf_ref, yb_ref, state_ref, *, n_chunks):
    @pl.when(pl.program_id(1) == 0)
    def _():
        state_ref[...] = jnp.zeros_like(state_ref)

    dirs = ((0, rf_ref, lwf_ref, kf_ref, vf_ref, af_ref, bf_ref, yf_ref),
            (1, rb_ref, lwb_ref, kb_ref, vb_ref, ab_ref, bb_ref, yb_ref))
    n_seq = rf_ref.shape[0]

    def body(j, carry):
        cols = [[] for _ in range(9)]
        where = []
        for d, r_ref, lw_ref, k_ref, v_ref, a_ref, b_ref, y_ref in dirs:
            cj = j if d == 0 else n_chunks - 1 - j
            sl = pl.ds(pl.multiple_of(cj * CHUNK, CHUNK), CHUNK)
            for g in range(n_seq):
                vals = (r_ref[g, sl, :], lw_ref[g, sl, :], k_ref[g, sl, :], v_ref[g, sl, :], a_ref[g, sl, :],
                        b_ref[g, sl, :], state_ref[d, g], mask_ref.at[d], mcum_ref[d])
                for col, val in zip(cols, vals):
                    col.append(val)
                where.append((y_ref, d, g, sl))
        ys, sts = _scan_chunks(*cols, bd16_ref[...], bd32_ref[...])
        for (y_ref, d, g, sl), y, st in zip(where, ys, sts):
            y_ref[g, sl, :] = y
            state_ref[d, g] = st
        return carry

    lax.fori_loop(0, n_chunks, body, 0)


def _rwkv_scan(r, v, nkk, lw, kd, bd, tb):
    b, t, w = r.shape
    nb = t // tb
    g = SCAN_SEQS if b % SCAN_SEQS == 0 else 1
    masks, mcum = _scan_tables(w)
    bd32 = _group_matrix(HEAD_GROUP, RW_HEAD, 1.0).astype(F32)
    bd16 = bd32.astype(BF16)
    fwd = lambda bi, i: (bi, i, 0)
    bwd = lambda bi, i: (bi, nb - 1 - i, 0)
    fwd_d = lambda bi, i: (0, bi, i, 0)
    bwd_d = lambda bi, i: (1, bi, nb - 1 - i, 0)
    const = lambda nd: (lambda bi, i: (0,) * nd)
    blk = (g, tb, w)
    dblk = (None, g, tb, w)
    out = jax.ShapeDtypeStruct((b, t, w), F32)
    return pl.pallas_call(
        functools.partial(_rwkv_scan_kernel, n_chunks=tb // CHUNK),
        grid=(b // g, nb),
        in_specs=[
            pl.BlockSpec(blk, fwd), pl.BlockSpec(blk, fwd), pl.BlockSpec(blk, fwd),
            pl.BlockSpec(blk, bwd), pl.BlockSpec(blk, bwd), pl.BlockSpec(blk, bwd),
            pl.BlockSpec(dblk, fwd_d), pl.BlockSpec(dblk, fwd_d), pl.BlockSpec(dblk, fwd_d),
            pl.BlockSpec(dblk, bwd_d), pl.BlockSpec(dblk, bwd_d), pl.BlockSpec(dblk, bwd_d),
            pl.BlockSpec(masks.shape, const(4)),
            pl.BlockSpec(mcum.shape, const(3)),
            pl.BlockSpec(bd16.shape, const(2)),
            pl.BlockSpec(bd32.shape, const(2)),
        ],
        out_specs=[pl.BlockSpec(blk, fwd), pl.BlockSpec(blk, bwd)],
        out_shape=[out, out],
        scratch_shapes=[pltpu.VMEM((2, g, CHUNK, w), F32)],
        compiler_params=_cparams(("parallel", "arbitrary")),
        name="rwkv_scan",
    )(r, v, nkk, r, v, nkk, lw, kd, bd, lw, kd, bd, masks, mcum, bd16, bd32)


def _attn_kernel(slope_ref, q_ref, k_ref, v_ref, lq1_ref, lk1_ref, lq2_ref, lk2_ref, sg_ref, o_ref,
                 m_ref, l_ref, acc_ref, *, qk_dim, lambda_init):
    h = pl.program_id(1)
    i = pl.program_id(2)
    j = pl.program_id(3)
    nk = pl.num_programs(3)
    tq = q_ref.shape[0]
    tk = k_ref.shape[0]

    @pl.when(j == 0)
    def _():
        m_ref[...] = jnp.full(m_ref.shape, -jnp.inf, F32)
        l_ref[...] = jnp.zeros_like(l_ref)
        acc_ref[...] = jnp.zeros_like(acc_ref)

    q = q_ref[...]
    k = k_ref[...]
    v = v_ref[...]
    lane = lax.broadcasted_iota(jnp.int32, q.shape, 1)
    zero = jnp.zeros_like(q)
    qi = lax.broadcasted_iota(jnp.int32, (tq, tk), 0) + i * tq
    kj = lax.broadcasted_iota(jnp.int32, (tq, tk), 1) + j * tk
    bias = slope_ref[h] * jnp.abs(qi - kj).astype(F32)
    for c in range(2):
        qc = jnp.where((lane < qk_dim) == (c == 0), q, zero)
        s = lax.dot_general(qc, k, (((1,), (1,)), ((), ())), preferred_element_type=F32) - bias
        m_old = m_ref[c]
        m_new = jnp.maximum(m_old, jnp.max(s, axis=-1, keepdims=True))
        alpha = jnp.exp(m_old - m_new)
        p = jnp.exp(s - m_new)
        l_ref[c] = alpha * l_ref[c] + jnp.sum(p, axis=-1, keepdims=True)
        acc_ref[c] = alpha * acc_ref[c] + _dot(p.astype(BF16), v)
        m_ref[c] = m_new

    @pl.when(j == nk - 1)
    def _():
        lam = (jnp.exp(jnp.sum(lq1_ref[...] * lk1_ref[...], axis=-1, keepdims=True))
               - jnp.exp(jnp.sum(lq2_ref[...] * lk2_ref[...], axis=-1, keepdims=True)) + lambda_init)
        o = acc_ref[0] / l_ref[0] - lam * (acc_ref[1] / l_ref[1])
        ms = jnp.mean(o * o, axis=-1, keepdims=True)
        o_ref[...] = o * lax.rsqrt(ms + NORM_EPS) * (sg_ref[...] * (1.0 - lambda_init))


def _attention(slopes, q3, k3, v3, lq1, lk1, lq2, lk2, subln_g, heads, v_dim, lambda_init, tq, tk):
    b, t, _ = q3.shape
    small = lambda s, bi, h, i, j: (0, 0)
    grid_spec = pltpu.PrefetchScalarGridSpec(
        num_scalar_prefetch=1,
        grid=(b, heads, t // tq, t // tk),
        in_specs=[
            pl.BlockSpec((None, tq, v_dim), lambda bi, h, i, j, s: (bi, i, h)),
            pl.BlockSpec((None, tk, v_dim), lambda bi, h, i, j, s: (bi, j, h)),
            pl.BlockSpec((None, tk, v_dim), lambda bi, h, i, j, s: (bi, j, h)),
            pl.BlockSpec(lq1.shape, lambda bi, h, i, j, s: (0, 0)),
            pl.BlockSpec(lk1.shape, lambda bi, h, i, j, s: (0, 0)),
            pl.BlockSpec(lq2.shape, lambda bi, h, i, j, s: (0, 0)),
            pl.BlockSpec(lk2.shape, lambda bi, h, i, j, s: (0, 0)),
            pl.BlockSpec(subln_g.shape, lambda bi, h, i, j, s: (0, 0)),
        ],
        out_specs=pl.BlockSpec((None, tq, v_dim), lambda bi, h, i, j, s: (bi, i, h)),
        scratch_shapes=[
            pltpu.VMEM((2, tq, 1), F32),
            pltpu.VMEM((2, tq, 1), F32),
            pltpu.VMEM((2, tq, v_dim), F32),
        ],
    )
    del small
    return pl.pallas_call(
        functools.partial(_attn_kernel, qk_dim=v_dim // 2, lambda_init=lambda_init),
        grid_spec=grid_spec,
        out_shape=jax.ShapeDtypeStruct((b, t, heads * v_dim), F32),
        compiler_params=_cparams(("parallel", "parallel", "parallel", "arbitrary")),
        name="attn",
    )(slopes, q3, k3, v3, lq1, lk1, lq2, lk2, subln_g)


EXP_ZERO_ARG = 104.0
MAX_FIXED_SHIFT = 40.0
N_AUG = 6


def _attn_fixed_kernel(w_ref, slope_ref, mb_ref, q_ref, k_ref, v_ref, dbias_ref, qaug_ref, kaug_ref,
                       lq1_ref, lk1_ref, lq2_ref, lk2_ref, sg_ref, o_ref, acc_ref, l_ref, *, qk_dim, lambda_init):
    h = pl.program_id(1)
    i = pl.program_id(2)
    tb = q_ref.shape[0]
    nk = k_ref.shape[0] // tb
    slope = slope_ref[h]
    mb = mb_ref[0]
    w = w_ref[h]
    acc_ref[...] = jnp.zeros_like(acc_ref)
    l_ref[...] = jnp.zeros_like(l_ref)

    q = q_ref[...]
    lane = lax.broadcasted_iota(jnp.int32, q.shape, 1)
    lane_row = lax.broadcasted_iota(jnp.int32, (1, q.shape[1]), 1)
    is_c0 = lane < qk_dim

    def block(j, qvar, kvar, delta, diag):
        rows = pl.ds(pl.multiple_of(j * tb, tb), tb)
        kblk = k_ref[rows, :]
        vblk = v_ref[rows, :]
        for c in range(2):
            base = qk_dim * (1 - c)
            data = is_c0 if c == 0 else jnp.logical_not(is_c0)
            dyn = jnp.where(lane_row == base + 4, -mb,
                            jnp.where(lane_row == base + 5, -(slope * tb) * delta, 0.0)).astype(BF16)
            qa = jnp.where(data, q, qaug_ref[qvar, c])
            ka = jnp.where(data, kblk, kaug_ref[kvar, c] + dyn)
            s = lax.dot_general(qa, ka, (((1,), (1,)), ((), ())), preferred_element_type=F32)
            if diag:
                s = s - dbias_ref[...]
            p = jnp.exp(s)
            part = p[:, 0:V7X_LANES]
            for u in range(1, tb // V7X_LANES):
                part = part + p[:, u * V7X_LANES:(u + 1) * V7X_LANES]
            l_ref[c] += part
            acc_ref[c] += _dot(p.astype(BF16), vblk)

    def below(j, carry):
        block(j, 0, 0, (i - j).astype(F32), False)
        return carry

    def above(j, carry):
        block(j, 1, 1, (j - i).astype(F32), False)
        return carry

    lax.fori_loop(jnp.maximum(i - w, 0), i, below, 0)
    block(i, 2, 0, jnp.zeros((), F32), True)
    lax.fori_loop(i + 1, jnp.minimum(i + w, nk - 1) + 1, above, 0)

    lam = (jnp.exp(jnp.sum(lq1_ref[...] * lk1_ref[...], axis=-1, keepdims=True))
           - jnp.exp(jnp.sum(lq2_ref[...] * lk2_ref[...], axis=-1, keepdims=True)) + lambda_init)
    l0 = jnp.sum(l_ref[0], axis=-1, keepdims=True)
    l1 = jnp.sum(l_ref[1], axis=-1, keepdims=True)
    o = acc_ref[0] / l0 - lam * (acc_ref[1] / l1)
    ms = jnp.mean(o * o, axis=-1, keepdims=True)
    o_ref[...] = o * lax.rsqrt(ms + NORM_EPS) * (sg_ref[...] * (1.0 - lambda_init))


def _attn_aug_tables(slopes, tb, qk_dim):
    heads = len(slopes)
    lanes = 2 * qk_dim
    pos = jnp.arange(tb, dtype=jnp.int32)
    qaug = []
    kaug = []
    for hd in range(heads):
        sl = slopes[hd]
        qh = []
        kh = []
        for var in range(3):
            qc = []
            kc = []
            for c in range(2):
                base = qk_dim * (1 - c)
                rr = pos if var != 1 else tb - 1 - pos
                hi = (rr // 16).astype(F32) * (16.0 * sl)
                lo = (rr % 16).astype(F32) * sl
                one = jnp.ones((tb,), F32)
                tq_ = jnp.zeros((tb, lanes), F32)
                if var == 2:
                    tq_ = tq_.at[:, base + 4].set(one)
                else:
                    tq_ = (tq_.at[:, base + 0].set(-hi).at[:, base + 1].set(-lo).at[:, base + 2].set(one)
                           .at[:, base + 3].set(one).at[:, base + 4].set(one).at[:, base + 5].set(one))
                qc.append(tq_)
                if var < 2:
                    tk_ = (jnp.zeros((tb, lanes), F32).at[:, base + 0].set(one).at[:, base + 1].set(one)
                           .at[:, base + 2].set(hi).at[:, base + 3].set(lo))
                    kc.append(tk_)
            qh.append(jnp.stack(qc))
            if var < 2:
                kh.append(jnp.stack(kc))
        qaug.append(jnp.stack(qh))
        kaug.append(jnp.stack(kh))
    return jnp.stack(qaug).astype(BF16), jnp.stack(kaug).astype(BF16)


def _attention_fixed(mb, q3, k3, v3, lq1, lk1, lq2, lk2, subln_g, heads, v_dim, lambda_init, tb):
    b, t, _ = q3.shape
    qk_dim = v_dim // 2
    slopes = [2.0 ** (-8.0 * (hd + 1) / heads) for hd in range(heads)]
    widths = [int(math.ceil((EXP_ZERO_ARG / s - 1.0) / tb)) for s in slopes]
    qaug, kaug = _attn_aug_tables(slopes, tb, qk_dim)
    pos = jnp.arange(tb, dtype=jnp.int32)
    dist = jnp.abs(pos[:, None] - pos[None, :]).astype(F32)
    dbias = jnp.stack([dist * s for s in slopes])
    const = lambda bi, h, i, *_: (0, 0)
    grid_spec = pltpu.PrefetchScalarGridSpec(
        num_scalar_prefetch=3,
        grid=(b, heads, t // tb),
        in_specs=[
            pl.BlockSpec((None, tb, v_dim), lambda bi, h, i, *_: (bi, i, h)),
            pl.BlockSpec((None, t, v_dim), lambda bi, h, i, *_: (bi, 0, h)),
            pl.BlockSpec((None, t, v_dim), lambda bi, h, i, *_: (bi, 0, h)),
            pl.BlockSpec((None, tb, tb), lambda bi, h, i, *_: (h, 0, 0)),
            pl.BlockSpec((None, 3, 2, tb, v_dim), lambda bi, h, i, *_: (h, 0, 0, 0, 0)),
            pl.BlockSpec((None, 2, 2, tb, v_dim), lambda bi, h, i, *_: (h, 0, 0, 0, 0)),
            pl.BlockSpec(lq1.shape, const),
            pl.BlockSpec(lk1.shape, const),
            pl.BlockSpec(lq2.shape, const),
            pl.BlockSpec(lk2.shape, const),
            pl.BlockSpec(subln_g.shape, const),
        ],
        out_specs=pl.BlockSpec((None, tb, v_dim), lambda bi, h, i, *_: (bi, i, h)),
        scratch_shapes=[pltpu.VMEM((2, tb, v_dim), F32), pltpu.VMEM((2, tb, v_dim), F32)],
    )
    return pl.pallas_call(
        functools.partial(_attn_fixed_kernel, qk_dim=qk_dim, lambda_init=lambda_init),
        grid_spec=grid_spec,
        out_shape=jax.ShapeDtypeStruct((b, t, heads * v_dim), F32),
        compiler_params=_cparams(("parallel", "parallel", "parallel")),
        name="attn_fixed",
    )(jnp.asarray(widths, jnp.int32), jnp.asarray(slopes, F32), mb.reshape(1).astype(F32),
      q3, k3, v3, dbias, qaug, kaug, lq1, lk1, lq2, lk2, subln_g)


def _post_kernel(yf_ref, yb_ref, bonus_ref, g_ref, oda_ref, x_ref, lng_ref, lnb_ref, gmean_ref, wo_rw_ref, wo_da_ref,
                 gffn_ref, wr_ref, br_ref, h_ref, m_ref, idx_ref, gate_ref, *, gn_eps):
    gmean = gmean_ref[...]
    y = yf_ref[...] + yb_ref[...]
    mean = _dot_exact_rhs(y, gmean)
    yc = y - mean
    var = _dot_exact_rhs(yc * yc, gmean)
    yn = yc * lax.rsqrt(var + gn_eps) * lng_ref[...] + lnb_ref[...]
    o_rw = (yn + bonus_ref[...]) * g_ref[...]
    h = x_ref[...] + _dot(o_rw.astype(BF16), wo_rw_ref[...]) + _dot(oda_ref[...].astype(BF16), wo_da_ref[...])
    h_ref[...] = h
    ms = jnp.mean(h * h, axis=-1, keepdims=True)
    m = h * lax.rsqrt(ms + NORM_EPS) * gffn_ref[...]
    m_ref[...] = m

    logits = _dot_f32(m, wr_ref[...]) + br_ref[...]
    ne = logits.shape[1]
    lane = lax.broadcasted_iota(jnp.int32, logits.shape, 1)
    kcol = lax.broadcasted_iota(jnp.int32, (logits.shape[0], TOP_K), 1)
    idx_all = jnp.zeros((logits.shape[0], TOP_K), jnp.int32)
    val_all = jnp.zeros((logits.shape[0], TOP_K), F32)
    work = logits
    for kk in range(TOP_K):
        mx = jnp.max(work, axis=-1, keepdims=True)
        sel = jnp.min(jnp.where(work == mx, lane, ne), axis=-1, keepdims=True)
        idx_all = jnp.where(kcol == kk, sel, idx_all)
        val_all = jnp.where(kcol == kk, mx, val_all)
        work = jnp.where(lane == sel, -jnp.inf, work)
    e = jnp.exp(val_all - jnp.max(val_all, axis=-1, keepdims=True))
    idx_ref[...] = idx_all
    gate_ref[...] = e / jnp.sum(e, axis=-1, keepdims=True)


def _post(yf, yb, bonus, g, oda, x2, lng, lnb, gmean, wo_rw, wo_da, gffn, wr, br, gn_eps, tm):
    n, c = yf.shape
    d = x2.shape[1]
    ne = wr.shape[1]
    full = lambda i: (0, 0)
    row = lambda i: (i, 0)
    return pl.pallas_call(
        functools.partial(_post_kernel, gn_eps=gn_eps),
        grid=(n // tm,),
        in_specs=[
            pl.BlockSpec((tm, c), row),
            pl.BlockSpec((tm, c), row),
            pl.BlockSpec((tm, c), row),
            pl.BlockSpec((tm, c), row),
            pl.BlockSpec((tm, c), row),
            pl.BlockSpec((tm, d), row),
            pl.BlockSpec((1, c), full),
            pl.BlockSpec((1, c), full),
            pl.BlockSpec((c, c), full),
            pl.BlockSpec((c, d), full),
            pl.BlockSpec((c, d), full),
            pl.BlockSpec((1, d), full),
            pl.BlockSpec((d, ne), full),
            pl.BlockSpec((1, ne), full),
        ],
        out_specs=[
            pl.BlockSpec((tm, d), row),
            pl.BlockSpec((tm, d), row),
            pl.BlockSpec((tm, TOP_K), row),
            pl.BlockSpec((tm, TOP_K), row),
        ],
        out_shape=[
            jax.ShapeDtypeStruct((n, d), F32),
            jax.ShapeDtypeStruct((n, d), F32),
            jax.ShapeDtypeStruct((n, TOP_K), jnp.int32),
            jax.ShapeDtypeStruct((n, TOP_K), F32),
        ],
        compiler_params=_cparams(("parallel",)),
        name="post",
    )(yf, yb, bonus, g, oda, x2, lng, lnb, gmean, wo_rw, wo_da, gffn, wr, br)


def _rank_kernel(idx_ref, ltri_ref, rank_ref, count_ref, carry_ref):
    @pl.when(pl.program_id(0) == 0)
    def _():
        carry_ref[...] = jnp.zeros_like(carry_ref)

    idx = idx_ref[...]
    tr = idx.shape[0]
    lane = lax.broadcasted_iota(jnp.int32, (tr, N_EXPERTS), 1)
    hots = [jnp.where(lane == idx[:, kk:kk + 1], 1.0, 0.0) for kk in range(TOP_K)]
    mask = hots[0]
    for kk in range(1, TOP_K):
        mask = mask + hots[kk]
    before = _dot(ltri_ref[...], mask.astype(BF16)) + carry_ref[...]
    kcol = lax.broadcasted_iota(jnp.int32, (tr, TOP_K), 1)
    rank = jnp.zeros((tr, TOP_K), F32)
    for kk in range(TOP_K):
        rk = jnp.sum(hots[kk] * before, axis=-1, keepdims=True)
        rank = jnp.where(kcol == kk, rk, rank)
    rank_ref[...] = rank.astype(jnp.int32)
    carry_ref[...] = carry_ref[...] + jnp.sum(mask, axis=0, keepdims=True)
    count_ref[...] = carry_ref[...].astype(jnp.int32)


def _moe_rank(idx, tr):
    n = idx.shape[0]
    r = lax.broadcasted_iota(jnp.int32, (tr, tr), 0)
    c = lax.broadcasted_iota(jnp.int32, (tr, tr), 1)
    ltri = jnp.where(c < r, 1.0, 0.0).astype(BF16)
    return pl.pallas_call(
        _rank_kernel,
        grid=(n // tr,),
        in_specs=[pl.BlockSpec((tr, TOP_K), lambda i: (i, 0)), pl.BlockSpec((tr, tr), lambda i: (0, 0))],
        out_specs=[pl.BlockSpec((tr, TOP_K), lambda i: (i, 0)), pl.BlockSpec((1, N_EXPERTS), lambda i: (0, 0))],
        out_shape=[jax.ShapeDtypeStruct((n, TOP_K), jnp.int32), jax.ShapeDtypeStruct((1, N_EXPERTS), jnp.int32)],
        scratch_shapes=[pltpu.VMEM((1, N_EXPERTS), F32)],
        compiler_params=_cparams(("arbitrary",)),
        name="moe_rank",
    )(idx, ltri)


def _dispatch_kernel(offs_ref, idx_ref, rank_ref, m_ref, xs_ref, sem):
    td = m_ref.shape[0]

    def row_copy(t, kk):
        e = idx_ref[t * TOP_K + kk]
        dest = offs_ref[e] + rank_ref[t * TOP_K + kk]
        return pltpu.make_async_copy(m_ref.at[pl.ds(t, 1), :], xs_ref.at[pl.ds(dest, 1), :], sem)

    def issue(t, carry):
        for kk in range(TOP_K):
            row_copy(t, kk).start()
        return carry

    lax.fori_loop(0, td, issue, 0)

    def drain(t, carry):
        for kk in range(TOP_K):
            row_copy(t, kk).wait()
        return carry

    lax.fori_loop(0, td, drain, 0)


def _dispatch(offs, idx_flat, rank_flat, m, p_rows, td):
    n, d = m.shape
    grid_spec = pltpu.PrefetchScalarGridSpec(
        num_scalar_prefetch=1,
        grid=(n // td,),
        in_specs=[
            pl.BlockSpec((td * TOP_K,), lambda i, o: (i,), memory_space=pltpu.SMEM),
            pl.BlockSpec((td * TOP_K,), lambda i, o: (i,), memory_space=pltpu.SMEM),
            pl.BlockSpec((td, d), lambda i, o: (i, 0)),
        ],
        out_specs=pl.BlockSpec(memory_space=pl.ANY),
        scratch_shapes=[pltpu.SemaphoreType.DMA(())],
    )
    return pl.pallas_call(
        _dispatch_kernel,
        grid_spec=grid_spec,
        out_shape=jax.ShapeDtypeStruct((p_rows, d), F32),
        compiler_params=_cparams(("arbitrary",), has_side_effects=True),
        name="moe_dispatch",
    )(offs, idx_flat, rank_flat, m)


def _ffn_kernel(be_ref, nu_ref, x_ref, w1_ref, b1_ref, w2_ref, b2_ref, y_ref, *, d_ff):
    @pl.when(pl.program_id(0) < nu_ref[0])
    def _():
        x = x_ref[...].astype(BF16)
        hcat = _dot(x, w1_ref[...]) + b1_ref[...]
        glu = jnp.minimum(hcat[:, :d_ff], SWIGLU_LIMIT)
        lin = jnp.clip(hcat[:, d_ff:], -SWIGLU_LIMIT, SWIGLU_LIMIT)
        act = glu * jax.nn.sigmoid(SWIGLU_ALPHA * glu) * (lin + 1.0)
        y_ref[...] = _dot(act.astype(BF16), w2_ref[...]) + b2_ref[...]


def _moe_ffn(blk_e, n_used, xs, w1_bf, b1, w2_bf, b2, bm):
    p_rows, d = xs.shape
    ne, _, f2 = w1_bf.shape
    d_ff = f2 // 2
    grid_spec = pltpu.PrefetchScalarGridSpec(
        num_scalar_prefetch=2,
        grid=(p_rows // bm,),
        in_specs=[
            pl.BlockSpec((bm, d), lambda i, be, nu: (jnp.minimum(i, nu[0] - 1), 0)),
            pl.BlockSpec((None, d, f2), lambda i, be, nu: (be[i], 0, 0)),
            pl.BlockSpec((None, 1, f2), lambda i, be, nu: (be[i], 0, 0)),
            pl.BlockSpec((None, d_ff, d), lambda i, be, nu: (be[i], 0, 0)),
            pl.BlockSpec((None, 1, d), lambda i, be, nu: (be[i], 0, 0)),
        ],
        out_specs=pl.BlockSpec((bm, d), lambda i, be, nu: (i, 0)),
    )
    return pl.pallas_call(
        functools.partial(_ffn_kernel, d_ff=d_ff),
        grid_spec=grid_spec,
        out_shape=jax.ShapeDtypeStruct((p_rows, d), F32),
        compiler_params=_cparams(("arbitrary",)),
        name="moe_ffn",
    )(blk_e, n_used, xs, w1_bf, b1.reshape(ne, 1, f2), w2_bf, b2.reshape(ne, 1, d))


def _combine_kernel(offs_ref, idx_ref, rank_ref, h_ref, gate_ref, y_ref, o_ref, buf_ref, sem):
    tc = h_ref.shape[0]

    def row_copy(t, kk):
        e = idx_ref[t * TOP_K + kk]
        src = offs_ref[e] + rank_ref[t * TOP_K + kk]
        return pltpu.make_async_copy(y_ref.at[pl.ds(src, 1), :], buf_ref.at[kk, pl.ds(t, 1), :], sem)

    def issue(t, carry):
        for kk in range(TOP_K):
            row_copy(t, kk).start()
        return carry

    lax.fori_loop(0, tc, issue, 0)

    def drain(t, carry):
        for kk in range(TOP_K):
            row_copy(t, kk).wait()
        return carry

    lax.fori_loop(0, tc, drain, 0)

    gate = gate_ref[...]
    acc = h_ref[...]
    for kk in range(TOP_K):
        acc = acc + gate[:, kk:kk + 1] * buf_ref[kk]
    o_ref[...] = acc


def _combine(offs, idx_flat, rank_flat, h, gates, y, tc):
    n, d = h.shape
    grid_spec = pltpu.PrefetchScalarGridSpec(
        num_scalar_prefetch=1,
        grid=(n // tc,),
        in_specs=[
            pl.BlockSpec((tc * TOP_K,), lambda i, o: (i,), memory_space=pltpu.SMEM),
            pl.BlockSpec((tc * TOP_K,), lambda i, o: (i,), memory_space=pltpu.SMEM),
            pl.BlockSpec((tc, d), lambda i, o: (i, 0)),
            pl.BlockSpec((tc, TOP_K), lambda i, o: (i, 0)),
            pl.BlockSpec(memory_space=pl.ANY),
        ],
        out_specs=pl.BlockSpec((tc, d), lambda i, o: (i, 0)),
        scratch_shapes=[pltpu.VMEM((TOP_K, tc, d), F32), pltpu.SemaphoreType.DMA(())],
    )
    return pl.pallas_call(
        _combine_kernel,
        grid_spec=grid_spec,
        out_shape=jax.ShapeDtypeStruct((n, d), F32),
        compiler_params=_cparams(("arbitrary",)),
        name="moe_combine",
    )(offs, idx_flat, rank_flat, h, gates, y)


def _ple_kernel(h_ref, p_ref, wp_ref, gp_ref, wg_ref, o_ref):
    h = h_ref[...]
    e = _dot(p_ref[...].astype(BF16), wp_ref[...])
    ms = jnp.mean(e * e, axis=-1, keepdims=True)
    e = e * lax.rsqrt(ms + NORM_EPS) * gp_ref[...]
    gate = jax.nn.sigmoid(_dot(h.astype(BF16), wg_ref[...]))
    o_ref[...] = h + e * gate


def _ple(h, p2, wp_bf, gp, wg_bf, tm):
    n, d = h.shape
    pd = p2.shape[1]
    full = lambda i: (0, 0)
    row = lambda i: (i, 0)
    return pl.pallas_call(
        _ple_kernel,
        grid=(n // tm,),
        in_specs=[
            pl.BlockSpec((tm, d), row),
            pl.BlockSpec((tm, pd), row),
            pl.BlockSpec((pd, d), full),
            pl.BlockSpec((1, d), full),
            pl.BlockSpec((d, d), full),
        ],
        out_specs=pl.BlockSpec((tm, d), row),
        out_shape=jax.ShapeDtypeStruct((n, d), F32),
        compiler_params=_cparams(("parallel",)),
        name="ple",
    )(h, p2, wp_bf, gp, wg_bf)


def _tile(n, want):
    t = min(n, want)
    assert n % t == 0, (n, t)
    return t


def _layer(h3, p3, lambda_init, norm_mix_g, w_in, rw_mu, rw_w0, rw_w2, rw_a0, rw_a2, rw_g2, rw_k_k, rw_k_a, rw_r_k,
           rw_ln_g, rw_ln_b, da_q_norm_g, da_k_norm_g, da_lq1, da_lk1, da_lq2, da_lk2, da_subln_g, w_out,
           norm_ffn_g, w_router, b_router, w1, b1, w2, b2, w_ple, ple_norm_g, w_ple_gate):
    b, t, d = h3.shape
    n = b * t
    c_rw = rw_k_k.shape[0]
    lora = rw_w2.shape[1]
    rw_cols = rw_mu.shape[0]
    rw_heads = rw_r_k.shape[0]
    c_da = w_in.shape[1] - rw_cols
    assert c_da % 3 == 0
    c_da //= 3
    qk_dim = da_q_norm_g.shape[0]
    v_dim = da_subln_g.shape[0]
    da_heads = c_da // v_dim
    assert rw_heads * RW_HEAD == c_rw and c_rw % HEAD_GROUP == 0 and v_dim == V7X_LANES and 2 * qk_dim == v_dim
    assert 2 * lora == V7X_LANES and w_router.shape[1] == N_EXPERTS

    x2 = h3.reshape(n, d)
    row = lambda a: a.reshape(1, -1).astype(F32)

    gq_t = row(jnp.tile(da_q_norm_g, c_da // qk_dim))
    gk_t = row(jnp.tile(da_k_norm_g, c_da // qk_dim))
    gm_qk = _group_matrix(c_da, qk_dim, 1.0 / qk_dim)
    zrw, q, k, v = _inproj(x2, row(norm_mix_g), w_in.astype(BF16), gq_t, gk_t, gm_qk, rw_cols, c_da,
                           1.0 / math.sqrt(qk_dim), _tile(n, 256))

    zeros = jnp.zeros((lora, c_rw), F32)
    w2bd = jnp.concatenate([jnp.concatenate([rw_w2[0], zeros], axis=1),
                            jnp.concatenate([zeros, rw_w2[1]], axis=1)], axis=0).astype(BF16)
    a2bd = jnp.concatenate([jnp.concatenate([rw_a2[0], zeros], axis=1),
                            jnp.concatenate([zeros, rw_a2[1]], axis=1)], axis=0).astype(BF16)
    gsum = _group_matrix(c_rw, RW_HEAD, 1.0)
    r, vv, nkk, lw, kd, bd, g, bonus = _rwkv_prep(
        zrw.reshape(b, t, rw_cols), row(rw_mu), w2bd, row(rw_w0), a2bd, row(rw_a0), rw_g2.astype(BF16),
        row(rw_k_k), row(rw_k_a), row(rw_r_k), gsum, c_rw, lora, _tile(t, 256))
    yf, yb = _rwkv_scan(r, vv, nkk, lw, kd, bd, _tile(t, 128))

    slopes = 2.0 ** (-8.0 * jnp.arange(1, da_heads + 1, dtype=F32) / da_heads)
    ta = _tile(t, 512)
    att_args = (q.reshape(b, t, c_da), k.reshape(b, t, c_da), v.reshape(b, t, c_da),
                row(da_lq1), row(da_lk1), row(da_lq2), row(da_lk2), row(da_subln_g))
    score_bound = 1.01 * math.sqrt(qk_dim) * jnp.max(jnp.abs(da_q_norm_g)) * jnp.max(jnp.abs(da_k_norm_g))
    mb = jnp.ceil(2.0 * score_bound.astype(F32)) * 0.5
    o_da = lax.cond(
        mb <= MAX_FIXED_SHIFT,
        lambda: _attention_fixed(mb, *att_args, da_heads, v_dim, lambda_init, ta),
        lambda: _attention(slopes, *att_args, da_heads, v_dim, lambda_init, ta, ta))

    gmean = _group_matrix(c_rw, RW_HEAD, 1.0 / RW_HEAD)
    w_out_bf = w_out.astype(BF16)
    h1, m, idx, gates = _post(
        yf.reshape(n, c_rw), yb.reshape(n, c_rw), bonus.reshape(n, c_rw), g.reshape(n, c_rw), o_da.reshape(n, c_da), x2,
        row(rw_ln_g), row(rw_ln_b), gmean, w_out_bf[:c_rw], w_out_bf[c_rw:], row(norm_ffn_g),
        w_router.astype(F32), row(b_router), RW_HEAD * 1e-5, _tile(n, 256))

    bm = 256
    rank, counts = _moe_rank(idx, _tile(n, 512))
    counts = counts.reshape(N_EXPERTS)
    padded = (counts + bm - 1) // bm * bm
    pend = jnp.cumsum(padded)
    offs = (pend - padded).astype(jnp.int32)
    n_blk = (n * TOP_K) // bm + N_EXPERTS
    blk_start = jnp.arange(n_blk, dtype=jnp.int32) * bm
    blk_e = jnp.minimum(jnp.sum(blk_start[:, None] >= pend[None, :], axis=1), N_EXPERTS - 1).astype(jnp.int32)
    n_used = (pend[-1] // bm).astype(jnp.int32).reshape(1)
    idx_flat = idx.reshape(n * TOP_K)
    rank_flat = rank.reshape(n * TOP_K)
    xs = _dispatch(offs, idx_flat, rank_flat, m, n_blk * bm, _tile(n, 256))
    ys = _moe_ffn(blk_e, n_used, xs, w1.astype(BF16), b1, w2.astype(BF16), b2, bm)
    h2 = _combine(offs, idx_flat, rank_flat, h1, gates, ys, _tile(n, 256))

    out = _ple(h2, p3.reshape(n, -1), w_ple.astype(BF16), row(ple_norm_g), w_ple_gate.astype(BF16), _tile(n, 512))
    return out.reshape(b, t, d)


def kernel(x, p, norm_mix_g, w_in, rw_mu, rw_w0, rw_w2, rw_a0, rw_a2, rw_g2, rw_k_k, rw_k_a, rw_r_k, rw_ln_g,
           rw_ln_b, da_q_norm_g, da_k_norm_g, da_lq1, da_lk1, da_lq2, da_lk2, da_subln_g, w_out, norm_ffn_g,
           w_router, b_router, w1, b1, w2, b2, w_ple, ple_norm_g, w_ple_gate):
    h = x.astype(F32)
    params = (norm_mix_g, w_in, rw_mu, rw_w0, rw_w2, rw_a0, rw_a2, rw_g2, rw_k_k, rw_k_a, rw_r_k, rw_ln_g, rw_ln_b,
              da_q_norm_g, da_k_norm_g, da_lq1, da_lk1, da_lq2, da_lk2, da_subln_g, w_out, norm_ffn_g,
              w_router, b_router, w1, b1, w2, b2, w_ple, ple_norm_g, w_ple_gate)
    for i in range(p.shape[0]):
        lambda_init = 0.8 - 0.6 * math.exp(-0.3 * i)
        h = _layer(h, p[i], lambda_init, *(a[i] for a in params))
    return h.astype(x.dtype)
```

```python
import functools
import math

import jax
import jax.numpy as jnp
import numpy as np
from jax import lax
from jax.experimental import pallas as pl
from jax.experimental.pallas import tpu as pltpu

F32 = jnp.float32
BF16 = jnp.bfloat16

V7X_LANES = 128
V7X_VMEM_BYTES = 64 * 1024 * 1024
VMEM_LIMIT = 52 * 1024 * 1024

NORM_EPS = 1e-6
RW_HEAD = 64
CHUNK = 64
HEAD_GROUP = 256
N_EXPERTS = 32
TOP_K = 4
MOE_BLOCK_ROWS = 512
SWIGLU_ALPHA = 1.702
SWIGLU_LIMIT = 7.0


def _cparams(sem, **kw):
    return pltpu.CompilerParams(dimension_semantics=sem, vmem_limit_bytes=VMEM_LIMIT, **kw)


def _dot(a, b):
    return jnp.dot(a, b, preferred_element_type=F32)


def _split3(x):
    hi = x.astype(BF16)
    r1 = x - hi.astype(F32)
    mid = r1.astype(BF16)
    lo = (r1 - mid.astype(F32)).astype(BF16)
    return hi, mid, lo


def _dot_exact_rhs(x, g_bf16):
    hi, mid, lo = _split3(x)
    return _dot(hi, g_bf16) + _dot(mid, g_bf16) + _dot(lo, g_bf16)


def _dot_exact_lhs(g_bf16, x):
    hi, mid, lo = _split3(x)
    return _dot(g_bf16, hi) + _dot(g_bf16, mid) + _dot(g_bf16, lo)


def _dot_f32(a, b):
    ah, am, al = _split3(a)
    bh, bm, bl = _split3(b)
    return (_dot(ah, bh) + (_dot(ah, bm) + _dot(am, bh))
            + (_dot(am, bm) + _dot(ah, bl) + _dot(al, bh)))


def _store_rows(ref, x2):
    rows, d = x2.shape
    c = d // V7X_LANES
    for j in range(c):
        ref[pl.ds(j, rows, stride=c), :] = x2[:, j * V7X_LANES:(j + 1) * V7X_LANES]


def _load_rows(ref, c):
    rows = ref.shape[0] // c
    return jnp.concatenate([ref[pl.ds(j, rows, stride=c), :] for j in range(c)], axis=1)


def _group_matrix(width, group, value):
    g = np.arange(width) // group
    return jnp.asarray(np.where(g[:, None] == g[None, :], value, 0.0), BF16)


def _inproj_kernel(x_ref, g_ref, w_ref, gq_ref, gk_ref, gm_ref, zrw_ref, q_ref, k_ref, v_ref, *, rw_cols, c_da, scale):
    x = x_ref[...]
    ms = jnp.mean(x * x, axis=-1, keepdims=True)
    u = (x * lax.rsqrt(ms + NORM_EPS) * g_ref[...]).astype(BF16)
    z = _dot(u, w_ref[...])
    zrw_ref[...] = z[:, :rw_cols]
    zq = z[:, rw_cols:rw_cols + c_da]
    zk = z[:, rw_cols + c_da:rw_cols + 2 * c_da]
    zv = z[:, rw_cols + 2 * c_da:]
    gm = gm_ref[...]
    msq = _dot_exact_rhs(zq * zq, gm)
    msk = _dot_exact_rhs(zk * zk, gm)
    q_ref[...] = (zq * lax.rsqrt(msq + NORM_EPS) * (gq_ref[...] * scale)).astype(BF16)
    k_ref[...] = (zk * lax.rsqrt(msk + NORM_EPS) * gk_ref[...]).astype(BF16)
    v_ref[...] = zv.astype(BF16)


def _inproj(x2, g, w_bf, gq_t, gk_t, gm, rw_cols, c_da, scale, tm):
    n, d = x2.shape
    cols = w_bf.shape[1]
    full = lambda i: (0, 0)
    row = lambda i: (i, 0)
    return pl.pallas_call(
        functools.partial(_inproj_kernel, rw_cols=rw_cols, c_da=c_da, scale=scale),
        grid=(n // tm,),
        in_specs=[
            pl.BlockSpec((tm, d), row),
            pl.BlockSpec((1, d), full),
            pl.BlockSpec((d, cols), full),
            pl.BlockSpec((1, c_da), full),
            pl.BlockSpec((1, c_da), full),
            pl.BlockSpec((c_da, c_da), full),
        ],
        out_specs=[
            pl.BlockSpec((tm, rw_cols), row),
            pl.BlockSpec((tm, c_da), row),
            pl.BlockSpec((tm, c_da), row),
            pl.BlockSpec((tm, c_da), row),
        ],
        out_shape=[
            jax.ShapeDtypeStruct((n, rw_cols), F32),
            jax.ShapeDtypeStruct((n, c_da), BF16),
            jax.ShapeDtypeStruct((n, c_da), BF16),
            jax.ShapeDtypeStruct((n, c_da), BF16),
        ],
        compiler_params=_cparams(("parallel",)),
        name="inproj",
    )(x2, g, w_bf, gq_t, gk_t, gm)


def _rwkv_prep_kernel(z_ref, zp_ref, zn_ref, mu_ref, w2_ref, w0_ref, a2_ref, a0_ref, g2_ref, kk_ref, ka_ref,
                      rk_ref, gsum_ref,
                      r_ref, v_ref, nkk_ref, lw_ref, kd_ref, bd_ref, g_ref, bonus_ref, *, c_rw, lora):
    i = pl.program_id(1)
    nt = pl.num_programs(1)
    z = z_ref[...]
    tt = z.shape[0]
    rows = lax.broadcasted_iota(jnp.int32, z.shape, 0)
    prev_row = jnp.where(i == 0, 0.0, zp_ref[7:8, :])
    next_row = jnp.where(i == nt - 1, 0.0, zn_ref[0:1, :])
    zprev = jnp.where(rows == 0, prev_row, pltpu.roll(z, 1, 0))
    znext = jnp.where(rows == tt - 1, next_row, pltpu.roll(z, tt - 1, 0))
    z = z + (0.5 * (zprev + znext) - z) * mu_ref[...]

    r = z[:, 0:c_rw]
    k = z[:, c_rw:2 * c_rw]
    v = z[:, 2 * c_rw:3 * c_rw]
    o = 3 * c_rw
    wd = z[:, o:o + 2 * lora]
    ad = z[:, o + 2 * lora:o + 4 * lora]
    gd = z[:, o + 4 * lora:]

    xw = w0_ref[...] + _dot(jnp.tanh(wd).astype(BF16), w2_ref[...])
    xa = a0_ref[...] + _dot(ad.astype(BF16), a2_ref[...])
    lw = -math.exp(-0.5) * jax.nn.sigmoid(xw)
    rate = jax.nn.sigmoid(xa)
    g = _dot(jax.nn.sigmoid(gd).astype(BF16), g2_ref[...])

    gsum = gsum_ref[...]
    kkr = k * kk_ref[...]
    kk = kkr * lax.rsqrt(_dot_exact_rhs(kkr * kkr, gsum) + 1e-12)
    ka = ka_ref[...]
    kd0 = k * (1.0 + (rate[:, :c_rw] - 1.0) * ka)
    kd1 = k * (1.0 + (rate[:, c_rw:] - 1.0) * ka)
    bonus = _dot_exact_rhs(r * (0.5 * (kd0 + kd1)) * rk_ref[...], gsum) * v

    r_ref[...] = r
    v_ref[...] = v
    nkk_ref[...] = -kk
    lw_ref[0] = lw[:, :c_rw]
    lw_ref[1] = lw[:, c_rw:]
    kd_ref[0] = kd0
    kd_ref[1] = kd1
    bd_ref[0] = kk * rate[:, :c_rw]
    bd_ref[1] = kk * rate[:, c_rw:]
    g_ref[...] = g
    bonus_ref[...] = bonus


def _rwkv_prep(zrw3, mu, w2bd, w0f, a2bd, a0f, g2_bf, k_k, k_a, r_kf, gsum, c_rw, lora, tt):
    b, t, cols = zrw3.shape
    nt = t // tt
    hb = tt // 8
    nhb = t // 8
    full = lambda bi, i: (0, 0)
    tile = lambda bi, i: (bi, i, 0)
    dtile = lambda bi, i: (0, bi, i, 0)
    one = jax.ShapeDtypeStruct((b, t, c_rw), F32)
    two = jax.ShapeDtypeStruct((2, b, t, c_rw), F32)
    return pl.pallas_call(
        functools.partial(_rwkv_prep_kernel, c_rw=c_rw, lora=lora),
        grid=(b, nt),
        in_specs=[
            pl.BlockSpec((None, tt, cols), tile),
            pl.BlockSpec((None, 8, cols), lambda bi, i: (bi, jnp.maximum(i * hb - 1, 0), 0)),
            pl.BlockSpec((None, 8, cols), lambda bi, i: (bi, jnp.minimum((i + 1) * hb, nhb - 1), 0)),
            pl.BlockSpec((1, cols), full),
            pl.BlockSpec(w2bd.shape, full),
            pl.BlockSpec((1, 2 * c_rw), full),
            pl.BlockSpec(a2bd.shape, full),
            pl.BlockSpec((1, 2 * c_rw), full),
            pl.BlockSpec(g2_bf.shape, full),
            pl.BlockSpec((1, c_rw), full),
            pl.BlockSpec((1, c_rw), full),
            pl.BlockSpec((1, c_rw), full),
            pl.BlockSpec((c_rw, c_rw), full),
        ],
        out_specs=[
            pl.BlockSpec((None, tt, c_rw), tile),
            pl.BlockSpec((None, tt, c_rw), tile),
            pl.BlockSpec((None, tt, c_rw), tile),
            pl.BlockSpec((2, None, tt, c_rw), dtile),
            pl.BlockSpec((2, None, tt, c_rw), dtile),
            pl.BlockSpec((2, None, tt, c_rw), dtile),
            pl.BlockSpec((None, tt, c_rw), tile),
            pl.BlockSpec((None, tt, c_rw), tile),
        ],
        out_shape=[one, one, one, two, two, two, one, one],
        compiler_params=_cparams(("parallel", "parallel")),
        name="rwkv_prep",
    )(zrw3, zrw3, zrw3, mu, w2bd, w0f, a2bd, a0f, g2_bf, k_k, k_a, r_kf, gsum)


M_STRICT, M_INCL, M_EYE, M_PAIR, M_LEVEL = 0, 1, 2, 3, 4
N_LEVELS = 5
SCAN_SEQS = 2


def _scan_tables(w):
    c = CHUNK
    masks = []
    mcums = []
    for d in range(2):
        t = np.broadcast_to(np.arange(c)[:, None], (c, w))
        s = np.broadcast_to((np.arange(w) % RW_HEAD)[None, :], (c, w))
        tt = np.broadcast_to(np.arange(c)[:, None], (c, c))
        ss = np.broadcast_to(np.arange(c)[None, :], (c, c))
        if d == 1:
            t, s, tt, ss = c - 1 - t, c - 1 - s, c - 1 - tt, c - 1 - ss
        rows = [s < t, s <= t, s == t, (t // 2 == s // 2) & (t > s)]
        sz = 2
        while sz < c:
            rows.append((t // (2 * sz) == s // (2 * sz)) & ((t // sz) % 2 == 1) & ((s // sz) % 2 == 0))
            sz *= 2
        assert len(rows) == M_LEVEL + N_LEVELS
        masks.append(np.stack(rows).astype(np.float32))
        mcums.append((ss <= tt).astype(np.float32))
    return jnp.asarray(np.stack(masks), F32), jnp.asarray(np.stack(mcums), BF16)


def _block_diag(y, bd16):
    g = y.shape[1] // RW_HEAD
    return jnp.concatenate([y.astype(BF16)] * g, axis=0) * bd16


def _bmm(x, y, bd16):
    outs = []
    for s in range(0, y.shape[1], HEAD_GROUP):
        outs.append(_dot(x[:, s:s + HEAD_GROUP].astype(BF16), _block_diag(y[:, s:s + HEAD_GROUP], bd16)))
    return jnp.concatenate(outs, axis=1)


def _bmm_nt(x, y, bd16):
    outs = []
    for s in range(0, y.shape[1], HEAD_GROUP):
        bd = _block_diag(y[:, s:s + HEAD_GROUP], bd16)
        outs.append(lax.dot_general(x[:, s:s + HEAD_GROUP].astype(BF16), bd, (((1,), (1,)), ((), ())),
                                    preferred_element_type=F32))
    return jnp.concatenate(outs, axis=1)


def _bmm_tn(x, y, bd32):
    outs = []
    for s in range(0, y.shape[1], HEAD_GROUP):
        xt = x[:, s:s + HEAD_GROUP].T.astype(BF16)
        full = _dot(xt, y[:, s:s + HEAD_GROUP].astype(BF16)) * bd32
        o = full[0:RW_HEAD]
        for j in range(1, HEAD_GROUP // RW_HEAD):
            o = o + full[j * RW_HEAD:(j + 1) * RW_HEAD]
        outs.append(o)
    return jnp.concatenate(outs, axis=1)


def _scan_chunks(r, lw, k, v, a, b, state, masks, mcum, bd16, bd32):
    c = r[0].shape[0]

    def each(f, *cols):
        return [f(*xs) for xs in zip(*cols)]

    cum = each(_dot_exact_lhs, mcum, lw)
    tot = each(lambda x: jnp.sum(x, axis=0, keepdims=True), lw)
    e_neg = each(lambda x: jnp.exp(-x), cum)
    e_tail = each(lambda t_, x: jnp.exp(t_ - x), tot, cum)
    rt = each(lambda x, cm: x * jnp.exp(cm), r, cum)
    at = each(lambda x, cm, l_: x * jnp.exp(cm - l_), a, cum, lw)
    bt = each(jnp.multiply, b, e_neg)
    kt = each(jnp.multiply, k, e_neg)
    bh = each(jnp.multiply, b, e_tail)
    kh = each(jnp.multiply, k, e_tail)

    ar = each(lambda x, y_: jnp.concatenate([x, y_], axis=0), at, rt)
    pb = each(lambda x, y_: _bmm_nt(x, y_, bd16), ar, bt)
    pk = each(lambda x, y_: _bmm_nt(x, y_, bd16), ar, kt)
    a_ab = each(lambda x, m: x[:c] * m[M_STRICT], pb, masks)
    a_rb = each(lambda x, m: x[c:] * m[M_INCL], pb, masks)
    a_ak = each(lambda x, m: x[:c] * m[M_STRICT], pk, masks)
    a_rk = each(lambda x, m: x[c:] * m[M_INCL], pk, masks)

    tinv = each(lambda x, m: m[M_EYE] + x * m[M_PAIR], a_ab, masks)
    for lvl in range(N_LEVELS):
        x = each(lambda aa, m, ti: _bmm(aa * m[M_LEVEL + lvl], ti, bd16), a_ab, masks, tinv)
        tinv = each(lambda ti, x_: ti + _bmm(ti, x_, bd16), tinv, x)

    bmm = lambda p, q: _bmm(p, q, bd16)
    att = each(bmm, tinv, at)
    w1 = each(bmm, a_ak, v)
    vt = each(bmm, tinv, w1)
    rh = each(lambda x, p, q: x + _bmm(p, q, bd16), rt, a_rb, att)
    y_in = each(lambda p, q, p2, q2: _bmm(p, q, bd16) + _bmm(p2, q2, bd16), a_rb, vt, a_rk, v)
    mc = each(lambda m, t_, p, q: m[M_EYE] * jnp.exp(t_) + _bmm_tn(p, q, bd32), masks, tot, bh, att)
    nc = each(lambda p, p2, q, q2: _bmm_tn(jnp.concatenate([p, p2], axis=0), jnp.concatenate([q, q2], axis=0), bd32),
              bh, kh, vt, v)
    y = each(lambda p, q, yi: _bmm(p, q, bd16) + yi, rh, state, y_in)
    new_state = each(lambda p, q, n_: _bmm(p, q, bd16) + n_, mc, state, nc)
    return y, new_state


def _rwkv_scan_kernel(rf_ref, vf_ref, af_ref, rb_ref, vb_ref, ab_ref, lwf_ref, kf_ref, bf_ref, lwb_ref, kb_ref,
                      bb_ref, mask_ref, mcum_ref, bd16_ref, bd32_ref, yf_ref, yb_ref, state_ref, *, n_chunks):
    @pl.when(pl.program_id(1) == 0)
    def _():
        state_ref[...] = jnp.zeros_like(state_ref)

    dirs = ((0, rf_ref, lwf_ref, kf_ref, vf_ref, af_ref, bf_ref, yf_ref),
            (1, rb_ref, lwb_ref, kb_ref, vb_ref, ab_ref, bb_ref, yb_ref))
    n_seq = rf_ref.shape[0]

    def body(j, carry):
        cols = [[] for _ in range(9)]
        where = []
        for d, r_ref, lw_ref, k_ref, v_ref, a_ref, b_ref, y_ref in dirs:
            cj = j if d == 0 else n_chunks - 1 - j
            sl = pl.ds(pl.multiple_of(cj * CHUNK, CHUNK), CHUNK)
            for g in range(n_seq):
                vals = (r_ref[g, sl, :], lw_ref[g, sl, :], k_ref[g, sl, :], v_ref[g, sl, :], a_ref[g, sl, :],
                        b_ref[g, sl, :], state_ref[d, g], mask_ref.at[d], mcum_ref[d])
                for col, val in zip(cols, vals):
                    col.append(val)
                where.append((y_ref, d, g, sl))
        ys, sts = _scan_chunks(*cols, bd16_ref[...], bd32_ref[...])
        for (y_ref, d, g, sl), y, st in zip(where, ys, sts):
            y_ref[g, sl, :] = y
            state_ref[d, g] = st
        return carry

    lax.fori_loop(0, n_chunks, body, 0)


def _rwkv_scan(r, v, nkk, lw, kd, bd, tb):
    b, t, w = r.shape
    nb = t // tb
    g = SCAN_SEQS if b % SCAN_SEQS == 0 else 1
    masks, mcum = _scan_tables(w)
    bd32 = _group_matrix(HEAD_GROUP, RW_HEAD, 1.0).astype(F32)
    bd16 = bd32.astype(BF16)
    fwd = lambda bi, i: (bi, i, 0)
    bwd = lambda bi, i: (bi, nb - 1 - i, 0)
    fwd_d = lambda bi, i: (0, bi, i, 0)
    bwd_d = lambda bi, i: (1, bi, nb - 1 - i, 0)
    const = lambda nd: (lambda bi, i: (0,) * nd)
    blk = (g, tb, w)
    dblk = (None, g, tb, w)
    out = jax.ShapeDtypeStruct((b, t, w), F32)
    return pl.pallas_call(
        functools.partial(_rwkv_scan_kernel, n_chunks=tb // CHUNK),
        grid=(b // g, nb),
        in_specs=[
            pl.BlockSpec(blk, fwd), pl.BlockSpec(blk, fwd), pl.BlockSpec(blk, fwd),
            pl.BlockSpec(blk, bwd), pl.BlockSpec(blk, bwd), pl.BlockSpec(blk, bwd),
            pl.BlockSpec(dblk, fwd_d), pl.BlockSpec(dblk, fwd_d), pl.BlockSpec(dblk, fwd_d),
            pl.BlockSpec(dblk, bwd_d), pl.BlockSpec(dblk, bwd_d), pl.BlockSpec(dblk, bwd_d),
            pl.BlockSpec(masks.shape, const(4)),
            pl.BlockSpec(mcum.shape, const(3)),
            pl.BlockSpec(bd16.shape, const(2)),
            pl.BlockSpec(bd32.shape, const(2)),
        ],
        out_specs=[pl.BlockSpec(blk, fwd), pl.BlockSpec(blk, bwd)],
        out_shape=[out, out],
        scratch_shapes=[pltpu.VMEM((2, g, CHUNK, w), F32)],
        compiler_params=_cparams(("parallel", "arbitrary")),
        name="rwkv_scan",
    )(r, v, nkk, r, v, nkk, lw, kd, bd, lw, kd, bd, masks, mcum, bd16, bd32)


def _attn_kernel(slope_ref, q_ref, k_ref, v_ref, lq1_ref, lk1_ref, lq2_ref, lk2_ref, sg_ref, o_ref,
                 m_ref, l_ref, acc_ref, *, qk_dim, lambda_init):
    h = pl.program_id(1)
    i = pl.program_id(2)
    j = pl.program_id(3)
    nk = pl.num_programs(3)
    tq = q_ref.shape[0]
    tk = k_ref.shape[0]

    @pl.when(j == 0)
    def _():
        m_ref[...] = jnp.full(m_ref.shape, -jnp.inf, F32)
        l_ref[...] = jnp.zeros_like(l_ref)
        acc_ref[...] = jnp.zeros_like(acc_ref)

    q = q_ref[...]
    k = k_ref[...]
    v = v_ref[...]
    lane = lax.broadcasted_iota(jnp.int32, q.shape, 1)
    zero = jnp.zeros_like(q)
    qi = lax.broadcasted_iota(jnp.int32, (tq, tk), 0) + i * tq
    kj = lax.broadcasted_iota(jnp.int32, (tq, tk), 1) + j * tk
    bias = slope_ref[h] * jnp.abs(qi - kj).astype(F32)
    for c in range(2):
        qc = jnp.where((lane < qk_dim) == (c == 0), q, zero)
        s = lax.dot_general(qc, k, (((1,), (1,)), ((), ())), preferred_element_type=F32) - bias
        m_old = m_ref[c]
        m_new = jnp.maximum(m_old, jnp.max(s, axis=-1, keepdims=True))
        alpha = jnp.exp(m_old - m_new)
        p = jnp.exp(s - m_new)
        l_ref[c] = alpha * l_ref[c] + jnp.sum(p, axis=-1, keepdims=True)
        acc_ref[c] = alpha * acc_ref[c] + _dot(p.astype(BF16), v)
        m_ref[c] = m_new

    @pl.when(j == nk - 1)
    def _():
        lam = (jnp.exp(jnp.sum(lq1_ref[...] * lk1_ref[...], axis=-1, keepdims=True))
               - jnp.exp(jnp.sum(lq2_ref[...] * lk2_ref[...], axis=-1, keepdims=True)) + lambda_init)
        o = acc_ref[0] / l_ref[0] - lam * (acc_ref[1] / l_ref[1])
        ms = jnp.mean(o * o, axis=-1, keepdims=True)
        o_ref[...] = o * lax.rsqrt(ms + NORM_EPS) * (sg_ref[...] * (1.0 - lambda_init))


def _attention(slopes, q3, k3, v3, lq1, lk1, lq2, lk2, subln_g, heads, v_dim, lambda_init, tq, tk):
    b, t, _ = q3.shape
    small = lambda s, bi, h, i, j: (0, 0)
    grid_spec = pltpu.PrefetchScalarGridSpec(
        num_scalar_prefetch=1,
        grid=(b, heads, t // tq, t // tk),
        in_specs=[
            pl.BlockSpec((None, tq, v_dim), lambda bi, h, i, j, s: (bi, i, h)),
            pl.BlockSpec((None, tk, v_dim), lambda bi, h, i, j, s: (bi, j, h)),
            pl.BlockSpec((None, tk, v_dim), lambda bi, h, i, j, s: (bi, j, h)),
            pl.BlockSpec(lq1.shape, lambda bi, h, i, j, s: (0, 0)),
            pl.BlockSpec(lk1.shape, lambda bi, h, i, j, s: (0, 0)),
            pl.BlockSpec(lq2.shape, lambda bi, h, i, j, s: (0, 0)),
            pl.BlockSpec(lk2.shape, lambda bi, h, i, j, s: (0, 0)),
            pl.BlockSpec(subln_g.shape, lambda bi, h, i, j, s: (0, 0)),
        ],
        out_specs=pl.BlockSpec((None, tq, v_dim), lambda bi, h, i, j, s: (bi, i, h)),
        scratch_shapes=[
            pltpu.VMEM((2, tq, 1), F32),
            pltpu.VMEM((2, tq, 1), F32),
            pltpu.VMEM((2, tq, v_dim), F32),
        ],
    )
    del small
    return pl.pallas_call(
        functools.partial(_attn_kernel, qk_dim=v_dim // 2, lambda_init=lambda_init),
        grid_spec=grid_spec,
        out_shape=jax.ShapeDtypeStruct((b, t, heads * v_dim), F32),
        compiler_params=_cparams(("parallel", "parallel", "parallel", "arbitrary")),
        name="attn",
    )(slopes, q3, k3, v3, lq1, lk1, lq2, lk2, subln_g)


EXP_ZERO_ARG = 104.0
MAX_FIXED_SHIFT = 40.0
N_AUG = 6


def _attn_fixed_kernel(w_ref, slope_ref, mb_ref, q_ref, k_ref, v_ref, dbias_ref, qaug_ref, kaug_ref,
                       lq1_ref, lk1_ref, lq2_ref, lk2_ref, sg_ref, o_ref, acc_ref, l_ref, *, qk_dim, lambda_init):
    h = pl.program_id(1)
    i = pl.program_id(2)
    tb = q_ref.shape[0]
    nk = k_ref.shape[0] // tb
    slope = slope_ref[h]
    mb = mb_ref[0]
    w = w_ref[h]
    acc_ref[...] = jnp.zeros_like(acc_ref)
    l_ref[...] = jnp.zeros_like(l_ref)

    q = q_ref[...]
    lane = lax.broadcasted_iota(jnp.int32, q.shape, 1)
    lane_row = lax.broadcasted_iota(jnp.int32, (1, q.shape[1]), 1)
    is_c0 = lane < qk_dim

    def block(j, qvar, kvar, delta, diag):
        rows = pl.ds(pl.multiple_of(j * tb, tb), tb)
        kblk = k_ref[rows, :]
        vblk = v_ref[rows, :]
        for c in range(2):
            base = qk_dim * (1 - c)
            data = is_c0 if c == 0 else jnp.logical_not(is_c0)
            dyn = jnp.where(lane_row == base + 4, -mb,
                            jnp.where(lane_row == base + 5, -(slope * tb) * delta, 0.0)).astype(BF16)
            qa = jnp.where(data, q, qaug_ref[qvar, c])
            ka = jnp.where(data, kblk, kaug_ref[kvar, c] + dyn)
            s = lax.dot_general(qa, ka, (((1,), (1,)), ((), ())), preferred_element_type=F32)
            if diag:
                s = s - dbias_ref[...]
            p = jnp.exp(s)
            part = p[:, 0:V7X_LANES]
            for u in range(1, tb // V7X_LANES):
                part = part + p[:, u * V7X_LANES:(u + 1) * V7X_LANES]
            l_ref[c] += part
            acc_ref[c] += _dot(p.astype(BF16), vblk)

    def below(j, carry):
        block(j, 0, 0, (i - j).astype(F32), False)
        return carry

    def above(j, carry):
        block(j, 1, 1, (j - i).astype(F32), False)
        return carry

    lax.fori_loop(jnp.maximum(i - w, 0), i, below, 0)
    block(i, 2, 0, jnp.zeros((), F32), True)
    lax.fori_loop(i + 1, jnp.minimum(i + w, nk - 1) + 1, above, 0)

    lam = (jnp.exp(jnp.sum(lq1_ref[...] * lk1_ref[...], axis=-1, keepdims=True))
           - jnp.exp(jnp.sum(lq2_ref[...] * lk2_ref[...], axis=-1, keepdims=True)) + lambda_init)
    l0 = jnp.sum(l_ref[0], axis=-1, keepdims=True)
    l1 = jnp.sum(l_ref[1], axis=-1, keepdims=True)
    o = acc_ref[0] / l0 - lam * (acc_ref[1] / l1)
    ms = jnp.mean(o * o, axis=-1, keepdims=True)
    o_ref[...] = o * lax.rsqrt(ms + NORM_EPS) * (sg_ref[...] * (1.0 - lambda_init))


def _attn_aug_tables(slopes, tb, qk_dim):
    heads = len(slopes)
    lanes = 2 * qk_dim
    pos = np.arange(tb)
    qaug = np.zeros((heads, 3, 2, tb, lanes), np.float32)
    kaug = np.zeros((heads, 2, 2, tb, lanes), np.float32)
    for hd, sl in enumerate(slopes):
        for var in range(3):
            rr = pos if var != 1 else tb - 1 - pos
            hi = (rr // 16) * (16.0 * sl)
            lo = (rr % 16) * sl
            for c in range(2):
                base = qk_dim * (1 - c)
                if var == 2:
                    qaug[hd, var, c, :, base + 4] = 1.0
                    continue
                qaug[hd, var, c, :, base + 0] = -hi
                qaug[hd, var, c, :, base + 1] = -lo
                qaug[hd, var, c, :, base + 2:base + N_AUG] = 1.0
                kaug[hd, var, c, :, base + 0:base + 2] = 1.0
                kaug[hd, var, c, :, base + 2] = hi
                kaug[hd, var, c, :, base + 3] = lo
    return jnp.asarray(qaug, BF16), jnp.asarray(kaug, BF16)


def _attention_fixed(mb, q3, k3, v3, lq1, lk1, lq2, lk2, subln_g, heads, v_dim, lambda_init, tb):
    b, t, _ = q3.shape
    qk_dim = v_dim // 2
    slopes = [2.0 ** (-8.0 * (hd + 1) / heads) for hd in range(heads)]
    widths = [int(math.ceil((EXP_ZERO_ARG / s - 1.0) / tb)) for s in slopes]
    qaug, kaug = _attn_aug_tables(slopes, tb, qk_dim)
    pos = np.arange(tb)
    dist = np.abs(pos[:, None] - pos[None, :]).astype(np.float32)
    dbias = jnp.asarray(np.stack([dist * s for s in slopes]), F32)
    const = lambda bi, h, i, *_: (0, 0)
    grid_spec = pltpu.PrefetchScalarGridSpec(
        num_scalar_prefetch=3,
        grid=(b, heads, t // tb),
        in_specs=[
            pl.BlockSpec((None, tb, v_dim), lambda bi, h, i, *_: (bi, i, h)),
            pl.BlockSpec((None, t, v_dim), lambda bi, h, i, *_: (bi, 0, h)),
            pl.BlockSpec((None, t, v_dim), lambda bi, h, i, *_: (bi, 0, h)),
            pl.BlockSpec((None, tb, tb), lambda bi, h, i, *_: (h, 0, 0)),
            pl.BlockSpec((None, 3, 2, tb, v_dim), lambda bi, h, i, *_: (h, 0, 0, 0, 0)),
            pl.BlockSpec((None, 2, 2, tb, v_dim), lambda bi, h, i, *_: (h, 0, 0, 0, 0)),
            pl.BlockSpec(lq1.shape, const),
            pl.BlockSpec(lk1.shape, const),
            pl.BlockSpec(lq2.shape, const),
            pl.BlockSpec(lk2.shape, const),
            pl.BlockSpec(subln_g.shape, const),
        ],
        out_specs=pl.BlockSpec((None, tb, v_dim), lambda bi, h, i, *_: (bi, i, h)),
        scratch_shapes=[pltpu.VMEM((2, tb, v_dim), F32), pltpu.VMEM((2, tb, v_dim), F32)],
    )
    return pl.pallas_call(
        functools.partial(_attn_fixed_kernel, qk_dim=qk_dim, lambda_init=lambda_init),
        grid_spec=grid_spec,
        out_shape=jax.ShapeDtypeStruct((b, t, heads * v_dim), F32),
        compiler_params=_cparams(("parallel", "parallel", "parallel")),
        name="attn_fixed",
    )(jnp.asarray(widths, jnp.int32), jnp.asarray(slopes, F32), mb.reshape(1).astype(F32),
      q3, k3, v3, dbias, qaug, kaug, lq1, lk1, lq2, lk2, subln_g)


def _post_kernel(yf_ref, yb_ref, bonus_ref, g_ref, oda_ref, x_ref, lng_ref, lnb_ref, gmean_ref, wo_rw_ref, wo_da_ref,
                 gffn_ref, wr_ref, br_ref, h_ref, m_ref, idx_ref, gate_ref, *, gn_eps):
    gmean = gmean_ref[...]
    y = yf_ref[...] + yb_ref[...]
    mean = _dot_exact_rhs(y, gmean)
    yc = y - mean
    var = _dot_exact_rhs(yc * yc, gmean)
    yn = yc * lax.rsqrt(var + gn_eps) * lng_ref[...] + lnb_ref[...]
    o_rw = (yn + bonus_ref[...]) * g_ref[...]
    h = x_ref[...] + _dot(o_rw.astype(BF16), wo_rw_ref[...]) + _dot(oda_ref[...].astype(BF16), wo_da_ref[...])
    h_ref[...] = h
    ms = jnp.mean(h * h, axis=-1, keepdims=True)
    m = h * lax.rsqrt(ms + NORM_EPS) * gffn_ref[...]
    _store_rows(m_ref, m)

    logits = _dot_f32(m, wr_ref[...]) + br_ref[...]
    ne = logits.shape[1]
    lane = lax.broadcasted_iota(jnp.int32, logits.shape, 1)
    kcol = lax.broadcasted_iota(jnp.int32, (logits.shape[0], TOP_K), 1)
    idx_all = jnp.zeros((logits.shape[0], TOP_K), jnp.int32)
    val_all = jnp.zeros((logits.shape[0], TOP_K), F32)
    work = logits
    for kk in range(TOP_K):
        mx = jnp.max(work, axis=-1, keepdims=True)
        sel = jnp.min(jnp.where(work == mx, lane, ne), axis=-1, keepdims=True)
        idx_all = jnp.where(kcol == kk, sel, idx_all)
        val_all = jnp.where(kcol == kk, mx, val_all)
        work = jnp.where(lane == sel, -jnp.inf, work)
    e = jnp.exp(val_all - jnp.max(val_all, axis=-1, keepdims=True))
    idx_ref[...] = idx_all
    gate_ref[...] = e / jnp.sum(e, axis=-1, keepdims=True)


def _post(yf, yb, bonus, g, oda, x2, lng, lnb, gmean, wo_rw, wo_da, gffn, wr, br, gn_eps, tm):
    n, c = yf.shape
    d = x2.shape[1]
    ne = wr.shape[1]
    full = lambda i: (0, 0)
    row = lambda i: (i, 0)
    return pl.pallas_call(
        functools.partial(_post_kernel, gn_eps=gn_eps),
        grid=(n // tm,),
        in_specs=[
            pl.BlockSpec((tm, c), row),
            pl.BlockSpec((tm, c), row),
            pl.BlockSpec((tm, c), row),
            pl.BlockSpec((tm, c), row),
            pl.BlockSpec((tm, c), row),
            pl.BlockSpec((tm, d), row),
            pl.BlockSpec((1, c), full),
            pl.BlockSpec((1, c), full),
            pl.BlockSpec((c, c), full),
            pl.BlockSpec((c, d), full),
            pl.BlockSpec((c, d), full),
            pl.BlockSpec((1, d), full),
            pl.BlockSpec((d, ne), full),
            pl.BlockSpec((1, ne), full),
        ],
        out_specs=[
            pl.BlockSpec((tm, d), row),
            pl.BlockSpec((tm * d // V7X_LANES, V7X_LANES), row),
            pl.BlockSpec((tm, TOP_K), row),
            pl.BlockSpec((tm, TOP_K), row),
        ],
        out_shape=[
            jax.ShapeDtypeStruct((n, d), F32),
            jax.ShapeDtypeStruct((n * d // V7X_LANES, V7X_LANES), F32),
            jax.ShapeDtypeStruct((n, TOP_K), jnp.int32),
            jax.ShapeDtypeStruct((n, TOP_K), F32),
        ],
        compiler_params=_cparams(("parallel",)),
        name="post",
    )(yf, yb, bonus, g, oda, x2, lng, lnb, gmean, wo_rw, wo_da, gffn, wr, br)


def _rank_kernel(idx_ref, ltri_ref, rank_ref, count_ref, carry_ref):
    @pl.when(pl.program_id(0) == 0)
    def _():
        carry_ref[...] = jnp.zeros_like(carry_ref)

    idx = idx_ref[...]
    tr = idx.shape[0]
    lane = lax.broadcasted_iota(jnp.int32, (tr, N_EXPERTS), 1)
    hots = [jnp.where(lane == idx[:, kk:kk + 1], 1.0, 0.0) for kk in range(TOP_K)]
    mask = hots[0]
    for kk in range(1, TOP_K):
        mask = mask + hots[kk]
    before = _dot(ltri_ref[...], mask.astype(BF16)) + carry_ref[...]
    kcol = lax.broadcasted_iota(jnp.int32, (tr, TOP_K), 1)
    rank = jnp.zeros((tr, TOP_K), F32)
    for kk in range(TOP_K):
        rk = jnp.sum(hots[kk] * before, axis=-1, keepdims=True)
        rank = jnp.where(kcol == kk, rk, rank)
    rank_ref[...] = rank.astype(jnp.int32)
    carry_ref[...] = carry_ref[...] + jnp.sum(mask, axis=0, keepdims=True)
    count_ref[...] = carry_ref[...].astype(jnp.int32)


def _moe_rank(idx, tr):
    n = idx.shape[0]
    ltri = jnp.asarray(np.tril(np.ones((tr, tr), np.float32), -1), BF16)
    return pl.pallas_call(
        _rank_kernel,
        grid=(n // tr,),
        in_specs=[pl.BlockSpec((tr, TOP_K), lambda i: (i, 0)), pl.BlockSpec((tr, tr), lambda i: (0, 0))],
        out_specs=[pl.BlockSpec((tr, TOP_K), lambda i: (i, 0)), pl.BlockSpec((1, N_EXPERTS), lambda i: (0, 0))],
        out_shape=[jax.ShapeDtypeStruct((n, TOP_K), jnp.int32), jax.ShapeDtypeStruct((1, N_EXPERTS), jnp.int32)],
        scratch_shapes=[pltpu.VMEM((1, N_EXPERTS), F32)],
        compiler_params=_cparams(("arbitrary",)),
        name="moe_rank",
    )(idx, ltri)


ROW_UNROLL = 8


def _row(r, c):
    return pl.ds(pl.multiple_of(r * c, c), c)


def _dispatch_kernel(slot_ref, m_ref, xs_ref, sem, *, c):
    td = m_ref.shape[0] // c

    def issue(g, carry):
        for u in range(ROW_UNROLL):
            t = g * ROW_UNROLL + u
            for kk in range(TOP_K):
                dest = slot_ref[t * TOP_K + kk]
                pltpu.make_async_copy(m_ref.at[_row(t, c), :], xs_ref.at[_row(dest, c), :], sem).start(
                    priority=kk % 2)
        return carry

    lax.fori_loop(0, td // ROW_UNROLL, issue, 0)
    for kk in range(TOP_K):
        pltpu.make_async_copy(m_ref, xs_ref.at[pl.ds(0, td * c), :], sem).wait()


def _dispatch(slot_flat, mrows, c, p_rows, td):
    n = mrows.shape[0] // c
    assert td % ROW_UNROLL == 0
    return pl.pallas_call(
        functools.partial(_dispatch_kernel, c=c),
        grid=(n // td,),
        in_specs=[
            pl.BlockSpec((td * TOP_K,), lambda i: (i,), memory_space=pltpu.SMEM),
            pl.BlockSpec((td * c, V7X_LANES), lambda i: (i, 0)),
        ],
        out_specs=pl.BlockSpec(memory_space=pl.ANY),
        scratch_shapes=[pltpu.SemaphoreType.DMA(())],
        out_shape=jax.ShapeDtypeStruct((p_rows * c, V7X_LANES), F32),
        compiler_params=_cparams(("arbitrary",), has_side_effects=True),
        name="moe_dispatch",
    )(slot_flat, mrows)


def _ffn_kernel(be_ref, nu_ref, x_ref, w1_ref, b1_ref, w2_ref, b2_ref, y_ref, *, d_ff):
    @pl.when(pl.program_id(0) < nu_ref[0])
    def _():
        x = _load_rows(x_ref, w1_ref.shape[0] // V7X_LANES).astype(BF16)
        hcat = _dot(x, w1_ref[...]) + b1_ref[...]
        glu = jnp.minimum(hcat[:, :d_ff], SWIGLU_LIMIT)
        lin = jnp.clip(hcat[:, d_ff:], -SWIGLU_LIMIT, SWIGLU_LIMIT)
        act = glu * jax.nn.sigmoid(SWIGLU_ALPHA * glu) * (lin + 1.0)
        _store_rows(y_ref, _dot(act.astype(BF16), w2_ref[...]) + b2_ref[...])


def _moe_ffn(blk_e, n_used, xrows, w1_bf, b1, w2_bf, b2, bm):
    ne, d, f2 = w1_bf.shape
    d_ff = f2 // 2
    c = d // V7X_LANES
    p_rows = xrows.shape[0] // c
    blk = (bm * c, V7X_LANES)
    grid_spec = pltpu.PrefetchScalarGridSpec(
        num_scalar_prefetch=2,
        grid=(p_rows // bm,),
        in_specs=[
            pl.BlockSpec(blk, lambda i, be, nu: (jnp.minimum(i, nu[0] - 1), 0)),
            pl.BlockSpec((None, d, f2), lambda i, be, nu: (be[i], 0, 0)),
            pl.BlockSpec((None, 1, f2), lambda i, be, nu: (be[i], 0, 0)),
            pl.BlockSpec((None, d_ff, d), lambda i, be, nu: (be[i], 0, 0)),
            pl.BlockSpec((None, 1, d), lambda i, be, nu: (be[i], 0, 0)),
        ],
        out_specs=pl.BlockSpec(blk, lambda i, be, nu: (i, 0)),
    )
    return pl.pallas_call(
        functools.partial(_ffn_kernel, d_ff=d_ff),
        grid_spec=grid_spec,
        out_shape=jax.ShapeDtypeStruct(xrows.shape, F32),
        compiler_params=_cparams(("arbitrary",)),
        name="moe_ffn",
    )(blk_e, n_used, xrows, w1_bf, b1.reshape(ne, 1, f2), w2_bf, b2.reshape(ne, 1, d))


def _combine_kernel(slot_ref, h_ref, gate_ref, y_ref, o_ref, buf_ref, sem):
    tc, d = h_ref.shape
    c = d // V7X_LANES

    def issue(g, carry):
        for u in range(ROW_UNROLL):
            t = g * ROW_UNROLL + u
            for kk in range(TOP_K):
                src = slot_ref[t * TOP_K + kk]
                pltpu.make_async_copy(y_ref.at[_row(src, c), :], buf_ref.at[kk, _row(t, c), :], sem).start(
                    priority=kk % 2)
        return carry

    lax.fori_loop(0, tc // ROW_UNROLL, issue, 0)
    for kk in range(TOP_K):
        pltpu.make_async_copy(y_ref.at[pl.ds(0, tc * c), :], buf_ref.at[kk], sem).wait()

    gate = gate_ref[...]
    for j in range(c):
        cols = slice(j * V7X_LANES, (j + 1) * V7X_LANES)
        acc = h_ref[:, cols]
        for kk in range(TOP_K):
            acc = acc + gate[:, kk:kk + 1] * buf_ref[kk, pl.ds(j, tc, stride=c), :]
        o_ref[:, cols] = acc


def _combine(slot_flat, h, gates, yrows, tc):
    n, d = h.shape
    assert tc % ROW_UNROLL == 0
    return pl.pallas_call(
        _combine_kernel,
        grid=(n // tc,),
        in_specs=[
            pl.BlockSpec((tc * TOP_K,), lambda i: (i,), memory_space=pltpu.SMEM),
            pl.BlockSpec((tc, d), lambda i: (i, 0)),
            pl.BlockSpec((tc, TOP_K), lambda i: (i, 0)),
            pl.BlockSpec(memory_space=pl.ANY),
        ],
        out_specs=pl.BlockSpec((tc, d), lambda i: (i, 0)),
        scratch_shapes=[pltpu.VMEM((TOP_K, tc * d // V7X_LANES, V7X_LANES), F32), pltpu.SemaphoreType.DMA(())],
        out_shape=jax.ShapeDtypeStruct((n, d), F32),
        compiler_params=_cparams(("arbitrary",)),
        name="moe_combine",
    )(slot_flat, h, gates, yrows)


def _ple_kernel(h_ref, p_ref, wp_ref, gp_ref, wg_ref, o_ref):
    h = h_ref[...]
    e = _dot(p_ref[...].astype(BF16), wp_ref[...])
    ms = jnp.mean(e * e, axis=-1, keepdims=True)
    e = e * lax.rsqrt(ms + NORM_EPS) * gp_ref[...]
    gate = jax.nn.sigmoid(_dot(h.astype(BF16), wg_ref[...]))
    o_ref[...] = h + e * gate


def _ple(h, p2, wp_bf, gp, wg_bf, tm):
    n, d = h.shape
    pd = p2.shape[1]
    full = lambda i: (0, 0)
    row = lambda i: (i, 0)
    return pl.pallas_call(
        _ple_kernel,
        grid=(n // tm,),
        in_specs=[
            pl.BlockSpec((tm, d), row),
            pl.BlockSpec((tm, pd), row),
            pl.BlockSpec((pd, d), full),
            pl.BlockSpec((1, d), full),
            pl.BlockSpec((d, d), full),
        ],
        out_specs=pl.BlockSpec((tm, d), row),
        out_shape=jax.ShapeDtypeStruct((n, d), F32),
        compiler_params=_cparams(("parallel",)),
        name="ple",
    )(h, p2, wp_bf, gp, wg_bf)


def _tile(n, want):
    t = min(n, want)
    assert n % t == 0, (n, t)
    return t


def _layer(h3, p3, lambda_init, norm_mix_g, w_in, rw_mu, rw_w0, rw_w2, rw_a0, rw_a2, rw_g2, rw_k_k, rw_k_a, rw_r_k,
           rw_ln_g, rw_ln_b, da_q_norm_g, da_k_norm_g, da_lq1, da_lk1, da_lq2, da_lk2, da_subln_g, w_out,
           norm_ffn_g, w_router, b_router, w1, b1, w2, b2, w_ple, ple_norm_g, w_ple_gate):
    b, t, d = h3.shape
    n = b * t
    c_rw = rw_k_k.shape[0]
    lora = rw_w2.shape[1]
    rw_cols = rw_mu.shape[0]
    rw_heads = rw_r_k.shape[0]
    c_da = w_in.shape[1] - rw_cols
    assert c_da % 3 == 0
    c_da //= 3
    qk_dim = da_q_norm_g.shape[0]
    v_dim = da_subln_g.shape[0]
    da_heads = c_da // v_dim
    assert rw_heads * RW_HEAD == c_rw and c_rw % HEAD_GROUP == 0 and v_dim == V7X_LANES and 2 * qk_dim == v_dim
    assert 2 * lora == V7X_LANES and w_router.shape[1] == N_EXPERTS

    x2 = h3.reshape(n, d)
    row = lambda a: a.reshape(1, -1).astype(F32)

    gq_t = row(jnp.tile(da_q_norm_g, c_da // qk_dim))
    gk_t = row(jnp.tile(da_k_norm_g, c_da // qk_dim))
    gm_qk = _group_matrix(c_da, qk_dim, 1.0 / qk_dim)
    zrw, q, k, v = _inproj(x2, row(norm_mix_g), w_in.astype(BF16), gq_t, gk_t, gm_qk, rw_cols, c_da,
                           1.0 / math.sqrt(qk_dim), _tile(n, 256))

    zeros = jnp.zeros((lora, c_rw), F32)
    w2bd = jnp.concatenate([jnp.concatenate([rw_w2[0], zeros], axis=1),
                            jnp.concatenate([zeros, rw_w2[1]], axis=1)], axis=0).astype(BF16)
    a2bd = jnp.concatenate([jnp.concatenate([rw_a2[0], zeros], axis=1),
                            jnp.concatenate([zeros, rw_a2[1]], axis=1)], axis=0).astype(BF16)
    gsum = _group_matrix(c_rw, RW_HEAD, 1.0)
    r, vv, nkk, lw, kd, bd, g, bonus = _rwkv_prep(
        zrw.reshape(b, t, rw_cols), row(rw_mu), w2bd, row(rw_w0), a2bd, row(rw_a0), rw_g2.astype(BF16),
        row(rw_k_k), row(rw_k_a), row(rw_r_k), gsum, c_rw, lora, _tile(t, 256))
    yf, yb = _rwkv_scan(r, vv, nkk, lw, kd, bd, _tile(t, 128))

    slopes = 2.0 ** (-8.0 * jnp.arange(1, da_heads + 1, dtype=F32) / da_heads)
    ta = _tile(t, 512)
    att_args = (q.reshape(b, t, c_da), k.reshape(b, t, c_da), v.reshape(b, t, c_da),
                row(da_lq1), row(da_lk1), row(da_lq2), row(da_lk2), row(da_subln_g))
    score_bound = 1.01 * math.sqrt(qk_dim) * jnp.max(jnp.abs(da_q_norm_g)) * jnp.max(jnp.abs(da_k_norm_g))
    mb = jnp.ceil(2.0 * score_bound.astype(F32)) * 0.5
    o_da = lax.cond(
        mb <= MAX_FIXED_SHIFT,
        lambda: _attention_fixed(mb, *att_args, da_heads, v_dim, lambda_init, ta),
        lambda: _attention(slopes, *att_args, da_heads, v_dim, lambda_init, ta, ta))

    gmean = _group_matrix(c_rw, RW_HEAD, 1.0 / RW_HEAD)
    w_out_bf = w_out.astype(BF16)
    h1, m, idx, gates = _post(
        yf.reshape(n, c_rw), yb.reshape(n, c_rw), bonus.reshape(n, c_rw), g.reshape(n, c_rw), o_da.reshape(n, c_da), x2,
        row(rw_ln_g), row(rw_ln_b), gmean, w_out_bf[:c_rw], w_out_bf[c_rw:], row(norm_ffn_g),
        w_router.astype(F32), row(b_router), RW_HEAD * 1e-5, _tile(n, 256))

    bm = MOE_BLOCK_ROWS
    rank, counts = _moe_rank(idx, _tile(n, 512))
    counts = counts.reshape(N_EXPERTS)
    padded = (counts + bm - 1) // bm * bm
    pend = jnp.cumsum(padded)
    offs = (pend - padded).astype(jnp.int32)
    n_blk = (n * TOP_K) // bm + N_EXPERTS
    blk_start = jnp.arange(n_blk, dtype=jnp.int32) * bm
    blk_e = jnp.minimum(jnp.sum(blk_start[:, None] >= pend[None, :], axis=1), N_EXPERTS - 1).astype(jnp.int32)
    n_used = (pend[-1] // bm).astype(jnp.int32).reshape(1)
    slot_flat = (jnp.take(offs, idx, axis=0) + rank).reshape(n * TOP_K)
    xs = _dispatch(slot_flat, m, d // V7X_LANES, n_blk * bm, _tile(n, 256))
    ys = _moe_ffn(blk_e, n_used, xs, w1.astype(BF16), b1, w2.astype(BF16), b2, bm)
    h2 = _combine(slot_flat, h1, gates, ys, _tile(n, 256))

    out = _ple(h2, p3.reshape(n, -1), w_ple.astype(BF16), row(ple_norm_g), w_ple_gate.astype(BF16), _tile(n, 512))
    return out.reshape(b, t, d)


def kernel(x, p, norm_mix_g, w_in, rw_mu, rw_w0, rw_w2, rw_a0, rw_a2, rw_g2, rw_k_k, rw_k_a, rw_r_k, rw_ln_g,
           rw_ln_b, da_q_norm_g, da_k_norm_g, da_lq1, da_lk1, da_lq2, da_lk2, da_subln_g, w_out, norm_ffn_g,
           w_router, b_router, w1, b1, w2, b2, w_ple, ple_norm_g, w_ple_gate):
    h = x.astype(F32)
    params = (norm_mix_g, w_in, rw_mu, rw_w0, rw_w2, rw_a0, rw_a2, rw_g2, rw_k_k, rw_k_a, rw_r_k, rw_ln_g, rw_ln_b,
              da_q_norm_g, da_k_norm_g, da_lq1, da_lk1, da_lq2, da_lk2, da_subln_g, w_out, norm_ffn_g,
              w_router, b_router, w1, b1, w2, b2, w_ple, ple_norm_g, w_ple_gate)
    for i in range(p.shape[0]):
        lambda_init = 0.8 - 0.6 * math.exp(-0.3 * i)
        h = _layer(h, p[i], lambda_init, *(a[i] for a in params))
    return h.astype(x.dtype)
```

```python
import functools
import math

import jax
import jax.numpy as jnp
import numpy as np
from jax import lax
from jax.experimental import pallas as pl
from jax.experimental.pallas import tpu as pltpu

F32 = jnp.float32
BF16 = jnp.bfloat16

V7X_LANES = 128
V7X_VMEM_BYTES = 64 * 1024 * 1024
VMEM_LIMIT = 52 * 1024 * 1024

NORM_EPS = 1e-6
RW_HEAD = 64
CHUNK = 64
HEAD_GROUP = 256
N_EXPERTS = 32
TOP_K = 4
MOE_BLOCK_ROWS = 512
SWIGLU_ALPHA = 1.702
SWIGLU_LIMIT = 7.0


def _cparams(sem, **kw):
    return pltpu.CompilerParams(dimension_semantics=sem, vmem_limit_bytes=VMEM_LIMIT, **kw)


def _dot(a, b):
    return jnp.dot(a, b, preferred_element_type=F32)


def _split3(x):
    hi = x.astype(BF16)
    r1 = x - hi.astype(F32)
    mid = r1.astype(BF16)
    lo = (r1 - mid.astype(F32)).astype(BF16)
    return hi, mid, lo


def _dot_exact_lhs(g_bf16, x):
    hi, mid, lo = _split3(x)
    return _dot(g_bf16, hi) + _dot(g_bf16, mid) + _dot(g_bf16, lo)


def _split2(x):
    hi = x.astype(BF16)
    return hi, (x - hi.astype(F32)).astype(BF16)


def _dot_hilo_rhs(x, g_bf16):
    hi, lo = _split2(x)
    out = _dot(jnp.concatenate([hi, lo], axis=0), g_bf16)
    return out[:x.shape[0]] + out[x.shape[0]:]


def _dot_hilo(a, b):
    ah, al = _split2(a)
    bh, bl = _split2(b)
    return _dot(ah, bh) + (_dot(ah, bl) + _dot(al, bh))


def _store_rows(ref, x2):
    rows, d = x2.shape
    c = d // V7X_LANES
    for j in range(c):
        ref[pl.ds(j, rows, stride=c), :] = x2[:, j * V7X_LANES:(j + 1) * V7X_LANES]


def _load_rows(ref, c):
    rows = ref.shape[0] // c
    return jnp.concatenate([ref[pl.ds(j, rows, stride=c), :] for j in range(c)], axis=1)


def _group_matrix(width, group, value):
    g = np.arange(width) // group
    return jnp.asarray(np.where(g[:, None] == g[None, :], value, 0.0), BF16)


def _inproj_kernel(x_ref, g_ref, w_ref, gq_ref, gk_ref, gm_ref, zrw_ref, q_ref, k_ref, v_ref, *, rw_cols, c_da, scale):
    x = x_ref[...]
    ms = jnp.mean(x * x, axis=-1, keepdims=True)
    u = (x * lax.rsqrt(ms + NORM_EPS) * g_ref[...]).astype(BF16)
    z = _dot(u, w_ref[...])
    zrw_ref[...] = z[:, :rw_cols]
    zq = z[:, rw_cols:rw_cols + c_da]
    zk = z[:, rw_cols + c_da:rw_cols + 2 * c_da]
    zv = z[:, rw_cols + 2 * c_da:]
    tm = zq.shape[0]
    msqk = _dot(jnp.concatenate([zq * zq, zk * zk], axis=0).astype(BF16), gm_ref[...])
    q_ref[...] = (zq * lax.rsqrt(msqk[:tm] + NORM_EPS) * (gq_ref[...] * scale)).astype(BF16)
    k_ref[...] = (zk * lax.rsqrt(msqk[tm:] + NORM_EPS) * gk_ref[...]).astype(BF16)
    v_ref[...] = zv.astype(BF16)


def _inproj(x2, g, w_bf, gq_t, gk_t, gm, rw_cols, c_da, scale, tm):
    n, d = x2.shape
    cols = w_bf.shape[1]
    full = lambda i: (0, 0)
    row = lambda i: (i, 0)
    return pl.pallas_call(
        functools.partial(_inproj_kernel, rw_cols=rw_cols, c_da=c_da, scale=scale),
        grid=(n // tm,),
        in_specs=[
            pl.BlockSpec((tm, d), row),
            pl.BlockSpec((1, d), full),
            pl.BlockSpec((d, cols), full),
            pl.BlockSpec((1, c_da), full),
            pl.BlockSpec((1, c_da), full),
            pl.BlockSpec((c_da, c_da), full),
        ],
        out_specs=[
            pl.BlockSpec((tm, rw_cols), row),
            pl.BlockSpec((tm, c_da), row),
            pl.BlockSpec((tm, c_da), row),
            pl.BlockSpec((tm, c_da), row),
        ],
        out_shape=[
            jax.ShapeDtypeStruct((n, rw_cols), F32),
            jax.ShapeDtypeStruct((n, c_da), BF16),
            jax.ShapeDtypeStruct((n, c_da), BF16),
            jax.ShapeDtypeStruct((n, c_da), BF16),
        ],
        compiler_params=_cparams(("parallel",)),
        name="inproj",
    )(x2, g, w_bf, gq_t, gk_t, gm)


def _rwkv_prep_kernel(z_ref, zp_ref, zn_ref, mu_ref, w2_ref, w0_ref, a2_ref, a0_ref, g2_ref, kk_ref, ka_ref,
                      rk_ref, gsum_ref,
                      r_ref, v_ref, nkk_ref, lw_ref, kd_ref, bd_ref, g_ref, bonus_ref, *, c_rw, lora):
    i = pl.program_id(1)
    nt = pl.num_programs(1)
    z = z_ref[...]
    tt = z.shape[0]
    rows = lax.broadcasted_iota(jnp.int32, z.shape, 0)
    prev_row = jnp.where(i == 0, 0.0, zp_ref[7:8, :])
    next_row = jnp.where(i == nt - 1, 0.0, zn_ref[0:1, :])
    zprev = jnp.where(rows == 0, prev_row, pltpu.roll(z, 1, 0))
    znext = jnp.where(rows == tt - 1, next_row, pltpu.roll(z, tt - 1, 0))
    z = z + (0.5 * (zprev + znext) - z) * mu_ref[...]

    r = z[:, 0:c_rw]
    k = z[:, c_rw:2 * c_rw]
    v = z[:, 2 * c_rw:3 * c_rw]
    o = 3 * c_rw
    wd = z[:, o:o + 2 * lora]
    ad = z[:, o + 2 * lora:o + 4 * lora]
    gd = z[:, o + 4 * lora:]

    xw = w0_ref[...] + _dot(jnp.tanh(wd).astype(BF16), w2_ref[...])
    xa = a0_ref[...] + _dot(ad.astype(BF16), a2_ref[...])
    lw = -math.exp(-0.5) * jax.nn.sigmoid(xw)
    rate = jax.nn.sigmoid(xa)
    g = _dot(jax.nn.sigmoid(gd).astype(BF16), g2_ref[...])

    gsum = gsum_ref[...]
    kkr = k * kk_ref[...]
    ka = ka_ref[...]
    kd0 = k * (1.0 + (rate[:, :c_rw] - 1.0) * ka)
    kd1 = k * (1.0 + (rate[:, c_rw:] - 1.0) * ka)
    sums = _dot_hilo_rhs(jnp.concatenate([kkr * kkr, r * (0.5 * (kd0 + kd1)) * rk_ref[...]], axis=0), gsum)
    kk = kkr * lax.rsqrt(sums[:tt] + 1e-12)
    bonus = sums[tt:] * v

    r_ref[...] = r
    v_ref[...] = v
    nkk_ref[...] = -kk
    lw_ref[0] = lw[:, :c_rw]
    lw_ref[1] = lw[:, c_rw:]
    kd_ref[0] = kd0
    kd_ref[1] = kd1
    bd_ref[0] = kk * rate[:, :c_rw]
    bd_ref[1] = kk * rate[:, c_rw:]
    g_ref[...] = g
    bonus_ref[...] = bonus


def _rwkv_prep(zrw3, mu, w2bd, w0f, a2bd, a0f, g2_bf, k_k, k_a, r_kf, gsum, c_rw, lora, tt):
    b, t, cols = zrw3.shape
    nt = t // tt
    hb = tt // 8
    nhb = t // 8
    full = lambda bi, i: (0, 0)
    tile = lambda bi, i: (bi, i, 0)
    dtile = lambda bi, i: (0, bi, i, 0)
    one = jax.ShapeDtypeStruct((b, t, c_rw), F32)
    two = jax.ShapeDtypeStruct((2, b, t, c_rw), F32)
    return pl.pallas_call(
        functools.partial(_rwkv_prep_kernel, c_rw=c_rw, lora=lora),
        grid=(b, nt),
        in_specs=[
            pl.BlockSpec((None, tt, cols), tile),
            pl.BlockSpec((None, 8, cols), lambda bi, i: (bi, jnp.maximum(i * hb - 1, 0), 0)),
            pl.BlockSpec((None, 8, cols), lambda bi, i: (bi, jnp.minimum((i + 1) * hb, nhb - 1), 0)),
            pl.BlockSpec((1, cols), full),
            pl.BlockSpec(w2bd.shape, full),
            pl.BlockSpec((1, 2 * c_rw), full),
            pl.BlockSpec(a2bd.shape, full),
            pl.BlockSpec((1, 2 * c_rw), full),
            pl.BlockSpec(g2_bf.shape, full),
            pl.BlockSpec((1, c_rw), full),
            pl.BlockSpec((1, c_rw), full),
            pl.BlockSpec((1, c_rw), full),
            pl.BlockSpec((c_rw, c_rw), full),
        ],
        out_specs=[
            pl.BlockSpec((None, tt, c_rw), tile),
            pl.BlockSpec((None, tt, c_rw), tile),
            pl.BlockSpec((None, tt, c_rw), tile),
            pl.BlockSpec((2, None, tt, c_rw), dtile),
            pl.BlockSpec((2, None, tt, c_rw), dtile),
            pl.BlockSpec((2, None, tt, c_rw), dtile),
            pl.BlockSpec((None, tt, c_rw), tile),
            pl.BlockSpec((None, tt, c_rw), tile),
        ],
        out_shape=[one, one, one, two, two, two, one, one],
        compiler_params=_cparams(("parallel", "parallel")),
        name="rwkv_prep",
    )(zrw3, zrw3, zrw3, mu, w2bd, w0f, a2bd, a0f, g2_bf, k_k, k_a, r_kf, gsum)


M_STRICT, M_INCL, M_EYE, M_PAIR, M_LEVEL = 0, 1, 2, 3, 4
N_LEVELS = 5
SCAN_SEQS = 2


def _scan_tables(w):
    c = CHUNK
    masks = []
    mcums = []
    for d in range(2):
        t = np.broadcast_to(np.arange(c)[:, None], (c, w))
        s = np.broadcast_to((np.arange(w) % RW_HEAD)[None, :], (c, w))
        tt = np.broadcast_to(np.arange(c)[:, None], (c, c))
        ss = np.broadcast_to(np.arange(c)[None, :], (c, c))
        if d == 1:
            t, s, tt, ss = c - 1 - t, c - 1 - s, c - 1 - tt, c - 1 - ss
        rows = [s < t, s <= t, s == t, (t // 2 == s // 2) & (t > s)]
        sz = 2
        while sz < c:
            rows.append((t // (2 * sz) == s // (2 * sz)) & ((t // sz) % 2 == 1) & ((s // sz) % 2 == 0))
            sz *= 2
        assert len(rows) == M_LEVEL + N_LEVELS
        masks.append(np.stack(rows).astype(np.float32))
        mcums.append((ss <= tt).astype(np.float32))
    return jnp.asarray(np.stack(masks), F32), jnp.asarray(np.stack(mcums), BF16)


def _block_diag(y, bd16):
    g = y.shape[1] // RW_HEAD
    return jnp.concatenate([y.astype(BF16)] * g, axis=0) * bd16


def _bmm(x, y, bd16):
    outs = []
    for s in range(0, y.shape[1], HEAD_GROUP):
        outs.append(_dot(x[:, s:s + HEAD_GROUP].astype(BF16), _block_diag(y[:, s:s + HEAD_GROUP], bd16)))
    return jnp.concatenate(outs, axis=1)


def _bmm_nt(x, y, bd16):
    outs = []
    for s in range(0, y.shape[1], HEAD_GROUP):
        bd = _block_diag(y[:, s:s + HEAD_GROUP], bd16)
        outs.append(lax.dot_general(x[:, s:s + HEAD_GROUP].astype(BF16), bd, (((1,), (1,)), ((), ())),
                                    preferred_element_type=F32))
    return jnp.concatenate(outs, axis=1)


def _bmm_tn(x, y, bd32):
    outs = []
    for s in range(0, y.shape[1], HEAD_GROUP):
        xt = x[:, s:s + HEAD_GROUP].T.astype(BF16)
        full = _dot(xt, y[:, s:s + HEAD_GROUP].astype(BF16)) * bd32
        o = full[0:RW_HEAD]
        for j in range(1, HEAD_GROUP // RW_HEAD):
            o = o + full[j * RW_HEAD:(j + 1) * RW_HEAD]
        outs.append(o)
    return jnp.concatenate(outs, axis=1)


def _scan_chunks(r, lw, k, v, a, b, state, masks, mcum, bd16, bd32):
    c = r[0].shape[0]

    def each(f, *cols):
        return [f(*xs) for xs in zip(*cols)]

    cum = each(_dot_exact_lhs, mcum, lw)
    tot = each(lambda x: jnp.sum(x, axis=0, keepdims=True), lw)
    e_neg = each(lambda x: jnp.exp(-x), cum)
    e_tail = each(lambda t_, x: jnp.exp(t_ - x), tot, cum)
    rt = each(lambda x, cm: x * jnp.exp(cm), r, cum)
    at = each(lambda x, cm, l_: x * jnp.exp(cm - l_), a, cum, lw)
    bt = each(jnp.multiply, b, e_neg)
    kt = each(jnp.multiply, k, e_neg)
    bh = each(jnp.multiply, b, e_tail)
    kh = each(jnp.multiply, k, e_tail)

    ar = each(lambda x, y_: jnp.concatenate([x, y_], axis=0), at, rt)
    pb = each(lambda x, y_: _bmm_nt(x, y_, bd16), ar, bt)
    pk = each(lambda x, y_: _bmm_nt(x, y_, bd16), ar, kt)
    a_ab = each(lambda x, m: x[:c] * m[M_STRICT], pb, masks)
    a_rb = each(lambda x, m: x[c:] * m[M_INCL], pb, masks)
    a_ak = each(lambda x, m: x[:c] * m[M_STRICT], pk, masks)
    a_rk = each(lambda x, m: x[c:] * m[M_INCL], pk, masks)

    tinv = each(lambda x, m: m[M_EYE] + x * m[M_PAIR], a_ab, masks)
    for lvl in range(N_LEVELS):
        x = each(lambda aa, m, ti: _bmm(aa * m[M_LEVEL + lvl], ti, bd16), a_ab, masks, tinv)
        tinv = each(lambda ti, x_: ti + _bmm(ti, x_, bd16), tinv, x)

    bmm = lambda p, q: _bmm(p, q, bd16)
    att = each(bmm, tinv, at)
    w1 = each(bmm, a_ak, v)
    vt = each(bmm, tinv, w1)
    rh = each(lambda x, p, q: x + _bmm(p, q, bd16), rt, a_rb, att)
    y_in = each(lambda p, q, p2, q2: _bmm(p, q, bd16) + _bmm(p2, q2, bd16), a_rb, vt, a_rk, v)
    mc = each(lambda m, t_, p, q: m[M_EYE] * jnp.exp(t_) + _bmm_tn(p, q, bd32), masks, tot, bh, att)
    nc = each(lambda p, p2, q, q2: _bmm_tn(jnp.concatenate([p, p2], axis=0), jnp.concatenate([q, q2], axis=0), bd32),
              bh, kh, vt, v)
    y = each(lambda p, q, yi: _bmm(p, q, bd16) + yi, rh, state, y_in)
    new_state = each(lambda p, q, n_: _bmm(p, q, bd16) + n_, mc, state, nc)
    return y, new_state


def _rwkv_scan_kernel(rf_ref, vf_ref, af_ref, rb_ref, vb_ref, ab_ref, lwf_ref, kf_ref, bf_ref, lwb_ref, kb_ref,
                      bb_ref, mask_ref, mcum_ref, bd16_ref, bd32_ref, yf_ref, yb_ref, state_ref, *, n_chunks):
    @pl.when(pl.program_id(1) == 0)
    def _():
        state_ref[...] = jnp.zeros_like(state_ref)

    dirs = ((0, rf_ref, lwf_ref, kf_ref, vf_ref, af_ref, bf_ref, yf_ref),
            (1, rb_ref, lwb_ref, kb_ref, vb_ref, ab_ref, bb_ref, yb_ref))
    n_seq = rf_ref.shape[0]

    def body(j, carry):
        cols = [[] for _ in range(9)]
        where = []
        for d, r_ref, lw_ref, k_ref, v_ref, a_ref, b_ref, y_ref in dirs:
            cj = j if d == 0 else n_chunks - 1 - j
            sl = pl.ds(pl.multiple_of(cj * CHUNK, CHUNK), CHUNK)
            for g in range(n_seq):
                vals = (r_ref[g, sl, :], lw_ref[g, sl, :], k_ref[g, sl, :], v_ref[g, sl, :], a_ref[g, sl, :],
                        b_ref[g, sl, :], state_ref[d, g], mask_ref.at[d], mcum_ref[d])
                for col, val in zip(cols, vals):
                    col.append(val)
                where.append((y_ref, d, g, sl))
        ys, sts = _scan_chunks(*cols, bd16_ref[...], bd32_ref[...])
        for (y_ref, d, g, sl), y, st in zip(where, ys, sts):
            y_ref[g, sl, :] = y
            state_ref[d, g] = st
        return carry

    lax.fori_loop(0, n_chunks, body, 0)


def _rwkv_scan(r, v, nkk, lw, kd, bd, tb):
    b, t, w = r.shape
    nb = t // tb
    g = SCAN_SEQS if b % SCAN_SEQS == 0 else 1
    masks, mcum = _scan_tables(w)
    bd32 = _group_matrix(HEAD_GROUP, RW_HEAD, 1.0).astype(F32)
    bd16 = bd32.astype(BF16)
    fwd = lambda bi, i: (bi, i, 0)
    bwd = lambda bi, i: (bi, nb - 1 - i, 0)
    fwd_d = lambda bi, i: (0, bi, i, 0)
    bwd_d = lambda bi, i: (1, bi, nb - 1 - i, 0)
    const = lambda nd: (lambda bi, i: (0,) * nd)
    blk = (g, tb, w)
    dblk = (None, g, tb, w)
    out = jax.ShapeDtypeStruct((b, t, w), F32)
    return pl.pallas_call(
        functools.partial(_rwkv_scan_kernel, n_chunks=tb // CHUNK),
        grid=(b // g, nb),
        in_specs=[
            pl.BlockSpec(blk, fwd), pl.BlockSpec(blk, fwd), pl.BlockSpec(blk, fwd),
            pl.BlockSpec(blk, bwd), pl.BlockSpec(blk, bwd), pl.BlockSpec(blk, bwd),
            pl.BlockSpec(dblk, fwd_d), pl.BlockSpec(dblk, fwd_d), pl.BlockSpec(dblk, fwd_d),
            pl.BlockSpec(dblk, bwd_d), pl.BlockSpec(dblk, bwd_d), pl.BlockSpec(dblk, bwd_d),
            pl.BlockSpec(masks.shape, const(4)),
            pl.BlockSpec(mcum.shape, const(3)),
            pl.BlockSpec(bd16.shape, const(2)),
            pl.BlockSpec(bd32.shape, const(2)),
        ],
        out_specs=[pl.BlockSpec(blk, fwd), pl.BlockSpec(blk, bwd)],
        out_shape=[out, out],
        scratch_shapes=[pltpu.VMEM((2, g, CHUNK, w), F32)],
        compiler_params=_cparams(("parallel", "arbitrary")),
        name="rwkv_scan",
    )(r, v, nkk, r, v, nkk, lw, kd, bd, lw, kd, bd, masks, mcum, bd16, bd32)


def _attn_kernel(slope_ref, q_ref, k_ref, v_ref, lq1_ref, lk1_ref, lq2_ref, lk2_ref, sg_ref, o_ref,
                 m_ref, l_ref, acc_ref, *, qk_dim, lambda_init):
    h = pl.program_id(1)
    i = pl.program_id(2)
    j = pl.program_id(3)
    nk = pl.num_programs(3)
    tq = q_ref.shape[0]
    tk = k_ref.shape[0]

    @pl.when(j == 0)
    def _():
        m_ref[...] = jnp.full(m_ref.shape, -jnp.inf, F32)
        l_ref[...] = jnp.zeros_like(l_ref)
        acc_ref[...] = jnp.zeros_like(acc_ref)

    q = q_ref[...]
    k = k_ref[...]
    v = v_ref[...]
    lane = lax.broadcasted_iota(jnp.int32, q.shape, 1)
    zero = jnp.zeros_like(q)
    qi = lax.broadcasted_iota(jnp.int32, (tq, tk), 0) + i * tq
    kj = lax.broadcasted_iota(jnp.int32, (tq, tk), 1) + j * tk
    bias = slope_ref[h] * jnp.abs(qi - kj).astype(F32)
    for c in range(2):
        qc = jnp.where((lane < qk_dim) == (c == 0), q, zero)
        s = lax.dot_general(qc, k, (((1,), (1,)), ((), ())), preferred_element_type=F32) - bias
        m_old = m_ref[c]
        m_new = jnp.maximum(m_old, jnp.max(s, axis=-1, keepdims=True))
        alpha = jnp.exp(m_old - m_new)
        p = jnp.exp(s - m_new)
        l_ref[c] = alpha * l_ref[c] + jnp.sum(p, axis=-1, keepdims=True)
        acc_ref[c] = alpha * acc_ref[c] + _dot(p.astype(BF16), v)
        m_ref[c] = m_new

    @pl.when(j == nk - 1)
    def _():
        lam = (jnp.exp(jnp.sum(lq1_ref[...] * lk1_ref[...], axis=-1, keepdims=True))
               - jnp.exp(jnp.sum(lq2_ref[...] * lk2_ref[...], axis=-1, keepdims=True)) + lambda_init)
        o = acc_ref[0] / l_ref[0] - lam * (acc_ref[1] / l_ref[1])
        ms = jnp.mean(o * o, axis=-1, keepdims=True)
        o_ref[...] = o * lax.rsqrt(ms + NORM_EPS) * (sg_ref[...] * (1.0 - lambda_init))


def _attention(slopes, q3, k3, v3, lq1, lk1, lq2, lk2, subln_g, heads, v_dim, lambda_init, tq, tk):
    b, t, _ = q3.shape
    small = lambda s, bi, h, i, j: (0, 0)
    grid_spec = pltpu.PrefetchScalarGridSpec(
        num_scalar_prefetch=1,
        grid=(b, heads, t // tq, t // tk),
        in_specs=[
            pl.BlockSpec((None, tq, v_dim), lambda bi, h, i, j, s: (bi, i, h)),
            pl.BlockSpec((None, tk, v_dim), lambda bi, h, i, j, s: (bi, j, h)),
            pl.BlockSpec((None, tk, v_dim), lambda bi, h, i, j, s: (bi, j, h)),
            pl.BlockSpec(lq1.shape, lambda bi, h, i, j, s: (0, 0)),
            pl.BlockSpec(lk1.shape, lambda bi, h, i, j, s: (0, 0)),
            pl.BlockSpec(lq2.shape, lambda bi, h, i, j, s: (0, 0)),
            pl.BlockSpec(lk2.shape, lambda bi, h, i, j, s: (0, 0)),
            pl.BlockSpec(subln_g.shape, lambda bi, h, i, j, s: (0, 0)),
        ],
        out_specs=pl.BlockSpec((None, tq, v_dim), lambda bi, h, i, j, s: (bi, i, h)),
        scratch_shapes=[
            pltpu.VMEM((2, tq, 1), F32),
            pltpu.VMEM((2, tq, 1), F32),
            pltpu.VMEM((2, tq, v_dim), F32),
        ],
    )
    del small
    return pl.pallas_call(
        functools.partial(_attn_kernel, qk_dim=v_dim // 2, lambda_init=lambda_init),
        grid_spec=grid_spec,
        out_shape=jax.ShapeDtypeStruct((b, t, heads * v_dim), F32),
        compiler_params=_cparams(("parallel", "parallel", "parallel", "arbitrary")),
        name="attn",
    )(slopes, q3, k3, v3, lq1, lk1, lq2, lk2, subln_g)


EXP_ZERO_ARG = 104.0
MAX_FIXED_SHIFT = 40.0
N_AUG = 6
ATTN_KEY_SPLIT = 2
ATTN_V_PAD = 16


def _attn_fixed_kernel(w_ref, slope_ref, mb_ref, q_ref, k_ref, vt_ref, dbias_ref, qaug_ref, kaug_ref,
                       lq1_ref, lk1_ref, lq2_ref, lk2_ref, sg_ref, o_ref, acc_ref, *, qk_dim, lambda_init):
    h = pl.program_id(1)
    i = pl.program_id(2)
    tb = q_ref.shape[0]
    nk = k_ref.shape[0] // tb
    v_dim = 2 * qk_dim
    slope = slope_ref[h]
    mb = mb_ref[0]
    w = w_ref[h]
    acc_ref[...] = jnp.zeros_like(acc_ref)

    q = q_ref[...]
    lane = lax.broadcasted_iota(jnp.int32, q.shape, 1)
    lane_row = lax.broadcasted_iota(jnp.int32, (1, q.shape[1]), 1)
    is_c0 = lane < qk_dim

    def block(j, qvar, kvar, delta, diag):
        kblk = k_ref[pl.ds(pl.multiple_of(j * tb, tb), tb), :]
        vtblk = vt_ref[j]
        sub = tb // ATTN_KEY_SPLIT
        chains = []
        for c in range(2):
            base = qk_dim * (1 - c)
            data = is_c0 if c == 0 else jnp.logical_not(is_c0)
            dyn = jnp.where(lane_row == base + 4, -mb,
                            jnp.where(lane_row == base + 5, -(slope * tb) * delta, 0.0)).astype(BF16)
            qa = jnp.where(data, q, qaug_ref[qvar, c])
            ka = jnp.where(data, kblk, kaug_ref[kvar, c] + dyn)
            for u in range(ATTN_KEY_SPLIT):
                chains.append((c, u, qa, ka[u * sub:(u + 1) * sub]))
        sts = [lax.dot_general(ka, qa, (((1,), (1,)), ((), ())), preferred_element_type=F32)
               for _, _, qa, ka in chains]
        if diag:
            sts = [st - dbias_ref[u * sub:(u + 1) * sub, :] for st, (_, u, _, _) in zip(sts, chains)]
        pts = [jnp.exp(st).astype(BF16) for st in sts]
        for c in range(2):
            upd = None
            for pt, (cc, u, _, _) in zip(pts, chains):
                if cc == c:
                    part = _dot(vtblk[:, u * sub:(u + 1) * sub], pt)
                    upd = part if upd is None else upd + part
            acc_ref[c] += upd

    def below(j, carry):
        block(j, 0, 0, (i - j).astype(F32), False)
        return carry

    def above(j, carry):
        block(j, 1, 1, (j - i).astype(F32), False)
        return carry

    lax.fori_loop(jnp.maximum(i - w, 0), i, below, 0)
    block(i, 2, 0, jnp.zeros((), F32), True)
    lax.fori_loop(i + 1, jnp.minimum(i + w, nk - 1) + 1, above, 0)

    lam = (jnp.exp(jnp.sum(lq1_ref[...] * lk1_ref[...], axis=-1, keepdims=True))
           - jnp.exp(jnp.sum(lq2_ref[...] * lk2_ref[...], axis=-1, keepdims=True)) + lambda_init)
    a0 = acc_ref[0]
    a1 = acc_ref[1]
    ot = a0[:v_dim] / a0[v_dim:v_dim + 1] - lam * (a1[:v_dim] / a1[v_dim:v_dim + 1])
    ms = jnp.mean(ot * ot, axis=0, keepdims=True)
    o_ref[...] = (ot * lax.rsqrt(ms + NORM_EPS) * (sg_ref[...] * (1.0 - lambda_init))).T


def _attn_aug_tables(slopes, tb, qk_dim):
    heads = len(slopes)
    lanes = 2 * qk_dim
    pos = np.arange(tb)
    qaug = np.zeros((heads, 3, 2, tb, lanes), np.float32)
    kaug = np.zeros((heads, 2, 2, tb, lanes), np.float32)
    for hd, sl in enumerate(slopes):
        for var in range(3):
            rr = pos if var != 1 else tb - 1 - pos
            hi = (rr // 16) * (16.0 * sl)
            lo = (rr % 16) * sl
            for c in range(2):
                base = qk_dim * (1 - c)
                if var == 2:
                    qaug[hd, var, c, :, base + 4] = 1.0
                    continue
                qaug[hd, var, c, :, base + 0] = -hi
                qaug[hd, var, c, :, base + 1] = -lo
                qaug[hd, var, c, :, base + 2:base + N_AUG] = 1.0
                kaug[hd, var, c, :, base + 0:base + 2] = 1.0
                kaug[hd, var, c, :, base + 2] = hi
                kaug[hd, var, c, :, base + 3] = lo
    return jnp.asarray(qaug, BF16), jnp.asarray(kaug, BF16)


def _attention_fixed(mb, q3, k3, v3, lq1, lk1, lq2, lk2, subln_g, heads, v_dim, lambda_init, tb):
    b, t, _ = q3.shape
    qk_dim = v_dim // 2
    slopes = [2.0 ** (-8.0 * (hd + 1) / heads) for hd in range(heads)]
    widths = [int(math.ceil((EXP_ZERO_ARG / s - 1.0) / tb)) for s in slopes]
    qaug, kaug = _attn_aug_tables(slopes, tb, qk_dim)
    pos = np.arange(tb)
    dist = np.abs(pos[:, None] - pos[None, :]).astype(np.float32)
    dbias = jnp.asarray(np.stack([dist * s for s in slopes]), F32)
    nb = t // tb
    vrows = v_dim + ATTN_V_PAD
    vt = v3.reshape(b, nb, tb, heads, v_dim).transpose(0, 3, 1, 4, 2)
    extra = jnp.zeros((b, heads, nb, ATTN_V_PAD, tb), BF16).at[:, :, :, 0, :].set(1.0)
    vt = jnp.concatenate([vt, extra], axis=3)
    const = lambda bi, h, i, *_: (0, 0)
    grid_spec = pltpu.PrefetchScalarGridSpec(
        num_scalar_prefetch=3,
        grid=(b, heads, t // tb),
        in_specs=[
            pl.BlockSpec((None, tb, v_dim), lambda bi, h, i, *_: (bi, i, h)),
            pl.BlockSpec((None, t, v_dim), lambda bi, h, i, *_: (bi, 0, h)),
            pl.BlockSpec((None, None, nb, vrows, tb), lambda bi, h, i, *_: (bi, h, 0, 0, 0)),
            pl.BlockSpec((None, tb, tb), lambda bi, h, i, *_: (h, 0, 0)),
            pl.BlockSpec((None, 3, 2, tb, v_dim), lambda bi, h, i, *_: (h, 0, 0, 0, 0)),
            pl.BlockSpec((None, 2, 2, tb, v_dim), lambda bi, h, i, *_: (h, 0, 0, 0, 0)),
            pl.BlockSpec(lq1.shape, const),
            pl.BlockSpec(lk1.shape, const),
            pl.BlockSpec(lq2.shape, const),
            pl.BlockSpec(lk2.shape, const),
            pl.BlockSpec((v_dim, 1), const),
        ],
        out_specs=pl.BlockSpec((None, tb, v_dim), lambda bi, h, i, *_: (bi, i, h)),
        scratch_shapes=[pltpu.VMEM((2, vrows, tb), F32)],
    )
    return pl.pallas_call(
        functools.partial(_attn_fixed_kernel, qk_dim=qk_dim, lambda_init=lambda_init),
        grid_spec=grid_spec,
        out_shape=jax.ShapeDtypeStruct((b, t, heads * v_dim), F32),
        compiler_params=_cparams(("parallel", "parallel", "parallel")),
        name="attn_fixed",
    )(jnp.asarray(widths, jnp.int32), jnp.asarray(slopes, F32), mb.reshape(1).astype(F32),
      q3, k3, vt, dbias, qaug, kaug, lq1, lk1, lq2, lk2, subln_g.reshape(v_dim, 1))


def _post_kernel(yf_ref, yb_ref, bonus_ref, g_ref, oda_ref, x_ref, lng_ref, lnb_ref, gmean_ref, wo_rw_ref, wo_da_ref,
                 gffn_ref, wr_ref, br_ref, h_ref, m_ref, idx_ref, gate_ref, *, gn_eps):
    gmean = gmean_ref[...]
    y = yf_ref[...] + yb_ref[...]
    mean = _dot_hilo_rhs(y, gmean)
    yc = y - mean
    var = _dot_hilo_rhs(yc * yc, gmean)
    yn = yc * lax.rsqrt(var + gn_eps) * lng_ref[...] + lnb_ref[...]
    o_rw = (yn + bonus_ref[...]) * g_ref[...]
    h = x_ref[...] + _dot(o_rw.astype(BF16), wo_rw_ref[...]) + _dot(oda_ref[...].astype(BF16), wo_da_ref[...])
    h_ref[...] = h
    ms = jnp.mean(h * h, axis=-1, keepdims=True)
    m = h * lax.rsqrt(ms + NORM_EPS) * gffn_ref[...]
    _store_rows(m_ref, m)

    logits = _dot_hilo(m, wr_ref[...]) + br_ref[...]
    ne = logits.shape[1]
    lane = lax.broadcasted_iota(jnp.int32, logits.shape, 1)
    kcol = lax.broadcasted_iota(jnp.int32, (logits.shape[0], TOP_K), 1)
    idx_all = jnp.zeros((logits.shape[0], TOP_K), jnp.int32)
    val_all = jnp.zeros((logits.shape[0], TOP_K), F32)
    work = logits
    for kk in range(TOP_K):
        mx = jnp.max(work, axis=-1, keepdims=True)
        sel = jnp.min(jnp.where(work == mx, lane, ne), axis=-1, keepdims=True)
        idx_all = jnp.where(kcol == kk, sel, idx_all)
        val_all = jnp.where(kcol == kk, mx, val_all)
        work = jnp.where(lane == sel, -jnp.inf, work)
    e = jnp.exp(val_all - jnp.max(val_all, axis=-1, keepdims=True))
    idx_ref[...] = idx_all
    gate_ref[...] = e / jnp.sum(e, axis=-1, keepdims=True)


def _post(yf, yb, bonus, g, oda, x2, lng, lnb, gmean, wo_rw, wo_da, gffn, wr, br, gn_eps, tm):
    n, c = yf.shape
    d = x2.shape[1]
    ne = wr.shape[1]
    full = lambda i: (0, 0)
    row = lambda i: (i, 0)
    return pl.pallas_call(
        functools.partial(_post_kernel, gn_eps=gn_eps),
        grid=(n // tm,),
        in_specs=[
            pl.BlockSpec((tm, c), row),
            pl.BlockSpec((tm, c), row),
            pl.BlockSpec((tm, c), row),
            pl.BlockSpec((tm, c), row),
            pl.BlockSpec((tm, c), row),
            pl.BlockSpec((tm, d), row),
            pl.BlockSpec((1, c), full),
            pl.BlockSpec((1, c), full),
            pl.BlockSpec((c, c), full),
            pl.BlockSpec((c, d), full),
            pl.BlockSpec((c, d), full),
            pl.BlockSpec((1, d), full),
            pl.BlockSpec((d, ne), full),
            pl.BlockSpec((1, ne), full),
        ],
        out_specs=[
            pl.BlockSpec((tm, d), row),
            pl.BlockSpec((tm * d // V7X_LANES, V7X_LANES), row),
            pl.BlockSpec((tm, TOP_K), row),
            pl.BlockSpec((tm, TOP_K), row),
        ],
        out_shape=[
            jax.ShapeDtypeStruct((n, d), F32),
            jax.ShapeDtypeStruct((n * d // V7X_LANES, V7X_LANES), F32),
            jax.ShapeDtypeStruct((n, TOP_K), jnp.int32),
            jax.ShapeDtypeStruct((n, TOP_K), F32),
        ],
        compiler_params=_cparams(("parallel",)),
        name="post",
    )(yf, yb, bonus, g, oda, x2, lng, lnb, gmean, wo_rw, wo_da, gffn, wr, br)


def _rank_kernel(idx_ref, ltri_ref, rank_ref, count_ref, carry_ref):
    @pl.when(pl.program_id(0) == 0)
    def _():
        carry_ref[...] = jnp.zeros_like(carry_ref)

    idx = idx_ref[...]
    tr = idx.shape[0]
    lane = lax.broadcasted_iota(jnp.int32, (tr, N_EXPERTS), 1)
    hots = [jnp.where(lane == idx[:, kk:kk + 1], 1.0, 0.0) for kk in range(TOP_K)]
    mask = hots[0]
    for kk in range(1, TOP_K):
        mask = mask + hots[kk]
    before = _dot(ltri_ref[...], mask.astype(BF16)) + carry_ref[...]
    kcol = lax.broadcasted_iota(jnp.int32, (tr, TOP_K), 1)
    rank = jnp.zeros((tr, TOP_K), F32)
    for kk in range(TOP_K):
        rk = jnp.sum(hots[kk] * before, axis=-1, keepdims=True)
        rank = jnp.where(kcol == kk, rk, rank)
    rank_ref[...] = rank.astype(jnp.int32)
    carry_ref[...] = carry_ref[...] + jnp.sum(mask, axis=0, keepdims=True)
    count_ref[...] = carry_ref[...].astype(jnp.int32)


def _moe_rank(idx, tr):
    n = idx.shape[0]
    ltri = jnp.asarray(np.tril(np.ones((tr, tr), np.float32), -1), BF16)
    return pl.pallas_call(
        _rank_kernel,
        grid=(n // tr,),
        in_specs=[pl.BlockSpec((tr, TOP_K), lambda i: (i, 0)), pl.BlockSpec((tr, tr), lambda i: (0, 0))],
        out_specs=[pl.BlockSpec((tr, TOP_K), lambda i: (i, 0)), pl.BlockSpec((1, N_EXPERTS), lambda i: (0, 0))],
        out_shape=[jax.ShapeDtypeStruct((n, TOP_K), jnp.int32), jax.ShapeDtypeStruct((1, N_EXPERTS), jnp.int32)],
        scratch_shapes=[pltpu.VMEM((1, N_EXPERTS), F32)],
        compiler_params=_cparams(("arbitrary",)),
        name="moe_rank",
    )(idx, ltri)


ROW_UNROLL = 8


def _row(r, c):
    return pl.ds(pl.multiple_of(r * c, c), c)


def _dispatch_kernel(slot_ref, m_ref, xs_ref, sem, *, c):
    td = m_ref.shape[0] // c

    def issue(g, carry):
        for u in range(ROW_UNROLL):
            t = g * ROW_UNROLL + u
            for kk in range(TOP_K):
                dest = slot_ref[t * TOP_K + kk]
                pltpu.make_async_copy(m_ref.at[_row(t, c), :], xs_ref.at[_row(dest, c), :], sem).start(
                    priority=kk % 2)
        return carry

    lax.fori_loop(0, td // ROW_UNROLL, issue, 0)
    for kk in range(TOP_K):
        pltpu.make_async_copy(m_ref, xs_ref.at[pl.ds(0, td * c), :], sem).wait()


def _dispatch(slot_flat, mrows, c, p_rows, td):
    n = mrows.shape[0] // c
    assert td % ROW_UNROLL == 0
    return pl.pallas_call(
        functools.partial(_dispatch_kernel, c=c),
        grid=(n // td,),
        in_specs=[
            pl.BlockSpec((td * TOP_K,), lambda i: (i,), memory_space=pltpu.SMEM),
            pl.BlockSpec((td * c, V7X_LANES), lambda i: (i, 0)),
        ],
        out_specs=pl.BlockSpec(memory_space=pl.ANY),
        scratch_shapes=[pltpu.SemaphoreType.DMA(())],
        out_shape=jax.ShapeDtypeStruct((p_rows * c, V7X_LANES), F32),
        compiler_params=_cparams(("arbitrary",), has_side_effects=True),
        name="moe_dispatch",
    )(slot_flat, mrows)


def _ffn_kernel(be_ref, nu_ref, x_ref, w1_ref, b1_ref, w2_ref, b2_ref, y_ref, *, d_ff):
    @pl.when(pl.program_id(0) < nu_ref[0])
    def _():
        x = _load_rows(x_ref, w1_ref.shape[0] // V7X_LANES).astype(BF16)
        hcat = _dot(x, w1_ref[...]) + b1_ref[...]
        glu = jnp.minimum(hcat[:, :d_ff], SWIGLU_LIMIT)
        lin = jnp.clip(hcat[:, d_ff:], -SWIGLU_LIMIT, SWIGLU_LIMIT)
        act = glu * jax.nn.sigmoid(SWIGLU_ALPHA * glu) * (lin + 1.0)
        _store_rows(y_ref, _dot(act.astype(BF16), w2_ref[...]) + b2_ref[...])


def _moe_ffn(blk_e, n_used, xrows, w1_bf, b1, w2_bf, b2, bm):
    ne, d, f2 = w1_bf.shape
    d_ff = f2 // 2
    c = d // V7X_LANES
    p_rows = xrows.shape[0] // c
    blk = (bm * c, V7X_LANES)
    grid_spec = pltpu.PrefetchScalarGridSpec(
        num_scalar_prefetch=2,
        grid=(p_rows // bm,),
        in_specs=[
            pl.BlockSpec(blk, lambda i, be, nu: (jnp.minimum(i, nu[0] - 1), 0)),
            pl.BlockSpec((None, d, f2), lambda i, be, nu: (be[i], 0, 0)),
            pl.BlockSpec((None, 1, f2), lambda i, be, nu: (be[i], 0, 0)),
            pl.BlockSpec((None, d_ff, d), lambda i, be, nu: (be[i], 0, 0)),
            pl.BlockSpec((None, 1, d), lambda i, be, nu: (be[i], 0, 0)),
        ],
        out_specs=pl.BlockSpec(blk, lambda i, be, nu: (i, 0)),
    )
    return pl.pallas_call(
        functools.partial(_ffn_kernel, d_ff=d_ff),
        grid_spec=grid_spec,
        out_shape=jax.ShapeDtypeStruct(xrows.shape, F32),
        compiler_params=_cparams(("arbitrary",)),
        name="moe_ffn",
    )(blk_e, n_used, xrows, w1_bf, b1.reshape(ne, 1, f2), w2_bf, b2.reshape(ne, 1, d))


def _combine_kernel(slot_ref, h_ref, gate_ref, y_ref, o_ref, buf_ref, sem):
    tc, d = h_ref.shape
    c = d // V7X_LANES

    def issue(g, carry):
        for u in range(ROW_UNROLL):
            t = g * ROW_UNROLL + u
            for kk in range(TOP_K):
                src = slot_ref[t * TOP_K + kk]
                pltpu.make_async_copy(y_ref.at[_row(src, c), :], buf_ref.at[kk, _row(t, c), :], sem).start(
                    priority=kk % 2)
        return carry

    lax.fori_loop(0, tc // ROW_UNROLL, issue, 0)
    for kk in range(TOP_K):
        pltpu.make_async_copy(y_ref.at[pl.ds(0, tc * c), :], buf_ref.at[kk], sem).wait()

    gate = gate_ref[...]
    for j in range(c):
        cols = slice(j * V7X_LANES, (j + 1) * V7X_LANES)
        acc = h_ref[:, cols]
        for kk in range(TOP_K):
            acc = acc + gate[:, kk:kk + 1] * buf_ref[kk, pl.ds(j, tc, stride=c), :]
        o_ref[:, cols] = acc


def _combine(slot_flat, h, gates, yrows, tc):
    n, d = h.shape
    assert tc % ROW_UNROLL == 0
    return pl.pallas_call(
        _combine_kernel,
        grid=(n // tc,),
        in_specs=[
            pl.BlockSpec((tc * TOP_K,), lambda i: (i,), memory_space=pltpu.SMEM),
            pl.BlockSpec((tc, d), lambda i: (i, 0)),
            pl.BlockSpec((tc, TOP_K), lambda i: (i, 0)),
            pl.BlockSpec(memory_space=pl.ANY),
        ],
        out_specs=pl.BlockSpec((tc, d), lambda i: (i, 0)),
        scratch_shapes=[pltpu.VMEM((TOP_K, tc * d // V7X_LANES, V7X_LANES), F32), pltpu.SemaphoreType.DMA(())],
        out_shape=jax.ShapeDtypeStruct((n, d), F32),
        compiler_params=_cparams(("arbitrary",)),
        name="moe_combine",
    )(slot_flat, h, gates, yrows)


def _ple_kernel(h_ref, p_ref, wp_ref, gp_ref, wg_ref, o_ref):
    h = h_ref[...]
    e = _dot(p_ref[...].astype(BF16), wp_ref[...])
    ms = jnp.mean(e * e, axis=-1, keepdims=True)
    e = e * lax.rsqrt(ms + NORM_EPS) * gp_ref[...]
    gate = jax.nn.sigmoid(_dot(h.astype(BF16), wg_ref[...]))
    o_ref[...] = h + e * gate


def _ple(h, p2, wp_bf, gp, wg_bf, tm):
    n, d = h.shape
    pd = p2.shape[1]
    full = lambda i: (0, 0)
    row = lambda i: (i, 0)
    return pl.pallas_call(
        _ple_kernel,
        grid=(n // tm,),
        in_specs=[
            pl.BlockSpec((tm, d), row),
            pl.BlockSpec((tm, pd), row),
            pl.BlockSpec((pd, d), full),
            pl.BlockSpec((1, d), full),
            pl.BlockSpec((d, d), full),
        ],
        out_specs=pl.BlockSpec((tm, d), row),
        out_shape=jax.ShapeDtypeStruct((n, d), F32),
        compiler_params=_cparams(("parallel",)),
        name="ple",
    )(h, p2, wp_bf, gp, wg_bf)


def _tile(n, want):
    t = min(n, want)
    assert n % t == 0, (n, t)
    return t


def _layer(h3, p3, lambda_init, norm_mix_g, w_in, rw_mu, rw_w0, rw_w2, rw_a0, rw_a2, rw_g2, rw_k_k, rw_k_a, rw_r_k,
           rw_ln_g, rw_ln_b, da_q_norm_g, da_k_norm_g, da_lq1, da_lk1, da_lq2, da_lk2, da_subln_g, w_out,
           norm_ffn_g, w_router, b_router, w1, b1, w2, b2, w_ple, ple_norm_g, w_ple_gate):
    b, t, d = h3.shape
    n = b * t
    c_rw = rw_k_k.shape[0]
    lora = rw_w2.shape[1]
    rw_cols = rw_mu.shape[0]
    rw_heads = rw_r_k.shape[0]
    c_da = w_in.shape[1] - rw_cols
    assert c_da % 3 == 0
    c_da //= 3
    qk_dim = da_q_norm_g.shape[0]
    v_dim = da_subln_g.shape[0]
    da_heads = c_da // v_dim
    assert rw_heads * RW_HEAD == c_rw and c_rw % HEAD_GROUP == 0 and v_dim == V7X_LANES and 2 * qk_dim == v_dim
    assert 2 * lora == V7X_LANES and w_router.shape[1] == N_EXPERTS

    x2 = h3.reshape(n, d)
    row = lambda a: a.reshape(1, -1).astype(F32)

    gq_t = row(jnp.tile(da_q_norm_g, c_da // qk_dim))
    gk_t = row(jnp.tile(da_k_norm_g, c_da // qk_dim))
    gm_qk = _group_matrix(c_da, qk_dim, 1.0 / qk_dim)
    zrw, q, k, v = _inproj(x2, row(norm_mix_g), w_in.astype(BF16), gq_t, gk_t, gm_qk, rw_cols, c_da,
                           1.0 / math.sqrt(qk_dim), _tile(n, 256))

    zeros = jnp.zeros((lora, c_rw), F32)
    w2bd = jnp.concatenate([jnp.concatenate([rw_w2[0], zeros], axis=1),
                            jnp.concatenate([zeros, rw_w2[1]], axis=1)], axis=0).astype(BF16)
    a2bd = jnp.concatenate([jnp.concatenate([rw_a2[0], zeros], axis=1),
                            jnp.concatenate([zeros, rw_a2[1]], axis=1)], axis=0).astype(BF16)
    gsum = _group_matrix(c_rw, RW_HEAD, 1.0)
    r, vv, nkk, lw, kd, bd, g, bonus = _rwkv_prep(
        zrw.reshape(b, t, rw_cols), row(rw_mu), w2bd, row(rw_w0), a2bd, row(rw_a0), rw_g2.astype(BF16),
        row(rw_k_k), row(rw_k_a), row(rw_r_k), gsum, c_rw, lora, _tile(t, 256))
    yf, yb = _rwkv_scan(r, vv, nkk, lw, kd, bd, _tile(t, 128))

    slopes = 2.0 ** (-8.0 * jnp.arange(1, da_heads + 1, dtype=F32) / da_heads)
    ta = _tile(t, 512)
    att_args = (q.reshape(b, t, c_da), k.reshape(b, t, c_da), v.reshape(b, t, c_da),
                row(da_lq1), row(da_lk1), row(da_lq2), row(da_lk2), row(da_subln_g))
    score_bound = 1.01 * math.sqrt(qk_dim) * jnp.max(jnp.abs(da_q_norm_g)) * jnp.max(jnp.abs(da_k_norm_g))
    mb = jnp.ceil(2.0 * score_bound.astype(F32)) * 0.5
    o_da = lax.cond(
        mb <= MAX_FIXED_SHIFT,
        lambda: _attention_fixed(mb, *att_args, da_heads, v_dim, lambda_init, ta),
        lambda: _attention(slopes, *att_args, da_heads, v_dim, lambda_init, ta, ta))

    gmean = _group_matrix(c_rw, RW_HEAD, 1.0 / RW_HEAD)
    w_out_bf = w_out.astype(BF16)
    h1, m, idx, gates = _post(
        yf.reshape(n, c_rw), yb.reshape(n, c_rw), bonus.reshape(n, c_rw), g.reshape(n, c_rw), o_da.reshape(n, c_da), x2,
        row(rw_ln_g), row(rw_ln_b), gmean, w_out_bf[:c_rw], w_out_bf[c_rw:], row(norm_ffn_g),
        w_router.astype(F32), row(b_router), RW_HEAD * 1e-5, _tile(n, 256))

    bm = MOE_BLOCK_ROWS
    rank, counts = _moe_rank(idx, _tile(n, 512))
    counts = counts.reshape(N_EXPERTS)
    padded = (counts + bm - 1) // bm * bm
    pend = jnp.cumsum(padded)
    offs = (pend - padded).astype(jnp.int32)
    n_blk = (n * TOP_K) // bm + N_EXPERTS
    blk_start = jnp.arange(n_blk, dtype=jnp.int32) * bm
    blk_e = jnp.minimum(jnp.sum(blk_start[:, None] >= pend[None, :], axis=1), N_EXPERTS - 1).astype(jnp.int32)
    n_used = (pend[-1] // bm).astype(jnp.int32).reshape(1)
    slot_flat = (jnp.take(offs, idx, axis=0) + rank).reshape(n * TOP_K)
    xs = _dispatch(slot_flat, m, d // V7X_LANES, n_blk * bm, _tile(n, 256))
    ys = _moe_ffn(blk_e, n_used, xs, w1.astype(BF16), b1, w2.astype(BF16), b2, bm)
    h2 = _combine(slot_flat, h1, gates, ys, _tile(n, 256))

    out = _ple(h2, p3.reshape(n, -1), w_ple.astype(BF16), row(ple_norm_g), w_ple_gate.astype(BF16), _tile(n, 512))
    return out.reshape(b, t, d)


def kernel(x, p, norm_mix_g, w_in, rw_mu, rw_w0, rw_w2, rw_a0, rw_a2, rw_g2, rw_k_k, rw_k_a, rw_r_k, rw_ln_g,
           rw_ln_b, da_q_norm_g, da_k_norm_g, da_lq1, da_lk1, da_lq2, da_lk2, da_subln_g, w_out, norm_ffn_g,
           w_router, b_router, w1, b1, w2, b2, w_ple, ple_norm_g, w_ple_gate):
    h = x.astype(F32)
    params = (norm_mix_g, w_in, rw_mu, rw_w0, rw_w2, rw_a0, rw_a2, rw_g2, rw_k_k, rw_k_a, rw_r_k, rw_ln_g, rw_ln_b,
              da_q_norm_g, da_k_norm_g, da_lq1, da_lk1, da_lq2, da_lk2, da_subln_g, w_out, norm_ffn_g,
              w_router, b_router, w1, b1, w2, b2, w_ple, ple_norm_g, w_ple_gate)
    for i in range(p.shape[0]):
        lambda_init = 0.8 - 0.6 * math.exp(-0.3 * i)
        h = _layer(h, p[i], lambda_init, *(a[i] for a in params))
    return h.astype(x.dtype)
```

```python
import functools
import math

import jax
import jax.numpy as jnp
import numpy as np
from jax import lax
from jax.experimental import pallas as pl
from jax.experimental.pallas import tpu as pltpu

F32 = jnp.float32
BF16 = jnp.bfloat16

V7X_LANES = 128
V7X_VMEM_BYTES = 64 * 1024 * 1024
VMEM_LIMIT = 52 * 1024 * 1024

NORM_EPS = 1e-6
RW_HEAD = 64
CHUNK = 64
HEAD_GROUP = 256
N_EXPERTS = 32
TOP_K = 4
MOE_BLOCK_ROWS = 512
SWIGLU_ALPHA = 1.702
SWIGLU_LIMIT = 7.0


def _cparams(sem, **kw):
    return pltpu.CompilerParams(dimension_semantics=sem, vmem_limit_bytes=VMEM_LIMIT, **kw)


def _dot(a, b):
    return jnp.dot(a, b, preferred_element_type=F32)


def _dot_hilo_lhs(g_bf16, x):
    hi, lo = _split2(x)
    return _dot(g_bf16, hi) + _dot(g_bf16, lo)


def _split2(x):
    hi = x.astype(BF16)
    return hi, (x - hi.astype(F32)).astype(BF16)


def _dot_hilo_rhs(x, g_bf16):
    hi, lo = _split2(x)
    out = _dot(jnp.concatenate([hi, lo], axis=0), g_bf16)
    return out[:x.shape[0]] + out[x.shape[0]:]


def _dot_hilo(a, b):
    ah, al = _split2(a)
    bh, bl = _split2(b)
    return _dot(ah, bh) + (_dot(ah, bl) + _dot(al, bh))


def _store_rows(ref, x2):
    rows, d = x2.shape
    c = d // V7X_LANES
    for j in range(c):
        ref[pl.ds(j, rows, stride=c), :] = x2[:, j * V7X_LANES:(j + 1) * V7X_LANES]


def _load_rows(ref, c):
    rows = ref.shape[0] // c
    return jnp.concatenate([ref[pl.ds(j, rows, stride=c), :] for j in range(c)], axis=1)


def _group_matrix(width, group, value):
    g = np.arange(width) // group
    return jnp.asarray(np.where(g[:, None] == g[None, :], value, 0.0), BF16)


def _inproj_kernel(x_ref, g_ref, w_ref, gq_ref, gk_ref, gm_ref, zrw_ref, q_ref, k_ref, vt_ref, *, rw_cols, c_da, scale):
    x = x_ref[...]
    ms = jnp.mean(x * x, axis=-1, keepdims=True)
    u = (x * lax.rsqrt(ms + NORM_EPS) * g_ref[...]).astype(BF16)
    z = _dot(u, w_ref[...])
    zrw_ref[...] = z[:, :rw_cols]
    zq = z[:, rw_cols:rw_cols + c_da]
    zk = z[:, rw_cols + c_da:rw_cols + 2 * c_da]
    zv = z[:, rw_cols + 2 * c_da:]
    tm = zq.shape[0]
    msqk = _dot(jnp.concatenate([zq * zq, zk * zk], axis=0).astype(BF16), gm_ref[...])
    q_ref[...] = (zq * lax.rsqrt(msqk[:tm] + NORM_EPS) * (gq_ref[...] * scale)).astype(BF16)
    k_ref[...] = (zk * lax.rsqrt(msqk[tm:] + NORM_EPS) * gk_ref[...]).astype(BF16)
    heads, vrows, _ = vt_ref.shape
    v_dim = c_da // heads
    pad_row = lax.broadcasted_iota(jnp.int32, (vrows - v_dim, tm), 0)
    pad = jnp.where(pad_row == 0, 1.0, 0.0).astype(BF16)
    for hd in range(heads):
        vt_ref[hd, 0:v_dim, :] = zv[:, hd * v_dim:(hd + 1) * v_dim].T.astype(BF16)
        vt_ref[hd, v_dim:, :] = pad


def _inproj(x2, g, w_bf, gq_t, gk_t, gm, rw_cols, c_da, scale, tm, b, heads, tb):
    n, d = x2.shape
    cols = w_bf.shape[1]
    t = n // b
    tps = t // tm
    per = tb // tm
    vrows = c_da // heads + ATTN_V_PAD
    full = lambda i: (0, 0)
    row = lambda i: (i, 0)
    return pl.pallas_call(
        functools.partial(_inproj_kernel, rw_cols=rw_cols, c_da=c_da, scale=scale),
        grid=(n // tm,),
        in_specs=[
            pl.BlockSpec((tm, d), row),
            pl.BlockSpec((1, d), full),
            pl.BlockSpec((d, cols), full),
            pl.BlockSpec((1, c_da), full),
            pl.BlockSpec((1, c_da), full),
            pl.BlockSpec((c_da, c_da), full),
        ],
        out_specs=[
            pl.BlockSpec((tm, rw_cols), row),
            pl.BlockSpec((tm, c_da), row),
            pl.BlockSpec((tm, c_da), row),
            pl.BlockSpec((None, heads, None, vrows, tm),
                         lambda i: (i // tps, 0, (i % tps) // per, 0, (i % tps) % per)),
        ],
        out_shape=[
            jax.ShapeDtypeStruct((n, rw_cols), F32),
            jax.ShapeDtypeStruct((n, c_da), BF16),
            jax.ShapeDtypeStruct((n, c_da), BF16),
            jax.ShapeDtypeStruct((b, heads, t // tb, vrows, tb), BF16),
        ],
        compiler_params=_cparams(("parallel",)),
        name="inproj",
    )(x2, g, w_bf, gq_t, gk_t, gm)


def _rwkv_prep_kernel(z_ref, zp_ref, zn_ref, mu_ref, w2_ref, w0_ref, a2_ref, a0_ref, g2_ref, kk_ref, ka_ref,
                      rk_ref, gsum_ref,
                      r_ref, v_ref, nkk_ref, lw_ref, kd_ref, bd_ref, g_ref, bonus_ref, *, c_rw, lora):
    i = pl.program_id(1)
    nt = pl.num_programs(1)
    z = z_ref[...]
    tt = z.shape[0]
    rows = lax.broadcasted_iota(jnp.int32, z.shape, 0)
    prev_row = jnp.where(i == 0, 0.0, zp_ref[7:8, :])
    next_row = jnp.where(i == nt - 1, 0.0, zn_ref[0:1, :])
    zprev = jnp.where(rows == 0, prev_row, pltpu.roll(z, 1, 0))
    znext = jnp.where(rows == tt - 1, next_row, pltpu.roll(z, tt - 1, 0))
    z = z + (0.5 * (zprev + znext) - z) * mu_ref[...]

    r = z[:, 0:c_rw]
    k = z[:, c_rw:2 * c_rw]
    v = z[:, 2 * c_rw:3 * c_rw]
    o = 3 * c_rw
    wd = z[:, o:o + 2 * lora]
    ad = z[:, o + 2 * lora:o + 4 * lora]
    gd = z[:, o + 4 * lora:]

    xw = w0_ref[...] + _dot(jnp.tanh(wd).astype(BF16), w2_ref[...])
    xa = a0_ref[...] + _dot(ad.astype(BF16), a2_ref[...])
    lw = -math.exp(-0.5) * jax.nn.sigmoid(xw)
    rate = jax.nn.sigmoid(xa)
    g = _dot(jax.nn.sigmoid(gd).astype(BF16), g2_ref[...])

    gsum = gsum_ref[...]
    kkr = k * kk_ref[...]
    ka = ka_ref[...]
    kd0 = k * (1.0 + (rate[:, :c_rw] - 1.0) * ka)
    kd1 = k * (1.0 + (rate[:, c_rw:] - 1.0) * ka)
    sums = _dot_hilo_rhs(jnp.concatenate([kkr * kkr, r * (0.5 * (kd0 + kd1)) * rk_ref[...]], axis=0), gsum)
    kk = kkr * lax.rsqrt(sums[:tt] + 1e-12)
    bonus = sums[tt:] * v

    r_ref[...] = r.astype(BF16)
    v_ref[...] = v.astype(BF16)
    nkk_ref[...] = (-kk).astype(BF16)
    lw_ref[0] = lw[:, :c_rw]
    lw_ref[1] = lw[:, c_rw:]
    kd_ref[0] = kd0.astype(BF16)
    kd_ref[1] = kd1.astype(BF16)
    bd_ref[0] = (kk * rate[:, :c_rw]).astype(BF16)
    bd_ref[1] = (kk * rate[:, c_rw:]).astype(BF16)
    g_ref[...] = g
    bonus_ref[...] = bonus


def _rwkv_prep(zrw3, mu, w2bd, w0f, a2bd, a0f, g2_bf, k_k, k_a, r_kf, gsum, c_rw, lora, tt):
    b, t, cols = zrw3.shape
    nt = t // tt
    hb = tt // 8
    nhb = t // 8
    full = lambda bi, i: (0, 0)
    tile = lambda bi, i: (bi, i, 0)
    dtile = lambda bi, i: (0, bi, i, 0)
    one = jax.ShapeDtypeStruct((b, t, c_rw), F32)
    two = jax.ShapeDtypeStruct((2, b, t, c_rw), F32)
    one16 = jax.ShapeDtypeStruct((b, t, c_rw), BF16)
    two16 = jax.ShapeDtypeStruct((2, b, t, c_rw), BF16)
    return pl.pallas_call(
        functools.partial(_rwkv_prep_kernel, c_rw=c_rw, lora=lora),
        grid=(b, nt),
        in_specs=[
            pl.BlockSpec((None, tt, cols), tile),
            pl.BlockSpec((None, 8, cols), lambda bi, i: (bi, jnp.maximum(i * hb - 1, 0), 0)),
            pl.BlockSpec((None, 8, cols), lambda bi, i: (bi, jnp.minimum((i + 1) * hb, nhb - 1), 0)),
            pl.BlockSpec((1, cols), full),
            pl.BlockSpec(w2bd.shape, full),
            pl.BlockSpec((1, 2 * c_rw), full),
            pl.BlockSpec(a2bd.shape, full),
            pl.BlockSpec((1, 2 * c_rw), full),
            pl.BlockSpec(g2_bf.shape, full),
            pl.BlockSpec((1, c_rw), full),
            pl.BlockSpec((1, c_rw), full),
            pl.BlockSpec((1, c_rw), full),
            pl.BlockSpec((c_rw, c_rw), full),
        ],
        out_specs=[
            pl.BlockSpec((None, tt, c_rw), tile),
            pl.BlockSpec((None, tt, c_rw), tile),
            pl.BlockSpec((None, tt, c_rw), tile),
            pl.BlockSpec((2, None, tt, c_rw), dtile),
            pl.BlockSpec((2, None, tt, c_rw), dtile),
            pl.BlockSpec((2, None, tt, c_rw), dtile),
            pl.BlockSpec((None, tt, c_rw), tile),
            pl.BlockSpec((None, tt, c_rw), tile),
        ],
        out_shape=[one16, one16, one16, two, two16, two16, one, one],
        compiler_params=_cparams(("parallel", "parallel")),
        name="rwkv_prep",
    )(zrw3, zrw3, zrw3, mu, w2bd, w0f, a2bd, a0f, g2_bf, k_k, k_a, r_kf, gsum)


M_STRICT, M_INCL, M_EYE, M_PAIR, M_LEVEL = 0, 1, 2, 3, 4
N_LEVELS = 5
SCAN_SEQS = 2


def _scan_tables(w):
    c = CHUNK
    masks = []
    mcums = []
    for d in range(2):
        t = np.broadcast_to(np.arange(c)[:, None], (c, w))
        s = np.broadcast_to((np.arange(w) % RW_HEAD)[None, :], (c, w))
        tt = np.broadcast_to(np.arange(c)[:, None], (c, c))
        ss = np.broadcast_to(np.arange(c)[None, :], (c, c))
        if d == 1:
            t, s, tt, ss = c - 1 - t, c - 1 - s, c - 1 - tt, c - 1 - ss
        rows = [s < t, s <= t, s == t, (t // 2 == s // 2) & (t > s)]
        sz = 2
        while sz < c:
            rows.append((t // (2 * sz) == s // (2 * sz)) & ((t // sz) % 2 == 1) & ((s // sz) % 2 == 0))
            sz *= 2
        assert len(rows) == M_LEVEL + N_LEVELS
        masks.append(np.stack(rows).astype(np.float32))
        mcums.append((ss <= tt).astype(np.float32))
    return jnp.asarray(np.stack(masks), F32), jnp.asarray(np.stack(mcums), BF16)


def _block_diag(y, bd16):
    g = y.shape[1] // RW_HEAD
    return jnp.concatenate([y.astype(BF16)] * g, axis=0) * bd16


def _bmm(x, y, bd16):
    outs = []
    for s in range(0, y.shape[1], HEAD_GROUP):
        outs.append(_dot(x[:, s:s + HEAD_GROUP].astype(BF16), _block_diag(y[:, s:s + HEAD_GROUP], bd16)))
    return jnp.concatenate(outs, axis=1)


def _bmm_nt(x, y, bd16):
    outs = []
    for s in range(0, y.shape[1], HEAD_GROUP):
        bd = _block_diag(y[:, s:s + HEAD_GROUP], bd16)
        outs.append(lax.dot_general(x[:, s:s + HEAD_GROUP].astype(BF16), bd, (((1,), (1,)), ((), ())),
                                    preferred_element_type=F32))
    return jnp.concatenate(outs, axis=1)


def _bmm_tn(x, y, bd32):
    outs = []
    for s in range(0, y.shape[1], HEAD_GROUP):
        xt = x[:, s:s + HEAD_GROUP].T.astype(BF16)
        full = _dot(xt, y[:, s:s + HEAD_GROUP].astype(BF16)) * bd32
        o = full[0:RW_HEAD]
        for j in range(1, HEAD_GROUP // RW_HEAD):
            o = o + full[j * RW_HEAD:(j + 1) * RW_HEAD]
        outs.append(o)
    return jnp.concatenate(outs, axis=1)


def _scan_chunks(r, lw, k, v, a, b, state, masks, mcum, bd16, bd32):
    c = r[0].shape[0]

    def each(f, *cols):
        return [f(*xs) for xs in zip(*cols)]

    cum = each(_dot_hilo_lhs, mcum, lw)
    tot = each(lambda x: jnp.sum(x, axis=0, keepdims=True), lw)
    e_neg = each(lambda x: jnp.exp(-x), cum)
    e_tail = each(lambda t_, x: jnp.exp(t_ - x), tot, cum)
    rt = each(lambda x, cm: x * jnp.exp(cm), r, cum)
    at = each(lambda x, cm, l_: x * jnp.exp(cm - l_), a, cum, lw)
    bt = each(jnp.multiply, b, e_neg)
    kt = each(jnp.multiply, k, e_neg)
    bh = each(jnp.multiply, b, e_tail)
    kh = each(jnp.multiply, k, e_tail)

    ar = each(lambda x, y_: jnp.concatenate([x, y_], axis=0), at, rt)
    pb = each(lambda x, y_: _bmm_nt(x, y_, bd16), ar, bt)
    pk = each(lambda x, y_: _bmm_nt(x, y_, bd16), ar, kt)
    a_ab = each(lambda x, m: x[:c] * m[M_STRICT], pb, masks)
    a_rb = each(lambda x, m: x[c:] * m[M_INCL], pb, masks)
    a_ak = each(lambda x, m: x[:c] * m[M_STRICT], pk, masks)
    a_rk = each(lambda x, m: x[c:] * m[M_INCL], pk, masks)

    tinv = each(lambda x, m: m[M_EYE] + x * m[M_PAIR], a_ab, masks)
    for lvl in range(N_LEVELS):
        x = each(lambda aa, m, ti: _bmm(aa * m[M_LEVEL + lvl], ti, bd16), a_ab, masks, tinv)
        tinv = each(lambda ti, x_: ti + _bmm(ti, x_, bd16), tinv, x)

    bmm = lambda p, q: _bmm(p, q, bd16)
    stack = lambda p, q: jnp.concatenate([p, q], axis=0)
    att = each(bmm, tinv, at)
    akv = each(lambda p, p2, q: _bmm(stack(p, p2), q, bd16), a_ak, a_rk, v)
    vt = each(lambda ti, x_: _bmm(ti, x_[:c], bd16), tinv, akv)
    rh = each(lambda x, p, q: x + _bmm(p, q, bd16), rt, a_rb, att)
    y_in = each(lambda p, q, x_: _bmm(p, q, bd16) + x_[c:], a_rb, vt, akv)
    mc = each(lambda m, t_, p, q: m[M_EYE] * jnp.exp(t_) + _bmm_tn(p, q, bd32), masks, tot, bh, att)
    nc = each(lambda p, p2, q, q2: _bmm_tn(stack(p, p2), stack(q, q2), bd32), bh, kh, vt, v)
    onst = each(lambda p, p2, q: _bmm(stack(p, p2), q, bd16), rh, mc, state)
    y = each(lambda x_, yi: x_[:c] + yi, onst, y_in)
    new_state = each(lambda x_, n_: x_[c:] + n_, onst, nc)
    return y, new_state


def _rwkv_scan_kernel(rf_ref, vf_ref, af_ref, rb_ref, vb_ref, ab_ref, lwf_ref, kf_ref, bf_ref, lwb_ref, kb_ref,
                      bb_ref, mask_ref, mcum_ref, bd16_ref, bd32_ref, yf_ref, yb_ref, state_ref, *, n_chunks):
    @pl.when(pl.program_id(1) == 0)
    def _():
        state_ref[...] = jnp.zeros_like(state_ref)

    dirs = ((0, rf_ref, lwf_ref, kf_ref, vf_ref, af_ref, bf_ref, yf_ref),
            (1, rb_ref, lwb_ref, kb_ref, vb_ref, ab_ref, bb_ref, yb_ref))
    n_seq = rf_ref.shape[0]

    def body(j, carry):
        cols = [[] for _ in range(9)]
        where = []
        for d, r_ref, lw_ref, k_ref, v_ref, a_ref, b_ref, y_ref in dirs:
            cj = j if d == 0 else n_chunks - 1 - j
            sl = pl.ds(pl.multiple_of(cj * CHUNK, CHUNK), CHUNK)
            for g in range(n_seq):
                vals = (r_ref[g, sl, :], lw_ref[g, sl, :], k_ref[g, sl, :], v_ref[g, sl, :], a_ref[g, sl, :],
                        b_ref[g, sl, :], state_ref[d, g], mask_ref.at[d], mcum_ref[d])
                for col, val in zip(cols, vals):
                    col.append(val)
                where.append((y_ref, d, g, sl))
        ys, sts = _scan_chunks(*cols, bd16_ref[...], bd32_ref[...])
        for (y_ref, d, g, sl), y, st in zip(where, ys, sts):
            y_ref[g, sl, :] = y
            state_ref[d, g] = st
        return carry

    lax.fori_loop(0, n_chunks, body, 0)


def _rwkv_scan(r, v, nkk, lw, kd, bd, tb):
    b, t, w = r.shape
    nb = t // tb
    g = SCAN_SEQS if b % SCAN_SEQS == 0 else 1
    masks, mcum = _scan_tables(w)
    bd32 = _group_matrix(HEAD_GROUP, RW_HEAD, 1.0).astype(F32)
    bd16 = bd32.astype(BF16)
    fwd = lambda bi, i: (bi, i, 0)
    bwd = lambda bi, i: (bi, nb - 1 - i, 0)
    fwd_d = lambda bi, i: (0, bi, i, 0)
    bwd_d = lambda bi, i: (1, bi, nb - 1 - i, 0)
    const = lambda nd: (lambda bi, i: (0,) * nd)
    blk = (g, tb, w)
    dblk = (None, g, tb, w)
    out = jax.ShapeDtypeStruct((b, t, w), F32)
    return pl.pallas_call(
        functools.partial(_rwkv_scan_kernel, n_chunks=tb // CHUNK),
        grid=(b // g, nb),
        in_specs=[
            pl.BlockSpec(blk, fwd), pl.BlockSpec(blk, fwd), pl.BlockSpec(blk, fwd),
            pl.BlockSpec(blk, bwd), pl.BlockSpec(blk, bwd), pl.BlockSpec(blk, bwd),
            pl.BlockSpec(dblk, fwd_d), pl.BlockSpec(dblk, fwd_d), pl.BlockSpec(dblk, fwd_d),
            pl.BlockSpec(dblk, bwd_d), pl.BlockSpec(dblk, bwd_d), pl.BlockSpec(dblk, bwd_d),
            pl.BlockSpec(masks.shape, const(4)),
            pl.BlockSpec(mcum.shape, const(3)),
            pl.BlockSpec(bd16.shape, const(2)),
            pl.BlockSpec(bd32.shape, const(2)),
        ],
        out_specs=[pl.BlockSpec(blk, fwd), pl.BlockSpec(blk, bwd)],
        out_shape=[out, out],
        scratch_shapes=[pltpu.VMEM((2, g, CHUNK, w), F32)],
        compiler_params=_cparams(("parallel", "arbitrary")),
        name="rwkv_scan",
    )(r, v, nkk, r, v, nkk, lw, kd, bd, lw, kd, bd, masks, mcum, bd16, bd32)


def _attn_kernel(slope_ref, q_ref, k_ref, v_ref, lq1_ref, lk1_ref, lq2_ref, lk2_ref, sg_ref, o_ref,
                 m_ref, l_ref, acc_ref, *, qk_dim, lambda_init):
    h = pl.program_id(1)
    i = pl.program_id(2)
    j = pl.program_id(3)
    nk = pl.num_programs(3)
    tq = q_ref.shape[0]
    tk = k_ref.shape[0]

    @pl.when(j == 0)
    def _():
        m_ref[...] = jnp.full(m_ref.shape, -jnp.inf, F32)
        l_ref[...] = jnp.zeros_like(l_ref)
        acc_ref[...] = jnp.zeros_like(acc_ref)

    q = q_ref[...]
    k = k_ref[...]
    v = v_ref[...]
    lane = lax.broadcasted_iota(jnp.int32, q.shape, 1)
    zero = jnp.zeros_like(q)
    qi = lax.broadcasted_iota(jnp.int32, (tq, tk), 0) + i * tq
    kj = lax.broadcasted_iota(jnp.int32, (tq, tk), 1) + j * tk
    bias = slope_ref[h] * jnp.abs(qi - kj).astype(F32)
    for c in range(2):
        qc = jnp.where((lane < qk_dim) == (c == 0), q, zero)
        s = lax.dot_general(qc, k, (((1,), (1,)), ((), ())), preferred_element_type=F32) - bias
        m_old = m_ref[c]
        m_new = jnp.maximum(m_old, jnp.max(s, axis=-1, keepdims=True))
        alpha = jnp.exp(m_old - m_new)
        p = jnp.exp(s - m_new)
        l_ref[c] = alpha * l_ref[c] + jnp.sum(p, axis=-1, keepdims=True)
        acc_ref[c] = alpha * acc_ref[c] + _dot(p.astype(BF16), v)
        m_ref[c] = m_new

    @pl.when(j == nk - 1)
    def _():
        lam = (jnp.exp(jnp.sum(lq1_ref[...] * lk1_ref[...], axis=-1, keepdims=True))
               - jnp.exp(jnp.sum(lq2_ref[...] * lk2_ref[...], axis=-1, keepdims=True)) + lambda_init)
        o = acc_ref[0] / l_ref[0] - lam * (acc_ref[1] / l_ref[1])
        ms = jnp.mean(o * o, axis=-1, keepdims=True)
        o_ref[...] = o * lax.rsqrt(ms + NORM_EPS) * (sg_ref[...] * (1.0 - lambda_init))


def _attention(slopes, q3, k3, v3, lq1, lk1, lq2, lk2, subln_g, heads, v_dim, lambda_init, tq, tk):
    b, t, _ = q3.shape
    small = lambda s, bi, h, i, j: (0, 0)
    grid_spec = pltpu.PrefetchScalarGridSpec(
        num_scalar_prefetch=1,
        grid=(b, heads, t // tq, t // tk),
        in_specs=[
            pl.BlockSpec((None, tq, v_dim), lambda bi, h, i, j, s: (bi, i, h)),
            pl.BlockSpec((None, tk, v_dim), lambda bi, h, i, j, s: (bi, j, h)),
            pl.BlockSpec((None, tk, v_dim), lambda bi, h, i, j, s: (bi, j, h)),
            pl.BlockSpec(lq1.shape, lambda bi, h, i, j, s: (0, 0)),
            pl.BlockSpec(lk1.shape, lambda bi, h, i, j, s: (0, 0)),
            pl.BlockSpec(lq2.shape, lambda bi, h, i, j, s: (0, 0)),
            pl.BlockSpec(lk2.shape, lambda bi, h, i, j, s: (0, 0)),
            pl.BlockSpec(subln_g.shape, lambda bi, h, i, j, s: (0, 0)),
        ],
        out_specs=pl.BlockSpec((None, tq, v_dim), lambda bi, h, i, j, s: (bi, i, h)),
        scratch_shapes=[
            pltpu.VMEM((2, tq, 1), F32),
            pltpu.VMEM((2, tq, 1), F32),
            pltpu.VMEM((2, tq, v_dim), F32),
        ],
    )
    del small
    return pl.pallas_call(
        functools.partial(_attn_kernel, qk_dim=v_dim // 2, lambda_init=lambda_init),
        grid_spec=grid_spec,
        out_shape=jax.ShapeDtypeStruct((b, t, heads * v_dim), F32),
        compiler_params=_cparams(("parallel", "parallel", "parallel", "arbitrary")),
        name="attn",
    )(slopes, q3, k3, v3, lq1, lk1, lq2, lk2, subln_g)


EXP_ZERO_ARG = 104.0
MAX_FIXED_SHIFT = 40.0
N_AUG = 6
ATTN_BLOCK = 512
ATTN_KEY_SPLIT = 2
ATTN_V_PAD = 16


def _attn_fixed_kernel(w_ref, slope_ref, mb_ref, q_ref, k_ref, vt_ref, dbias_ref, qaug_ref, kaug_ref,
                       lq1_ref, lk1_ref, lq2_ref, lk2_ref, sg_ref, o_ref, acc_ref, *, qk_dim, lambda_init):
    h = pl.program_id(1)
    i = pl.program_id(2)
    tb = q_ref.shape[0]
    nk = k_ref.shape[0] // tb
    v_dim = 2 * qk_dim
    slope = slope_ref[h]
    mb = mb_ref[0]
    w = w_ref[h]
    acc_ref[...] = jnp.zeros_like(acc_ref)

    q = q_ref[...]
    lane = lax.broadcasted_iota(jnp.int32, q.shape, 1)
    lane_row = lax.broadcasted_iota(jnp.int32, (1, q.shape[1]), 1)
    is_c0 = lane < qk_dim

    def block(j, qvar, kvar, delta, diag):
        kblk = k_ref[pl.ds(pl.multiple_of(j * tb, tb), tb), :]
        vtblk = vt_ref[j]
        sub = tb // ATTN_KEY_SPLIT
        chains = []
        for c in range(2):
            base = qk_dim * (1 - c)
            data = is_c0 if c == 0 else jnp.logical_not(is_c0)
            dyn = jnp.where(lane_row == base + 4, -mb,
                            jnp.where(lane_row == base + 5, -(slope * tb) * delta, 0.0)).astype(BF16)
            qa = jnp.where(data, q, qaug_ref[qvar, c])
            ka = jnp.where(data, kblk, kaug_ref[kvar, c] + dyn)
            for u in range(ATTN_KEY_SPLIT):
                chains.append((c, u, qa, ka[u * sub:(u + 1) * sub]))
        sts = [lax.dot_general(ka, qa, (((1,), (1,)), ((), ())), preferred_element_type=F32)
               for _, _, qa, ka in chains]
        if diag:
            sts = [st - dbias_ref[u * sub:(u + 1) * sub, :] for st, (_, u, _, _) in zip(sts, chains)]
        pts = [jnp.exp(st).astype(BF16) for st in sts]
        for c in range(2):
            upd = None
            for pt, (cc, u, _, _) in zip(pts, chains):
                if cc == c:
                    part = _dot(vtblk[:, u * sub:(u + 1) * sub], pt)
                    upd = part if upd is None else upd + part
            acc_ref[c] += upd

    def below(j, carry):
        block(j, 0, 0, (i - j).astype(F32), False)
        return carry

    def above(j, carry):
        block(j, 1, 1, (j - i).astype(F32), False)
        return carry

    lax.fori_loop(jnp.maximum(i - w, 0), i, below, 0)
    block(i, 2, 0, jnp.zeros((), F32), True)
    lax.fori_loop(i + 1, jnp.minimum(i + w, nk - 1) + 1, above, 0)

    lam = (jnp.exp(jnp.sum(lq1_ref[...] * lk1_ref[...], axis=-1, keepdims=True))
           - jnp.exp(jnp.sum(lq2_ref[...] * lk2_ref[...], axis=-1, keepdims=True)) + lambda_init)
    a0 = acc_ref[0]
    a1 = acc_ref[1]
    ot = a0[:v_dim] / a0[v_dim:v_dim + 1] - lam * (a1[:v_dim] / a1[v_dim:v_dim + 1])
    ms = jnp.mean(ot * ot, axis=0, keepdims=True)
    o_ref[...] = (ot * lax.rsqrt(ms + NORM_EPS) * (sg_ref[...] * (1.0 - lambda_init))).T


def _attn_aug_tables(slopes, tb, qk_dim):
    heads = len(slopes)
    lanes = 2 * qk_dim
    pos = np.arange(tb)
    qaug = np.zeros((heads, 3, 2, tb, lanes), np.float32)
    kaug = np.zeros((heads, 2, 2, tb, lanes), np.float32)
    for hd, sl in enumerate(slopes):
        for var in range(3):
            rr = pos if var != 1 else tb - 1 - pos
            hi = (rr // 16) * (16.0 * sl)
            lo = (rr % 16) * sl
            for c in range(2):
                base = qk_dim * (1 - c)
                if var == 2:
                    qaug[hd, var, c, :, base + 4] = 1.0
                    continue
                qaug[hd, var, c, :, base + 0] = -hi
                qaug[hd, var, c, :, base + 1] = -lo
                qaug[hd, var, c, :, base + 2:base + N_AUG] = 1.0
                kaug[hd, var, c, :, base + 0:base + 2] = 1.0
                kaug[hd, var, c, :, base + 2] = hi
                kaug[hd, var, c, :, base + 3] = lo
    return jnp.asarray(qaug, BF16), jnp.asarray(kaug, BF16)


def _attention_fixed(mb, q3, k3, vt, lq1, lk1, lq2, lk2, subln_g, heads, v_dim, lambda_init, tb):
    b, t, _ = q3.shape
    qk_dim = v_dim // 2
    slopes = [2.0 ** (-8.0 * (hd + 1) / heads) for hd in range(heads)]
    widths = [int(math.ceil((EXP_ZERO_ARG / s - 1.0) / tb)) for s in slopes]
    qaug, kaug = _attn_aug_tables(slopes, tb, qk_dim)
    pos = np.arange(tb)
    dist = np.abs(pos[:, None] - pos[None, :]).astype(np.float32)
    dbias = jnp.asarray(np.stack([dist * s for s in slopes]), F32)
    nb = t // tb
    vrows = v_dim + ATTN_V_PAD
    assert vt.shape == (b, heads, nb, vrows, tb)
    const = lambda bi, h, i, *_: (0, 0)
    grid_spec = pltpu.PrefetchScalarGridSpec(
        num_scalar_prefetch=3,
        grid=(b, heads, t // tb),
        in_specs=[
            pl.BlockSpec((None, tb, v_dim), lambda bi, h, i, *_: (bi, i, h)),
            pl.BlockSpec((None, t, v_dim), lambda bi, h, i, *_: (bi, 0, h)),
            pl.BlockSpec((None, None, nb, vrows, tb), lambda bi, h, i, *_: (bi, h, 0, 0, 0)),
            pl.BlockSpec((None, tb, tb), lambda bi, h, i, *_: (h, 0, 0)),
            pl.BlockSpec((None, 3, 2, tb, v_dim), lambda bi, h, i, *_: (h, 0, 0, 0, 0)),
            pl.BlockSpec((None, 2, 2, tb, v_dim), lambda bi, h, i, *_: (h, 0, 0, 0, 0)),
            pl.BlockSpec(lq1.shape, const),
            pl.BlockSpec(lk1.shape, const),
            pl.BlockSpec(lq2.shape, const),
            pl.BlockSpec(lk2.shape, const),
            pl.BlockSpec((v_dim, 1), const),
        ],
        out_specs=pl.BlockSpec((None, tb, v_dim), lambda bi, h, i, *_: (bi, i, h)),
        scratch_shapes=[pltpu.VMEM((2, vrows, tb), F32)],
    )
    return pl.pallas_call(
        functools.partial(_attn_fixed_kernel, qk_dim=qk_dim, lambda_init=lambda_init),
        grid_spec=grid_spec,
        out_shape=jax.ShapeDtypeStruct((b, t, heads * v_dim), F32),
        compiler_params=_cparams(("parallel", "parallel", "parallel")),
        name="attn_fixed",
    )(jnp.asarray(widths, jnp.int32), jnp.asarray(slopes, F32), mb.reshape(1).astype(F32),
      q3, k3, vt, dbias, qaug, kaug, lq1, lk1, lq2, lk2, subln_g.reshape(v_dim, 1))


def _post_kernel(yf_ref, yb_ref, bonus_ref, g_ref, oda_ref, x_ref, lng_ref, lnb_ref, gmean_ref, wo_rw_ref, wo_da_ref,
                 gffn_ref, wr_ref, br_ref, h_ref, m_ref, idx_ref, gate_ref, *, gn_eps):
    gmean = gmean_ref[...]
    y = yf_ref[...] + yb_ref[...]
    mean = _dot_hilo_rhs(y, gmean)
    yc = y - mean
    var = _dot_hilo_rhs(yc * yc, gmean)
    yn = yc * lax.rsqrt(var + gn_eps) * lng_ref[...] + lnb_ref[...]
    o_rw = (yn + bonus_ref[...]) * g_ref[...]
    h = x_ref[...] + _dot(o_rw.astype(BF16), wo_rw_ref[...]) + _dot(oda_ref[...].astype(BF16), wo_da_ref[...])
    h_ref[...] = h
    ms = jnp.mean(h * h, axis=-1, keepdims=True)
    m = h * lax.rsqrt(ms + NORM_EPS) * gffn_ref[...]
    _store_rows(m_ref, m)

    logits = _dot_hilo(m, wr_ref[...]) + br_ref[...]
    ne = logits.shape[1]
    lane = lax.broadcasted_iota(jnp.int32, logits.shape, 1)
    kcol = lax.broadcasted_iota(jnp.int32, (logits.shape[0], TOP_K), 1)
    idx_all = jnp.zeros((logits.shape[0], TOP_K), jnp.int32)
    val_all = jnp.zeros((logits.shape[0], TOP_K), F32)
    work = logits
    for kk in range(TOP_K):
        mx = jnp.max(work, axis=-1, keepdims=True)
        sel = jnp.min(jnp.where(work == mx, lane, ne), axis=-1, keepdims=True)
        idx_all = jnp.where(kcol == kk, sel, idx_all)
        val_all = jnp.where(kcol == kk, mx, val_all)
        work = jnp.where(lane == sel, -jnp.inf, work)
    e = jnp.exp(val_all - jnp.max(val_all, axis=-1, keepdims=True))
    idx_ref[...] = idx_all
    gate_ref[...] = e / jnp.sum(e, axis=-1, keepdims=True)


def _post(yf, yb, bonus, g, oda, x2, lng, lnb, gmean, wo_rw, wo_da, gffn, wr, br, gn_eps, tm):
    n, c = yf.shape
    d = x2.shape[1]
    ne = wr.shape[1]
    full = lambda i: (0, 0)
    row = lambda i: (i, 0)
    return pl.pallas_call(
        functools.partial(_post_kernel, gn_eps=gn_eps),
        grid=(n // tm,),
        in_specs=[
            pl.BlockSpec((tm, c), row),
            pl.BlockSpec((tm, c), row),
            pl.BlockSpec((tm, c), row),
            pl.BlockSpec((tm, c), row),
            pl.BlockSpec((tm, c), row),
            pl.BlockSpec((tm, d), row),
            pl.BlockSpec((1, c), full),
            pl.BlockSpec((1, c), full),
            pl.BlockSpec((c, c), full),
            pl.BlockSpec((c, d), full),
            pl.BlockSpec((c, d), full),
            pl.BlockSpec((1, d), full),
            pl.BlockSpec((d, ne), full),
            pl.BlockSpec((1, ne), full),
        ],
        out_specs=[
            pl.BlockSpec((tm, d), row),
            pl.BlockSpec((tm * d // V7X_LANES, V7X_LANES), row),
            pl.BlockSpec((tm, TOP_K), row),
            pl.BlockSpec((tm, TOP_K), row),
        ],
        out_shape=[
            jax.ShapeDtypeStruct((n, d), F32),
            jax.ShapeDtypeStruct((n * d // V7X_LANES, V7X_LANES), F32),
            jax.ShapeDtypeStruct((n, TOP_K), jnp.int32),
            jax.ShapeDtypeStruct((n, TOP_K), F32),
        ],
        compiler_params=_cparams(("parallel",)),
        name="post",
    )(yf, yb, bonus, g, oda, x2, lng, lnb, gmean, wo_rw, wo_da, gffn, wr, br)


def _rank_kernel(idx_ref, ltri_ref, rank_ref, count_ref, carry_ref):
    @pl.when(pl.program_id(0) == 0)
    def _():
        carry_ref[...] = jnp.zeros_like(carry_ref)

    idx = idx_ref[...]
    tr = idx.shape[0]
    lane = lax.broadcasted_iota(jnp.int32, (tr, N_EXPERTS), 1)
    hots = [jnp.where(lane == idx[:, kk:kk + 1], 1.0, 0.0) for kk in range(TOP_K)]
    mask = hots[0]
    for kk in range(1, TOP_K):
        mask = mask + hots[kk]
    before = _dot(ltri_ref[...], mask.astype(BF16)) + carry_ref[...]
    kcol = lax.broadcasted_iota(jnp.int32, (tr, TOP_K), 1)
    rank = jnp.zeros((tr, TOP_K), F32)
    for kk in range(TOP_K):
        rk = jnp.sum(hots[kk] * before, axis=-1, keepdims=True)
        rank = jnp.where(kcol == kk, rk, rank)
    rank_ref[...] = rank.astype(jnp.int32)
    carry_ref[...] = carry_ref[...] + jnp.sum(mask, axis=0, keepdims=True)
    count_ref[...] = carry_ref[...].astype(jnp.int32)


def _moe_rank(idx, tr):
    n = idx.shape[0]
    ltri = jnp.asarray(np.tril(np.ones((tr, tr), np.float32), -1), BF16)
    return pl.pallas_call(
        _rank_kernel,
        grid=(n // tr,),
        in_specs=[pl.BlockSpec((tr, TOP_K), lambda i: (i, 0)), pl.BlockSpec((tr, tr), lambda i: (0, 0))],
        out_specs=[pl.BlockSpec((tr, TOP_K), lambda i: (i, 0)), pl.BlockSpec((1, N_EXPERTS), lambda i: (0, 0))],
        out_shape=[jax.ShapeDtypeStruct((n, TOP_K), jnp.int32), jax.ShapeDtypeStruct((1, N_EXPERTS), jnp.int32)],
        scratch_shapes=[pltpu.VMEM((1, N_EXPERTS), F32)],
        compiler_params=_cparams(("arbitrary",)),
        name="moe_rank",
    )(idx, ltri)


ROW_UNROLL = 8


def _row(r, c):
    return pl.ds(pl.multiple_of(r * c, c), c)


def _dispatch_kernel(offs_ref, idx_ref, rank_ref, m_ref, xs_ref, slot_ref, sem, *, c):
    td = m_ref.shape[0] // c

    def issue(g, carry):
        for u in range(ROW_UNROLL):
            t = g * ROW_UNROLL + u
            for kk in range(TOP_K):
                pos = t * TOP_K + kk
                dest = offs_ref[idx_ref[pos]] + rank_ref[pos]
                slot_ref[pos] = dest
                pltpu.make_async_copy(m_ref.at[_row(t, c), :], xs_ref.at[_row(dest, c), :], sem).start(
                    priority=kk % 2)
        return carry

    lax.fori_loop(0, td // ROW_UNROLL, issue, 0)
    for kk in range(TOP_K):
        pltpu.make_async_copy(m_ref, xs_ref.at[pl.ds(0, td * c), :], sem).wait()


def _dispatch(offs, idx_flat, rank_flat, mrows, c, p_rows, td):
    n = mrows.shape[0] // c
    assert td % ROW_UNROLL == 0
    flat = pl.BlockSpec((td * TOP_K,), lambda i, o: (i,), memory_space=pltpu.SMEM)
    grid_spec = pltpu.PrefetchScalarGridSpec(
        num_scalar_prefetch=1,
        grid=(n // td,),
        in_specs=[flat, flat, pl.BlockSpec((td * c, V7X_LANES), lambda i, o: (i, 0))],
        out_specs=[pl.BlockSpec(memory_space=pl.ANY), flat],
        scratch_shapes=[pltpu.SemaphoreType.DMA(())],
    )
    return pl.pallas_call(
        functools.partial(_dispatch_kernel, c=c),
        grid_spec=grid_spec,
        out_shape=[jax.ShapeDtypeStruct((p_rows * c, V7X_LANES), F32),
                   jax.ShapeDtypeStruct((n * TOP_K,), jnp.int32)],
        compiler_params=_cparams(("arbitrary",), has_side_effects=True),
        name="moe_dispatch",
    )(offs, idx_flat, rank_flat, mrows)


def _ffn_kernel(be_ref, nu_ref, x_ref, w1_ref, b1_ref, w2_ref, b2_ref, y_ref, *, d_ff):
    @pl.when(pl.program_id(0) < nu_ref[0])
    def _():
        x = _load_rows(x_ref, w1_ref.shape[0] // V7X_LANES).astype(BF16)
        hcat = _dot(x, w1_ref[...]) + b1_ref[...]
        glu = jnp.minimum(hcat[:, :d_ff], SWIGLU_LIMIT)
        lin = jnp.clip(hcat[:, d_ff:], -SWIGLU_LIMIT, SWIGLU_LIMIT)
        act = glu * jax.nn.sigmoid(SWIGLU_ALPHA * glu) * (lin + 1.0)
        _store_rows(y_ref, _dot(act.astype(BF16), w2_ref[...]) + b2_ref[...])


def _moe_ffn(blk_e, n_used, xrows, w1_bf, b1, w2_bf, b2, bm):
    ne, d, f2 = w1_bf.shape
    d_ff = f2 // 2
    c = d // V7X_LANES
    p_rows = xrows.shape[0] // c
    blk = (bm * c, V7X_LANES)
    grid_spec = pltpu.PrefetchScalarGridSpec(
        num_scalar_prefetch=2,
        grid=(p_rows // bm,),
        in_specs=[
            pl.BlockSpec(blk, lambda i, be, nu: (jnp.minimum(i, nu[0] - 1), 0)),
            pl.BlockSpec((None, d, f2), lambda i, be, nu: (be[i], 0, 0)),
            pl.BlockSpec((None, 1, f2), lambda i, be, nu: (be[i], 0, 0)),
            pl.BlockSpec((None, d_ff, d), lambda i, be, nu: (be[i], 0, 0)),
            pl.BlockSpec((None, 1, d), lambda i, be, nu: (be[i], 0, 0)),
        ],
        out_specs=pl.BlockSpec(blk, lambda i, be, nu: (i, 0)),
    )
    return pl.pallas_call(
        functools.partial(_ffn_kernel, d_ff=d_ff),
        grid_spec=grid_spec,
        out_shape=jax.ShapeDtypeStruct(xrows.shape, F32),
        compiler_params=_cparams(("arbitrary",)),
        name="moe_ffn",
    )(blk_e, n_used, xrows, w1_bf, b1.reshape(ne, 1, f2), w2_bf, b2.reshape(ne, 1, d))


def _combine_kernel(slot_ref, next_slot_ref, h_ref, gate_ref, y_ref, o_ref, buf_ref, sems):
    i = pl.program_id(0)
    tc, d = h_ref.shape
    c = d // V7X_LANES

    def start_gather(slots, half):
        def issue(g, carry):
            for u in range(ROW_UNROLL):
                t = g * ROW_UNROLL + u
                for kk in range(TOP_K):
                    src = slots[t * TOP_K + kk]
                    pltpu.make_async_copy(y_ref.at[_row(src, c), :], buf_ref.at[half, kk, _row(t, c), :],
                                          sems.at[half]).start(priority=kk % 2)
            return carry

        lax.fori_loop(0, tc // ROW_UNROLL, issue, 0)

    cur = i % 2

    @pl.when(i == 0)
    def _():
        start_gather(slot_ref, 0)

    @pl.when(i + 1 < pl.num_programs(0))
    def _():
        start_gather(next_slot_ref, 1 - cur)

    for kk in range(TOP_K):
        pltpu.make_async_copy(y_ref.at[pl.ds(0, tc * c), :], buf_ref.at[cur, kk], sems.at[cur]).wait()

    gate = gate_ref[...]
    for j in range(c):
        cols = slice(j * V7X_LANES, (j + 1) * V7X_LANES)
        acc = h_ref[:, cols]
        for kk in range(TOP_K):
            acc = acc + gate[:, kk:kk + 1] * buf_ref[cur, kk, pl.ds(j, tc, stride=c), :]
        o_ref[:, cols] = acc


def _combine(slot_flat, h, gates, yrows, tc):
    n, d = h.shape
    assert tc % ROW_UNROLL == 0
    last = n // tc - 1
    return pl.pallas_call(
        _combine_kernel,
        grid=(n // tc,),
        in_specs=[
            pl.BlockSpec((tc * TOP_K,), lambda i: (i,), memory_space=pltpu.SMEM),
            pl.BlockSpec((tc * TOP_K,), lambda i: (jnp.minimum(i + 1, last),), memory_space=pltpu.SMEM),
            pl.BlockSpec((tc, d), lambda i: (i, 0)),
            pl.BlockSpec((tc, TOP_K), lambda i: (i, 0)),
            pl.BlockSpec(memory_space=pl.ANY),
        ],
        out_specs=pl.BlockSpec((tc, d), lambda i: (i, 0)),
        scratch_shapes=[pltpu.VMEM((2, TOP_K, tc * d // V7X_LANES, V7X_LANES), F32),
                        pltpu.SemaphoreType.DMA((2,))],
        out_shape=jax.ShapeDtypeStruct((n, d), F32),
        compiler_params=_cparams(("arbitrary",)),
        name="moe_combine",
    )(slot_flat, slot_flat, h, gates, yrows)


def _ple_kernel(h_ref, p_ref, wp_ref, gp_ref, wg_ref, o_ref):
    h = h_ref[...]
    e = _dot(p_ref[...].astype(BF16), wp_ref[...])
    ms = jnp.mean(e * e, axis=-1, keepdims=True)
    e = e * lax.rsqrt(ms + NORM_EPS) * gp_ref[...]
    gate = jax.nn.sigmoid(_dot(h.astype(BF16), wg_ref[...]))
    o_ref[...] = h + e * gate


def _ple(h, p2, wp_bf, gp, wg_bf, tm):
    n, d = h.shape
    pd = p2.shape[1]
    full = lambda i: (0, 0)
    row = lambda i: (i, 0)
    return pl.pallas_call(
        _ple_kernel,
        grid=(n // tm,),
        in_specs=[
            pl.BlockSpec((tm, d), row),
            pl.BlockSpec((tm, pd), row),
            pl.BlockSpec((pd, d), full),
            pl.BlockSpec((1, d), full),
            pl.BlockSpec((d, d), full),
        ],
        out_specs=pl.BlockSpec((tm, d), row),
        out_shape=jax.ShapeDtypeStruct((n, d), F32),
        compiler_params=_cparams(("parallel",)),
        name="ple",
    )(h, p2, wp_bf, gp, wg_bf)


def _tile(n, want):
    t = min(n, want)
    assert n % t == 0, (n, t)
    return t


def _layer(h3, p3, lambda_init, norm_mix_g, w_in, rw_mu, rw_w0, rw_w2, rw_a0, rw_a2, rw_g2, rw_k_k, rw_k_a, rw_r_k,
           rw_ln_g, rw_ln_b, da_q_norm_g, da_k_norm_g, da_lq1, da_lk1, da_lq2, da_lk2, da_subln_g, w_out,
           norm_ffn_g, w_router, b_router, w1, b1, w2, b2, w_ple, ple_norm_g, w_ple_gate):
    b, t, d = h3.shape
    n = b * t
    c_rw = rw_k_k.shape[0]
    lora = rw_w2.shape[1]
    rw_cols = rw_mu.shape[0]
    rw_heads = rw_r_k.shape[0]
    c_da = w_in.shape[1] - rw_cols
    assert c_da % 3 == 0
    c_da //= 3
    qk_dim = da_q_norm_g.shape[0]
    v_dim = da_subln_g.shape[0]
    da_heads = c_da // v_dim
    assert rw_heads * RW_HEAD == c_rw and c_rw % HEAD_GROUP == 0 and v_dim == V7X_LANES and 2 * qk_dim == v_dim
    assert 2 * lora == V7X_LANES and w_router.shape[1] == N_EXPERTS

    x2 = h3.reshape(n, d)
    row = lambda a: a.reshape(1, -1).astype(F32)

    gq_t = row(jnp.tile(da_q_norm_g, c_da // qk_dim))
    gk_t = row(jnp.tile(da_k_norm_g, c_da // qk_dim))
    gm_qk = _group_matrix(c_da, qk_dim, 1.0 / qk_dim)
    ta = _tile(t, ATTN_BLOCK)
    zrw, q, k, vt = _inproj(x2, row(norm_mix_g), w_in.astype(BF16), gq_t, gk_t, gm_qk, rw_cols, c_da,
                            1.0 / math.sqrt(qk_dim), _tile(ta, 256), b, da_heads, ta)

    zeros = jnp.zeros((lora, c_rw), F32)
    w2bd = jnp.concatenate([jnp.concatenate([rw_w2[0], zeros], axis=1),
                            jnp.concatenate([zeros, rw_w2[1]], axis=1)], axis=0).astype(BF16)
    a2bd = jnp.concatenate([jnp.concatenate([rw_a2[0], zeros], axis=1),
                            jnp.concatenate([zeros, rw_a2[1]], axis=1)], axis=0).astype(BF16)
    gsum = _group_matrix(c_rw, RW_HEAD, 1.0)
    r, vv, nkk, lw, kd, bd, g, bonus = _rwkv_prep(
        zrw.reshape(b, t, rw_cols), row(rw_mu), w2bd, row(rw_w0), a2bd, row(rw_a0), rw_g2.astype(BF16),
        row(rw_k_k), row(rw_k_a), row(rw_r_k), gsum, c_rw, lora, _tile(t, 256))
    yf, yb = _rwkv_scan(r, vv, nkk, lw, kd, bd, _tile(t, 128))

    slopes = 2.0 ** (-8.0 * jnp.arange(1, da_heads + 1, dtype=F32) / da_heads)
    q3 = q.reshape(b, t, c_da)
    k3 = k.reshape(b, t, c_da)
    small = (row(da_lq1), row(da_lk1), row(da_lq2), row(da_lk2), row(da_subln_g))

    def attention_running_max():
        v3 = vt[:, :, :, :v_dim, :].transpose(0, 2, 4, 1, 3).reshape(b, t, c_da)
        return _attention(slopes, q3, k3, v3, *small, da_heads, v_dim, lambda_init, ta, ta)

    score_bound = 1.01 * math.sqrt(qk_dim) * jnp.max(jnp.abs(da_q_norm_g)) * jnp.max(jnp.abs(da_k_norm_g))
    mb = jnp.ceil(2.0 * score_bound.astype(F32)) * 0.5
    o_da = lax.cond(
        mb <= MAX_FIXED_SHIFT,
        lambda: _attention_fixed(mb, q3, k3, vt, *small, da_heads, v_dim, lambda_init, ta),
        attention_running_max)

    gmean = _group_matrix(c_rw, RW_HEAD, 1.0 / RW_HEAD)
    w_out_bf = w_out.astype(BF16)
    h1, m, idx, gates = _post(
        yf.reshape(n, c_rw), yb.reshape(n, c_rw), bonus.reshape(n, c_rw), g.reshape(n, c_rw), o_da.reshape(n, c_da), x2,
        row(rw_ln_g), row(rw_ln_b), gmean, w_out_bf[:c_rw], w_out_bf[c_rw:], row(norm_ffn_g),
        w_router.astype(F32), row(b_router), RW_HEAD * 1e-5, _tile(n, 256))

    bm = MOE_BLOCK_ROWS
    rank, counts = _moe_rank(idx, _tile(n, 512))
    counts = counts.reshape(N_EXPERTS)
    padded = (counts + bm - 1) // bm * bm
    pend = jnp.cumsum(padded)
    offs = (pend - padded).astype(jnp.int32)
    n_blk = (n * TOP_K) // bm + N_EXPERTS
    blk_start = jnp.arange(n_blk, dtype=jnp.int32) * bm
    blk_e = jnp.minimum(jnp.sum(blk_start[:, None] >= pend[None, :], axis=1), N_EXPERTS - 1).astype(jnp.int32)
    n_used = (pend[-1] // bm).astype(jnp.int32).reshape(1)
    xs, slot_flat = _dispatch(offs, idx.reshape(n * TOP_K), rank.reshape(n * TOP_K), m, d // V7X_LANES,
                              n_blk * bm, _tile(n, 256))
    ys = _moe_ffn(blk_e, n_used, xs, w1.astype(BF16), b1, w2.astype(BF16), b2, bm)
    h2 = _combine(slot_flat, h1, gates, ys, _tile(n, 256))

    out = _ple(h2, p3.reshape(n, -1), w_ple.astype(BF16), row(ple_norm_g), w_ple_gate.astype(BF16), _tile(n, 512))
    return out.reshape(b, t, d)


def kernel(x, p, norm_mix_g, w_in, rw_mu, rw_w0, rw_w2, rw_a0, rw_a2, rw_g2, rw_k_k, rw_k_a, rw_r_k, rw_ln_g,
           rw_ln_b, da_q_norm_g, da_k_norm_g, da_lq1, da_lk1, da_lq2, da_lk2, da_subln_g, w_out, norm_ffn_g,
           w_router, b_router, w1, b1, w2, b2, w_ple, ple_norm_g, w_ple_gate):
    h = x.astype(F32)
    params = (norm_mix_g, w_in, rw_mu, rw_w0, rw_w2, rw_a0, rw_a2, rw_g2, rw_k_k, rw_k_a, rw_r_k, rw_ln_g, rw_ln_b,
              da_q_norm_g, da_k_norm_g, da_lq1, da_lk1, da_lq2, da_lk2, da_subln_g, w_out, norm_ffn_g,
              w_router, b_router, w1, b1, w2, b2, w_ple, ple_norm_g, w_ple_gate)
    for i in range(p.shape[0]):
        lambda_init = 0.8 - 0.6 * math.exp(-0.3 * i)
        h = _layer(h, p[i], lambda_init, *(a[i] for a in params))
    return h.astype(x.dtype)
```

```python
import functools
import math

import jax
import jax.numpy as jnp
import numpy as np
from jax import lax
from jax.experimental import pallas as pl
from jax.experimental.pallas import tpu as pltpu

F32 = jnp.float32
BF16 = jnp.bfloat16

V7X_LANES = 128
V7X_VMEM_BYTES = 64 * 1024 * 1024
VMEM_LIMIT = 52 * 1024 * 1024

NORM_EPS = 1e-6
RW_HEAD = 64
CHUNK = 64
HEAD_GROUP = 256
N_EXPERTS = 32
TOP_K = 4
MOE_BLOCK_ROWS = 512
SWIGLU_ALPHA = 1.702
SWIGLU_LIMIT = 7.0


def _cparams(sem, **kw):
    return pltpu.CompilerParams(dimension_semantics=sem, vmem_limit_bytes=VMEM_LIMIT, **kw)


def _dot(a, b):
    return jnp.dot(a, b, preferred_element_type=F32)


def _dot_hilo_lhs(g_bf16, x):
    hi, lo = _split2(x)
    return _dot(g_bf16, hi) + _dot(g_bf16, lo)


def _split2(x):
    hi = x.astype(BF16)
    return hi, (x - hi.astype(F32)).astype(BF16)


def _dot_hilo_rhs(x, g_bf16):
    hi, lo = _split2(x)
    out = _dot(jnp.concatenate([hi, lo], axis=0), g_bf16)
    return out[:x.shape[0]] + out[x.shape[0]:]


def _dot_hilo(a, b):
    ah, al = _split2(a)
    bh, bl = _split2(b)
    return _dot(ah, bh) + (_dot(ah, bl) + _dot(al, bh))


def _store_rows(ref, x2):
    rows, d = x2.shape
    c = d // V7X_LANES
    for j in range(c):
        ref[pl.ds(j, rows, stride=c), :] = x2[:, j * V7X_LANES:(j + 1) * V7X_LANES]


def _load_rows(ref, c):
    rows = ref.shape[0] // c
    return jnp.concatenate([ref[pl.ds(j, rows, stride=c), :] for j in range(c)], axis=1)


def _group_matrix(width, group, value):
    g = np.arange(width) // group
    return jnp.asarray(np.where(g[:, None] == g[None, :], value, 0.0), BF16)


def _inproj_kernel(x_ref, g_ref, w_ref, gq_ref, gk_ref, gm_ref, zrw_ref, q_ref, k_ref, vt_ref, *, rw_cols, c_da, scale):
    x = x_ref[...]
    ms = jnp.mean(x * x, axis=-1, keepdims=True)
    u = (x * lax.rsqrt(ms + NORM_EPS) * g_ref[...]).astype(BF16)
    z = _dot(u, w_ref[...])
    zrw_ref[...] = z[:, :rw_cols]
    zq = z[:, rw_cols:rw_cols + c_da]
    zk = z[:, rw_cols + c_da:rw_cols + 2 * c_da]
    zv = z[:, rw_cols + 2 * c_da:]
    tm = zq.shape[0]
    msqk = _dot(jnp.concatenate([zq * zq, zk * zk], axis=0).astype(BF16), gm_ref[...])
    q_ref[...] = (zq * lax.rsqrt(msqk[:tm] + NORM_EPS) * (gq_ref[...] * scale)).astype(BF16)
    k_ref[...] = (zk * lax.rsqrt(msqk[tm:] + NORM_EPS) * gk_ref[...]).astype(BF16)
    heads, vrows, _ = vt_ref.shape
    v_dim = c_da // heads
    pad_row = lax.broadcasted_iota(jnp.int32, (vrows - v_dim, tm), 0)
    pad = jnp.where(pad_row == 0, 1.0, 0.0).astype(BF16)
    for hd in range(heads):
        vt_ref[hd, 0:v_dim, :] = zv[:, hd * v_dim:(hd + 1) * v_dim].T.astype(BF16)
        vt_ref[hd, v_dim:, :] = pad


def _inproj(x2, g, w_bf, gq_t, gk_t, gm, rw_cols, c_da, scale, tm, b, heads, tb):
    n, d = x2.shape
    cols = w_bf.shape[1]
    t = n // b
    tps = t // tm
    per = tb // tm
    vrows = c_da // heads + ATTN_V_PAD
    full = lambda i: (0, 0)
    row = lambda i: (i, 0)
    return pl.pallas_call(
        functools.partial(_inproj_kernel, rw_cols=rw_cols, c_da=c_da, scale=scale),
        grid=(n // tm,),
        in_specs=[
            pl.BlockSpec((tm, d), row),
            pl.BlockSpec((1, d), full),
            pl.BlockSpec((d, cols), full),
            pl.BlockSpec((1, c_da), full),
            pl.BlockSpec((1, c_da), full),
            pl.BlockSpec((c_da, c_da), full),
        ],
        out_specs=[
            pl.BlockSpec((tm, rw_cols), row),
            pl.BlockSpec((tm, c_da), row),
            pl.BlockSpec((tm, c_da), row),
            pl.BlockSpec((None, heads, None, vrows, tm),
                         lambda i: (i // tps, 0, (i % tps) // per, 0, (i % tps) % per)),
        ],
        out_shape=[
            jax.ShapeDtypeStruct((n, rw_cols), F32),
            jax.ShapeDtypeStruct((n, c_da), BF16),
            jax.ShapeDtypeStruct((n, c_da), BF16),
            jax.ShapeDtypeStruct((b, heads, t // tb, vrows, tb), BF16),
        ],
        compiler_params=_cparams(("parallel",)),
        name="inproj",
    )(x2, g, w_bf, gq_t, gk_t, gm)


def _rwkv_prep_kernel(z_ref, zp_ref, zn_ref, mu_ref, w2_ref, w0_ref, a2_ref, a0_ref, g2_ref, kk_ref, ka_ref,
                      rk_ref, gsum_ref,
                      r_ref, v_ref, nkk_ref, lw_ref, kd_ref, bd_ref, g_ref, bonus_ref, *, c_rw, lora):
    i = pl.program_id(1)
    nt = pl.num_programs(1)
    z = z_ref[...]
    tt = z.shape[0]
    rows = lax.broadcasted_iota(jnp.int32, z.shape, 0)
    prev_row = jnp.where(i == 0, 0.0, zp_ref[7:8, :])
    next_row = jnp.where(i == nt - 1, 0.0, zn_ref[0:1, :])
    zprev = jnp.where(rows == 0, prev_row, pltpu.roll(z, 1, 0))
    znext = jnp.where(rows == tt - 1, next_row, pltpu.roll(z, tt - 1, 0))
    z = z + (0.5 * (zprev + znext) - z) * mu_ref[...]

    r = z[:, 0:c_rw]
    k = z[:, c_rw:2 * c_rw]
    v = z[:, 2 * c_rw:3 * c_rw]
    o = 3 * c_rw
    wd = z[:, o:o + 2 * lora]
    ad = z[:, o + 2 * lora:o + 4 * lora]
    gd = z[:, o + 4 * lora:]

    xw = w0_ref[...] + _dot(jnp.tanh(wd).astype(BF16), w2_ref[...])
    xa = a0_ref[...] + _dot(ad.astype(BF16), a2_ref[...])
    lw = -math.exp(-0.5) * jax.nn.sigmoid(xw)
    rate = jax.nn.sigmoid(xa)
    g = _dot(jax.nn.sigmoid(gd).astype(BF16), g2_ref[...])

    gsum = gsum_ref[...]
    kkr = k * kk_ref[...]
    ka = ka_ref[...]
    kd0 = k * (1.0 + (rate[:, :c_rw] - 1.0) * ka)
    kd1 = k * (1.0 + (rate[:, c_rw:] - 1.0) * ka)
    sums = _dot_hilo_rhs(jnp.concatenate([kkr * kkr, r * (0.5 * (kd0 + kd1)) * rk_ref[...]], axis=0), gsum)
    kk = kkr * lax.rsqrt(sums[:tt] + 1e-12)
    bonus = sums[tt:] * v

    r_ref[...] = r.astype(BF16)
    v_ref[...] = v.astype(BF16)
    nkk_ref[...] = (-kk).astype(BF16)
    lw_ref[0] = lw[:, :c_rw]
    lw_ref[1] = lw[:, c_rw:]
    kd_ref[0] = kd0.astype(BF16)
    kd_ref[1] = kd1.astype(BF16)
    bd_ref[0] = (kk * rate[:, :c_rw]).astype(BF16)
    bd_ref[1] = (kk * rate[:, c_rw:]).astype(BF16)
    g_ref[...] = g
    bonus_ref[...] = bonus


def _rwkv_prep(zrw3, mu, w2bd, w0f, a2bd, a0f, g2_bf, k_k, k_a, r_kf, gsum, c_rw, lora, tt):
    b, t, cols = zrw3.shape
    nt = t // tt
    hb = tt // 8
    nhb = t // 8
    full = lambda bi, i: (0, 0)
    tile = lambda bi, i: (bi, i, 0)
    dtile = lambda bi, i: (0, bi, i, 0)
    one = jax.ShapeDtypeStruct((b, t, c_rw), F32)
    two = jax.ShapeDtypeStruct((2, b, t, c_rw), F32)
    one16 = jax.ShapeDtypeStruct((b, t, c_rw), BF16)
    two16 = jax.ShapeDtypeStruct((2, b, t, c_rw), BF16)
    return pl.pallas_call(
        functools.partial(_rwkv_prep_kernel, c_rw=c_rw, lora=lora),
        grid=(b, nt),
        in_specs=[
            pl.BlockSpec((None, tt, cols), tile),
            pl.BlockSpec((None, 8, cols), lambda bi, i: (bi, jnp.maximum(i * hb - 1, 0), 0)),
            pl.BlockSpec((None, 8, cols), lambda bi, i: (bi, jnp.minimum((i + 1) * hb, nhb - 1), 0)),
            pl.BlockSpec((1, cols), full),
            pl.BlockSpec(w2bd.shape, full),
            pl.BlockSpec((1, 2 * c_rw), full),
            pl.BlockSpec(a2bd.shape, full),
            pl.BlockSpec((1, 2 * c_rw), full),
            pl.BlockSpec(g2_bf.shape, full),
            pl.BlockSpec((1, c_rw), full),
            pl.BlockSpec((1, c_rw), full),
            pl.BlockSpec((1, c_rw), full),
            pl.BlockSpec((c_rw, c_rw), full),
        ],
        out_specs=[
            pl.BlockSpec((None, tt, c_rw), tile),
            pl.BlockSpec((None, tt, c_rw), tile),
            pl.BlockSpec((None, tt, c_rw), tile),
            pl.BlockSpec((2, None, tt, c_rw), dtile),
            pl.BlockSpec((2, None, tt, c_rw), dtile),
            pl.BlockSpec((2, None, tt, c_rw), dtile),
            pl.BlockSpec((None, tt, c_rw), tile),
            pl.BlockSpec((None, tt, c_rw), tile),
        ],
        out_shape=[one16, one16, one16, two, two16, two16, one, one],
        compiler_params=_cparams(("parallel", "parallel")),
        name="rwkv_prep",
    )(zrw3, zrw3, zrw3, mu, w2bd, w0f, a2bd, a0f, g2_bf, k_k, k_a, r_kf, gsum)


M_STRICT, M_INCL, M_EYE, M_PAIR, M_LEVEL = 0, 1, 2, 3, 4
N_LEVELS = 5
SCAN_SEQS = 2


def _scan_tables(w):
    c = CHUNK
    masks = []
    mcums = []
    for d in range(2):
        t = np.broadcast_to(np.arange(c)[:, None], (c, w))
        s = np.broadcast_to((np.arange(w) % RW_HEAD)[None, :], (c, w))
        tt = np.broadcast_to(np.arange(c)[:, None], (c, c))
        ss = np.broadcast_to(np.arange(c)[None, :], (c, c))
        if d == 1:
            t, s, tt, ss = c - 1 - t, c - 1 - s, c - 1 - tt, c - 1 - ss
        rows = [s < t, s <= t, s == t, (t // 2 == s // 2) & (t > s)]
        sz = 2
        while sz < c:
            rows.append((t // (2 * sz) == s // (2 * sz)) & ((t // sz) % 2 == 1) & ((s // sz) % 2 == 0))
            sz *= 2
        assert len(rows) == M_LEVEL + N_LEVELS
        masks.append(np.stack(rows).astype(np.float32))
        mcums.append((ss <= tt).astype(np.float32))
    return jnp.asarray(np.stack(masks), F32), jnp.asarray(np.stack(mcums), BF16)


def _block_diag(y, bd16):
    g = y.shape[1] // RW_HEAD
    return jnp.concatenate([y.astype(BF16)] * g, axis=0) * bd16


def _bmm(x, y, bd16):
    outs = []
    for s in range(0, y.shape[1], HEAD_GROUP):
        outs.append(_dot(x[:, s:s + HEAD_GROUP].astype(BF16), _block_diag(y[:, s:s + HEAD_GROUP], bd16)))
    return jnp.concatenate(outs, axis=1)


def _bmm_nt(x, y, bd16):
    outs = []
    for s in range(0, y.shape[1], HEAD_GROUP):
        bd = _block_diag(y[:, s:s + HEAD_GROUP], bd16)
        outs.append(lax.dot_general(x[:, s:s + HEAD_GROUP].astype(BF16), bd, (((1,), (1,)), ((), ())),
                                    preferred_element_type=F32))
    return jnp.concatenate(outs, axis=1)


def _bmm_tn(x, y, bd32):
    outs = []
    for s in range(0, y.shape[1], HEAD_GROUP):
        xt = x[:, s:s + HEAD_GROUP].T.astype(BF16)
        full = _dot(xt, y[:, s:s + HEAD_GROUP].astype(BF16)) * bd32
        o = full[0:RW_HEAD]
        for j in range(1, HEAD_GROUP // RW_HEAD):
            o = o + full[j * RW_HEAD:(j + 1) * RW_HEAD]
        outs.append(o)
    return jnp.concatenate(outs, axis=1)


def _scan_chunks(r, lw, k, v, a, b, state, masks, mcum, bd16, bd32):
    c = r[0].shape[0]

    def each(f, *cols):
        return [f(*xs) for xs in zip(*cols)]

    cum = each(_dot_hilo_lhs, mcum, lw)
    tot = each(lambda x: jnp.sum(x, axis=0, keepdims=True), lw)
    e_neg = each(lambda x: jnp.exp(-x), cum)
    e_tail = each(lambda t_, x: jnp.exp(t_ - x), tot, cum)
    rt = each(lambda x, cm: x * jnp.exp(cm), r, cum)
    at = each(lambda x, cm, l_: x * jnp.exp(cm - l_), a, cum, lw)
    bt = each(jnp.multiply, b, e_neg)
    kt = each(jnp.multiply, k, e_neg)
    bh = each(jnp.multiply, b, e_tail)
    kh = each(jnp.multiply, k, e_tail)

    ar = each(lambda x, y_: jnp.concatenate([x, y_], axis=0), at, rt)
    pb = each(lambda x, y_: _bmm_nt(x, y_, bd16), ar, bt)
    pk = each(lambda x, y_: _bmm_nt(x, y_, bd16), ar, kt)
    a_ab = each(lambda x, m: x[:c] * m[M_STRICT], pb, masks)
    a_rb = each(lambda x, m: x[c:] * m[M_INCL], pb, masks)
    a_ak = each(lambda x, m: x[:c] * m[M_STRICT], pk, masks)
    a_rk = each(lambda x, m: x[c:] * m[M_INCL], pk, masks)

    tinv = each(lambda x, m: m[M_EYE] + x * m[M_PAIR], a_ab, masks)
    for lvl in range(N_LEVELS):
        x = each(lambda aa, m, ti: _bmm(aa * m[M_LEVEL + lvl], ti, bd16), a_ab, masks, tinv)
        tinv = each(lambda ti, x_: ti + _bmm(ti, x_, bd16), tinv, x)

    bmm = lambda p, q: _bmm(p, q, bd16)
    stack = lambda p, q: jnp.concatenate([p, q], axis=0)
    att = each(bmm, tinv, at)
    akv = each(lambda p, p2, q: _bmm(stack(p, p2), q, bd16), a_ak, a_rk, v)
    vt = each(lambda ti, x_: _bmm(ti, x_[:c], bd16), tinv, akv)
    rh = each(lambda x, p, q: x + _bmm(p, q, bd16), rt, a_rb, att)
    y_in = each(lambda p, q, x_: _bmm(p, q, bd16) + x_[c:], a_rb, vt, akv)
    mc = each(lambda m, t_, p, q: m[M_EYE] * jnp.exp(t_) + _bmm_tn(p, q, bd32), masks, tot, bh, att)
    nc = each(lambda p, p2, q, q2: _bmm_tn(stack(p, p2), stack(q, q2), bd32), bh, kh, vt, v)
    onst = each(lambda p, p2, q: _bmm(stack(p, p2), q, bd16), rh, mc, state)
    y = each(lambda x_, yi: x_[:c] + yi, onst, y_in)
    new_state = each(lambda x_, n_: x_[c:] + n_, onst, nc)
    return y, new_state


def _rwkv_scan_kernel(rf_ref, vf_ref, af_ref, rb_ref, vb_ref, ab_ref, lwf_ref, kf_ref, bf_ref, lwb_ref, kb_ref,
                      bb_ref, mask_ref, mcum_ref, bd16_ref, bd32_ref, yf_ref, yb_ref, state_ref, *, n_chunks):
    @pl.when(pl.program_id(1) == 0)
    def _():
        state_ref[...] = jnp.zeros_like(state_ref)

    dirs = ((0, rf_ref, lwf_ref, kf_ref, vf_ref, af_ref, bf_ref, yf_ref),
            (1, rb_ref, lwb_ref, kb_ref, vb_ref, ab_ref, bb_ref, yb_ref))
    n_seq = rf_ref.shape[0]

    def body(j, carry):
        cols = [[] for _ in range(9)]
        where = []
        for d, r_ref, lw_ref, k_ref, v_ref, a_ref, b_ref, y_ref in dirs:
            cj = j if d == 0 else n_chunks - 1 - j
            sl = pl.ds(pl.multiple_of(cj * CHUNK, CHUNK), CHUNK)
            for g in range(n_seq):
                vals = (r_ref[g, sl, :], lw_ref[g, sl, :], k_ref[g, sl, :], v_ref[g, sl, :], a_ref[g, sl, :],
                        b_ref[g, sl, :], state_ref[d, g], mask_ref.at[d], mcum_ref[d])
                for col, val in zip(cols, vals):
                    col.append(val)
                where.append((y_ref, d, g, sl))
        ys, sts = _scan_chunks(*cols, bd16_ref[...], bd32_ref[...])
        for (y_ref, d, g, sl), y, st in zip(where, ys, sts):
            y_ref[g, sl, :] = y
            state_ref[d, g] = st
        return carry

    lax.fori_loop(0, n_chunks, body, 0)


def _rwkv_scan(r, v, nkk, lw, kd, bd, tb):
    b, t, w = r.shape
    nb = t // tb
    g = SCAN_SEQS if b % SCAN_SEQS == 0 else 1
    masks, mcum = _scan_tables(w)
    bd32 = _group_matrix(HEAD_GROUP, RW_HEAD, 1.0).astype(F32)
    bd16 = bd32.astype(BF16)
    fwd = lambda bi, i: (bi, i, 0)
    bwd = lambda bi, i: (bi, nb - 1 - i, 0)
    fwd_d = lambda bi, i: (0, bi, i, 0)
    bwd_d = lambda bi, i: (1, bi, nb - 1 - i, 0)
    const = lambda nd: (lambda bi, i: (0,) * nd)
    blk = (g, tb, w)
    dblk = (None, g, tb, w)
    out = jax.ShapeDtypeStruct((b, t, w), F32)
    return pl.pallas_call(
        functools.partial(_rwkv_scan_kernel, n_chunks=tb // CHUNK),
        grid=(b // g, nb),
        in_specs=[
            pl.BlockSpec(blk, fwd), pl.BlockSpec(blk, fwd), pl.BlockSpec(blk, fwd),
            pl.BlockSpec(blk, bwd), pl.BlockSpec(blk, bwd), pl.BlockSpec(blk, bwd),
            pl.BlockSpec(dblk, fwd_d), pl.BlockSpec(dblk, fwd_d), pl.BlockSpec(dblk, fwd_d),
            pl.BlockSpec(dblk, bwd_d), pl.BlockSpec(dblk, bwd_d), pl.BlockSpec(dblk, bwd_d),
            pl.BlockSpec(masks.shape, const(4)),
            pl.BlockSpec(mcum.shape, const(3)),
            pl.BlockSpec(bd16.shape, const(2)),
            pl.BlockSpec(bd32.shape, const(2)),
        ],
        out_specs=[pl.BlockSpec(blk, fwd), pl.BlockSpec(blk, bwd)],
        out_shape=[out, out],
        scratch_shapes=[pltpu.VMEM((2, g, CHUNK, w), F32)],
        compiler_params=_cparams(("parallel", "arbitrary")),
        name="rwkv_scan",
    )(r, v, nkk, r, v, nkk, lw, kd, bd, lw, kd, bd, masks, mcum, bd16, bd32)


def _attn_kernel(slope_ref, q_ref, k_ref, v_ref, lq1_ref, lk1_ref, lq2_ref, lk2_ref, sg_ref, o_ref,
                 m_ref, l_ref, acc_ref, *, qk_dim, lambda_init):
    h = pl.program_id(1)
    i = pl.program_id(2)
    j = pl.program_id(3)
    nk = pl.num_programs(3)
    tq = q_ref.shape[0]
    tk = k_ref.shape[0]

    @pl.when(j == 0)
    def _():
        m_ref[...] = jnp.full(m_ref.shape, -jnp.inf, F32)
        l_ref[...] = jnp.zeros_like(l_ref)
        acc_ref[...] = jnp.zeros_like(acc_ref)

    q = q_ref[...]
    k = k_ref[...]
    v = v_ref[...]
    lane = lax.broadcasted_iota(jnp.int32, q.shape, 1)
    zero = jnp.zeros_like(q)
    qi = lax.broadcasted_iota(jnp.int32, (tq, tk), 0) + i * tq
    kj = lax.broadcasted_iota(jnp.int32, (tq, tk), 1) + j * tk
    bias = slope_ref[h] * jnp.abs(qi - kj).astype(F32)
    for c in range(2):
        qc = jnp.where((lane < qk_dim) == (c == 0), q, zero)
        s = lax.dot_general(qc, k, (((1,), (1,)), ((), ())), preferred_element_type=F32) - bias
        m_old = m_ref[c]
        m_new = jnp.maximum(m_old, jnp.max(s, axis=-1, keepdims=True))
        alpha = jnp.exp(m_old - m_new)
        p = jnp.exp(s - m_new)
        l_ref[c] = alpha * l_ref[c] + jnp.sum(p, axis=-1, keepdims=True)
        acc_ref[c] = alpha * acc_ref[c] + _dot(p.astype(BF16), v)
        m_ref[c] = m_new

    @pl.when(j == nk - 1)
    def _():
        lam = (jnp.exp(jnp.sum(lq1_ref[...] * lk1_ref[...], axis=-1, keepdims=True))
               - jnp.exp(jnp.sum(lq2_ref[...] * lk2_ref[...], axis=-1, keepdims=True)) + lambda_init)
        o = acc_ref[0] / l_ref[0] - lam * (acc_ref[1] / l_ref[1])
        ms = jnp.mean(o * o, axis=-1, keepdims=True)
        o_ref[...] = o * lax.rsqrt(ms + NORM_EPS) * (sg_ref[...] * (1.0 - lambda_init))


def _attention(slopes, q3, k3, v3, lq1, lk1, lq2, lk2, subln_g, heads, v_dim, lambda_init, tq, tk):
    b, t, _ = q3.shape
    small = lambda s, bi, h, i, j: (0, 0)
    grid_spec = pltpu.PrefetchScalarGridSpec(
        num_scalar_prefetch=1,
        grid=(b, heads, t // tq, t // tk),
        in_specs=[
            pl.BlockSpec((None, tq, v_dim), lambda bi, h, i, j, s: (bi, i, h)),
            pl.BlockSpec((None, tk, v_dim), lambda bi, h, i, j, s: (bi, j, h)),
            pl.BlockSpec((None, tk, v_dim), lambda bi, h, i, j, s: (bi, j, h)),
            pl.BlockSpec(lq1.shape, lambda bi, h, i, j, s: (0, 0)),
            pl.BlockSpec(lk1.shape, lambda bi, h, i, j, s: (0, 0)),
            pl.BlockSpec(lq2.shape, lambda bi, h, i, j, s: (0, 0)),
            pl.BlockSpec(lk2.shape, lambda bi, h, i, j, s: (0, 0)),
            pl.BlockSpec(subln_g.shape, lambda bi, h, i, j, s: (0, 0)),
        ],
        out_specs=pl.BlockSpec((None, tq, v_dim), lambda bi, h, i, j, s: (bi, i, h)),
        scratch_shapes=[
            pltpu.VMEM((2, tq, 1), F32),
            pltpu.VMEM((2, tq, 1), F32),
            pltpu.VMEM((2, tq, v_dim), F32),
        ],
    )
    del small
    return pl.pallas_call(
        functools.partial(_attn_kernel, qk_dim=v_dim // 2, lambda_init=lambda_init),
        grid_spec=grid_spec,
        out_shape=jax.ShapeDtypeStruct((b, t, heads * v_dim), F32),
        compiler_params=_cparams(("parallel", "parallel", "parallel", "arbitrary")),
        name="attn",
    )(slopes, q3, k3, v3, lq1, lk1, lq2, lk2, subln_g)


EXP_ZERO_ARG = 104.0
MAX_FIXED_SHIFT = 40.0
N_AUG = 6
ATTN_BLOCK = 512
ATTN_KEY_SPLIT = 2
ATTN_V_PAD = 16


def _attn_fixed_kernel(w_ref, slope_ref, mb_ref, q_ref, k_ref, vt_ref, dbias_ref, qaug_ref, kaug_ref,
                       lq1_ref, lk1_ref, lq2_ref, lk2_ref, sg_ref, o_ref, acc_ref, *, qk_dim, lambda_init):
    h = pl.program_id(1)
    i = pl.program_id(2)
    tb = q_ref.shape[0]
    nk = k_ref.shape[0] // tb
    v_dim = 2 * qk_dim
    slope = slope_ref[h]
    mb = mb_ref[0]
    w = w_ref[h]
    acc_ref[...] = jnp.zeros_like(acc_ref)

    q = q_ref[...]
    lane = lax.broadcasted_iota(jnp.int32, q.shape, 1)
    lane_row = lax.broadcasted_iota(jnp.int32, (1, q.shape[1]), 1)
    is_c0 = lane < qk_dim

    def block(j, qvar, kvar, delta, diag):
        kblk = k_ref[pl.ds(pl.multiple_of(j * tb, tb), tb), :]
        vtblk = vt_ref[j]
        sub = tb // ATTN_KEY_SPLIT
        chains = []
        for c in range(2):
            base = qk_dim * (1 - c)
            data = is_c0 if c == 0 else jnp.logical_not(is_c0)
            dyn = jnp.where(lane_row == base + 4, -mb,
                            jnp.where(lane_row == base + 5, -(slope * tb) * delta, 0.0)).astype(BF16)
            qa = jnp.where(data, q, qaug_ref[qvar, c])
            ka = jnp.where(data, kblk, kaug_ref[kvar, c] + dyn)
            for u in range(ATTN_KEY_SPLIT):
                chains.append((c, u, qa, ka[u * sub:(u + 1) * sub]))
        sts = [lax.dot_general(ka, qa, (((1,), (1,)), ((), ())), preferred_element_type=F32)
               for _, _, qa, ka in chains]
        if diag:
            sts = [st - dbias_ref[u * sub:(u + 1) * sub, :] for st, (_, u, _, _) in zip(sts, chains)]
        pts = [jnp.exp(st).astype(BF16) for st in sts]
        for c in range(2):
            upd = None
            for pt, (cc, u, _, _) in zip(pts, chains):
                if cc == c:
                    part = _dot(vtblk[:, u * sub:(u + 1) * sub], pt)
                    upd = part if upd is None else upd + part
            acc_ref[c] += upd

    def below(j, carry):
        block(j, 0, 0, (i - j).astype(F32), False)
        return carry

    def above(j, carry):
        block(j, 1, 1, (j - i).astype(F32), False)
        return carry

    lax.fori_loop(jnp.maximum(i - w, 0), i, below, 0)
    block(i, 2, 0, jnp.zeros((), F32), True)
    lax.fori_loop(i + 1, jnp.minimum(i + w, nk - 1) + 1, above, 0)

    lam = (jnp.exp(jnp.sum(lq1_ref[...] * lk1_ref[...], axis=-1, keepdims=True))
           - jnp.exp(jnp.sum(lq2_ref[...] * lk2_ref[...], axis=-1, keepdims=True)) + lambda_init)
    a0 = acc_ref[0]
    a1 = acc_ref[1]
    ot = a0[:v_dim] / a0[v_dim:v_dim + 1] - lam * (a1[:v_dim] / a1[v_dim:v_dim + 1])
    ms = jnp.mean(ot * ot, axis=0, keepdims=True)
    o_ref[...] = (ot * lax.rsqrt(ms + NORM_EPS) * (sg_ref[...] * (1.0 - lambda_init))).T


def _attn_aug_tables(slopes, tb, qk_dim):
    heads = len(slopes)
    lanes = 2 * qk_dim
    pos = np.arange(tb)
    qaug = np.zeros((heads, 3, 2, tb, lanes), np.float32)
    kaug = np.zeros((heads, 2, 2, tb, lanes), np.float32)
    for hd, sl in enumerate(slopes):
        for var in range(3):
            rr = pos if var != 1 else tb - 1 - pos
            hi = (rr // 16) * (16.0 * sl)
            lo = (rr % 16) * sl
            for c in range(2):
                base = qk_dim * (1 - c)
                if var == 2:
                    qaug[hd, var, c, :, base + 4] = 1.0
                    continue
                qaug[hd, var, c, :, base + 0] = -hi
                qaug[hd, var, c, :, base + 1] = -lo
                qaug[hd, var, c, :, base + 2:base + N_AUG] = 1.0
                kaug[hd, var, c, :, base + 0:base + 2] = 1.0
                kaug[hd, var, c, :, base + 2] = hi
                kaug[hd, var, c, :, base + 3] = lo
    return jnp.asarray(qaug, BF16), jnp.asarray(kaug, BF16)


def _attention_fixed(mb, q3, k3, vt, lq1, lk1, lq2, lk2, subln_g, heads, v_dim, lambda_init, tb):
    b, t, _ = q3.shape
    qk_dim = v_dim // 2
    slopes = [2.0 ** (-8.0 * (hd + 1) / heads) for hd in range(heads)]
    widths = [int(math.ceil((EXP_ZERO_ARG / s - 1.0) / tb)) for s in slopes]
    qaug, kaug = _attn_aug_tables(slopes, tb, qk_dim)
    pos = np.arange(tb)
    dist = np.abs(pos[:, None] - pos[None, :]).astype(np.float32)
    dbias = jnp.asarray(np.stack([dist * s for s in slopes]), F32)
    nb = t // tb
    vrows = v_dim + ATTN_V_PAD
    assert vt.shape == (b, heads, nb, vrows, tb)
    const = lambda bi, h, i, *_: (0, 0)
    grid_spec = pltpu.PrefetchScalarGridSpec(
        num_scalar_prefetch=3,
        grid=(b, heads, t // tb),
        in_specs=[
            pl.BlockSpec((None, tb, v_dim), lambda bi, h, i, *_: (bi, i, h)),
            pl.BlockSpec((None, t, v_dim), lambda bi, h, i, *_: (bi, 0, h)),
            pl.BlockSpec((None, None, nb, vrows, tb), lambda bi, h, i, *_: (bi, h, 0, 0, 0)),
            pl.BlockSpec((None, tb, tb), lambda bi, h, i, *_: (h, 0, 0)),
            pl.BlockSpec((None, 3, 2, tb, v_dim), lambda bi, h, i, *_: (h, 0, 0, 0, 0)),
            pl.BlockSpec((None, 2, 2, tb, v_dim), lambda bi, h, i, *_: (h, 0, 0, 0, 0)),
            pl.BlockSpec(lq1.shape, const),
            pl.BlockSpec(lk1.shape, const),
            pl.BlockSpec(lq2.shape, const),
            pl.BlockSpec(lk2.shape, const),
            pl.BlockSpec((v_dim, 1), const),
        ],
        out_specs=pl.BlockSpec((None, tb, v_dim), lambda bi, h, i, *_: (bi, i, h)),
        scratch_shapes=[pltpu.VMEM((2, vrows, tb), F32)],
    )
    return pl.pallas_call(
        functools.partial(_attn_fixed_kernel, qk_dim=qk_dim, lambda_init=lambda_init),
        grid_spec=grid_spec,
        out_shape=jax.ShapeDtypeStruct((b, t, heads * v_dim), F32),
        compiler_params=_cparams(("parallel", "parallel", "parallel")),
        name="attn_fixed",
    )(jnp.asarray(widths, jnp.int32), jnp.asarray(slopes, F32), mb.reshape(1).astype(F32),
      q3, k3, vt, dbias, qaug, kaug, lq1, lk1, lq2, lk2, subln_g.reshape(v_dim, 1))


def _post_kernel(yf_ref, yb_ref, bonus_ref, g_ref, oda_ref, x_ref, lng_ref, lnb_ref, gmean_ref, wo_rw_ref, wo_da_ref,
                 gffn_ref, wr_ref, br_ref, h_ref, m_ref, idx_ref, gate_ref, cnt_ref, *, gn_eps):
    @pl.when(pl.program_id(0) == 0)
    def _():
        cnt_ref[...] = jnp.zeros_like(cnt_ref)

    gmean = gmean_ref[...]
    y = yf_ref[...] + yb_ref[...]
    mean = _dot_hilo_rhs(y, gmean)
    yc = y - mean
    var = _dot_hilo_rhs(yc * yc, gmean)
    yn = yc * lax.rsqrt(var + gn_eps) * lng_ref[...] + lnb_ref[...]
    o_rw = (yn + bonus_ref[...]) * g_ref[...]
    h = x_ref[...] + _dot(o_rw.astype(BF16), wo_rw_ref[...]) + _dot(oda_ref[...].astype(BF16), wo_da_ref[...])
    h_ref[...] = h
    ms = jnp.mean(h * h, axis=-1, keepdims=True)
    m = h * lax.rsqrt(ms + NORM_EPS) * gffn_ref[...]
    _store_rows(m_ref, m)

    logits = _dot_hilo(m, wr_ref[...]) + br_ref[...]
    ne = logits.shape[1]
    lane = lax.broadcasted_iota(jnp.int32, logits.shape, 1)
    kcol = lax.broadcasted_iota(jnp.int32, (logits.shape[0], TOP_K), 1)
    idx_all = jnp.zeros((logits.shape[0], TOP_K), jnp.int32)
    val_all = jnp.zeros((logits.shape[0], TOP_K), F32)
    work = logits
    chosen = jnp.zeros(logits.shape, F32)
    for kk in range(TOP_K):
        mx = jnp.max(work, axis=-1, keepdims=True)
        sel = jnp.min(jnp.where(work == mx, lane, ne), axis=-1, keepdims=True)
        idx_all = jnp.where(kcol == kk, sel, idx_all)
        val_all = jnp.where(kcol == kk, mx, val_all)
        hit = lane == sel
        chosen = jnp.where(hit, 1.0, chosen)
        work = jnp.where(hit, -jnp.inf, work)
    e = jnp.exp(val_all - jnp.max(val_all, axis=-1, keepdims=True))
    idx_ref[...] = idx_all
    gate_ref[...] = e / jnp.sum(e, axis=-1, keepdims=True)
    cnt_ref[...] += jnp.sum(chosen, axis=0, keepdims=True)


def _post(yf, yb, bonus, g, oda, x2, lng, lnb, gmean, wo_rw, wo_da, gffn, wr, br, gn_eps, tm):
    n, c = yf.shape
    d = x2.shape[1]
    ne = wr.shape[1]
    full = lambda i: (0, 0)
    row = lambda i: (i, 0)
    return pl.pallas_call(
        functools.partial(_post_kernel, gn_eps=gn_eps),
        grid=(n // tm,),
        in_specs=[
            pl.BlockSpec((tm, c), row),
            pl.BlockSpec((tm, c), row),
            pl.BlockSpec((tm, c), row),
            pl.BlockSpec((tm, c), row),
            pl.BlockSpec((tm, c), row),
            pl.BlockSpec((tm, d), row),
            pl.BlockSpec((1, c), full),
            pl.BlockSpec((1, c), full),
            pl.BlockSpec((c, c), full),
            pl.BlockSpec((c, d), full),
            pl.BlockSpec((c, d), full),
            pl.BlockSpec((1, d), full),
            pl.BlockSpec((d, ne), full),
            pl.BlockSpec((1, ne), full),
        ],
        out_specs=[
            pl.BlockSpec((tm, d), row),
            pl.BlockSpec((tm * d // V7X_LANES, V7X_LANES), row),
            pl.BlockSpec((tm, TOP_K), row),
            pl.BlockSpec((tm, TOP_K), row),
            pl.BlockSpec((1, ne), full),
        ],
        out_shape=[
            jax.ShapeDtypeStruct((n, d), F32),
            jax.ShapeDtypeStruct((n * d // V7X_LANES, V7X_LANES), F32),
            jax.ShapeDtypeStruct((n, TOP_K), jnp.int32),
            jax.ShapeDtypeStruct((n, TOP_K), F32),
            jax.ShapeDtypeStruct((1, ne), F32),
        ],
        compiler_params=_cparams(("arbitrary",)),
        name="post",
    )(yf, yb, bonus, g, oda, x2, lng, lnb, gmean, wo_rw, wo_da, gffn, wr, br)


def _rank_kernel(idx_ref, ltri_ref, offs_ref, slot_ref, carry_ref):
    @pl.when(pl.program_id(0) == 0)
    def _():
        carry_ref[...] = offs_ref[...]

    idx = idx_ref[...]
    tr = idx.shape[0]
    lane = lax.broadcasted_iota(jnp.int32, (tr, N_EXPERTS), 1)
    hots = [jnp.where(lane == idx[:, kk:kk + 1], 1.0, 0.0) for kk in range(TOP_K)]
    mask = hots[0]
    for kk in range(1, TOP_K):
        mask = mask + hots[kk]
    before = _dot(ltri_ref[...], mask.astype(BF16)) + carry_ref[...]
    kcol = lax.broadcasted_iota(jnp.int32, (tr, TOP_K), 1)
    slot = jnp.zeros((tr, TOP_K), F32)
    for kk in range(TOP_K):
        rk = jnp.sum(hots[kk] * before, axis=-1, keepdims=True)
        slot = jnp.where(kcol == kk, rk, slot)
    slot_ref[...] = slot.astype(jnp.int32)
    carry_ref[...] = carry_ref[...] + jnp.sum(mask, axis=0, keepdims=True)


def _moe_rank(idx, offs_row, tr):
    n = idx.shape[0]
    ltri = jnp.asarray(np.tril(np.ones((tr, tr), np.float32), -1), BF16)
    return pl.pallas_call(
        _rank_kernel,
        grid=(n // tr,),
        in_specs=[pl.BlockSpec((tr, TOP_K), lambda i: (i, 0)), pl.BlockSpec((tr, tr), lambda i: (0, 0)),
                  pl.BlockSpec((1, N_EXPERTS), lambda i: (0, 0))],
        out_specs=pl.BlockSpec((tr, TOP_K), lambda i: (i, 0)),
        out_shape=jax.ShapeDtypeStruct((n, TOP_K), jnp.int32),
        scratch_shapes=[pltpu.VMEM((1, N_EXPERTS), F32)],
        compiler_params=_cparams(("arbitrary",)),
        name="moe_rank",
    )(idx, ltri, offs_row)


ROW_UNROLL = 8


def _row(r, c):
    return pl.ds(pl.multiple_of(r * c, c), c)


def _dispatch_kernel(slot_ref, m_ref, xs_ref, sem, *, c):
    td = m_ref.shape[0] // c

    def issue(g, carry):
        for u in range(ROW_UNROLL):
            t = g * ROW_UNROLL + u
            for kk in range(TOP_K):
                dest = slot_ref[t * TOP_K + kk]
                pltpu.make_async_copy(m_ref.at[_row(t, c), :], xs_ref.at[_row(dest, c), :], sem).start(
                    priority=kk % 2)
        return carry

    lax.fori_loop(0, td // ROW_UNROLL, issue, 0)
    for kk in range(TOP_K):
        pltpu.make_async_copy(m_ref, xs_ref.at[pl.ds(0, td * c), :], sem).wait()


def _dispatch(slot_flat, mrows, c, p_rows, td):
    n = mrows.shape[0] // c
    assert td % ROW_UNROLL == 0
    return pl.pallas_call(
        functools.partial(_dispatch_kernel, c=c),
        grid=(n // td,),
        in_specs=[
            pl.BlockSpec((td * TOP_K,), lambda i: (i,), memory_space=pltpu.SMEM),
            pl.BlockSpec((td * c, V7X_LANES), lambda i: (i, 0)),
        ],
        out_specs=pl.BlockSpec(memory_space=pl.ANY),
        scratch_shapes=[pltpu.SemaphoreType.DMA(())],
        out_shape=jax.ShapeDtypeStruct((p_rows * c, V7X_LANES), F32),
        compiler_params=_cparams(("arbitrary",), has_side_effects=True),
        name="moe_dispatch",
    )(slot_flat, mrows)


def _ffn_kernel(be_ref, nu_ref, x_ref, w1_ref, b1_ref, w2_ref, b2_ref, y_ref, w1s_ref, w2s_ref, *, d_ff):
    i = pl.program_id(0)

    @pl.when((i == 0) | (be_ref[i] != be_ref[jnp.maximum(i - 1, 0)]))
    def _():
        w1s_ref[...] = w1_ref[...].astype(BF16)
        w2s_ref[...] = w2_ref[...].astype(BF16)

    @pl.when(i < nu_ref[0])
    def _():
        x = _load_rows(x_ref, w1_ref.shape[0] // V7X_LANES).astype(BF16)
        hcat = _dot(x, w1s_ref[...]) + b1_ref[...]
        glu = jnp.minimum(hcat[:, :d_ff], SWIGLU_LIMIT)
        lin = jnp.clip(hcat[:, d_ff:], -SWIGLU_LIMIT, SWIGLU_LIMIT)
        act = glu * jax.nn.sigmoid(SWIGLU_ALPHA * glu) * (lin + 1.0)
        _store_rows(y_ref, _dot(act.astype(BF16), w2s_ref[...]) + b2_ref[...])


def _moe_ffn(blk_e, n_used, xrows, w1, b1, w2, b2, bm):
    ne, d, f2 = w1.shape
    d_ff = f2 // 2
    c = d // V7X_LANES
    p_rows = xrows.shape[0] // c
    blk = (bm * c, V7X_LANES)
    grid_spec = pltpu.PrefetchScalarGridSpec(
        num_scalar_prefetch=2,
        grid=(p_rows // bm,),
        in_specs=[
            pl.BlockSpec(blk, lambda i, be, nu: (jnp.minimum(i, nu[0] - 1), 0)),
            pl.BlockSpec((None, d, f2), lambda i, be, nu: (be[i], 0, 0)),
            pl.BlockSpec((None, 1, f2), lambda i, be, nu: (be[i], 0, 0)),
            pl.BlockSpec((None, d_ff, d), lambda i, be, nu: (be[i], 0, 0)),
            pl.BlockSpec((None, 1, d), lambda i, be, nu: (be[i], 0, 0)),
        ],
        out_specs=pl.BlockSpec(blk, lambda i, be, nu: (i, 0)),
        scratch_shapes=[pltpu.VMEM((d, f2), BF16), pltpu.VMEM((d_ff, d), BF16)],
    )
    return pl.pallas_call(
        functools.partial(_ffn_kernel, d_ff=d_ff),
        grid_spec=grid_spec,
        out_shape=jax.ShapeDtypeStruct(xrows.shape, F32),
        compiler_params=_cparams(("arbitrary",)),
        name="moe_ffn",
    )(blk_e, n_used, xrows, w1, b1.reshape(ne, 1, f2), w2, b2.reshape(ne, 1, d))


def _combine_kernel(slot_ref, next_slot_ref, h_ref, gate_ref, p_ref, wp_ref, gp_ref, wg_ref, y_ref, o_ref,
                    buf_ref, h2_ref, sems):
    i = pl.program_id(0)
    tc, d = h_ref.shape
    c = d // V7X_LANES

    def start_gather(slots, half):
        def issue(g, carry):
            for u in range(ROW_UNROLL):
                t = g * ROW_UNROLL + u
                for kk in range(TOP_K):
                    src = slots[t * TOP_K + kk]
                    pltpu.make_async_copy(y_ref.at[_row(src, c), :], buf_ref.at[half, kk, _row(t, c), :],
                                          sems.at[half]).start(priority=kk % 2)
            return carry

        lax.fori_loop(0, tc // ROW_UNROLL, issue, 0)

    cur = i % 2

    @pl.when(i == 0)
    def _():
        start_gather(slot_ref, 0)

    @pl.when(i + 1 < pl.num_programs(0))
    def _():
        start_gather(next_slot_ref, 1 - cur)

    for kk in range(TOP_K):
        pltpu.make_async_copy(y_ref.at[pl.ds(0, tc * c), :], buf_ref.at[cur, kk], sems.at[cur]).wait()

    gate = gate_ref[...]
    for j in range(c):
        cols = slice(j * V7X_LANES, (j + 1) * V7X_LANES)
        acc = h_ref[:, cols]
        for kk in range(TOP_K):
            acc = acc + gate[:, kk:kk + 1] * buf_ref[cur, kk, pl.ds(j, tc, stride=c), :]
        h2_ref[:, cols] = acc

    h2 = h2_ref[...]
    e = _dot(p_ref[...].astype(BF16), wp_ref[...])
    ms = jnp.mean(e * e, axis=-1, keepdims=True)
    e = e * lax.rsqrt(ms + NORM_EPS) * gp_ref[...]
    o_ref[...] = h2 + e * jax.nn.sigmoid(_dot(h2.astype(BF16), wg_ref[...]))


def _combine_ple(slot_flat, h, gates, yrows, p2, wp_bf, gp, wg_bf, tc):
    n, d = h.shape
    pd = p2.shape[1]
    assert tc % ROW_UNROLL == 0
    last = n // tc - 1
    full = lambda i: (0, 0)
    row = lambda i: (i, 0)
    return pl.pallas_call(
        _combine_kernel,
        grid=(n // tc,),
        in_specs=[
            pl.BlockSpec((tc * TOP_K,), lambda i: (i,), memory_space=pltpu.SMEM),
            pl.BlockSpec((tc * TOP_K,), lambda i: (jnp.minimum(i + 1, last),), memory_space=pltpu.SMEM),
            pl.BlockSpec((tc, d), row),
            pl.BlockSpec((tc, TOP_K), row),
            pl.BlockSpec((tc, pd), row),
            pl.BlockSpec((pd, d), full),
            pl.BlockSpec((1, d), full),
            pl.BlockSpec((d, d), full),
            pl.BlockSpec(memory_space=pl.ANY),
        ],
        out_specs=pl.BlockSpec((tc, d), row),
        scratch_shapes=[pltpu.VMEM((2, TOP_K, tc * d // V7X_LANES, V7X_LANES), F32),
                        pltpu.VMEM((tc, d), F32),
                        pltpu.SemaphoreType.DMA((2,))],
        out_shape=jax.ShapeDtypeStruct((n, d), F32),
        compiler_params=_cparams(("arbitrary",)),
        name="moe_combine_ple",
    )(slot_flat, slot_flat, h, gates, p2, wp_bf, gp, wg_bf, yrows)


def _tile(n, want):
    t = min(n, want)
    assert n % t == 0, (n, t)
    return t


def _layer(h3, p3, lambda_init, norm_mix_g, w_in, rw_mu, rw_w0, rw_w2, rw_a0, rw_a2, rw_g2, rw_k_k, rw_k_a, rw_r_k,
           rw_ln_g, rw_ln_b, da_q_norm_g, da_k_norm_g, da_lq1, da_lk1, da_lq2, da_lk2, da_subln_g, w_out,
           norm_ffn_g, w_router, b_router, w1, b1, w2, b2, w_ple, ple_norm_g, w_ple_gate):
    b, t, d = h3.shape
    n = b * t
    c_rw = rw_k_k.shape[0]
    lora = rw_w2.shape[1]
    rw_cols = rw_mu.shape[0]
    rw_heads = rw_r_k.shape[0]
    c_da = w_in.shape[1] - rw_cols
    assert c_da % 3 == 0
    c_da //= 3
    qk_dim = da_q_norm_g.shape[0]
    v_dim = da_subln_g.shape[0]
    da_heads = c_da // v_dim
    assert rw_heads * RW_HEAD == c_rw and c_rw % HEAD_GROUP == 0 and v_dim == V7X_LANES and 2 * qk_dim == v_dim
    assert 2 * lora == V7X_LANES and w_router.shape[1] == N_EXPERTS

    x2 = h3.reshape(n, d)
    row = lambda a: a.reshape(1, -1).astype(F32)

    gq_t = row(jnp.tile(da_q_norm_g, c_da // qk_dim))
    gk_t = row(jnp.tile(da_k_norm_g, c_da // qk_dim))
    gm_qk = _group_matrix(c_da, qk_dim, 1.0 / qk_dim)
    ta = _tile(t, ATTN_BLOCK)
    zrw, q, k, vt = _inproj(x2, row(norm_mix_g), w_in.astype(BF16), gq_t, gk_t, gm_qk, rw_cols, c_da,
                            1.0 / math.sqrt(qk_dim), _tile(ta, 256), b, da_heads, ta)

    zeros = jnp.zeros((lora, c_rw), F32)
    w2bd = jnp.concatenate([jnp.concatenate([rw_w2[0], zeros], axis=1),
                            jnp.concatenate([zeros, rw_w2[1]], axis=1)], axis=0).astype(BF16)
    a2bd = jnp.concatenate([jnp.concatenate([rw_a2[0], zeros], axis=1),
                            jnp.concatenate([zeros, rw_a2[1]], axis=1)], axis=0).astype(BF16)
    gsum = _group_matrix(c_rw, RW_HEAD, 1.0)
    r, vv, nkk, lw, kd, bd, g, bonus = _rwkv_prep(
        zrw.reshape(b, t, rw_cols), row(rw_mu), w2bd, row(rw_w0), a2bd, row(rw_a0), rw_g2.astype(BF16),
        row(rw_k_k), row(rw_k_a), row(rw_r_k), gsum, c_rw, lora, _tile(t, 256))
    yf, yb = _rwkv_scan(r, vv, nkk, lw, kd, bd, _tile(t, 128))

    slopes = 2.0 ** (-8.0 * jnp.arange(1, da_heads + 1, dtype=F32) / da_heads)
    q3 = q.reshape(b, t, c_da)
    k3 = k.reshape(b, t, c_da)
    small = (row(da_lq1), row(da_lk1), row(da_lq2), row(da_lk2), row(da_subln_g))

    def attention_running_max():
        v3 = vt[:, :, :, :v_dim, :].transpose(0, 2, 4, 1, 3).reshape(b, t, c_da)
        return _attention(slopes, q3, k3, v3, *small, da_heads, v_dim, lambda_init, ta, ta)

    score_bound = 1.01 * math.sqrt(qk_dim) * jnp.max(jnp.abs(da_q_norm_g)) * jnp.max(jnp.abs(da_k_norm_g))
    mb = jnp.ceil(2.0 * score_bound.astype(F32)) * 0.5
    o_da = lax.cond(
        mb <= MAX_FIXED_SHIFT,
        lambda: _attention_fixed(mb, q3, k3, vt, *small, da_heads, v_dim, lambda_init, ta),
        attention_running_max)

    gmean = _group_matrix(c_rw, RW_HEAD, 1.0 / RW_HEAD)
    w_out_bf = w_out.astype(BF16)
    h1, m, idx, gates, counts = _post(
        yf.reshape(n, c_rw), yb.reshape(n, c_rw), bonus.reshape(n, c_rw), g.reshape(n, c_rw), o_da.reshape(n, c_da), x2,
        row(rw_ln_g), row(rw_ln_b), gmean, w_out_bf[:c_rw], w_out_bf[c_rw:], row(norm_ffn_g),
        w_router.astype(F32), row(b_router), RW_HEAD * 1e-5, _tile(n, 256))

    bm = MOE_BLOCK_ROWS
    counts = counts.reshape(N_EXPERTS).astype(jnp.int32)
    padded = (counts + bm - 1) // bm * bm
    pend = jnp.cumsum(padded)
    offs = (pend - padded).astype(jnp.int32)
    slot = _moe_rank(idx, offs.astype(F32).reshape(1, N_EXPERTS), _tile(n, 512))
    n_blk = (n * TOP_K) // bm + N_EXPERTS
    blk_start = jnp.arange(n_blk, dtype=jnp.int32) * bm
    blk_e = jnp.minimum(jnp.sum(blk_start[:, None] >= pend[None, :], axis=1), N_EXPERTS - 1).astype(jnp.int32)
    n_used = (pend[-1] // bm).astype(jnp.int32).reshape(1)
    slot_flat = slot.reshape(n * TOP_K)
    xs = _dispatch(slot_flat, m, d // V7X_LANES, n_blk * bm, _tile(n, 256))
    ys = _moe_ffn(blk_e, n_used, xs, w1, b1, w2, b2, bm)
    out = _combine_ple(slot_flat, h1, gates, ys, p3.reshape(n, -1), w_ple.astype(BF16), row(ple_norm_g),
                       w_ple_gate.astype(BF16), _tile(n, 256))
    return out.reshape(b, t, d)


def kernel(x, p, norm_mix_g, w_in, rw_mu, rw_w0, rw_w2, rw_a0, rw_a2, rw_g2, rw_k_k, rw_k_a, rw_r_k, rw_ln_g,
           rw_ln_b, da_q_norm_g, da_k_norm_g, da_lq1, da_lk1, da_lq2, da_lk2, da_subln_g, w_out, norm_ffn_g,
           w_router, b_router, w1, b1, w2, b2, w_ple, ple_norm_g, w_ple_gate):
    h = x.astype(F32)
    params = (norm_mix_g, w_in, rw_mu, rw_w0, rw_w2, rw_a0, rw_a2, rw_g2, rw_k_k, rw_k_a, rw_r_k, rw_ln_g, rw_ln_b,
              da_q_norm_g, da_k_norm_g, da_lq1, da_lk1, da_lq2, da_lk2, da_subln_g, w_out, norm_ffn_g,
              w_router, b_router, w1, b1, w2, b2, w_ple, ple_norm_g, w_ple_gate)
    for i in range(p.shape[0]):
        lambda_init = 0.8 - 0.6 * math.exp(-0.3 * i)
        h = _layer(h, p[i], lambda_init, *(a[i] for a in params))
    return h.astype(x.dtype)
```

```python
import functools
import math

import jax
import jax.numpy as jnp
import numpy as np
from jax import lax
from jax.experimental import pallas as pl
from jax.experimental.pallas import tpu as pltpu

F32 = jnp.float32
BF16 = jnp.bfloat16

V7X_LANES = 128
V7X_VMEM_BYTES = 64 * 1024 * 1024
VMEM_LIMIT = 52 * 1024 * 1024

NORM_EPS = 1e-6
RW_HEAD = 64
CHUNK = 64
HEAD_GROUP = 256
N_EXPERTS = 32
TOP_K = 4
MOE_BLOCK_ROWS = 512
SWIGLU_ALPHA = 1.702
SWIGLU_LIMIT = 7.0


def _cparams(sem, **kw):
    return pltpu.CompilerParams(dimension_semantics=sem, vmem_limit_bytes=VMEM_LIMIT, **kw)


def _dot(a, b):
    return jnp.dot(a, b, preferred_element_type=F32)


def _dot_hilo_lhs(g_bf16, x):
    hi, lo = _split2(x)
    return _dot(g_bf16, hi) + _dot(g_bf16, lo)


def _split2(x):
    hi = x.astype(BF16)
    return hi, (x - hi.astype(F32)).astype(BF16)


def _dot_hilo_rhs(x, g_bf16):
    hi, lo = _split2(x)
    out = _dot(jnp.concatenate([hi, lo], axis=0), g_bf16)
    return out[:x.shape[0]] + out[x.shape[0]:]


def _dot_hilo(a, b):
    ah, al = _split2(a)
    bh, bl = _split2(b)
    return _dot(ah, bh) + (_dot(ah, bl) + _dot(al, bh))


def _store_rows(ref, x2):
    rows, d = x2.shape
    c = d // V7X_LANES
    for j in range(c):
        ref[pl.ds(j, rows, stride=c), :] = x2[:, j * V7X_LANES:(j + 1) * V7X_LANES]


def _load_rows(ref, c):
    rows = ref.shape[0] // c
    return jnp.concatenate([ref[pl.ds(j, rows, stride=c), :] for j in range(c)], axis=1)


def _group_matrix(width, group, value):
    g = np.arange(width) // group
    return jnp.asarray(np.where(g[:, None] == g[None, :], value, 0.0), BF16)


def _inproj_kernel(x_ref, g_ref, w_ref, gq_ref, gk_ref, gm_ref, zrw_ref, q_ref, k_ref, vt_ref, *, rw_cols, c_da, scale):
    x = x_ref[...]
    ms = jnp.mean(x * x, axis=-1, keepdims=True)
    u = (x * lax.rsqrt(ms + NORM_EPS) * g_ref[...]).astype(BF16)
    z = _dot(u, w_ref[...])
    zrw_ref[...] = z[:, :rw_cols]
    zq = z[:, rw_cols:rw_cols + c_da]
    zk = z[:, rw_cols + c_da:rw_cols + 2 * c_da]
    zv = z[:, rw_cols + 2 * c_da:]
    tm = zq.shape[0]
    msqk = _dot(jnp.concatenate([zq * zq, zk * zk], axis=0).astype(BF16), gm_ref[...])
    q_ref[...] = (zq * lax.rsqrt(msqk[:tm] + NORM_EPS) * (gq_ref[...] * scale)).astype(BF16)
    k_ref[...] = (zk * lax.rsqrt(msqk[tm:] + NORM_EPS) * gk_ref[...]).astype(BF16)
    heads, vrows, _ = vt_ref.shape
    v_dim = c_da // heads
    pad_row = lax.broadcasted_iota(jnp.int32, (vrows - v_dim, tm), 0)
    pad = jnp.where(pad_row == 0, 1.0, 0.0).astype(BF16)
    for hd in range(heads):
        vt_ref[hd, 0:v_dim, :] = zv[:, hd * v_dim:(hd + 1) * v_dim].T.astype(BF16)
        vt_ref[hd, v_dim:, :] = pad


def _inproj(x2, g, w_bf, gq_t, gk_t, gm, rw_cols, c_da, scale, tm, b, heads, tb):
    n, d = x2.shape
    cols = w_bf.shape[1]
    t = n // b
    tps = t // tm
    per = tb // tm
    vrows = c_da // heads + ATTN_V_PAD
    full = lambda i: (0, 0)
    row = lambda i: (i, 0)
    return pl.pallas_call(
        functools.partial(_inproj_kernel, rw_cols=rw_cols, c_da=c_da, scale=scale),
        grid=(n // tm,),
        in_specs=[
            pl.BlockSpec((tm, d), row),
            pl.BlockSpec((1, d), full),
            pl.BlockSpec((d, cols), full),
            pl.BlockSpec((1, c_da), full),
            pl.BlockSpec((1, c_da), full),
            pl.BlockSpec((c_da, c_da), full),
        ],
        out_specs=[
            pl.BlockSpec((tm, rw_cols), row),
            pl.BlockSpec((tm, c_da), row),
            pl.BlockSpec((tm, c_da), row),
            pl.BlockSpec((None, heads, None, vrows, tm),
                         lambda i: (i // tps, 0, (i % tps) // per, 0, (i % tps) % per)),
        ],
        out_shape=[
            jax.ShapeDtypeStruct((n, rw_cols), F32),
            jax.ShapeDtypeStruct((n, c_da), BF16),
            jax.ShapeDtypeStruct((n, c_da), BF16),
            jax.ShapeDtypeStruct((b, heads, t // tb, vrows, tb), BF16),
        ],
        compiler_params=_cparams(("parallel",)),
        name="inproj",
    )(x2, g, w_bf, gq_t, gk_t, gm)


def _rwkv_prep_kernel(z_ref, zp_ref, zn_ref, mu_ref, w2_ref, w0_ref, a2_ref, a0_ref, g2_ref, kk_ref, ka_ref,
                      rk_ref, gsum_ref,
                      r_ref, v_ref, nkk_ref, lw_ref, kd_ref, bd_ref, g_ref, bonus_ref, *, c_rw, lora):
    i = pl.program_id(1)
    nt = pl.num_programs(1)
    z = z_ref[...]
    tt = z.shape[0]
    rows = lax.broadcasted_iota(jnp.int32, z.shape, 0)
    prev_row = jnp.where(i == 0, 0.0, zp_ref[7:8, :])
    next_row = jnp.where(i == nt - 1, 0.0, zn_ref[0:1, :])
    zprev = jnp.where(rows == 0, prev_row, pltpu.roll(z, 1, 0))
    znext = jnp.where(rows == tt - 1, next_row, pltpu.roll(z, tt - 1, 0))
    z = z + (0.5 * (zprev + znext) - z) * mu_ref[...]

    r = z[:, 0:c_rw]
    k = z[:, c_rw:2 * c_rw]
    v = z[:, 2 * c_rw:3 * c_rw]
    o = 3 * c_rw
    wd = z[:, o:o + 2 * lora]
    ad = z[:, o + 2 * lora:o + 4 * lora]
    gd = z[:, o + 4 * lora:]

    xw = w0_ref[...] + _dot(jnp.tanh(wd).astype(BF16), w2_ref[...])
    xa = a0_ref[...] + _dot(ad.astype(BF16), a2_ref[...])
    lw = -math.exp(-0.5) * jax.nn.sigmoid(xw)
    rate = jax.nn.sigmoid(xa)
    g = _dot(jax.nn.sigmoid(gd).astype(BF16), g2_ref[...])

    gsum = gsum_ref[...]
    kkr = k * kk_ref[...]
    ka = ka_ref[...]
    kd0 = k * (1.0 + (rate[:, :c_rw] - 1.0) * ka)
    kd1 = k * (1.0 + (rate[:, c_rw:] - 1.0) * ka)
    sums = _dot_hilo_rhs(jnp.concatenate([kkr * kkr, r * (0.5 * (kd0 + kd1)) * rk_ref[...]], axis=0), gsum)
    kk = kkr * lax.rsqrt(sums[:tt] + 1e-12)
    bonus = sums[tt:] * v

    r_ref[...] = r.astype(BF16)
    v_ref[...] = v.astype(BF16)
    nkk_ref[...] = (-kk).astype(BF16)
    lw_ref[0] = lw[:, :c_rw]
    lw_ref[1] = lw[:, c_rw:]
    kd_ref[0] = kd0.astype(BF16)
    kd_ref[1] = kd1.astype(BF16)
    bd_ref[0] = (kk * rate[:, :c_rw]).astype(BF16)
    bd_ref[1] = (kk * rate[:, c_rw:]).astype(BF16)
    g_ref[...] = g
    bonus_ref[...] = bonus


def _rwkv_prep(zrw3, mu, w2bd, w0f, a2bd, a0f, g2_bf, k_k, k_a, r_kf, gsum, c_rw, lora, tt):
    b, t, cols = zrw3.shape
    nt = t // tt
    hb = tt // 8
    nhb = t // 8
    full = lambda bi, i: (0, 0)
    tile = lambda bi, i: (bi, i, 0)
    dtile = lambda bi, i: (0, bi, i, 0)
    one = jax.ShapeDtypeStruct((b, t, c_rw), F32)
    two = jax.ShapeDtypeStruct((2, b, t, c_rw), F32)
    one16 = jax.ShapeDtypeStruct((b, t, c_rw), BF16)
    two16 = jax.ShapeDtypeStruct((2, b, t, c_rw), BF16)
    return pl.pallas_call(
        functools.partial(_rwkv_prep_kernel, c_rw=c_rw, lora=lora),
        grid=(b, nt),
        in_specs=[
            pl.BlockSpec((None, tt, cols), tile),
            pl.BlockSpec((None, 8, cols), lambda bi, i: (bi, jnp.maximum(i * hb - 1, 0), 0)),
            pl.BlockSpec((None, 8, cols), lambda bi, i: (bi, jnp.minimum((i + 1) * hb, nhb - 1), 0)),
            pl.BlockSpec((1, cols), full),
            pl.BlockSpec(w2bd.shape, full),
            pl.BlockSpec((1, 2 * c_rw), full),
            pl.BlockSpec(a2bd.shape, full),
            pl.BlockSpec((1, 2 * c_rw), full),
            pl.BlockSpec(g2_bf.shape, full),
            pl.BlockSpec((1, c_rw), full),
            pl.BlockSpec((1, c_rw), full),
            pl.BlockSpec((1, c_rw), full),
            pl.BlockSpec((c_rw, c_rw), full),
        ],
        out_specs=[
            pl.BlockSpec((None, tt, c_rw), tile),
            pl.BlockSpec((None, tt, c_rw), tile),
            pl.BlockSpec((None, tt, c_rw), tile),
            pl.BlockSpec((2, None, tt, c_rw), dtile),
            pl.BlockSpec((2, None, tt, c_rw), dtile),
            pl.BlockSpec((2, None, tt, c_rw), dtile),
            pl.BlockSpec((None, tt, c_rw), tile),
            pl.BlockSpec((None, tt, c_rw), tile),
        ],
        out_shape=[one16, one16, one16, two, two16, two16, one, one],
        compiler_params=_cparams(("parallel", "parallel")),
        name="rwkv_prep",
    )(zrw3, zrw3, zrw3, mu, w2bd, w0f, a2bd, a0f, g2_bf, k_k, k_a, r_kf, gsum)


M_STRICT, M_INCL, M_EYE, M_PAIR, M_LEVEL = 0, 1, 2, 3, 4
N_LEVELS = 5
SCAN_SEQS = 2


def _scan_tables(w):
    c = CHUNK
    masks = []
    mcums = []
    for d in range(2):
        t = np.broadcast_to(np.arange(c)[:, None], (c, w))
        s = np.broadcast_to((np.arange(w) % RW_HEAD)[None, :], (c, w))
        tt = np.broadcast_to(np.arange(c)[:, None], (c, c))
        ss = np.broadcast_to(np.arange(c)[None, :], (c, c))
        if d == 1:
            t, s, tt, ss = c - 1 - t, c - 1 - s, c - 1 - tt, c - 1 - ss
        rows = [s < t, s <= t, s == t, (t // 2 == s // 2) & (t > s)]
        sz = 2
        while sz < c:
            rows.append((t // (2 * sz) == s // (2 * sz)) & ((t // sz) % 2 == 1) & ((s // sz) % 2 == 0))
            sz *= 2
        assert len(rows) == M_LEVEL + N_LEVELS
        masks.append(np.stack(rows).astype(np.float32))
        mcums.append((ss <= tt).astype(np.float32))
    return jnp.asarray(np.stack(masks), F32), jnp.asarray(np.stack(mcums), BF16)


def _block_diag(y, bd16):
    g = y.shape[1] // RW_HEAD
    return jnp.concatenate([y.astype(BF16)] * g, axis=0) * bd16


def _bmm(x, y, bd16):
    outs = []
    for s in range(0, y.shape[1], HEAD_GROUP):
        outs.append(_dot(x[:, s:s + HEAD_GROUP].astype(BF16), _block_diag(y[:, s:s + HEAD_GROUP], bd16)))
    return jnp.concatenate(outs, axis=1)


def _bmm_nt(x, y, bd16):
    outs = []
    for s in range(0, y.shape[1], HEAD_GROUP):
        bd = _block_diag(y[:, s:s + HEAD_GROUP], bd16)
        outs.append(lax.dot_general(x[:, s:s + HEAD_GROUP].astype(BF16), bd, (((1,), (1,)), ((), ())),
                                    preferred_element_type=F32))
    return jnp.concatenate(outs, axis=1)


def _bmm_tn(x, y, bd32):
    outs = []
    for s in range(0, y.shape[1], HEAD_GROUP):
        xt = x[:, s:s + HEAD_GROUP].T.astype(BF16)
        full = _dot(xt, y[:, s:s + HEAD_GROUP].astype(BF16)) * bd32
        o = full[0:RW_HEAD]
        for j in range(1, HEAD_GROUP // RW_HEAD):
            o = o + full[j * RW_HEAD:(j + 1) * RW_HEAD]
        outs.append(o)
    return jnp.concatenate(outs, axis=1)


def _scan_chunks(r, lw, k, v, a, b, state, masks, mcum, bd16, bd32):
    c = r[0].shape[0]

    def each(f, *cols):
        return [f(*xs) for xs in zip(*cols)]

    cum = each(_dot_hilo_lhs, mcum, lw)
    tot = each(lambda x: jnp.sum(x, axis=0, keepdims=True), lw)
    e_neg = each(lambda x: jnp.exp(-x), cum)
    e_tail = each(lambda t_, x: jnp.exp(t_ - x), tot, cum)
    rt = each(lambda x, cm: x * jnp.exp(cm), r, cum)
    at = each(lambda x, cm, l_: x * jnp.exp(cm - l_), a, cum, lw)
    bt = each(jnp.multiply, b, e_neg)
    kt = each(jnp.multiply, k, e_neg)
    bh = each(jnp.multiply, b, e_tail)
    kh = each(jnp.multiply, k, e_tail)

    ar = each(lambda x, y_: jnp.concatenate([x, y_], axis=0), at, rt)
    pb = each(lambda x, y_: _bmm_nt(x, y_, bd16), ar, bt)
    pk = each(lambda x, y_: _bmm_nt(x, y_, bd16), ar, kt)
    a_ab = each(lambda x, m: x[:c] * m[M_STRICT], pb, masks)
    a_rb = each(lambda x, m: x[c:] * m[M_INCL], pb, masks)
    a_ak = each(lambda x, m: x[:c] * m[M_STRICT], pk, masks)
    a_rk = each(lambda x, m: x[c:] * m[M_INCL], pk, masks)

    tinv = each(lambda x, m: m[M_EYE] + x * m[M_PAIR], a_ab, masks)
    for lvl in range(N_LEVELS):
        x = each(lambda aa, m, ti: _bmm(aa * m[M_LEVEL + lvl], ti, bd16), a_ab, masks, tinv)
        tinv = each(lambda ti, x_: ti + _bmm(ti, x_, bd16), tinv, x)

    bmm = lambda p, q: _bmm(p, q, bd16)
    stack = lambda p, q: jnp.concatenate([p, q], axis=0)
    att = each(bmm, tinv, at)
    akv = each(lambda p, p2, q: _bmm(stack(p, p2), q, bd16), a_ak, a_rk, v)
    vt = each(lambda ti, x_: _bmm(ti, x_[:c], bd16), tinv, akv)
    rh = each(lambda x, p, q: x + _bmm(p, q, bd16), rt, a_rb, att)
    y_in = each(lambda p, q, x_: _bmm(p, q, bd16) + x_[c:], a_rb, vt, akv)
    mc = each(lambda m, t_, p, q: m[M_EYE] * jnp.exp(t_) + _bmm_tn(p, q, bd32), masks, tot, bh, att)
    nc = each(lambda p, p2, q, q2: _bmm_tn(stack(p, p2), stack(q, q2), bd32), bh, kh, vt, v)
    onst = each(lambda p, p2, q: _bmm(stack(p, p2), q, bd16), rh, mc, state)
    y = each(lambda x_, yi: x_[:c] + yi, onst, y_in)
    new_state = each(lambda x_, n_: x_[c:] + n_, onst, nc)
    return y, new_state


def _rwkv_scan_kernel(rf_ref, vf_ref, af_ref, rb_ref, vb_ref, ab_ref, lwf_ref, kf_ref, bf_ref, lwb_ref, kb_ref,
                      bb_ref, mask_ref, mcum_ref, bd16_ref, bd32_ref, yf_ref, yb_ref, state_ref, *, n_chunks):
    @pl.when(pl.program_id(1) == 0)
    def _():
        state_ref[...] = jnp.zeros_like(state_ref)

    dirs = ((0, rf_ref, lwf_ref, kf_ref, vf_ref, af_ref, bf_ref, yf_ref),
            (1, rb_ref, lwb_ref, kb_ref, vb_ref, ab_ref, bb_ref, yb_ref))
    n_seq = rf_ref.shape[0]

    def body(j, carry):
        cols = [[] for _ in range(9)]
        where = []
        for d, r_ref, lw_ref, k_ref, v_ref, a_ref, b_ref, y_ref in dirs:
            cj = j if d == 0 else n_chunks - 1 - j
            sl = pl.ds(pl.multiple_of(cj * CHUNK, CHUNK), CHUNK)
            for g in range(n_seq):
                vals = (r_ref[g, sl, :], lw_ref[g, sl, :], k_ref[g, sl, :], v_ref[g, sl, :], a_ref[g, sl, :],
                        b_ref[g, sl, :], state_ref[d, g], mask_ref.at[d], mcum_ref[d])
                for col, val in zip(cols, vals):
                    col.append(val)
                where.append((y_ref, d, g, sl))
        ys, sts = _scan_chunks(*cols, bd16_ref[...], bd32_ref[...])
        for (y_ref, d, g, sl), y, st in zip(where, ys, sts):
            y_ref[g, sl, :] = y
            state_ref[d, g] = st
        return carry

    lax.fori_loop(0, n_chunks, body, 0)


def _rwkv_scan(r, v, nkk, lw, kd, bd, tb):
    b, t, w = r.shape
    nb = t // tb
    g = SCAN_SEQS if b % SCAN_SEQS == 0 else 1
    masks, mcum = _scan_tables(w)
    bd32 = _group_matrix(HEAD_GROUP, RW_HEAD, 1.0).astype(F32)
    bd16 = bd32.astype(BF16)
    fwd = lambda bi, i: (bi, i, 0)
    bwd = lambda bi, i: (bi, nb - 1 - i, 0)
    fwd_d = lambda bi, i: (0, bi, i, 0)
    bwd_d = lambda bi, i: (1, bi, nb - 1 - i, 0)
    const = lambda nd: (lambda bi, i: (0,) * nd)
    blk = (g, tb, w)
    dblk = (None, g, tb, w)
    out = jax.ShapeDtypeStruct((b, t, w), F32)
    return pl.pallas_call(
        functools.partial(_rwkv_scan_kernel, n_chunks=tb // CHUNK),
        grid=(b // g, nb),
        in_specs=[
            pl.BlockSpec(blk, fwd), pl.BlockSpec(blk, fwd), pl.BlockSpec(blk, fwd),
            pl.BlockSpec(blk, bwd), pl.BlockSpec(blk, bwd), pl.BlockSpec(blk, bwd),
            pl.BlockSpec(dblk, fwd_d), pl.BlockSpec(dblk, fwd_d), pl.BlockSpec(dblk, fwd_d),
            pl.BlockSpec(dblk, bwd_d), pl.BlockSpec(dblk, bwd_d), pl.BlockSpec(dblk, bwd_d),
            pl.BlockSpec(masks.shape, const(4)),
            pl.BlockSpec(mcum.shape, const(3)),
            pl.BlockSpec(bd16.shape, const(2)),
            pl.BlockSpec(bd32.shape, const(2)),
        ],
        out_specs=[pl.BlockSpec(blk, fwd), pl.BlockSpec(blk, bwd)],
        out_shape=[out, out],
        scratch_shapes=[pltpu.VMEM((2, g, CHUNK, w), F32)],
        compiler_params=_cparams(("parallel", "arbitrary")),
        name="rwkv_scan",
    )(r, v, nkk, r, v, nkk, lw, kd, bd, lw, kd, bd, masks, mcum, bd16, bd32)


def _attn_kernel(slope_ref, q_ref, k_ref, v_ref, lq1_ref, lk1_ref, lq2_ref, lk2_ref, sg_ref, o_ref,
                 m_ref, l_ref, acc_ref, *, qk_dim, lambda_init):
    h = pl.program_id(1)
    i = pl.program_id(2)
    j = pl.program_id(3)
    nk = pl.num_programs(3)
    tq = q_ref.shape[0]
    tk = k_ref.shape[0]

    @pl.when(j == 0)
    def _():
        m_ref[...] = jnp.full(m_ref.shape, -jnp.inf, F32)
        l_ref[...] = jnp.zeros_like(l_ref)
        acc_ref[...] = jnp.zeros_like(acc_ref)

    q = q_ref[...]
    k = k_ref[...]
    v = v_ref[...]
    lane = lax.broadcasted_iota(jnp.int32, q.shape, 1)
    zero = jnp.zeros_like(q)
    qi = lax.broadcasted_iota(jnp.int32, (tq, tk), 0) + i * tq
    kj = lax.broadcasted_iota(jnp.int32, (tq, tk), 1) + j * tk
    bias = slope_ref[h] * jnp.abs(qi - kj).astype(F32)
    for c in range(2):
        qc = jnp.where((lane < qk_dim) == (c == 0), q, zero)
        s = lax.dot_general(qc, k, (((1,), (1,)), ((), ())), preferred_element_type=F32) - bias
        m_old = m_ref[c]
        m_new = jnp.maximum(m_old, jnp.max(s, axis=-1, keepdims=True))
        alpha = jnp.exp(m_old - m_new)
        p = jnp.exp(s - m_new)
        l_ref[c] = alpha * l_ref[c] + jnp.sum(p, axis=-1, keepdims=True)
        acc_ref[c] = alpha * acc_ref[c] + _dot(p.astype(BF16), v)
        m_ref[c] = m_new

    @pl.when(j == nk - 1)
    def _():
        lam = (jnp.exp(jnp.sum(lq1_ref[...] * lk1_ref[...], axis=-1, keepdims=True))
               - jnp.exp(jnp.sum(lq2_ref[...] * lk2_ref[...], axis=-1, keepdims=True)) + lambda_init)
        o = acc_ref[0] / l_ref[0] - lam * (acc_ref[1] / l_ref[1])
        ms = jnp.mean(o * o, axis=-1, keepdims=True)
        o_ref[...] = o * lax.rsqrt(ms + NORM_EPS) * (sg_ref[...] * (1.0 - lambda_init))


def _attention(slopes, q3, k3, v3, lq1, lk1, lq2, lk2, subln_g, heads, v_dim, lambda_init, tq, tk):
    b, t, _ = q3.shape
    small = lambda s, bi, h, i, j: (0, 0)
    grid_spec = pltpu.PrefetchScalarGridSpec(
        num_scalar_prefetch=1,
        grid=(b, heads, t // tq, t // tk),
        in_specs=[
            pl.BlockSpec((None, tq, v_dim), lambda bi, h, i, j, s: (bi, i, h)),
            pl.BlockSpec((None, tk, v_dim), lambda bi, h, i, j, s: (bi, j, h)),
            pl.BlockSpec((None, tk, v_dim), lambda bi, h, i, j, s: (bi, j, h)),
            pl.BlockSpec(lq1.shape, lambda bi, h, i, j, s: (0, 0)),
            pl.BlockSpec(lk1.shape, lambda bi, h, i, j, s: (0, 0)),
            pl.BlockSpec(lq2.shape, lambda bi, h, i, j, s: (0, 0)),
            pl.BlockSpec(lk2.shape, lambda bi, h, i, j, s: (0, 0)),
            pl.BlockSpec(subln_g.shape, lambda bi, h, i, j, s: (0, 0)),
        ],
        out_specs=pl.BlockSpec((None, tq, v_dim), lambda bi, h, i, j, s: (bi, i, h)),
        scratch_shapes=[
            pltpu.VMEM((2, tq, 1), F32),
            pltpu.VMEM((2, tq, 1), F32),
            pltpu.VMEM((2, tq, v_dim), F32),
        ],
    )
    del small
    return pl.pallas_call(
        functools.partial(_attn_kernel, qk_dim=v_dim // 2, lambda_init=lambda_init),
        grid_spec=grid_spec,
        out_shape=jax.ShapeDtypeStruct((b, t, heads * v_dim), F32),
        compiler_params=_cparams(("parallel", "parallel", "parallel", "arbitrary")),
        name="attn",
    )(slopes, q3, k3, v3, lq1, lk1, lq2, lk2, subln_g)


EXP_ZERO_ARG = 104.0
MAX_FIXED_SHIFT = 40.0
N_AUG = 6
ATTN_BLOCK = 512
ATTN_BLOCKS_PER_STEP = 2
ATTN_KEY_SPLIT = 2
ATTN_V_PAD = 16


def _attn_fixed_kernel(w_ref, slope_ref, mb_ref, q_ref, k_ref, vt_ref, dbias_ref, qaug_ref, kaug_ref,
                       lq1_ref, lk1_ref, lq2_ref, lk2_ref, sg_ref, o_ref, acc_ref, *, qk_dim, lambda_init):
    h = pl.program_id(1)
    i = pl.program_id(2)
    tb = q_ref.shape[0]
    nk = k_ref.shape[0] // tb
    v_dim = 2 * qk_dim
    slope = slope_ref[h]
    mb = mb_ref[0]
    w = w_ref[h]
    acc_ref[...] = jnp.zeros_like(acc_ref)

    q = q_ref[...]
    lane = lax.broadcasted_iota(jnp.int32, q.shape, 1)
    lane_row = lax.broadcasted_iota(jnp.int32, (1, q.shape[1]), 1)
    is_c0 = lane < qk_dim

    def blocks(items, qvar, kvar, diag):
        sub = tb // ATTN_KEY_SPLIT
        datas = (is_c0, jnp.logical_not(is_c0))
        qas = [jnp.where(datas[c], q, qaug_ref[qvar, c]) for c in range(2)]
        chains = []
        for n, (j, delta) in enumerate(items):
            kblk = k_ref[pl.ds(pl.multiple_of(j * tb, tb), tb), :]
            vtblk = vt_ref[j]
            for c in range(2):
                base = qk_dim * (1 - c)
                dyn = jnp.where(lane_row == base + 4, -mb,
                                jnp.where(lane_row == base + 5, -(slope * tb) * delta, 0.0)).astype(BF16)
                ka = jnp.where(datas[c], kblk, kaug_ref[kvar, c] + dyn)
                for u in range(ATTN_KEY_SPLIT):
                    rows = slice(u * sub, (u + 1) * sub)
                    chains.append((c, u, qas[c], ka[rows], vtblk[:, rows]))
        sts = [lax.dot_general(ka, qa, (((1,), (1,)), ((), ())), preferred_element_type=F32)
               for _, _, qa, ka, _ in chains]
        if diag:
            sts = [st - dbias_ref[u * sub:(u + 1) * sub, :] for st, (_, u, _, _, _) in zip(sts, chains)]
        pts = [jnp.exp(st).astype(BF16) for st in sts]
        for c in range(2):
            upd = None
            for pt, (cc, _, _, _, vsub) in zip(pts, chains):
                if cc == c:
                    part = _dot(vsub, pt)
                    upd = part if upd is None else upd + part
            acc_ref[c] += upd

    def span(first, count, qvar, kvar, dist):
        per = ATTN_BLOCKS_PER_STEP

        def group(g, carry):
            j0 = first + g * per
            blocks([(j0 + n, dist(j0 + n)) for n in range(per)], qvar, kvar, False)
            return carry

        lax.fori_loop(0, count // per, group, 0)

        def single(j, carry):
            blocks([(j, dist(j))], qvar, kvar, False)
            return carry

        lax.fori_loop(first + (count // per) * per, first + count, single, 0)

    lo = jnp.maximum(i - w, 0)
    hi = jnp.minimum(i + w, nk - 1)
    span(lo, i - lo, 0, 0, lambda j: (i - j).astype(F32))
    blocks([(i, jnp.zeros((), F32))], 2, 0, True)
    span(i + 1, hi - i, 1, 1, lambda j: (j - i).astype(F32))

    lam = (jnp.exp(jnp.sum(lq1_ref[...] * lk1_ref[...], axis=-1, keepdims=True))
           - jnp.exp(jnp.sum(lq2_ref[...] * lk2_ref[...], axis=-1, keepdims=True)) + lambda_init)
    a0 = acc_ref[0]
    a1 = acc_ref[1]
    ot = a0[:v_dim] / a0[v_dim:v_dim + 1] - lam * (a1[:v_dim] / a1[v_dim:v_dim + 1])
    ms = jnp.mean(ot * ot, axis=0, keepdims=True)
    o_ref[...] = (ot * lax.rsqrt(ms + NORM_EPS) * (sg_ref[...] * (1.0 - lambda_init))).T


def _attn_aug_tables(slopes, tb, qk_dim):
    heads = len(slopes)
    lanes = 2 * qk_dim
    pos = np.arange(tb)
    qaug = np.zeros((heads, 3, 2, tb, lanes), np.float32)
    kaug = np.zeros((heads, 2, 2, tb, lanes), np.float32)
    for hd, sl in enumerate(slopes):
        for var in range(3):
            rr = pos if var != 1 else tb - 1 - pos
            hi = (rr // 16) * (16.0 * sl)
            lo = (rr % 16) * sl
            for c in range(2):
                base = qk_dim * (1 - c)
                if var == 2:
                    qaug[hd, var, c, :, base + 4] = 1.0
                    continue
                qaug[hd, var, c, :, base + 0] = -hi
                qaug[hd, var, c, :, base + 1] = -lo
                qaug[hd, var, c, :, base + 2:base + N_AUG] = 1.0
                kaug[hd, var, c, :, base + 0:base + 2] = 1.0
                kaug[hd, var, c, :, base + 2] = hi
                kaug[hd, var, c, :, base + 3] = lo
    return jnp.asarray(qaug, BF16), jnp.asarray(kaug, BF16)


def _attention_fixed(mb, q3, k3, vt, lq1, lk1, lq2, lk2, subln_g, heads, v_dim, lambda_init, tb):
    b, t, _ = q3.shape
    qk_dim = v_dim // 2
    slopes = [2.0 ** (-8.0 * (hd + 1) / heads) for hd in range(heads)]
    widths = [int(math.ceil((EXP_ZERO_ARG / s - 1.0) / tb)) for s in slopes]
    qaug, kaug = _attn_aug_tables(slopes, tb, qk_dim)
    pos = np.arange(tb)
    dist = np.abs(pos[:, None] - pos[None, :]).astype(np.float32)
    dbias = jnp.asarray(np.stack([dist * s for s in slopes]), F32)
    nb = t // tb
    vrows = v_dim + ATTN_V_PAD
    assert vt.shape == (b, heads, nb, vrows, tb)
    const = lambda bi, h, i, *_: (0, 0)
    grid_spec = pltpu.PrefetchScalarGridSpec(
        num_scalar_prefetch=3,
        grid=(b, heads, t // tb),
        in_specs=[
            pl.BlockSpec((None, tb, v_dim), lambda bi, h, i, *_: (bi, i, h)),
            pl.BlockSpec((None, t, v_dim), lambda bi, h, i, *_: (bi, 0, h)),
            pl.BlockSpec((None, None, nb, vrows, tb), lambda bi, h, i, *_: (bi, h, 0, 0, 0)),
            pl.BlockSpec((None, tb, tb), lambda bi, h, i, *_: (h, 0, 0)),
            pl.BlockSpec((None, 3, 2, tb, v_dim), lambda bi, h, i, *_: (h, 0, 0, 0, 0)),
            pl.BlockSpec((None, 2, 2, tb, v_dim), lambda bi, h, i, *_: (h, 0, 0, 0, 0)),
            pl.BlockSpec(lq1.shape, const),
            pl.BlockSpec(lk1.shape, const),
            pl.BlockSpec(lq2.shape, const),
            pl.BlockSpec(lk2.shape, const),
            pl.BlockSpec((v_dim, 1), const),
        ],
        out_specs=pl.BlockSpec((None, tb, v_dim), lambda bi, h, i, *_: (bi, i, h)),
        scratch_shapes=[pltpu.VMEM((2, vrows, tb), F32)],
    )
    return pl.pallas_call(
        functools.partial(_attn_fixed_kernel, qk_dim=qk_dim, lambda_init=lambda_init),
        grid_spec=grid_spec,
        out_shape=jax.ShapeDtypeStruct((b, t, heads * v_dim), F32),
        compiler_params=_cparams(("parallel", "parallel", "parallel")),
        name="attn_fixed",
    )(jnp.asarray(widths, jnp.int32), jnp.asarray(slopes, F32), mb.reshape(1).astype(F32),
      q3, k3, vt, dbias, qaug, kaug, lq1, lk1, lq2, lk2, subln_g.reshape(v_dim, 1))


def _post_kernel(yf_ref, yb_ref, bonus_ref, g_ref, oda_ref, x_ref, lng_ref, lnb_ref, gmean_ref, wo_rw_ref, wo_da_ref,
                 gffn_ref, wr_ref, br_ref, h_ref, m_ref, idx_ref, gate_ref, cnt_ref, *, gn_eps):
    @pl.when(pl.program_id(0) == 0)
    def _():
        cnt_ref[...] = jnp.zeros_like(cnt_ref)

    gmean = gmean_ref[...]
    y = yf_ref[...] + yb_ref[...]
    mean = _dot_hilo_rhs(y, gmean)
    yc = y - mean
    var = _dot_hilo_rhs(yc * yc, gmean)
    yn = yc * lax.rsqrt(var + gn_eps) * lng_ref[...] + lnb_ref[...]
    o_rw = (yn + bonus_ref[...]) * g_ref[...]
    h = x_ref[...] + _dot(o_rw.astype(BF16), wo_rw_ref[...]) + _dot(oda_ref[...].astype(BF16), wo_da_ref[...])
    h_ref[...] = h
    ms = jnp.mean(h * h, axis=-1, keepdims=True)
    m = h * lax.rsqrt(ms + NORM_EPS) * gffn_ref[...]
    _store_rows(m_ref, m)

    logits = _dot_hilo(m, wr_ref[...]) + br_ref[...]
    ne = logits.shape[1]
    lane = lax.broadcasted_iota(jnp.int32, logits.shape, 1)
    kcol = lax.broadcasted_iota(jnp.int32, (logits.shape[0], TOP_K), 1)
    idx_all = jnp.zeros((logits.shape[0], TOP_K), jnp.int32)
    val_all = jnp.zeros((logits.shape[0], TOP_K), F32)
    work = logits
    chosen = jnp.zeros(logits.shape, F32)
    for kk in range(TOP_K):
        mx = jnp.max(work, axis=-1, keepdims=True)
        sel = jnp.min(jnp.where(work == mx, lane, ne), axis=-1, keepdims=True)
        idx_all = jnp.where(kcol == kk, sel, idx_all)
        val_all = jnp.where(kcol == kk, mx, val_all)
        hit = lane == sel
        chosen = jnp.where(hit, 1.0, chosen)
        work = jnp.where(hit, -jnp.inf, work)
    e = jnp.exp(val_all - jnp.max(val_all, axis=-1, keepdims=True))
    idx_ref[...] = idx_all
    gate_ref[...] = e / jnp.sum(e, axis=-1, keepdims=True)
    cnt_ref[...] += jnp.sum(chosen, axis=0, keepdims=True)


def _post(yf, yb, bonus, g, oda, x2, lng, lnb, gmean, wo_rw, wo_da, gffn, wr, br, gn_eps, tm):
    n, c = yf.shape
    d = x2.shape[1]
    ne = wr.shape[1]
    full = lambda i: (0, 0)
    row = lambda i: (i, 0)
    return pl.pallas_call(
        functools.partial(_post_kernel, gn_eps=gn_eps),
        grid=(n // tm,),
        in_specs=[
            pl.BlockSpec((tm, c), row),
            pl.BlockSpec((tm, c), row),
            pl.BlockSpec((tm, c), row),
            pl.BlockSpec((tm, c), row),
            pl.BlockSpec((tm, c), row),
            pl.BlockSpec((tm, d), row),
            pl.BlockSpec((1, c), full),
            pl.BlockSpec((1, c), full),
            pl.BlockSpec((c, c), full),
            pl.BlockSpec((c, d), full),
            pl.BlockSpec((c, d), full),
            pl.BlockSpec((1, d), full),
            pl.BlockSpec((d, ne), full),
            pl.BlockSpec((1, ne), full),
        ],
        out_specs=[
            pl.BlockSpec((tm, d), row),
            pl.BlockSpec((tm * d // V7X_LANES, V7X_LANES), row),
            pl.BlockSpec((tm, TOP_K), row),
            pl.BlockSpec((tm, TOP_K), row),
            pl.BlockSpec((1, ne), full),
        ],
        out_shape=[
            jax.ShapeDtypeStruct((n, d), F32),
            jax.ShapeDtypeStruct((n * d // V7X_LANES, V7X_LANES), F32),
            jax.ShapeDtypeStruct((n, TOP_K), jnp.int32),
            jax.ShapeDtypeStruct((n, TOP_K), F32),
            jax.ShapeDtypeStruct((1, ne), F32),
        ],
        compiler_params=_cparams(("arbitrary",)),
        name="post",
    )(yf, yb, bonus, g, oda, x2, lng, lnb, gmean, wo_rw, wo_da, gffn, wr, br)


def _rank_kernel(idx_ref, ltri_ref, offs_ref, slot_ref, carry_ref):
    @pl.when(pl.program_id(0) == 0)
    def _():
        carry_ref[...] = offs_ref[...]

    idx = idx_ref[...]
    tr = idx.shape[0]
    lane = lax.broadcasted_iota(jnp.int32, (tr, N_EXPERTS), 1)
    hots = [jnp.where(lane == idx[:, kk:kk + 1], 1.0, 0.0) for kk in range(TOP_K)]
    mask = hots[0]
    for kk in range(1, TOP_K):
        mask = mask + hots[kk]
    before = _dot(ltri_ref[...], mask.astype(BF16)) + carry_ref[...]
    kcol = lax.broadcasted_iota(jnp.int32, (tr, TOP_K), 1)
    slot = jnp.zeros((tr, TOP_K), F32)
    for kk in range(TOP_K):
        rk = jnp.sum(hots[kk] * before, axis=-1, keepdims=True)
        slot = jnp.where(kcol == kk, rk, slot)
    slot_ref[...] = slot.astype(jnp.int32)
    carry_ref[...] = carry_ref[...] + jnp.sum(mask, axis=0, keepdims=True)


def _moe_rank(idx, offs_row, tr):
    n = idx.shape[0]
    ltri = jnp.asarray(np.tril(np.ones((tr, tr), np.float32), -1), BF16)
    return pl.pallas_call(
        _rank_kernel,
        grid=(n // tr,),
        in_specs=[pl.BlockSpec((tr, TOP_K), lambda i: (i, 0)), pl.BlockSpec((tr, tr), lambda i: (0, 0)),
                  pl.BlockSpec((1, N_EXPERTS), lambda i: (0, 0))],
        out_specs=pl.BlockSpec((tr, TOP_K), lambda i: (i, 0)),
        out_shape=jax.ShapeDtypeStruct((n, TOP_K), jnp.int32),
        scratch_shapes=[pltpu.VMEM((1, N_EXPERTS), F32)],
        compiler_params=_cparams(("arbitrary",)),
        name="moe_rank",
    )(idx, ltri, offs_row)


ROW_UNROLL = 8


def _row(r, c):
    return pl.ds(pl.multiple_of(r * c, c), c)


def _dispatch_kernel(slot_ref, m_ref, xs_ref, sem, *, c):
    td = m_ref.shape[0] // c

    def issue(g, carry):
        for u in range(ROW_UNROLL):
            t = g * ROW_UNROLL + u
            for kk in range(TOP_K):
                dest = slot_ref[t * TOP_K + kk]
                pltpu.make_async_copy(m_ref.at[_row(t, c), :], xs_ref.at[_row(dest, c), :], sem).start(
                    priority=kk % 2)
        return carry

    lax.fori_loop(0, td // ROW_UNROLL, issue, 0)
    for kk in range(TOP_K):
        pltpu.make_async_copy(m_ref, xs_ref.at[pl.ds(0, td * c), :], sem).wait()


def _dispatch(slot_flat, mrows, c, p_rows, td):
    n = mrows.shape[0] // c
    assert td % ROW_UNROLL == 0
    return pl.pallas_call(
        functools.partial(_dispatch_kernel, c=c),
        grid=(n // td,),
        in_specs=[
            pl.BlockSpec((td * TOP_K,), lambda i: (i,), memory_space=pltpu.SMEM),
            pl.BlockSpec((td * c, V7X_LANES), lambda i: (i, 0)),
        ],
        out_specs=pl.BlockSpec(memory_space=pl.ANY),
        scratch_shapes=[pltpu.SemaphoreType.DMA(())],
        out_shape=jax.ShapeDtypeStruct((p_rows * c, V7X_LANES), F32),
        compiler_params=_cparams(("arbitrary",), has_side_effects=True),
        name="moe_dispatch",
    )(slot_flat, mrows)


def _ffn_kernel(be_ref, nu_ref, x_ref, w1_ref, b1_ref, w2_ref, b2_ref, y_ref, w1s_ref, w2s_ref, *, d_ff):
    i = pl.program_id(0)

    @pl.when((i == 0) | (be_ref[i] != be_ref[jnp.maximum(i - 1, 0)]))
    def _():
        w1s_ref[...] = w1_ref[...].astype(BF16)
        w2s_ref[...] = w2_ref[...].astype(BF16)

    @pl.when(i < nu_ref[0])
    def _():
        x = _load_rows(x_ref, w1_ref.shape[0] // V7X_LANES).astype(BF16)
        hcat = _dot(x, w1s_ref[...]) + b1_ref[...]
        glu = jnp.minimum(hcat[:, :d_ff], SWIGLU_LIMIT)
        lin = jnp.clip(hcat[:, d_ff:], -SWIGLU_LIMIT, SWIGLU_LIMIT)
        act = glu * jax.nn.sigmoid(SWIGLU_ALPHA * glu) * (lin + 1.0)
        _store_rows(y_ref, _dot(act.astype(BF16), w2s_ref[...]) + b2_ref[...])


def _moe_ffn(blk_e, n_used, xrows, w1, b1, w2, b2, bm):
    ne, d, f2 = w1.shape
    d_ff = f2 // 2
    c = d // V7X_LANES
    p_rows = xrows.shape[0] // c
    blk = (bm * c, V7X_LANES)
    grid_spec = pltpu.PrefetchScalarGridSpec(
        num_scalar_prefetch=2,
        grid=(p_rows // bm,),
        in_specs=[
            pl.BlockSpec(blk, lambda i, be, nu: (jnp.minimum(i, nu[0] - 1), 0)),
            pl.BlockSpec((None, d, f2), lambda i, be, nu: (be[i], 0, 0)),
            pl.BlockSpec((None, 1, f2), lambda i, be, nu: (be[i], 0, 0)),
            pl.BlockSpec((None, d_ff, d), lambda i, be, nu: (be[i], 0, 0)),
            pl.BlockSpec((None, 1, d), lambda i, be, nu: (be[i], 0, 0)),
        ],
        out_specs=pl.BlockSpec(blk, lambda i, be, nu: (i, 0)),
        scratch_shapes=[pltpu.VMEM((d, f2), BF16), pltpu.VMEM((d_ff, d), BF16)],
    )
    return pl.pallas_call(
        functools.partial(_ffn_kernel, d_ff=d_ff),
        grid_spec=grid_spec,
        out_shape=jax.ShapeDtypeStruct(xrows.shape, F32),
        compiler_params=_cparams(("arbitrary",)),
        name="moe_ffn",
    )(blk_e, n_used, xrows, w1, b1.reshape(ne, 1, f2), w2, b2.reshape(ne, 1, d))


def _combine_kernel(slot_ref, next_slot_ref, h_ref, gate_ref, p_ref, wp_ref, gp_ref, wg_ref, y_ref, o_ref,
                    buf_ref, h2_ref, sems):
    i = pl.program_id(0)
    tc, d = h_ref.shape
    c = d // V7X_LANES

    def start_gather(slots, half):
        def issue(g, carry):
            for u in range(ROW_UNROLL):
                t = g * ROW_UNROLL + u
                for kk in range(TOP_K):
                    src = slots[t * TOP_K + kk]
                    pltpu.make_async_copy(y_ref.at[_row(src, c), :], buf_ref.at[half, kk, _row(t, c), :],
                                          sems.at[half]).start(priority=kk % 2)
            return carry

        lax.fori_loop(0, tc // ROW_UNROLL, issue, 0)

    cur = i % 2

    @pl.when(i == 0)
    def _():
        start_gather(slot_ref, 0)

    @pl.when(i + 1 < pl.num_programs(0))
    def _():
        start_gather(next_slot_ref, 1 - cur)

    for kk in range(TOP_K):
        pltpu.make_async_copy(y_ref.at[pl.ds(0, tc * c), :], buf_ref.at[cur, kk], sems.at[cur]).wait()

    gate = gate_ref[...]
    for j in range(c):
        cols = slice(j * V7X_LANES, (j + 1) * V7X_LANES)
        acc = h_ref[:, cols]
        for kk in range(TOP_K):
            acc = acc + gate[:, kk:kk + 1] * buf_ref[cur, kk, pl.ds(j, tc, stride=c), :]
        h2_ref[:, cols] = acc

    h2 = h2_ref[...]
    e = _dot(p_ref[...].astype(BF16), wp_ref[...])
    ms = jnp.mean(e * e, axis=-1, keepdims=True)
    e = e * lax.rsqrt(ms + NORM_EPS) * gp_ref[...]
    o_ref[...] = h2 + e * jax.nn.sigmoid(_dot(h2.astype(BF16), wg_ref[...]))


def _combine_ple(slot_flat, h, gates, yrows, p2, wp_bf, gp, wg_bf, tc):
    n, d = h.shape
    pd = p2.shape[1]
    assert tc % ROW_UNROLL == 0
    last = n // tc - 1
    full = lambda i: (0, 0)
    row = lambda i: (i, 0)
    return pl.pallas_call(
        _combine_kernel,
        grid=(n // tc,),
        in_specs=[
            pl.BlockSpec((tc * TOP_K,), lambda i: (i,), memory_space=pltpu.SMEM),
            pl.BlockSpec((tc * TOP_K,), lambda i: (jnp.minimum(i + 1, last),), memory_space=pltpu.SMEM),
            pl.BlockSpec((tc, d), row),
            pl.BlockSpec((tc, TOP_K), row),
            pl.BlockSpec((tc, pd), row),
            pl.BlockSpec((pd, d), full),
            pl.BlockSpec((1, d), full),
            pl.BlockSpec((d, d), full),
            pl.BlockSpec(memory_space=pl.ANY),
        ],
        out_specs=pl.BlockSpec((tc, d), row),
        scratch_shapes=[pltpu.VMEM((2, TOP_K, tc * d // V7X_LANES, V7X_LANES), F32),
                        pltpu.VMEM((tc, d), F32),
                        pltpu.SemaphoreType.DMA((2,))],
        out_shape=jax.ShapeDtypeStruct((n, d), F32),
        compiler_params=_cparams(("arbitrary",)),
        name="moe_combine_ple",
    )(slot_flat, slot_flat, h, gates, p2, wp_bf, gp, wg_bf, yrows)


def _tile(n, want):
    t = min(n, want)
    assert n % t == 0, (n, t)
    return t


def _layer(h3, p3, lambda_init, norm_mix_g, w_in, rw_mu, rw_w0, rw_w2, rw_a0, rw_a2, rw_g2, rw_k_k, rw_k_a, rw_r_k,
           rw_ln_g, rw_ln_b, da_q_norm_g, da_k_norm_g, da_lq1, da_lk1, da_lq2, da_lk2, da_subln_g, w_out,
           norm_ffn_g, w_router, b_router, w1, b1, w2, b2, w_ple, ple_norm_g, w_ple_gate):
    b, t, d = h3.shape
    n = b * t
    c_rw = rw_k_k.shape[0]
    lora = rw_w2.shape[1]
    rw_cols = rw_mu.shape[0]
    rw_heads = rw_r_k.shape[0]
    c_da = w_in.shape[1] - rw_cols
    assert c_da % 3 == 0
    c_da //= 3
    qk_dim = da_q_norm_g.shape[0]
    v_dim = da_subln_g.shape[0]
    da_heads = c_da // v_dim
    assert rw_heads * RW_HEAD == c_rw and c_rw % HEAD_GROUP == 0 and v_dim == V7X_LANES and 2 * qk_dim == v_dim
    assert 2 * lora == V7X_LANES and w_router.shape[1] == N_EXPERTS

    x2 = h3.reshape(n, d)
    row = lambda a: a.reshape(1, -1).astype(F32)

    gq_t = row(jnp.tile(da_q_norm_g, c_da // qk_dim))
    gk_t = row(jnp.tile(da_k_norm_g, c_da // qk_dim))
    gm_qk = _group_matrix(c_da, qk_dim, 1.0 / qk_dim)
    ta = _tile(t, ATTN_BLOCK)
    zrw, q, k, vt = _inproj(x2, row(norm_mix_g), w_in.astype(BF16), gq_t, gk_t, gm_qk, rw_cols, c_da,
                            1.0 / math.sqrt(qk_dim), _tile(ta, 256), b, da_heads, ta)

    zeros = jnp.zeros((lora, c_rw), F32)
    w2bd = jnp.concatenate([jnp.concatenate([rw_w2[0], zeros], axis=1),
                            jnp.concatenate([zeros, rw_w2[1]], axis=1)], axis=0).astype(BF16)
    a2bd = jnp.concatenate([jnp.concatenate([rw_a2[0], zeros], axis=1),
                            jnp.concatenate([zeros, rw_a2[1]], axis=1)], axis=0).astype(BF16)
    gsum = _group_matrix(c_rw, RW_HEAD, 1.0)
    r, vv, nkk, lw, kd, bd, g, bonus = _rwkv_prep(
        zrw.reshape(b, t, rw_cols), row(rw_mu), w2bd, row(rw_w0), a2bd, row(rw_a0), rw_g2.astype(BF16),
        row(rw_k_k), row(rw_k_a), row(rw_r_k), gsum, c_rw, lora, _tile(t, 256))
    yf, yb = _rwkv_scan(r, vv, nkk, lw, kd, bd, _tile(t, 128))

    slopes = 2.0 ** (-8.0 * jnp.arange(1, da_heads + 1, dtype=F32) / da_heads)
    q3 = q.reshape(b, t, c_da)
    k3 = k.reshape(b, t, c_da)
    small = (row(da_lq1), row(da_lk1), row(da_lq2), row(da_lk2), row(da_subln_g))

    def attention_running_max():
        v3 = vt[:, :, :, :v_dim, :].transpose(0, 2, 4, 1, 3).reshape(b, t, c_da)
        return _attention(slopes, q3, k3, v3, *small, da_heads, v_dim, lambda_init, ta, ta)

    score_bound = 1.01 * math.sqrt(qk_dim) * jnp.max(jnp.abs(da_q_norm_g)) * jnp.max(jnp.abs(da_k_norm_g))
    mb = jnp.ceil(2.0 * score_bound.astype(F32)) * 0.5
    o_da = lax.cond(
        mb <= MAX_FIXED_SHIFT,
        lambda: _attention_fixed(mb, q3, k3, vt, *small, da_heads, v_dim, lambda_init, ta),
        attention_running_max)

    gmean = _group_matrix(c_rw, RW_HEAD, 1.0 / RW_HEAD)
    w_out_bf = w_out.astype(BF16)
    h1, m, idx, gates, counts = _post(
        yf.reshape(n, c_rw), yb.reshape(n, c_rw), bonus.reshape(n, c_rw), g.reshape(n, c_rw), o_da.reshape(n, c_da), x2,
        row(rw_ln_g), row(rw_ln_b), gmean, w_out_bf[:c_rw], w_out_bf[c_rw:], row(norm_ffn_g),
        w_router.astype(F32), row(b_router), RW_HEAD * 1e-5, _tile(n, 256))

    bm = MOE_BLOCK_ROWS
    counts = counts.reshape(N_EXPERTS).astype(jnp.int32)
    padded = (counts + bm - 1) // bm * bm
    pend = jnp.cumsum(padded)
    offs = (pend - padded).astype(jnp.int32)
    slot = _moe_rank(idx, offs.astype(F32).reshape(1, N_EXPERTS), _tile(n, 512))
    n_blk = (n * TOP_K) // bm + N_EXPERTS
    blk_start = jnp.arange(n_blk, dtype=jnp.int32) * bm
    blk_e = jnp.minimum(jnp.sum(blk_start[:, None] >= pend[None, :], axis=1), N_EXPERTS - 1).astype(jnp.int32)
    n_used = (pend[-1] // bm).astype(jnp.int32).reshape(1)
    slot_flat = slot.reshape(n * TOP_K)
    xs = _dispatch(slot_flat, m, d // V7X_LANES, n_blk * bm, _tile(n, 256))
    ys = _moe_ffn(blk_e, n_used, xs, w1, b1, w2, b2, bm)
    out = _combine_ple(slot_flat, h1, gates, ys, p3.reshape(n, -1), w_ple.astype(BF16), row(ple_norm_g),
                       w_ple_gate.astype(BF16), _tile(n, 256))
    return out.reshape(b, t, d)


def kernel(x, p, norm_mix_g, w_in, rw_mu, rw_w0, rw_w2, rw_a0, rw_a2, rw_g2, rw_k_k, rw_k_a, rw_r_k, rw_ln_g,
           rw_ln_b, da_q_norm_g, da_k_norm_g, da_lq1, da_lk1, da_lq2, da_lk2, da_subln_g, w_out, norm_ffn_g,
           w_router, b_router, w1, b1, w2, b2, w_ple, ple_norm_g, w_ple_gate):
    h = x.astype(F32)
    params = (norm_mix_g, w_in, rw_mu, rw_w0, rw_w2, rw_a0, rw_a2, rw_g2, rw_k_k, rw_k_a, rw_r_k, rw_ln_g, rw_ln_b,
              da_q_norm_g, da_k_norm_g, da_lq1, da_lk1, da_lq2, da_lk2, da_subln_g, w_out, norm_ffn_g,
              w_router, b_router, w1, b1, w2, b2, w_ple, ple_norm_g, w_ple_gate)
    for i in range(p.shape[0]):
        lambda_init = 0.8 - 0.6 * math.exp(-0.3 * i)
        h = _layer(h, p[i], lambda_init, *(a[i] for a in params))
    return h.astype(x.dtype)
```

```python
import functools
import math

import jax
import jax.numpy as jnp
import numpy as np
from jax import lax
from jax.experimental import pallas as pl
from jax.experimental.pallas import tpu as pltpu

F32 = jnp.float32
BF16 = jnp.bfloat16

V7X_LANES = 128
V7X_VMEM_BYTES = 64 * 1024 * 1024
VMEM_LIMIT = 52 * 1024 * 1024

NORM_EPS = 1e-6
RW_HEAD = 64
CHUNK = 64
HEAD_GROUP = 256
N_EXPERTS = 32
TOP_K = 4
MOE_BLOCK_ROWS = 512
SWIGLU_ALPHA = 1.702
SWIGLU_LIMIT = 7.0


def _cparams(sem, **kw):
    return pltpu.CompilerParams(dimension_semantics=sem, vmem_limit_bytes=VMEM_LIMIT, **kw)


def _dot(a, b):
    return jnp.dot(a, b, preferred_element_type=F32)


def _dot_hilo_lhs(g_bf16, x):
    hi, lo = _split2(x)
    return _dot(g_bf16, hi) + _dot(g_bf16, lo)


def _split2(x):
    hi = x.astype(BF16)
    return hi, (x - hi.astype(F32)).astype(BF16)


def _dot_hilo_rhs(x, g_bf16):
    hi, lo = _split2(x)
    out = _dot(jnp.concatenate([hi, lo], axis=0), g_bf16)
    return out[:x.shape[0]] + out[x.shape[0]:]


def _dot_hilo(a, b):
    ah, al = _split2(a)
    bh, bl = _split2(b)
    return _dot(ah, bh) + (_dot(ah, bl) + _dot(al, bh))


def _store_rows(ref, x2):
    rows, d = x2.shape
    c = d // V7X_LANES
    for j in range(c):
        ref[pl.ds(j, rows, stride=c), :] = x2[:, j * V7X_LANES:(j + 1) * V7X_LANES]


def _load_rows(ref, c):
    rows = ref.shape[0] // c
    return jnp.concatenate([ref[pl.ds(j, rows, stride=c), :] for j in range(c)], axis=1)


def _group_matrix(width, group, value):
    g = np.arange(width) // group
    return jnp.asarray(np.where(g[:, None] == g[None, :], value, 0.0), BF16)


def _inproj_kernel(x_ref, g_ref, w_ref, gq_ref, gk_ref, gm_ref, zrw_ref, q_ref, k_ref, vt_ref, *, rw_cols, c_da, scale):
    x = x_ref[...]
    ms = jnp.mean(x * x, axis=-1, keepdims=True)
    u = (x * lax.rsqrt(ms + NORM_EPS) * g_ref[...]).astype(BF16)
    z = _dot(u, w_ref[...])
    zrw_ref[...] = z[:, :rw_cols]
    zq = z[:, rw_cols:rw_cols + c_da]
    zk = z[:, rw_cols + c_da:rw_cols + 2 * c_da]
    zv = z[:, rw_cols + 2 * c_da:]
    tm = zq.shape[0]
    msqk = _dot(jnp.concatenate([zq * zq, zk * zk], axis=0).astype(BF16), gm_ref[...])
    q_ref[...] = (zq * lax.rsqrt(msqk[:tm] + NORM_EPS) * (gq_ref[...] * scale)).astype(BF16)
    k_ref[...] = (zk * lax.rsqrt(msqk[tm:] + NORM_EPS) * gk_ref[...]).astype(BF16)
    heads, vrows, _ = vt_ref.shape
    v_dim = c_da // heads
    pad_row = lax.broadcasted_iota(jnp.int32, (vrows - v_dim, tm), 0)
    pad = jnp.where(pad_row == 0, 1.0, 0.0).astype(BF16)
    for hd in range(heads):
        vt_ref[hd, 0:v_dim, :] = zv[:, hd * v_dim:(hd + 1) * v_dim].T.astype(BF16)
        vt_ref[hd, v_dim:, :] = pad


def _inproj(x2, g, w_bf, gq_t, gk_t, gm, rw_cols, c_da, scale, tm, b, heads, tb):
    n, d = x2.shape
    cols = w_bf.shape[1]
    t = n // b
    tps = t // tm
    per = tb // tm
    vrows = c_da // heads + ATTN_V_PAD
    full = lambda i: (0, 0)
    row = lambda i: (i, 0)
    return pl.pallas_call(
        functools.partial(_inproj_kernel, rw_cols=rw_cols, c_da=c_da, scale=scale),
        grid=(n // tm,),
        in_specs=[
            pl.BlockSpec((tm, d), row),
            pl.BlockSpec((1, d), full),
            pl.BlockSpec((d, cols), full),
            pl.BlockSpec((1, c_da), full),
            pl.BlockSpec((1, c_da), full),
            pl.BlockSpec((c_da, c_da), full),
        ],
        out_specs=[
            pl.BlockSpec((tm, rw_cols), row),
            pl.BlockSpec((tm, c_da), row),
            pl.BlockSpec((tm, c_da), row),
            pl.BlockSpec((None, heads, None, vrows, tm),
                         lambda i: (i // tps, 0, (i % tps) // per, 0, (i % tps) % per)),
        ],
        out_shape=[
            jax.ShapeDtypeStruct((n, rw_cols), F32),
            jax.ShapeDtypeStruct((n, c_da), BF16),
            jax.ShapeDtypeStruct((n, c_da), BF16),
            jax.ShapeDtypeStruct((b, heads, t // tb, vrows, tb), BF16),
        ],
        compiler_params=_cparams(("parallel",)),
        name="inproj",
    )(x2, g, w_bf, gq_t, gk_t, gm)


def _rwkv_prep_kernel(z_ref, zp_ref, zn_ref, mu_ref, w2_ref, w0_ref, a2_ref, a0_ref, g2_ref, kk_ref, ka_ref,
                      rk_ref, gsum_ref,
                      r_ref, v_ref, nkk_ref, lw_ref, kd_ref, bd_ref, g_ref, bonus_ref, *, c_rw, lora):
    i = pl.program_id(1)
    nt = pl.num_programs(1)
    z = z_ref[...]
    tt = z.shape[0]
    rows = lax.broadcasted_iota(jnp.int32, z.shape, 0)
    prev_row = jnp.where(i == 0, 0.0, zp_ref[7:8, :])
    next_row = jnp.where(i == nt - 1, 0.0, zn_ref[0:1, :])
    zprev = jnp.where(rows == 0, prev_row, pltpu.roll(z, 1, 0))
    znext = jnp.where(rows == tt - 1, next_row, pltpu.roll(z, tt - 1, 0))
    z = z + (0.5 * (zprev + znext) - z) * mu_ref[...]

    r = z[:, 0:c_rw]
    k = z[:, c_rw:2 * c_rw]
    v = z[:, 2 * c_rw:3 * c_rw]
    o = 3 * c_rw
    wd = z[:, o:o + 2 * lora]
    ad = z[:, o + 2 * lora:o + 4 * lora]
    gd = z[:, o + 4 * lora:]

    xw = w0_ref[...] + _dot(jnp.tanh(wd).astype(BF16), w2_ref[...])
    xa = a0_ref[...] + _dot(ad.astype(BF16), a2_ref[...])
    lw = -math.exp(-0.5) * jax.nn.sigmoid(xw)
    rate = jax.nn.sigmoid(xa)
    g = _dot(jax.nn.sigmoid(gd).astype(BF16), g2_ref[...])

    gsum = gsum_ref[...]
    kkr = k * kk_ref[...]
    ka = ka_ref[...]
    kd0 = k * (1.0 + (rate[:, :c_rw] - 1.0) * ka)
    kd1 = k * (1.0 + (rate[:, c_rw:] - 1.0) * ka)
    sums = _dot_hilo_rhs(jnp.concatenate([kkr * kkr, r * (0.5 * (kd0 + kd1)) * rk_ref[...]], axis=0), gsum)
    kk = kkr * lax.rsqrt(sums[:tt] + 1e-12)
    bonus = sums[tt:] * v

    r_ref[...] = r.astype(BF16)
    v_ref[...] = v.astype(BF16)
    nkk_ref[...] = (-kk).astype(BF16)
    lw_ref[0] = lw[:, :c_rw]
    lw_ref[1] = lw[:, c_rw:]
    kd_ref[0] = kd0.astype(BF16)
    kd_ref[1] = kd1.astype(BF16)
    bd_ref[0] = (kk * rate[:, :c_rw]).astype(BF16)
    bd_ref[1] = (kk * rate[:, c_rw:]).astype(BF16)
    g_ref[...] = g
    bonus_ref[...] = bonus


def _rwkv_prep(zrw3, mu, w2bd, w0f, a2bd, a0f, g2_bf, k_k, k_a, r_kf, gsum, c_rw, lora, tt):
    b, t, cols = zrw3.shape
    nt = t // tt
    hb = tt // 8
    nhb = t // 8
    full = lambda bi, i: (0, 0)
    tile = lambda bi, i: (bi, i, 0)
    dtile = lambda bi, i: (0, bi, i, 0)
    one = jax.ShapeDtypeStruct((b, t, c_rw), F32)
    two = jax.ShapeDtypeStruct((2, b, t, c_rw), F32)
    one16 = jax.ShapeDtypeStruct((b, t, c_rw), BF16)
    two16 = jax.ShapeDtypeStruct((2, b, t, c_rw), BF16)
    return pl.pallas_call(
        functools.partial(_rwkv_prep_kernel, c_rw=c_rw, lora=lora),
        grid=(b, nt),
        in_specs=[
            pl.BlockSpec((None, tt, cols), tile),
            pl.BlockSpec((None, 8, cols), lambda bi, i: (bi, jnp.maximum(i * hb - 1, 0), 0)),
            pl.BlockSpec((None, 8, cols), lambda bi, i: (bi, jnp.minimum((i + 1) * hb, nhb - 1), 0)),
            pl.BlockSpec((1, cols), full),
            pl.BlockSpec(w2bd.shape, full),
            pl.BlockSpec((1, 2 * c_rw), full),
            pl.BlockSpec(a2bd.shape, full),
            pl.BlockSpec((1, 2 * c_rw), full),
            pl.BlockSpec(g2_bf.shape, full),
            pl.BlockSpec((1, c_rw), full),
            pl.BlockSpec((1, c_rw), full),
            pl.BlockSpec((1, c_rw), full),
            pl.BlockSpec((c_rw, c_rw), full),
        ],
        out_specs=[
            pl.BlockSpec((None, tt, c_rw), tile),
            pl.BlockSpec((None, tt, c_rw), tile),
            pl.BlockSpec((None, tt, c_rw), tile),
            pl.BlockSpec((2, None, tt, c_rw), dtile),
            pl.BlockSpec((2, None, tt, c_rw), dtile),
            pl.BlockSpec((2, None, tt, c_rw), dtile),
            pl.BlockSpec((None, tt, c_rw), tile),
            pl.BlockSpec((None, tt, c_rw), tile),
        ],
        out_shape=[one16, one16, one16, two, two16, two16, one, one],
        compiler_params=_cparams(("parallel", "parallel")),
        name="rwkv_prep",
    )(zrw3, zrw3, zrw3, mu, w2bd, w0f, a2bd, a0f, g2_bf, k_k, k_a, r_kf, gsum)


M_STRICT, M_INCL, M_EYE, M_PAIR, M_LEVEL = 0, 1, 2, 3, 4
N_LEVELS = 5
SCAN_SEQS = 2


def _scan_tables(w):
    c = CHUNK
    masks = []
    mcums = []
    for d in range(2):
        t = np.broadcast_to(np.arange(c)[:, None], (c, w))
        s = np.broadcast_to((np.arange(w) % RW_HEAD)[None, :], (c, w))
        tt = np.broadcast_to(np.arange(c)[:, None], (c, c))
        ss = np.broadcast_to(np.arange(c)[None, :], (c, c))
        if d == 1:
            t, s, tt, ss = c - 1 - t, c - 1 - s, c - 1 - tt, c - 1 - ss
        rows = [s < t, s <= t, s == t, (t // 2 == s // 2) & (t > s)]
        sz = 2
        while sz < c:
            rows.append((t // (2 * sz) == s // (2 * sz)) & ((t // sz) % 2 == 1) & ((s // sz) % 2 == 0))
            sz *= 2
        assert len(rows) == M_LEVEL + N_LEVELS
        masks.append(np.stack(rows).astype(np.float32))
        mcums.append((ss <= tt).astype(np.float32))
    return jnp.asarray(np.stack(masks), F32), jnp.asarray(np.stack(mcums), BF16)


def _block_diag(y, bd16):
    g = y.shape[1] // RW_HEAD
    return jnp.concatenate([y.astype(BF16)] * g, axis=0) * bd16


def _bmm(x, y, bd16):
    outs = []
    for s in range(0, y.shape[1], HEAD_GROUP):
        outs.append(_dot(x[:, s:s + HEAD_GROUP].astype(BF16), _block_diag(y[:, s:s + HEAD_GROUP], bd16)))
    return jnp.concatenate(outs, axis=1)


def _bmm_nt(x, y, bd16):
    outs = []
    for s in range(0, y.shape[1], HEAD_GROUP):
        bd = _block_diag(y[:, s:s + HEAD_GROUP], bd16)
        outs.append(lax.dot_general(x[:, s:s + HEAD_GROUP].astype(BF16), bd, (((1,), (1,)), ((), ())),
                                    preferred_element_type=F32))
    return jnp.concatenate(outs, axis=1)


def _bmm_tn(x, y, bd32):
    outs = []
    for s in range(0, y.shape[1], HEAD_GROUP):
        xt = x[:, s:s + HEAD_GROUP].T.astype(BF16)
        full = _dot(xt, y[:, s:s + HEAD_GROUP].astype(BF16)) * bd32
        o = full[0:RW_HEAD]
        for j in range(1, HEAD_GROUP // RW_HEAD):
            o = o + full[j * RW_HEAD:(j + 1) * RW_HEAD]
        outs.append(o)
    return jnp.concatenate(outs, axis=1)


def _scan_chunks(r, lw, k, v, a, b, state, masks, mcum, bd16, bd32):
    c = r[0].shape[0]

    def each(f, *cols):
        return [f(*xs) for xs in zip(*cols)]

    cum = each(_dot_hilo_lhs, mcum, lw)
    tot = each(lambda x: jnp.sum(x, axis=0, keepdims=True), lw)
    e_neg = each(lambda x: jnp.exp(-x), cum)
    e_tail = each(lambda t_, x: jnp.exp(t_ - x), tot, cum)
    rt = each(lambda x, cm: x * jnp.exp(cm), r, cum)
    at = each(lambda x, cm, l_: x * jnp.exp(cm - l_), a, cum, lw)
    bt = each(jnp.multiply, b, e_neg)
    kt = each(jnp.multiply, k, e_neg)
    bh = each(jnp.multiply, b, e_tail)
    kh = each(jnp.multiply, k, e_tail)

    ar = each(lambda x, y_: jnp.concatenate([x, y_], axis=0), at, rt)
    pb = each(lambda x, y_: _bmm_nt(x, y_, bd16), ar, bt)
    pk = each(lambda x, y_: _bmm_nt(x, y_, bd16), ar, kt)
    a_ab = each(lambda x, m: x[:c] * m[M_STRICT], pb, masks)
    a_rb = each(lambda x, m: x[c:] * m[M_INCL], pb, masks)
    a_ak = each(lambda x, m: x[:c] * m[M_STRICT], pk, masks)
    a_rk = each(lambda x, m: x[c:] * m[M_INCL], pk, masks)

    tinv = each(lambda x, m: m[M_EYE] + x * m[M_PAIR], a_ab, masks)
    for lvl in range(N_LEVELS):
        x = each(lambda aa, m, ti: _bmm(aa * m[M_LEVEL + lvl], ti, bd16), a_ab, masks, tinv)
        tinv = each(lambda ti, x_: ti + _bmm(ti, x_, bd16), tinv, x)

    bmm = lambda p, q: _bmm(p, q, bd16)
    stack = lambda p, q: jnp.concatenate([p, q], axis=0)
    att = each(bmm, tinv, at)
    akv = each(lambda p, p2, q: _bmm(stack(p, p2), q, bd16), a_ak, a_rk, v)
    vt = each(lambda ti, x_: _bmm(ti, x_[:c], bd16), tinv, akv)
    rh = each(lambda x, p, q: x + _bmm(p, q, bd16), rt, a_rb, att)
    y_in = each(lambda p, q, x_: _bmm(p, q, bd16) + x_[c:], a_rb, vt, akv)
    mc = each(lambda m, t_, p, q: m[M_EYE] * jnp.exp(t_) + _bmm_tn(p, q, bd32), masks, tot, bh, att)
    nc = each(lambda p, p2, q, q2: _bmm_tn(stack(p, p2), stack(q, q2), bd32), bh, kh, vt, v)
    onst = each(lambda p, p2, q: _bmm(stack(p, p2), q, bd16), rh, mc, state)
    y = each(lambda x_, yi: x_[:c] + yi, onst, y_in)
    new_state = each(lambda x_, n_: x_[c:] + n_, onst, nc)
    return y, new_state


def _rwkv_scan_kernel(rf_ref, vf_ref, af_ref, rb_ref, vb_ref, ab_ref, lwf_ref, kf_ref, bf_ref, lwb_ref, kb_ref,
                      bb_ref, mask_ref, mcum_ref, bd16_ref, bd32_ref, yf_ref, yb_ref, state_ref, *, n_chunks):
    @pl.when(pl.program_id(1) == 0)
    def _():
        state_ref[...] = jnp.zeros_like(state_ref)

    dirs = ((0, rf_ref, lwf_ref, kf_ref, vf_ref, af_ref, bf_ref, yf_ref),
            (1, rb_ref, lwb_ref, kb_ref, vb_ref, ab_ref, bb_ref, yb_ref))
    n_seq = rf_ref.shape[0]

    def body(j, carry):
        cols = [[] for _ in range(9)]
        where = []
        for d, r_ref, lw_ref, k_ref, v_ref, a_ref, b_ref, y_ref in dirs:
            cj = j if d == 0 else n_chunks - 1 - j
            sl = pl.ds(pl.multiple_of(cj * CHUNK, CHUNK), CHUNK)
            for g in range(n_seq):
                vals = (r_ref[g, sl, :], lw_ref[g, sl, :], k_ref[g, sl, :], v_ref[g, sl, :], a_ref[g, sl, :],
                        b_ref[g, sl, :], state_ref[d, g], mask_ref.at[d], mcum_ref[d])
                for col, val in zip(cols, vals):
                    col.append(val)
                where.append((y_ref, d, g, sl))
        ys, sts = _scan_chunks(*cols, bd16_ref[...], bd32_ref[...])
        for (y_ref, d, g, sl), y, st in zip(where, ys, sts):
            y_ref[g, sl, :] = y
            state_ref[d, g] = st
        return carry

    lax.fori_loop(0, n_chunks, body, 0)


def _rwkv_scan(r, v, nkk, lw, kd, bd, tb):
    b, t, w = r.shape
    nb = t // tb
    g = SCAN_SEQS if b % SCAN_SEQS == 0 else 1
    masks, mcum = _scan_tables(w)
    bd32 = _group_matrix(HEAD_GROUP, RW_HEAD, 1.0).astype(F32)
    bd16 = bd32.astype(BF16)
    fwd = lambda bi, i: (bi, i, 0)
    bwd = lambda bi, i: (bi, nb - 1 - i, 0)
    fwd_d = lambda bi, i: (0, bi, i, 0)
    bwd_d = lambda bi, i: (1, bi, nb - 1 - i, 0)
    const = lambda nd: (lambda bi, i: (0,) * nd)
    blk = (g, tb, w)
    dblk = (None, g, tb, w)
    out = jax.ShapeDtypeStruct((b, t, w), F32)
    return pl.pallas_call(
        functools.partial(_rwkv_scan_kernel, n_chunks=tb // CHUNK),
        grid=(b // g, nb),
        in_specs=[
            pl.BlockSpec(blk, fwd), pl.BlockSpec(blk, fwd), pl.BlockSpec(blk, fwd),
            pl.BlockSpec(blk, bwd), pl.BlockSpec(blk, bwd), pl.BlockSpec(blk, bwd),
            pl.BlockSpec(dblk, fwd_d), pl.BlockSpec(dblk, fwd_d), pl.BlockSpec(dblk, fwd_d),
            pl.BlockSpec(dblk, bwd_d), pl.BlockSpec(dblk, bwd_d), pl.BlockSpec(dblk, bwd_d),
            pl.BlockSpec(masks.shape, const(4)),
            pl.BlockSpec(mcum.shape, const(3)),
            pl.BlockSpec(bd16.shape, const(2)),
            pl.BlockSpec(bd32.shape, const(2)),
        ],
        out_specs=[pl.BlockSpec(blk, fwd), pl.BlockSpec(blk, bwd)],
        out_shape=[out, out],
        scratch_shapes=[pltpu.VMEM((2, g, CHUNK, w), F32)],
        compiler_params=_cparams(("parallel", "arbitrary")),
        name="rwkv_scan",
    )(r, v, nkk, r, v, nkk, lw, kd, bd, lw, kd, bd, masks, mcum, bd16, bd32)


def _attn_kernel(slope_ref, q_ref, k_ref, v_ref, lq1_ref, lk1_ref, lq2_ref, lk2_ref, sg_ref, o_ref,
                 m_ref, l_ref, acc_ref, *, qk_dim, lambda_init):
    h = pl.program_id(1)
    i = pl.program_id(2)
    j = pl.program_id(3)
    nk = pl.num_programs(3)
    tq = q_ref.shape[0]
    tk = k_ref.shape[0]

    @pl.when(j == 0)
    def _():
        m_ref[...] = jnp.full(m_ref.shape, -jnp.inf, F32)
        l_ref[...] = jnp.zeros_like(l_ref)
        acc_ref[...] = jnp.zeros_like(acc_ref)

    q = q_ref[...]
    k = k_ref[...]
    v = v_ref[...]
    lane = lax.broadcasted_iota(jnp.int32, q.shape, 1)
    zero = jnp.zeros_like(q)
    qi = lax.broadcasted_iota(jnp.int32, (tq, tk), 0) + i * tq
    kj = lax.broadcasted_iota(jnp.int32, (tq, tk), 1) + j * tk
    bias = slope_ref[h] * jnp.abs(qi - kj).astype(F32)
    for c in range(2):
        qc = jnp.where((lane < qk_dim) == (c == 0), q, zero)
        s = lax.dot_general(qc, k, (((1,), (1,)), ((), ())), preferred_element_type=F32) - bias
        m_old = m_ref[c]
        m_new = jnp.maximum(m_old, jnp.max(s, axis=-1, keepdims=True))
        alpha = jnp.exp(m_old - m_new)
        p = jnp.exp(s - m_new)
        l_ref[c] = alpha * l_ref[c] + jnp.sum(p, axis=-1, keepdims=True)
        acc_ref[c] = alpha * acc_ref[c] + _dot(p.astype(BF16), v)
        m_ref[c] = m_new

    @pl.when(j == nk - 1)
    def _():
        lam = (jnp.exp(jnp.sum(lq1_ref[...] * lk1_ref[...], axis=-1, keepdims=True))
               - jnp.exp(jnp.sum(lq2_ref[...] * lk2_ref[...], axis=-1, keepdims=True)) + lambda_init)
        o = acc_ref[0] / l_ref[0] - lam * (acc_ref[1] / l_ref[1])
        ms = jnp.mean(o * o, axis=-1, keepdims=True)
        o_ref[...] = o * lax.rsqrt(ms + NORM_EPS) * (sg_ref[...] * (1.0 - lambda_init))


def _attention(slopes, q3, k3, v3, lq1, lk1, lq2, lk2, subln_g, heads, v_dim, lambda_init, tq, tk):
    b, t, _ = q3.shape
    small = lambda s, bi, h, i, j: (0, 0)
    grid_spec = pltpu.PrefetchScalarGridSpec(
        num_scalar_prefetch=1,
        grid=(b, heads, t // tq, t // tk),
        in_specs=[
            pl.BlockSpec((None, tq, v_dim), lambda bi, h, i, j, s: (bi, i, h)),
            pl.BlockSpec((None, tk, v_dim), lambda bi, h, i, j, s: (bi, j, h)),
            pl.BlockSpec((None, tk, v_dim), lambda bi, h, i, j, s: (bi, j, h)),
            pl.BlockSpec(lq1.shape, lambda bi, h, i, j, s: (0, 0)),
            pl.BlockSpec(lk1.shape, lambda bi, h, i, j, s: (0, 0)),
            pl.BlockSpec(lq2.shape, lambda bi, h, i, j, s: (0, 0)),
            pl.BlockSpec(lk2.shape, lambda bi, h, i, j, s: (0, 0)),
            pl.BlockSpec(subln_g.shape, lambda bi, h, i, j, s: (0, 0)),
        ],
        out_specs=pl.BlockSpec((None, tq, v_dim), lambda bi, h, i, j, s: (bi, i, h)),
        scratch_shapes=[
            pltpu.VMEM((2, tq, 1), F32),
            pltpu.VMEM((2, tq, 1), F32),
            pltpu.VMEM((2, tq, v_dim), F32),
        ],
    )
    del small
    return pl.pallas_call(
        functools.partial(_attn_kernel, qk_dim=v_dim // 2, lambda_init=lambda_init),
        grid_spec=grid_spec,
        out_shape=jax.ShapeDtypeStruct((b, t, heads * v_dim), F32),
        compiler_params=_cparams(("parallel", "parallel", "parallel", "arbitrary")),
        name="attn",
    )(slopes, q3, k3, v3, lq1, lk1, lq2, lk2, subln_g)


EXP_ZERO_ARG = 104.0
MAX_FIXED_SHIFT = 40.0
N_AUG = 6
ATTN_BLOCK = 512
ATTN_GROUP_SIZES = (4, 2, 1)
ATTN_KEY_SPLIT = 2
ATTN_V_PAD = 16


def _attn_fixed_kernel(w_ref, slope_ref, mb_ref, q_ref, k_ref, vt_ref, dbias_ref, qaug_ref, kaug_ref,
                       lq1_ref, lk1_ref, lq2_ref, lk2_ref, sg_ref, o_ref, acc_ref, *, qk_dim, lambda_init):
    h = pl.program_id(1)
    i = pl.program_id(2)
    tb = q_ref.shape[0]
    nk = k_ref.shape[0] // tb
    v_dim = 2 * qk_dim
    slope = slope_ref[h]
    mb = mb_ref[0]
    w = w_ref[h]
    acc_ref[...] = jnp.zeros_like(acc_ref)

    q = q_ref[...]
    lane = lax.broadcasted_iota(jnp.int32, q.shape, 1)
    lane_row = lax.broadcasted_iota(jnp.int32, (1, q.shape[1]), 1)
    is_c0 = lane < qk_dim

    def blocks(items, qvar, kvar, diag):
        sub = tb // ATTN_KEY_SPLIT
        datas = (is_c0, jnp.logical_not(is_c0))
        qas = [jnp.where(datas[c], q, qaug_ref[qvar, c]) for c in range(2)]
        chains = []
        for n, (j, delta) in enumerate(items):
            kblk = k_ref[pl.ds(pl.multiple_of(j * tb, tb), tb), :]
            vtblk = vt_ref[j]
            for c in range(2):
                base = qk_dim * (1 - c)
                dyn = jnp.where(lane_row == base + 4, -mb,
                                jnp.where(lane_row == base + 5, -(slope * tb) * delta, 0.0)).astype(BF16)
                ka = jnp.where(datas[c], kblk, kaug_ref[kvar, c] + dyn)
                for u in range(ATTN_KEY_SPLIT):
                    rows = slice(u * sub, (u + 1) * sub)
                    chains.append((c, u, qas[c], ka[rows], vtblk[:, rows]))
        sts = [lax.dot_general(ka, qa, (((1,), (1,)), ((), ())), preferred_element_type=F32)
               for _, _, qa, ka, _ in chains]
        if diag:
            sts = [st - dbias_ref[u * sub:(u + 1) * sub, :] for st, (_, u, _, _, _) in zip(sts, chains)]
        pts = [jnp.exp(st).astype(BF16) for st in sts]
        for c in range(2):
            upd = None
            for pt, (cc, _, _, _, vsub) in zip(pts, chains):
                if cc == c:
                    part = _dot(vsub, pt)
                    upd = part if upd is None else upd + part
            acc_ref[c] += upd

    def span(first, count, qvar, kvar, dist):
        for per in ATTN_GROUP_SIZES:
            def group(g, carry, first=first, per=per):
                j0 = first + g * per
                blocks([(j0 + n, dist(j0 + n)) for n in range(per)], qvar, kvar, False)
                return carry

            groups = count // per
            lax.fori_loop(0, groups, group, 0)
            first = first + groups * per
            count = count - groups * per

    lo = jnp.maximum(i - w, 0)
    hi = jnp.minimum(i + w, nk - 1)
    span(lo, i - lo, 0, 0, lambda j: (i - j).astype(F32))
    blocks([(i, jnp.zeros((), F32))], 2, 0, True)
    span(i + 1, hi - i, 1, 1, lambda j: (j - i).astype(F32))

    lam = (jnp.exp(jnp.sum(lq1_ref[...] * lk1_ref[...], axis=-1, keepdims=True))
           - jnp.exp(jnp.sum(lq2_ref[...] * lk2_ref[...], axis=-1, keepdims=True)) + lambda_init)
    a0 = acc_ref[0]
    a1 = acc_ref[1]
    ot = a0[:v_dim] / a0[v_dim:v_dim + 1] - lam * (a1[:v_dim] / a1[v_dim:v_dim + 1])
    ms = jnp.mean(ot * ot, axis=0, keepdims=True)
    o_ref[...] = (ot * lax.rsqrt(ms + NORM_EPS) * (sg_ref[...] * (1.0 - lambda_init))).T


def _attn_aug_tables(slopes, tb, qk_dim):
    heads = len(slopes)
    lanes = 2 * qk_dim
    pos = np.arange(tb)
    qaug = np.zeros((heads, 3, 2, tb, lanes), np.float32)
    kaug = np.zeros((heads, 2, 2, tb, lanes), np.float32)
    for hd, sl in enumerate(slopes):
        for var in range(3):
            rr = pos if var != 1 else tb - 1 - pos
            hi = (rr // 16) * (16.0 * sl)
            lo = (rr % 16) * sl
            for c in range(2):
                base = qk_dim * (1 - c)
                if var == 2:
                    qaug[hd, var, c, :, base + 4] = 1.0
                    continue
                qaug[hd, var, c, :, base + 0] = -hi
                qaug[hd, var, c, :, base + 1] = -lo
                qaug[hd, var, c, :, base + 2:base + N_AUG] = 1.0
                kaug[hd, var, c, :, base + 0:base + 2] = 1.0
                kaug[hd, var, c, :, base + 2] = hi
                kaug[hd, var, c, :, base + 3] = lo
    return jnp.asarray(qaug, BF16), jnp.asarray(kaug, BF16)


def _attention_fixed(mb, q3, k3, vt, lq1, lk1, lq2, lk2, subln_g, heads, v_dim, lambda_init, tb):
    b, t, _ = q3.shape
    qk_dim = v_dim // 2
    slopes = [2.0 ** (-8.0 * (hd + 1) / heads) for hd in range(heads)]
    widths = [int(math.ceil((EXP_ZERO_ARG / s - 1.0) / tb)) for s in slopes]
    qaug, kaug = _attn_aug_tables(slopes, tb, qk_dim)
    pos = np.arange(tb)
    dist = np.abs(pos[:, None] - pos[None, :]).astype(np.float32)
    dbias = jnp.asarray(np.stack([dist * s for s in slopes]), F32)
    nb = t // tb
    vrows = v_dim + ATTN_V_PAD
    assert vt.shape == (b, heads, nb, vrows, tb)
    const = lambda bi, h, i, *_: (0, 0)
    grid_spec = pltpu.PrefetchScalarGridSpec(
        num_scalar_prefetch=3,
        grid=(b, heads, t // tb),
        in_specs=[
            pl.BlockSpec((None, tb, v_dim), lambda bi, h, i, *_: (bi, i, h)),
            pl.BlockSpec((None, t, v_dim), lambda bi, h, i, *_: (bi, 0, h)),
            pl.BlockSpec((None, None, nb, vrows, tb), lambda bi, h, i, *_: (bi, h, 0, 0, 0)),
            pl.BlockSpec((None, tb, tb), lambda bi, h, i, *_: (h, 0, 0)),
            pl.BlockSpec((None, 3, 2, tb, v_dim), lambda bi, h, i, *_: (h, 0, 0, 0, 0)),
            pl.BlockSpec((None, 2, 2, tb, v_dim), lambda bi, h, i, *_: (h, 0, 0, 0, 0)),
            pl.BlockSpec(lq1.shape, const),
            pl.BlockSpec(lk1.shape, const),
            pl.BlockSpec(lq2.shape, const),
            pl.BlockSpec(lk2.shape, const),
            pl.BlockSpec((v_dim, 1), const),
        ],
        out_specs=pl.BlockSpec((None, tb, v_dim), lambda bi, h, i, *_: (bi, i, h)),
        scratch_shapes=[pltpu.VMEM((2, vrows, tb), F32)],
    )
    return pl.pallas_call(
        functools.partial(_attn_fixed_kernel, qk_dim=qk_dim, lambda_init=lambda_init),
        grid_spec=grid_spec,
        out_shape=jax.ShapeDtypeStruct((b, t, heads * v_dim), F32),
        compiler_params=_cparams(("parallel", "parallel", "parallel")),
        name="attn_fixed",
    )(jnp.asarray(widths, jnp.int32), jnp.asarray(slopes, F32), mb.reshape(1).astype(F32),
      q3, k3, vt, dbias, qaug, kaug, lq1, lk1, lq2, lk2, subln_g.reshape(v_dim, 1))


def _post_kernel(yf_ref, yb_ref, bonus_ref, g_ref, oda_ref, x_ref, lng_ref, lnb_ref, gmean_ref, wo_rw_ref, wo_da_ref,
                 gffn_ref, wr_ref, br_ref, h_ref, m_ref, idx_ref, gate_ref, cnt_ref, *, gn_eps):
    @pl.when(pl.program_id(0) == 0)
    def _():
        cnt_ref[...] = jnp.zeros_like(cnt_ref)

    gmean = gmean_ref[...]
    y = yf_ref[...] + yb_ref[...]
    mean = _dot_hilo_rhs(y, gmean)
    yc = y - mean
    var = _dot_hilo_rhs(yc * yc, gmean)
    yn = yc * lax.rsqrt(var + gn_eps) * lng_ref[...] + lnb_ref[...]
    o_rw = (yn + bonus_ref[...]) * g_ref[...]
    h = x_ref[...] + _dot(o_rw.astype(BF16), wo_rw_ref[...]) + _dot(oda_ref[...].astype(BF16), wo_da_ref[...])
    h_ref[...] = h
    ms = jnp.mean(h * h, axis=-1, keepdims=True)
    m = h * lax.rsqrt(ms + NORM_EPS) * gffn_ref[...]
    _store_rows(m_ref, m)

    logits = _dot_hilo(m, wr_ref[...]) + br_ref[...]
    ne = logits.shape[1]
    lane = lax.broadcasted_iota(jnp.int32, logits.shape, 1).astype(F32)
    kcol = lax.broadcasted_iota(jnp.int32, (logits.shape[0], TOP_K), 1)
    idx_all = jnp.zeros((logits.shape[0], TOP_K), F32)
    val_all = jnp.zeros((logits.shape[0], TOP_K), F32)
    work = logits
    chosen = jnp.zeros(logits.shape, F32)
    for kk in range(TOP_K):
        mx = jnp.max(work, axis=-1, keepdims=True)
        sel = jnp.min(jnp.where(work == mx, lane, float(ne)), axis=-1, keepdims=True)
        idx_all = jnp.where(kcol == kk, sel, idx_all)
        val_all = jnp.where(kcol == kk, mx, val_all)
        hit = lane == sel
        chosen = jnp.where(hit, 1.0, chosen)
        work = jnp.where(hit, -jnp.inf, work)
    e = jnp.exp(val_all - jnp.max(val_all, axis=-1, keepdims=True))
    idx_ref[...] = idx_all.astype(jnp.int32)
    gate_ref[...] = e / jnp.sum(e, axis=-1, keepdims=True)
    cnt_ref[...] += jnp.sum(chosen, axis=0, keepdims=True)


def _post(yf, yb, bonus, g, oda, x2, lng, lnb, gmean, wo_rw, wo_da, gffn, wr, br, gn_eps, tm):
    n, c = yf.shape
    d = x2.shape[1]
    ne = wr.shape[1]
    full = lambda i: (0, 0)
    row = lambda i: (i, 0)
    return pl.pallas_call(
        functools.partial(_post_kernel, gn_eps=gn_eps),
        grid=(n // tm,),
        in_specs=[
            pl.BlockSpec((tm, c), row),
            pl.BlockSpec((tm, c), row),
            pl.BlockSpec((tm, c), row),
            pl.BlockSpec((tm, c), row),
            pl.BlockSpec((tm, c), row),
            pl.BlockSpec((tm, d), row),
            pl.BlockSpec((1, c), full),
            pl.BlockSpec((1, c), full),
            pl.BlockSpec((c, c), full),
            pl.BlockSpec((c, d), full),
            pl.BlockSpec((c, d), full),
            pl.BlockSpec((1, d), full),
            pl.BlockSpec((d, ne), full),
            pl.BlockSpec((1, ne), full),
        ],
        out_specs=[
            pl.BlockSpec((tm, d), row),
            pl.BlockSpec((tm * d // V7X_LANES, V7X_LANES), row),
            pl.BlockSpec((tm, TOP_K), row),
            pl.BlockSpec((tm, TOP_K), row),
            pl.BlockSpec((1, ne), full),
        ],
        out_shape=[
            jax.ShapeDtypeStruct((n, d), F32),
            jax.ShapeDtypeStruct((n * d // V7X_LANES, V7X_LANES), F32),
            jax.ShapeDtypeStruct((n, TOP_K), jnp.int32),
            jax.ShapeDtypeStruct((n, TOP_K), F32),
            jax.ShapeDtypeStruct((1, ne), F32),
        ],
        compiler_params=_cparams(("arbitrary",)),
        name="post",
    )(yf, yb, bonus, g, oda, x2, lng, lnb, gmean, wo_rw, wo_da, gffn, wr, br)


def _rank_kernel(idx_ref, ltri_ref, offs_ref, slot_ref, carry_ref):
    @pl.when(pl.program_id(0) == 0)
    def _():
        carry_ref[...] = offs_ref[...]

    idx = idx_ref[...]
    tr = idx.shape[0]
    lane = lax.broadcasted_iota(jnp.int32, (tr, N_EXPERTS), 1)
    hots = [jnp.where(lane == idx[:, kk:kk + 1], 1.0, 0.0) for kk in range(TOP_K)]
    mask = hots[0]
    for kk in range(1, TOP_K):
        mask = mask + hots[kk]
    before = _dot(ltri_ref[...], mask.astype(BF16)) + carry_ref[...]
    kcol = lax.broadcasted_iota(jnp.int32, (tr, TOP_K), 1)
    slot = jnp.zeros((tr, TOP_K), F32)
    for kk in range(TOP_K):
        rk = jnp.sum(hots[kk] * before, axis=-1, keepdims=True)
        slot = jnp.where(kcol == kk, rk, slot)
    slot_ref[...] = slot.astype(jnp.int32)
    carry_ref[...] = carry_ref[...] + jnp.sum(mask, axis=0, keepdims=True)


def _moe_rank(idx, offs_row, tr):
    n = idx.shape[0]
    ltri = jnp.asarray(np.tril(np.ones((tr, tr), np.float32), -1), BF16)
    return pl.pallas_call(
        _rank_kernel,
        grid=(n // tr,),
        in_specs=[pl.BlockSpec((tr, TOP_K), lambda i: (i, 0)), pl.BlockSpec((tr, tr), lambda i: (0, 0)),
                  pl.BlockSpec((1, N_EXPERTS), lambda i: (0, 0))],
        out_specs=pl.BlockSpec((tr, TOP_K), lambda i: (i, 0)),
        out_shape=jax.ShapeDtypeStruct((n, TOP_K), jnp.int32),
        scratch_shapes=[pltpu.VMEM((1, N_EXPERTS), F32)],
        compiler_params=_cparams(("arbitrary",)),
        name="moe_rank",
    )(idx, ltri, offs_row)


ROW_UNROLL = 8


def _row(r, c):
    return pl.ds(pl.multiple_of(r * c, c), c)


def _dispatch_kernel(slot_ref, m_ref, xs_ref, sem, *, c):
    td = m_ref.shape[0] // c

    def issue(g, carry):
        for u in range(ROW_UNROLL):
            t = g * ROW_UNROLL + u
            for kk in range(TOP_K):
                dest = slot_ref[t * TOP_K + kk]
                pltpu.make_async_copy(m_ref.at[_row(t, c), :], xs_ref.at[_row(dest, c), :], sem).start(
                    priority=kk % 2)
        return carry

    lax.fori_loop(0, td // ROW_UNROLL, issue, 0)
    for kk in range(TOP_K):
        pltpu.make_async_copy(m_ref, xs_ref.at[pl.ds(0, td * c), :], sem).wait()


def _dispatch(slot_flat, mrows, c, p_rows, td):
    n = mrows.shape[0] // c
    assert td % ROW_UNROLL == 0
    return pl.pallas_call(
        functools.partial(_dispatch_kernel, c=c),
        grid=(n // td,),
        in_specs=[
            pl.BlockSpec((td * TOP_K,), lambda i: (i,), memory_space=pltpu.SMEM),
            pl.BlockSpec((td * c, V7X_LANES), lambda i: (i, 0)),
        ],
        out_specs=pl.BlockSpec(memory_space=pl.ANY),
        scratch_shapes=[pltpu.SemaphoreType.DMA(())],
        out_shape=jax.ShapeDtypeStruct((p_rows * c, V7X_LANES), F32),
        compiler_params=_cparams(("arbitrary",), has_side_effects=True),
        name="moe_dispatch",
    )(slot_flat, mrows)


def _ffn_kernel(be_ref, nu_ref, x_ref, w1_ref, b1_ref, w2_ref, b2_ref, y_ref, w1s_ref, w2s_ref, *, d_ff):
    i = pl.program_id(0)

    @pl.when((i == 0) | (be_ref[i] != be_ref[jnp.maximum(i - 1, 0)]))
    def _():
        w1s_ref[...] = w1_ref[...].astype(BF16)
        w2s_ref[...] = w2_ref[...].astype(BF16)

    @pl.when(i < nu_ref[0])
    def _():
        x = _load_rows(x_ref, w1_ref.shape[0] // V7X_LANES).astype(BF16)
        hcat = _dot(x, w1s_ref[...]) + b1_ref[...]
        glu = jnp.minimum(hcat[:, :d_ff], SWIGLU_LIMIT)
        lin = jnp.clip(hcat[:, d_ff:], -SWIGLU_LIMIT, SWIGLU_LIMIT)
        act = glu * jax.nn.sigmoid(SWIGLU_ALPHA * glu) * (lin + 1.0)
        _store_rows(y_ref, _dot(act.astype(BF16), w2s_ref[...]) + b2_ref[...])


def _moe_ffn(blk_e, n_used, xrows, w1, b1, w2, b2, bm):
    ne, d, f2 = w1.shape
    d_ff = f2 // 2
    c = d // V7X_LANES
    p_rows = xrows.shape[0] // c
    blk = (bm * c, V7X_LANES)
    grid_spec = pltpu.PrefetchScalarGridSpec(
        num_scalar_prefetch=2,
        grid=(p_rows // bm,),
        in_specs=[
            pl.BlockSpec(blk, lambda i, be, nu: (jnp.minimum(i, nu[0] - 1), 0)),
            pl.BlockSpec((None, d, f2), lambda i, be, nu: (be[i], 0, 0)),
            pl.BlockSpec((None, 1, f2), lambda i, be, nu: (be[i], 0, 0)),
            pl.BlockSpec((None, d_ff, d), lambda i, be, nu: (be[i], 0, 0)),
            pl.BlockSpec((None, 1, d), lambda i, be, nu: (be[i], 0, 0)),
        ],
        out_specs=pl.BlockSpec(blk, lambda i, be, nu: (i, 0)),
        scratch_shapes=[pltpu.VMEM((d, f2), BF16), pltpu.VMEM((d_ff, d), BF16)],
    )
    return pl.pallas_call(
        functools.partial(_ffn_kernel, d_ff=d_ff),
        grid_spec=grid_spec,
        out_shape=jax.ShapeDtypeStruct(xrows.shape, F32),
        compiler_params=_cparams(("arbitrary",)),
        name="moe_ffn",
    )(blk_e, n_used, xrows, w1, b1.reshape(ne, 1, f2), w2, b2.reshape(ne, 1, d))


def _combine_kernel(slot_ref, next_slot_ref, h_ref, gate_ref, p_ref, wp_ref, gp_ref, wg_ref, y_ref, o_ref,
                    buf_ref, h2_ref, sems):
    i = pl.program_id(0)
    tc, d = h_ref.shape
    c = d // V7X_LANES

    def start_gather(slots, half):
        def issue(g, carry):
            for u in range(ROW_UNROLL):
                t = g * ROW_UNROLL + u
                for kk in range(TOP_K):
                    src = slots[t * TOP_K + kk]
                    pltpu.make_async_copy(y_ref.at[_row(src, c), :], buf_ref.at[half, kk, _row(t, c), :],
                                          sems.at[half]).start(priority=kk % 2)
            return carry

        lax.fori_loop(0, tc // ROW_UNROLL, issue, 0)

    cur = i % 2

    @pl.when(i == 0)
    def _():
        start_gather(slot_ref, 0)

    @pl.when(i + 1 < pl.num_programs(0))
    def _():
        start_gather(next_slot_ref, 1 - cur)

    for kk in range(TOP_K):
        pltpu.make_async_copy(y_ref.at[pl.ds(0, tc * c), :], buf_ref.at[cur, kk], sems.at[cur]).wait()

    gate = gate_ref[...]
    for j in range(c):
        cols = slice(j * V7X_LANES, (j + 1) * V7X_LANES)
        acc = h_ref[:, cols]
        for kk in range(TOP_K):
            acc = acc + gate[:, kk:kk + 1] * buf_ref[cur, kk, pl.ds(j, tc, stride=c), :]
        h2_ref[:, cols] = acc

    h2 = h2_ref[...]
    e = _dot(p_ref[...].astype(BF16), wp_ref[...])
    ms = jnp.mean(e * e, axis=-1, keepdims=True)
    e = e * lax.rsqrt(ms + NORM_EPS) * gp_ref[...]
    o_ref[...] = h2 + e * jax.nn.sigmoid(_dot(h2.astype(BF16), wg_ref[...]))


def _combine_ple(slot_flat, h, gates, yrows, p2, wp_bf, gp, wg_bf, tc):
    n, d = h.shape
    pd = p2.shape[1]
    assert tc % ROW_UNROLL == 0
    last = n // tc - 1
    full = lambda i: (0, 0)
    row = lambda i: (i, 0)
    return pl.pallas_call(
        _combine_kernel,
        grid=(n // tc,),
        in_specs=[
            pl.BlockSpec((tc * TOP_K,), lambda i: (i,), memory_space=pltpu.SMEM),
            pl.BlockSpec((tc * TOP_K,), lambda i: (jnp.minimum(i + 1, last),), memory_space=pltpu.SMEM),
            pl.BlockSpec((tc, d), row),
            pl.BlockSpec((tc, TOP_K), row),
            pl.BlockSpec((tc, pd), row),
            pl.BlockSpec((pd, d), full),
            pl.BlockSpec((1, d), full),
            pl.BlockSpec((d, d), full),
            pl.BlockSpec(memory_space=pl.ANY),
        ],
        out_specs=pl.BlockSpec((tc, d), row),
        scratch_shapes=[pltpu.VMEM((2, TOP_K, tc * d // V7X_LANES, V7X_LANES), F32),
                        pltpu.VMEM((tc, d), F32),
                        pltpu.SemaphoreType.DMA((2,))],
        out_shape=jax.ShapeDtypeStruct((n, d), F32),
        compiler_params=_cparams(("arbitrary",)),
        name="moe_combine_ple",
    )(slot_flat, slot_flat, h, gates, p2, wp_bf, gp, wg_bf, yrows)


def _tile(n, want):
    t = min(n, want)
    assert n % t == 0, (n, t)
    return t


def _layer(h3, p3, lambda_init, norm_mix_g, w_in, rw_mu, rw_w0, rw_w2, rw_a0, rw_a2, rw_g2, rw_k_k, rw_k_a, rw_r_k,
           rw_ln_g, rw_ln_b, da_q_norm_g, da_k_norm_g, da_lq1, da_lk1, da_lq2, da_lk2, da_subln_g, w_out,
           norm_ffn_g, w_router, b_router, w1, b1, w2, b2, w_ple, ple_norm_g, w_ple_gate):
    b, t, d = h3.shape
    n = b * t
    c_rw = rw_k_k.shape[0]
    lora = rw_w2.shape[1]
    rw_cols = rw_mu.shape[0]
    rw_heads = rw_r_k.shape[0]
    c_da = w_in.shape[1] - rw_cols
    assert c_da % 3 == 0
    c_da //= 3
    qk_dim = da_q_norm_g.shape[0]
    v_dim = da_subln_g.shape[0]
    da_heads = c_da // v_dim
    assert rw_heads * RW_HEAD == c_rw and c_rw % HEAD_GROUP == 0 and v_dim == V7X_LANES and 2 * qk_dim == v_dim
    assert 2 * lora == V7X_LANES and w_router.shape[1] == N_EXPERTS

    x2 = h3.reshape(n, d)
    row = lambda a: a.reshape(1, -1).astype(F32)

    gq_t = row(jnp.tile(da_q_norm_g, c_da // qk_dim))
    gk_t = row(jnp.tile(da_k_norm_g, c_da // qk_dim))
    gm_qk = _group_matrix(c_da, qk_dim, 1.0 / qk_dim)
    ta = _tile(t, ATTN_BLOCK)
    zrw, q, k, vt = _inproj(x2, row(norm_mix_g), w_in.astype(BF16), gq_t, gk_t, gm_qk, rw_cols, c_da,
                            1.0 / math.sqrt(qk_dim), _tile(ta, 256), b, da_heads, ta)

    zeros = jnp.zeros((lora, c_rw), F32)
    w2bd = jnp.concatenate([jnp.concatenate([rw_w2[0], zeros], axis=1),
                            jnp.concatenate([zeros, rw_w2[1]], axis=1)], axis=0).astype(BF16)
    a2bd = jnp.concatenate([jnp.concatenate([rw_a2[0], zeros], axis=1),
                            jnp.concatenate([zeros, rw_a2[1]], axis=1)], axis=0).astype(BF16)
    gsum = _group_matrix(c_rw, RW_HEAD, 1.0)
    r, vv, nkk, lw, kd, bd, g, bonus = _rwkv_prep(
        zrw.reshape(b, t, rw_cols), row(rw_mu), w2bd, row(rw_w0), a2bd, row(rw_a0), rw_g2.astype(BF16),
        row(rw_k_k), row(rw_k_a), row(rw_r_k), gsum, c_rw, lora, _tile(t, 256))
    yf, yb = _rwkv_scan(r, vv, nkk, lw, kd, bd, _tile(t, 128))

    slopes = 2.0 ** (-8.0 * jnp.arange(1, da_heads + 1, dtype=F32) / da_heads)
    q3 = q.reshape(b, t, c_da)
    k3 = k.reshape(b, t, c_da)
    small = (row(da_lq1), row(da_lk1), row(da_lq2), row(da_lk2), row(da_subln_g))

    def attention_running_max():
        v3 = vt[:, :, :, :v_dim, :].transpose(0, 2, 4, 1, 3).reshape(b, t, c_da)
        return _attention(slopes, q3, k3, v3, *small, da_heads, v_dim, lambda_init, ta, ta)

    score_bound = 1.01 * math.sqrt(qk_dim) * jnp.max(jnp.abs(da_q_norm_g)) * jnp.max(jnp.abs(da_k_norm_g))
    mb = jnp.ceil(2.0 * score_bound.astype(F32)) * 0.5
    o_da = lax.cond(
        mb <= MAX_FIXED_SHIFT,
        lambda: _attention_fixed(mb, q3, k3, vt, *small, da_heads, v_dim, lambda_init, ta),
        attention_running_max)

    gmean = _group_matrix(c_rw, RW_HEAD, 1.0 / RW_HEAD)
    w_out_bf = w_out.astype(BF16)
    h1, m, idx, gates, counts = _post(
        yf.reshape(n, c_rw), yb.reshape(n, c_rw), bonus.reshape(n, c_rw), g.reshape(n, c_rw), o_da.reshape(n, c_da), x2,
        row(rw_ln_g), row(rw_ln_b), gmean, w_out_bf[:c_rw], w_out_bf[c_rw:], row(norm_ffn_g),
        w_router.astype(F32), row(b_router), RW_HEAD * 1e-5, _tile(n, 256))

    bm = MOE_BLOCK_ROWS
    counts = counts.reshape(N_EXPERTS).astype(jnp.int32)
    padded = (counts + bm - 1) // bm * bm
    pend = jnp.cumsum(padded)
    offs = (pend - padded).astype(jnp.int32)
    slot = _moe_rank(idx, offs.astype(F32).reshape(1, N_EXPERTS), _tile(n, 512))
    n_blk = (n * TOP_K) // bm + N_EXPERTS
    blk_start = jnp.arange(n_blk, dtype=jnp.int32) * bm
    blk_e = jnp.minimum(jnp.sum(blk_start[:, None] >= pend[None, :], axis=1), N_EXPERTS - 1).astype(jnp.int32)
    n_used = (pend[-1] // bm).astype(jnp.int32).reshape(1)
    slot_flat = slot.reshape(n * TOP_K)
    xs = _dispatch(slot_flat, m, d // V7X_LANES, n_blk * bm, _tile(n, 256))
    ys = _moe_ffn(blk_e, n_used, xs, w1, b1, w2, b2, bm)
    out = _combine_ple(slot_flat, h1, gates, ys, p3.reshape(n, -1), w_ple.astype(BF16), row(ple_norm_g),
                       w_ple_gate.astype(BF16), _tile(n, 256))
    return out.reshape(b, t, d)


def kernel(x, p, norm_mix_g, w_in, rw_mu, rw_w0, rw_w2, rw_a0, rw_a2, rw_g2, rw_k_k, rw_k_a, rw_r_k, rw_ln_g,
           rw_ln_b, da_q_norm_g, da_k_norm_g, da_lq1, da_lk1, da_lq2, da_lk2, da_subln_g, w_out, norm_ffn_g,
           w_router, b_router, w1, b1, w2, b2, w_ple, ple_norm_g, w_ple_gate):
    h = x.astype(F32)
    params = (norm_mix_g, w_in, rw_mu, rw_w0, rw_w2, rw_a0, rw_a2, rw_g2, rw_k_k, rw_k_a, rw_r_k, rw_ln_g, rw_ln_b,
              da_q_norm_g, da_k_norm_g, da_lq1, da_lk1, da_lq2, da_lk2, da_subln_g, w_out, norm_ffn_g,
              w_router, b_router, w1, b1, w2, b2, w_ple, ple_norm_g, w_ple_gate)
    for i in range(p.shape[0]):
        lambda_init = 0.8 - 0.6 * math.exp(-0.3 * i)
        h = _layer(h, p[i], lambda_init, *(a[i] for a in params))
    return h.astype(x.dtype)
```

```python
import functools
import math

import jax
import jax.numpy as jnp
import numpy as np
from jax import lax
from jax.experimental import pallas as pl
from jax.experimental.pallas import tpu as pltpu

F32 = jnp.float32
BF16 = jnp.bfloat16

V7X_LANES = 128
V7X_VMEM_BYTES = 64 * 1024 * 1024
VMEM_LIMIT = 52 * 1024 * 1024

NORM_EPS = 1e-6
RW_HEAD = 64
CHUNK = 64
HEAD_GROUP = 256
N_EXPERTS = 32
TOP_K = 4
MOE_BLOCK_ROWS = 512
SWIGLU_ALPHA = 1.702
SWIGLU_LIMIT = 7.0


def _cparams(sem, **kw):
    return pltpu.CompilerParams(dimension_semantics=sem, vmem_limit_bytes=VMEM_LIMIT, **kw)


def _dot(a, b):
    return jnp.dot(a, b, preferred_element_type=F32)


def _dot_hilo_lhs(g_bf16, x):
    hi, lo = _split2(x)
    return _dot(g_bf16, hi) + _dot(g_bf16, lo)


def _split2(x):
    hi = x.astype(BF16)
    return hi, (x - hi.astype(F32)).astype(BF16)


def _dot_hilo_rhs(x, g_bf16):
    hi, lo = _split2(x)
    out = _dot(jnp.concatenate([hi, lo], axis=0), g_bf16)
    return out[:x.shape[0]] + out[x.shape[0]:]


def _dot_hilo(a, b):
    ah, al = _split2(a)
    bh, bl = _split2(b)
    return _dot(ah, bh) + (_dot(ah, bl) + _dot(al, bh))


def _store_rows(ref, x2):
    rows, d = x2.shape
    c = d // V7X_LANES
    for j in range(c):
        ref[pl.ds(j, rows, stride=c), :] = x2[:, j * V7X_LANES:(j + 1) * V7X_LANES]


def _load_rows(ref, c):
    rows = ref.shape[0] // c
    return jnp.concatenate([ref[pl.ds(j, rows, stride=c), :] for j in range(c)], axis=1)


def _group_matrix(width, group, value):
    g = np.arange(width) // group
    return jnp.asarray(np.where(g[:, None] == g[None, :], value, 0.0), BF16)


def _inproj_kernel(x_ref, g_ref, w_ref, gq_ref, gk_ref, gm_ref, zrw_ref, q_ref, k_ref, vt_ref, *, rw_cols, c_da, scale):
    x = x_ref[...]
    ms = jnp.mean(x * x, axis=-1, keepdims=True)
    u = (x * lax.rsqrt(ms + NORM_EPS) * g_ref[...]).astype(BF16)
    z = _dot(u, w_ref[...])
    zrw_ref[...] = z[:, :rw_cols]
    zq = z[:, rw_cols:rw_cols + c_da]
    zk = z[:, rw_cols + c_da:rw_cols + 2 * c_da]
    zv = z[:, rw_cols + 2 * c_da:]
    tm = zq.shape[0]
    msqk = _dot(jnp.concatenate([zq * zq, zk * zk], axis=0).astype(BF16), gm_ref[...])
    q_ref[...] = (zq * lax.rsqrt(msqk[:tm] + NORM_EPS) * (gq_ref[...] * scale)).astype(BF16)
    k_ref[...] = (zk * lax.rsqrt(msqk[tm:] + NORM_EPS) * gk_ref[...]).astype(BF16)
    heads, vrows, _ = vt_ref.shape
    v_dim = c_da // heads
    pad_row = lax.broadcasted_iota(jnp.int32, (vrows - v_dim, tm), 0)
    pad = jnp.where(pad_row == 0, 1.0, 0.0).astype(BF16)
    for hd in range(heads):
        vt_ref[hd, 0:v_dim, :] = zv[:, hd * v_dim:(hd + 1) * v_dim].T.astype(BF16)
        vt_ref[hd, v_dim:, :] = pad


def _inproj(x2, g, w_bf, gq_t, gk_t, gm, rw_cols, c_da, scale, tm, b, heads, tb):
    n, d = x2.shape
    cols = w_bf.shape[1]
    t = n // b
    tps = t // tm
    per = tb // tm
    vrows = c_da // heads + ATTN_V_PAD
    full = lambda i: (0, 0)
    row = lambda i: (i, 0)
    return pl.pallas_call(
        functools.partial(_inproj_kernel, rw_cols=rw_cols, c_da=c_da, scale=scale),
        grid=(n // tm,),
        in_specs=[
            pl.BlockSpec((tm, d), row),
            pl.BlockSpec((1, d), full),
            pl.BlockSpec((d, cols), full),
            pl.BlockSpec((1, c_da), full),
            pl.BlockSpec((1, c_da), full),
            pl.BlockSpec((c_da, c_da), full),
        ],
        out_specs=[
            pl.BlockSpec((tm, rw_cols), row),
            pl.BlockSpec((tm, c_da), row),
            pl.BlockSpec((tm, c_da), row),
            pl.BlockSpec((None, heads, None, vrows, tm),
                         lambda i: (i // tps, 0, (i % tps) // per, 0, (i % tps) % per)),
        ],
        out_shape=[
            jax.ShapeDtypeStruct((n, rw_cols), F32),
            jax.ShapeDtypeStruct((n, c_da), BF16),
            jax.ShapeDtypeStruct((n, c_da), BF16),
            jax.ShapeDtypeStruct((b, heads, t // tb, vrows, tb), BF16),
        ],
        compiler_params=_cparams(("parallel",)),
        name="inproj",
    )(x2, g, w_bf, gq_t, gk_t, gm)


def _rwkv_prep_kernel(z_ref, zp_ref, zn_ref, mu_ref, w2_ref, w0_ref, a2_ref, a0_ref, g2_ref, kk_ref, ka_ref,
                      rk_ref, gsum_ref,
                      r_ref, v_ref, nkk_ref, lw_ref, kd_ref, bd_ref, g_ref, bonus_ref, *, c_rw, lora):
    i = pl.program_id(1)
    nt = pl.num_programs(1)
    z = z_ref[...]
    tt = z.shape[0]
    rows = lax.broadcasted_iota(jnp.int32, z.shape, 0)
    prev_row = jnp.where(i == 0, 0.0, zp_ref[7:8, :])
    next_row = jnp.where(i == nt - 1, 0.0, zn_ref[0:1, :])
    zprev = jnp.where(rows == 0, prev_row, pltpu.roll(z, 1, 0))
    znext = jnp.where(rows == tt - 1, next_row, pltpu.roll(z, tt - 1, 0))
    z = z + (0.5 * (zprev + znext) - z) * mu_ref[...]

    r = z[:, 0:c_rw]
    k = z[:, c_rw:2 * c_rw]
    v = z[:, 2 * c_rw:3 * c_rw]
    o = 3 * c_rw
    wd = z[:, o:o + 2 * lora]
    ad = z[:, o + 2 * lora:o + 4 * lora]
    gd = z[:, o + 4 * lora:]

    xw = w0_ref[...] + _dot(jnp.tanh(wd).astype(BF16), w2_ref[...])
    xa = a0_ref[...] + _dot(ad.astype(BF16), a2_ref[...])
    lw = -math.exp(-0.5) * jax.nn.sigmoid(xw)
    rate = jax.nn.sigmoid(xa)
    g = _dot(jax.nn.sigmoid(gd).astype(BF16), g2_ref[...])

    gsum = gsum_ref[...]
    kkr = k * kk_ref[...]
    ka = ka_ref[...]
    kd0 = k * (1.0 + (rate[:, :c_rw] - 1.0) * ka)
    kd1 = k * (1.0 + (rate[:, c_rw:] - 1.0) * ka)
    sums = _dot_hilo_rhs(jnp.concatenate([kkr * kkr, r * (0.5 * (kd0 + kd1)) * rk_ref[...]], axis=0), gsum)
    kk = kkr * lax.rsqrt(sums[:tt] + 1e-12)
    bonus = sums[tt:] * v

    r_ref[...] = r.astype(BF16)
    v_ref[...] = v.astype(BF16)
    nkk_ref[...] = (-kk).astype(BF16)
    lw_ref[0] = lw[:, :c_rw]
    lw_ref[1] = lw[:, c_rw:]
    kd_ref[0] = kd0.astype(BF16)
    kd_ref[1] = kd1.astype(BF16)
    bd_ref[0] = (kk * rate[:, :c_rw]).astype(BF16)
    bd_ref[1] = (kk * rate[:, c_rw:]).astype(BF16)
    g_ref[...] = g
    bonus_ref[...] = bonus


def _rwkv_prep(zrw3, mu, w2bd, w0f, a2bd, a0f, g2_bf, k_k, k_a, r_kf, gsum, c_rw, lora, tt):
    b, t, cols = zrw3.shape
    nt = t // tt
    hb = tt // 8
    nhb = t // 8
    full = lambda bi, i: (0, 0)
    tile = lambda bi, i: (bi, i, 0)
    dtile = lambda bi, i: (0, bi, i, 0)
    one = jax.ShapeDtypeStruct((b, t, c_rw), F32)
    two = jax.ShapeDtypeStruct((2, b, t, c_rw), F32)
    one16 = jax.ShapeDtypeStruct((b, t, c_rw), BF16)
    two16 = jax.ShapeDtypeStruct((2, b, t, c_rw), BF16)
    return pl.pallas_call(
        functools.partial(_rwkv_prep_kernel, c_rw=c_rw, lora=lora),
        grid=(b, nt),
        in_specs=[
            pl.BlockSpec((None, tt, cols), tile),
            pl.BlockSpec((None, 8, cols), lambda bi, i: (bi, jnp.maximum(i * hb - 1, 0), 0)),
            pl.BlockSpec((None, 8, cols), lambda bi, i: (bi, jnp.minimum((i + 1) * hb, nhb - 1), 0)),
            pl.BlockSpec((1, cols), full),
            pl.BlockSpec(w2bd.shape, full),
            pl.BlockSpec((1, 2 * c_rw), full),
            pl.BlockSpec(a2bd.shape, full),
            pl.BlockSpec((1, 2 * c_rw), full),
            pl.BlockSpec(g2_bf.shape, full),
            pl.BlockSpec((1, c_rw), full),
            pl.BlockSpec((1, c_rw), full),
            pl.BlockSpec((1, c_rw), full),
            pl.BlockSpec((c_rw, c_rw), full),
        ],
        out_specs=[
            pl.BlockSpec((None, tt, c_rw), tile),
            pl.BlockSpec((None, tt, c_rw), tile),
            pl.BlockSpec((None, tt, c_rw), tile),
            pl.BlockSpec((2, None, tt, c_rw), dtile),
            pl.BlockSpec((2, None, tt, c_rw), dtile),
            pl.BlockSpec((2, None, tt, c_rw), dtile),
            pl.BlockSpec((None, tt, c_rw), tile),
            pl.BlockSpec((None, tt, c_rw), tile),
        ],
        out_shape=[one16, one16, one16, two, two16, two16, one, one],
        compiler_params=_cparams(("parallel", "parallel")),
        name="rwkv_prep",
    )(zrw3, zrw3, zrw3, mu, w2bd, w0f, a2bd, a0f, g2_bf, k_k, k_a, r_kf, gsum)


M_STRICT, M_INCL, M_EYE, M_PAIR, M_LEVEL = 0, 1, 2, 3, 4
N_LEVELS = 5
SCAN_SEQS = 2


def _scan_tables(w):
    c = CHUNK
    masks = []
    mcums = []
    for d in range(2):
        t = np.broadcast_to(np.arange(c)[:, None], (c, w))
        s = np.broadcast_to((np.arange(w) % RW_HEAD)[None, :], (c, w))
        tt = np.broadcast_to(np.arange(c)[:, None], (c, c))
        ss = np.broadcast_to(np.arange(c)[None, :], (c, c))
        if d == 1:
            t, s, tt, ss = c - 1 - t, c - 1 - s, c - 1 - tt, c - 1 - ss
        rows = [s < t, s <= t, s == t, (t // 2 == s // 2) & (t > s)]
        sz = 2
        while sz < c:
            rows.append((t // (2 * sz) == s // (2 * sz)) & ((t // sz) % 2 == 1) & ((s // sz) % 2 == 0))
            sz *= 2
        assert len(rows) == M_LEVEL + N_LEVELS
        masks.append(np.stack(rows).astype(np.float32))
        mcums.append((ss <= tt).astype(np.float32))
    return jnp.asarray(np.stack(masks), F32), jnp.asarray(np.stack(mcums), BF16)


def _block_diag(y, bd16):
    g = y.shape[1] // RW_HEAD
    return jnp.concatenate([y.astype(BF16)] * g, axis=0) * bd16


def _bmm(x, y, bd16):
    outs = []
    for s in range(0, y.shape[1], HEAD_GROUP):
        outs.append(_dot(x[:, s:s + HEAD_GROUP].astype(BF16), _block_diag(y[:, s:s + HEAD_GROUP], bd16)))
    return jnp.concatenate(outs, axis=1)


def _bmm_nt(x, y, bd16):
    outs = []
    for s in range(0, y.shape[1], HEAD_GROUP):
        bd = _block_diag(y[:, s:s + HEAD_GROUP], bd16)
        outs.append(lax.dot_general(x[:, s:s + HEAD_GROUP].astype(BF16), bd, (((1,), (1,)), ((), ())),
                                    preferred_element_type=F32))
    return jnp.concatenate(outs, axis=1)


def _bmm_tn(x, y, bd32):
    outs = []
    for s in range(0, y.shape[1], HEAD_GROUP):
        xt = x[:, s:s + HEAD_GROUP].T.astype(BF16)
        full = _dot(xt, y[:, s:s + HEAD_GROUP].astype(BF16)) * bd32
        o = full[0:RW_HEAD]
        for j in range(1, HEAD_GROUP // RW_HEAD):
            o = o + full[j * RW_HEAD:(j + 1) * RW_HEAD]
        outs.append(o)
    return jnp.concatenate(outs, axis=1)


def _scan_chunks(r, lw, k, v, a, b, state, masks, mcum, bd16, bd32):
    c = r[0].shape[0]

    def each(f, *cols):
        return [f(*xs) for xs in zip(*cols)]

    cum = each(_dot_hilo_lhs, mcum, lw)
    tot = each(lambda x: jnp.sum(x, axis=0, keepdims=True), lw)
    e_neg = each(lambda x: jnp.exp(-x), cum)
    e_tail = each(lambda t_, x: jnp.exp(t_ - x), tot, cum)
    rt = each(lambda x, cm: x * jnp.exp(cm), r, cum)
    at = each(lambda x, cm, l_: x * jnp.exp(cm - l_), a, cum, lw)
    bt = each(jnp.multiply, b, e_neg)
    kt = each(jnp.multiply, k, e_neg)
    bh = each(jnp.multiply, b, e_tail)
    kh = each(jnp.multiply, k, e_tail)

    ar = each(lambda x, y_: jnp.concatenate([x, y_], axis=0), at, rt)
    pb = each(lambda x, y_: _bmm_nt(x, y_, bd16), ar, bt)
    pk = each(lambda x, y_: _bmm_nt(x, y_, bd16), ar, kt)
    a_ab = each(lambda x, m: x[:c] * m[M_STRICT], pb, masks)
    a_rb = each(lambda x, m: x[c:] * m[M_INCL], pb, masks)
    a_ak = each(lambda x, m: x[:c] * m[M_STRICT], pk, masks)
    a_rk = each(lambda x, m: x[c:] * m[M_INCL], pk, masks)

    tinv = each(lambda x, m: m[M_EYE] + x * m[M_PAIR], a_ab, masks)
    for lvl in range(N_LEVELS):
        x = each(lambda aa, m, ti: _bmm(aa * m[M_LEVEL + lvl], ti, bd16), a_ab, masks, tinv)
        tinv = each(lambda ti, x_: ti + _bmm(ti, x_, bd16), tinv, x)

    bmm = lambda p, q: _bmm(p, q, bd16)
    stack = lambda p, q: jnp.concatenate([p, q], axis=0)
    att = each(bmm, tinv, at)
    akv = each(lambda p, p2, q: _bmm(stack(p, p2), q, bd16), a_ak, a_rk, v)
    vt = each(lambda ti, x_: _bmm(ti, x_[:c], bd16), tinv, akv)
    rh = each(lambda x, p, q: x + _bmm(p, q, bd16), rt, a_rb, att)
    y_in = each(lambda p, q, x_: _bmm(p, q, bd16) + x_[c:], a_rb, vt, akv)
    mc = each(lambda m, t_, p, q: m[M_EYE] * jnp.exp(t_) + _bmm_tn(p, q, bd32), masks, tot, bh, att)
    nc = each(lambda p, p2, q, q2: _bmm_tn(stack(p, p2), stack(q, q2), bd32), bh, kh, vt, v)
    onst = each(lambda p, p2, q: _bmm(stack(p, p2), q, bd16), rh, mc, state)
    y = each(lambda x_, yi: x_[:c] + yi, onst, y_in)
    new_state = each(lambda x_, n_: x_[c:] + n_, onst, nc)
    return y, new_state


def _rwkv_scan_kernel(rf_ref, vf_ref, af_ref, rb_ref, vb_ref, ab_ref, lwf_ref, kf_ref, bf_ref, lwb_ref, kb_ref,
                      bb_ref, mask_ref, mcum_ref, bd16_ref, bd32_ref, yf_ref, yb_ref, state_ref, *, n_chunks):
    @pl.when(pl.program_id(1) == 0)
    def _():
        state_ref[...] = jnp.zeros_like(state_ref)

    dirs = ((0, rf_ref, lwf_ref, kf_ref, vf_ref, af_ref, bf_ref, yf_ref),
            (1, rb_ref, lwb_ref, kb_ref, vb_ref, ab_ref, bb_ref, yb_ref))
    n_seq = rf_ref.shape[0]

    def body(j, carry):
        cols = [[] for _ in range(9)]
        where = []
        for d, r_ref, lw_ref, k_ref, v_ref, a_ref, b_ref, y_ref in dirs:
            cj = j if d == 0 else n_chunks - 1 - j
            sl = pl.ds(pl.multiple_of(cj * CHUNK, CHUNK), CHUNK)
            for g in range(n_seq):
                vals = (r_ref[g, sl, :], lw_ref[g, sl, :], k_ref[g, sl, :], v_ref[g, sl, :], a_ref[g, sl, :],
                        b_ref[g, sl, :], state_ref[d, g], mask_ref.at[d], mcum_ref[d])
                for col, val in zip(cols, vals):
                    col.append(val)
                where.append((y_ref, d, g, sl))
        ys, sts = _scan_chunks(*cols, bd16_ref[...], bd32_ref[...])
        for (y_ref, d, g, sl), y, st in zip(where, ys, sts):
            y_ref[g, sl, :] = y
            state_ref[d, g] = st
        return carry

    lax.fori_loop(0, n_chunks, body, 0)


def _rwkv_scan(r, v, nkk, lw, kd, bd, tb):
    b, t, w = r.shape
    nb = t // tb
    g = SCAN_SEQS if b % SCAN_SEQS == 0 else 1
    masks, mcum = _scan_tables(w)
    bd32 = _group_matrix(HEAD_GROUP, RW_HEAD, 1.0).astype(F32)
    bd16 = bd32.astype(BF16)
    fwd = lambda bi, i: (bi, i, 0)
    bwd = lambda bi, i: (bi, nb - 1 - i, 0)
    fwd_d = lambda bi, i: (0, bi, i, 0)
    bwd_d = lambda bi, i: (1, bi, nb - 1 - i, 0)
    const = lambda nd: (lambda bi, i: (0,) * nd)
    blk = (g, tb, w)
    dblk = (None, g, tb, w)
    out = jax.ShapeDtypeStruct((b, t, w), F32)
    return pl.pallas_call(
        functools.partial(_rwkv_scan_kernel, n_chunks=tb // CHUNK),
        grid=(b // g, nb),
        in_specs=[
            pl.BlockSpec(blk, fwd), pl.BlockSpec(blk, fwd), pl.BlockSpec(blk, fwd),
            pl.BlockSpec(blk, bwd), pl.BlockSpec(blk, bwd), pl.BlockSpec(blk, bwd),
            pl.BlockSpec(dblk, fwd_d), pl.BlockSpec(dblk, fwd_d), pl.BlockSpec(dblk, fwd_d),
            pl.BlockSpec(dblk, bwd_d), pl.BlockSpec(dblk, bwd_d), pl.BlockSpec(dblk, bwd_d),
            pl.BlockSpec(masks.shape, const(4)),
            pl.BlockSpec(mcum.shape, const(3)),
            pl.BlockSpec(bd16.shape, const(2)),
            pl.BlockSpec(bd32.shape, const(2)),
        ],
        out_specs=[pl.BlockSpec(blk, fwd), pl.BlockSpec(blk, bwd)],
        out_shape=[out, out],
        scratch_shapes=[pltpu.VMEM((2, g, CHUNK, w), F32)],
        compiler_params=_cparams(("parallel", "arbitrary")),
        name="rwkv_scan",
    )(r, v, nkk, r, v, nkk, lw, kd, bd, lw, kd, bd, masks, mcum, bd16, bd32)


def _attn_kernel(slope_ref, q_ref, k_ref, v_ref, lq1_ref, lk1_ref, lq2_ref, lk2_ref, sg_ref, o_ref,
                 m_ref, l_ref, acc_ref, *, qk_dim, lambda_init):
    h = pl.program_id(1)
    i = pl.program_id(2)
    j = pl.program_id(3)
    nk = pl.num_programs(3)
    tq = q_ref.shape[0]
    tk = k_ref.shape[0]

    @pl.when(j == 0)
    def _():
        m_ref[...] = jnp.full(m_ref.shape, -jnp.inf, F32)
        l_ref[...] = jnp.zeros_like(l_ref)
        acc_ref[...] = jnp.zeros_like(acc_ref)

    q = q_ref[...]
    k = k_ref[...]
    v = v_ref[...]
    lane = lax.broadcasted_iota(jnp.int32, q.shape, 1)
    zero = jnp.zeros_like(q)
    qi = lax.broadcasted_iota(jnp.int32, (tq, tk), 0) + i * tq
    kj = lax.broadcasted_iota(jnp.int32, (tq, tk), 1) + j * tk
    bias = slope_ref[h] * jnp.abs(qi - kj).astype(F32)
    for c in range(2):
        qc = jnp.where((lane < qk_dim) == (c == 0), q, zero)
        s = lax.dot_general(qc, k, (((1,), (1,)), ((), ())), preferred_element_type=F32) - bias
        m_old = m_ref[c]
        m_new = jnp.maximum(m_old, jnp.max(s, axis=-1, keepdims=True))
        alpha = jnp.exp(m_old - m_new)
        p = jnp.exp(s - m_new)
        l_ref[c] = alpha * l_ref[c] + jnp.sum(p, axis=-1, keepdims=True)
        acc_ref[c] = alpha * acc_ref[c] + _dot(p.astype(BF16), v)
        m_ref[c] = m_new

    @pl.when(j == nk - 1)
    def _():
        lam = (jnp.exp(jnp.sum(lq1_ref[...] * lk1_ref[...], axis=-1, keepdims=True))
               - jnp.exp(jnp.sum(lq2_ref[...] * lk2_ref[...], axis=-1, keepdims=True)) + lambda_init)
        o = acc_ref[0] / l_ref[0] - lam * (acc_ref[1] / l_ref[1])
        ms = jnp.mean(o * o, axis=-1, keepdims=True)
        o_ref[...] = o * lax.rsqrt(ms + NORM_EPS) * (sg_ref[...] * (1.0 - lambda_init))


def _attention(slopes, q3, k3, v3, lq1, lk1, lq2, lk2, subln_g, heads, v_dim, lambda_init, tq, tk):
    b, t, _ = q3.shape
    small = lambda s, bi, h, i, j: (0, 0)
    grid_spec = pltpu.PrefetchScalarGridSpec(
        num_scalar_prefetch=1,
        grid=(b, heads, t // tq, t // tk),
        in_specs=[
            pl.BlockSpec((None, tq, v_dim), lambda bi, h, i, j, s: (bi, i, h)),
            pl.BlockSpec((None, tk, v_dim), lambda bi, h, i, j, s: (bi, j, h)),
            pl.BlockSpec((None, tk, v_dim), lambda bi, h, i, j, s: (bi, j, h)),
            pl.BlockSpec(lq1.shape, lambda bi, h, i, j, s: (0, 0)),
            pl.BlockSpec(lk1.shape, lambda bi, h, i, j, s: (0, 0)),
            pl.BlockSpec(lq2.shape, lambda bi, h, i, j, s: (0, 0)),
            pl.BlockSpec(lk2.shape, lambda bi, h, i, j, s: (0, 0)),
            pl.BlockSpec(subln_g.shape, lambda bi, h, i, j, s: (0, 0)),
        ],
        out_specs=pl.BlockSpec((None, tq, v_dim), lambda bi, h, i, j, s: (bi, i, h)),
        scratch_shapes=[
            pltpu.VMEM((2, tq, 1), F32),
            pltpu.VMEM((2, tq, 1), F32),
            pltpu.VMEM((2, tq, v_dim), F32),
        ],
    )
    del small
    return pl.pallas_call(
        functools.partial(_attn_kernel, qk_dim=v_dim // 2, lambda_init=lambda_init),
        grid_spec=grid_spec,
        out_shape=jax.ShapeDtypeStruct((b, t, heads * v_dim), F32),
        compiler_params=_cparams(("parallel", "parallel", "parallel", "arbitrary")),
        name="attn",
    )(slopes, q3, k3, v3, lq1, lk1, lq2, lk2, subln_g)


EXP_ZERO_ARG = 104.0
MAX_FIXED_SHIFT = 40.0
N_AUG = 6
ATTN_BLOCK = 512
ATTN_GROUP_SIZES = (4, 2, 1)
ATTN_KEY_SPLIT = 2
ATTN_V_PAD = 16


def _attn_fixed_kernel(w_ref, slope_ref, mb_ref, q_ref, k_ref, vt_ref, dbias_ref, qaug_ref, kaug_ref,
                       lq1_ref, lk1_ref, lq2_ref, lk2_ref, sg_ref, o_ref, acc_ref, *, qk_dim, lambda_init):
    h = pl.program_id(1)
    i = pl.program_id(2)
    tb = q_ref.shape[0]
    nk = k_ref.shape[0] // tb
    v_dim = 2 * qk_dim
    slope = slope_ref[h]
    mb = mb_ref[0]
    w = w_ref[h]
    acc_ref[...] = jnp.zeros_like(acc_ref)

    q = q_ref[...]
    lane = lax.broadcasted_iota(jnp.int32, q.shape, 1)
    lane_row = lax.broadcasted_iota(jnp.int32, (1, q.shape[1]), 1)
    is_c0 = lane < qk_dim
    datas = (is_c0, jnp.logical_not(is_c0))

    def blocks(items, qvar, kvar, diag):
        sub = tb // ATTN_KEY_SPLIT
        qas = [jnp.where(datas[c], q, qaug_ref[qvar, c]) for c in range(2)]
        chains = []
        for n, (j, delta) in enumerate(items):
            kblk = k_ref[pl.ds(pl.multiple_of(j * tb, tb), tb), :]
            vtblk = vt_ref[j]
            for c in range(2):
                base = qk_dim * (1 - c)
                dyn = jnp.where(lane_row == base + 4, -mb,
                                jnp.where(lane_row == base + 5, -(slope * tb) * delta, 0.0)).astype(BF16)
                ka = jnp.where(datas[c], kblk, kaug_ref[kvar, c] + dyn)
                for u in range(ATTN_KEY_SPLIT):
                    rows = slice(u * sub, (u + 1) * sub)
                    chains.append((c, u, qas[c], ka[rows], vtblk[:, rows]))
        sts = [lax.dot_general(ka, qa, (((1,), (1,)), ((), ())), preferred_element_type=F32)
               for _, _, qa, ka, _ in chains]
        if diag:
            sts = [st - dbias_ref[u * sub:(u + 1) * sub, :] for st, (_, u, _, _, _) in zip(sts, chains)]
        pts = [jnp.exp(st).astype(BF16) for st in sts]
        for c in range(2):
            upd = None
            for pt, (cc, _, _, _, vsub) in zip(pts, chains):
                if cc == c:
                    part = _dot(vsub, pt)
                    upd = part if upd is None else upd + part
            acc_ref[c] += upd

    def span(first, count, qvar, kvar, dist):
        for per in ATTN_GROUP_SIZES:
            def group(g, carry, first=first, per=per):
                j0 = first + g * per
                blocks([(j0 + n, dist(j0 + n)) for n in range(per)], qvar, kvar, False)
                return carry

            groups = count // per
            lax.fori_loop(0, groups, group, 0)
            first = first + groups * per
            count = count - groups * per

    lo = jnp.maximum(i - w, 0)
    hi = jnp.minimum(i + w, nk - 1)
    span(lo, i - lo, 0, 0, lambda j: (i - j).astype(F32))
    blocks([(i, jnp.zeros((), F32))], 2, 0, True)
    span(i + 1, hi - i, 1, 1, lambda j: (j - i).astype(F32))

    lam = (jnp.exp(jnp.sum(lq1_ref[...] * lk1_ref[...], axis=-1, keepdims=True))
           - jnp.exp(jnp.sum(lq2_ref[...] * lk2_ref[...], axis=-1, keepdims=True)) + lambda_init)
    a0 = acc_ref[0]
    a1 = acc_ref[1]
    ot = a0[:v_dim] / a0[v_dim:v_dim + 1] - lam * (a1[:v_dim] / a1[v_dim:v_dim + 1])
    ms = jnp.mean(ot * ot, axis=0, keepdims=True)
    o_ref[...] = (ot * lax.rsqrt(ms + NORM_EPS) * (sg_ref[...] * (1.0 - lambda_init))).T


def _attn_aug_tables(slopes, tb, qk_dim):
    heads = len(slopes)
    lanes = 2 * qk_dim
    pos = np.arange(tb)
    qaug = np.zeros((heads, 3, 2, tb, lanes), np.float32)
    kaug = np.zeros((heads, 2, 2, tb, lanes), np.float32)
    for hd, sl in enumerate(slopes):
        for var in range(3):
            rr = pos if var != 1 else tb - 1 - pos
            hi = (rr // 16) * (16.0 * sl)
            lo = (rr % 16) * sl
            for c in range(2):
                base = qk_dim * (1 - c)
                if var == 2:
                    qaug[hd, var, c, :, base + 4] = 1.0
                    continue
                qaug[hd, var, c, :, base + 0] = -hi
                qaug[hd, var, c, :, base + 1] = -lo
                qaug[hd, var, c, :, base + 2:base + N_AUG] = 1.0
                kaug[hd, var, c, :, base + 0:base + 2] = 1.0
                kaug[hd, var, c, :, base + 2] = hi
                kaug[hd, var, c, :, base + 3] = lo
    return jnp.asarray(qaug, BF16), jnp.asarray(kaug, BF16)


def _attention_fixed(mb, q3, k3, vt, lq1, lk1, lq2, lk2, subln_g, heads, v_dim, lambda_init, tb):
    b, t, _ = q3.shape
    qk_dim = v_dim // 2
    slopes = [2.0 ** (-8.0 * (hd + 1) / heads) for hd in range(heads)]
    widths = [int(math.ceil((EXP_ZERO_ARG / s - 1.0) / tb)) for s in slopes]
    qaug, kaug = _attn_aug_tables(slopes, tb, qk_dim)
    pos = np.arange(tb)
    dist = np.abs(pos[:, None] - pos[None, :]).astype(np.float32)
    dbias = jnp.asarray(np.stack([dist * s for s in slopes]), F32)
    nb = t // tb
    vrows = v_dim + ATTN_V_PAD
    assert vt.shape == (b, heads, nb, vrows, tb)
    const = lambda bi, h, i, *_: (0, 0)
    grid_spec = pltpu.PrefetchScalarGridSpec(
        num_scalar_prefetch=3,
        grid=(b, heads, t // tb),
        in_specs=[
            pl.BlockSpec((None, tb, v_dim), lambda bi, h, i, *_: (bi, i, h)),
            pl.BlockSpec((None, t, v_dim), lambda bi, h, i, *_: (bi, 0, h)),
            pl.BlockSpec((None, None, nb, vrows, tb), lambda bi, h, i, *_: (bi, h, 0, 0, 0)),
            pl.BlockSpec((None, tb, tb), lambda bi, h, i, *_: (h, 0, 0)),
            pl.BlockSpec((None, 3, 2, tb, v_dim), lambda bi, h, i, *_: (h, 0, 0, 0, 0)),
            pl.BlockSpec((None, 2, 2, tb, v_dim), lambda bi, h, i, *_: (h, 0, 0, 0, 0)),
            pl.BlockSpec(lq1.shape, const),
            pl.BlockSpec(lk1.shape, const),
            pl.BlockSpec(lq2.shape, const),
            pl.BlockSpec(lk2.shape, const),
            pl.BlockSpec((v_dim, 1), const),
        ],
        out_specs=pl.BlockSpec((None, tb, v_dim), lambda bi, h, i, *_: (bi, i, h)),
        scratch_shapes=[pltpu.VMEM((2, vrows, tb), F32)],
    )
    return pl.pallas_call(
        functools.partial(_attn_fixed_kernel, qk_dim=qk_dim, lambda_init=lambda_init),
        grid_spec=grid_spec,
        out_shape=jax.ShapeDtypeStruct((b, t, heads * v_dim), F32),
        compiler_params=_cparams(("parallel", "parallel", "parallel")),
        name="attn_fixed",
    )(jnp.asarray(widths, jnp.int32), jnp.asarray(slopes, F32), mb.reshape(1).astype(F32),
      q3, k3, vt, dbias, qaug, kaug, lq1, lk1, lq2, lk2, subln_g.reshape(v_dim, 1))


def _post_kernel(yf_ref, yb_ref, bonus_ref, g_ref, oda_ref, x_ref, lng_ref, lnb_ref, gmean_ref, wo_rw_ref, wo_da_ref,
                 gffn_ref, wr_ref, br_ref, h_ref, m_ref, idx_ref, gate_ref, cnt_ref, *, gn_eps):
    @pl.when(pl.program_id(0) == 0)
    def _():
        cnt_ref[...] = jnp.zeros_like(cnt_ref)

    gmean = gmean_ref[...]
    y = yf_ref[...] + yb_ref[...]
    mean = _dot_hilo_rhs(y, gmean)
    yc = y - mean
    var = _dot_hilo_rhs(yc * yc, gmean)
    yn = yc * lax.rsqrt(var + gn_eps) * lng_ref[...] + lnb_ref[...]
    o_rw = (yn + bonus_ref[...]) * g_ref[...]
    h = x_ref[...] + _dot(o_rw.astype(BF16), wo_rw_ref[...]) + _dot(oda_ref[...].astype(BF16), wo_da_ref[...])
    h_ref[...] = h
    ms = jnp.mean(h * h, axis=-1, keepdims=True)
    m = h * lax.rsqrt(ms + NORM_EPS) * gffn_ref[...]
    _store_rows(m_ref, m)

    logits = _dot_hilo(m, wr_ref[...]) + br_ref[...]
    ne = logits.shape[1]
    lane = lax.broadcasted_iota(jnp.int32, logits.shape, 1).astype(F32)
    kcol = lax.broadcasted_iota(jnp.int32, (logits.shape[0], TOP_K), 1)
    idx_all = jnp.zeros((logits.shape[0], TOP_K), F32)
    val_all = jnp.zeros((logits.shape[0], TOP_K), F32)
    work = logits
    chosen = jnp.zeros(logits.shape, F32)
    for kk in range(TOP_K):
        mx = jnp.max(work, axis=-1, keepdims=True)
        sel = jnp.min(jnp.where(work == mx, lane, float(ne)), axis=-1, keepdims=True)
        idx_all = jnp.where(kcol == kk, sel, idx_all)
        val_all = jnp.where(kcol == kk, mx, val_all)
        hit = lane == sel
        chosen = jnp.where(hit, 1.0, chosen)
        work = jnp.where(hit, -jnp.inf, work)
    e = jnp.exp(val_all - jnp.max(val_all, axis=-1, keepdims=True))
    idx_ref[...] = idx_all.astype(jnp.int32)
    gate_ref[...] = e / jnp.sum(e, axis=-1, keepdims=True)
    cnt_ref[...] += jnp.sum(chosen, axis=0, keepdims=True)


def _post(yf, yb, bonus, g, oda, x2, lng, lnb, gmean, wo_rw, wo_da, gffn, wr, br, gn_eps, tm):
    n, c = yf.shape
    d = x2.shape[1]
    ne = wr.shape[1]
    full = lambda i: (0, 0)
    row = lambda i: (i, 0)
    return pl.pallas_call(
        functools.partial(_post_kernel, gn_eps=gn_eps),
        grid=(n // tm,),
        in_specs=[
            pl.BlockSpec((tm, c), row),
            pl.BlockSpec((tm, c), row),
            pl.BlockSpec((tm, c), row),
            pl.BlockSpec((tm, c), row),
            pl.BlockSpec((tm, c), row),
            pl.BlockSpec((tm, d), row),
            pl.BlockSpec((1, c), full),
            pl.BlockSpec((1, c), full),
            pl.BlockSpec((c, c), full),
            pl.BlockSpec((c, d), full),
            pl.BlockSpec((c, d), full),
            pl.BlockSpec((1, d), full),
            pl.BlockSpec((d, ne), full),
            pl.BlockSpec((1, ne), full),
        ],
        out_specs=[
            pl.BlockSpec((tm, d), row),
            pl.BlockSpec((tm * d // V7X_LANES, V7X_LANES), row),
            pl.BlockSpec((tm, TOP_K), row),
            pl.BlockSpec((tm, TOP_K), row),
            pl.BlockSpec((1, ne), full),
        ],
        out_shape=[
            jax.ShapeDtypeStruct((n, d), F32),
            jax.ShapeDtypeStruct((n * d // V7X_LANES, V7X_LANES), F32),
            jax.ShapeDtypeStruct((n, TOP_K), jnp.int32),
            jax.ShapeDtypeStruct((n, TOP_K), F32),
            jax.ShapeDtypeStruct((1, ne), F32),
        ],
        compiler_params=_cparams(("arbitrary",)),
        name="post",
    )(yf, yb, bonus, g, oda, x2, lng, lnb, gmean, wo_rw, wo_da, gffn, wr, br)


def _rank_kernel(idx_ref, ltri_ref, offs_ref, slot_ref, carry_ref):
    @pl.when(pl.program_id(0) == 0)
    def _():
        carry_ref[...] = offs_ref[...]

    idx = idx_ref[...]
    tr = idx.shape[0]
    lane = lax.broadcasted_iota(jnp.int32, (tr, N_EXPERTS), 1)
    hots = [jnp.where(lane == idx[:, kk:kk + 1], 1.0, 0.0) for kk in range(TOP_K)]
    mask = hots[0]
    for kk in range(1, TOP_K):
        mask = mask + hots[kk]
    before = _dot(ltri_ref[...], mask.astype(BF16)) + carry_ref[...]
    kcol = lax.broadcasted_iota(jnp.int32, (tr, TOP_K), 1)
    slot = jnp.zeros((tr, TOP_K), F32)
    for kk in range(TOP_K):
        rk = jnp.sum(hots[kk] * before, axis=-1, keepdims=True)
        slot = jnp.where(kcol == kk, rk, slot)
    slot_ref[...] = slot.astype(jnp.int32)
    carry_ref[...] = carry_ref[...] + jnp.sum(mask, axis=0, keepdims=True)


def _moe_rank(idx, offs_row, tr):
    n = idx.shape[0]
    ltri = jnp.asarray(np.tril(np.ones((tr, tr), np.float32), -1), BF16)
    return pl.pallas_call(
        _rank_kernel,
        grid=(n // tr,),
        in_specs=[pl.BlockSpec((tr, TOP_K), lambda i: (i, 0)), pl.BlockSpec((tr, tr), lambda i: (0, 0)),
                  pl.BlockSpec((1, N_EXPERTS), lambda i: (0, 0))],
        out_specs=pl.BlockSpec((tr, TOP_K), lambda i: (i, 0)),
        out_shape=jax.ShapeDtypeStruct((n, TOP_K), jnp.int32),
        scratch_shapes=[pltpu.VMEM((1, N_EXPERTS), F32)],
        compiler_params=_cparams(("arbitrary",)),
        name="moe_rank",
    )(idx, ltri, offs_row)


ROW_UNROLL = 8


def _row(r, c):
    return pl.ds(pl.multiple_of(r * c, c), c)


def _dispatch_kernel(slot_ref, m_ref, xs_ref, sem, *, c):
    td = m_ref.shape[0] // c

    def issue(g, carry):
        for u in range(ROW_UNROLL):
            t = g * ROW_UNROLL + u
            for kk in range(TOP_K):
                dest = slot_ref[t * TOP_K + kk]
                pltpu.make_async_copy(m_ref.at[_row(t, c), :], xs_ref.at[_row(dest, c), :], sem).start(
                    priority=kk % 2)
        return carry

    lax.fori_loop(0, td // ROW_UNROLL, issue, 0)
    for kk in range(TOP_K):
        pltpu.make_async_copy(m_ref, xs_ref.at[pl.ds(0, td * c), :], sem).wait()


def _dispatch(slot_flat, mrows, c, p_rows, td):
    n = mrows.shape[0] // c
    assert td % ROW_UNROLL == 0
    return pl.pallas_call(
        functools.partial(_dispatch_kernel, c=c),
        grid=(n // td,),
        in_specs=[
            pl.BlockSpec((td * TOP_K,), lambda i: (i,), memory_space=pltpu.SMEM),
            pl.BlockSpec((td * c, V7X_LANES), lambda i: (i, 0)),
        ],
        out_specs=pl.BlockSpec(memory_space=pl.ANY),
        scratch_shapes=[pltpu.SemaphoreType.DMA(())],
        out_shape=jax.ShapeDtypeStruct((p_rows * c, V7X_LANES), F32),
        compiler_params=_cparams(("arbitrary",), has_side_effects=True),
        name="moe_dispatch",
    )(slot_flat, mrows)


def _ffn_kernel(be_ref, nu_ref, x_ref, w1_ref, b1_ref, w2_ref, b2_ref, y_ref, w1s_ref, w2s_ref, *, d_ff):
    i = pl.program_id(0)

    @pl.when((i == 0) | (be_ref[i] != be_ref[jnp.maximum(i - 1, 0)]))
    def _():
        w1s_ref[...] = w1_ref[...].astype(BF16)
        w2s_ref[...] = w2_ref[...].astype(BF16)

    @pl.when(i < nu_ref[0])
    def _():
        x = _load_rows(x_ref, w1_ref.shape[0] // V7X_LANES).astype(BF16)
        hcat = _dot(x, w1s_ref[...]) + b1_ref[...]
        glu = jnp.minimum(hcat[:, :d_ff], SWIGLU_LIMIT)
        lin = jnp.clip(hcat[:, d_ff:], -SWIGLU_LIMIT, SWIGLU_LIMIT)
        act = glu * jax.nn.sigmoid(SWIGLU_ALPHA * glu) * (lin + 1.0)
        _store_rows(y_ref, _dot(act.astype(BF16), w2s_ref[...]) + b2_ref[...])


def _moe_ffn(blk_e, n_used, xrows, w1, b1, w2, b2, bm):
    ne, d, f2 = w1.shape
    d_ff = f2 // 2
    c = d // V7X_LANES
    p_rows = xrows.shape[0] // c
    blk = (bm * c, V7X_LANES)
    grid_spec = pltpu.PrefetchScalarGridSpec(
        num_scalar_prefetch=2,
        grid=(p_rows // bm,),
        in_specs=[
            pl.BlockSpec(blk, lambda i, be, nu: (jnp.minimum(i, nu[0] - 1), 0)),
            pl.BlockSpec((None, d, f2), lambda i, be, nu: (be[i], 0, 0)),
            pl.BlockSpec((None, 1, f2), lambda i, be, nu: (be[i], 0, 0)),
            pl.BlockSpec((None, d_ff, d), lambda i, be, nu: (be[i], 0, 0)),
            pl.BlockSpec((None, 1, d), lambda i, be, nu: (be[i], 0, 0)),
        ],
        out_specs=pl.BlockSpec(blk, lambda i, be, nu: (i, 0)),
        scratch_shapes=[pltpu.VMEM((d, f2), BF16), pltpu.VMEM((d_ff, d), BF16)],
    )
    return pl.pallas_call(
        functools.partial(_ffn_kernel, d_ff=d_ff),
        grid_spec=grid_spec,
        out_shape=jax.ShapeDtypeStruct(xrows.shape, F32),
        compiler_params=_cparams(("arbitrary",)),
        name="moe_ffn",
    )(blk_e, n_used, xrows, w1, b1.reshape(ne, 1, f2), w2, b2.reshape(ne, 1, d))


def _combine_kernel(slot_ref, next_slot_ref, h_ref, gate_ref, p_ref, wp_ref, gp_ref, wg_ref, y_ref, o_ref,
                    buf_ref, h2_ref, sems):
    i = pl.program_id(0)
    tc, d = h_ref.shape
    c = d // V7X_LANES

    def start_gather(slots, half):
        def issue(g, carry):
            for u in range(ROW_UNROLL):
                t = g * ROW_UNROLL + u
                for kk in range(TOP_K):
                    src = slots[t * TOP_K + kk]
                    pltpu.make_async_copy(y_ref.at[_row(src, c), :], buf_ref.at[half, kk, _row(t, c), :],
                                          sems.at[half]).start(priority=kk % 2)
            return carry

        lax.fori_loop(0, tc // ROW_UNROLL, issue, 0)

    cur = i % 2

    @pl.when(i == 0)
    def _():
        start_gather(slot_ref, 0)

    @pl.when(i + 1 < pl.num_programs(0))
    def _():
        start_gather(next_slot_ref, 1 - cur)

    for kk in range(TOP_K):
        pltpu.make_async_copy(y_ref.at[pl.ds(0, tc * c), :], buf_ref.at[cur, kk], sems.at[cur]).wait()

    gate = gate_ref[...]
    for j in range(c):
        cols = slice(j * V7X_LANES, (j + 1) * V7X_LANES)
        acc = h_ref[:, cols]
        for kk in range(TOP_K):
            acc = acc + gate[:, kk:kk + 1] * buf_ref[cur, kk, pl.ds(j, tc, stride=c), :]
        h2_ref[:, cols] = acc

    h2 = h2_ref[...]
    e = _dot(p_ref[...].astype(BF16), wp_ref[...])
    ms = jnp.mean(e * e, axis=-1, keepdims=True)
    e = e * lax.rsqrt(ms + NORM_EPS) * gp_ref[...]
    o_ref[...] = h2 + e * jax.nn.sigmoid(_dot(h2.astype(BF16), wg_ref[...]))


def _combine_ple(slot_flat, h, gates, yrows, p2, wp_bf, gp, wg_bf, tc):
    n, d = h.shape
    pd = p2.shape[1]
    assert tc % ROW_UNROLL == 0
    last = n // tc - 1
    full = lambda i: (0, 0)
    row = lambda i: (i, 0)
    return pl.pallas_call(
        _combine_kernel,
        grid=(n // tc,),
        in_specs=[
            pl.BlockSpec((tc * TOP_K,), lambda i: (i,), memory_space=pltpu.SMEM),
            pl.BlockSpec((tc * TOP_K,), lambda i: (jnp.minimum(i + 1, last),), memory_space=pltpu.SMEM),
            pl.BlockSpec((tc, d), row),
            pl.BlockSpec((tc, TOP_K), row),
            pl.BlockSpec((tc, pd), row),
            pl.BlockSpec((pd, d), full),
            pl.BlockSpec((1, d), full),
            pl.BlockSpec((d, d), full),
            pl.BlockSpec(memory_space=pl.ANY),
        ],
        out_specs=pl.BlockSpec((tc, d), row),
        scratch_shapes=[pltpu.VMEM((2, TOP_K, tc * d // V7X_LANES, V7X_LANES), F32),
                        pltpu.VMEM((tc, d), F32),
                        pltpu.SemaphoreType.DMA((2,))],
        out_shape=jax.ShapeDtypeStruct((n, d), F32),
        compiler_params=_cparams(("arbitrary",)),
        name="moe_combine_ple",
    )(slot_flat, slot_flat, h, gates, p2, wp_bf, gp, wg_bf, yrows)


def _tile(n, want):
    t = min(n, want)
    assert n % t == 0, (n, t)
    return t


def _layer(h3, p3, lambda_init, norm_mix_g, w_in, rw_mu, rw_w0, rw_w2, rw_a0, rw_a2, rw_g2, rw_k_k, rw_k_a, rw_r_k,
           rw_ln_g, rw_ln_b, da_q_norm_g, da_k_norm_g, da_lq1, da_lk1, da_lq2, da_lk2, da_subln_g, w_out,
           norm_ffn_g, w_router, b_router, w1, b1, w2, b2, w_ple, ple_norm_g, w_ple_gate):
    b, t, d = h3.shape
    n = b * t
    c_rw = rw_k_k.shape[0]
    lora = rw_w2.shape[1]
    rw_cols = rw_mu.shape[0]
    rw_heads = rw_r_k.shape[0]
    c_da = w_in.shape[1] - rw_cols
    assert c_da % 3 == 0
    c_da //= 3
    qk_dim = da_q_norm_g.shape[0]
    v_dim = da_subln_g.shape[0]
    da_heads = c_da // v_dim
    assert rw_heads * RW_HEAD == c_rw and c_rw % HEAD_GROUP == 0 and v_dim == V7X_LANES and 2 * qk_dim == v_dim
    assert 2 * lora == V7X_LANES and w_router.shape[1] == N_EXPERTS

    x2 = h3.reshape(n, d)
    row = lambda a: a.reshape(1, -1).astype(F32)

    gq_t = row(jnp.tile(da_q_norm_g, c_da // qk_dim))
    gk_t = row(jnp.tile(da_k_norm_g, c_da // qk_dim))
    gm_qk = _group_matrix(c_da, qk_dim, 1.0 / qk_dim)
    ta = _tile(t, ATTN_BLOCK)
    zrw, q, k, vt = _inproj(x2, row(norm_mix_g), w_in.astype(BF16), gq_t, gk_t, gm_qk, rw_cols, c_da,
                            1.0 / math.sqrt(qk_dim), _tile(ta, 512), b, da_heads, ta)

    zeros = jnp.zeros((lora, c_rw), F32)
    w2bd = jnp.concatenate([jnp.concatenate([rw_w2[0], zeros], axis=1),
                            jnp.concatenate([zeros, rw_w2[1]], axis=1)], axis=0).astype(BF16)
    a2bd = jnp.concatenate([jnp.concatenate([rw_a2[0], zeros], axis=1),
                            jnp.concatenate([zeros, rw_a2[1]], axis=1)], axis=0).astype(BF16)
    gsum = _group_matrix(c_rw, RW_HEAD, 1.0)
    r, vv, nkk, lw, kd, bd, g, bonus = _rwkv_prep(
        zrw.reshape(b, t, rw_cols), row(rw_mu), w2bd, row(rw_w0), a2bd, row(rw_a0), rw_g2.astype(BF16),
        row(rw_k_k), row(rw_k_a), row(rw_r_k), gsum, c_rw, lora, _tile(t, 512))
    yf, yb = _rwkv_scan(r, vv, nkk, lw, kd, bd, _tile(t, 256))

    slopes = 2.0 ** (-8.0 * jnp.arange(1, da_heads + 1, dtype=F32) / da_heads)
    q3 = q.reshape(b, t, c_da)
    k3 = k.reshape(b, t, c_da)
    small = (row(da_lq1), row(da_lk1), row(da_lq2), row(da_lk2), row(da_subln_g))

    def attention_running_max():
        v3 = vt[:, :, :, :v_dim, :].transpose(0, 2, 4, 1, 3).reshape(b, t, c_da)
        return _attention(slopes, q3, k3, v3, *small, da_heads, v_dim, lambda_init, ta, ta)

    score_bound = 1.01 * math.sqrt(qk_dim) * jnp.max(jnp.abs(da_q_norm_g)) * jnp.max(jnp.abs(da_k_norm_g))
    mb = jnp.ceil(2.0 * score_bound.astype(F32)) * 0.5
    o_da = lax.cond(
        mb <= MAX_FIXED_SHIFT,
        lambda: _attention_fixed(mb, q3, k3, vt, *small, da_heads, v_dim, lambda_init, ta),
        attention_running_max)

    gmean = _group_matrix(c_rw, RW_HEAD, 1.0 / RW_HEAD)
    w_out_bf = w_out.astype(BF16)
    h1, m, idx, gates, counts = _post(
        yf.reshape(n, c_rw), yb.reshape(n, c_rw), bonus.reshape(n, c_rw), g.reshape(n, c_rw), o_da.reshape(n, c_da), x2,
        row(rw_ln_g), row(rw_ln_b), gmean, w_out_bf[:c_rw], w_out_bf[c_rw:], row(norm_ffn_g),
        w_router.astype(F32), row(b_router), RW_HEAD * 1e-5, _tile(n, 512))

    bm = MOE_BLOCK_ROWS
    counts = counts.reshape(N_EXPERTS).astype(jnp.int32)
    padded = (counts + bm - 1) // bm * bm
    pend = jnp.cumsum(padded)
    offs = (pend - padded).astype(jnp.int32)
    slot = _moe_rank(idx, offs.astype(F32).reshape(1, N_EXPERTS), _tile(n, 512))
    n_blk = (n * TOP_K) // bm + N_EXPERTS
    blk_start = jnp.arange(n_blk, dtype=jnp.int32) * bm
    blk_e = jnp.minimum(jnp.sum(blk_start[:, None] >= pend[None, :], axis=1), N_EXPERTS - 1).astype(jnp.int32)
    n_used = (pend[-1] // bm).astype(jnp.int32).reshape(1)
    slot_flat = slot.reshape(n * TOP_K)
    xs = _dispatch(slot_flat, m, d // V7X_LANES, n_blk * bm, _tile(n, 512))
    ys = _moe_ffn(blk_e, n_used, xs, w1, b1, w2, b2, bm)
    out = _combine_ple(slot_flat, h1, gates, ys, p3.reshape(n, -1), w_ple.astype(BF16), row(ple_norm_g),
                       w_ple_gate.astype(BF16), _tile(n, 512))
    return out.reshape(b, t, d)


def kernel(x, p, norm_mix_g, w_in, rw_mu, rw_w0, rw_w2, rw_a0, rw_a2, rw_g2, rw_k_k, rw_k_a, rw_r_k, rw_ln_g,
           rw_ln_b, da_q_norm_g, da_k_norm_g, da_lq1, da_lk1, da_lq2, da_lk2, da_subln_g, w_out, norm_ffn_g,
           w_router, b_router, w1, b1, w2, b2, w_ple, ple_norm_g, w_ple_gate):
    h = x.astype(F32)
    params = (norm_mix_g, w_in, rw_mu, rw_w0, rw_w2, rw_a0, rw_a2, rw_g2, rw_k_k, rw_k_a, rw_r_k, rw_ln_g, rw_ln_b,
              da_q_norm_g, da_k_norm_g, da_lq1, da_lk1, da_lq2, da_lk2, da_subln_g, w_out, norm_ffn_g,
              w_router, b_router, w1, b1, w2, b2, w_ple, ple_norm_g, w_ple_gate)
    for i in range(p.shape[0]):
        lambda_init = 0.8 - 0.6 * math.exp(-0.3 * i)
        h = _layer(h, p[i], lambda_init, *(a[i] for a in params))
    return h.astype(x.dtype)
```

```python
import functools
import math

import jax
import jax.numpy as jnp
import numpy as np
from jax import lax
from jax.experimental import pallas as pl
from jax.experimental.pallas import tpu as pltpu

F32 = jnp.float32
BF16 = jnp.bfloat16

V7X_LANES = 128
V7X_VMEM_BYTES = 64 * 1024 * 1024
VMEM_LIMIT = V7X_VMEM_BYTES - 12 * 1024 * 1024

ROW_TILE = 512
COMBINE_TILE = 256
SCAN_TILE = 256

NORM_EPS = 1e-6
RW_HEAD = 64
CHUNK = 64
HEAD_GROUP = 256
N_EXPERTS = 32
TOP_K = 4
MOE_BLOCK_ROWS = 512
SWIGLU_ALPHA = 1.702
SWIGLU_LIMIT = 7.0


def _cparams(sem, **kw):
    return pltpu.CompilerParams(dimension_semantics=sem, vmem_limit_bytes=VMEM_LIMIT, **kw)


def _dot(a, b):
    return jnp.dot(a, b, preferred_element_type=F32)


def _dot_hilo_lhs(g_bf16, x):
    hi, lo = _split2(x)
    return _dot(g_bf16, hi) + _dot(g_bf16, lo)


def _split2(x):
    hi = x.astype(BF16)
    return hi, (x - hi.astype(F32)).astype(BF16)


def _dot_hilo_rhs(x, g_bf16):
    hi, lo = _split2(x)
    out = _dot(jnp.concatenate([hi, lo], axis=0), g_bf16)
    return out[:x.shape[0]] + out[x.shape[0]:]


def _dot_hilo(a, b):
    ah, al = _split2(a)
    bh, bl = _split2(b)
    return _dot(ah, bh) + (_dot(ah, bl) + _dot(al, bh))


def _store_rows(ref, x2):
    rows, d = x2.shape
    c = d // V7X_LANES
    for j in range(c):
        ref[pl.ds(j, rows, stride=c), :] = x2[:, j * V7X_LANES:(j + 1) * V7X_LANES]


def _load_rows(ref, c):
    rows = ref.shape[0] // c
    return jnp.concatenate([ref[pl.ds(j, rows, stride=c), :] for j in range(c)], axis=1)


def _group_matrix(width, group, value):
    g = np.arange(width) // group
    return jnp.asarray(np.where(g[:, None] == g[None, :], value, 0.0), BF16)


def _inproj_kernel(x_ref, g_ref, w_ref, gq_ref, gk_ref, gm_ref, zrw_ref, q_ref, k_ref, vt_ref, *, rw_cols, c_da, scale):
    x = x_ref[...]
    ms = jnp.mean(x * x, axis=-1, keepdims=True)
    u = (x * lax.rsqrt(ms + NORM_EPS) * g_ref[...]).astype(BF16)
    z = _dot(u, w_ref[...])
    zrw_ref[...] = z[:, :rw_cols]
    zq = z[:, rw_cols:rw_cols + c_da]
    zk = z[:, rw_cols + c_da:rw_cols + 2 * c_da]
    zv = z[:, rw_cols + 2 * c_da:]
    tm = zq.shape[0]
    msqk = _dot(jnp.concatenate([zq * zq, zk * zk], axis=0).astype(BF16), gm_ref[...])
    q_ref[...] = (zq * lax.rsqrt(msqk[:tm] + NORM_EPS) * (gq_ref[...] * scale)).astype(BF16)
    k_ref[...] = (zk * lax.rsqrt(msqk[tm:] + NORM_EPS) * gk_ref[...]).astype(BF16)
    heads, vrows, _ = vt_ref.shape
    v_dim = c_da // heads
    pad_row = lax.broadcasted_iota(jnp.int32, (vrows - v_dim, tm), 0)
    pad = jnp.where(pad_row == 0, 1.0, 0.0).astype(BF16)
    for hd in range(heads):
        vt_ref[hd, 0:v_dim, :] = zv[:, hd * v_dim:(hd + 1) * v_dim].T.astype(BF16)
        vt_ref[hd, v_dim:, :] = pad


def _inproj(x2, g, w_bf, gq_t, gk_t, gm, rw_cols, c_da, scale, tm, b, heads, tb):
    n, d = x2.shape
    cols = w_bf.shape[1]
    t = n // b
    tps = t // tm
    per = tb // tm
    vrows = c_da // heads + ATTN_V_PAD
    full = lambda i: (0, 0)
    row = lambda i: (i, 0)
    return pl.pallas_call(
        functools.partial(_inproj_kernel, rw_cols=rw_cols, c_da=c_da, scale=scale),
        grid=(n // tm,),
        in_specs=[
            pl.BlockSpec((tm, d), row),
            pl.BlockSpec((1, d), full),
            pl.BlockSpec((d, cols), full),
            pl.BlockSpec((1, c_da), full),
            pl.BlockSpec((1, c_da), full),
            pl.BlockSpec((c_da, c_da), full),
        ],
        out_specs=[
            pl.BlockSpec((tm, rw_cols), row),
            pl.BlockSpec((tm, c_da), row),
            pl.BlockSpec((tm, c_da), row),
            pl.BlockSpec((None, heads, None, vrows, tm),
                         lambda i: (i // tps, 0, (i % tps) // per, 0, (i % tps) % per)),
        ],
        out_shape=[
            jax.ShapeDtypeStruct((n, rw_cols), F32),
            jax.ShapeDtypeStruct((n, c_da), BF16),
            jax.ShapeDtypeStruct((n, c_da), BF16),
            jax.ShapeDtypeStruct((b, heads, t // tb, vrows, tb), BF16),
        ],
        compiler_params=_cparams(("parallel",)),
        name="inproj",
    )(x2, g, w_bf, gq_t, gk_t, gm)


def _rwkv_prep_kernel(z_ref, zp_ref, zn_ref, mu_ref, w2_ref, w0_ref, a2_ref, a0_ref, g2_ref, kk_ref, ka_ref,
                      rk_ref, gsum_ref,
                      r_ref, v_ref, nkk_ref, lw_ref, kd_ref, bd_ref, g_ref, bonus_ref, *, c_rw, lora):
    i = pl.program_id(1)
    nt = pl.num_programs(1)
    z = z_ref[...]
    tt = z.shape[0]
    rows = lax.broadcasted_iota(jnp.int32, z.shape, 0)
    prev_row = jnp.where(i == 0, 0.0, zp_ref[7:8, :])
    next_row = jnp.where(i == nt - 1, 0.0, zn_ref[0:1, :])
    zprev = jnp.where(rows == 0, prev_row, pltpu.roll(z, 1, 0))
    znext = jnp.where(rows == tt - 1, next_row, pltpu.roll(z, tt - 1, 0))
    z = z + (0.5 * (zprev + znext) - z) * mu_ref[...]

    r = z[:, 0:c_rw]
    k = z[:, c_rw:2 * c_rw]
    v = z[:, 2 * c_rw:3 * c_rw]
    o = 3 * c_rw
    wd = z[:, o:o + 2 * lora]
    ad = z[:, o + 2 * lora:o + 4 * lora]
    gd = z[:, o + 4 * lora:]

    xw = w0_ref[...] + _dot(jnp.tanh(wd).astype(BF16), w2_ref[...])
    xa = a0_ref[...] + _dot(ad.astype(BF16), a2_ref[...])
    lw = -math.exp(-0.5) * jax.nn.sigmoid(xw)
    rate = jax.nn.sigmoid(xa)
    g = _dot(jax.nn.sigmoid(gd).astype(BF16), g2_ref[...])

    gsum = gsum_ref[...]
    kkr = k * kk_ref[...]
    ka = ka_ref[...]
    kd0 = k * (1.0 + (rate[:, :c_rw] - 1.0) * ka)
    kd1 = k * (1.0 + (rate[:, c_rw:] - 1.0) * ka)
    sums = _dot_hilo_rhs(jnp.concatenate([kkr * kkr, r * (0.5 * (kd0 + kd1)) * rk_ref[...]], axis=0), gsum)
    kk = kkr * lax.rsqrt(sums[:tt] + 1e-12)
    bonus = sums[tt:] * v

    r_ref[...] = r.astype(BF16)
    v_ref[...] = v.astype(BF16)
    nkk_ref[...] = (-kk).astype(BF16)
    lw_ref[0] = lw[:, :c_rw]
    lw_ref[1] = lw[:, c_rw:]
    kd_ref[0] = kd0.astype(BF16)
    kd_ref[1] = kd1.astype(BF16)
    bd_ref[0] = (kk * rate[:, :c_rw]).astype(BF16)
    bd_ref[1] = (kk * rate[:, c_rw:]).astype(BF16)
    g_ref[...] = g
    bonus_ref[...] = bonus


def _rwkv_prep(zrw3, mu, w2bd, w0f, a2bd, a0f, g2_bf, k_k, k_a, r_kf, gsum, c_rw, lora, tt):
    b, t, cols = zrw3.shape
    nt = t // tt
    hb = tt // 8
    nhb = t // 8
    full = lambda bi, i: (0, 0)
    tile = lambda bi, i: (bi, i, 0)
    dtile = lambda bi, i: (0, bi, i, 0)
    one = jax.ShapeDtypeStruct((b, t, c_rw), F32)
    two = jax.ShapeDtypeStruct((2, b, t, c_rw), F32)
    one16 = jax.ShapeDtypeStruct((b, t, c_rw), BF16)
    two16 = jax.ShapeDtypeStruct((2, b, t, c_rw), BF16)
    return pl.pallas_call(
        functools.partial(_rwkv_prep_kernel, c_rw=c_rw, lora=lora),
        grid=(b, nt),
        in_specs=[
            pl.BlockSpec((None, tt, cols), tile),
            pl.BlockSpec((None, 8, cols), lambda bi, i: (bi, jnp.maximum(i * hb - 1, 0), 0)),
            pl.BlockSpec((None, 8, cols), lambda bi, i: (bi, jnp.minimum((i + 1) * hb, nhb - 1), 0)),
            pl.BlockSpec((1, cols), full),
            pl.BlockSpec(w2bd.shape, full),
            pl.BlockSpec((1, 2 * c_rw), full),
            pl.BlockSpec(a2bd.shape, full),
            pl.BlockSpec((1, 2 * c_rw), full),
            pl.BlockSpec(g2_bf.shape, full),
            pl.BlockSpec((1, c_rw), full),
            pl.BlockSpec((1, c_rw), full),
            pl.BlockSpec((1, c_rw), full),
            pl.BlockSpec((c_rw, c_rw), full),
        ],
        out_specs=[
            pl.BlockSpec((None, tt, c_rw), tile),
            pl.BlockSpec((None, tt, c_rw), tile),
            pl.BlockSpec((None, tt, c_rw), tile),
            pl.BlockSpec((2, None, tt, c_rw), dtile),
            pl.BlockSpec((2, None, tt, c_rw), dtile),
            pl.BlockSpec((2, None, tt, c_rw), dtile),
            pl.BlockSpec((None, tt, c_rw), tile),
            pl.BlockSpec((None, tt, c_rw), tile),
        ],
        out_shape=[one16, one16, one16, two, two16, two16, one, one],
        compiler_params=_cparams(("parallel", "parallel")),
        name="rwkv_prep",
    )(zrw3, zrw3, zrw3, mu, w2bd, w0f, a2bd, a0f, g2_bf, k_k, k_a, r_kf, gsum)


M_STRICT, M_INCL, M_EYE, M_PAIR, M_LEVEL = 0, 1, 2, 3, 4
N_LEVELS = 5
SCAN_SEQS = 2


def _scan_tables(w):
    c = CHUNK
    masks = []
    mcums = []
    for d in range(2):
        t = np.broadcast_to(np.arange(c)[:, None], (c, w))
        s = np.broadcast_to((np.arange(w) % RW_HEAD)[None, :], (c, w))
        tt = np.broadcast_to(np.arange(c)[:, None], (c, c))
        ss = np.broadcast_to(np.arange(c)[None, :], (c, c))
        if d == 1:
            t, s, tt, ss = c - 1 - t, c - 1 - s, c - 1 - tt, c - 1 - ss
        rows = [s < t, s <= t, s == t, (t // 2 == s // 2) & (t > s)]
        sz = 2
        while sz < c:
            rows.append((t // (2 * sz) == s // (2 * sz)) & ((t // sz) % 2 == 1) & ((s // sz) % 2 == 0))
            sz *= 2
        assert len(rows) == M_LEVEL + N_LEVELS
        masks.append(np.stack(rows).astype(np.float32))
        mcums.append((ss <= tt).astype(np.float32))
    return jnp.asarray(np.stack(masks), F32), jnp.asarray(np.stack(mcums), BF16)


def _block_diag(y, bd16):
    g = y.shape[1] // RW_HEAD
    return jnp.concatenate([y.astype(BF16)] * g, axis=0) * bd16


def _bmm(x, y, bd16):
    outs = []
    for s in range(0, y.shape[1], HEAD_GROUP):
        outs.append(_dot(x[:, s:s + HEAD_GROUP].astype(BF16), _block_diag(y[:, s:s + HEAD_GROUP], bd16)))
    return jnp.concatenate(outs, axis=1)


def _bmm_nt(x, y, bd16):
    outs = []
    for s in range(0, y.shape[1], HEAD_GROUP):
        bd = _block_diag(y[:, s:s + HEAD_GROUP], bd16)
        outs.append(lax.dot_general(x[:, s:s + HEAD_GROUP].astype(BF16), bd, (((1,), (1,)), ((), ())),
                                    preferred_element_type=F32))
    return jnp.concatenate(outs, axis=1)


def _bmm_tn(x, y, bd32):
    outs = []
    for s in range(0, y.shape[1], HEAD_GROUP):
        xt = x[:, s:s + HEAD_GROUP].T.astype(BF16)
        full = _dot(xt, y[:, s:s + HEAD_GROUP].astype(BF16)) * bd32
        o = full[0:RW_HEAD]
        for j in range(1, HEAD_GROUP // RW_HEAD):
            o = o + full[j * RW_HEAD:(j + 1) * RW_HEAD]
        outs.append(o)
    return jnp.concatenate(outs, axis=1)


def _scan_chunks(r, lw, k, v, a, b, state, masks, mcum, bd16, bd32):
    c = r[0].shape[0]

    def each(f, *cols):
        return [f(*xs) for xs in zip(*cols)]

    cum = each(_dot_hilo_lhs, mcum, lw)
    tot = each(lambda x: jnp.sum(x, axis=0, keepdims=True), lw)
    e_neg = each(lambda x: jnp.exp(-x), cum)
    e_tail = each(lambda t_, x: jnp.exp(t_ - x), tot, cum)
    rt = each(lambda x, cm: x * jnp.exp(cm), r, cum)
    at = each(lambda x, cm, l_: x * jnp.exp(cm - l_), a, cum, lw)
    bt = each(jnp.multiply, b, e_neg)
    kt = each(jnp.multiply, k, e_neg)
    bh = each(jnp.multiply, b, e_tail)
    kh = each(jnp.multiply, k, e_tail)

    ar = each(lambda x, y_: jnp.concatenate([x, y_], axis=0), at, rt)
    pb = each(lambda x, y_: _bmm_nt(x, y_, bd16), ar, bt)
    pk = each(lambda x, y_: _bmm_nt(x, y_, bd16), ar, kt)
    a_ab = each(lambda x, m: x[:c] * m[M_STRICT], pb, masks)
    a_rb = each(lambda x, m: x[c:] * m[M_INCL], pb, masks)
    a_ak = each(lambda x, m: x[:c] * m[M_STRICT], pk, masks)
    a_rk = each(lambda x, m: x[c:] * m[M_INCL], pk, masks)

    tinv = each(lambda x, m: m[M_EYE] + x * m[M_PAIR], a_ab, masks)
    for lvl in range(N_LEVELS):
        x = each(lambda aa, m, ti: _bmm(aa * m[M_LEVEL + lvl], ti, bd16), a_ab, masks, tinv)
        tinv = each(lambda ti, x_: ti + _bmm(ti, x_, bd16), tinv, x)

    bmm = lambda p, q: _bmm(p, q, bd16)
    stack = lambda p, q: jnp.concatenate([p, q], axis=0)
    att = each(bmm, tinv, at)
    akv = each(lambda p, p2, q: _bmm(stack(p, p2), q, bd16), a_ak, a_rk, v)
    vt = each(lambda ti, x_: _bmm(ti, x_[:c], bd16), tinv, akv)
    rh = each(lambda x, p, q: x + _bmm(p, q, bd16), rt, a_rb, att)
    y_in = each(lambda p, q, x_: _bmm(p, q, bd16) + x_[c:], a_rb, vt, akv)
    mc = each(lambda m, t_, p, q: m[M_EYE] * jnp.exp(t_) + _bmm_tn(p, q, bd32), masks, tot, bh, att)
    nc = each(lambda p, p2, q, q2: _bmm_tn(stack(p, p2), stack(q, q2), bd32), bh, kh, vt, v)
    onst = each(lambda p, p2, q: _bmm(stack(p, p2), q, bd16), rh, mc, state)
    y = each(lambda x_, yi: x_[:c] + yi, onst, y_in)
    new_state = each(lambda x_, n_: x_[c:] + n_, onst, nc)
    return y, new_state


def _rwkv_scan_kernel(rf_ref, vf_ref, af_ref, rb_ref, vb_ref, ab_ref, lwf_ref, kf_ref, bf_ref, lwb_ref, kb_ref,
                      bb_ref, mask_ref, mcum_ref, bd16_ref, bd32_ref, yf_ref, yb_ref, state_ref, *, n_chunks):
    @pl.when(pl.program_id(1) == 0)
    def _():
        state_ref[...] = jnp.zeros_like(state_ref)

    dirs = ((0, rf_ref, lwf_ref, kf_ref, vf_ref, af_ref, bf_ref, yf_ref),
            (1, rb_ref, lwb_ref, kb_ref, vb_ref, ab_ref, bb_ref, yb_ref))
    n_seq = rf_ref.shape[0]

    def body(j, carry):
        cols = [[] for _ in range(9)]
        where = []
        for d, r_ref, lw_ref, k_ref, v_ref, a_ref, b_ref, y_ref in dirs:
            cj = j if d == 0 else n_chunks - 1 - j
            sl = pl.ds(pl.multiple_of(cj * CHUNK, CHUNK), CHUNK)
            for g in range(n_seq):
                vals = (r_ref[g, sl, :], lw_ref[g, sl, :], k_ref[g, sl, :], v_ref[g, sl, :], a_ref[g, sl, :],
                        b_ref[g, sl, :], state_ref[d, g], mask_ref.at[d], mcum_ref[d])
                for col, val in zip(cols, vals):
                    col.append(val)
                where.append((y_ref, d, g, sl))
        ys, sts = _scan_chunks(*cols, bd16_ref[...], bd32_ref[...])
        for (y_ref, d, g, sl), y, st in zip(where, ys, sts):
            y_ref[g, sl, :] = y
            state_ref[d, g] = st
        return carry

    lax.fori_loop(0, n_chunks, body, 0)


def _rwkv_scan(r, v, nkk, lw, kd, bd, tb):
    b, t, w = r.shape
    nb = t // tb
    g = SCAN_SEQS if b % SCAN_SEQS == 0 else 1
    masks, mcum = _scan_tables(w)
    bd32 = _group_matrix(HEAD_GROUP, RW_HEAD, 1.0).astype(F32)
    bd16 = bd32.astype(BF16)
    fwd = lambda bi, i: (bi, i, 0)
    bwd = lambda bi, i: (bi, nb - 1 - i, 0)
    fwd_d = lambda bi, i: (0, bi, i, 0)
    bwd_d = lambda bi, i: (1, bi, nb - 1 - i, 0)
    const = lambda nd: (lambda bi, i: (0,) * nd)
    blk = (g, tb, w)
    dblk = (None, g, tb, w)
    out = jax.ShapeDtypeStruct((b, t, w), F32)
    return pl.pallas_call(
        functools.partial(_rwkv_scan_kernel, n_chunks=tb // CHUNK),
        grid=(b // g, nb),
        in_specs=[
            pl.BlockSpec(blk, fwd), pl.BlockSpec(blk, fwd), pl.BlockSpec(blk, fwd),
            pl.BlockSpec(blk, bwd), pl.BlockSpec(blk, bwd), pl.BlockSpec(blk, bwd),
            pl.BlockSpec(dblk, fwd_d), pl.BlockSpec(dblk, fwd_d), pl.BlockSpec(dblk, fwd_d),
            pl.BlockSpec(dblk, bwd_d), pl.BlockSpec(dblk, bwd_d), pl.BlockSpec(dblk, bwd_d),
            pl.BlockSpec(masks.shape, const(4)),
            pl.BlockSpec(mcum.shape, const(3)),
            pl.BlockSpec(bd16.shape, const(2)),
            pl.BlockSpec(bd32.shape, const(2)),
        ],
        out_specs=[pl.BlockSpec(blk, fwd), pl.BlockSpec(blk, bwd)],
        out_shape=[out, out],
        scratch_shapes=[pltpu.VMEM((2, g, CHUNK, w), F32)],
        compiler_params=_cparams(("parallel", "arbitrary")),
        name="rwkv_scan",
    )(r, v, nkk, r, v, nkk, lw, kd, bd, lw, kd, bd, masks, mcum, bd16, bd32)


def _attn_kernel(slope_ref, q_ref, k_ref, v_ref, lq1_ref, lk1_ref, lq2_ref, lk2_ref, sg_ref, o_ref,
                 m_ref, l_ref, acc_ref, *, qk_dim, lambda_init):
    h = pl.program_id(1)
    i = pl.program_id(2)
    j = pl.program_id(3)
    nk = pl.num_programs(3)
    tq = q_ref.shape[0]
    tk = k_ref.shape[0]

    @pl.when(j == 0)
    def _():
        m_ref[...] = jnp.full(m_ref.shape, -jnp.inf, F32)
        l_ref[...] = jnp.zeros_like(l_ref)
        acc_ref[...] = jnp.zeros_like(acc_ref)

    q = q_ref[...]
    k = k_ref[...]
    v = v_ref[...]
    lane = lax.broadcasted_iota(jnp.int32, q.shape, 1)
    zero = jnp.zeros_like(q)
    qi = lax.broadcasted_iota(jnp.int32, (tq, tk), 0) + i * tq
    kj = lax.broadcasted_iota(jnp.int32, (tq, tk), 1) + j * tk
    bias = slope_ref[h] * jnp.abs(qi - kj).astype(F32)
    for c in range(2):
        qc = jnp.where((lane < qk_dim) == (c == 0), q, zero)
        s = lax.dot_general(qc, k, (((1,), (1,)), ((), ())), preferred_element_type=F32) - bias
        m_old = m_ref[c]
        m_new = jnp.maximum(m_old, jnp.max(s, axis=-1, keepdims=True))
        alpha = jnp.exp(m_old - m_new)
        p = jnp.exp(s - m_new)
        l_ref[c] = alpha * l_ref[c] + jnp.sum(p, axis=-1, keepdims=True)
        acc_ref[c] = alpha * acc_ref[c] + _dot(p.astype(BF16), v)
        m_ref[c] = m_new

    @pl.when(j == nk - 1)
    def _():
        lam = (jnp.exp(jnp.sum(lq1_ref[...] * lk1_ref[...], axis=-1, keepdims=True))
               - jnp.exp(jnp.sum(lq2_ref[...] * lk2_ref[...], axis=-1, keepdims=True)) + lambda_init)
        o = acc_ref[0] / l_ref[0] - lam * (acc_ref[1] / l_ref[1])
        ms = jnp.mean(o * o, axis=-1, keepdims=True)
        o_ref[...] = o * lax.rsqrt(ms + NORM_EPS) * (sg_ref[...] * (1.0 - lambda_init))


def _attention(slopes, q3, k3, v3, lq1, lk1, lq2, lk2, subln_g, heads, v_dim, lambda_init, tq, tk):
    b, t, _ = q3.shape
    small = lambda s, bi, h, i, j: (0, 0)
    grid_spec = pltpu.PrefetchScalarGridSpec(
        num_scalar_prefetch=1,
        grid=(b, heads, t // tq, t // tk),
        in_specs=[
            pl.BlockSpec((None, tq, v_dim), lambda bi, h, i, j, s: (bi, i, h)),
            pl.BlockSpec((None, tk, v_dim), lambda bi, h, i, j, s: (bi, j, h)),
            pl.BlockSpec((None, tk, v_dim), lambda bi, h, i, j, s: (bi, j, h)),
            pl.BlockSpec(lq1.shape, lambda bi, h, i, j, s: (0, 0)),
            pl.BlockSpec(lk1.shape, lambda bi, h, i, j, s: (0, 0)),
            pl.BlockSpec(lq2.shape, lambda bi, h, i, j, s: (0, 0)),
            pl.BlockSpec(lk2.shape, lambda bi, h, i, j, s: (0, 0)),
            pl.BlockSpec(subln_g.shape, lambda bi, h, i, j, s: (0, 0)),
        ],
        out_specs=pl.BlockSpec((None, tq, v_dim), lambda bi, h, i, j, s: (bi, i, h)),
        scratch_shapes=[
            pltpu.VMEM((2, tq, 1), F32),
            pltpu.VMEM((2, tq, 1), F32),
            pltpu.VMEM((2, tq, v_dim), F32),
        ],
    )
    del small
    return pl.pallas_call(
        functools.partial(_attn_kernel, qk_dim=v_dim // 2, lambda_init=lambda_init),
        grid_spec=grid_spec,
        out_shape=jax.ShapeDtypeStruct((b, t, heads * v_dim), F32),
        compiler_params=_cparams(("parallel", "parallel", "parallel", "arbitrary")),
        name="attn",
    )(slopes, q3, k3, v3, lq1, lk1, lq2, lk2, subln_g)


EXP_ZERO_ARG = 104.0
MAX_FIXED_SHIFT = 40.0
N_AUG = 6
ATTN_BLOCK = 512
ATTN_GROUP_SIZES = (4, 2, 1)
ATTN_KEY_SPLIT = 1
ATTN_V_PAD = 16


def _attn_fixed_kernel(w_ref, slope_ref, mb_ref, q_ref, k_ref, vt_ref, dbias_ref, qaug_ref, kaug_ref,
                       lq1_ref, lk1_ref, lq2_ref, lk2_ref, sg_ref, o_ref, acc_ref, *, qk_dim, lambda_init):
    h = pl.program_id(1)
    i = pl.program_id(2)
    tb = q_ref.shape[0]
    nk = k_ref.shape[0] // tb
    v_dim = 2 * qk_dim
    slope = slope_ref[h]
    mb = mb_ref[0]
    w = w_ref[h]
    acc_ref[...] = jnp.zeros_like(acc_ref)

    q = q_ref[...]
    lane = lax.broadcasted_iota(jnp.int32, q.shape, 1)
    lane_row = lax.broadcasted_iota(jnp.int32, (1, q.shape[1]), 1)
    is_c0 = lane < qk_dim
    datas = (is_c0, jnp.logical_not(is_c0))

    def blocks(items, qvar, kvar, diag):
        sub = tb // ATTN_KEY_SPLIT
        qas = [jnp.where(datas[c], q, qaug_ref[qvar, c]) for c in range(2)]
        chains = []
        for n, (j, delta) in enumerate(items):
            kblk = k_ref[pl.ds(pl.multiple_of(j * tb, tb), tb), :]
            vtblk = vt_ref[j]
            for c in range(2):
                base = qk_dim * (1 - c)
                dyn = jnp.where(lane_row == base + 4, -mb,
                                jnp.where(lane_row == base + 5, -(slope * tb) * delta, 0.0)).astype(BF16)
                ka = jnp.where(datas[c], kblk, kaug_ref[kvar, c] + dyn)
                for u in range(ATTN_KEY_SPLIT):
                    rows = slice(u * sub, (u + 1) * sub)
                    chains.append((c, u, qas[c], ka[rows], vtblk[:, rows]))
        sts = [lax.dot_general(ka, qa, (((1,), (1,)), ((), ())), preferred_element_type=F32)
               for _, _, qa, ka, _ in chains]
        if diag:
            sts = [st - dbias_ref[u * sub:(u + 1) * sub, :] for st, (_, u, _, _, _) in zip(sts, chains)]
        pts = [jnp.exp(st).astype(BF16) for st in sts]
        for c in range(2):
            upd = None
            for pt, (cc, _, _, _, vsub) in zip(pts, chains):
                if cc == c:
                    part = _dot(vsub, pt)
                    upd = part if upd is None else upd + part
            acc_ref[c] += upd

    def span(first, count, qvar, kvar, dist):
        for per in ATTN_GROUP_SIZES:
            def group(g, carry, first=first, per=per):
                j0 = first + g * per
                blocks([(j0 + n, dist(j0 + n)) for n in range(per)], qvar, kvar, False)
                return carry

            groups = count // per
            lax.fori_loop(0, groups, group, 0)
            first = first + groups * per
            count = count - groups * per

    lo = jnp.maximum(i - w, 0)
    hi = jnp.minimum(i + w, nk - 1)
    span(lo, i - lo, 0, 0, lambda j: (i - j).astype(F32))
    blocks([(i, jnp.zeros((), F32))], 2, 0, True)
    span(i + 1, hi - i, 1, 1, lambda j: (j - i).astype(F32))

    lam = (jnp.exp(jnp.sum(lq1_ref[...] * lk1_ref[...], axis=-1, keepdims=True))
           - jnp.exp(jnp.sum(lq2_ref[...] * lk2_ref[...], axis=-1, keepdims=True)) + lambda_init)
    a0 = acc_ref[0]
    a1 = acc_ref[1]
    ot = a0[:v_dim] / a0[v_dim:v_dim + 1] - lam * (a1[:v_dim] / a1[v_dim:v_dim + 1])
    ms = jnp.mean(ot * ot, axis=0, keepdims=True)
    o_ref[...] = (ot * lax.rsqrt(ms + NORM_EPS) * (sg_ref[...] * (1.0 - lambda_init))).T


def _attn_aug_tables(slopes, tb, qk_dim):
    heads = len(slopes)
    lanes = 2 * qk_dim
    pos = np.arange(tb)
    qaug = np.zeros((heads, 3, 2, tb, lanes), np.float32)
    kaug = np.zeros((heads, 2, 2, tb, lanes), np.float32)
    for hd, sl in enumerate(slopes):
        for var in range(3):
            rr = pos if var != 1 else tb - 1 - pos
            hi = (rr // 16) * (16.0 * sl)
            lo = (rr % 16) * sl
            for c in range(2):
                base = qk_dim * (1 - c)
                if var == 2:
                    qaug[hd, var, c, :, base + 4] = 1.0
                    continue
                qaug[hd, var, c, :, base + 0] = -hi
                qaug[hd, var, c, :, base + 1] = -lo
                qaug[hd, var, c, :, base + 2:base + N_AUG] = 1.0
                kaug[hd, var, c, :, base + 0:base + 2] = 1.0
                kaug[hd, var, c, :, base + 2] = hi
                kaug[hd, var, c, :, base + 3] = lo
    return jnp.asarray(qaug, BF16), jnp.asarray(kaug, BF16)


def _attention_fixed(mb, q3, k3, vt, lq1, lk1, lq2, lk2, subln_g, heads, v_dim, lambda_init, tb):
    b, t, _ = q3.shape
    qk_dim = v_dim // 2
    slopes = [2.0 ** (-8.0 * (hd + 1) / heads) for hd in range(heads)]
    widths = [int(math.ceil((EXP_ZERO_ARG / s - 1.0) / tb)) for s in slopes]
    qaug, kaug = _attn_aug_tables(slopes, tb, qk_dim)
    pos = np.arange(tb)
    dist = np.abs(pos[:, None] - pos[None, :]).astype(np.float32)
    dbias = jnp.asarray(np.stack([dist * s for s in slopes]), F32)
    nb = t // tb
    vrows = v_dim + ATTN_V_PAD
    assert vt.shape == (b, heads, nb, vrows, tb)
    const = lambda bi, h, i, *_: (0, 0)
    grid_spec = pltpu.PrefetchScalarGridSpec(
        num_scalar_prefetch=3,
        grid=(b, heads, t // tb),
        in_specs=[
            pl.BlockSpec((None, tb, v_dim), lambda bi, h, i, *_: (bi, i, h)),
            pl.BlockSpec((None, t, v_dim), lambda bi, h, i, *_: (bi, 0, h)),
            pl.BlockSpec((None, None, nb, vrows, tb), lambda bi, h, i, *_: (bi, h, 0, 0, 0)),
            pl.BlockSpec((None, tb, tb), lambda bi, h, i, *_: (h, 0, 0)),
            pl.BlockSpec((None, 3, 2, tb, v_dim), lambda bi, h, i, *_: (h, 0, 0, 0, 0)),
            pl.BlockSpec((None, 2, 2, tb, v_dim), lambda bi, h, i, *_: (h, 0, 0, 0, 0)),
            pl.BlockSpec(lq1.shape, const),
            pl.BlockSpec(lk1.shape, const),
            pl.BlockSpec(lq2.shape, const),
            pl.BlockSpec(lk2.shape, const),
            pl.BlockSpec((v_dim, 1), const),
        ],
        out_specs=pl.BlockSpec((None, tb, v_dim), lambda bi, h, i, *_: (bi, i, h)),
        scratch_shapes=[pltpu.VMEM((2, vrows, tb), F32)],
    )
    return pl.pallas_call(
        functools.partial(_attn_fixed_kernel, qk_dim=qk_dim, lambda_init=lambda_init),
        grid_spec=grid_spec,
        out_shape=jax.ShapeDtypeStruct((b, t, heads * v_dim), F32),
        compiler_params=_cparams(("parallel", "parallel", "parallel")),
        name="attn_fixed",
    )(jnp.asarray(widths, jnp.int32), jnp.asarray(slopes, F32), mb.reshape(1).astype(F32),
      q3, k3, vt, dbias, qaug, kaug, lq1, lk1, lq2, lk2, subln_g.reshape(v_dim, 1))


def _post_kernel(yf_ref, yb_ref, bonus_ref, g_ref, oda_ref, x_ref, lng_ref, lnb_ref, gmean_ref, wo_rw_ref, wo_da_ref,
                 gffn_ref, wr_ref, br_ref, h_ref, m_ref, idx_ref, gate_ref, cnt_ref, *, gn_eps):
    @pl.when(pl.program_id(0) == 0)
    def _():
        cnt_ref[...] = jnp.zeros_like(cnt_ref)

    gmean = gmean_ref[...]
    y = yf_ref[...] + yb_ref[...]
    mean = _dot_hilo_rhs(y, gmean)
    yc = y - mean
    var = _dot_hilo_rhs(yc * yc, gmean)
    yn = yc * lax.rsqrt(var + gn_eps) * lng_ref[...] + lnb_ref[...]
    o_rw = (yn + bonus_ref[...]) * g_ref[...]
    h = x_ref[...] + _dot(o_rw.astype(BF16), wo_rw_ref[...]) + _dot(oda_ref[...].astype(BF16), wo_da_ref[...])
    h_ref[...] = h
    ms = jnp.mean(h * h, axis=-1, keepdims=True)
    m = h * lax.rsqrt(ms + NORM_EPS) * gffn_ref[...]
    _store_rows(m_ref, m)

    logits = _dot_hilo(m, wr_ref[...]) + br_ref[...]
    ne = logits.shape[1]
    lane = lax.broadcasted_iota(jnp.int32, logits.shape, 1).astype(F32)
    kcol = lax.broadcasted_iota(jnp.int32, (logits.shape[0], TOP_K), 1)
    idx_all = jnp.zeros((logits.shape[0], TOP_K), F32)
    val_all = jnp.zeros((logits.shape[0], TOP_K), F32)
    work = logits
    chosen = jnp.zeros(logits.shape, F32)
    for kk in range(TOP_K):
        mx = jnp.max(work, axis=-1, keepdims=True)
        sel = jnp.min(jnp.where(work == mx, lane, float(ne)), axis=-1, keepdims=True)
        idx_all = jnp.where(kcol == kk, sel, idx_all)
        val_all = jnp.where(kcol == kk, mx, val_all)
        hit = lane == sel
        chosen = jnp.where(hit, 1.0, chosen)
        work = jnp.where(hit, -jnp.inf, work)
    e = jnp.exp(val_all - jnp.max(val_all, axis=-1, keepdims=True))
    idx_ref[...] = idx_all.astype(jnp.int32)
    gate_ref[...] = e / jnp.sum(e, axis=-1, keepdims=True)
    cnt_ref[...] += jnp.sum(chosen, axis=0, keepdims=True)


def _post(yf, yb, bonus, g, oda, x2, lng, lnb, gmean, wo_rw, wo_da, gffn, wr, br, gn_eps, tm):
    n, c = yf.shape
    d = x2.shape[1]
    ne = wr.shape[1]
    full = lambda i: (0, 0)
    row = lambda i: (i, 0)
    return pl.pallas_call(
        functools.partial(_post_kernel, gn_eps=gn_eps),
        grid=(n // tm,),
        in_specs=[
            pl.BlockSpec((tm, c), row),
            pl.BlockSpec((tm, c), row),
            pl.BlockSpec((tm, c), row),
            pl.BlockSpec((tm, c), row),
            pl.BlockSpec((tm, c), row),
            pl.BlockSpec((tm, d), row),
            pl.BlockSpec((1, c), full),
            pl.BlockSpec((1, c), full),
            pl.BlockSpec((c, c), full),
            pl.BlockSpec((c, d), full),
            pl.BlockSpec((c, d), full),
            pl.BlockSpec((1, d), full),
            pl.BlockSpec((d, ne), full),
            pl.BlockSpec((1, ne), full),
        ],
        out_specs=[
            pl.BlockSpec((tm, d), row),
            pl.BlockSpec((tm * d // V7X_LANES, V7X_LANES), row),
            pl.BlockSpec((tm, TOP_K), row),
            pl.BlockSpec((tm, TOP_K), row),
            pl.BlockSpec((1, ne), full),
        ],
        out_shape=[
            jax.ShapeDtypeStruct((n, d), F32),
            jax.ShapeDtypeStruct((n * d // V7X_LANES, V7X_LANES), F32),
            jax.ShapeDtypeStruct((n, TOP_K), jnp.int32),
            jax.ShapeDtypeStruct((n, TOP_K), F32),
            jax.ShapeDtypeStruct((1, ne), F32),
        ],
        compiler_params=_cparams(("arbitrary",)),
        name="post",
    )(yf, yb, bonus, g, oda, x2, lng, lnb, gmean, wo_rw, wo_da, gffn, wr, br)


def _rank_kernel(idx_ref, ltri_ref, offs_ref, slot_ref, carry_ref):
    @pl.when(pl.program_id(0) == 0)
    def _():
        carry_ref[...] = offs_ref[...]

    idx = idx_ref[...]
    tr = idx.shape[0]
    lane = lax.broadcasted_iota(jnp.int32, (tr, N_EXPERTS), 1)
    hots = [jnp.where(lane == idx[:, kk:kk + 1], 1.0, 0.0) for kk in range(TOP_K)]
    mask = hots[0]
    for kk in range(1, TOP_K):
        mask = mask + hots[kk]
    before = _dot(ltri_ref[...], mask.astype(BF16)) + carry_ref[...]
    kcol = lax.broadcasted_iota(jnp.int32, (tr, TOP_K), 1)
    slot = jnp.zeros((tr, TOP_K), F32)
    for kk in range(TOP_K):
        rk = jnp.sum(hots[kk] * before, axis=-1, keepdims=True)
        slot = jnp.where(kcol == kk, rk, slot)
    slot_ref[...] = slot.astype(jnp.int32)
    carry_ref[...] = carry_ref[...] + jnp.sum(mask, axis=0, keepdims=True)


def _moe_rank(idx, offs_row, tr):
    n = idx.shape[0]
    ltri = jnp.asarray(np.tril(np.ones((tr, tr), np.float32), -1), BF16)
    return pl.pallas_call(
        _rank_kernel,
        grid=(n // tr,),
        in_specs=[pl.BlockSpec((tr, TOP_K), lambda i: (i, 0)), pl.BlockSpec((tr, tr), lambda i: (0, 0)),
                  pl.BlockSpec((1, N_EXPERTS), lambda i: (0, 0))],
        out_specs=pl.BlockSpec((tr, TOP_K), lambda i: (i, 0)),
        out_shape=jax.ShapeDtypeStruct((n, TOP_K), jnp.int32),
        scratch_shapes=[pltpu.VMEM((1, N_EXPERTS), F32)],
        compiler_params=_cparams(("arbitrary",)),
        name="moe_rank",
    )(idx, ltri, offs_row)


ROW_UNROLL = 8


def _row(r, c):
    return pl.ds(pl.multiple_of(r * c, c), c)


def _dispatch_kernel(slot_ref, m_ref, xs_ref, sem, *, c):
    td = m_ref.shape[0] // c

    def issue(g, carry):
        for u in range(ROW_UNROLL):
            t = g * ROW_UNROLL + u
            for kk in range(TOP_K):
                dest = slot_ref[t * TOP_K + kk]
                pltpu.make_async_copy(m_ref.at[_row(t, c), :], xs_ref.at[_row(dest, c), :], sem).start(
                    priority=kk % 2)
        return carry

    lax.fori_loop(0, td // ROW_UNROLL, issue, 0)
    for kk in range(TOP_K):
        pltpu.make_async_copy(m_ref, xs_ref.at[pl.ds(0, td * c), :], sem).wait()


def _dispatch(slot_flat, mrows, c, p_rows, td):
    n = mrows.shape[0] // c
    assert td % ROW_UNROLL == 0
    return pl.pallas_call(
        functools.partial(_dispatch_kernel, c=c),
        grid=(n // td,),
        in_specs=[
            pl.BlockSpec((td * TOP_K,), lambda i: (i,), memory_space=pltpu.SMEM),
            pl.BlockSpec((td * c, V7X_LANES), lambda i: (i, 0)),
        ],
        out_specs=pl.BlockSpec(memory_space=pl.ANY),
        scratch_shapes=[pltpu.SemaphoreType.DMA(())],
        out_shape=jax.ShapeDtypeStruct((p_rows * c, V7X_LANES), F32),
        compiler_params=_cparams(("arbitrary",), has_side_effects=True),
        name="moe_dispatch",
    )(slot_flat, mrows)


def _ffn_kernel(be_ref, nu_ref, x_ref, w1_ref, b1_ref, w2_ref, b2_ref, y_ref, w1s_ref, w2s_ref, *, d_ff):
    i = pl.program_id(0)

    @pl.when((i == 0) | (be_ref[i] != be_ref[jnp.maximum(i - 1, 0)]))
    def _():
        w1s_ref[...] = w1_ref[...].astype(BF16)
        w2s_ref[...] = w2_ref[...].astype(BF16)

    @pl.when(i < nu_ref[0])
    def _():
        x = _load_rows(x_ref, w1_ref.shape[0] // V7X_LANES).astype(BF16)
        hcat = _dot(x, w1s_ref[...]) + b1_ref[...]
        glu = jnp.minimum(hcat[:, :d_ff], SWIGLU_LIMIT)
        lin = jnp.clip(hcat[:, d_ff:], -SWIGLU_LIMIT, SWIGLU_LIMIT)
        act = glu * jax.nn.sigmoid(SWIGLU_ALPHA * glu) * (lin + 1.0)
        _store_rows(y_ref, _dot(act.astype(BF16), w2s_ref[...]) + b2_ref[...])


def _moe_ffn(blk_e, n_used, xrows, w1, b1, w2, b2, bm):
    ne, d, f2 = w1.shape
    d_ff = f2 // 2
    c = d // V7X_LANES
    p_rows = xrows.shape[0] // c
    blk = (bm * c, V7X_LANES)
    grid_spec = pltpu.PrefetchScalarGridSpec(
        num_scalar_prefetch=2,
        grid=(p_rows // bm,),
        in_specs=[
            pl.BlockSpec(blk, lambda i, be, nu: (jnp.minimum(i, nu[0] - 1), 0)),
            pl.BlockSpec((None, d, f2), lambda i, be, nu: (be[i], 0, 0)),
            pl.BlockSpec((None, 1, f2), lambda i, be, nu: (be[i], 0, 0)),
            pl.BlockSpec((None, d_ff, d), lambda i, be, nu: (be[i], 0, 0)),
            pl.BlockSpec((None, 1, d), lambda i, be, nu: (be[i], 0, 0)),
        ],
        out_specs=pl.BlockSpec(blk, lambda i, be, nu: (i, 0)),
        scratch_shapes=[pltpu.VMEM((d, f2), BF16), pltpu.VMEM((d_ff, d), BF16)],
    )
    return pl.pallas_call(
        functools.partial(_ffn_kernel, d_ff=d_ff),
        grid_spec=grid_spec,
        out_shape=jax.ShapeDtypeStruct(xrows.shape, F32),
        compiler_params=_cparams(("arbitrary",)),
        name="moe_ffn",
    )(blk_e, n_used, xrows, w1, b1.reshape(ne, 1, f2), w2, b2.reshape(ne, 1, d))


def _combine_kernel(slot_ref, next_slot_ref, h_ref, gate_ref, p_ref, wp_ref, gp_ref, wg_ref, y_ref, o_ref,
                    buf_ref, h2_ref, sems):
    i = pl.program_id(0)
    tc, d = h_ref.shape
    c = d // V7X_LANES

    def start_gather(slots, half):
        def issue(g, carry):
            for u in range(ROW_UNROLL):
                t = g * ROW_UNROLL + u
                for kk in range(TOP_K):
                    src = slots[t * TOP_K + kk]
                    pltpu.make_async_copy(y_ref.at[_row(src, c), :], buf_ref.at[half, kk, _row(t, c), :],
                                          sems.at[half]).start(priority=kk % 2)
            return carry

        lax.fori_loop(0, tc // ROW_UNROLL, issue, 0)

    cur = i % 2

    @pl.when(i == 0)
    def _():
        start_gather(slot_ref, 0)

    @pl.when(i + 1 < pl.num_programs(0))
    def _():
        start_gather(next_slot_ref, 1 - cur)

    for kk in range(TOP_K):
        pltpu.make_async_copy(y_ref.at[pl.ds(0, tc * c), :], buf_ref.at[cur, kk], sems.at[cur]).wait()

    gate = gate_ref[...]
    for j in range(c):
        cols = slice(j * V7X_LANES, (j + 1) * V7X_LANES)
        acc = h_ref[:, cols]
        for kk in range(TOP_K):
            acc = acc + gate[:, kk:kk + 1] * buf_ref[cur, kk, pl.ds(j, tc, stride=c), :]
        h2_ref[:, cols] = acc

    h2 = h2_ref[...]
    e = _dot(p_ref[...].astype(BF16), wp_ref[...])
    ms = jnp.mean(e * e, axis=-1, keepdims=True)
    e = e * lax.rsqrt(ms + NORM_EPS) * gp_ref[...]
    o_ref[...] = h2 + e * jax.nn.sigmoid(_dot(h2.astype(BF16), wg_ref[...]))


def _combine_ple(slot_flat, h, gates, yrows, p2, wp_bf, gp, wg_bf, tc):
    n, d = h.shape
    pd = p2.shape[1]
    assert tc % ROW_UNROLL == 0
    last = n // tc - 1
    full = lambda i: (0, 0)
    row = lambda i: (i, 0)
    return pl.pallas_call(
        _combine_kernel,
        grid=(n // tc,),
        in_specs=[
            pl.BlockSpec((tc * TOP_K,), lambda i: (i,), memory_space=pltpu.SMEM),
            pl.BlockSpec((tc * TOP_K,), lambda i: (jnp.minimum(i + 1, last),), memory_space=pltpu.SMEM),
            pl.BlockSpec((tc, d), row),
            pl.BlockSpec((tc, TOP_K), row),
            pl.BlockSpec((tc, pd), row),
            pl.BlockSpec((pd, d), full),
            pl.BlockSpec((1, d), full),
            pl.BlockSpec((d, d), full),
            pl.BlockSpec(memory_space=pl.ANY),
        ],
        out_specs=pl.BlockSpec((tc, d), row),
        scratch_shapes=[pltpu.VMEM((2, TOP_K, tc * d // V7X_LANES, V7X_LANES), F32),
                        pltpu.VMEM((tc, d), F32),
                        pltpu.SemaphoreType.DMA((2,))],
        out_shape=jax.ShapeDtypeStruct((n, d), F32),
        compiler_params=_cparams(("arbitrary",)),
        name="moe_combine_ple",
    )(slot_flat, slot_flat, h, gates, p2, wp_bf, gp, wg_bf, yrows)


def _tile(n, want):
    t = min(n, want)
    assert n % t == 0, (n, t)
    return t


def _layer(h3, p3, lambda_init, norm_mix_g, w_in, rw_mu, rw_w0, rw_w2, rw_a0, rw_a2, rw_g2, rw_k_k, rw_k_a, rw_r_k,
           rw_ln_g, rw_ln_b, da_q_norm_g, da_k_norm_g, da_lq1, da_lk1, da_lq2, da_lk2, da_subln_g, w_out,
           norm_ffn_g, w_router, b_router, w1, b1, w2, b2, w_ple, ple_norm_g, w_ple_gate):
    b, t, d = h3.shape
    n = b * t
    c_rw = rw_k_k.shape[0]
    lora = rw_w2.shape[1]
    rw_cols = rw_mu.shape[0]
    rw_heads = rw_r_k.shape[0]
    c_da = w_in.shape[1] - rw_cols
    assert c_da % 3 == 0
    c_da //= 3
    qk_dim = da_q_norm_g.shape[0]
    v_dim = da_subln_g.shape[0]
    da_heads = c_da // v_dim
    assert rw_heads * RW_HEAD == c_rw and c_rw % HEAD_GROUP == 0 and v_dim == V7X_LANES and 2 * qk_dim == v_dim
    assert 2 * lora == V7X_LANES and w_router.shape[1] == N_EXPERTS

    x2 = h3.reshape(n, d)
    row = lambda a: a.reshape(1, -1).astype(F32)

    gq_t = row(jnp.tile(da_q_norm_g, c_da // qk_dim))
    gk_t = row(jnp.tile(da_k_norm_g, c_da // qk_dim))
    gm_qk = _group_matrix(c_da, qk_dim, 1.0 / qk_dim)
    ta = _tile(t, ATTN_BLOCK)
    zrw, q, k, vt = _inproj(x2, row(norm_mix_g), w_in.astype(BF16), gq_t, gk_t, gm_qk, rw_cols, c_da,
                            1.0 / math.sqrt(qk_dim), _tile(ta, ROW_TILE), b, da_heads, ta)

    zeros = jnp.zeros((lora, c_rw), F32)
    w2bd = jnp.concatenate([jnp.concatenate([rw_w2[0], zeros], axis=1),
                            jnp.concatenate([zeros, rw_w2[1]], axis=1)], axis=0).astype(BF16)
    a2bd = jnp.concatenate([jnp.concatenate([rw_a2[0], zeros], axis=1),
                            jnp.concatenate([zeros, rw_a2[1]], axis=1)], axis=0).astype(BF16)
    gsum = _group_matrix(c_rw, RW_HEAD, 1.0)
    r, vv, nkk, lw, kd, bd, g, bonus = _rwkv_prep(
        zrw.reshape(b, t, rw_cols), row(rw_mu), w2bd, row(rw_w0), a2bd, row(rw_a0), rw_g2.astype(BF16),
        row(rw_k_k), row(rw_k_a), row(rw_r_k), gsum, c_rw, lora, _tile(t, ROW_TILE))
    yf, yb = _rwkv_scan(r, vv, nkk, lw, kd, bd, _tile(t, SCAN_TILE))

    slopes = 2.0 ** (-8.0 * jnp.arange(1, da_heads + 1, dtype=F32) / da_heads)
    q3 = q.reshape(b, t, c_da)
    k3 = k.reshape(b, t, c_da)
    small = (row(da_lq1), row(da_lk1), row(da_lq2), row(da_lk2), row(da_subln_g))

    def attention_running_max():
        v3 = vt[:, :, :, :v_dim, :].transpose(0, 2, 4, 1, 3).reshape(b, t, c_da)
        return _attention(slopes, q3, k3, v3, *small, da_heads, v_dim, lambda_init, ta, ta)

    score_bound = 1.01 * math.sqrt(qk_dim) * jnp.max(jnp.abs(da_q_norm_g)) * jnp.max(jnp.abs(da_k_norm_g))
    mb = jnp.ceil(2.0 * score_bound.astype(F32)) * 0.5
    o_da = lax.cond(
        mb <= MAX_FIXED_SHIFT,
        lambda: _attention_fixed(mb, q3, k3, vt, *small, da_heads, v_dim, lambda_init, ta),
        attention_running_max)

    gmean = _group_matrix(c_rw, RW_HEAD, 1.0 / RW_HEAD)
    w_out_bf = w_out.astype(BF16)
    h1, m, idx, gates, counts = _post(
        yf.reshape(n, c_rw), yb.reshape(n, c_rw), bonus.reshape(n, c_rw), g.reshape(n, c_rw), o_da.reshape(n, c_da), x2,
        row(rw_ln_g), row(rw_ln_b), gmean, w_out_bf[:c_rw], w_out_bf[c_rw:], row(norm_ffn_g),
        w_router.astype(F32), row(b_router), RW_HEAD * 1e-5, _tile(n, ROW_TILE))

    bm = MOE_BLOCK_ROWS
    counts = counts.reshape(N_EXPERTS).astype(jnp.int32)
    padded = (counts + bm - 1) // bm * bm
    pend = jnp.cumsum(padded)
    offs = (pend - padded).astype(jnp.int32)
    slot = _moe_rank(idx, offs.astype(F32).reshape(1, N_EXPERTS), _tile(n, ROW_TILE))
    n_blk = (n * TOP_K) // bm + N_EXPERTS
    blk_start = jnp.arange(n_blk, dtype=jnp.int32) * bm
    blk_e = jnp.minimum(jnp.sum(blk_start[:, None] >= pend[None, :], axis=1), N_EXPERTS - 1).astype(jnp.int32)
    n_used = (pend[-1] // bm).astype(jnp.int32).reshape(1)
    slot_flat = slot.reshape(n * TOP_K)
    xs = _dispatch(slot_flat, m, d // V7X_LANES, n_blk * bm, _tile(n, ROW_TILE))
    ys = _moe_ffn(blk_e, n_used, xs, w1, b1, w2, b2, bm)
    out = _combine_ple(slot_flat, h1, gates, ys, p3.reshape(n, -1), w_ple.astype(BF16), row(ple_norm_g),
                       w_ple_gate.astype(BF16), _tile(n, COMBINE_TILE))
    return out.reshape(b, t, d)


def kernel(x, p, norm_mix_g, w_in, rw_mu, rw_w0, rw_w2, rw_a0, rw_a2, rw_g2, rw_k_k, rw_k_a, rw_r_k, rw_ln_g,
           rw_ln_b, da_q_norm_g, da_k_norm_g, da_lq1, da_lk1, da_lq2, da_lk2, da_subln_g, w_out, norm_ffn_g,
           w_router, b_router, w1, b1, w2, b2, w_ple, ple_norm_g, w_ple_gate):
    h = x.astype(F32)
    params = (norm_mix_g, w_in, rw_mu, rw_w0, rw_w2, rw_a0, rw_a2, rw_g2, rw_k_k, rw_k_a, rw_r_k, rw_ln_g, rw_ln_b,
              da_q_norm_g, da_k_norm_g, da_lq1, da_lk1, da_lq2, da_lk2, da_subln_g, w_out, norm_ffn_g,
              w_router, b_router, w1, b1, w2, b2, w_ple, ple_norm_g, w_ple_gate)
    for i in range(p.shape[0]):
        lambda_init = 0.8 - 0.6 * math.exp(-0.3 * i)
        h = _layer(h, p[i], lambda_init, *(a[i] for a in params))
    return h.astype(x.dtype)
```

```python
import functools
import math

import jax
import jax.numpy as jnp
import numpy as np
from jax import lax
from jax.experimental import pallas as pl
from jax.experimental.pallas import tpu as pltpu

F32 = jnp.float32
BF16 = jnp.bfloat16

V7X_LANES = 128
V7X_VMEM_BYTES = 64 * 1024 * 1024
VMEM_LIMIT = V7X_VMEM_BYTES - 12 * 1024 * 1024

ROW_TILE = 512
COMBINE_TILE = 256
SCAN_TILE = 256

NORM_EPS = 1e-6
RW_HEAD = 64
CHUNK = 64
HEAD_GROUP = 256
N_EXPERTS = 32
TOP_K = 4
MOE_BLOCK_ROWS = 512
SWIGLU_ALPHA = 1.702
SWIGLU_LIMIT = 7.0


def _cparams(sem, **kw):
    return pltpu.CompilerParams(dimension_semantics=sem, vmem_limit_bytes=VMEM_LIMIT, **kw)


def _dot(a, b):
    return jnp.dot(a, b, preferred_element_type=F32)


def _dot_hilo_lhs(g_bf16, x):
    hi, lo = _split2(x)
    return _dot(g_bf16, hi) + _dot(g_bf16, lo)


def _split2(x):
    hi = x.astype(BF16)
    return hi, (x - hi.astype(F32)).astype(BF16)


def _dot_hilo_rhs(x, g_bf16):
    hi, lo = _split2(x)
    out = _dot(jnp.concatenate([hi, lo], axis=0), g_bf16)
    return out[:x.shape[0]] + out[x.shape[0]:]


def _dot_hilo(a, b):
    ah, al = _split2(a)
    bh, bl = _split2(b)
    return _dot(ah, bh) + (_dot(ah, bl) + _dot(al, bh))


def _store_rows(ref, x2):
    rows, d = x2.shape
    c = d // V7X_LANES
    for j in range(c):
        ref[pl.ds(j, rows, stride=c), :] = x2[:, j * V7X_LANES:(j + 1) * V7X_LANES]


def _load_rows(ref, c):
    rows = ref.shape[0] // c
    return jnp.concatenate([ref[pl.ds(j, rows, stride=c), :] for j in range(c)], axis=1)


def _group_matrix(width, group, value):
    g = np.arange(width) // group
    return jnp.asarray(np.where(g[:, None] == g[None, :], value, 0.0), BF16)


def _inproj_kernel(x_ref, g_ref, w_ref, gq_ref, gk_ref, gm_ref, zrw_ref, q_ref, k_ref, vt_ref, *, rw_cols, c_da, scale):
    x = x_ref[...]
    ms = jnp.mean(x * x, axis=-1, keepdims=True)
    u = (x * lax.rsqrt(ms + NORM_EPS) * g_ref[...]).astype(BF16)
    z = _dot(u, w_ref[...])
    zrw_ref[...] = z[:, :rw_cols]
    zq = z[:, rw_cols:rw_cols + c_da]
    zk = z[:, rw_cols + c_da:rw_cols + 2 * c_da]
    zv = z[:, rw_cols + 2 * c_da:]
    tm = zq.shape[0]
    msqk = _dot(jnp.concatenate([zq * zq, zk * zk], axis=0).astype(BF16), gm_ref[...])
    q_ref[...] = (zq * lax.rsqrt(msqk[:tm] + NORM_EPS) * (gq_ref[...] * scale)).astype(BF16)
    k_ref[...] = (zk * lax.rsqrt(msqk[tm:] + NORM_EPS) * gk_ref[...]).astype(BF16)
    heads, vrows, _ = vt_ref.shape
    v_dim = c_da // heads
    pad_row = lax.broadcasted_iota(jnp.int32, (vrows - v_dim, tm), 0)
    pad = jnp.where(pad_row == 0, 1.0, 0.0).astype(BF16)
    for hd in range(heads):
        vt_ref[hd, 0:v_dim, :] = zv[:, hd * v_dim:(hd + 1) * v_dim].T.astype(BF16)
        vt_ref[hd, v_dim:, :] = pad


def _inproj(x2, g, w_bf, gq_t, gk_t, gm, rw_cols, c_da, scale, tm, b, heads, tb):
    n, d = x2.shape
    cols = w_bf.shape[1]
    t = n // b
    tps = t // tm
    per = tb // tm
    vrows = c_da // heads + ATTN_V_PAD
    full = lambda i: (0, 0)
    row = lambda i: (i, 0)
    return pl.pallas_call(
        functools.partial(_inproj_kernel, rw_cols=rw_cols, c_da=c_da, scale=scale),
        grid=(n // tm,),
        in_specs=[
            pl.BlockSpec((tm, d), row),
            pl.BlockSpec((1, d), full),
            pl.BlockSpec((d, cols), full),
            pl.BlockSpec((1, c_da), full),
            pl.BlockSpec((1, c_da), full),
            pl.BlockSpec((c_da, c_da), full),
        ],
        out_specs=[
            pl.BlockSpec((tm, rw_cols), row),
            pl.BlockSpec((tm, c_da), row),
            pl.BlockSpec((tm, c_da), row),
            pl.BlockSpec((None, heads, None, vrows, tm),
                         lambda i: (i // tps, 0, (i % tps) // per, 0, (i % tps) % per)),
        ],
        out_shape=[
            jax.ShapeDtypeStruct((n, rw_cols), F32),
            jax.ShapeDtypeStruct((n, c_da), BF16),
            jax.ShapeDtypeStruct((n, c_da), BF16),
            jax.ShapeDtypeStruct((b, heads, t // tb, vrows, tb), BF16),
        ],
        compiler_params=_cparams(("parallel",)),
        name="inproj",
    )(x2, g, w_bf, gq_t, gk_t, gm)


def _rwkv_prep_kernel(z_ref, zp_ref, zn_ref, mu_ref, w2_ref, w0_ref, a2_ref, a0_ref, g2_ref, kk_ref, ka_ref,
                      rk_ref, gsum_ref,
                      r_ref, v_ref, nkk_ref, lw_ref, kd_ref, bd_ref, g_ref, bonus_ref, *, c_rw, lora):
    i = pl.program_id(1)
    nt = pl.num_programs(1)
    z = z_ref[...]
    tt = z.shape[0]
    rows = lax.broadcasted_iota(jnp.int32, z.shape, 0)
    prev_row = jnp.where(i == 0, 0.0, zp_ref[7:8, :])
    next_row = jnp.where(i == nt - 1, 0.0, zn_ref[0:1, :])
    zprev = jnp.where(rows == 0, prev_row, pltpu.roll(z, 1, 0))
    znext = jnp.where(rows == tt - 1, next_row, pltpu.roll(z, tt - 1, 0))
    z = z + (0.5 * (zprev + znext) - z) * mu_ref[...]

    r = z[:, 0:c_rw]
    k = z[:, c_rw:2 * c_rw]
    v = z[:, 2 * c_rw:3 * c_rw]
    o = 3 * c_rw
    wd = z[:, o:o + 2 * lora]
    ad = z[:, o + 2 * lora:o + 4 * lora]
    gd = z[:, o + 4 * lora:]

    xw = w0_ref[...] + _dot(jnp.tanh(wd).astype(BF16), w2_ref[...])
    xa = a0_ref[...] + _dot(ad.astype(BF16), a2_ref[...])
    lw = -math.exp(-0.5) * jax.nn.sigmoid(xw)
    rate = jax.nn.sigmoid(xa)
    g = _dot(jax.nn.sigmoid(gd).astype(BF16), g2_ref[...])

    gsum = gsum_ref[...]
    kkr = k * kk_ref[...]
    ka = ka_ref[...]
    kd0 = k * (1.0 + (rate[:, :c_rw] - 1.0) * ka)
    kd1 = k * (1.0 + (rate[:, c_rw:] - 1.0) * ka)
    sums = _dot_hilo_rhs(jnp.concatenate([kkr * kkr, r * (0.5 * (kd0 + kd1)) * rk_ref[...]], axis=0), gsum)
    kk = kkr * lax.rsqrt(sums[:tt] + 1e-12)
    bonus = sums[tt:] * v

    r_ref[...] = r.astype(BF16)
    v_ref[...] = v.astype(BF16)
    nkk_ref[...] = (-kk).astype(BF16)
    lw_ref[0] = lw[:, :c_rw]
    lw_ref[1] = lw[:, c_rw:]
    kd_ref[0] = kd0.astype(BF16)
    kd_ref[1] = kd1.astype(BF16)
    bd_ref[0] = (kk * rate[:, :c_rw]).astype(BF16)
    bd_ref[1] = (kk * rate[:, c_rw:]).astype(BF16)
    g_ref[...] = g.astype(BF16)
    bonus_ref[...] = bonus.astype(BF16)


def _rwkv_prep(zrw3, mu, w2bd, w0f, a2bd, a0f, g2_bf, k_k, k_a, r_kf, gsum, c_rw, lora, tt):
    b, t, cols = zrw3.shape
    nt = t // tt
    hb = tt // 8
    nhb = t // 8
    full = lambda bi, i: (0, 0)
    tile = lambda bi, i: (bi, i, 0)
    dtile = lambda bi, i: (0, bi, i, 0)
    one = jax.ShapeDtypeStruct((b, t, c_rw), F32)
    two = jax.ShapeDtypeStruct((2, b, t, c_rw), F32)
    one16 = jax.ShapeDtypeStruct((b, t, c_rw), BF16)
    two16 = jax.ShapeDtypeStruct((2, b, t, c_rw), BF16)
    return pl.pallas_call(
        functools.partial(_rwkv_prep_kernel, c_rw=c_rw, lora=lora),
        grid=(b, nt),
        in_specs=[
            pl.BlockSpec((None, tt, cols), tile),
            pl.BlockSpec((None, 8, cols), lambda bi, i: (bi, jnp.maximum(i * hb - 1, 0), 0)),
            pl.BlockSpec((None, 8, cols), lambda bi, i: (bi, jnp.minimum((i + 1) * hb, nhb - 1), 0)),
            pl.BlockSpec((1, cols), full),
            pl.BlockSpec(w2bd.shape, full),
            pl.BlockSpec((1, 2 * c_rw), full),
            pl.BlockSpec(a2bd.shape, full),
            pl.BlockSpec((1, 2 * c_rw), full),
            pl.BlockSpec(g2_bf.shape, full),
            pl.BlockSpec((1, c_rw), full),
            pl.BlockSpec((1, c_rw), full),
            pl.BlockSpec((1, c_rw), full),
            pl.BlockSpec((c_rw, c_rw), full),
        ],
        out_specs=[
            pl.BlockSpec((None, tt, c_rw), tile),
            pl.BlockSpec((None, tt, c_rw), tile),
            pl.BlockSpec((None, tt, c_rw), tile),
            pl.BlockSpec((2, None, tt, c_rw), dtile),
            pl.BlockSpec((2, None, tt, c_rw), dtile),
            pl.BlockSpec((2, None, tt, c_rw), dtile),
            pl.BlockSpec((None, tt, c_rw), tile),
            pl.BlockSpec((None, tt, c_rw), tile),
        ],
        out_shape=[one16, one16, one16, two, two16, two16, one16, one16],
        compiler_params=_cparams(("parallel", "parallel")),
        name="rwkv_prep",
    )(zrw3, zrw3, zrw3, mu, w2bd, w0f, a2bd, a0f, g2_bf, k_k, k_a, r_kf, gsum)


M_STRICT, M_INCL, M_EYE, M_PAIR, M_LEVEL = 0, 1, 2, 3, 4
N_LEVELS = 5
SCAN_SEQS = 2


def _scan_tables(w):
    c = CHUNK
    masks = []
    mcums = []
    for d in range(2):
        t = np.broadcast_to(np.arange(c)[:, None], (c, w))
        s = np.broadcast_to((np.arange(w) % RW_HEAD)[None, :], (c, w))
        tt = np.broadcast_to(np.arange(c)[:, None], (c, c))
        ss = np.broadcast_to(np.arange(c)[None, :], (c, c))
        if d == 1:
            t, s, tt, ss = c - 1 - t, c - 1 - s, c - 1 - tt, c - 1 - ss
        rows = [s < t, s <= t, s == t, (t // 2 == s // 2) & (t > s)]
        sz = 2
        while sz < c:
            rows.append((t // (2 * sz) == s // (2 * sz)) & ((t // sz) % 2 == 1) & ((s // sz) % 2 == 0))
            sz *= 2
        assert len(rows) == M_LEVEL + N_LEVELS
        masks.append(np.stack(rows).astype(np.float32))
        mcums.append((ss <= tt).astype(np.float32))
    return jnp.asarray(np.stack(masks), F32), jnp.asarray(np.stack(mcums), BF16)


def _block_diag(y, bd16):
    g = y.shape[1] // RW_HEAD
    return jnp.concatenate([y.astype(BF16)] * g, axis=0) * bd16


def _bmm(x, y, bd16):
    outs = []
    for s in range(0, y.shape[1], HEAD_GROUP):
        outs.append(_dot(x[:, s:s + HEAD_GROUP].astype(BF16), _block_diag(y[:, s:s + HEAD_GROUP], bd16)))
    return jnp.concatenate(outs, axis=1)


def _bmm_nt(x, y, bd16):
    outs = []
    for s in range(0, y.shape[1], HEAD_GROUP):
        bd = _block_diag(y[:, s:s + HEAD_GROUP], bd16)
        outs.append(lax.dot_general(x[:, s:s + HEAD_GROUP].astype(BF16), bd, (((1,), (1,)), ((), ())),
                                    preferred_element_type=F32))
    return jnp.concatenate(outs, axis=1)


def _bmm_tn(x, y, bd32):
    outs = []
    for s in range(0, y.shape[1], HEAD_GROUP):
        xt = x[:, s:s + HEAD_GROUP].T.astype(BF16)
        full = _dot(xt, y[:, s:s + HEAD_GROUP].astype(BF16)) * bd32
        o = full[0:RW_HEAD]
        for j in range(1, HEAD_GROUP // RW_HEAD):
            o = o + full[j * RW_HEAD:(j + 1) * RW_HEAD]
        outs.append(o)
    return jnp.concatenate(outs, axis=1)


def _scan_chunks(r, lw, k, v, a, b, state, masks, mcum, bd16, bd32):
    c = r[0].shape[0]

    def each(f, *cols):
        return [f(*xs) for xs in zip(*cols)]

    cum = each(_dot_hilo_lhs, mcum, lw)
    tot = each(lambda x: jnp.sum(x, axis=0, keepdims=True), lw)
    e_neg = each(lambda x: jnp.exp(-x), cum)
    e_tail = each(lambda t_, x: jnp.exp(t_ - x), tot, cum)
    rt = each(lambda x, cm: x * jnp.exp(cm), r, cum)
    at = each(lambda x, cm, l_: x * jnp.exp(cm - l_), a, cum, lw)
    bt = each(jnp.multiply, b, e_neg)
    kt = each(jnp.multiply, k, e_neg)
    bh = each(jnp.multiply, b, e_tail)
    kh = each(jnp.multiply, k, e_tail)

    ar = each(lambda x, y_: jnp.concatenate([x, y_], axis=0), at, rt)
    pb = each(lambda x, y_: _bmm_nt(x, y_, bd16), ar, bt)
    pk = each(lambda x, y_: _bmm_nt(x, y_, bd16), ar, kt)
    a_ab = each(lambda x, m: x[:c] * m[M_STRICT], pb, masks)
    a_rb = each(lambda x, m: x[c:] * m[M_INCL], pb, masks)
    a_ak = each(lambda x, m: x[:c] * m[M_STRICT], pk, masks)
    a_rk = each(lambda x, m: x[c:] * m[M_INCL], pk, masks)

    tinv = each(lambda x, m: m[M_EYE] + x * m[M_PAIR], a_ab, masks)
    for lvl in range(N_LEVELS):
        x = each(lambda aa, m, ti: _bmm(aa * m[M_LEVEL + lvl], ti, bd16), a_ab, masks, tinv)
        tinv = each(lambda ti, x_: ti + _bmm(ti, x_, bd16), tinv, x)

    bmm = lambda p, q: _bmm(p, q, bd16)
    stack = lambda p, q: jnp.concatenate([p, q], axis=0)
    att = each(bmm, tinv, at)
    akv = each(lambda p, p2, q: _bmm(stack(p, p2), q, bd16), a_ak, a_rk, v)
    vt = each(lambda ti, x_: _bmm(ti, x_[:c], bd16), tinv, akv)
    rh = each(lambda x, p, q: x + _bmm(p, q, bd16), rt, a_rb, att)
    y_in = each(lambda p, q, x_: _bmm(p, q, bd16) + x_[c:], a_rb, vt, akv)
    mc = each(lambda m, t_, p, q: m[M_EYE] * jnp.exp(t_) + _bmm_tn(p, q, bd32), masks, tot, bh, att)
    nc = each(lambda p, p2, q, q2: _bmm_tn(stack(p, p2), stack(q, q2), bd32), bh, kh, vt, v)
    onst = each(lambda p, p2, q: _bmm(stack(p, p2), q, bd16), rh, mc, state)
    y = each(lambda x_, yi: x_[:c] + yi, onst, y_in)
    new_state = each(lambda x_, n_: x_[c:] + n_, onst, nc)
    return y, new_state


def _rwkv_scan_kernel(rf_ref, vf_ref, af_ref, rb_ref, vb_ref, ab_ref, lwf_ref, kf_ref, bf_ref, lwb_ref, kb_ref,
                      bb_ref, mask_ref, mcum_ref, bd16_ref, bd32_ref, yf_ref, yb_ref, state_ref, *, n_chunks):
    @pl.when(pl.program_id(1) == 0)
    def _():
        state_ref[...] = jnp.zeros_like(state_ref)

    dirs = ((0, rf_ref, lwf_ref, kf_ref, vf_ref, af_ref, bf_ref, yf_ref),
            (1, rb_ref, lwb_ref, kb_ref, vb_ref, ab_ref, bb_ref, yb_ref))
    n_seq = rf_ref.shape[0]

    def body(j, carry):
        cols = [[] for _ in range(9)]
        where = []
        for d, r_ref, lw_ref, k_ref, v_ref, a_ref, b_ref, y_ref in dirs:
            cj = j if d == 0 else n_chunks - 1 - j
            sl = pl.ds(pl.multiple_of(cj * CHUNK, CHUNK), CHUNK)
            for g in range(n_seq):
                vals = (r_ref[g, sl, :], lw_ref[g, sl, :], k_ref[g, sl, :], v_ref[g, sl, :], a_ref[g, sl, :],
                        b_ref[g, sl, :], state_ref[d, g], mask_ref.at[d], mcum_ref[d])
                for col, val in zip(cols, vals):
                    col.append(val)
                where.append((y_ref, d, g, sl))
        ys, sts = _scan_chunks(*cols, bd16_ref[...], bd32_ref[...])
        for (y_ref, d, g, sl), y, st in zip(where, ys, sts):
            y_ref[g, sl, :] = y.astype(y_ref.dtype)
            state_ref[d, g] = st
        return carry

    lax.fori_loop(0, n_chunks, body, 0)


def _rwkv_scan(r, v, nkk, lw, kd, bd, tb):
    b, t, w = r.shape
    nb = t // tb
    g = SCAN_SEQS if b % SCAN_SEQS == 0 else 1
    masks, mcum = _scan_tables(w)
    bd32 = _group_matrix(HEAD_GROUP, RW_HEAD, 1.0).astype(F32)
    bd16 = bd32.astype(BF16)
    fwd = lambda bi, i: (bi, i, 0)
    bwd = lambda bi, i: (bi, nb - 1 - i, 0)
    fwd_d = lambda bi, i: (0, bi, i, 0)
    bwd_d = lambda bi, i: (1, bi, nb - 1 - i, 0)
    const = lambda nd: (lambda bi, i: (0,) * nd)
    blk = (g, tb, w)
    dblk = (None, g, tb, w)
    out = jax.ShapeDtypeStruct((b, t, w), BF16)
    return pl.pallas_call(
        functools.partial(_rwkv_scan_kernel, n_chunks=tb // CHUNK),
        grid=(b // g, nb),
        in_specs=[
            pl.BlockSpec(blk, fwd), pl.BlockSpec(blk, fwd), pl.BlockSpec(blk, fwd),
            pl.BlockSpec(blk, bwd), pl.BlockSpec(blk, bwd), pl.BlockSpec(blk, bwd),
            pl.BlockSpec(dblk, fwd_d), pl.BlockSpec(dblk, fwd_d), pl.BlockSpec(dblk, fwd_d),
            pl.BlockSpec(dblk, bwd_d), pl.BlockSpec(dblk, bwd_d), pl.BlockSpec(dblk, bwd_d),
            pl.BlockSpec(masks.shape, const(4)),
            pl.BlockSpec(mcum.shape, const(3)),
            pl.BlockSpec(bd16.shape, const(2)),
            pl.BlockSpec(bd32.shape, const(2)),
        ],
        out_specs=[pl.BlockSpec(blk, fwd), pl.BlockSpec(blk, bwd)],
        out_shape=[out, out],
        scratch_shapes=[pltpu.VMEM((2, g, CHUNK, w), F32)],
        compiler_params=_cparams(("parallel", "arbitrary")),
        name="rwkv_scan",
    )(r, v, nkk, r, v, nkk, lw, kd, bd, lw, kd, bd, masks, mcum, bd16, bd32)


def _attn_kernel(slope_ref, q_ref, k_ref, v_ref, lq1_ref, lk1_ref, lq2_ref, lk2_ref, sg_ref, o_ref,
                 m_ref, l_ref, acc_ref, *, qk_dim, lambda_init):
    h = pl.program_id(1)
    i = pl.program_id(2)
    j = pl.program_id(3)
    nk = pl.num_programs(3)
    tq = q_ref.shape[0]
    tk = k_ref.shape[0]

    @pl.when(j == 0)
    def _():
        m_ref[...] = jnp.full(m_ref.shape, -jnp.inf, F32)
        l_ref[...] = jnp.zeros_like(l_ref)
        acc_ref[...] = jnp.zeros_like(acc_ref)

    q = q_ref[...]
    k = k_ref[...]
    v = v_ref[...]
    lane = lax.broadcasted_iota(jnp.int32, q.shape, 1)
    zero = jnp.zeros_like(q)
    qi = lax.broadcasted_iota(jnp.int32, (tq, tk), 0) + i * tq
    kj = lax.broadcasted_iota(jnp.int32, (tq, tk), 1) + j * tk
    bias = slope_ref[h] * jnp.abs(qi - kj).astype(F32)
    for c in range(2):
        qc = jnp.where((lane < qk_dim) == (c == 0), q, zero)
        s = lax.dot_general(qc, k, (((1,), (1,)), ((), ())), preferred_element_type=F32) - bias
        m_old = m_ref[c]
        m_new = jnp.maximum(m_old, jnp.max(s, axis=-1, keepdims=True))
        alpha = jnp.exp(m_old - m_new)
        p = jnp.exp(s - m_new)
        l_ref[c] = alpha * l_ref[c] + jnp.sum(p, axis=-1, keepdims=True)
        acc_ref[c] = alpha * acc_ref[c] + _dot(p.astype(BF16), v)
        m_ref[c] = m_new

    @pl.when(j == nk - 1)
    def _():
        lam = (jnp.exp(jnp.sum(lq1_ref[...] * lk1_ref[...], axis=-1, keepdims=True))
               - jnp.exp(jnp.sum(lq2_ref[...] * lk2_ref[...], axis=-1, keepdims=True)) + lambda_init)
        o = acc_ref[0] / l_ref[0] - lam * (acc_ref[1] / l_ref[1])
        ms = jnp.mean(o * o, axis=-1, keepdims=True)
        o_ref[...] = o * lax.rsqrt(ms + NORM_EPS) * (sg_ref[...] * (1.0 - lambda_init))


def _attention(slopes, q3, k3, v3, lq1, lk1, lq2, lk2, subln_g, heads, v_dim, lambda_init, tq, tk):
    b, t, _ = q3.shape
    small = lambda s, bi, h, i, j: (0, 0)
    grid_spec = pltpu.PrefetchScalarGridSpec(
        num_scalar_prefetch=1,
        grid=(b, heads, t // tq, t // tk),
        in_specs=[
            pl.BlockSpec((None, tq, v_dim), lambda bi, h, i, j, s: (bi, i, h)),
            pl.BlockSpec((None, tk, v_dim), lambda bi, h, i, j, s: (bi, j, h)),
            pl.BlockSpec((None, tk, v_dim), lambda bi, h, i, j, s: (bi, j, h)),
            pl.BlockSpec(lq1.shape, lambda bi, h, i, j, s: (0, 0)),
            pl.BlockSpec(lk1.shape, lambda bi, h, i, j, s: (0, 0)),
            pl.BlockSpec(lq2.shape, lambda bi, h, i, j, s: (0, 0)),
            pl.BlockSpec(lk2.shape, lambda bi, h, i, j, s: (0, 0)),
            pl.BlockSpec(subln_g.shape, lambda bi, h, i, j, s: (0, 0)),
        ],
        out_specs=pl.BlockSpec((None, tq, v_dim), lambda bi, h, i, j, s: (bi, i, h)),
        scratch_shapes=[
            pltpu.VMEM((2, tq, 1), F32),
            pltpu.VMEM((2, tq, 1), F32),
            pltpu.VMEM((2, tq, v_dim), F32),
        ],
    )
    del small
    return pl.pallas_call(
        functools.partial(_attn_kernel, qk_dim=v_dim // 2, lambda_init=lambda_init),
        grid_spec=grid_spec,
        out_shape=jax.ShapeDtypeStruct((b, t, heads * v_dim), F32),
        compiler_params=_cparams(("parallel", "parallel", "parallel", "arbitrary")),
        name="attn",
    )(slopes, q3, k3, v3, lq1, lk1, lq2, lk2, subln_g)


EXP_ZERO_ARG = 104.0
MAX_FIXED_SHIFT = 40.0
N_AUG = 6
ATTN_BLOCK = 512
ATTN_GROUP_SIZES = (4, 2, 1)
ATTN_KEY_SPLIT = 1
ATTN_V_PAD = 16


def _attn_fixed_kernel(w_ref, slope_ref, mb_ref, q_ref, k_ref, vt_ref, dbias_ref, qaug_ref, kaug_ref,
                       lq1_ref, lk1_ref, lq2_ref, lk2_ref, sg_ref, o_ref, acc_ref, *, qk_dim, lambda_init):
    h = pl.program_id(1)
    i = pl.program_id(2)
    tb = q_ref.shape[0]
    nk = k_ref.shape[0] // tb
    v_dim = 2 * qk_dim
    slope = slope_ref[h]
    mb = mb_ref[0]
    w = w_ref[h]

    q = q_ref[...]
    lane = lax.broadcasted_iota(jnp.int32, q.shape, 1)
    lane_row = lax.broadcasted_iota(jnp.int32, (1, q.shape[1]), 1)
    is_c0 = lane < qk_dim
    datas = (is_c0, jnp.logical_not(is_c0))

    def blocks(items, qvar, kvar, diag):
        sub = tb // ATTN_KEY_SPLIT
        qas = [jnp.where(datas[c], q, qaug_ref[qvar, c]) for c in range(2)]
        chains = []
        for n, (j, delta) in enumerate(items):
            kblk = k_ref[pl.ds(pl.multiple_of(j * tb, tb), tb), :]
            vtblk = vt_ref[j]
            for c in range(2):
                base = qk_dim * (1 - c)
                dyn = jnp.where(lane_row == base + 4, -mb,
                                jnp.where(lane_row == base + 5, -(slope * tb) * delta, 0.0)).astype(BF16)
                ka = jnp.where(datas[c], kblk, kaug_ref[kvar, c] + dyn)
                for u in range(ATTN_KEY_SPLIT):
                    rows = slice(u * sub, (u + 1) * sub)
                    chains.append((c, u, qas[c], ka[rows], vtblk[:, rows]))
        sts = [lax.dot_general(ka, qa, (((1,), (1,)), ((), ())), preferred_element_type=F32)
               for _, _, qa, ka, _ in chains]
        if diag:
            sts = [st - dbias_ref[u * sub:(u + 1) * sub, :] for st, (_, u, _, _, _) in zip(sts, chains)]
        pts = [jnp.exp(st).astype(BF16) for st in sts]
        for c in range(2):
            upd = None
            for pt, (cc, _, _, _, vsub) in zip(pts, chains):
                if cc == c:
                    part = _dot(vsub, pt)
                    upd = part if upd is None else upd + part
            if diag:
                acc_ref[c] = upd
            else:
                acc_ref[c] += upd

    def span(first, count, qvar, kvar, dist):
        for per in ATTN_GROUP_SIZES:
            def group(g, carry, first=first, per=per):
                j0 = first + g * per
                blocks([(j0 + n, dist(j0 + n)) for n in range(per)], qvar, kvar, False)
                return carry

            groups = count // per
            lax.fori_loop(0, groups, group, 0)
            first = first + groups * per
            count = count - groups * per

    lo = jnp.maximum(i - w, 0)
    hi = jnp.minimum(i + w, nk - 1)
    blocks([(i, jnp.zeros((), F32))], 2, 0, True)
    span(lo, i - lo, 0, 0, lambda j: (i - j).astype(F32))
    span(i + 1, hi - i, 1, 1, lambda j: (j - i).astype(F32))

    lam = (jnp.exp(jnp.sum(lq1_ref[...] * lk1_ref[...], axis=-1, keepdims=True))
           - jnp.exp(jnp.sum(lq2_ref[...] * lk2_ref[...], axis=-1, keepdims=True)) + lambda_init)
    a0 = acc_ref[0]
    a1 = acc_ref[1]
    ot = a0[:v_dim] / a0[v_dim:v_dim + 1] - lam * (a1[:v_dim] / a1[v_dim:v_dim + 1])
    ms = jnp.mean(ot * ot, axis=0, keepdims=True)
    o_ref[...] = (ot * lax.rsqrt(ms + NORM_EPS) * (sg_ref[...] * (1.0 - lambda_init))).T.astype(o_ref.dtype)


def _attn_aug_tables(slopes, tb, qk_dim):
    heads = len(slopes)
    lanes = 2 * qk_dim
    pos = np.arange(tb)
    qaug = np.zeros((heads, 3, 2, tb, lanes), np.float32)
    kaug = np.zeros((heads, 2, 2, tb, lanes), np.float32)
    for hd, sl in enumerate(slopes):
        for var in range(3):
            rr = pos if var != 1 else tb - 1 - pos
            hi = (rr // 16) * (16.0 * sl)
            lo = (rr % 16) * sl
            for c in range(2):
                base = qk_dim * (1 - c)
                if var == 2:
                    qaug[hd, var, c, :, base + 4] = 1.0
                    continue
                qaug[hd, var, c, :, base + 0] = -hi
                qaug[hd, var, c, :, base + 1] = -lo
                qaug[hd, var, c, :, base + 2:base + N_AUG] = 1.0
                kaug[hd, var, c, :, base + 0:base + 2] = 1.0
                kaug[hd, var, c, :, base + 2] = hi
                kaug[hd, var, c, :, base + 3] = lo
    return jnp.asarray(qaug, BF16), jnp.asarray(kaug, BF16)


def _attention_fixed(mb, q3, k3, vt, lq1, lk1, lq2, lk2, subln_g, heads, v_dim, lambda_init, tb):
    b, t, _ = q3.shape
    qk_dim = v_dim // 2
    slopes = [2.0 ** (-8.0 * (hd + 1) / heads) for hd in range(heads)]
    widths = [int(math.ceil((EXP_ZERO_ARG / s - 1.0) / tb)) for s in slopes]
    qaug, kaug = _attn_aug_tables(slopes, tb, qk_dim)
    pos = np.arange(tb)
    dist = np.abs(pos[:, None] - pos[None, :]).astype(np.float32)
    dbias = jnp.asarray(np.stack([dist * s for s in slopes]), F32)
    nb = t // tb
    vrows = v_dim + ATTN_V_PAD
    assert vt.shape == (b, heads, nb, vrows, tb)
    const = lambda bi, h, i, *_: (0, 0)
    grid_spec = pltpu.PrefetchScalarGridSpec(
        num_scalar_prefetch=3,
        grid=(b, heads, t // tb),
        in_specs=[
            pl.BlockSpec((None, tb, v_dim), lambda bi, h, i, *_: (bi, i, h)),
            pl.BlockSpec((None, t, v_dim), lambda bi, h, i, *_: (bi, 0, h)),
            pl.BlockSpec((None, None, nb, vrows, tb), lambda bi, h, i, *_: (bi, h, 0, 0, 0)),
            pl.BlockSpec((None, tb, tb), lambda bi, h, i, *_: (h, 0, 0)),
            pl.BlockSpec((None, 3, 2, tb, v_dim), lambda bi, h, i, *_: (h, 0, 0, 0, 0)),
            pl.BlockSpec((None, 2, 2, tb, v_dim), lambda bi, h, i, *_: (h, 0, 0, 0, 0)),
            pl.BlockSpec(lq1.shape, const),
            pl.BlockSpec(lk1.shape, const),
            pl.BlockSpec(lq2.shape, const),
            pl.BlockSpec(lk2.shape, const),
            pl.BlockSpec((v_dim, 1), const),
        ],
        out_specs=pl.BlockSpec((None, tb, v_dim), lambda bi, h, i, *_: (bi, i, h)),
        scratch_shapes=[pltpu.VMEM((2, vrows, tb), F32)],
    )
    return pl.pallas_call(
        functools.partial(_attn_fixed_kernel, qk_dim=qk_dim, lambda_init=lambda_init),
        grid_spec=grid_spec,
        out_shape=jax.ShapeDtypeStruct((b, t, heads * v_dim), BF16),
        compiler_params=_cparams(("parallel", "parallel", "parallel")),
        name="attn_fixed",
    )(jnp.asarray(widths, jnp.int32), jnp.asarray(slopes, F32), mb.reshape(1).astype(F32),
      q3, k3, vt, dbias, qaug, kaug, lq1, lk1, lq2, lk2, subln_g.reshape(v_dim, 1))


def _post_kernel(yf_ref, yb_ref, bonus_ref, g_ref, oda_ref, x_ref, lng_ref, lnb_ref, gmean_ref, wo_rw_ref, wo_da_ref,
                 gffn_ref, wr_ref, br_ref, h_ref, m_ref, idx_ref, gate_ref, cnt_ref, *, gn_eps):
    @pl.when(pl.program_id(0) == 0)
    def _():
        cnt_ref[...] = jnp.zeros_like(cnt_ref)

    gmean = gmean_ref[...]
    y = yf_ref[...].astype(F32) + yb_ref[...].astype(F32)
    mean = _dot_hilo_rhs(y, gmean)
    yc = y - mean
    var = _dot_hilo_rhs(yc * yc, gmean)
    yn = yc * lax.rsqrt(var + gn_eps) * lng_ref[...] + lnb_ref[...]
    o_rw = (yn + bonus_ref[...]) * g_ref[...]
    h = x_ref[...] + _dot(o_rw.astype(BF16), wo_rw_ref[...]) + _dot(oda_ref[...].astype(BF16), wo_da_ref[...])
    h_ref[...] = h
    ms = jnp.mean(h * h, axis=-1, keepdims=True)
    m = h * lax.rsqrt(ms + NORM_EPS) * gffn_ref[...]
    _store_rows(m_ref, m)

    logits = _dot_hilo(m, wr_ref[...]) + br_ref[...]
    ne = logits.shape[1]
    lane = lax.broadcasted_iota(jnp.int32, logits.shape, 1).astype(F32)
    kcol = lax.broadcasted_iota(jnp.int32, (logits.shape[0], TOP_K), 1)
    idx_all = jnp.zeros((logits.shape[0], TOP_K), F32)
    val_all = jnp.zeros((logits.shape[0], TOP_K), F32)
    work = logits
    chosen = jnp.zeros(logits.shape, F32)
    for kk in range(TOP_K):
        mx = jnp.max(work, axis=-1, keepdims=True)
        sel = jnp.min(jnp.where(work == mx, lane, float(ne)), axis=-1, keepdims=True)
        idx_all = jnp.where(kcol == kk, sel, idx_all)
        val_all = jnp.where(kcol == kk, mx, val_all)
        hit = lane == sel
        chosen = jnp.where(hit, 1.0, chosen)
        work = jnp.where(hit, -jnp.inf, work)
    e = jnp.exp(val_all - jnp.max(val_all, axis=-1, keepdims=True))
    idx_ref[...] = idx_all.astype(jnp.int32)
    gate_ref[...] = e / jnp.sum(e, axis=-1, keepdims=True)
    cnt_ref[...] += jnp.sum(chosen, axis=0, keepdims=True)


def _post(yf, yb, bonus, g, oda, x2, lng, lnb, gmean, wo_rw, wo_da, gffn, wr, br, gn_eps, tm):
    n, c = yf.shape
    d = x2.shape[1]
    ne = wr.shape[1]
    full = lambda i: (0, 0)
    row = lambda i: (i, 0)
    return pl.pallas_call(
        functools.partial(_post_kernel, gn_eps=gn_eps),
        grid=(n // tm,),
        in_specs=[
            pl.BlockSpec((tm, c), row),
            pl.BlockSpec((tm, c), row),
            pl.BlockSpec((tm, c), row),
            pl.BlockSpec((tm, c), row),
            pl.BlockSpec((tm, c), row),
            pl.BlockSpec((tm, d), row),
            pl.BlockSpec((1, c), full),
            pl.BlockSpec((1, c), full),
            pl.BlockSpec((c, c), full),
            pl.BlockSpec((c, d), full),
            pl.BlockSpec((c, d), full),
            pl.BlockSpec((1, d), full),
            pl.BlockSpec((d, ne), full),
            pl.BlockSpec((1, ne), full),
        ],
        out_specs=[
            pl.BlockSpec((tm, d), row),
            pl.BlockSpec((tm * d // V7X_LANES, V7X_LANES), row),
            pl.BlockSpec((tm, TOP_K), row),
            pl.BlockSpec((tm, TOP_K), row),
            pl.BlockSpec((1, ne), full),
        ],
        out_shape=[
            jax.ShapeDtypeStruct((n, d), F32),
            jax.ShapeDtypeStruct((n * d // V7X_LANES, V7X_LANES), F32),
            jax.ShapeDtypeStruct((n, TOP_K), jnp.int32),
            jax.ShapeDtypeStruct((n, TOP_K), F32),
            jax.ShapeDtypeStruct((1, ne), F32),
        ],
        compiler_params=_cparams(("arbitrary",)),
        name="post",
    )(yf, yb, bonus, g, oda, x2, lng, lnb, gmean, wo_rw, wo_da, gffn, wr, br)


def _rank_kernel(idx_ref, ltri_ref, offs_ref, slot_ref, carry_ref):
    @pl.when(pl.program_id(0) == 0)
    def _():
        carry_ref[...] = offs_ref[...]

    idx = idx_ref[...]
    tr = idx.shape[0]
    lane = lax.broadcasted_iota(jnp.int32, (tr, N_EXPERTS), 1)
    hots = [jnp.where(lane == idx[:, kk:kk + 1], 1.0, 0.0) for kk in range(TOP_K)]
    mask = hots[0]
    for kk in range(1, TOP_K):
        mask = mask + hots[kk]
    before = _dot(ltri_ref[...], mask.astype(BF16)) + carry_ref[...]
    kcol = lax.broadcasted_iota(jnp.int32, (tr, TOP_K), 1)
    slot = jnp.zeros((tr, TOP_K), F32)
    for kk in range(TOP_K):
        rk = jnp.sum(hots[kk] * before, axis=-1, keepdims=True)
        slot = jnp.where(kcol == kk, rk, slot)
    slot_ref[...] = slot.astype(jnp.int32)
    carry_ref[...] = carry_ref[...] + jnp.sum(mask, axis=0, keepdims=True)


def _moe_rank(idx, offs_row, tr):
    n = idx.shape[0]
    ltri = jnp.asarray(np.tril(np.ones((tr, tr), np.float32), -1), BF16)
    return pl.pallas_call(
        _rank_kernel,
        grid=(n // tr,),
        in_specs=[pl.BlockSpec((tr, TOP_K), lambda i: (i, 0)), pl.BlockSpec((tr, tr), lambda i: (0, 0)),
                  pl.BlockSpec((1, N_EXPERTS), lambda i: (0, 0))],
        out_specs=pl.BlockSpec((tr, TOP_K), lambda i: (i, 0)),
        out_shape=jax.ShapeDtypeStruct((n, TOP_K), jnp.int32),
        scratch_shapes=[pltpu.VMEM((1, N_EXPERTS), F32)],
        compiler_params=_cparams(("arbitrary",)),
        name="moe_rank",
    )(idx, ltri, offs_row)


ROW_UNROLL = 8


def _row(r, c):
    return pl.ds(pl.multiple_of(r * c, c), c)


def _dispatch_kernel(slot_ref, m_ref, xs_ref, sem, *, c):
    td = m_ref.shape[0] // c

    def issue(g, carry):
        for u in range(ROW_UNROLL):
            t = g * ROW_UNROLL + u
            for kk in range(TOP_K):
                dest = slot_ref[t * TOP_K + kk]
                pltpu.make_async_copy(m_ref.at[_row(t, c), :], xs_ref.at[_row(dest, c), :], sem).start(
                    priority=kk % 2)
        return carry

    lax.fori_loop(0, td // ROW_UNROLL, issue, 0)
    for kk in range(TOP_K):
        pltpu.make_async_copy(m_ref, xs_ref.at[pl.ds(0, td * c), :], sem).wait()


def _dispatch(slot_flat, mrows, c, p_rows, td):
    n = mrows.shape[0] // c
    assert td % ROW_UNROLL == 0
    return pl.pallas_call(
        functools.partial(_dispatch_kernel, c=c),
        grid=(n // td,),
        in_specs=[
            pl.BlockSpec((td * TOP_K,), lambda i: (i,), memory_space=pltpu.SMEM),
            pl.BlockSpec((td * c, V7X_LANES), lambda i: (i, 0)),
        ],
        out_specs=pl.BlockSpec(memory_space=pl.ANY),
        scratch_shapes=[pltpu.SemaphoreType.DMA(())],
        out_shape=jax.ShapeDtypeStruct((p_rows * c, V7X_LANES), F32),
        compiler_params=_cparams(("arbitrary",), has_side_effects=True),
        name="moe_dispatch",
    )(slot_flat, mrows)


def _ffn_kernel(be_ref, nu_ref, x_ref, w1_ref, b1_ref, w2_ref, b2_ref, y_ref, w1s_ref, w2s_ref, *, d_ff):
    i = pl.program_id(0)

    @pl.when((i == 0) | (be_ref[i] != be_ref[jnp.maximum(i - 1, 0)]))
    def _():
        w1s_ref[...] = w1_ref[...].astype(BF16)
        w2s_ref[...] = w2_ref[...].astype(BF16)

    @pl.when(i < nu_ref[0])
    def _():
        x = _load_rows(x_ref, w1_ref.shape[0] // V7X_LANES).astype(BF16)
        hcat = _dot(x, w1s_ref[...]) + b1_ref[...]
        glu = jnp.minimum(hcat[:, :d_ff], SWIGLU_LIMIT)
        lin = jnp.clip(hcat[:, d_ff:], -SWIGLU_LIMIT, SWIGLU_LIMIT)
        act = glu * jax.nn.sigmoid(SWIGLU_ALPHA * glu) * (lin + 1.0)
        _store_rows(y_ref, _dot(act.astype(BF16), w2s_ref[...]) + b2_ref[...])


def _moe_ffn(blk_e, n_used, xrows, w1, b1, w2, b2, bm):
    ne, d, f2 = w1.shape
    d_ff = f2 // 2
    c = d // V7X_LANES
    p_rows = xrows.shape[0] // c
    blk = (bm * c, V7X_LANES)
    grid_spec = pltpu.PrefetchScalarGridSpec(
        num_scalar_prefetch=2,
        grid=(p_rows // bm,),
        in_specs=[
            pl.BlockSpec(blk, lambda i, be, nu: (jnp.minimum(i, nu[0] - 1), 0)),
            pl.BlockSpec((None, d, f2), lambda i, be, nu: (be[i], 0, 0)),
            pl.BlockSpec((None, 1, f2), lambda i, be, nu: (be[i], 0, 0)),
            pl.BlockSpec((None, d_ff, d), lambda i, be, nu: (be[i], 0, 0)),
            pl.BlockSpec((None, 1, d), lambda i, be, nu: (be[i], 0, 0)),
        ],
        out_specs=pl.BlockSpec(blk, lambda i, be, nu: (i, 0)),
        scratch_shapes=[pltpu.VMEM((d, f2), BF16), pltpu.VMEM((d_ff, d), BF16)],
    )
    return pl.pallas_call(
        functools.partial(_ffn_kernel, d_ff=d_ff),
        grid_spec=grid_spec,
        out_shape=jax.ShapeDtypeStruct(xrows.shape, F32),
        compiler_params=_cparams(("arbitrary",)),
        name="moe_ffn",
    )(blk_e, n_used, xrows, w1, b1.reshape(ne, 1, f2), w2, b2.reshape(ne, 1, d))


def _combine_kernel(slot_ref, next_slot_ref, h_ref, gate_ref, p_ref, wp_ref, gp_ref, wg_ref, y_ref, o_ref,
                    buf_ref, h2_ref, sems):
    i = pl.program_id(0)
    tc, d = h_ref.shape
    c = d // V7X_LANES

    def start_gather(slots, half):
        def issue(g, carry):
            for u in range(ROW_UNROLL):
                t = g * ROW_UNROLL + u
                for kk in range(TOP_K):
                    src = slots[t * TOP_K + kk]
                    pltpu.make_async_copy(y_ref.at[_row(src, c), :], buf_ref.at[half, kk, _row(t, c), :],
                                          sems.at[half]).start(priority=kk % 2)
            return carry

        lax.fori_loop(0, tc // ROW_UNROLL, issue, 0)

    cur = i % 2

    @pl.when(i == 0)
    def _():
        start_gather(slot_ref, 0)

    @pl.when(i + 1 < pl.num_programs(0))
    def _():
        start_gather(next_slot_ref, 1 - cur)

    for kk in range(TOP_K):
        pltpu.make_async_copy(y_ref.at[pl.ds(0, tc * c), :], buf_ref.at[cur, kk], sems.at[cur]).wait()

    gate = gate_ref[...]
    for j in range(c):
        cols = slice(j * V7X_LANES, (j + 1) * V7X_LANES)
        acc = h_ref[:, cols]
        for kk in range(TOP_K):
            acc = acc + gate[:, kk:kk + 1] * buf_ref[cur, kk, pl.ds(j, tc, stride=c), :]
        h2_ref[:, cols] = acc

    h2 = h2_ref[...]
    e = _dot(p_ref[...].astype(BF16), wp_ref[...])
    ms = jnp.mean(e * e, axis=-1, keepdims=True)
    e = e * lax.rsqrt(ms + NORM_EPS) * gp_ref[...]
    o_ref[...] = h2 + e * jax.nn.sigmoid(_dot(h2.astype(BF16), wg_ref[...]))


def _combine_ple(slot_flat, h, gates, yrows, p2, wp_bf, gp, wg_bf, tc):
    n, d = h.shape
    pd = p2.shape[1]
    assert tc % ROW_UNROLL == 0
    last = n // tc - 1
    full = lambda i: (0, 0)
    row = lambda i: (i, 0)
    return pl.pallas_call(
        _combine_kernel,
        grid=(n // tc,),
        in_specs=[
            pl.BlockSpec((tc * TOP_K,), lambda i: (i,), memory_space=pltpu.SMEM),
            pl.BlockSpec((tc * TOP_K,), lambda i: (jnp.minimum(i + 1, last),), memory_space=pltpu.SMEM),
            pl.BlockSpec((tc, d), row),
            pl.BlockSpec((tc, TOP_K), row),
            pl.BlockSpec((tc, pd), row),
            pl.BlockSpec((pd, d), full),
            pl.BlockSpec((1, d), full),
            pl.BlockSpec((d, d), full),
            pl.BlockSpec(memory_space=pl.ANY),
        ],
        out_specs=pl.BlockSpec((tc, d), row),
        scratch_shapes=[pltpu.VMEM((2, TOP_K, tc * d // V7X_LANES, V7X_LANES), F32),
                        pltpu.VMEM((tc, d), F32),
                        pltpu.SemaphoreType.DMA((2,))],
        out_shape=jax.ShapeDtypeStruct((n, d), F32),
        compiler_params=_cparams(("arbitrary",)),
        name="moe_combine_ple",
    )(slot_flat, slot_flat, h, gates, p2, wp_bf, gp, wg_bf, yrows)


def _tile(n, want):
    t = min(n, want)
    assert n % t == 0, (n, t)
    return t


def _layer(h3, p3, lambda_init, norm_mix_g, w_in, rw_mu, rw_w0, rw_w2, rw_a0, rw_a2, rw_g2, rw_k_k, rw_k_a, rw_r_k,
           rw_ln_g, rw_ln_b, da_q_norm_g, da_k_norm_g, da_lq1, da_lk1, da_lq2, da_lk2, da_subln_g, w_out,
           norm_ffn_g, w_router, b_router, w1, b1, w2, b2, w_ple, ple_norm_g, w_ple_gate):
    b, t, d = h3.shape
    n = b * t
    c_rw = rw_k_k.shape[0]
    lora = rw_w2.shape[1]
    rw_cols = rw_mu.shape[0]
    rw_heads = rw_r_k.shape[0]
    c_da = w_in.shape[1] - rw_cols
    assert c_da % 3 == 0
    c_da //= 3
    qk_dim = da_q_norm_g.shape[0]
    v_dim = da_subln_g.shape[0]
    da_heads = c_da // v_dim
    assert rw_heads * RW_HEAD == c_rw and c_rw % HEAD_GROUP == 0 and v_dim == V7X_LANES and 2 * qk_dim == v_dim
    assert 2 * lora == V7X_LANES and w_router.shape[1] == N_EXPERTS

    x2 = h3.reshape(n, d)
    row = lambda a: a.reshape(1, -1).astype(F32)

    gq_t = row(jnp.tile(da_q_norm_g, c_da // qk_dim))
    gk_t = row(jnp.tile(da_k_norm_g, c_da // qk_dim))
    gm_qk = _group_matrix(c_da, qk_dim, 1.0 / qk_dim)
    ta = _tile(t, ATTN_BLOCK)
    zrw, q, k, vt = _inproj(x2, row(norm_mix_g), w_in.astype(BF16), gq_t, gk_t, gm_qk, rw_cols, c_da,
                            1.0 / math.sqrt(qk_dim), _tile(ta, ROW_TILE), b, da_heads, ta)

    zeros = jnp.zeros((lora, c_rw), F32)
    w2bd = jnp.concatenate([jnp.concatenate([rw_w2[0], zeros], axis=1),
                            jnp.concatenate([zeros, rw_w2[1]], axis=1)], axis=0).astype(BF16)
    a2bd = jnp.concatenate([jnp.concatenate([rw_a2[0], zeros], axis=1),
                            jnp.concatenate([zeros, rw_a2[1]], axis=1)], axis=0).astype(BF16)
    gsum = _group_matrix(c_rw, RW_HEAD, 1.0)
    r, vv, nkk, lw, kd, bd, g, bonus = _rwkv_prep(
        zrw.reshape(b, t, rw_cols), row(rw_mu), w2bd, row(rw_w0), a2bd, row(rw_a0), rw_g2.astype(BF16),
        row(rw_k_k), row(rw_k_a), row(rw_r_k), gsum, c_rw, lora, _tile(t, ROW_TILE))
    yf, yb = _rwkv_scan(r, vv, nkk, lw, kd, bd, _tile(t, SCAN_TILE))

    slopes = 2.0 ** (-8.0 * jnp.arange(1, da_heads + 1, dtype=F32) / da_heads)
    q3 = q.reshape(b, t, c_da)
    k3 = k.reshape(b, t, c_da)
    small = (row(da_lq1), row(da_lk1), row(da_lq2), row(da_lk2), row(da_subln_g))

    def attention_running_max():
        v3 = vt[:, :, :, :v_dim, :].transpose(0, 2, 4, 1, 3).reshape(b, t, c_da)
        return _attention(slopes, q3, k3, v3, *small, da_heads, v_dim, lambda_init, ta, ta)

    score_bound = 1.01 * math.sqrt(qk_dim) * jnp.max(jnp.abs(da_q_norm_g)) * jnp.max(jnp.abs(da_k_norm_g))
    mb = jnp.ceil(2.0 * score_bound.astype(F32)) * 0.5
    o_da = lax.cond(
        mb <= MAX_FIXED_SHIFT,
        lambda: _attention_fixed(mb, q3, k3, vt, *small, da_heads, v_dim, lambda_init, ta),
        lambda: attention_running_max().astype(BF16))

    gmean = _group_matrix(c_rw, RW_HEAD, 1.0 / RW_HEAD)
    w_out_bf = w_out.astype(BF16)
    h1, m, idx, gates, counts = _post(
        yf.reshape(n, c_rw), yb.reshape(n, c_rw), bonus.reshape(n, c_rw), g.reshape(n, c_rw), o_da.reshape(n, c_da), x2,
        row(rw_ln_g), row(rw_ln_b), gmean, w_out_bf[:c_rw], w_out_bf[c_rw:], row(norm_ffn_g),
        w_router.astype(F32), row(b_router), RW_HEAD * 1e-5, _tile(n, ROW_TILE))

    bm = MOE_BLOCK_ROWS
    counts = counts.reshape(N_EXPERTS).astype(jnp.int32)
    padded = (counts + bm - 1) // bm * bm
    pend = jnp.cumsum(padded)
    offs = (pend - padded).astype(jnp.int32)
    slot = _moe_rank(idx, offs.astype(F32).reshape(1, N_EXPERTS), _tile(n, ROW_TILE))
    n_blk = (n * TOP_K) // bm + N_EXPERTS
    blk_start = jnp.arange(n_blk, dtype=jnp.int32) * bm
    blk_e = jnp.minimum(jnp.sum(blk_start[:, None] >= pend[None, :], axis=1), N_EXPERTS - 1).astype(jnp.int32)
    n_used = (pend[-1] // bm).astype(jnp.int32).reshape(1)
    slot_flat = slot.reshape(n * TOP_K)
    xs = _dispatch(slot_flat, m, d // V7X_LANES, n_blk * bm, _tile(n, ROW_TILE))
    ys = _moe_ffn(blk_e, n_used, xs, w1, b1, w2, b2, bm)
    out = _combine_ple(slot_flat, h1, gates, ys, p3.reshape(n, -1), w_ple.astype(BF16), row(ple_norm_g),
                       w_ple_gate.astype(BF16), _tile(n, COMBINE_TILE))
    return out.reshape(b, t, d)


def kernel(x, p, norm_mix_g, w_in, rw_mu, rw_w0, rw_w2, rw_a0, rw_a2, rw_g2, rw_k_k, rw_k_a, rw_r_k, rw_ln_g,
           rw_ln_b, da_q_norm_g, da_k_norm_g, da_lq1, da_lk1, da_lq2, da_lk2, da_subln_g, w_out, norm_ffn_g,
           w_router, b_router, w1, b1, w2, b2, w_ple, ple_norm_g, w_ple_gate):
    h = x.astype(F32)
    params = (norm_mix_g, w_in, rw_mu, rw_w0, rw_w2, rw_a0, rw_a2, rw_g2, rw_k_k, rw_k_a, rw_r_k, rw_ln_g, rw_ln_b,
              da_q_norm_g, da_k_norm_g, da_lq1, da_lk1, da_lq2, da_lk2, da_subln_g, w_out, norm_ffn_g,
              w_router, b_router, w1, b1, w2, b2, w_ple, ple_norm_g, w_ple_gate)
    for i in range(p.shape[0]):
        lambda_init = 0.8 - 0.6 * math.exp(-0.3 * i)
        h = _layer(h, p[i], lambda_init, *(a[i] for a in params))
    return h.astype(x.dtype)
```

```python
import functools
import math

import jax
import jax.numpy as jnp
import numpy as np
from jax import lax
from jax.experimental import pallas as pl
from jax.experimental.pallas import tpu as pltpu

F32 = jnp.float32
BF16 = jnp.bfloat16

V7X_LANES = 128
V7X_VMEM_BYTES = 64 * 1024 * 1024
VMEM_LIMIT = V7X_VMEM_BYTES - 12 * 1024 * 1024

ROW_TILE = 512
COMBINE_TILE = 256
SCAN_TILE = 512

NORM_EPS = 1e-6
RW_HEAD = 64
CHUNK = 64
HEAD_GROUP = 256
N_EXPERTS = 32
TOP_K = 4
MOE_BLOCK_ROWS = 512
SWIGLU_ALPHA = 1.702
SWIGLU_LIMIT = 7.0


def _cparams(sem, **kw):
    return pltpu.CompilerParams(dimension_semantics=sem, vmem_limit_bytes=VMEM_LIMIT, **kw)


def _dot(a, b):
    return jnp.dot(a, b, preferred_element_type=F32)


def _dot_hilo_lhs(g_bf16, x):
    hi, lo = _split2(x)
    return _dot(g_bf16, hi) + _dot(g_bf16, lo)


def _split2(x):
    hi = x.astype(BF16)
    return hi, (x - hi.astype(F32)).astype(BF16)


def _dot_hilo_rhs(x, g_bf16):
    hi, lo = _split2(x)
    out = _dot(jnp.concatenate([hi, lo], axis=0), g_bf16)
    return out[:x.shape[0]] + out[x.shape[0]:]


def _dot_hilo(a, b):
    ah, al = _split2(a)
    bh, bl = _split2(b)
    return _dot(ah, bh) + (_dot(ah, bl) + _dot(al, bh))


def _store_rows(ref, x2):
    rows, d = x2.shape
    c = d // V7X_LANES
    for j in range(c):
        ref[pl.ds(j, rows, stride=c), :] = x2[:, j * V7X_LANES:(j + 1) * V7X_LANES]


def _load_rows(ref, c):
    rows = ref.shape[0] // c
    return jnp.concatenate([ref[pl.ds(j, rows, stride=c), :] for j in range(c)], axis=1)


def _group_matrix(width, group, value):
    g = np.arange(width) // group
    return jnp.asarray(np.where(g[:, None] == g[None, :], value, 0.0), BF16)


def _inproj_kernel(x_ref, g_ref, w_ref, gq_ref, gk_ref, gm_ref, zrw_ref, q_ref, k_ref, vt_ref, *, rw_cols, c_da, scale):
    x = x_ref[...]
    ms = jnp.mean(x * x, axis=-1, keepdims=True)
    u = (x * lax.rsqrt(ms + NORM_EPS) * g_ref[...]).astype(BF16)
    z = _dot(u, w_ref[...])
    zrw_ref[...] = z[:, :rw_cols]
    zq = z[:, rw_cols:rw_cols + c_da]
    zk = z[:, rw_cols + c_da:rw_cols + 2 * c_da]
    zv = z[:, rw_cols + 2 * c_da:]
    tm = zq.shape[0]
    msqk = _dot(jnp.concatenate([zq * zq, zk * zk], axis=0).astype(BF16), gm_ref[...])
    q_ref[...] = (zq * lax.rsqrt(msqk[:tm] + NORM_EPS) * (gq_ref[...] * scale)).astype(BF16)
    k_ref[...] = (zk * lax.rsqrt(msqk[tm:] + NORM_EPS) * gk_ref[...]).astype(BF16)
    heads, vrows, _ = vt_ref.shape
    v_dim = c_da // heads
    pad_row = lax.broadcasted_iota(jnp.int32, (vrows - v_dim, tm), 0)
    pad = jnp.where(pad_row == 0, 1.0, 0.0).astype(BF16)
    for hd in range(heads):
        vt_ref[hd, 0:v_dim, :] = zv[:, hd * v_dim:(hd + 1) * v_dim].T.astype(BF16)
        vt_ref[hd, v_dim:, :] = pad


def _inproj(x2, g, w_bf, gq_t, gk_t, gm, rw_cols, c_da, scale, tm, b, heads, tb):
    n, d = x2.shape
    cols = w_bf.shape[1]
    t = n // b
    tps = t // tm
    per = tb // tm
    vrows = c_da // heads + ATTN_V_PAD
    full = lambda i: (0, 0)
    row = lambda i: (i, 0)
    return pl.pallas_call(
        functools.partial(_inproj_kernel, rw_cols=rw_cols, c_da=c_da, scale=scale),
        grid=(n // tm,),
        in_specs=[
            pl.BlockSpec((tm, d), row),
            pl.BlockSpec((1, d), full),
            pl.BlockSpec((d, cols), full),
            pl.BlockSpec((1, c_da), full),
            pl.BlockSpec((1, c_da), full),
            pl.BlockSpec((c_da, c_da), full),
        ],
        out_specs=[
            pl.BlockSpec((tm, rw_cols), row),
            pl.BlockSpec((tm, c_da), row),
            pl.BlockSpec((tm, c_da), row),
            pl.BlockSpec((None, heads, None, vrows, tm),
                         lambda i: (i // tps, 0, (i % tps) // per, 0, (i % tps) % per)),
        ],
        out_shape=[
            jax.ShapeDtypeStruct((n, rw_cols), F32),
            jax.ShapeDtypeStruct((n, c_da), BF16),
            jax.ShapeDtypeStruct((n, c_da), BF16),
            jax.ShapeDtypeStruct((b, heads, t // tb, vrows, tb), BF16),
        ],
        compiler_params=_cparams(("parallel",)),
        name="inproj",
    )(x2, g, w_bf, gq_t, gk_t, gm)


def _rwkv_prep_kernel(z_ref, zp_ref, zn_ref, mu_ref, w2_ref, w0_ref, a2_ref, a0_ref, g2_ref, kk_ref, ka_ref,
                      rk_ref, gsum_ref,
                      r_ref, v_ref, nkk_ref, lw_ref, kd_ref, bd_ref, g_ref, bonus_ref, *, c_rw, lora):
    i = pl.program_id(1)
    nt = pl.num_programs(1)
    z = z_ref[...]
    tt = z.shape[0]
    rows = lax.broadcasted_iota(jnp.int32, z.shape, 0)
    prev_row = jnp.where(i == 0, 0.0, zp_ref[7:8, :])
    next_row = jnp.where(i == nt - 1, 0.0, zn_ref[0:1, :])
    zprev = jnp.where(rows == 0, prev_row, pltpu.roll(z, 1, 0))
    znext = jnp.where(rows == tt - 1, next_row, pltpu.roll(z, tt - 1, 0))
    z = z + (0.5 * (zprev + znext) - z) * mu_ref[...]

    r = z[:, 0:c_rw]
    k = z[:, c_rw:2 * c_rw]
    v = z[:, 2 * c_rw:3 * c_rw]
    o = 3 * c_rw
    wd = z[:, o:o + 2 * lora]
    ad = z[:, o + 2 * lora:o + 4 * lora]
    gd = z[:, o + 4 * lora:]

    xw = w0_ref[...] + _dot(jnp.tanh(wd).astype(BF16), w2_ref[...])
    xa = a0_ref[...] + _dot(ad.astype(BF16), a2_ref[...])
    lw = -math.exp(-0.5) * jax.nn.sigmoid(xw)
    rate = jax.nn.sigmoid(xa)
    g = _dot(jax.nn.sigmoid(gd).astype(BF16), g2_ref[...])

    gsum = gsum_ref[...]
    kkr = k * kk_ref[...]
    ka = ka_ref[...]
    kd0 = k * (1.0 + (rate[:, :c_rw] - 1.0) * ka)
    kd1 = k * (1.0 + (rate[:, c_rw:] - 1.0) * ka)
    sums = _dot_hilo_rhs(jnp.concatenate([kkr * kkr, r * (0.5 * (kd0 + kd1)) * rk_ref[...]], axis=0), gsum)
    kk = kkr * lax.rsqrt(sums[:tt] + 1e-12)
    bonus = sums[tt:] * v

    r_ref[...] = r.astype(BF16)
    v_ref[...] = v.astype(BF16)
    nkk_ref[...] = (-kk).astype(BF16)
    lw_ref[0] = lw[:, :c_rw]
    lw_ref[1] = lw[:, c_rw:]
    kd_ref[0] = kd0.astype(BF16)
    kd_ref[1] = kd1.astype(BF16)
    bd_ref[0] = (kk * rate[:, :c_rw]).astype(BF16)
    bd_ref[1] = (kk * rate[:, c_rw:]).astype(BF16)
    g_ref[...] = g.astype(BF16)
    bonus_ref[...] = bonus.astype(BF16)


def _rwkv_prep(zrw3, mu, w2bd, w0f, a2bd, a0f, g2_bf, k_k, k_a, r_kf, gsum, c_rw, lora, tt):
    b, t, cols = zrw3.shape
    nt = t // tt
    hb = tt // 8
    nhb = t // 8
    full = lambda bi, i: (0, 0)
    tile = lambda bi, i: (bi, i, 0)
    dtile = lambda bi, i: (0, bi, i, 0)
    one = jax.ShapeDtypeStruct((b, t, c_rw), F32)
    two = jax.ShapeDtypeStruct((2, b, t, c_rw), F32)
    one16 = jax.ShapeDtypeStruct((b, t, c_rw), BF16)
    two16 = jax.ShapeDtypeStruct((2, b, t, c_rw), BF16)
    return pl.pallas_call(
        functools.partial(_rwkv_prep_kernel, c_rw=c_rw, lora=lora),
        grid=(b, nt),
        in_specs=[
            pl.BlockSpec((None, tt, cols), tile),
            pl.BlockSpec((None, 8, cols), lambda bi, i: (bi, jnp.maximum(i * hb - 1, 0), 0)),
            pl.BlockSpec((None, 8, cols), lambda bi, i: (bi, jnp.minimum((i + 1) * hb, nhb - 1), 0)),
            pl.BlockSpec((1, cols), full),
            pl.BlockSpec(w2bd.shape, full),
            pl.BlockSpec((1, 2 * c_rw), full),
            pl.BlockSpec(a2bd.shape, full),
            pl.BlockSpec((1, 2 * c_rw), full),
            pl.BlockSpec(g2_bf.shape, full),
            pl.BlockSpec((1, c_rw), full),
            pl.BlockSpec((1, c_rw), full),
            pl.BlockSpec((1, c_rw), full),
            pl.BlockSpec((c_rw, c_rw), full),
        ],
        out_specs=[
            pl.BlockSpec((None, tt, c_rw), tile),
            pl.BlockSpec((None, tt, c_rw), tile),
            pl.BlockSpec((None, tt, c_rw), tile),
            pl.BlockSpec((2, None, tt, c_rw), dtile),
            pl.BlockSpec((2, None, tt, c_rw), dtile),
            pl.BlockSpec((2, None, tt, c_rw), dtile),
            pl.BlockSpec((None, tt, c_rw), tile),
            pl.BlockSpec((None, tt, c_rw), tile),
        ],
        out_shape=[one16, one16, one16, two, two16, two16, one16, one16],
        compiler_params=_cparams(("parallel", "parallel")),
        name="rwkv_prep",
    )(zrw3, zrw3, zrw3, mu, w2bd, w0f, a2bd, a0f, g2_bf, k_k, k_a, r_kf, gsum)


M_STRICT, M_INCL, M_EYE, M_PAIR, M_LEVEL = 0, 1, 2, 3, 4
N_LEVELS = 5
SCAN_SEQS = 2


def _scan_tables(w):
    c = CHUNK
    masks = []
    mcums = []
    for d in range(2):
        t = np.broadcast_to(np.arange(c)[:, None], (c, w))
        s = np.broadcast_to((np.arange(w) % RW_HEAD)[None, :], (c, w))
        tt = np.broadcast_to(np.arange(c)[:, None], (c, c))
        ss = np.broadcast_to(np.arange(c)[None, :], (c, c))
        if d == 1:
            t, s, tt, ss = c - 1 - t, c - 1 - s, c - 1 - tt, c - 1 - ss
        rows = [s < t, s <= t, s == t, (t // 2 == s // 2) & (t > s)]
        sz = 2
        while sz < c:
            rows.append((t // (2 * sz) == s // (2 * sz)) & ((t // sz) % 2 == 1) & ((s // sz) % 2 == 0))
            sz *= 2
        assert len(rows) == M_LEVEL + N_LEVELS
        masks.append(np.stack(rows).astype(np.float32))
        mcums.append((ss <= tt).astype(np.float32))
    return jnp.asarray(np.stack(masks), F32), jnp.asarray(np.stack(mcums), BF16)


def _block_diag(y, bd16):
    g = y.shape[1] // RW_HEAD
    return jnp.concatenate([y.astype(BF16)] * g, axis=0) * bd16


def _bmm(x, y, bd16):
    outs = []
    for s in range(0, y.shape[1], HEAD_GROUP):
        outs.append(_dot(x[:, s:s + HEAD_GROUP].astype(BF16), _block_diag(y[:, s:s + HEAD_GROUP], bd16)))
    return jnp.concatenate(outs, axis=1)


def _bmm_nt(x, y, bd16):
    outs = []
    for s in range(0, y.shape[1], HEAD_GROUP):
        bd = _block_diag(y[:, s:s + HEAD_GROUP], bd16)
        outs.append(lax.dot_general(x[:, s:s + HEAD_GROUP].astype(BF16), bd, (((1,), (1,)), ((), ())),
                                    preferred_element_type=F32))
    return jnp.concatenate(outs, axis=1)


def _bmm_tn(x, y, bd32):
    outs = []
    for s in range(0, y.shape[1], HEAD_GROUP):
        xt = x[:, s:s + HEAD_GROUP].T.astype(BF16)
        full = _dot(xt, y[:, s:s + HEAD_GROUP].astype(BF16)) * bd32
        o = full[0:RW_HEAD]
        for j in range(1, HEAD_GROUP // RW_HEAD):
            o = o + full[j * RW_HEAD:(j + 1) * RW_HEAD]
        outs.append(o)
    return jnp.concatenate(outs, axis=1)


def _scan_chunks(r, lw, k, v, a, b, state, masks, mcum, bd16, bd32):
    c = r[0].shape[0]

    def each(f, *cols):
        return [f(*xs) for xs in zip(*cols)]

    cum = each(_dot_hilo_lhs, mcum, lw)
    tot = each(lambda x: jnp.sum(x, axis=0, keepdims=True), lw)
    e_neg = each(lambda x: jnp.exp(-x), cum)
    e_tail = each(lambda t_, x: jnp.exp(t_ - x), tot, cum)
    rt = each(lambda x, cm: x * jnp.exp(cm), r, cum)
    at = each(lambda x, cm, l_: x * jnp.exp(cm - l_), a, cum, lw)
    bt = each(jnp.multiply, b, e_neg)
    kt = each(jnp.multiply, k, e_neg)
    bh = each(jnp.multiply, b, e_tail)
    kh = each(jnp.multiply, k, e_tail)

    ar = each(lambda x, y_: jnp.concatenate([x, y_], axis=0), at, rt)
    pb = each(lambda x, y_: _bmm_nt(x, y_, bd16), ar, bt)
    pk = each(lambda x, y_: _bmm_nt(x, y_, bd16), ar, kt)
    a_ab = each(lambda x, m: x[:c] * m[M_STRICT], pb, masks)
    a_rb = each(lambda x, m: x[c:] * m[M_INCL], pb, masks)
    a_ak = each(lambda x, m: x[:c] * m[M_STRICT], pk, masks)
    a_rk = each(lambda x, m: x[c:] * m[M_INCL], pk, masks)

    tinv = each(lambda x, m: m[M_EYE] + x * m[M_PAIR], a_ab, masks)
    for lvl in range(N_LEVELS):
        x = each(lambda aa, m, ti: _bmm(aa * m[M_LEVEL + lvl], ti, bd16), a_ab, masks, tinv)
        tinv = each(lambda ti, x_: ti + _bmm(ti, x_, bd16), tinv, x)

    bmm = lambda p, q: _bmm(p, q, bd16)
    stack = lambda p, q: jnp.concatenate([p, q], axis=0)
    att = each(bmm, tinv, at)
    akv = each(lambda p, p2, q: _bmm(stack(p, p2), q, bd16), a_ak, a_rk, v)
    vt = each(lambda ti, x_: _bmm(ti, x_[:c], bd16), tinv, akv)
    rh = each(lambda x, p, q: x + _bmm(p, q, bd16), rt, a_rb, att)
    y_in = each(lambda p, q, x_: _bmm(p, q, bd16) + x_[c:], a_rb, vt, akv)
    mc = each(lambda m, t_, p, q: m[M_EYE] * jnp.exp(t_) + _bmm_tn(p, q, bd32), masks, tot, bh, att)
    nc = each(lambda p, p2, q, q2: _bmm_tn(stack(p, p2), stack(q, q2), bd32), bh, kh, vt, v)
    onst = each(lambda p, p2, q: _bmm(stack(p, p2), q, bd16), rh, mc, state)
    y = each(lambda x_, yi: x_[:c] + yi, onst, y_in)
    new_state = each(lambda x_, n_: x_[c:] + n_, onst, nc)
    return y, new_state


def _rwkv_scan_kernel(rf_ref, vf_ref, af_ref, rb_ref, vb_ref, ab_ref, lwf_ref, kf_ref, bf_ref, lwb_ref, kb_ref,
                      bb_ref, mask_ref, mcum_ref, bd16_ref, bd32_ref, yf_ref, yb_ref, state_ref, *, n_chunks):
    @pl.when(pl.program_id(1) == 0)
    def _():
        state_ref[...] = jnp.zeros_like(state_ref)

    dirs = ((0, rf_ref, lwf_ref, kf_ref, vf_ref, af_ref, bf_ref, yf_ref),
            (1, rb_ref, lwb_ref, kb_ref, vb_ref, ab_ref, bb_ref, yb_ref))
    n_seq = rf_ref.shape[0]

    def body(j, carry):
        cols = [[] for _ in range(9)]
        where = []
        for d, r_ref, lw_ref, k_ref, v_ref, a_ref, b_ref, y_ref in dirs:
            cj = j if d == 0 else n_chunks - 1 - j
            sl = pl.ds(pl.multiple_of(cj * CHUNK, CHUNK), CHUNK)
            for g in range(n_seq):
                vals = (r_ref[g, sl, :], lw_ref[g, sl, :], k_ref[g, sl, :], v_ref[g, sl, :], a_ref[g, sl, :],
                        b_ref[g, sl, :], state_ref[d, g], mask_ref.at[d], mcum_ref[d])
                for col, val in zip(cols, vals):
                    col.append(val)
                where.append((y_ref, d, g, sl))
        ys, sts = _scan_chunks(*cols, bd16_ref[...], bd32_ref[...])
        for (y_ref, d, g, sl), y, st in zip(where, ys, sts):
            y_ref[g, sl, :] = y.astype(y_ref.dtype)
            state_ref[d, g] = st
        return carry

    lax.fori_loop(0, n_chunks, body, 0)


def _rwkv_scan(r, v, nkk, lw, kd, bd, tb):
    b, t, w = r.shape
    nb = t // tb
    g = SCAN_SEQS if b % SCAN_SEQS == 0 else 1
    masks, mcum = _scan_tables(w)
    bd32 = _group_matrix(HEAD_GROUP, RW_HEAD, 1.0).astype(F32)
    bd16 = bd32.astype(BF16)
    fwd = lambda bi, i: (bi, i, 0)
    bwd = lambda bi, i: (bi, nb - 1 - i, 0)
    fwd_d = lambda bi, i: (0, bi, i, 0)
    bwd_d = lambda bi, i: (1, bi, nb - 1 - i, 0)
    const = lambda nd: (lambda bi, i: (0,) * nd)
    blk = (g, tb, w)
    dblk = (None, g, tb, w)
    out = jax.ShapeDtypeStruct((b, t, w), BF16)
    return pl.pallas_call(
        functools.partial(_rwkv_scan_kernel, n_chunks=tb // CHUNK),
        grid=(b // g, nb),
        in_specs=[
            pl.BlockSpec(blk, fwd), pl.BlockSpec(blk, fwd), pl.BlockSpec(blk, fwd),
            pl.BlockSpec(blk, bwd), pl.BlockSpec(blk, bwd), pl.BlockSpec(blk, bwd),
            pl.BlockSpec(dblk, fwd_d), pl.BlockSpec(dblk, fwd_d), pl.BlockSpec(dblk, fwd_d),
            pl.BlockSpec(dblk, bwd_d), pl.BlockSpec(dblk, bwd_d), pl.BlockSpec(dblk, bwd_d),
            pl.BlockSpec(masks.shape, const(4)),
            pl.BlockSpec(mcum.shape, const(3)),
            pl.BlockSpec(bd16.shape, const(2)),
            pl.BlockSpec(bd32.shape, const(2)),
        ],
        out_specs=[pl.BlockSpec(blk, fwd), pl.BlockSpec(blk, bwd)],
        out_shape=[out, out],
        scratch_shapes=[pltpu.VMEM((2, g, CHUNK, w), F32)],
        compiler_params=_cparams(("parallel", "arbitrary")),
        name="rwkv_scan",
    )(r, v, nkk, r, v, nkk, lw, kd, bd, lw, kd, bd, masks, mcum, bd16, bd32)


def _attn_kernel(slope_ref, q_ref, k_ref, v_ref, lq1_ref, lk1_ref, lq2_ref, lk2_ref, sg_ref, o_ref,
                 m_ref, l_ref, acc_ref, *, qk_dim, lambda_init):
    h = pl.program_id(1)
    i = pl.program_id(2)
    j = pl.program_id(3)
    nk = pl.num_programs(3)
    tq = q_ref.shape[0]
    tk = k_ref.shape[0]

    @pl.when(j == 0)
    def _():
        m_ref[...] = jnp.full(m_ref.shape, -jnp.inf, F32)
        l_ref[...] = jnp.zeros_like(l_ref)
        acc_ref[...] = jnp.zeros_like(acc_ref)

    q = q_ref[...]
    k = k_ref[...]
    v = v_ref[...]
    lane = lax.broadcasted_iota(jnp.int32, q.shape, 1)
    zero = jnp.zeros_like(q)
    qi = lax.broadcasted_iota(jnp.int32, (tq, tk), 0) + i * tq
    kj = lax.broadcasted_iota(jnp.int32, (tq, tk), 1) + j * tk
    bias = slope_ref[h] * jnp.abs(qi - kj).astype(F32)
    for c in range(2):
        qc = jnp.where((lane < qk_dim) == (c == 0), q, zero)
        s = lax.dot_general(qc, k, (((1,), (1,)), ((), ())), preferred_element_type=F32) - bias
        m_old = m_ref[c]
        m_new = jnp.maximum(m_old, jnp.max(s, axis=-1, keepdims=True))
        alpha = jnp.exp(m_old - m_new)
        p = jnp.exp(s - m_new)
        l_ref[c] = alpha * l_ref[c] + jnp.sum(p, axis=-1, keepdims=True)
        acc_ref[c] = alpha * acc_ref[c] + _dot(p.astype(BF16), v)
        m_ref[c] = m_new

    @pl.when(j == nk - 1)
    def _():
        lam = (jnp.exp(jnp.sum(lq1_ref[...] * lk1_ref[...], axis=-1, keepdims=True))
               - jnp.exp(jnp.sum(lq2_ref[...] * lk2_ref[...], axis=-1, keepdims=True)) + lambda_init)
        o = acc_ref[0] / l_ref[0] - lam * (acc_ref[1] / l_ref[1])
        ms = jnp.mean(o * o, axis=-1, keepdims=True)
        o_ref[...] = o * lax.rsqrt(ms + NORM_EPS) * (sg_ref[...] * (1.0 - lambda_init))


def _attention(slopes, q3, k3, v3, lq1, lk1, lq2, lk2, subln_g, heads, v_dim, lambda_init, tq, tk):
    b, t, _ = q3.shape
    const = lambda bi, h, i, j, s: (0, 0)
    grid_spec = pltpu.PrefetchScalarGridSpec(
        num_scalar_prefetch=1,
        grid=(b, heads, t // tq, t // tk),
        in_specs=[
            pl.BlockSpec((None, tq, v_dim), lambda bi, h, i, j, s: (bi, i, h)),
            pl.BlockSpec((None, tk, v_dim), lambda bi, h, i, j, s: (bi, j, h)),
            pl.BlockSpec((None, tk, v_dim), lambda bi, h, i, j, s: (bi, j, h)),
            pl.BlockSpec(lq1.shape, const),
            pl.BlockSpec(lk1.shape, const),
            pl.BlockSpec(lq2.shape, const),
            pl.BlockSpec(lk2.shape, const),
            pl.BlockSpec(subln_g.shape, const),
        ],
        out_specs=pl.BlockSpec((None, tq, v_dim), lambda bi, h, i, j, s: (bi, i, h)),
        scratch_shapes=[
            pltpu.VMEM((2, tq, 1), F32),
            pltpu.VMEM((2, tq, 1), F32),
            pltpu.VMEM((2, tq, v_dim), F32),
        ],
    )
    return pl.pallas_call(
        functools.partial(_attn_kernel, qk_dim=v_dim // 2, lambda_init=lambda_init),
        grid_spec=grid_spec,
        out_shape=jax.ShapeDtypeStruct((b, t, heads * v_dim), F32),
        compiler_params=_cparams(("parallel", "parallel", "parallel", "arbitrary")),
        name="attn",
    )(slopes, q3, k3, v3, lq1, lk1, lq2, lk2, subln_g)


EXP_ZERO_ARG = 104.0
MAX_FIXED_SHIFT = 40.0
N_AUG = 6
ATTN_BLOCK = 512
ATTN_GROUP_SIZES = (4, 2, 1)
ATTN_KEY_SPLIT = 1
ATTN_V_PAD = 16


def _attn_fixed_kernel(w_ref, slope_ref, mb_ref, q_ref, k_ref, vt_ref, dbias_ref, qaug_ref, kaug_ref,
                       lq1_ref, lk1_ref, lq2_ref, lk2_ref, sg_ref, o_ref, acc_ref, *, qk_dim, lambda_init):
    h = pl.program_id(1)
    i = pl.program_id(2)
    tb = q_ref.shape[0]
    nk = k_ref.shape[0] // tb
    v_dim = 2 * qk_dim
    slope = slope_ref[h]
    mb = mb_ref[0]
    w = w_ref[h]

    q = q_ref[...]
    lane = lax.broadcasted_iota(jnp.int32, q.shape, 1)
    lane_row = lax.broadcasted_iota(jnp.int32, (1, q.shape[1]), 1)
    is_c0 = lane < qk_dim
    datas = (is_c0, jnp.logical_not(is_c0))

    def blocks(items, qvar, kvar, diag):
        sub = tb // ATTN_KEY_SPLIT
        qas = [jnp.where(datas[c], q, qaug_ref[qvar, c]) for c in range(2)]
        chains = []
        for n, (j, delta) in enumerate(items):
            kblk = k_ref[pl.ds(pl.multiple_of(j * tb, tb), tb), :]
            vtblk = vt_ref[j]
            for c in range(2):
                base = qk_dim * (1 - c)
                dyn = jnp.where(lane_row == base + 4, -mb,
                                jnp.where(lane_row == base + 5, -(slope * tb) * delta, 0.0)).astype(BF16)
                ka = jnp.where(datas[c], kblk, kaug_ref[kvar, c] + dyn)
                for u in range(ATTN_KEY_SPLIT):
                    rows = slice(u * sub, (u + 1) * sub)
                    chains.append((c, u, qas[c], ka[rows], vtblk[:, rows]))
        sts = [lax.dot_general(ka, qa, (((1,), (1,)), ((), ())), preferred_element_type=F32)
               for _, _, qa, ka, _ in chains]
        if diag:
            sts = [st - dbias_ref[u * sub:(u + 1) * sub, :] for st, (_, u, _, _, _) in zip(sts, chains)]
        pts = [jnp.exp(st).astype(BF16) for st in sts]
        for c in range(2):
            upd = None
            for pt, (cc, _, _, _, vsub) in zip(pts, chains):
                if cc == c:
                    part = _dot(vsub, pt)
                    upd = part if upd is None else upd + part
            if diag:
                acc_ref[c] = upd
            else:
                acc_ref[c] += upd

    def span(first, count, qvar, kvar, dist):
        for per in ATTN_GROUP_SIZES:
            def group(g, carry, first=first, per=per):
                j0 = first + g * per
                blocks([(j0 + n, dist(j0 + n)) for n in range(per)], qvar, kvar, False)
                return carry

            groups = count // per
            lax.fori_loop(0, groups, group, 0)
            first = first + groups * per
            count = count - groups * per

    lo = jnp.maximum(i - w, 0)
    hi = jnp.minimum(i + w, nk - 1)
    blocks([(i, jnp.zeros((), F32))], 2, 0, True)
    span(lo, i - lo, 0, 0, lambda j: (i - j).astype(F32))
    span(i + 1, hi - i, 1, 1, lambda j: (j - i).astype(F32))

    lam = (jnp.exp(jnp.sum(lq1_ref[...] * lk1_ref[...], axis=-1, keepdims=True))
           - jnp.exp(jnp.sum(lq2_ref[...] * lk2_ref[...], axis=-1, keepdims=True)) + lambda_init)
    a0 = acc_ref[0]
    a1 = acc_ref[1]
    inv0 = 1.0 / a0[v_dim:v_dim + 1]
    inv1 = lam / a1[v_dim:v_dim + 1]
    ot = a0[:v_dim] * inv0 - a1[:v_dim] * inv1
    ms = jnp.mean(ot * ot, axis=0, keepdims=True)
    o_ref[...] = (ot * lax.rsqrt(ms + NORM_EPS) * (sg_ref[...] * (1.0 - lambda_init))).T.astype(o_ref.dtype)


def _attn_aug_tables(slopes, tb, qk_dim):
    heads = len(slopes)
    lanes = 2 * qk_dim
    pos = np.arange(tb)
    qaug = np.zeros((heads, 3, 2, tb, lanes), np.float32)
    kaug = np.zeros((heads, 2, 2, tb, lanes), np.float32)
    for hd, sl in enumerate(slopes):
        for var in range(3):
            rr = pos if var != 1 else tb - 1 - pos
            hi = (rr // 16) * (16.0 * sl)
            lo = (rr % 16) * sl
            for c in range(2):
                base = qk_dim * (1 - c)
                if var == 2:
                    qaug[hd, var, c, :, base + 4] = 1.0
                    continue
                qaug[hd, var, c, :, base + 0] = -hi
                qaug[hd, var, c, :, base + 1] = -lo
                qaug[hd, var, c, :, base + 2:base + N_AUG] = 1.0
                kaug[hd, var, c, :, base + 0:base + 2] = 1.0
                kaug[hd, var, c, :, base + 2] = hi
                kaug[hd, var, c, :, base + 3] = lo
    return jnp.asarray(qaug, BF16), jnp.asarray(kaug, BF16)


def _attention_fixed(mb, q3, k3, vt, lq1, lk1, lq2, lk2, subln_g, heads, v_dim, lambda_init, tb):
    b, t, _ = q3.shape
    qk_dim = v_dim // 2
    slopes = [2.0 ** (-8.0 * (hd + 1) / heads) for hd in range(heads)]
    widths = [int(math.ceil((EXP_ZERO_ARG / s - 1.0) / tb)) for s in slopes]
    qaug, kaug = _attn_aug_tables(slopes, tb, qk_dim)
    pos = np.arange(tb)
    dist = np.abs(pos[:, None] - pos[None, :]).astype(np.float32)
    dbias = jnp.asarray(np.stack([dist * s for s in slopes]), F32)
    nb = t // tb
    vrows = v_dim + ATTN_V_PAD
    assert vt.shape == (b, heads, nb, vrows, tb)
    const = lambda bi, h, i, *_: (0, 0)
    grid_spec = pltpu.PrefetchScalarGridSpec(
        num_scalar_prefetch=3,
        grid=(b, heads, t // tb),
        in_specs=[
            pl.BlockSpec((None, tb, v_dim), lambda bi, h, i, *_: (bi, i, h)),
            pl.BlockSpec((None, t, v_dim), lambda bi, h, i, *_: (bi, 0, h)),
            pl.BlockSpec((None, None, nb, vrows, tb), lambda bi, h, i, *_: (bi, h, 0, 0, 0)),
            pl.BlockSpec((None, tb, tb), lambda bi, h, i, *_: (h, 0, 0)),
            pl.BlockSpec((None, 3, 2, tb, v_dim), lambda bi, h, i, *_: (h, 0, 0, 0, 0)),
            pl.BlockSpec((None, 2, 2, tb, v_dim), lambda bi, h, i, *_: (h, 0, 0, 0, 0)),
            pl.BlockSpec(lq1.shape, const),
            pl.BlockSpec(lk1.shape, const),
            pl.BlockSpec(lq2.shape, const),
            pl.BlockSpec(lk2.shape, const),
            pl.BlockSpec((v_dim, 1), const),
        ],
        out_specs=pl.BlockSpec((None, tb, v_dim), lambda bi, h, i, *_: (bi, i, h)),
        scratch_shapes=[pltpu.VMEM((2, vrows, tb), F32)],
    )
    return pl.pallas_call(
        functools.partial(_attn_fixed_kernel, qk_dim=qk_dim, lambda_init=lambda_init),
        grid_spec=grid_spec,
        out_shape=jax.ShapeDtypeStruct((b, t, heads * v_dim), BF16),
        compiler_params=_cparams(("parallel", "parallel", "parallel")),
        name="attn_fixed",
    )(jnp.asarray(widths, jnp.int32), jnp.asarray(slopes, F32), mb.reshape(1).astype(F32),
      q3, k3, vt, dbias, qaug, kaug, lq1, lk1, lq2, lk2, subln_g.reshape(v_dim, 1))


def _post_kernel(yf_ref, yb_ref, bonus_ref, g_ref, oda_ref, x_ref, lng_ref, lnb_ref, gmean_ref, wo_rw_ref, wo_da_ref,
                 gffn_ref, wr_ref, br_ref, h_ref, m_ref, idx_ref, gate_ref, cnt_ref, *, gn_eps):
    @pl.when(pl.program_id(0) == 0)
    def _():
        cnt_ref[...] = jnp.zeros_like(cnt_ref)

    gmean = gmean_ref[...]
    y = yf_ref[...].astype(F32) + yb_ref[...].astype(F32)
    mean = _dot_hilo_rhs(y, gmean)
    yc = y - mean
    var = _dot_hilo_rhs(yc * yc, gmean)
    yn = yc * lax.rsqrt(var + gn_eps) * lng_ref[...] + lnb_ref[...]
    o_rw = (yn + bonus_ref[...]) * g_ref[...]
    h = x_ref[...] + _dot(o_rw.astype(BF16), wo_rw_ref[...]) + _dot(oda_ref[...].astype(BF16), wo_da_ref[...])
    h_ref[...] = h
    ms = jnp.mean(h * h, axis=-1, keepdims=True)
    m = h * lax.rsqrt(ms + NORM_EPS) * gffn_ref[...]
    _store_rows(m_ref, m)

    logits = _dot_hilo(m, wr_ref[...]) + br_ref[...]
    ne = logits.shape[1]
    lane = lax.broadcasted_iota(jnp.int32, logits.shape, 1).astype(F32)
    kcol = lax.broadcasted_iota(jnp.int32, (logits.shape[0], TOP_K), 1)
    idx_all = jnp.zeros((logits.shape[0], TOP_K), F32)
    val_all = jnp.zeros((logits.shape[0], TOP_K), F32)
    work = logits
    chosen = jnp.zeros(logits.shape, F32)
    for kk in range(TOP_K):
        mx = jnp.max(work, axis=-1, keepdims=True)
        sel = jnp.min(jnp.where(work == mx, lane, float(ne)), axis=-1, keepdims=True)
        idx_all = jnp.where(kcol == kk, sel, idx_all)
        val_all = jnp.where(kcol == kk, mx, val_all)
        hit = lane == sel
        chosen = jnp.where(hit, 1.0, chosen)
        work = jnp.where(hit, -jnp.inf, work)
    e = jnp.exp(val_all - jnp.max(val_all, axis=-1, keepdims=True))
    idx_ref[...] = idx_all.astype(jnp.int32)
    gate_ref[...] = e / jnp.sum(e, axis=-1, keepdims=True)
    cnt_ref[...] += jnp.sum(chosen, axis=0, keepdims=True)


def _post(yf, yb, bonus, g, oda, x2, lng, lnb, gmean, wo_rw, wo_da, gffn, wr, br, gn_eps, tm):
    n, c = yf.shape
    d = x2.shape[1]
    ne = wr.shape[1]
    full = lambda i: (0, 0)
    row = lambda i: (i, 0)
    return pl.pallas_call(
        functools.partial(_post_kernel, gn_eps=gn_eps),
        grid=(n // tm,),
        in_specs=[
            pl.BlockSpec((tm, c), row),
            pl.BlockSpec((tm, c), row),
            pl.BlockSpec((tm, c), row),
            pl.BlockSpec((tm, c), row),
            pl.BlockSpec((tm, c), row),
            pl.BlockSpec((tm, d), row),
            pl.BlockSpec((1, c), full),
            pl.BlockSpec((1, c), full),
            pl.BlockSpec((c, c), full),
            pl.BlockSpec((c, d), full),
            pl.BlockSpec((c, d), full),
            pl.BlockSpec((1, d), full),
            pl.BlockSpec((d, ne), full),
            pl.BlockSpec((1, ne), full),
        ],
        out_specs=[
            pl.BlockSpec((tm, d), row),
            pl.BlockSpec((tm * d // V7X_LANES, V7X_LANES), row),
            pl.BlockSpec((tm, TOP_K), row),
            pl.BlockSpec((tm, TOP_K), row),
            pl.BlockSpec((1, ne), full),
        ],
        out_shape=[
            jax.ShapeDtypeStruct((n, d), F32),
            jax.ShapeDtypeStruct((n * d // V7X_LANES, V7X_LANES), F32),
            jax.ShapeDtypeStruct((n, TOP_K), jnp.int32),
            jax.ShapeDtypeStruct((n, TOP_K), F32),
            jax.ShapeDtypeStruct((1, ne), F32),
        ],
        compiler_params=_cparams(("arbitrary",)),
        name="post",
    )(yf, yb, bonus, g, oda, x2, lng, lnb, gmean, wo_rw, wo_da, gffn, wr, br)


def _rank_kernel(idx_ref, ltri_ref, offs_ref, slot_ref, carry_ref):
    @pl.when(pl.program_id(0) == 0)
    def _():
        carry_ref[...] = offs_ref[...]

    idx = idx_ref[...]
    tr = idx.shape[0]
    lane = lax.broadcasted_iota(jnp.int32, (tr, N_EXPERTS), 1)
    hots = [jnp.where(lane == idx[:, kk:kk + 1], 1.0, 0.0) for kk in range(TOP_K)]
    mask = hots[0]
    for kk in range(1, TOP_K):
        mask = mask + hots[kk]
    before = _dot(ltri_ref[...], mask.astype(BF16)) + carry_ref[...]
    kcol = lax.broadcasted_iota(jnp.int32, (tr, TOP_K), 1)
    slot = jnp.zeros((tr, TOP_K), F32)
    for kk in range(TOP_K):
        rk = jnp.sum(hots[kk] * before, axis=-1, keepdims=True)
        slot = jnp.where(kcol == kk, rk, slot)
    slot_ref[...] = slot.astype(jnp.int32)
    carry_ref[...] = carry_ref[...] + jnp.sum(mask, axis=0, keepdims=True)


def _moe_rank(idx, offs_row, tr):
    n = idx.shape[0]
    ltri = jnp.asarray(np.tril(np.ones((tr, tr), np.float32), -1), BF16)
    return pl.pallas_call(
        _rank_kernel,
        grid=(n // tr,),
        in_specs=[pl.BlockSpec((tr, TOP_K), lambda i: (i, 0)), pl.BlockSpec((tr, tr), lambda i: (0, 0)),
                  pl.BlockSpec((1, N_EXPERTS), lambda i: (0, 0))],
        out_specs=pl.BlockSpec((tr, TOP_K), lambda i: (i, 0)),
        out_shape=jax.ShapeDtypeStruct((n, TOP_K), jnp.int32),
        scratch_shapes=[pltpu.VMEM((1, N_EXPERTS), F32)],
        compiler_params=_cparams(("arbitrary",)),
        name="moe_rank",
    )(idx, ltri, offs_row)


ROW_UNROLL = 8


def _row(r, c):
    return pl.ds(pl.multiple_of(r * c, c), c)


def _dispatch_kernel(slot_ref, m_ref, xs_ref, sem, *, c):
    td = m_ref.shape[0] // c

    def issue(g, carry):
        for u in range(ROW_UNROLL):
            t = g * ROW_UNROLL + u
            for kk in range(TOP_K):
                dest = slot_ref[t * TOP_K + kk]
                pltpu.make_async_copy(m_ref.at[_row(t, c), :], xs_ref.at[_row(dest, c), :], sem).start(
                    priority=kk % 2)
        return carry

    lax.fori_loop(0, td // ROW_UNROLL, issue, 0)
    for kk in range(TOP_K):
        pltpu.make_async_copy(m_ref, xs_ref.at[pl.ds(0, td * c), :], sem).wait()


def _dispatch(slot_flat, mrows, c, p_rows, td):
    n = mrows.shape[0] // c
    assert td % ROW_UNROLL == 0
    return pl.pallas_call(
        functools.partial(_dispatch_kernel, c=c),
        grid=(n // td,),
        in_specs=[
            pl.BlockSpec((td * TOP_K,), lambda i: (i,), memory_space=pltpu.SMEM),
            pl.BlockSpec((td * c, V7X_LANES), lambda i: (i, 0)),
        ],
        out_specs=pl.BlockSpec(memory_space=pl.ANY),
        scratch_shapes=[pltpu.SemaphoreType.DMA(())],
        out_shape=jax.ShapeDtypeStruct((p_rows * c, V7X_LANES), F32),
        compiler_params=_cparams(("arbitrary",), has_side_effects=True),
        name="moe_dispatch",
    )(slot_flat, mrows)


def _ffn_kernel(be_ref, nu_ref, x_ref, w1_ref, b1_ref, w2_ref, b2_ref, y_ref, w1s_ref, w2s_ref, *, d_ff):
    i = pl.program_id(0)

    @pl.when((i == 0) | (be_ref[i] != be_ref[jnp.maximum(i - 1, 0)]))
    def _():
        w1s_ref[...] = w1_ref[...].astype(BF16)
        w2s_ref[...] = w2_ref[...].astype(BF16)

    @pl.when(i < nu_ref[0])
    def _():
        x = _load_rows(x_ref, w1_ref.shape[0] // V7X_LANES).astype(BF16)
        hcat = _dot(x, w1s_ref[...]) + b1_ref[...]
        glu = jnp.minimum(hcat[:, :d_ff], SWIGLU_LIMIT)
        lin = jnp.clip(hcat[:, d_ff:], -SWIGLU_LIMIT, SWIGLU_LIMIT)
        act = glu * jax.nn.sigmoid(SWIGLU_ALPHA * glu) * (lin + 1.0)
        _store_rows(y_ref, _dot(act.astype(BF16), w2s_ref[...]) + b2_ref[...])


def _moe_ffn(blk_e, n_used, xrows, w1, b1, w2, b2, bm):
    ne, d, f2 = w1.shape
    d_ff = f2 // 2
    c = d // V7X_LANES
    p_rows = xrows.shape[0] // c
    blk = (bm * c, V7X_LANES)
    grid_spec = pltpu.PrefetchScalarGridSpec(
        num_scalar_prefetch=2,
        grid=(p_rows // bm,),
        in_specs=[
            pl.BlockSpec(blk, lambda i, be, nu: (jnp.minimum(i, nu[0] - 1), 0)),
            pl.BlockSpec((None, d, f2), lambda i, be, nu: (be[i], 0, 0)),
            pl.BlockSpec((None, 1, f2), lambda i, be, nu: (be[i], 0, 0)),
            pl.BlockSpec((None, d_ff, d), lambda i, be, nu: (be[i], 0, 0)),
            pl.BlockSpec((None, 1, d), lambda i, be, nu: (be[i], 0, 0)),
        ],
        out_specs=pl.BlockSpec(blk, lambda i, be, nu: (i, 0)),
        scratch_shapes=[pltpu.VMEM((d, f2), BF16), pltpu.VMEM((d_ff, d), BF16)],
    )
    return pl.pallas_call(
        functools.partial(_ffn_kernel, d_ff=d_ff),
        grid_spec=grid_spec,
        out_shape=jax.ShapeDtypeStruct(xrows.shape, F32),
        compiler_params=_cparams(("arbitrary",)),
        name="moe_ffn",
    )(blk_e, n_used, xrows, w1, b1.reshape(ne, 1, f2), w2, b2.reshape(ne, 1, d))


def _combine_kernel(slot_ref, next_slot_ref, h_ref, gate_ref, p_ref, wp_ref, gp_ref, wg_ref, y_ref, o_ref,
                    buf_ref, h2_ref, sems):
    i = pl.program_id(0)
    tc, d = h_ref.shape
    c = d // V7X_LANES

    def start_gather(slots, half):
        def issue(g, carry):
            for u in range(ROW_UNROLL):
                t = g * ROW_UNROLL + u
                for kk in range(TOP_K):
                    src = slots[t * TOP_K + kk]
                    pltpu.make_async_copy(y_ref.at[_row(src, c), :], buf_ref.at[half, kk, _row(t, c), :],
                                          sems.at[half]).start(priority=kk % 2)
            return carry

        lax.fori_loop(0, tc // ROW_UNROLL, issue, 0)

    cur = i % 2

    @pl.when(i == 0)
    def _():
        start_gather(slot_ref, 0)

    @pl.when(i + 1 < pl.num_programs(0))
    def _():
        start_gather(next_slot_ref, 1 - cur)

    for kk in range(TOP_K):
        pltpu.make_async_copy(y_ref.at[pl.ds(0, tc * c), :], buf_ref.at[cur, kk], sems.at[cur]).wait()

    gate = gate_ref[...]
    for j in range(c):
        cols = slice(j * V7X_LANES, (j + 1) * V7X_LANES)
        acc = h_ref[:, cols]
        for kk in range(TOP_K):
            acc = acc + gate[:, kk:kk + 1] * buf_ref[cur, kk, pl.ds(j, tc, stride=c), :]
        h2_ref[:, cols] = acc

    h2 = h2_ref[...]
    e = _dot(p_ref[...].astype(BF16), wp_ref[...])
    ms = jnp.mean(e * e, axis=-1, keepdims=True)
    e = e * lax.rsqrt(ms + NORM_EPS) * gp_ref[...]
    o_ref[...] = h2 + e * jax.nn.sigmoid(_dot(h2.astype(BF16), wg_ref[...]))


def _combine_ple(slot_flat, h, gates, yrows, p2, wp_bf, gp, wg_bf, tc):
    n, d = h.shape
    pd = p2.shape[1]
    assert tc % ROW_UNROLL == 0
    last = n // tc - 1
    full = lambda i: (0, 0)
    row = lambda i: (i, 0)
    return pl.pallas_call(
        _combine_kernel,
        grid=(n // tc,),
        in_specs=[
            pl.BlockSpec((tc * TOP_K,), lambda i: (i,), memory_space=pltpu.SMEM),
            pl.BlockSpec((tc * TOP_K,), lambda i: (jnp.minimum(i + 1, last),), memory_space=pltpu.SMEM),
            pl.BlockSpec((tc, d), row),
            pl.BlockSpec((tc, TOP_K), row),
            pl.BlockSpec((tc, pd), row),
            pl.BlockSpec((pd, d), full),
            pl.BlockSpec((1, d), full),
            pl.BlockSpec((d, d), full),
            pl.BlockSpec(memory_space=pl.ANY),
        ],
        out_specs=pl.BlockSpec((tc, d), row),
        scratch_shapes=[pltpu.VMEM((2, TOP_K, tc * d // V7X_LANES, V7X_LANES), F32),
                        pltpu.VMEM((tc, d), F32),
                        pltpu.SemaphoreType.DMA((2,))],
        out_shape=jax.ShapeDtypeStruct((n, d), F32),
        compiler_params=_cparams(("arbitrary",)),
        name="moe_combine_ple",
    )(slot_flat, slot_flat, h, gates, p2, wp_bf, gp, wg_bf, yrows)


def _tile(n, want):
    t = min(n, want)
    assert n % t == 0, (n, t)
    return t


def _layer(h3, p3, lambda_init, norm_mix_g, w_in, rw_mu, rw_w0, rw_w2, rw_a0, rw_a2, rw_g2, rw_k_k, rw_k_a, rw_r_k,
           rw_ln_g, rw_ln_b, da_q_norm_g, da_k_norm_g, da_lq1, da_lk1, da_lq2, da_lk2, da_subln_g, w_out,
           norm_ffn_g, w_router, b_router, w1, b1, w2, b2, w_ple, ple_norm_g, w_ple_gate):
    b, t, d = h3.shape
    n = b * t
    c_rw = rw_k_k.shape[0]
    lora = rw_w2.shape[1]
    rw_cols = rw_mu.shape[0]
    rw_heads = rw_r_k.shape[0]
    c_da = w_in.shape[1] - rw_cols
    assert c_da % 3 == 0
    c_da //= 3
    qk_dim = da_q_norm_g.shape[0]
    v_dim = da_subln_g.shape[0]
    da_heads = c_da // v_dim
    assert rw_heads * RW_HEAD == c_rw and c_rw % HEAD_GROUP == 0 and v_dim == V7X_LANES and 2 * qk_dim == v_dim
    assert 2 * lora == V7X_LANES and w_router.shape[1] == N_EXPERTS

    x2 = h3.reshape(n, d)
    row = lambda a: a.reshape(1, -1).astype(F32)

    gq_t = row(jnp.tile(da_q_norm_g, c_da // qk_dim))
    gk_t = row(jnp.tile(da_k_norm_g, c_da // qk_dim))
    gm_qk = _group_matrix(c_da, qk_dim, 1.0 / qk_dim)
    ta = _tile(t, ATTN_BLOCK)
    zrw, q, k, vt = _inproj(x2, row(norm_mix_g), w_in.astype(BF16), gq_t, gk_t, gm_qk, rw_cols, c_da,
                            1.0 / math.sqrt(qk_dim), _tile(ta, ROW_TILE), b, da_heads, ta)

    zeros = jnp.zeros((lora, c_rw), F32)
    w2bd = jnp.concatenate([jnp.concatenate([rw_w2[0], zeros], axis=1),
                            jnp.concatenate([zeros, rw_w2[1]], axis=1)], axis=0).astype(BF16)
    a2bd = jnp.concatenate([jnp.concatenate([rw_a2[0], zeros], axis=1),
                            jnp.concatenate([zeros, rw_a2[1]], axis=1)], axis=0).astype(BF16)
    gsum = _group_matrix(c_rw, RW_HEAD, 1.0)
    r, vv, nkk, lw, kd, bd, g, bonus = _rwkv_prep(
        zrw.reshape(b, t, rw_cols), row(rw_mu), w2bd, row(rw_w0), a2bd, row(rw_a0), rw_g2.astype(BF16),
        row(rw_k_k), row(rw_k_a), row(rw_r_k), gsum, c_rw, lora, _tile(t, ROW_TILE))
    yf, yb = _rwkv_scan(r, vv, nkk, lw, kd, bd, _tile(t, SCAN_TILE))

    slopes = 2.0 ** (-8.0 * jnp.arange(1, da_heads + 1, dtype=F32) / da_heads)
    q3 = q.reshape(b, t, c_da)
    k3 = k.reshape(b, t, c_da)
    small = (row(da_lq1), row(da_lk1), row(da_lq2), row(da_lk2), row(da_subln_g))

    def attention_running_max():
        v3 = vt[:, :, :, :v_dim, :].transpose(0, 2, 4, 1, 3).reshape(b, t, c_da)
        return _attention(slopes, q3, k3, v3, *small, da_heads, v_dim, lambda_init, ta, ta)

    score_bound = 1.01 * math.sqrt(qk_dim) * jnp.max(jnp.abs(da_q_norm_g)) * jnp.max(jnp.abs(da_k_norm_g))
    mb = jnp.ceil(2.0 * score_bound.astype(F32)) * 0.5
    o_da = lax.cond(
        mb <= MAX_FIXED_SHIFT,
        lambda: _attention_fixed(mb, q3, k3, vt, *small, da_heads, v_dim, lambda_init, ta),
        lambda: attention_running_max().astype(BF16))

    gmean = _group_matrix(c_rw, RW_HEAD, 1.0 / RW_HEAD)
    w_out_bf = w_out.astype(BF16)
    h1, m, idx, gates, counts = _post(
        yf.reshape(n, c_rw), yb.reshape(n, c_rw), bonus.reshape(n, c_rw), g.reshape(n, c_rw), o_da.reshape(n, c_da), x2,
        row(rw_ln_g), row(rw_ln_b), gmean, w_out_bf[:c_rw], w_out_bf[c_rw:], row(norm_ffn_g),
        w_router.astype(F32), row(b_router), RW_HEAD * 1e-5, _tile(n, ROW_TILE))

    bm = MOE_BLOCK_ROWS
    counts = counts.reshape(N_EXPERTS).astype(jnp.int32)
    padded = (counts + bm - 1) // bm * bm
    pend = jnp.cumsum(padded)
    offs = (pend - padded).astype(jnp.int32)
    slot = _moe_rank(idx, offs.astype(F32).reshape(1, N_EXPERTS), _tile(n, ROW_TILE))
    n_blk = (n * TOP_K) // bm + N_EXPERTS
    blk_start = jnp.arange(n_blk, dtype=jnp.int32) * bm
    blk_e = jnp.minimum(jnp.sum(blk_start[:, None] >= pend[None, :], axis=1), N_EXPERTS - 1).astype(jnp.int32)
    n_used = (pend[-1] // bm).astype(jnp.int32).reshape(1)
    slot_flat = slot.reshape(n * TOP_K)
    xs = _dispatch(slot_flat, m, d // V7X_LANES, n_blk * bm, _tile(n, ROW_TILE))
    ys = _moe_ffn(blk_e, n_used, xs, w1, b1, w2, b2, bm)
    out = _combine_ple(slot_flat, h1, gates, ys, p3.reshape(n, -1), w_ple.astype(BF16), row(ple_norm_g),
                       w_ple_gate.astype(BF16), _tile(n, COMBINE_TILE))
    return out.reshape(b, t, d)


def kernel(x, p, norm_mix_g, w_in, rw_mu, rw_w0, rw_w2, rw_a0, rw_a2, rw_g2, rw_k_k, rw_k_a, rw_r_k, rw_ln_g,
           rw_ln_b, da_q_norm_g, da_k_norm_g, da_lq1, da_lk1, da_lq2, da_lk2, da_subln_g, w_out, norm_ffn_g,
           w_router, b_router, w1, b1, w2, b2, w_ple, ple_norm_g, w_ple_gate):
    h = x.astype(F32)
    params = (norm_mix_g, w_in, rw_mu, rw_w0, rw_w2, rw_a0, rw_a2, rw_g2, rw_k_k, rw_k_a, rw_r_k, rw_ln_g, rw_ln_b,
              da_q_norm_g, da_k_norm_g, da_lq1, da_lk1, da_lq2, da_lk2, da_subln_g, w_out, norm_ffn_g,
              w_router, b_router, w1, b1, w2, b2, w_ple, ple_norm_g, w_ple_gate)
    for i in range(p.shape[0]):
        lambda_init = 0.8 - 0.6 * math.exp(-0.3 * i)
        h = _layer(h, p[i], lambda_init, *(a[i] for a in params))
    return h.astype(x.dtype)
```

```python
import functools
import math

import jax
import jax.numpy as jnp
import numpy as np
from jax import lax
from jax.experimental import pallas as pl
from jax.experimental.pallas import tpu as pltpu

F32 = jnp.float32
BF16 = jnp.bfloat16

V7X_LANES = 128
V7X_VMEM_BYTES = 64 * 1024 * 1024
VMEM_LIMIT = V7X_VMEM_BYTES - 12 * 1024 * 1024

ROW_TILE = 512
DISPATCH_TILE = 1024
COMBINE_TILE = 256
SCAN_TILE = 512

NORM_EPS = 1e-6
RW_HEAD = 64
CHUNK = 64
HEAD_GROUP = 256
N_EXPERTS = 32
TOP_K = 4
MOE_BLOCK_ROWS = 512
SWIGLU_ALPHA = 1.702
SWIGLU_LIMIT = 7.0


def _cparams(sem, **kw):
    return pltpu.CompilerParams(dimension_semantics=sem, vmem_limit_bytes=VMEM_LIMIT, **kw)


def _dot(a, b):
    return jnp.dot(a, b, preferred_element_type=F32)


def _dot_hilo_lhs(g_bf16, x):
    hi, lo = _split2(x)
    return _dot(g_bf16, hi) + _dot(g_bf16, lo)


def _split2(x):
    hi = x.astype(BF16)
    return hi, (x - hi.astype(F32)).astype(BF16)


def _dot_hilo_rhs(x, g_bf16):
    hi, lo = _split2(x)
    out = _dot(jnp.concatenate([hi, lo], axis=0), g_bf16)
    return out[:x.shape[0]] + out[x.shape[0]:]


def _dot_hilo(a, b):
    ah, al = _split2(a)
    bh, bl = _split2(b)
    return _dot(ah, bh) + (_dot(ah, bl) + _dot(al, bh))


def _store_rows(ref, x2):
    rows, d = x2.shape
    c = d // V7X_LANES
    for j in range(c):
        ref[pl.ds(j, rows, stride=c), :] = x2[:, j * V7X_LANES:(j + 1) * V7X_LANES]


def _load_rows(ref, c):
    rows = ref.shape[0] // c
    return jnp.concatenate([ref[pl.ds(j, rows, stride=c), :] for j in range(c)], axis=1)


def _group_matrix(width, group, value):
    g = np.arange(width) // group
    return jnp.asarray(np.where(g[:, None] == g[None, :], value, 0.0), BF16)


def _inproj_kernel(x_ref, g_ref, w_ref, gq_ref, gk_ref, gm_ref, zrw_ref, q_ref, k_ref, vt_ref, *, rw_cols, c_da, scale):
    x = x_ref[...]
    ms = jnp.mean(x * x, axis=-1, keepdims=True)
    u = (x * lax.rsqrt(ms + NORM_EPS) * g_ref[...]).astype(BF16)
    z = _dot(u, w_ref[...])
    zrw_ref[...] = z[:, :rw_cols]
    zq = z[:, rw_cols:rw_cols + c_da]
    zk = z[:, rw_cols + c_da:rw_cols + 2 * c_da]
    zv = z[:, rw_cols + 2 * c_da:]
    tm = zq.shape[0]
    msqk = _dot(jnp.concatenate([zq * zq, zk * zk], axis=0).astype(BF16), gm_ref[...])
    q_ref[...] = (zq * lax.rsqrt(msqk[:tm] + NORM_EPS) * (gq_ref[...] * scale)).astype(BF16)
    k_ref[...] = (zk * lax.rsqrt(msqk[tm:] + NORM_EPS) * gk_ref[...]).astype(BF16)
    heads, vrows, _ = vt_ref.shape
    v_dim = c_da // heads
    pad_row = lax.broadcasted_iota(jnp.int32, (vrows - v_dim, tm), 0)
    pad = jnp.where(pad_row == 0, 1.0, 0.0).astype(BF16)
    for hd in range(heads):
        vt_ref[hd, 0:v_dim, :] = zv[:, hd * v_dim:(hd + 1) * v_dim].T.astype(BF16)
        vt_ref[hd, v_dim:, :] = pad


def _inproj(x2, g, w_bf, gq_t, gk_t, gm, rw_cols, c_da, scale, tm, b, heads, tb):
    n, d = x2.shape
    cols = w_bf.shape[1]
    t = n // b
    tps = t // tm
    per = tb // tm
    vrows = c_da // heads + ATTN_V_PAD
    full = lambda i: (0, 0)
    row = lambda i: (i, 0)
    return pl.pallas_call(
        functools.partial(_inproj_kernel, rw_cols=rw_cols, c_da=c_da, scale=scale),
        grid=(n // tm,),
        in_specs=[
            pl.BlockSpec((tm, d), row),
            pl.BlockSpec((1, d), full),
            pl.BlockSpec((d, cols), full),
            pl.BlockSpec((1, c_da), full),
            pl.BlockSpec((1, c_da), full),
            pl.BlockSpec((c_da, c_da), full),
        ],
        out_specs=[
            pl.BlockSpec((tm, rw_cols), row),
            pl.BlockSpec((tm, c_da), row),
            pl.BlockSpec((tm, c_da), row),
            pl.BlockSpec((None, heads, None, vrows, tm),
                         lambda i: (i // tps, 0, (i % tps) // per, 0, (i % tps) % per)),
        ],
        out_shape=[
            jax.ShapeDtypeStruct((n, rw_cols), F32),
            jax.ShapeDtypeStruct((n, c_da), BF16),
            jax.ShapeDtypeStruct((n, c_da), BF16),
            jax.ShapeDtypeStruct((b, heads, t // tb, vrows, tb), BF16),
        ],
        compiler_params=_cparams(("parallel",)),
        name="inproj",
    )(x2, g, w_bf, gq_t, gk_t, gm)


def _rwkv_prep_kernel(z_ref, zp_ref, zn_ref, mu_ref, w2_ref, w0_ref, a2_ref, a0_ref, g2_ref, kk_ref, ka_ref,
                      rk_ref, gsum_ref,
                      r_ref, v_ref, nkk_ref, lw_ref, kd_ref, bd_ref, g_ref, bonus_ref, *, c_rw, lora):
    i = pl.program_id(1)
    nt = pl.num_programs(1)
    z = z_ref[...]
    tt = z.shape[0]
    rows = lax.broadcasted_iota(jnp.int32, z.shape, 0)
    prev_row = jnp.where(i == 0, 0.0, zp_ref[7:8, :])
    next_row = jnp.where(i == nt - 1, 0.0, zn_ref[0:1, :])
    zprev = jnp.where(rows == 0, prev_row, pltpu.roll(z, 1, 0))
    znext = jnp.where(rows == tt - 1, next_row, pltpu.roll(z, tt - 1, 0))
    z = z + (0.5 * (zprev + znext) - z) * mu_ref[...]

    r = z[:, 0:c_rw]
    k = z[:, c_rw:2 * c_rw]
    v = z[:, 2 * c_rw:3 * c_rw]
    o = 3 * c_rw
    wd = z[:, o:o + 2 * lora]
    ad = z[:, o + 2 * lora:o + 4 * lora]
    gd = z[:, o + 4 * lora:]

    xw = w0_ref[...] + _dot(jnp.tanh(wd).astype(BF16), w2_ref[...])
    xa = a0_ref[...] + _dot(ad.astype(BF16), a2_ref[...])
    lw = -math.exp(-0.5) * jax.nn.sigmoid(xw)
    rate = jax.nn.sigmoid(xa)
    g = _dot(jax.nn.sigmoid(gd).astype(BF16), g2_ref[...])

    gsum = gsum_ref[...]
    kkr = k * kk_ref[...]
    ka = ka_ref[...]
    kd0 = k * (1.0 + (rate[:, :c_rw] - 1.0) * ka)
    kd1 = k * (1.0 + (rate[:, c_rw:] - 1.0) * ka)
    sums = _dot_hilo_rhs(jnp.concatenate([kkr * kkr, r * (0.5 * (kd0 + kd1)) * rk_ref[...]], axis=0), gsum)
    kk = kkr * lax.rsqrt(sums[:tt] + 1e-12)
    bonus = sums[tt:] * v

    r_ref[...] = r.astype(BF16)
    v_ref[...] = v.astype(BF16)
    nkk_ref[...] = (-kk).astype(BF16)
    lw_ref[0] = lw[:, :c_rw]
    lw_ref[1] = lw[:, c_rw:]
    kd_ref[0] = kd0.astype(BF16)
    kd_ref[1] = kd1.astype(BF16)
    bd_ref[0] = (kk * rate[:, :c_rw]).astype(BF16)
    bd_ref[1] = (kk * rate[:, c_rw:]).astype(BF16)
    g_ref[...] = g.astype(BF16)
    bonus_ref[...] = bonus.astype(BF16)


def _rwkv_prep(zrw3, mu, w2bd, w0f, a2bd, a0f, g2_bf, k_k, k_a, r_kf, gsum, c_rw, lora, tt):
    b, t, cols = zrw3.shape
    nt = t // tt
    hb = tt // 8
    nhb = t // 8
    full = lambda bi, i: (0, 0)
    tile = lambda bi, i: (bi, i, 0)
    dtile = lambda bi, i: (0, bi, i, 0)
    one = jax.ShapeDtypeStruct((b, t, c_rw), F32)
    two = jax.ShapeDtypeStruct((2, b, t, c_rw), F32)
    one16 = jax.ShapeDtypeStruct((b, t, c_rw), BF16)
    two16 = jax.ShapeDtypeStruct((2, b, t, c_rw), BF16)
    return pl.pallas_call(
        functools.partial(_rwkv_prep_kernel, c_rw=c_rw, lora=lora),
        grid=(b, nt),
        in_specs=[
            pl.BlockSpec((None, tt, cols), tile),
            pl.BlockSpec((None, 8, cols), lambda bi, i: (bi, jnp.maximum(i * hb - 1, 0), 0)),
            pl.BlockSpec((None, 8, cols), lambda bi, i: (bi, jnp.minimum((i + 1) * hb, nhb - 1), 0)),
            pl.BlockSpec((1, cols), full),
            pl.BlockSpec(w2bd.shape, full),
            pl.BlockSpec((1, 2 * c_rw), full),
            pl.BlockSpec(a2bd.shape, full),
            pl.BlockSpec((1, 2 * c_rw), full),
            pl.BlockSpec(g2_bf.shape, full),
            pl.BlockSpec((1, c_rw), full),
            pl.BlockSpec((1, c_rw), full),
            pl.BlockSpec((1, c_rw), full),
            pl.BlockSpec((c_rw, c_rw), full),
        ],
        out_specs=[
            pl.BlockSpec((None, tt, c_rw), tile),
            pl.BlockSpec((None, tt, c_rw), tile),
            pl.BlockSpec((None, tt, c_rw), tile),
            pl.BlockSpec((2, None, tt, c_rw), dtile),
            pl.BlockSpec((2, None, tt, c_rw), dtile),
            pl.BlockSpec((2, None, tt, c_rw), dtile),
            pl.BlockSpec((None, tt, c_rw), tile),
            pl.BlockSpec((None, tt, c_rw), tile),
        ],
        out_shape=[one16, one16, one16, two, two16, two16, one16, one16],
        compiler_params=_cparams(("parallel", "parallel")),
        name="rwkv_prep",
    )(zrw3, zrw3, zrw3, mu, w2bd, w0f, a2bd, a0f, g2_bf, k_k, k_a, r_kf, gsum)


M_STRICT, M_INCL, M_EYE, M_PAIR, M_LEVEL = 0, 1, 2, 3, 4
N_LEVELS = 5
SCAN_SEQS = 2


def _scan_tables(w):
    c = CHUNK
    masks = []
    mcums = []
    for d in range(2):
        t = np.broadcast_to(np.arange(c)[:, None], (c, w))
        s = np.broadcast_to((np.arange(w) % RW_HEAD)[None, :], (c, w))
        tt = np.broadcast_to(np.arange(c)[:, None], (c, c))
        ss = np.broadcast_to(np.arange(c)[None, :], (c, c))
        if d == 1:
            t, s, tt, ss = c - 1 - t, c - 1 - s, c - 1 - tt, c - 1 - ss
        rows = [s < t, s <= t, s == t, (t // 2 == s // 2) & (t > s)]
        sz = 2
        while sz < c:
            rows.append((t // (2 * sz) == s // (2 * sz)) & ((t // sz) % 2 == 1) & ((s // sz) % 2 == 0))
            sz *= 2
        assert len(rows) == M_LEVEL + N_LEVELS
        masks.append(np.stack(rows).astype(np.float32))
        mcums.append((ss <= tt).astype(np.float32))
    return jnp.asarray(np.stack(masks), F32), jnp.asarray(np.stack(mcums), BF16)


def _block_diag(y, bd16):
    g = y.shape[1] // RW_HEAD
    return jnp.concatenate([y.astype(BF16)] * g, axis=0) * bd16


def _bmm(x, y, bd16):
    outs = []
    for s in range(0, y.shape[1], HEAD_GROUP):
        outs.append(_dot(x[:, s:s + HEAD_GROUP].astype(BF16), _block_diag(y[:, s:s + HEAD_GROUP], bd16)))
    return jnp.concatenate(outs, axis=1)


def _bmm_nt(x, y, bd16):
    outs = []
    for s in range(0, y.shape[1], HEAD_GROUP):
        bd = _block_diag(y[:, s:s + HEAD_GROUP], bd16)
        outs.append(lax.dot_general(x[:, s:s + HEAD_GROUP].astype(BF16), bd, (((1,), (1,)), ((), ())),
                                    preferred_element_type=F32))
    return jnp.concatenate(outs, axis=1)


def _bmm_tn(x, y, bd32):
    outs = []
    for s in range(0, y.shape[1], HEAD_GROUP):
        xt = x[:, s:s + HEAD_GROUP].T.astype(BF16)
        full = _dot(xt, y[:, s:s + HEAD_GROUP].astype(BF16)) * bd32
        o = full[0:RW_HEAD]
        for j in range(1, HEAD_GROUP // RW_HEAD):
            o = o + full[j * RW_HEAD:(j + 1) * RW_HEAD]
        outs.append(o)
    return jnp.concatenate(outs, axis=1)


def _scan_chunks(r, lw, k, v, a, b, state, masks, mcum, bd16, bd32):
    c = r[0].shape[0]

    def each(f, *cols):
        return [f(*xs) for xs in zip(*cols)]

    cum = each(_dot_hilo_lhs, mcum, lw)
    tot = each(lambda x: jnp.sum(x, axis=0, keepdims=True), lw)
    e_neg = each(lambda x: jnp.exp(-x), cum)
    e_tail = each(lambda t_, x: jnp.exp(t_ - x), tot, cum)
    rt = each(lambda x, cm: x * jnp.exp(cm), r, cum)
    at = each(lambda x, cm, l_: x * jnp.exp(cm - l_), a, cum, lw)
    bt = each(jnp.multiply, b, e_neg)
    kt = each(jnp.multiply, k, e_neg)
    bh = each(jnp.multiply, b, e_tail)
    kh = each(jnp.multiply, k, e_tail)

    ar = each(lambda x, y_: jnp.concatenate([x, y_], axis=0), at, rt)
    pb = each(lambda x, y_: _bmm_nt(x, y_, bd16), ar, bt)
    pk = each(lambda x, y_: _bmm_nt(x, y_, bd16), ar, kt)
    a_ab = each(lambda x, m: x[:c] * m[M_STRICT], pb, masks)
    a_rb = each(lambda x, m: x[c:] * m[M_INCL], pb, masks)
    a_ak = each(lambda x, m: x[:c] * m[M_STRICT], pk, masks)
    a_rk = each(lambda x, m: x[c:] * m[M_INCL], pk, masks)

    tinv = each(lambda x, m: m[M_EYE] + x * m[M_PAIR], a_ab, masks)
    for lvl in range(N_LEVELS):
        x = each(lambda aa, m, ti: _bmm(aa * m[M_LEVEL + lvl], ti, bd16), a_ab, masks, tinv)
        tinv = each(lambda ti, x_: ti + _bmm(ti, x_, bd16), tinv, x)

    bmm = lambda p, q: _bmm(p, q, bd16)
    stack = lambda p, q: jnp.concatenate([p, q], axis=0)
    att = each(bmm, tinv, at)
    akv = each(lambda p, p2, q: _bmm(stack(p, p2), q, bd16), a_ak, a_rk, v)
    vt = each(lambda ti, x_: _bmm(ti, x_[:c], bd16), tinv, akv)
    rh = each(lambda x, p, q: x + _bmm(p, q, bd16), rt, a_rb, att)
    y_in = each(lambda p, q, x_: _bmm(p, q, bd16) + x_[c:], a_rb, vt, akv)
    mc = each(lambda m, t_, p, q: m[M_EYE] * jnp.exp(t_) + _bmm_tn(p, q, bd32), masks, tot, bh, att)
    nc = each(lambda p, p2, q, q2: _bmm_tn(stack(p, p2), stack(q, q2), bd32), bh, kh, vt, v)
    onst = each(lambda p, p2, q: _bmm(stack(p, p2), q, bd16), rh, mc, state)
    y = each(lambda x_, yi: x_[:c] + yi, onst, y_in)
    new_state = each(lambda x_, n_: x_[c:] + n_, onst, nc)
    return y, new_state


def _rwkv_scan_kernel(rf_ref, vf_ref, af_ref, rb_ref, vb_ref, ab_ref, lwf_ref, kf_ref, bf_ref, lwb_ref, kb_ref,
                      bb_ref, mask_ref, mcum_ref, bd16_ref, bd32_ref, yf_ref, yb_ref, state_ref, *, n_chunks):
    @pl.when(pl.program_id(1) == 0)
    def _():
        state_ref[...] = jnp.zeros_like(state_ref)

    dirs = ((0, rf_ref, lwf_ref, kf_ref, vf_ref, af_ref, bf_ref, yf_ref),
            (1, rb_ref, lwb_ref, kb_ref, vb_ref, ab_ref, bb_ref, yb_ref))
    n_seq = rf_ref.shape[0]

    def body(j, carry):
        cols = [[] for _ in range(9)]
        where = []
        for d, r_ref, lw_ref, k_ref, v_ref, a_ref, b_ref, y_ref in dirs:
            cj = j if d == 0 else n_chunks - 1 - j
            sl = pl.ds(pl.multiple_of(cj * CHUNK, CHUNK), CHUNK)
            for g in range(n_seq):
                vals = (r_ref[g, sl, :], lw_ref[g, sl, :], k_ref[g, sl, :], v_ref[g, sl, :], a_ref[g, sl, :],
                        b_ref[g, sl, :], state_ref[d, g], mask_ref.at[d], mcum_ref[d])
                for col, val in zip(cols, vals):
                    col.append(val)
                where.append((y_ref, d, g, sl))
        ys, sts = _scan_chunks(*cols, bd16_ref[...], bd32_ref[...])
        for (y_ref, d, g, sl), y, st in zip(where, ys, sts):
            y_ref[g, sl, :] = y.astype(y_ref.dtype)
            state_ref[d, g] = st
        return carry

    lax.fori_loop(0, n_chunks, body, 0)


def _rwkv_scan(r, v, nkk, lw, kd, bd, tb):
    b, t, w = r.shape
    nb = t // tb
    g = SCAN_SEQS if b % SCAN_SEQS == 0 else 1
    masks, mcum = _scan_tables(w)
    bd32 = _group_matrix(HEAD_GROUP, RW_HEAD, 1.0).astype(F32)
    bd16 = bd32.astype(BF16)
    fwd = lambda bi, i: (bi, i, 0)
    bwd = lambda bi, i: (bi, nb - 1 - i, 0)
    fwd_d = lambda bi, i: (0, bi, i, 0)
    bwd_d = lambda bi, i: (1, bi, nb - 1 - i, 0)
    const = lambda nd: (lambda bi, i: (0,) * nd)
    blk = (g, tb, w)
    dblk = (None, g, tb, w)
    out = jax.ShapeDtypeStruct((b, t, w), BF16)
    return pl.pallas_call(
        functools.partial(_rwkv_scan_kernel, n_chunks=tb // CHUNK),
        grid=(b // g, nb),
        in_specs=[
            pl.BlockSpec(blk, fwd), pl.BlockSpec(blk, fwd), pl.BlockSpec(blk, fwd),
            pl.BlockSpec(blk, bwd), pl.BlockSpec(blk, bwd), pl.BlockSpec(blk, bwd),
            pl.BlockSpec(dblk, fwd_d), pl.BlockSpec(dblk, fwd_d), pl.BlockSpec(dblk, fwd_d),
            pl.BlockSpec(dblk, bwd_d), pl.BlockSpec(dblk, bwd_d), pl.BlockSpec(dblk, bwd_d),
            pl.BlockSpec(masks.shape, const(4)),
            pl.BlockSpec(mcum.shape, const(3)),
            pl.BlockSpec(bd16.shape, const(2)),
            pl.BlockSpec(bd32.shape, const(2)),
        ],
        out_specs=[pl.BlockSpec(blk, fwd), pl.BlockSpec(blk, bwd)],
        out_shape=[out, out],
        scratch_shapes=[pltpu.VMEM((2, g, CHUNK, w), F32)],
        compiler_params=_cparams(("parallel", "arbitrary")),
        name="rwkv_scan",
    )(r, v, nkk, r, v, nkk, lw, kd, bd, lw, kd, bd, masks, mcum, bd16, bd32)


def _attn_kernel(slope_ref, q_ref, k_ref, v_ref, lq1_ref, lk1_ref, lq2_ref, lk2_ref, sg_ref, o_ref,
                 m_ref, l_ref, acc_ref, *, qk_dim, lambda_init):
    h = pl.program_id(1)
    i = pl.program_id(2)
    j = pl.program_id(3)
    nk = pl.num_programs(3)
    tq = q_ref.shape[0]
    tk = k_ref.shape[0]

    @pl.when(j == 0)
    def _():
        m_ref[...] = jnp.full(m_ref.shape, -jnp.inf, F32)
        l_ref[...] = jnp.zeros_like(l_ref)
        acc_ref[...] = jnp.zeros_like(acc_ref)

    q = q_ref[...]
    k = k_ref[...]
    v = v_ref[...]
    lane = lax.broadcasted_iota(jnp.int32, q.shape, 1)
    zero = jnp.zeros_like(q)
    qi = lax.broadcasted_iota(jnp.int32, (tq, tk), 0) + i * tq
    kj = lax.broadcasted_iota(jnp.int32, (tq, tk), 1) + j * tk
    bias = slope_ref[h] * jnp.abs(qi - kj).astype(F32)
    for c in range(2):
        qc = jnp.where((lane < qk_dim) == (c == 0), q, zero)
        s = lax.dot_general(qc, k, (((1,), (1,)), ((), ())), preferred_element_type=F32) - bias
        m_old = m_ref[c]
        m_new = jnp.maximum(m_old, jnp.max(s, axis=-1, keepdims=True))
        alpha = jnp.exp(m_old - m_new)
        p = jnp.exp(s - m_new)
        l_ref[c] = alpha * l_ref[c] + jnp.sum(p, axis=-1, keepdims=True)
        acc_ref[c] = alpha * acc_ref[c] + _dot(p.astype(BF16), v)
        m_ref[c] = m_new

    @pl.when(j == nk - 1)
    def _():
        lam = (jnp.exp(jnp.sum(lq1_ref[...] * lk1_ref[...], axis=-1, keepdims=True))
               - jnp.exp(jnp.sum(lq2_ref[...] * lk2_ref[...], axis=-1, keepdims=True)) + lambda_init)
        o = acc_ref[0] / l_ref[0] - lam * (acc_ref[1] / l_ref[1])
        ms = jnp.mean(o * o, axis=-1, keepdims=True)
        o_ref[...] = o * lax.rsqrt(ms + NORM_EPS) * (sg_ref[...] * (1.0 - lambda_init))


def _attention(slopes, q3, k3, v3, lq1, lk1, lq2, lk2, subln_g, heads, v_dim, lambda_init, tq, tk):
    b, t, _ = q3.shape
    const = lambda bi, h, i, j, s: (0, 0)
    grid_spec = pltpu.PrefetchScalarGridSpec(
        num_scalar_prefetch=1,
        grid=(b, heads, t // tq, t // tk),
        in_specs=[
            pl.BlockSpec((None, tq, v_dim), lambda bi, h, i, j, s: (bi, i, h)),
            pl.BlockSpec((None, tk, v_dim), lambda bi, h, i, j, s: (bi, j, h)),
            pl.BlockSpec((None, tk, v_dim), lambda bi, h, i, j, s: (bi, j, h)),
            pl.BlockSpec(lq1.shape, const),
            pl.BlockSpec(lk1.shape, const),
            pl.BlockSpec(lq2.shape, const),
            pl.BlockSpec(lk2.shape, const),
            pl.BlockSpec(subln_g.shape, const),
        ],
        out_specs=pl.BlockSpec((None, tq, v_dim), lambda bi, h, i, j, s: (bi, i, h)),
        scratch_shapes=[
            pltpu.VMEM((2, tq, 1), F32),
            pltpu.VMEM((2, tq, 1), F32),
            pltpu.VMEM((2, tq, v_dim), F32),
        ],
    )
    return pl.pallas_call(
        functools.partial(_attn_kernel, qk_dim=v_dim // 2, lambda_init=lambda_init),
        grid_spec=grid_spec,
        out_shape=jax.ShapeDtypeStruct((b, t, heads * v_dim), F32),
        compiler_params=_cparams(("parallel", "parallel", "parallel", "arbitrary")),
        name="attn",
    )(slopes, q3, k3, v3, lq1, lk1, lq2, lk2, subln_g)


EXP_ZERO_ARG = 104.0
MAX_FIXED_SHIFT = 40.0
N_AUG = 6
ATTN_BLOCK = 512
ATTN_GROUP_SIZES = (8, 4, 2, 1)
ATTN_KEY_SPLIT = 1
ATTN_V_PAD = 16


def _attn_fixed_kernel(w_ref, slope_ref, mb_ref, q_ref, k_ref, vt_ref, dbias_ref, qaug_ref, kaug_ref,
                       lq1_ref, lk1_ref, lq2_ref, lk2_ref, sg_ref, o_ref, acc_ref, *, qk_dim, lambda_init):
    h = pl.program_id(1)
    i = pl.program_id(2)
    tb = q_ref.shape[0]
    nk = k_ref.shape[0] // tb
    v_dim = 2 * qk_dim
    slope = slope_ref[h]
    mb = mb_ref[0]
    w = w_ref[h]

    q = q_ref[...]
    lane = lax.broadcasted_iota(jnp.int32, q.shape, 1)
    lane_row = lax.broadcasted_iota(jnp.int32, (1, q.shape[1]), 1)
    is_c0 = lane < qk_dim
    datas = (is_c0, jnp.logical_not(is_c0))

    def blocks(items, qvar, kvar, diag):
        sub = tb // ATTN_KEY_SPLIT
        qas = [jnp.where(datas[c], q, qaug_ref[qvar, c]) for c in range(2)]
        chains = []
        for n, (j, delta) in enumerate(items):
            kblk = k_ref[pl.ds(pl.multiple_of(j * tb, tb), tb), :]
            vtblk = vt_ref[j]
            for c in range(2):
                base = qk_dim * (1 - c)
                dyn = jnp.where(lane_row == base + 4, -mb,
                                jnp.where(lane_row == base + 5, -(slope * tb) * delta, 0.0)).astype(BF16)
                ka = jnp.where(datas[c], kblk, kaug_ref[kvar, c] + dyn)
                for u in range(ATTN_KEY_SPLIT):
                    rows = slice(u * sub, (u + 1) * sub)
                    chains.append((c, u, qas[c], ka[rows], vtblk[:, rows]))
        sts = [lax.dot_general(ka, qa, (((1,), (1,)), ((), ())), preferred_element_type=F32)
               for _, _, qa, ka, _ in chains]
        if diag:
            sts = [st - dbias_ref[u * sub:(u + 1) * sub, :] for st, (_, u, _, _, _) in zip(sts, chains)]
        pts = [jnp.exp(st).astype(BF16) for st in sts]
        for c in range(2):
            upd = None
            for pt, (cc, _, _, _, vsub) in zip(pts, chains):
                if cc == c:
                    part = _dot(vsub, pt)
                    upd = part if upd is None else upd + part
            if diag:
                acc_ref[c] = upd
            else:
                acc_ref[c] += upd

    def span(first, count, qvar, kvar, dist):
        for per in ATTN_GROUP_SIZES:
            def group(g, carry, first=first, per=per):
                j0 = first + g * per
                blocks([(j0 + n, dist(j0 + n)) for n in range(per)], qvar, kvar, False)
                return carry

            groups = count // per
            lax.fori_loop(0, groups, group, 0)
            first = first + groups * per
            count = count - groups * per

    lo = jnp.maximum(i - w, 0)
    hi = jnp.minimum(i + w, nk - 1)
    blocks([(i, jnp.zeros((), F32))], 2, 0, True)
    span(lo, i - lo, 0, 0, lambda j: (i - j).astype(F32))
    span(i + 1, hi - i, 1, 1, lambda j: (j - i).astype(F32))

    lam = (jnp.exp(jnp.sum(lq1_ref[...] * lk1_ref[...], axis=-1, keepdims=True))
           - jnp.exp(jnp.sum(lq2_ref[...] * lk2_ref[...], axis=-1, keepdims=True)) + lambda_init)
    a0 = acc_ref[0]
    a1 = acc_ref[1]
    inv0 = 1.0 / a0[v_dim:v_dim + 1]
    inv1 = lam / a1[v_dim:v_dim + 1]
    ot = a0[:v_dim] * inv0 - a1[:v_dim] * inv1
    ms = jnp.mean(ot * ot, axis=0, keepdims=True)
    o_ref[...] = (ot * lax.rsqrt(ms + NORM_EPS) * (sg_ref[...] * (1.0 - lambda_init))).T.astype(o_ref.dtype)


def _attn_aug_tables(slopes, tb, qk_dim):
    heads = len(slopes)
    lanes = 2 * qk_dim
    pos = np.arange(tb)
    qaug = np.zeros((heads, 3, 2, tb, lanes), np.float32)
    kaug = np.zeros((heads, 2, 2, tb, lanes), np.float32)
    for hd, sl in enumerate(slopes):
        for var in range(3):
            rr = pos if var != 1 else tb - 1 - pos
            hi = (rr // 16) * (16.0 * sl)
            lo = (rr % 16) * sl
            for c in range(2):
                base = qk_dim * (1 - c)
                if var == 2:
                    qaug[hd, var, c, :, base + 4] = 1.0
                    continue
                qaug[hd, var, c, :, base + 0] = -hi
                qaug[hd, var, c, :, base + 1] = -lo
                qaug[hd, var, c, :, base + 2:base + N_AUG] = 1.0
                kaug[hd, var, c, :, base + 0:base + 2] = 1.0
                kaug[hd, var, c, :, base + 2] = hi
                kaug[hd, var, c, :, base + 3] = lo
    return jnp.asarray(qaug, BF16), jnp.asarray(kaug, BF16)


def _attention_fixed(mb, q3, k3, vt, lq1, lk1, lq2, lk2, subln_g, heads, v_dim, lambda_init, tb):
    b, t, _ = q3.shape
    qk_dim = v_dim // 2
    slopes = [2.0 ** (-8.0 * (hd + 1) / heads) for hd in range(heads)]
    widths = [int(math.ceil((EXP_ZERO_ARG / s - 1.0) / tb)) for s in slopes]
    qaug, kaug = _attn_aug_tables(slopes, tb, qk_dim)
    pos = np.arange(tb)
    dist = np.abs(pos[:, None] - pos[None, :]).astype(np.float32)
    dbias = jnp.asarray(np.stack([dist * s for s in slopes]), F32)
    nb = t // tb
    vrows = v_dim + ATTN_V_PAD
    assert vt.shape == (b, heads, nb, vrows, tb)
    const = lambda bi, h, i, *_: (0, 0)
    grid_spec = pltpu.PrefetchScalarGridSpec(
        num_scalar_prefetch=3,
        grid=(b, heads, t // tb),
        in_specs=[
            pl.BlockSpec((None, tb, v_dim), lambda bi, h, i, *_: (bi, i, h)),
            pl.BlockSpec((None, t, v_dim), lambda bi, h, i, *_: (bi, 0, h)),
            pl.BlockSpec((None, None, nb, vrows, tb), lambda bi, h, i, *_: (bi, h, 0, 0, 0)),
            pl.BlockSpec((None, tb, tb), lambda bi, h, i, *_: (h, 0, 0)),
            pl.BlockSpec((None, 3, 2, tb, v_dim), lambda bi, h, i, *_: (h, 0, 0, 0, 0)),
            pl.BlockSpec((None, 2, 2, tb, v_dim), lambda bi, h, i, *_: (h, 0, 0, 0, 0)),
            pl.BlockSpec(lq1.shape, const),
            pl.BlockSpec(lk1.shape, const),
            pl.BlockSpec(lq2.shape, const),
            pl.BlockSpec(lk2.shape, const),
            pl.BlockSpec((v_dim, 1), const),
        ],
        out_specs=pl.BlockSpec((None, tb, v_dim), lambda bi, h, i, *_: (bi, i, h)),
        scratch_shapes=[pltpu.VMEM((2, vrows, tb), F32)],
    )
    return pl.pallas_call(
        functools.partial(_attn_fixed_kernel, qk_dim=qk_dim, lambda_init=lambda_init),
        grid_spec=grid_spec,
        out_shape=jax.ShapeDtypeStruct((b, t, heads * v_dim), BF16),
        compiler_params=_cparams(("parallel", "parallel", "parallel")),
        name="attn_fixed",
    )(jnp.asarray(widths, jnp.int32), jnp.asarray(slopes, F32), mb.reshape(1).astype(F32),
      q3, k3, vt, dbias, qaug, kaug, lq1, lk1, lq2, lk2, subln_g.reshape(v_dim, 1))


def _post_kernel(yf_ref, yb_ref, bonus_ref, g_ref, oda_ref, x_ref, lng_ref, lnb_ref, gmean_ref, wo_rw_ref, wo_da_ref,
                 gffn_ref, wr_ref, br_ref, h_ref, m_ref, idx_ref, gate_ref, cnt_ref, *, gn_eps):
    @pl.when(pl.program_id(0) == 0)
    def _():
        cnt_ref[...] = jnp.zeros_like(cnt_ref)

    gmean = gmean_ref[...]
    y = yf_ref[...].astype(F32) + yb_ref[...].astype(F32)
    mean = _dot_hilo_rhs(y, gmean)
    yc = y - mean
    var = _dot_hilo_rhs(yc * yc, gmean)
    yn = yc * lax.rsqrt(var + gn_eps) * lng_ref[...] + lnb_ref[...]
    o_rw = (yn + bonus_ref[...]) * g_ref[...]
    h = x_ref[...] + _dot(o_rw.astype(BF16), wo_rw_ref[...]) + _dot(oda_ref[...].astype(BF16), wo_da_ref[...])
    h_ref[...] = h
    ms = jnp.mean(h * h, axis=-1, keepdims=True)
    m = h * lax.rsqrt(ms + NORM_EPS) * gffn_ref[...]
    _store_rows(m_ref, m)

    logits = _dot_hilo(m, wr_ref[...]) + br_ref[...]
    ne = logits.shape[1]
    lane = lax.broadcasted_iota(jnp.int32, logits.shape, 1).astype(F32)
    kcol = lax.broadcasted_iota(jnp.int32, (logits.shape[0], TOP_K), 1)
    idx_all = jnp.zeros((logits.shape[0], TOP_K), F32)
    val_all = jnp.zeros((logits.shape[0], TOP_K), F32)
    work = logits
    chosen = jnp.zeros(logits.shape, F32)
    for kk in range(TOP_K):
        mx = jnp.max(work, axis=-1, keepdims=True)
        sel = jnp.min(jnp.where(work == mx, lane, float(ne)), axis=-1, keepdims=True)
        idx_all = jnp.where(kcol == kk, sel, idx_all)
        val_all = jnp.where(kcol == kk, mx, val_all)
        hit = lane == sel
        chosen = jnp.where(hit, 1.0, chosen)
        work = jnp.where(hit, -jnp.inf, work)
    e = jnp.exp(val_all - jnp.max(val_all, axis=-1, keepdims=True))
    idx_ref[...] = idx_all.astype(jnp.int32)
    gate_ref[...] = e / jnp.sum(e, axis=-1, keepdims=True)
    cnt_ref[...] += jnp.sum(chosen, axis=0, keepdims=True)


def _post(yf, yb, bonus, g, oda, x2, lng, lnb, gmean, wo_rw, wo_da, gffn, wr, br, gn_eps, tm):
    n, c = yf.shape
    d = x2.shape[1]
    ne = wr.shape[1]
    full = lambda i: (0, 0)
    row = lambda i: (i, 0)
    return pl.pallas_call(
        functools.partial(_post_kernel, gn_eps=gn_eps),
        grid=(n // tm,),
        in_specs=[
            pl.BlockSpec((tm, c), row),
            pl.BlockSpec((tm, c), row),
            pl.BlockSpec((tm, c), row),
            pl.BlockSpec((tm, c), row),
            pl.BlockSpec((tm, c), row),
            pl.BlockSpec((tm, d), row),
            pl.BlockSpec((1, c), full),
            pl.BlockSpec((1, c), full),
            pl.BlockSpec((c, c), full),
            pl.BlockSpec((c, d), full),
            pl.BlockSpec((c, d), full),
            pl.BlockSpec((1, d), full),
            pl.BlockSpec((d, ne), full),
            pl.BlockSpec((1, ne), full),
        ],
        out_specs=[
            pl.BlockSpec((tm, d), row),
            pl.BlockSpec((tm * d // V7X_LANES, V7X_LANES), row),
            pl.BlockSpec((tm, TOP_K), row),
            pl.BlockSpec((tm, TOP_K), row),
            pl.BlockSpec((1, ne), full),
        ],
        out_shape=[
            jax.ShapeDtypeStruct((n, d), F32),
            jax.ShapeDtypeStruct((n * d // V7X_LANES, V7X_LANES), F32),
            jax.ShapeDtypeStruct((n, TOP_K), jnp.int32),
            jax.ShapeDtypeStruct((n, TOP_K), F32),
            jax.ShapeDtypeStruct((1, ne), F32),
        ],
        compiler_params=_cparams(("arbitrary",)),
        name="post",
    )(yf, yb, bonus, g, oda, x2, lng, lnb, gmean, wo_rw, wo_da, gffn, wr, br)


def _rank_kernel(idx_ref, ltri_ref, offs_ref, slot_ref, carry_ref):
    @pl.when(pl.program_id(0) == 0)
    def _():
        carry_ref[...] = offs_ref[...]

    idx = idx_ref[...]
    tr = idx.shape[0]
    lane = lax.broadcasted_iota(jnp.int32, (tr, N_EXPERTS), 1)
    hots = [jnp.where(lane == idx[:, kk:kk + 1], 1.0, 0.0) for kk in range(TOP_K)]
    mask = hots[0]
    for kk in range(1, TOP_K):
        mask = mask + hots[kk]
    before = _dot(ltri_ref[...], mask.astype(BF16)) + carry_ref[...]
    kcol = lax.broadcasted_iota(jnp.int32, (tr, TOP_K), 1)
    slot = jnp.zeros((tr, TOP_K), F32)
    for kk in range(TOP_K):
        rk = jnp.sum(hots[kk] * before, axis=-1, keepdims=True)
        slot = jnp.where(kcol == kk, rk, slot)
    slot_ref[...] = slot.astype(jnp.int32)
    carry_ref[...] = carry_ref[...] + jnp.sum(mask, axis=0, keepdims=True)


def _moe_rank(idx, offs_row, tr):
    n = idx.shape[0]
    ltri = jnp.asarray(np.tril(np.ones((tr, tr), np.float32), -1), BF16)
    return pl.pallas_call(
        _rank_kernel,
        grid=(n // tr,),
        in_specs=[pl.BlockSpec((tr, TOP_K), lambda i: (i, 0)), pl.BlockSpec((tr, tr), lambda i: (0, 0)),
                  pl.BlockSpec((1, N_EXPERTS), lambda i: (0, 0))],
        out_specs=pl.BlockSpec((tr, TOP_K), lambda i: (i, 0)),
        out_shape=jax.ShapeDtypeStruct((n, TOP_K), jnp.int32),
        scratch_shapes=[pltpu.VMEM((1, N_EXPERTS), F32)],
        compiler_params=_cparams(("arbitrary",)),
        name="moe_rank",
    )(idx, ltri, offs_row)


ROW_UNROLL = 8


def _row(r, c):
    return pl.ds(pl.multiple_of(r * c, c), c)


def _dispatch_kernel(slot_ref, m_ref, xs_ref, sem, *, c):
    td = m_ref.shape[0] // c

    def issue(g, carry):
        for u in range(ROW_UNROLL):
            t = g * ROW_UNROLL + u
            for kk in range(TOP_K):
                dest = slot_ref[t * TOP_K + kk]
                pltpu.make_async_copy(m_ref.at[_row(t, c), :], xs_ref.at[_row(dest, c), :], sem).start(
                    priority=kk % 2)
        return carry

    lax.fori_loop(0, td // ROW_UNROLL, issue, 0)
    for kk in range(TOP_K):
        pltpu.make_async_copy(m_ref, xs_ref.at[pl.ds(0, td * c), :], sem).wait()


def _dispatch(slot_flat, mrows, c, p_rows, td):
    n = mrows.shape[0] // c
    assert td % ROW_UNROLL == 0
    return pl.pallas_call(
        functools.partial(_dispatch_kernel, c=c),
        grid=(n // td,),
        in_specs=[
            pl.BlockSpec((td * TOP_K,), lambda i: (i,), memory_space=pltpu.SMEM),
            pl.BlockSpec((td * c, V7X_LANES), lambda i: (i, 0)),
        ],
        out_specs=pl.BlockSpec(memory_space=pl.ANY),
        scratch_shapes=[pltpu.SemaphoreType.DMA(())],
        out_shape=jax.ShapeDtypeStruct((p_rows * c, V7X_LANES), F32),
        compiler_params=_cparams(("arbitrary",), has_side_effects=True),
        name="moe_dispatch",
    )(slot_flat, mrows)


def _ffn_kernel(be_ref, nu_ref, x_ref, w1_ref, b1_ref, w2_ref, b2_ref, y_ref, w1s_ref, w2s_ref, *, d_ff):
    i = pl.program_id(0)

    @pl.when((i == 0) | (be_ref[i] != be_ref[jnp.maximum(i - 1, 0)]))
    def _():
        w1s_ref[...] = w1_ref[...].astype(BF16)
        w2s_ref[...] = w2_ref[...].astype(BF16)

    @pl.when(i < nu_ref[0])
    def _():
        x = _load_rows(x_ref, w1_ref.shape[0] // V7X_LANES).astype(BF16)
        hcat = _dot(x, w1s_ref[...]) + b1_ref[...]
        glu = jnp.minimum(hcat[:, :d_ff], SWIGLU_LIMIT)
        lin = jnp.clip(hcat[:, d_ff:], -SWIGLU_LIMIT, SWIGLU_LIMIT)
        act = glu * jax.nn.sigmoid(SWIGLU_ALPHA * glu) * (lin + 1.0)
        _store_rows(y_ref, _dot(act.astype(BF16), w2s_ref[...]) + b2_ref[...])


def _moe_ffn(blk_e, n_used, xrows, w1, b1, w2, b2, bm):
    ne, d, f2 = w1.shape
    d_ff = f2 // 2
    c = d // V7X_LANES
    p_rows = xrows.shape[0] // c
    blk = (bm * c, V7X_LANES)
    grid_spec = pltpu.PrefetchScalarGridSpec(
        num_scalar_prefetch=2,
        grid=(p_rows // bm,),
        in_specs=[
            pl.BlockSpec(blk, lambda i, be, nu: (jnp.minimum(i, nu[0] - 1), 0)),
            pl.BlockSpec((None, d, f2), lambda i, be, nu: (be[i], 0, 0)),
            pl.BlockSpec((None, 1, f2), lambda i, be, nu: (be[i], 0, 0)),
            pl.BlockSpec((None, d_ff, d), lambda i, be, nu: (be[i], 0, 0)),
            pl.BlockSpec((None, 1, d), lambda i, be, nu: (be[i], 0, 0)),
        ],
        out_specs=pl.BlockSpec(blk, lambda i, be, nu: (i, 0)),
        scratch_shapes=[pltpu.VMEM((d, f2), BF16), pltpu.VMEM((d_ff, d), BF16)],
    )
    return pl.pallas_call(
        functools.partial(_ffn_kernel, d_ff=d_ff),
        grid_spec=grid_spec,
        out_shape=jax.ShapeDtypeStruct(xrows.shape, F32),
        compiler_params=_cparams(("arbitrary",)),
        name="moe_ffn",
    )(blk_e, n_used, xrows, w1, b1.reshape(ne, 1, f2), w2, b2.reshape(ne, 1, d))


def _combine_kernel(slot_ref, next_slot_ref, h_ref, gate_ref, p_ref, wp_ref, gp_ref, wg_ref, y_ref, o_ref,
                    buf_ref, h2_ref, sems):
    i = pl.program_id(0)
    tc, d = h_ref.shape
    c = d // V7X_LANES

    def start_gather(slots, half):
        def issue(g, carry):
            for u in range(ROW_UNROLL):
                t = g * ROW_UNROLL + u
                for kk in range(TOP_K):
                    src = slots[t * TOP_K + kk]
                    pltpu.make_async_copy(y_ref.at[_row(src, c), :], buf_ref.at[half, kk, _row(t, c), :],
                                          sems.at[half]).start(priority=kk % 2)
            return carry

        lax.fori_loop(0, tc // ROW_UNROLL, issue, 0)

    cur = i % 2

    @pl.when(i == 0)
    def _():
        start_gather(slot_ref, 0)

    @pl.when(i + 1 < pl.num_programs(0))
    def _():
        start_gather(next_slot_ref, 1 - cur)

    for kk in range(TOP_K):
        pltpu.make_async_copy(y_ref.at[pl.ds(0, tc * c), :], buf_ref.at[cur, kk], sems.at[cur]).wait()

    gate = gate_ref[...]
    for j in range(c):
        cols = slice(j * V7X_LANES, (j + 1) * V7X_LANES)
        acc = h_ref[:, cols]
        for kk in range(TOP_K):
            acc = acc + gate[:, kk:kk + 1] * buf_ref[cur, kk, pl.ds(j, tc, stride=c), :]
        h2_ref[:, cols] = acc

    h2 = h2_ref[...]
    e = _dot(p_ref[...].astype(BF16), wp_ref[...])
    ms = jnp.mean(e * e, axis=-1, keepdims=True)
    e = e * lax.rsqrt(ms + NORM_EPS) * gp_ref[...]
    o_ref[...] = h2 + e * jax.nn.sigmoid(_dot(h2.astype(BF16), wg_ref[...]))


def _combine_ple(slot_flat, h, gates, yrows, p2, wp_bf, gp, wg_bf, tc):
    n, d = h.shape
    pd = p2.shape[1]
    assert tc % ROW_UNROLL == 0
    last = n // tc - 1
    full = lambda i: (0, 0)
    row = lambda i: (i, 0)
    return pl.pallas_call(
        _combine_kernel,
        grid=(n // tc,),
        in_specs=[
            pl.BlockSpec((tc * TOP_K,), lambda i: (i,), memory_space=pltpu.SMEM),
            pl.BlockSpec((tc * TOP_K,), lambda i: (jnp.minimum(i + 1, last),), memory_space=pltpu.SMEM),
            pl.BlockSpec((tc, d), row),
            pl.BlockSpec((tc, TOP_K), row),
            pl.BlockSpec((tc, pd), row),
            pl.BlockSpec((pd, d), full),
            pl.BlockSpec((1, d), full),
            pl.BlockSpec((d, d), full),
            pl.BlockSpec(memory_space=pl.ANY),
        ],
        out_specs=pl.BlockSpec((tc, d), row),
        scratch_shapes=[pltpu.VMEM((2, TOP_K, tc * d // V7X_LANES, V7X_LANES), F32),
                        pltpu.VMEM((tc, d), F32),
                        pltpu.SemaphoreType.DMA((2,))],
        out_shape=jax.ShapeDtypeStruct((n, d), F32),
        compiler_params=_cparams(("arbitrary",)),
        name="moe_combine_ple",
    )(slot_flat, slot_flat, h, gates, p2, wp_bf, gp, wg_bf, yrows)


def _tile(n, want):
    t = min(n, want)
    assert n % t == 0, (n, t)
    return t


def _layer(h3, p3, lambda_init, norm_mix_g, w_in, rw_mu, rw_w0, rw_w2, rw_a0, rw_a2, rw_g2, rw_k_k, rw_k_a, rw_r_k,
           rw_ln_g, rw_ln_b, da_q_norm_g, da_k_norm_g, da_lq1, da_lk1, da_lq2, da_lk2, da_subln_g, w_out,
           norm_ffn_g, w_router, b_router, w1, b1, w2, b2, w_ple, ple_norm_g, w_ple_gate):
    b, t, d = h3.shape
    n = b * t
    c_rw = rw_k_k.shape[0]
    lora = rw_w2.shape[1]
    rw_cols = rw_mu.shape[0]
    rw_heads = rw_r_k.shape[0]
    c_da = w_in.shape[1] - rw_cols
    assert c_da % 3 == 0
    c_da //= 3
    qk_dim = da_q_norm_g.shape[0]
    v_dim = da_subln_g.shape[0]
    da_heads = c_da // v_dim
    assert rw_heads * RW_HEAD == c_rw and c_rw % HEAD_GROUP == 0 and v_dim == V7X_LANES and 2 * qk_dim == v_dim
    assert 2 * lora == V7X_LANES and w_router.shape[1] == N_EXPERTS

    x2 = h3.reshape(n, d)
    row = lambda a: a.reshape(1, -1).astype(F32)

    gq_t = row(jnp.tile(da_q_norm_g, c_da // qk_dim))
    gk_t = row(jnp.tile(da_k_norm_g, c_da // qk_dim))
    gm_qk = _group_matrix(c_da, qk_dim, 1.0 / qk_dim)
    ta = _tile(t, ATTN_BLOCK)
    zrw, q, k, vt = _inproj(x2, row(norm_mix_g), w_in.astype(BF16), gq_t, gk_t, gm_qk, rw_cols, c_da,
                            1.0 / math.sqrt(qk_dim), _tile(ta, ROW_TILE), b, da_heads, ta)

    zeros = jnp.zeros((lora, c_rw), F32)
    w2bd = jnp.concatenate([jnp.concatenate([rw_w2[0], zeros], axis=1),
                            jnp.concatenate([zeros, rw_w2[1]], axis=1)], axis=0).astype(BF16)
    a2bd = jnp.concatenate([jnp.concatenate([rw_a2[0], zeros], axis=1),
                            jnp.concatenate([zeros, rw_a2[1]], axis=1)], axis=0).astype(BF16)
    gsum = _group_matrix(c_rw, RW_HEAD, 1.0)
    r, vv, nkk, lw, kd, bd, g, bonus = _rwkv_prep(
        zrw.reshape(b, t, rw_cols), row(rw_mu), w2bd, row(rw_w0), a2bd, row(rw_a0), rw_g2.astype(BF16),
        row(rw_k_k), row(rw_k_a), row(rw_r_k), gsum, c_rw, lora, _tile(t, ROW_TILE))
    yf, yb = _rwkv_scan(r, vv, nkk, lw, kd, bd, _tile(t, SCAN_TILE))

    slopes = 2.0 ** (-8.0 * jnp.arange(1, da_heads + 1, dtype=F32) / da_heads)
    q3 = q.reshape(b, t, c_da)
    k3 = k.reshape(b, t, c_da)
    small = (row(da_lq1), row(da_lk1), row(da_lq2), row(da_lk2), row(da_subln_g))

    def attention_running_max():
        v3 = vt[:, :, :, :v_dim, :].transpose(0, 2, 4, 1, 3).reshape(b, t, c_da)
        return _attention(slopes, q3, k3, v3, *small, da_heads, v_dim, lambda_init, ta, ta)

    score_bound = 1.01 * math.sqrt(qk_dim) * jnp.max(jnp.abs(da_q_norm_g)) * jnp.max(jnp.abs(da_k_norm_g))
    mb = jnp.ceil(2.0 * score_bound.astype(F32)) * 0.5
    o_da = lax.cond(
        mb <= MAX_FIXED_SHIFT,
        lambda: _attention_fixed(mb, q3, k3, vt, *small, da_heads, v_dim, lambda_init, ta),
        lambda: attention_running_max().astype(BF16))

    gmean = _group_matrix(c_rw, RW_HEAD, 1.0 / RW_HEAD)
    w_out_bf = w_out.astype(BF16)
    h1, m, idx, gates, counts = _post(
        yf.reshape(n, c_rw), yb.reshape(n, c_rw), bonus.reshape(n, c_rw), g.reshape(n, c_rw), o_da.reshape(n, c_da), x2,
        row(rw_ln_g), row(rw_ln_b), gmean, w_out_bf[:c_rw], w_out_bf[c_rw:], row(norm_ffn_g),
        w_router.astype(F32), row(b_router), RW_HEAD * 1e-5, _tile(n, ROW_TILE))

    bm = MOE_BLOCK_ROWS
    counts = counts.reshape(N_EXPERTS).astype(jnp.int32)
    padded = (counts + bm - 1) // bm * bm
    pend = jnp.cumsum(padded)
    offs = (pend - padded).astype(jnp.int32)
    slot = _moe_rank(idx, offs.astype(F32).reshape(1, N_EXPERTS), _tile(n, ROW_TILE))
    n_blk = (n * TOP_K) // bm + N_EXPERTS
    blk_start = jnp.arange(n_blk, dtype=jnp.int32) * bm
    blk_e = jnp.minimum(jnp.sum(blk_start[:, None] >= pend[None, :], axis=1), N_EXPERTS - 1).astype(jnp.int32)
    n_used = (pend[-1] // bm).astype(jnp.int32).reshape(1)
    slot_flat = slot.reshape(n * TOP_K)
    xs = _dispatch(slot_flat, m, d // V7X_LANES, n_blk * bm, _tile(n, DISPATCH_TILE))
    ys = _moe_ffn(blk_e, n_used, xs, w1, b1, w2, b2, bm)
    out = _combine_ple(slot_flat, h1, gates, ys, p3.reshape(n, -1), w_ple.astype(BF16), row(ple_norm_g),
                       w_ple_gate.astype(BF16), _tile(n, COMBINE_TILE))
    return out.reshape(b, t, d)


def kernel(x, p, norm_mix_g, w_in, rw_mu, rw_w0, rw_w2, rw_a0, rw_a2, rw_g2, rw_k_k, rw_k_a, rw_r_k, rw_ln_g,
           rw_ln_b, da_q_norm_g, da_k_norm_g, da_lq1, da_lk1, da_lq2, da_lk2, da_subln_g, w_out, norm_ffn_g,
           w_router, b_router, w1, b1, w2, b2, w_ple, ple_norm_g, w_ple_gate):
    h = x.astype(F32)
    params = (norm_mix_g, w_in, rw_mu, rw_w0, rw_w2, rw_a0, rw_a2, rw_g2, rw_k_k, rw_k_a, rw_r_k, rw_ln_g, rw_ln_b,
              da_q_norm_g, da_k_norm_g, da_lq1, da_lk1, da_lq2, da_lk2, da_subln_g, w_out, norm_ffn_g,
              w_router, b_router, w1, b1, w2, b2, w_ple, ple_norm_g, w_ple_gate)
    for i in range(p.shape[0]):
        lambda_init = 0.8 - 0.6 * math.exp(-0.3 * i)
        h = _layer(h, p[i], lambda_init, *(a[i] for a in params))
    return h.astype(x.dtype)
```

```python
import functools
import math

import jax
import jax.numpy as jnp
import numpy as np
from jax import lax
from jax.experimental import pallas as pl
from jax.experimental.pallas import tpu as pltpu

F32 = jnp.float32
BF16 = jnp.bfloat16

V7X_LANES = 128
V7X_VMEM_BYTES = 64 * 1024 * 1024
VMEM_LIMIT = V7X_VMEM_BYTES - 12 * 1024 * 1024

ROW_TILE = 512
DISPATCH_TILE = 2048
COMBINE_TILE = 256
SCAN_TILE = 512

NORM_EPS = 1e-6
RW_HEAD = 64
CHUNK = 64
HEAD_GROUP = 256
N_EXPERTS = 32
TOP_K = 4
MOE_BLOCK_ROWS = 512
SWIGLU_ALPHA = 1.702
SWIGLU_LIMIT = 7.0


def _cparams(sem, **kw):
    return pltpu.CompilerParams(dimension_semantics=sem, vmem_limit_bytes=VMEM_LIMIT, **kw)


def _dot(a, b):
    return jnp.dot(a, b, preferred_element_type=F32)


def _dot_hilo_lhs(g_bf16, x):
    hi, lo = _split2(x)
    return _dot(g_bf16, hi) + _dot(g_bf16, lo)


def _split2(x):
    hi = x.astype(BF16)
    return hi, (x - hi.astype(F32)).astype(BF16)


def _dot_hilo_rhs(x, g_bf16):
    hi, lo = _split2(x)
    out = _dot(jnp.concatenate([hi, lo], axis=0), g_bf16)
    return out[:x.shape[0]] + out[x.shape[0]:]


def _dot_hilo(a, b):
    ah, al = _split2(a)
    bh, bl = _split2(b)
    return _dot(ah, bh) + (_dot(ah, bl) + _dot(al, bh))


def _store_rows(ref, x2):
    rows, d = x2.shape
    c = d // V7X_LANES
    for j in range(c):
        ref[pl.ds(j, rows, stride=c), :] = x2[:, j * V7X_LANES:(j + 1) * V7X_LANES]


def _load_rows(ref, c):
    rows = ref.shape[0] // c
    return jnp.concatenate([ref[pl.ds(j, rows, stride=c), :] for j in range(c)], axis=1)


def _group_matrix(width, group, value):
    g = np.arange(width) // group
    return jnp.asarray(np.where(g[:, None] == g[None, :], value, 0.0), BF16)


def _inproj_kernel(x_ref, g_ref, w_ref, gq_ref, gk_ref, gm_ref, zrw_ref, q_ref, k_ref, vt_ref, *, rw_cols, c_da, scale):
    x = x_ref[...]
    ms = jnp.mean(x * x, axis=-1, keepdims=True)
    u = (x * lax.rsqrt(ms + NORM_EPS) * g_ref[...]).astype(BF16)
    z = _dot(u, w_ref[...])
    zrw_ref[...] = z[:, :rw_cols]
    zq = z[:, rw_cols:rw_cols + c_da]
    zk = z[:, rw_cols + c_da:rw_cols + 2 * c_da]
    zv = z[:, rw_cols + 2 * c_da:]
    tm = zq.shape[0]
    msqk = _dot(jnp.concatenate([zq * zq, zk * zk], axis=0).astype(BF16), gm_ref[...])
    q_ref[...] = (zq * lax.rsqrt(msqk[:tm] + NORM_EPS) * (gq_ref[...] * scale)).astype(BF16)
    k_ref[...] = (zk * lax.rsqrt(msqk[tm:] + NORM_EPS) * gk_ref[...]).astype(BF16)
    heads, vrows, _ = vt_ref.shape
    v_dim = c_da // heads
    pad_row = lax.broadcasted_iota(jnp.int32, (vrows - v_dim, tm), 0)
    pad = jnp.where(pad_row == 0, 1.0, 0.0).astype(BF16)
    for hd in range(heads):
        vt_ref[hd, 0:v_dim, :] = zv[:, hd * v_dim:(hd + 1) * v_dim].T.astype(BF16)
        vt_ref[hd, v_dim:, :] = pad


def _inproj(x2, g, w_bf, gq_t, gk_t, gm, rw_cols, c_da, scale, tm, b, heads, tb):
    n, d = x2.shape
    cols = w_bf.shape[1]
    t = n // b
    tps = t // tm
    per = tb // tm
    vrows = c_da // heads + ATTN_V_PAD
    full = lambda i: (0, 0)
    row = lambda i: (i, 0)
    return pl.pallas_call(
        functools.partial(_inproj_kernel, rw_cols=rw_cols, c_da=c_da, scale=scale),
        grid=(n // tm,),
        in_specs=[
            pl.BlockSpec((tm, d), row),
            pl.BlockSpec((1, d), full),
            pl.BlockSpec((d, cols), full),
            pl.BlockSpec((1, c_da), full),
            pl.BlockSpec((1, c_da), full),
            pl.BlockSpec((c_da, c_da), full),
        ],
        out_specs=[
            pl.BlockSpec((tm, rw_cols), row),
            pl.BlockSpec((tm, c_da), row),
            pl.BlockSpec((tm, c_da), row),
            pl.BlockSpec((None, heads, None, vrows, tm),
                         lambda i: (i // tps, 0, (i % tps) // per, 0, (i % tps) % per)),
        ],
        out_shape=[
            jax.ShapeDtypeStruct((n, rw_cols), F32),
            jax.ShapeDtypeStruct((n, c_da), BF16),
            jax.ShapeDtypeStruct((n, c_da), BF16),
            jax.ShapeDtypeStruct((b, heads, t // tb, vrows, tb), BF16),
        ],
        compiler_params=_cparams(("parallel",)),
        name="inproj",
    )(x2, g, w_bf, gq_t, gk_t, gm)


def _rwkv_prep_kernel(z_ref, zp_ref, zn_ref, mu_ref, w2_ref, w0_ref, a2_ref, a0_ref, g2_ref, kk_ref, ka_ref,
                      rk_ref, gsum_ref,
                      r_ref, v_ref, nkk_ref, lw_ref, kd_ref, bd_ref, g_ref, bonus_ref, *, c_rw, lora):
    i = pl.program_id(1)
    nt = pl.num_programs(1)
    z = z_ref[...]
    tt = z.shape[0]
    rows = lax.broadcasted_iota(jnp.int32, z.shape, 0)
    prev_row = jnp.where(i == 0, 0.0, zp_ref[7:8, :])
    next_row = jnp.where(i == nt - 1, 0.0, zn_ref[0:1, :])
    zprev = jnp.where(rows == 0, prev_row, pltpu.roll(z, 1, 0))
    znext = jnp.where(rows == tt - 1, next_row, pltpu.roll(z, tt - 1, 0))
    z = z + (0.5 * (zprev + znext) - z) * mu_ref[...]

    r = z[:, 0:c_rw]
    k = z[:, c_rw:2 * c_rw]
    v = z[:, 2 * c_rw:3 * c_rw]
    o = 3 * c_rw
    wd = z[:, o:o + 2 * lora]
    ad = z[:, o + 2 * lora:o + 4 * lora]
    gd = z[:, o + 4 * lora:]

    xw = w0_ref[...] + _dot(jnp.tanh(wd).astype(BF16), w2_ref[...])
    xa = a0_ref[...] + _dot(ad.astype(BF16), a2_ref[...])
    lw = -math.exp(-0.5) * jax.nn.sigmoid(xw)
    rate = jax.nn.sigmoid(xa)
    g = _dot(jax.nn.sigmoid(gd).astype(BF16), g2_ref[...])

    gsum = gsum_ref[...]
    kkr = k * kk_ref[...]
    ka = ka_ref[...]
    kd0 = k * (1.0 + (rate[:, :c_rw] - 1.0) * ka)
    kd1 = k * (1.0 + (rate[:, c_rw:] - 1.0) * ka)
    sums = _dot_hilo_rhs(jnp.concatenate([kkr * kkr, r * (0.5 * (kd0 + kd1)) * rk_ref[...]], axis=0), gsum)
    kk = kkr * lax.rsqrt(sums[:tt] + 1e-12)
    bonus = sums[tt:] * v

    r_ref[...] = r.astype(BF16)
    v_ref[...] = v.astype(BF16)
    nkk_ref[...] = (-kk).astype(BF16)
    lw_ref[0] = lw[:, :c_rw]
    lw_ref[1] = lw[:, c_rw:]
    kd_ref[0] = kd0.astype(BF16)
    kd_ref[1] = kd1.astype(BF16)
    bd_ref[0] = (kk * rate[:, :c_rw]).astype(BF16)
    bd_ref[1] = (kk * rate[:, c_rw:]).astype(BF16)
    g_ref[...] = g.astype(BF16)
    bonus_ref[...] = bonus.astype(BF16)


def _rwkv_prep(zrw3, mu, w2bd, w0f, a2bd, a0f, g2_bf, k_k, k_a, r_kf, gsum, c_rw, lora, tt):
    b, t, cols = zrw3.shape
    nt = t // tt
    hb = tt // 8
    nhb = t // 8
    full = lambda bi, i: (0, 0)
    tile = lambda bi, i: (bi, i, 0)
    dtile = lambda bi, i: (0, bi, i, 0)
    one = jax.ShapeDtypeStruct((b, t, c_rw), F32)
    two = jax.ShapeDtypeStruct((2, b, t, c_rw), F32)
    one16 = jax.ShapeDtypeStruct((b, t, c_rw), BF16)
    two16 = jax.ShapeDtypeStruct((2, b, t, c_rw), BF16)
    return pl.pallas_call(
        functools.partial(_rwkv_prep_kernel, c_rw=c_rw, lora=lora),
        grid=(b, nt),
        in_specs=[
            pl.BlockSpec((None, tt, cols), tile),
            pl.BlockSpec((None, 8, cols), lambda bi, i: (bi, jnp.maximum(i * hb - 1, 0), 0)),
            pl.BlockSpec((None, 8, cols), lambda bi, i: (bi, jnp.minimum((i + 1) * hb, nhb - 1), 0)),
            pl.BlockSpec((1, cols), full),
            pl.BlockSpec(w2bd.shape, full),
            pl.BlockSpec((1, 2 * c_rw), full),
            pl.BlockSpec(a2bd.shape, full),
            pl.BlockSpec((1, 2 * c_rw), full),
            pl.BlockSpec(g2_bf.shape, full),
            pl.BlockSpec((1, c_rw), full),
            pl.BlockSpec((1, c_rw), full),
            pl.BlockSpec((1, c_rw), full),
            pl.BlockSpec((c_rw, c_rw), full),
        ],
        out_specs=[
            pl.BlockSpec((None, tt, c_rw), tile),
            pl.BlockSpec((None, tt, c_rw), tile),
            pl.BlockSpec((None, tt, c_rw), tile),
            pl.BlockSpec((2, None, tt, c_rw), dtile),
            pl.BlockSpec((2, None, tt, c_rw), dtile),
            pl.BlockSpec((2, None, tt, c_rw), dtile),
            pl.BlockSpec((None, tt, c_rw), tile),
            pl.BlockSpec((None, tt, c_rw), tile),
        ],
        out_shape=[one16, one16, one16, two, two16, two16, one16, one16],
        compiler_params=_cparams(("parallel", "parallel")),
        name="rwkv_prep",
    )(zrw3, zrw3, zrw3, mu, w2bd, w0f, a2bd, a0f, g2_bf, k_k, k_a, r_kf, gsum)


M_STRICT, M_INCL, M_EYE, M_PAIR, M_LEVEL = 0, 1, 2, 3, 4
N_LEVELS = 5
SCAN_SEQS = 2


def _scan_tables(w):
    c = CHUNK
    masks = []
    mcums = []
    for d in range(2):
        t = np.broadcast_to(np.arange(c)[:, None], (c, w))
        s = np.broadcast_to((np.arange(w) % RW_HEAD)[None, :], (c, w))
        tt = np.broadcast_to(np.arange(c)[:, None], (c, c))
        ss = np.broadcast_to(np.arange(c)[None, :], (c, c))
        if d == 1:
            t, s, tt, ss = c - 1 - t, c - 1 - s, c - 1 - tt, c - 1 - ss
        rows = [s < t, s <= t, s == t, (t // 2 == s // 2) & (t > s)]
        sz = 2
        while sz < c:
            rows.append((t // (2 * sz) == s // (2 * sz)) & ((t // sz) % 2 == 1) & ((s // sz) % 2 == 0))
            sz *= 2
        assert len(rows) == M_LEVEL + N_LEVELS
        masks.append(np.stack(rows).astype(np.float32))
        mcums.append((ss <= tt).astype(np.float32))
    return jnp.asarray(np.stack(masks), F32), jnp.asarray(np.stack(mcums), BF16)


def _block_diag(y, bd16):
    g = y.shape[1] // RW_HEAD
    return jnp.concatenate([y.astype(BF16)] * g, axis=0) * bd16


def _bmm(x, y, bd16):
    outs = []
    for s in range(0, y.shape[1], HEAD_GROUP):
        outs.append(_dot(x[:, s:s + HEAD_GROUP].astype(BF16), _block_diag(y[:, s:s + HEAD_GROUP], bd16)))
    return jnp.concatenate(outs, axis=1)


def _bmm_nt(x, y, bd16):
    outs = []
    for s in range(0, y.shape[1], HEAD_GROUP):
        bd = _block_diag(y[:, s:s + HEAD_GROUP], bd16)
        outs.append(lax.dot_general(x[:, s:s + HEAD_GROUP].astype(BF16), bd, (((1,), (1,)), ((), ())),
                                    preferred_element_type=F32))
    return jnp.concatenate(outs, axis=1)


def _bmm_tn(x, y, bd32):
    outs = []
    for s in range(0, y.shape[1], HEAD_GROUP):
        xt = x[:, s:s + HEAD_GROUP].T.astype(BF16)
        full = _dot(xt, y[:, s:s + HEAD_GROUP].astype(BF16)) * bd32
        o = full[0:RW_HEAD]
        for j in range(1, HEAD_GROUP // RW_HEAD):
            o = o + full[j * RW_HEAD:(j + 1) * RW_HEAD]
        outs.append(o)
    return jnp.concatenate(outs, axis=1)


def _scan_chunks(r, lw, k, v, a, b, state, masks, mcum, bd16, bd32):
    c = r[0].shape[0]

    def each(f, *cols):
        return [f(*xs) for xs in zip(*cols)]

    cum = each(_dot_hilo_lhs, mcum, lw)
    tot = each(lambda x: jnp.sum(x, axis=0, keepdims=True), lw)
    e_neg = each(lambda x: jnp.exp(-x), cum)
    e_tail = each(lambda t_, x: jnp.exp(t_ - x), tot, cum)
    rt = each(lambda x, cm: x * jnp.exp(cm), r, cum)
    at = each(lambda x, cm, l_: x * jnp.exp(cm - l_), a, cum, lw)
    bt = each(jnp.multiply, b, e_neg)
    kt = each(jnp.multiply, k, e_neg)
    bh = each(jnp.multiply, b, e_tail)
    kh = each(jnp.multiply, k, e_tail)

    ar = each(lambda x, y_: jnp.concatenate([x, y_], axis=0), at, rt)
    pb = each(lambda x, y_: _bmm_nt(x, y_, bd16), ar, bt)
    pk = each(lambda x, y_: _bmm_nt(x, y_, bd16), ar, kt)
    a_ab = each(lambda x, m: x[:c] * m[M_STRICT], pb, masks)
    a_rb = each(lambda x, m: x[c:] * m[M_INCL], pb, masks)
    a_ak = each(lambda x, m: x[:c] * m[M_STRICT], pk, masks)
    a_rk = each(lambda x, m: x[c:] * m[M_INCL], pk, masks)

    tinv = each(lambda x, m: m[M_EYE] + x * m[M_PAIR], a_ab, masks)
    for lvl in range(N_LEVELS):
        x = each(lambda aa, m, ti: _bmm(aa * m[M_LEVEL + lvl], ti, bd16), a_ab, masks, tinv)
        tinv = each(lambda ti, x_: ti + _bmm(ti, x_, bd16), tinv, x)

    bmm = lambda p, q: _bmm(p, q, bd16)
    stack = lambda p, q: jnp.concatenate([p, q], axis=0)
    att = each(bmm, tinv, at)
    akv = each(lambda p, p2, q: _bmm(stack(p, p2), q, bd16), a_ak, a_rk, v)
    vt = each(lambda ti, x_: _bmm(ti, x_[:c], bd16), tinv, akv)
    rh = each(lambda x, p, q: x + _bmm(p, q, bd16), rt, a_rb, att)
    y_in = each(lambda p, q, x_: _bmm(p, q, bd16) + x_[c:], a_rb, vt, akv)
    mc = each(lambda m, t_, p, q: m[M_EYE] * jnp.exp(t_) + _bmm_tn(p, q, bd32), masks, tot, bh, att)
    nc = each(lambda p, p2, q, q2: _bmm_tn(stack(p, p2), stack(q, q2), bd32), bh, kh, vt, v)
    onst = each(lambda p, p2, q: _bmm(stack(p, p2), q, bd16), rh, mc, state)
    y = each(lambda x_, yi: x_[:c] + yi, onst, y_in)
    new_state = each(lambda x_, n_: x_[c:] + n_, onst, nc)
    return y, new_state


def _rwkv_scan_kernel(rf_ref, vf_ref, af_ref, rb_ref, vb_ref, ab_ref, lwf_ref, kf_ref, bf_ref, lwb_ref, kb_ref,
                      bb_ref, mask_ref, mcum_ref, bd16_ref, bd32_ref, yf_ref, yb_ref, state_ref, *, n_chunks):
    @pl.when(pl.program_id(1) == 0)
    def _():
        state_ref[...] = jnp.zeros_like(state_ref)

    dirs = ((0, rf_ref, lwf_ref, kf_ref, vf_ref, af_ref, bf_ref, yf_ref),
            (1, rb_ref, lwb_ref, kb_ref, vb_ref, ab_ref, bb_ref, yb_ref))
    n_seq = rf_ref.shape[0]

    def body(j, carry):
        cols = [[] for _ in range(9)]
        where = []
        for d, r_ref, lw_ref, k_ref, v_ref, a_ref, b_ref, y_ref in dirs:
            cj = j if d == 0 else n_chunks - 1 - j
            sl = pl.ds(pl.multiple_of(cj * CHUNK, CHUNK), CHUNK)
            for g in range(n_seq):
                vals = (r_ref[g, sl, :], lw_ref[g, sl, :], k_ref[g, sl, :], v_ref[g, sl, :], a_ref[g, sl, :],
                        b_ref[g, sl, :], state_ref[d, g], mask_ref.at[d], mcum_ref[d])
                for col, val in zip(cols, vals):
                    col.append(val)
                where.append((y_ref, d, g, sl))
        ys, sts = _scan_chunks(*cols, bd16_ref[...], bd32_ref[...])
        for (y_ref, d, g, sl), y, st in zip(where, ys, sts):
            y_ref[g, sl, :] = y.astype(y_ref.dtype)
            state_ref[d, g] = st
        return carry

    lax.fori_loop(0, n_chunks, body, 0)


def _rwkv_scan(r, v, nkk, lw, kd, bd, tb):
    b, t, w = r.shape
    nb = t // tb
    g = SCAN_SEQS if b % SCAN_SEQS == 0 else 1
    masks, mcum = _scan_tables(w)
    bd32 = _group_matrix(HEAD_GROUP, RW_HEAD, 1.0).astype(F32)
    bd16 = bd32.astype(BF16)
    fwd = lambda bi, i: (bi, i, 0)
    bwd = lambda bi, i: (bi, nb - 1 - i, 0)
    fwd_d = lambda bi, i: (0, bi, i, 0)
    bwd_d = lambda bi, i: (1, bi, nb - 1 - i, 0)
    const = lambda nd: (lambda bi, i: (0,) * nd)
    blk = (g, tb, w)
    dblk = (None, g, tb, w)
    out = jax.ShapeDtypeStruct((b, t, w), BF16)
    return pl.pallas_call(
        functools.partial(_rwkv_scan_kernel, n_chunks=tb // CHUNK),
        grid=(b // g, nb),
        in_specs=[
            pl.BlockSpec(blk, fwd), pl.BlockSpec(blk, fwd), pl.BlockSpec(blk, fwd),
            pl.BlockSpec(blk, bwd), pl.BlockSpec(blk, bwd), pl.BlockSpec(blk, bwd),
            pl.BlockSpec(dblk, fwd_d), pl.BlockSpec(dblk, fwd_d), pl.BlockSpec(dblk, fwd_d),
            pl.BlockSpec(dblk, bwd_d), pl.BlockSpec(dblk, bwd_d), pl.BlockSpec(dblk, bwd_d),
            pl.BlockSpec(masks.shape, const(4)),
            pl.BlockSpec(mcum.shape, const(3)),
            pl.BlockSpec(bd16.shape, const(2)),
            pl.BlockSpec(bd32.shape, const(2)),
        ],
        out_specs=[pl.BlockSpec(blk, fwd), pl.BlockSpec(blk, bwd)],
        out_shape=[out, out],
        scratch_shapes=[pltpu.VMEM((2, g, CHUNK, w), F32)],
        compiler_params=_cparams(("parallel", "arbitrary")),
        name="rwkv_scan",
    )(r, v, nkk, r, v, nkk, lw, kd, bd, lw, kd, bd, masks, mcum, bd16, bd32)


def _attn_kernel(slope_ref, q_ref, k_ref, v_ref, lq1_ref, lk1_ref, lq2_ref, lk2_ref, sg_ref, o_ref,
                 m_ref, l_ref, acc_ref, *, qk_dim, lambda_init):
    h = pl.program_id(1)
    i = pl.program_id(2)
    j = pl.program_id(3)
    nk = pl.num_programs(3)
    tq = q_ref.shape[0]
    tk = k_ref.shape[0]

    @pl.when(j == 0)
    def _():
        m_ref[...] = jnp.full(m_ref.shape, -jnp.inf, F32)
        l_ref[...] = jnp.zeros_like(l_ref)
        acc_ref[...] = jnp.zeros_like(acc_ref)

    q = q_ref[...]
    k = k_ref[...]
    v = v_ref[...]
    lane = lax.broadcasted_iota(jnp.int32, q.shape, 1)
    zero = jnp.zeros_like(q)
    qi = lax.broadcasted_iota(jnp.int32, (tq, tk), 0) + i * tq
    kj = lax.broadcasted_iota(jnp.int32, (tq, tk), 1) + j * tk
    bias = slope_ref[h] * jnp.abs(qi - kj).astype(F32)
    for c in range(2):
        qc = jnp.where((lane < qk_dim) == (c == 0), q, zero)
        s = lax.dot_general(qc, k, (((1,), (1,)), ((), ())), preferred_element_type=F32) - bias
        m_old = m_ref[c]
        m_new = jnp.maximum(m_old, jnp.max(s, axis=-1, keepdims=True))
        alpha = jnp.exp(m_old - m_new)
        p = jnp.exp(s - m_new)
        l_ref[c] = alpha * l_ref[c] + jnp.sum(p, axis=-1, keepdims=True)
        acc_ref[c] = alpha * acc_ref[c] + _dot(p.astype(BF16), v)
        m_ref[c] = m_new

    @pl.when(j == nk - 1)
    def _():
        lam = (jnp.exp(jnp.sum(lq1_ref[...] * lk1_ref[...], axis=-1, keepdims=True))
               - jnp.exp(jnp.sum(lq2_ref[...] * lk2_ref[...], axis=-1, keepdims=True)) + lambda_init)
        o = acc_ref[0] / l_ref[0] - lam * (acc_ref[1] / l_ref[1])
        ms = jnp.mean(o * o, axis=-1, keepdims=True)
        o_ref[...] = o * lax.rsqrt(ms + NORM_EPS) * (sg_ref[...] * (1.0 - lambda_init))


def _attention(slopes, q3, k3, v3, lq1, lk1, lq2, lk2, subln_g, heads, v_dim, lambda_init, tq, tk):
    b, t, _ = q3.shape
    const = lambda bi, h, i, j, s: (0, 0)
    grid_spec = pltpu.PrefetchScalarGridSpec(
        num_scalar_prefetch=1,
        grid=(b, heads, t // tq, t // tk),
        in_specs=[
            pl.BlockSpec((None, tq, v_dim), lambda bi, h, i, j, s: (bi, i, h)),
            pl.BlockSpec((None, tk, v_dim), lambda bi, h, i, j, s: (bi, j, h)),
            pl.BlockSpec((None, tk, v_dim), lambda bi, h, i, j, s: (bi, j, h)),
            pl.BlockSpec(lq1.shape, const),
            pl.BlockSpec(lk1.shape, const),
            pl.BlockSpec(lq2.shape, const),
            pl.BlockSpec(lk2.shape, const),
            pl.BlockSpec(subln_g.shape, const),
        ],
        out_specs=pl.BlockSpec((None, tq, v_dim), lambda bi, h, i, j, s: (bi, i, h)),
        scratch_shapes=[
            pltpu.VMEM((2, tq, 1), F32),
            pltpu.VMEM((2, tq, 1), F32),
            pltpu.VMEM((2, tq, v_dim), F32),
        ],
    )
    return pl.pallas_call(
        functools.partial(_attn_kernel, qk_dim=v_dim // 2, lambda_init=lambda_init),
        grid_spec=grid_spec,
        out_shape=jax.ShapeDtypeStruct((b, t, heads * v_dim), F32),
        compiler_params=_cparams(("parallel", "parallel", "parallel", "arbitrary")),
        name="attn",
    )(slopes, q3, k3, v3, lq1, lk1, lq2, lk2, subln_g)


EXP_ZERO_ARG = 104.0
MAX_FIXED_SHIFT = 40.0
N_AUG = 6
ATTN_BLOCK = 512
ATTN_GROUP_SIZES = (8, 4, 2, 1)
ATTN_KEY_SPLIT = 1
ATTN_V_PAD = 16


def _attn_fixed_kernel(w_ref, slope_ref, mb_ref, q_ref, k_ref, vt_ref, dbias_ref, qaug_ref, kaug_ref,
                       lq1_ref, lk1_ref, lq2_ref, lk2_ref, sg_ref, o_ref, acc_ref, *, qk_dim, lambda_init):
    h = pl.program_id(1)
    i = pl.program_id(2)
    tb = q_ref.shape[0]
    nk = k_ref.shape[0] // tb
    v_dim = 2 * qk_dim
    slope = slope_ref[h]
    mb = mb_ref[0]
    w = w_ref[h]

    q = q_ref[...]
    lane = lax.broadcasted_iota(jnp.int32, q.shape, 1)
    lane_row = lax.broadcasted_iota(jnp.int32, (1, q.shape[1]), 1)
    is_c0 = lane < qk_dim
    datas = (is_c0, jnp.logical_not(is_c0))

    def blocks(items, qvar, kvar, diag):
        sub = tb // ATTN_KEY_SPLIT
        qas = [jnp.where(datas[c], q, qaug_ref[qvar, c]) for c in range(2)]
        chains = []
        for n, (j, delta) in enumerate(items):
            kblk = k_ref[pl.ds(pl.multiple_of(j * tb, tb), tb), :]
            vtblk = vt_ref[j]
            for c in range(2):
                base = qk_dim * (1 - c)
                dyn = jnp.where(lane_row == base + 4, -mb,
                                jnp.where(lane_row == base + 5, -(slope * tb) * delta, 0.0)).astype(BF16)
                ka = jnp.where(datas[c], kblk, kaug_ref[kvar, c] + dyn)
                for u in range(ATTN_KEY_SPLIT):
                    rows = slice(u * sub, (u + 1) * sub)
                    chains.append((c, u, qas[c], ka[rows], vtblk[:, rows]))
        sts = [lax.dot_general(ka, qa, (((1,), (1,)), ((), ())), preferred_element_type=F32)
               for _, _, qa, ka, _ in chains]
        if diag:
            sts = [st - dbias_ref[u * sub:(u + 1) * sub, :] for st, (_, u, _, _, _) in zip(sts, chains)]
        pts = [jnp.exp(st).astype(BF16) for st in sts]
        for c in range(2):
            upd = None
            for pt, (cc, _, _, _, vsub) in zip(pts, chains):
                if cc == c:
                    part = _dot(vsub, pt)
                    upd = part if upd is None else upd + part
            if diag:
                acc_ref[c] = upd
            else:
                acc_ref[c] += upd

    def span(first, count, qvar, kvar, dist):
        for per in ATTN_GROUP_SIZES:
            def group(g, carry, first=first, per=per):
                j0 = first + g * per
                blocks([(j0 + n, dist(j0 + n)) for n in range(per)], qvar, kvar, False)
                return carry

            groups = count // per
            lax.fori_loop(0, groups, group, 0)
            first = first + groups * per
            count = count - groups * per

    lo = jnp.maximum(i - w, 0)
    hi = jnp.minimum(i + w, nk - 1)
    blocks([(i, jnp.zeros((), F32))], 2, 0, True)
    span(lo, i - lo, 0, 0, lambda j: (i - j).astype(F32))
    span(i + 1, hi - i, 1, 1, lambda j: (j - i).astype(F32))

    lam = (jnp.exp(jnp.sum(lq1_ref[...] * lk1_ref[...], axis=-1, keepdims=True))
           - jnp.exp(jnp.sum(lq2_ref[...] * lk2_ref[...], axis=-1, keepdims=True)) + lambda_init)
    a0 = acc_ref[0]
    a1 = acc_ref[1]
    inv0 = 1.0 / a0[v_dim:v_dim + 1]
    inv1 = lam / a1[v_dim:v_dim + 1]
    ot = a0[:v_dim] * inv0 - a1[:v_dim] * inv1
    ms = jnp.mean(ot * ot, axis=0, keepdims=True)
    o_ref[...] = (ot * lax.rsqrt(ms + NORM_EPS) * (sg_ref[...] * (1.0 - lambda_init))).T.astype(o_ref.dtype)


def _attn_aug_tables(slopes, tb, qk_dim):
    heads = len(slopes)
    lanes = 2 * qk_dim
    pos = np.arange(tb)
    qaug = np.zeros((heads, 3, 2, tb, lanes), np.float32)
    kaug = np.zeros((heads, 2, 2, tb, lanes), np.float32)
    for hd, sl in enumerate(slopes):
        for var in range(3):
            rr = pos if var != 1 else tb - 1 - pos
            hi = (rr // 16) * (16.0 * sl)
            lo = (rr % 16) * sl
            for c in range(2):
                base = qk_dim * (1 - c)
                if var == 2:
                    qaug[hd, var, c, :, base + 4] = 1.0
                    continue
                qaug[hd, var, c, :, base + 0] = -hi
                qaug[hd, var, c, :, base + 1] = -lo
                qaug[hd, var, c, :, base + 2:base + N_AUG] = 1.0
                kaug[hd, var, c, :, base + 0:base + 2] = 1.0
                kaug[hd, var, c, :, base + 2] = hi
                kaug[hd, var, c, :, base + 3] = lo
    return jnp.asarray(qaug, BF16), jnp.asarray(kaug, BF16)


def _attention_fixed(mb, q3, k3, vt, lq1, lk1, lq2, lk2, subln_g, heads, v_dim, lambda_init, tb):
    b, t, _ = q3.shape
    qk_dim = v_dim // 2
    slopes = [2.0 ** (-8.0 * (hd + 1) / heads) for hd in range(heads)]
    widths = [int(math.ceil((EXP_ZERO_ARG / s - 1.0) / tb)) for s in slopes]
    qaug, kaug = _attn_aug_tables(slopes, tb, qk_dim)
    pos = np.arange(tb)
    dist = np.abs(pos[:, None] - pos[None, :]).astype(np.float32)
    dbias = jnp.asarray(np.stack([dist * s for s in slopes]), F32)
    nb = t // tb
    vrows = v_dim + ATTN_V_PAD
    assert vt.shape == (b, heads, nb, vrows, tb)
    const = lambda bi, h, i, *_: (0, 0)
    grid_spec = pltpu.PrefetchScalarGridSpec(
        num_scalar_prefetch=3,
        grid=(b, heads, t // tb),
        in_specs=[
            pl.BlockSpec((None, tb, v_dim), lambda bi, h, i, *_: (bi, i, h)),
            pl.BlockSpec((None, t, v_dim), lambda bi, h, i, *_: (bi, 0, h)),
            pl.BlockSpec((None, None, nb, vrows, tb), lambda bi, h, i, *_: (bi, h, 0, 0, 0)),
            pl.BlockSpec((None, tb, tb), lambda bi, h, i, *_: (h, 0, 0)),
            pl.BlockSpec((None, 3, 2, tb, v_dim), lambda bi, h, i, *_: (h, 0, 0, 0, 0)),
            pl.BlockSpec((None, 2, 2, tb, v_dim), lambda bi, h, i, *_: (h, 0, 0, 0, 0)),
            pl.BlockSpec(lq1.shape, const),
            pl.BlockSpec(lk1.shape, const),
            pl.BlockSpec(lq2.shape, const),
            pl.BlockSpec(lk2.shape, const),
            pl.BlockSpec((v_dim, 1), const),
        ],
        out_specs=pl.BlockSpec((None, tb, v_dim), lambda bi, h, i, *_: (bi, i, h)),
        scratch_shapes=[pltpu.VMEM((2, vrows, tb), F32)],
    )
    return pl.pallas_call(
        functools.partial(_attn_fixed_kernel, qk_dim=qk_dim, lambda_init=lambda_init),
        grid_spec=grid_spec,
        out_shape=jax.ShapeDtypeStruct((b, t, heads * v_dim), BF16),
        compiler_params=_cparams(("parallel", "parallel", "parallel")),
        name="attn_fixed",
    )(jnp.asarray(widths, jnp.int32), jnp.asarray(slopes, F32), mb.reshape(1).astype(F32),
      q3, k3, vt, dbias, qaug, kaug, lq1, lk1, lq2, lk2, subln_g.reshape(v_dim, 1))


def _post_kernel(yf_ref, yb_ref, bonus_ref, g_ref, oda_ref, x_ref, lng_ref, lnb_ref, gmean_ref, wo_rw_ref, wo_da_ref,
                 gffn_ref, wr_ref, br_ref, h_ref, m_ref, idx_ref, gate_ref, cnt_ref, *, gn_eps):
    @pl.when(pl.program_id(0) == 0)
    def _():
        cnt_ref[...] = jnp.zeros_like(cnt_ref)

    gmean = gmean_ref[...]
    y = yf_ref[...].astype(F32) + yb_ref[...].astype(F32)
    mean = _dot_hilo_rhs(y, gmean)
    yc = y - mean
    var = _dot_hilo_rhs(yc * yc, gmean)
    yn = yc * lax.rsqrt(var + gn_eps) * lng_ref[...] + lnb_ref[...]
    o_rw = (yn + bonus_ref[...]) * g_ref[...]
    h = x_ref[...] + _dot(o_rw.astype(BF16), wo_rw_ref[...]) + _dot(oda_ref[...].astype(BF16), wo_da_ref[...])
    h_ref[...] = h
    ms = jnp.mean(h * h, axis=-1, keepdims=True)
    m = h * lax.rsqrt(ms + NORM_EPS) * gffn_ref[...]
    _store_rows(m_ref, m)

    logits = _dot_hilo(m, wr_ref[...]) + br_ref[...]
    ne = logits.shape[1]
    lane = lax.broadcasted_iota(jnp.int32, logits.shape, 1).astype(F32)
    kcol = lax.broadcasted_iota(jnp.int32, (logits.shape[0], TOP_K), 1)
    idx_all = jnp.zeros((logits.shape[0], TOP_K), F32)
    val_all = jnp.zeros((logits.shape[0], TOP_K), F32)
    work = logits
    chosen = jnp.zeros(logits.shape, F32)
    for kk in range(TOP_K):
        mx = jnp.max(work, axis=-1, keepdims=True)
        sel = jnp.min(jnp.where(work == mx, lane, float(ne)), axis=-1, keepdims=True)
        idx_all = jnp.where(kcol == kk, sel, idx_all)
        val_all = jnp.where(kcol == kk, mx, val_all)
        hit = lane == sel
        chosen = jnp.where(hit, 1.0, chosen)
        work = jnp.where(hit, -jnp.inf, work)
    e = jnp.exp(val_all - jnp.max(val_all, axis=-1, keepdims=True))
    idx_ref[...] = idx_all.astype(jnp.int32)
    gate_ref[...] = e / jnp.sum(e, axis=-1, keepdims=True)
    cnt_ref[...] += jnp.sum(chosen, axis=0, keepdims=True)


def _post(yf, yb, bonus, g, oda, x2, lng, lnb, gmean, wo_rw, wo_da, gffn, wr, br, gn_eps, tm):
    n, c = yf.shape
    d = x2.shape[1]
    ne = wr.shape[1]
    full = lambda i: (0, 0)
    row = lambda i: (i, 0)
    return pl.pallas_call(
        functools.partial(_post_kernel, gn_eps=gn_eps),
        grid=(n // tm,),
        in_specs=[
            pl.BlockSpec((tm, c), row),
            pl.BlockSpec((tm, c), row),
            pl.BlockSpec((tm, c), row),
            pl.BlockSpec((tm, c), row),
            pl.BlockSpec((tm, c), row),
            pl.BlockSpec((tm, d), row),
            pl.BlockSpec((1, c), full),
            pl.BlockSpec((1, c), full),
            pl.BlockSpec((c, c), full),
            pl.BlockSpec((c, d), full),
            pl.BlockSpec((c, d), full),
            pl.BlockSpec((1, d), full),
            pl.BlockSpec((d, ne), full),
            pl.BlockSpec((1, ne), full),
        ],
        out_specs=[
            pl.BlockSpec((tm, d), row),
            pl.BlockSpec((tm * d // V7X_LANES, V7X_LANES), row),
            pl.BlockSpec((tm, TOP_K), row),
            pl.BlockSpec((tm, TOP_K), row),
            pl.BlockSpec((1, ne), full),
        ],
        out_shape=[
            jax.ShapeDtypeStruct((n, d), F32),
            jax.ShapeDtypeStruct((n * d // V7X_LANES, V7X_LANES), F32),
            jax.ShapeDtypeStruct((n, TOP_K), jnp.int32),
            jax.ShapeDtypeStruct((n, TOP_K), F32),
            jax.ShapeDtypeStruct((1, ne), F32),
        ],
        compiler_params=_cparams(("arbitrary",)),
        name="post",
    )(yf, yb, bonus, g, oda, x2, lng, lnb, gmean, wo_rw, wo_da, gffn, wr, br)


def _rank_kernel(idx_ref, ltri_ref, offs_ref, slot_ref, carry_ref):
    @pl.when(pl.program_id(0) == 0)
    def _():
        carry_ref[...] = offs_ref[...]

    idx = idx_ref[...]
    tr = idx.shape[0]
    lane = lax.broadcasted_iota(jnp.int32, (tr, N_EXPERTS), 1)
    hots = [jnp.where(lane == idx[:, kk:kk + 1], 1.0, 0.0) for kk in range(TOP_K)]
    mask = hots[0]
    for kk in range(1, TOP_K):
        mask = mask + hots[kk]
    before = _dot(ltri_ref[...], mask.astype(BF16)) + carry_ref[...]
    kcol = lax.broadcasted_iota(jnp.int32, (tr, TOP_K), 1)
    slot = jnp.zeros((tr, TOP_K), F32)
    for kk in range(TOP_K):
        rk = jnp.sum(hots[kk] * before, axis=-1, keepdims=True)
        slot = jnp.where(kcol == kk, rk, slot)
    slot_ref[...] = slot.astype(jnp.int32)
    carry_ref[...] = carry_ref[...] + jnp.sum(mask, axis=0, keepdims=True)


def _moe_rank(idx, offs_row, tr):
    n = idx.shape[0]
    ltri = jnp.asarray(np.tril(np.ones((tr, tr), np.float32), -1), BF16)
    return pl.pallas_call(
        _rank_kernel,
        grid=(n // tr,),
        in_specs=[pl.BlockSpec((tr, TOP_K), lambda i: (i, 0)), pl.BlockSpec((tr, tr), lambda i: (0, 0)),
                  pl.BlockSpec((1, N_EXPERTS), lambda i: (0, 0))],
        out_specs=pl.BlockSpec((tr, TOP_K), lambda i: (i, 0)),
        out_shape=jax.ShapeDtypeStruct((n, TOP_K), jnp.int32),
        scratch_shapes=[pltpu.VMEM((1, N_EXPERTS), F32)],
        compiler_params=_cparams(("arbitrary",)),
        name="moe_rank",
    )(idx, ltri, offs_row)


ROW_UNROLL = 8


def _row(r, c):
    return pl.ds(pl.multiple_of(r * c, c), c)


def _dispatch_kernel(slot_ref, m_ref, xs_ref, sem, *, c):
    td = m_ref.shape[0] // c

    def issue(g, carry):
        for u in range(ROW_UNROLL):
            t = g * ROW_UNROLL + u
            for kk in range(TOP_K):
                dest = slot_ref[t * TOP_K + kk]
                pltpu.make_async_copy(m_ref.at[_row(t, c), :], xs_ref.at[_row(dest, c), :], sem).start(
                    priority=kk % 2)
        return carry

    lax.fori_loop(0, td // ROW_UNROLL, issue, 0)
    for kk in range(TOP_K):
        pltpu.make_async_copy(m_ref, xs_ref.at[pl.ds(0, td * c), :], sem).wait()


def _dispatch(slot_flat, mrows, c, p_rows, td):
    n = mrows.shape[0] // c
    assert td % ROW_UNROLL == 0
    return pl.pallas_call(
        functools.partial(_dispatch_kernel, c=c),
        grid=(n // td,),
        in_specs=[
            pl.BlockSpec((td * TOP_K,), lambda i: (i,), memory_space=pltpu.SMEM),
            pl.BlockSpec((td * c, V7X_LANES), lambda i: (i, 0)),
        ],
        out_specs=pl.BlockSpec(memory_space=pl.ANY),
        scratch_shapes=[pltpu.SemaphoreType.DMA(())],
        out_shape=jax.ShapeDtypeStruct((p_rows * c, V7X_LANES), F32),
        compiler_params=_cparams(("arbitrary",), has_side_effects=True),
        name="moe_dispatch",
    )(slot_flat, mrows)


def _ffn_kernel(be_ref, nu_ref, x_ref, w1_ref, b1_ref, w2_ref, b2_ref, y_ref, w1s_ref, w2s_ref, *, d_ff):
    i = pl.program_id(0)

    @pl.when((i == 0) | (be_ref[i] != be_ref[jnp.maximum(i - 1, 0)]))
    def _():
        w1s_ref[...] = w1_ref[...].astype(BF16)
        w2s_ref[...] = w2_ref[...].astype(BF16)

    @pl.when(i < nu_ref[0])
    def _():
        x = _load_rows(x_ref, w1_ref.shape[0] // V7X_LANES).astype(BF16)
        hcat = _dot(x, w1s_ref[...]) + b1_ref[...]
        glu = jnp.minimum(hcat[:, :d_ff], SWIGLU_LIMIT)
        lin = jnp.clip(hcat[:, d_ff:], -SWIGLU_LIMIT, SWIGLU_LIMIT)
        act = glu * jax.nn.sigmoid(SWIGLU_ALPHA * glu) * (lin + 1.0)
        _store_rows(y_ref, _dot(act.astype(BF16), w2s_ref[...]) + b2_ref[...])


def _moe_ffn(blk_e, n_used, xrows, w1, b1, w2, b2, bm):
    ne, d, f2 = w1.shape
    d_ff = f2 // 2
    c = d // V7X_LANES
    p_rows = xrows.shape[0] // c
    blk = (bm * c, V7X_LANES)
    grid_spec = pltpu.PrefetchScalarGridSpec(
        num_scalar_prefetch=2,
        grid=(p_rows // bm,),
        in_specs=[
            pl.BlockSpec(blk, lambda i, be, nu: (jnp.minimum(i, nu[0] - 1), 0)),
            pl.BlockSpec((None, d, f2), lambda i, be, nu: (be[i], 0, 0)),
            pl.BlockSpec((None, 1, f2), lambda i, be, nu: (be[i], 0, 0)),
            pl.BlockSpec((None, d_ff, d), lambda i, be, nu: (be[i], 0, 0)),
            pl.BlockSpec((None, 1, d), lambda i, be, nu: (be[i], 0, 0)),
        ],
        out_specs=pl.BlockSpec(blk, lambda i, be, nu: (i, 0)),
        scratch_shapes=[pltpu.VMEM((d, f2), BF16), pltpu.VMEM((d_ff, d), BF16)],
    )
    return pl.pallas_call(
        functools.partial(_ffn_kernel, d_ff=d_ff),
        grid_spec=grid_spec,
        out_shape=jax.ShapeDtypeStruct(xrows.shape, F32),
        compiler_params=_cparams(("arbitrary",)),
        name="moe_ffn",
    )(blk_e, n_used, xrows, w1, b1.reshape(ne, 1, f2), w2, b2.reshape(ne, 1, d))


def _combine_kernel(slot_ref, next_slot_ref, h_ref, gate_ref, p_ref, wp_ref, gp_ref, wg_ref, y_ref, o_ref,
                    buf_ref, h2_ref, sems):
    i = pl.program_id(0)
    tc, d = h_ref.shape
    c = d // V7X_LANES

    def start_gather(slots, half):
        def issue(g, carry):
            for u in range(ROW_UNROLL):
                t = g * ROW_UNROLL + u
                for kk in range(TOP_K):
                    src = slots[t * TOP_K + kk]
                    pltpu.make_async_copy(y_ref.at[_row(src, c), :], buf_ref.at[half, kk, _row(t, c), :],
                                          sems.at[half]).start(priority=kk % 2)
            return carry

        lax.fori_loop(0, tc // ROW_UNROLL, issue, 0)

    cur = i % 2

    @pl.when(i == 0)
    def _():
        start_gather(slot_ref, 0)

    @pl.when(i + 1 < pl.num_programs(0))
    def _():
        start_gather(next_slot_ref, 1 - cur)

    for kk in range(TOP_K):
        pltpu.make_async_copy(y_ref.at[pl.ds(0, tc * c), :], buf_ref.at[cur, kk], sems.at[cur]).wait()

    gate = gate_ref[...]
    for j in range(c):
        cols = slice(j * V7X_LANES, (j + 1) * V7X_LANES)
        acc = h_ref[:, cols]
        for kk in range(TOP_K):
            acc = acc + gate[:, kk:kk + 1] * buf_ref[cur, kk, pl.ds(j, tc, stride=c), :]
        h2_ref[:, cols] = acc

    h2 = h2_ref[...]
    e = _dot(p_ref[...].astype(BF16), wp_ref[...])
    ms = jnp.mean(e * e, axis=-1, keepdims=True)
    e = e * lax.rsqrt(ms + NORM_EPS) * gp_ref[...]
    o_ref[...] = h2 + e * jax.nn.sigmoid(_dot(h2.astype(BF16), wg_ref[...]))


def _combine_ple(slot_flat, h, gates, yrows, p2, wp_bf, gp, wg_bf, tc):
    n, d = h.shape
    pd = p2.shape[1]
    assert tc % ROW_UNROLL == 0
    last = n // tc - 1
    full = lambda i: (0, 0)
    row = lambda i: (i, 0)
    return pl.pallas_call(
        _combine_kernel,
        grid=(n // tc,),
        in_specs=[
            pl.BlockSpec((tc * TOP_K,), lambda i: (i,), memory_space=pltpu.SMEM),
            pl.BlockSpec((tc * TOP_K,), lambda i: (jnp.minimum(i + 1, last),), memory_space=pltpu.SMEM),
            pl.BlockSpec((tc, d), row),
            pl.BlockSpec((tc, TOP_K), row),
            pl.BlockSpec((tc, pd), row),
            pl.BlockSpec((pd, d), full),
            pl.BlockSpec((1, d), full),
            pl.BlockSpec((d, d), full),
            pl.BlockSpec(memory_space=pl.ANY),
        ],
        out_specs=pl.BlockSpec((tc, d), row),
        scratch_shapes=[pltpu.VMEM((2, TOP_K, tc * d // V7X_LANES, V7X_LANES), F32),
                        pltpu.VMEM((tc, d), F32),
                        pltpu.SemaphoreType.DMA((2,))],
        out_shape=jax.ShapeDtypeStruct((n, d), F32),
        compiler_params=_cparams(("arbitrary",)),
        name="moe_combine_ple",
    )(slot_flat, slot_flat, h, gates, p2, wp_bf, gp, wg_bf, yrows)


def _tile(n, want):
    t = min(n, want)
    assert n % t == 0, (n, t)
    return t


def _layer(h3, p3, lambda_init, norm_mix_g, w_in, rw_mu, rw_w0, rw_w2, rw_a0, rw_a2, rw_g2, rw_k_k, rw_k_a, rw_r_k,
           rw_ln_g, rw_ln_b, da_q_norm_g, da_k_norm_g, da_lq1, da_lk1, da_lq2, da_lk2, da_subln_g, w_out,
           norm_ffn_g, w_router, b_router, w1, b1, w2, b2, w_ple, ple_norm_g, w_ple_gate):
    b, t, d = h3.shape
    n = b * t
    c_rw = rw_k_k.shape[0]
    lora = rw_w2.shape[1]
    rw_cols = rw_mu.shape[0]
    rw_heads = rw_r_k.shape[0]
    c_da = w_in.shape[1] - rw_cols
    assert c_da % 3 == 0
    c_da //= 3
    qk_dim = da_q_norm_g.shape[0]
    v_dim = da_subln_g.shape[0]
    da_heads = c_da // v_dim
    assert rw_heads * RW_HEAD == c_rw and c_rw % HEAD_GROUP == 0 and v_dim == V7X_LANES and 2 * qk_dim == v_dim
    assert 2 * lora == V7X_LANES and w_router.shape[1] == N_EXPERTS

    x2 = h3.reshape(n, d)
    row = lambda a: a.reshape(1, -1).astype(F32)

    gq_t = row(jnp.tile(da_q_norm_g, c_da // qk_dim))
    gk_t = row(jnp.tile(da_k_norm_g, c_da // qk_dim))
    gm_qk = _group_matrix(c_da, qk_dim, 1.0 / qk_dim)
    ta = _tile(t, ATTN_BLOCK)
    zrw, q, k, vt = _inproj(x2, row(norm_mix_g), w_in.astype(BF16), gq_t, gk_t, gm_qk, rw_cols, c_da,
                            1.0 / math.sqrt(qk_dim), _tile(ta, ROW_TILE), b, da_heads, ta)

    zeros = jnp.zeros((lora, c_rw), F32)
    w2bd = jnp.concatenate([jnp.concatenate([rw_w2[0], zeros], axis=1),
                            jnp.concatenate([zeros, rw_w2[1]], axis=1)], axis=0).astype(BF16)
    a2bd = jnp.concatenate([jnp.concatenate([rw_a2[0], zeros], axis=1),
                            jnp.concatenate([zeros, rw_a2[1]], axis=1)], axis=0).astype(BF16)
    gsum = _group_matrix(c_rw, RW_HEAD, 1.0)
    r, vv, nkk, lw, kd, bd, g, bonus = _rwkv_prep(
        zrw.reshape(b, t, rw_cols), row(rw_mu), w2bd, row(rw_w0), a2bd, row(rw_a0), rw_g2.astype(BF16),
        row(rw_k_k), row(rw_k_a), row(rw_r_k), gsum, c_rw, lora, _tile(t, ROW_TILE))
    yf, yb = _rwkv_scan(r, vv, nkk, lw, kd, bd, _tile(t, SCAN_TILE))

    slopes = 2.0 ** (-8.0 * jnp.arange(1, da_heads + 1, dtype=F32) / da_heads)
    q3 = q.reshape(b, t, c_da)
    k3 = k.reshape(b, t, c_da)
    small = (row(da_lq1), row(da_lk1), row(da_lq2), row(da_lk2), row(da_subln_g))

    def attention_running_max():
        v3 = vt[:, :, :, :v_dim, :].transpose(0, 2, 4, 1, 3).reshape(b, t, c_da)
        return _attention(slopes, q3, k3, v3, *small, da_heads, v_dim, lambda_init, ta, ta)

    score_bound = 1.01 * math.sqrt(qk_dim) * jnp.max(jnp.abs(da_q_norm_g)) * jnp.max(jnp.abs(da_k_norm_g))
    mb = jnp.ceil(2.0 * score_bound.astype(F32)) * 0.5
    o_da = lax.cond(
        mb <= MAX_FIXED_SHIFT,
        lambda: _attention_fixed(mb, q3, k3, vt, *small, da_heads, v_dim, lambda_init, ta),
        lambda: attention_running_max().astype(BF16))

    gmean = _group_matrix(c_rw, RW_HEAD, 1.0 / RW_HEAD)
    w_out_bf = w_out.astype(BF16)
    h1, m, idx, gates, counts = _post(
        yf.reshape(n, c_rw), yb.reshape(n, c_rw), bonus.reshape(n, c_rw), g.reshape(n, c_rw), o_da.reshape(n, c_da), x2,
        row(rw_ln_g), row(rw_ln_b), gmean, w_out_bf[:c_rw], w_out_bf[c_rw:], row(norm_ffn_g),
        w_router.astype(F32), row(b_router), RW_HEAD * 1e-5, _tile(n, ROW_TILE))

    bm = MOE_BLOCK_ROWS
    counts = counts.reshape(N_EXPERTS).astype(jnp.int32)
    padded = (counts + bm - 1) // bm * bm
    pend = jnp.cumsum(padded)
    offs = (pend - padded).astype(jnp.int32)
    slot = _moe_rank(idx, offs.astype(F32).reshape(1, N_EXPERTS), _tile(n, ROW_TILE))
    n_blk = (n * TOP_K) // bm + N_EXPERTS
    blk_start = jnp.arange(n_blk, dtype=jnp.int32) * bm
    blk_e = jnp.minimum(jnp.sum(blk_start[:, None] >= pend[None, :], axis=1), N_EXPERTS - 1).astype(jnp.int32)
    n_used = (pend[-1] // bm).astype(jnp.int32).reshape(1)
    slot_flat = slot.reshape(n * TOP_K)
    xs = _dispatch(slot_flat, m, d // V7X_LANES, n_blk * bm, _tile(n, DISPATCH_TILE))
    ys = _moe_ffn(blk_e, n_used, xs, w1, b1, w2, b2, bm)
    out = _combine_ple(slot_flat, h1, gates, ys, p3.reshape(n, -1), w_ple.astype(BF16), row(ple_norm_g),
                       w_ple_gate.astype(BF16), _tile(n, COMBINE_TILE))
    return out.reshape(b, t, d)


def kernel(x, p, norm_mix_g, w_in, rw_mu, rw_w0, rw_w2, rw_a0, rw_a2, rw_g2, rw_k_k, rw_k_a, rw_r_k, rw_ln_g,
           rw_ln_b, da_q_norm_g, da_k_norm_g, da_lq1, da_lk1, da_lq2, da_lk2, da_subln_g, w_out, norm_ffn_g,
           w_router, b_router, w1, b1, w2, b2, w_ple, ple_norm_g, w_ple_gate):
    h = x.astype(F32)
    params = (norm_mix_g, w_in, rw_mu, rw_w0, rw_w2, rw_a0, rw_a2, rw_g2, rw_k_k, rw_k_a, rw_r_k, rw_ln_g, rw_ln_b,
              da_q_norm_g, da_k_norm_g, da_lq1, da_lk1, da_lq2, da_lk2, da_subln_g, w_out, norm_ffn_g,
              w_router, b_router, w1, b1, w2, b2, w_ple, ple_norm_g, w_ple_gate)
    for i in range(p.shape[0]):
        lambda_init = 0.8 - 0.6 * math.exp(-0.3 * i)
        h = _layer(h, p[i], lambda_init, *(a[i] for a in params))
    return h.astype(x.dtype)
```

```python
import functools
import math

import jax
import jax.numpy as jnp
import numpy as np
from jax import lax
from jax.experimental import pallas as pl
from jax.experimental.pallas import tpu as pltpu

F32 = jnp.float32
BF16 = jnp.bfloat16

V7X_LANES = 128
V7X_VMEM_BYTES = 64 * 1024 * 1024
VMEM_LIMIT = V7X_VMEM_BYTES - 12 * 1024 * 1024

ROW_TILE = 512
WIDE_TILE = 1024
DISPATCH_TILE = 2048
COMBINE_TILE = 256
SCAN_TILE = 512

NORM_EPS = 1e-6
RW_HEAD = 64
CHUNK = 64
HEAD_GROUP = 256
N_EXPERTS = 32
TOP_K = 4
MOE_BLOCK_ROWS = 512
SWIGLU_ALPHA = 1.702
SWIGLU_LIMIT = 7.0


def _cparams(sem, **kw):
    return pltpu.CompilerParams(dimension_semantics=sem, vmem_limit_bytes=VMEM_LIMIT, **kw)


def _dot(a, b):
    return jnp.dot(a, b, preferred_element_type=F32)


def _dot_hilo_lhs(g_bf16, x):
    hi, lo = _split2(x)
    return _dot(g_bf16, hi) + _dot(g_bf16, lo)


def _split2(x):
    hi = x.astype(BF16)
    return hi, (x - hi.astype(F32)).astype(BF16)


def _dot_hilo_rhs(x, g_bf16):
    hi, lo = _split2(x)
    out = _dot(jnp.concatenate([hi, lo], axis=0), g_bf16)
    return out[:x.shape[0]] + out[x.shape[0]:]


def _dot_hilo(a, b):
    ah, al = _split2(a)
    bh, bl = _split2(b)
    return _dot(ah, bh) + (_dot(ah, bl) + _dot(al, bh))


def _store_rows(ref, x2):
    rows, d = x2.shape
    c = d // V7X_LANES
    for j in range(c):
        ref[pl.ds(j, rows, stride=c), :] = x2[:, j * V7X_LANES:(j + 1) * V7X_LANES]


def _load_rows(ref, c):
    rows = ref.shape[0] // c
    return jnp.concatenate([ref[pl.ds(j, rows, stride=c), :] for j in range(c)], axis=1)


def _group_matrix(width, group, value):
    g = np.arange(width) // group
    return jnp.asarray(np.where(g[:, None] == g[None, :], value, 0.0), BF16)


def _inproj_kernel(x_ref, g_ref, w_ref, gq_ref, gk_ref, gm_ref, zrw_ref, q_ref, k_ref, vt_ref, *, rw_cols, c_da, scale):
    x = x_ref[...]
    ms = jnp.mean(x * x, axis=-1, keepdims=True)
    u = (x * lax.rsqrt(ms + NORM_EPS) * g_ref[...]).astype(BF16)
    z = _dot(u, w_ref[...])
    zrw_ref[...] = z[:, :rw_cols]
    zq = z[:, rw_cols:rw_cols + c_da]
    zk = z[:, rw_cols + c_da:rw_cols + 2 * c_da]
    zv = z[:, rw_cols + 2 * c_da:]
    tm = zq.shape[0]
    msqk = _dot(jnp.concatenate([zq * zq, zk * zk], axis=0).astype(BF16), gm_ref[...])
    q_ref[...] = (zq * lax.rsqrt(msqk[:tm] + NORM_EPS) * (gq_ref[...] * scale)).astype(BF16)
    k_ref[...] = (zk * lax.rsqrt(msqk[tm:] + NORM_EPS) * gk_ref[...]).astype(BF16)
    heads, vrows, _ = vt_ref.shape
    v_dim = c_da // heads
    pad_row = lax.broadcasted_iota(jnp.int32, (vrows - v_dim, tm), 0)
    pad = jnp.where(pad_row == 0, 1.0, 0.0).astype(BF16)
    for hd in range(heads):
        vt_ref[hd, 0:v_dim, :] = zv[:, hd * v_dim:(hd + 1) * v_dim].T.astype(BF16)
        vt_ref[hd, v_dim:, :] = pad


def _inproj(x2, g, w_bf, gq_t, gk_t, gm, rw_cols, c_da, scale, tm, b, heads, tb):
    n, d = x2.shape
    cols = w_bf.shape[1]
    t = n // b
    tps = t // tm
    per = tb // tm
    vrows = c_da // heads + ATTN_V_PAD
    full = lambda i: (0, 0)
    row = lambda i: (i, 0)
    return pl.pallas_call(
        functools.partial(_inproj_kernel, rw_cols=rw_cols, c_da=c_da, scale=scale),
        grid=(n // tm,),
        in_specs=[
            pl.BlockSpec((tm, d), row),
            pl.BlockSpec((1, d), full),
            pl.BlockSpec((d, cols), full),
            pl.BlockSpec((1, c_da), full),
            pl.BlockSpec((1, c_da), full),
            pl.BlockSpec((c_da, c_da), full),
        ],
        out_specs=[
            pl.BlockSpec((tm, rw_cols), row),
            pl.BlockSpec((tm, c_da), row),
            pl.BlockSpec((tm, c_da), row),
            pl.BlockSpec((None, heads, None, vrows, tm),
                         lambda i: (i // tps, 0, (i % tps) // per, 0, (i % tps) % per)),
        ],
        out_shape=[
            jax.ShapeDtypeStruct((n, rw_cols), F32),
            jax.ShapeDtypeStruct((n, c_da), BF16),
            jax.ShapeDtypeStruct((n, c_da), BF16),
            jax.ShapeDtypeStruct((b, heads, t // tb, vrows, tb), BF16),
        ],
        compiler_params=_cparams(("parallel",)),
        name="inproj",
    )(x2, g, w_bf, gq_t, gk_t, gm)


def _rwkv_prep_kernel(z_ref, zp_ref, zn_ref, mu_ref, w2_ref, w0_ref, a2_ref, a0_ref, g2_ref, kk_ref, ka_ref,
                      rk_ref, gsum_ref,
                      r_ref, v_ref, nkk_ref, lw_ref, kd_ref, bd_ref, g_ref, bonus_ref, *, c_rw, lora):
    i = pl.program_id(1)
    nt = pl.num_programs(1)
    z = z_ref[...]
    tt = z.shape[0]
    rows = lax.broadcasted_iota(jnp.int32, z.shape, 0)
    prev_row = jnp.where(i == 0, 0.0, zp_ref[7:8, :])
    next_row = jnp.where(i == nt - 1, 0.0, zn_ref[0:1, :])
    zprev = jnp.where(rows == 0, prev_row, pltpu.roll(z, 1, 0))
    znext = jnp.where(rows == tt - 1, next_row, pltpu.roll(z, tt - 1, 0))
    z = z + (0.5 * (zprev + znext) - z) * mu_ref[...]

    r = z[:, 0:c_rw]
    k = z[:, c_rw:2 * c_rw]
    v = z[:, 2 * c_rw:3 * c_rw]
    o = 3 * c_rw
    wd = z[:, o:o + 2 * lora]
    ad = z[:, o + 2 * lora:o + 4 * lora]
    gd = z[:, o + 4 * lora:]

    xw = w0_ref[...] + _dot(jnp.tanh(wd).astype(BF16), w2_ref[...])
    xa = a0_ref[...] + _dot(ad.astype(BF16), a2_ref[...])
    lw = -math.exp(-0.5) * jax.nn.sigmoid(xw)
    rate = jax.nn.sigmoid(xa)
    g = _dot(jax.nn.sigmoid(gd).astype(BF16), g2_ref[...])

    gsum = gsum_ref[...]
    kkr = k * kk_ref[...]
    ka = ka_ref[...]
    kd0 = k * (1.0 + (rate[:, :c_rw] - 1.0) * ka)
    kd1 = k * (1.0 + (rate[:, c_rw:] - 1.0) * ka)
    sums = _dot_hilo_rhs(jnp.concatenate([kkr * kkr, r * (0.5 * (kd0 + kd1)) * rk_ref[...]], axis=0), gsum)
    kk = kkr * lax.rsqrt(sums[:tt] + 1e-12)
    bonus = sums[tt:] * v

    r_ref[...] = r.astype(BF16)
    v_ref[...] = v.astype(BF16)
    nkk_ref[...] = (-kk).astype(BF16)
    lw_ref[0] = lw[:, :c_rw]
    lw_ref[1] = lw[:, c_rw:]
    kd_ref[0] = kd0.astype(BF16)
    kd_ref[1] = kd1.astype(BF16)
    bd_ref[0] = (kk * rate[:, :c_rw]).astype(BF16)
    bd_ref[1] = (kk * rate[:, c_rw:]).astype(BF16)
    g_ref[...] = g.astype(BF16)
    bonus_ref[...] = bonus.astype(BF16)


def _rwkv_prep(zrw3, mu, w2bd, w0f, a2bd, a0f, g2_bf, k_k, k_a, r_kf, gsum, c_rw, lora, tt):
    b, t, cols = zrw3.shape
    nt = t // tt
    hb = tt // 8
    nhb = t // 8
    full = lambda bi, i: (0, 0)
    tile = lambda bi, i: (bi, i, 0)
    dtile = lambda bi, i: (0, bi, i, 0)
    one = jax.ShapeDtypeStruct((b, t, c_rw), F32)
    two = jax.ShapeDtypeStruct((2, b, t, c_rw), F32)
    one16 = jax.ShapeDtypeStruct((b, t, c_rw), BF16)
    two16 = jax.ShapeDtypeStruct((2, b, t, c_rw), BF16)
    return pl.pallas_call(
        functools.partial(_rwkv_prep_kernel, c_rw=c_rw, lora=lora),
        grid=(b, nt),
        in_specs=[
            pl.BlockSpec((None, tt, cols), tile),
            pl.BlockSpec((None, 8, cols), lambda bi, i: (bi, jnp.maximum(i * hb - 1, 0), 0)),
            pl.BlockSpec((None, 8, cols), lambda bi, i: (bi, jnp.minimum((i + 1) * hb, nhb - 1), 0)),
            pl.BlockSpec((1, cols), full),
            pl.BlockSpec(w2bd.shape, full),
            pl.BlockSpec((1, 2 * c_rw), full),
            pl.BlockSpec(a2bd.shape, full),
            pl.BlockSpec((1, 2 * c_rw), full),
            pl.BlockSpec(g2_bf.shape, full),
            pl.BlockSpec((1, c_rw), full),
            pl.BlockSpec((1, c_rw), full),
            pl.BlockSpec((1, c_rw), full),
            pl.BlockSpec((c_rw, c_rw), full),
        ],
        out_specs=[
            pl.BlockSpec((None, tt, c_rw), tile),
            pl.BlockSpec((None, tt, c_rw), tile),
            pl.BlockSpec((None, tt, c_rw), tile),
            pl.BlockSpec((2, None, tt, c_rw), dtile),
            pl.BlockSpec((2, None, tt, c_rw), dtile),
            pl.BlockSpec((2, None, tt, c_rw), dtile),
            pl.BlockSpec((None, tt, c_rw), tile),
            pl.BlockSpec((None, tt, c_rw), tile),
        ],
        out_shape=[one16, one16, one16, two, two16, two16, one16, one16],
        compiler_params=_cparams(("parallel", "parallel")),
        name="rwkv_prep",
    )(zrw3, zrw3, zrw3, mu, w2bd, w0f, a2bd, a0f, g2_bf, k_k, k_a, r_kf, gsum)


M_STRICT, M_INCL, M_EYE, M_PAIR, M_LEVEL = 0, 1, 2, 3, 4
N_LEVELS = 5
SCAN_SEQS = 2


def _scan_tables(w):
    c = CHUNK
    masks = []
    mcums = []
    for d in range(2):
        t = np.broadcast_to(np.arange(c)[:, None], (c, w))
        s = np.broadcast_to((np.arange(w) % RW_HEAD)[None, :], (c, w))
        tt = np.broadcast_to(np.arange(c)[:, None], (c, c))
        ss = np.broadcast_to(np.arange(c)[None, :], (c, c))
        if d == 1:
            t, s, tt, ss = c - 1 - t, c - 1 - s, c - 1 - tt, c - 1 - ss
        rows = [s < t, s <= t, s == t, (t // 2 == s // 2) & (t > s)]
        sz = 2
        while sz < c:
            rows.append((t // (2 * sz) == s // (2 * sz)) & ((t // sz) % 2 == 1) & ((s // sz) % 2 == 0))
            sz *= 2
        assert len(rows) == M_LEVEL + N_LEVELS
        masks.append(np.stack(rows).astype(np.float32))
        mcums.append((ss <= tt).astype(np.float32))
    return jnp.asarray(np.stack(masks), F32), jnp.asarray(np.stack(mcums), BF16)


def _block_diag(y, bd16):
    g = y.shape[1] // RW_HEAD
    return jnp.concatenate([y.astype(BF16)] * g, axis=0) * bd16


def _bmm(x, y, bd16):
    outs = []
    for s in range(0, y.shape[1], HEAD_GROUP):
        outs.append(_dot(x[:, s:s + HEAD_GROUP].astype(BF16), _block_diag(y[:, s:s + HEAD_GROUP], bd16)))
    return jnp.concatenate(outs, axis=1)


def _bmm_nt(x, y, bd16):
    outs = []
    for s in range(0, y.shape[1], HEAD_GROUP):
        bd = _block_diag(y[:, s:s + HEAD_GROUP], bd16)
        outs.append(lax.dot_general(x[:, s:s + HEAD_GROUP].astype(BF16), bd, (((1,), (1,)), ((), ())),
                                    preferred_element_type=F32))
    return jnp.concatenate(outs, axis=1)


def _bmm_tn(x, y, bd32):
    outs = []
    for s in range(0, y.shape[1], HEAD_GROUP):
        xt = x[:, s:s + HEAD_GROUP].T.astype(BF16)
        full = _dot(xt, y[:, s:s + HEAD_GROUP].astype(BF16)) * bd32
        o = full[0:RW_HEAD]
        for j in range(1, HEAD_GROUP // RW_HEAD):
            o = o + full[j * RW_HEAD:(j + 1) * RW_HEAD]
        outs.append(o)
    return jnp.concatenate(outs, axis=1)


def _scan_chunks(r, lw, k, v, a, b, state, masks, mcum, bd16, bd32):
    c = r[0].shape[0]

    def each(f, *cols):
        return [f(*xs) for xs in zip(*cols)]

    cum = each(_dot_hilo_lhs, mcum, lw)
    tot = each(lambda x: jnp.sum(x, axis=0, keepdims=True), lw)
    e_neg = each(lambda x: jnp.exp(-x), cum)
    e_tail = each(lambda t_, x: jnp.exp(t_ - x), tot, cum)
    rt = each(lambda x, cm: x * jnp.exp(cm), r, cum)
    at = each(lambda x, cm, l_: x * jnp.exp(cm - l_), a, cum, lw)
    bt = each(jnp.multiply, b, e_neg)
    kt = each(jnp.multiply, k, e_neg)
    bh = each(jnp.multiply, b, e_tail)
    kh = each(jnp.multiply, k, e_tail)

    ar = each(lambda x, y_: jnp.concatenate([x, y_], axis=0), at, rt)
    pb = each(lambda x, y_: _bmm_nt(x, y_, bd16), ar, bt)
    pk = each(lambda x, y_: _bmm_nt(x, y_, bd16), ar, kt)
    a_ab = each(lambda x, m: x[:c] * m[M_STRICT], pb, masks)
    a_rb = each(lambda x, m: x[c:] * m[M_INCL], pb, masks)
    a_ak = each(lambda x, m: x[:c] * m[M_STRICT], pk, masks)
    a_rk = each(lambda x, m: x[c:] * m[M_INCL], pk, masks)

    tinv = each(lambda x, m: m[M_EYE] + x * m[M_PAIR], a_ab, masks)
    for lvl in range(N_LEVELS):
        x = each(lambda aa, m, ti: _bmm(aa * m[M_LEVEL + lvl], ti, bd16), a_ab, masks, tinv)
        tinv = each(lambda ti, x_: ti + _bmm(ti, x_, bd16), tinv, x)

    bmm = lambda p, q: _bmm(p, q, bd16)
    stack = lambda p, q: jnp.concatenate([p, q], axis=0)
    att = each(bmm, tinv, at)
    akv = each(lambda p, p2, q: _bmm(stack(p, p2), q, bd16), a_ak, a_rk, v)
    vt = each(lambda ti, x_: _bmm(ti, x_[:c], bd16), tinv, akv)
    rh = each(lambda x, p, q: x + _bmm(p, q, bd16), rt, a_rb, att)
    y_in = each(lambda p, q, x_: _bmm(p, q, bd16) + x_[c:], a_rb, vt, akv)
    mc = each(lambda m, t_, p, q: m[M_EYE] * jnp.exp(t_) + _bmm_tn(p, q, bd32), masks, tot, bh, att)
    nc = each(lambda p, p2, q, q2: _bmm_tn(stack(p, p2), stack(q, q2), bd32), bh, kh, vt, v)
    onst = each(lambda p, p2, q: _bmm(stack(p, p2), q, bd16), rh, mc, state)
    y = each(lambda x_, yi: x_[:c] + yi, onst, y_in)
    new_state = each(lambda x_, n_: x_[c:] + n_, onst, nc)
    return y, new_state


def _rwkv_scan_kernel(rf_ref, vf_ref, af_ref, rb_ref, vb_ref, ab_ref, lwf_ref, kf_ref, bf_ref, lwb_ref, kb_ref,
                      bb_ref, mask_ref, mcum_ref, bd16_ref, bd32_ref, yf_ref, yb_ref, state_ref, *, n_chunks):
    @pl.when(pl.program_id(1) == 0)
    def _():
        state_ref[...] = jnp.zeros_like(state_ref)

    dirs = ((0, rf_ref, lwf_ref, kf_ref, vf_ref, af_ref, bf_ref, yf_ref),
            (1, rb_ref, lwb_ref, kb_ref, vb_ref, ab_ref, bb_ref, yb_ref))
    n_seq = rf_ref.shape[0]

    def body(j, carry):
        cols = [[] for _ in range(9)]
        where = []
        for d, r_ref, lw_ref, k_ref, v_ref, a_ref, b_ref, y_ref in dirs:
            cj = j if d == 0 else n_chunks - 1 - j
            sl = pl.ds(pl.multiple_of(cj * CHUNK, CHUNK), CHUNK)
            for g in range(n_seq):
                vals = (r_ref[g, sl, :], lw_ref[g, sl, :], k_ref[g, sl, :], v_ref[g, sl, :], a_ref[g, sl, :],
                        b_ref[g, sl, :], state_ref[d, g], mask_ref.at[d], mcum_ref[d])
                for col, val in zip(cols, vals):
                    col.append(val)
                where.append((y_ref, d, g, sl))
        ys, sts = _scan_chunks(*cols, bd16_ref[...], bd32_ref[...])
        for (y_ref, d, g, sl), y, st in zip(where, ys, sts):
            y_ref[g, sl, :] = y.astype(y_ref.dtype)
            state_ref[d, g] = st
        return carry

    lax.fori_loop(0, n_chunks, body, 0)


def _rwkv_scan(r, v, nkk, lw, kd, bd, tb):
    b, t, w = r.shape
    nb = t // tb
    g = SCAN_SEQS if b % SCAN_SEQS == 0 else 1
    masks, mcum = _scan_tables(w)
    bd32 = _group_matrix(HEAD_GROUP, RW_HEAD, 1.0).astype(F32)
    bd16 = bd32.astype(BF16)
    fwd = lambda bi, i: (bi, i, 0)
    bwd = lambda bi, i: (bi, nb - 1 - i, 0)
    fwd_d = lambda bi, i: (0, bi, i, 0)
    bwd_d = lambda bi, i: (1, bi, nb - 1 - i, 0)
    const = lambda nd: (lambda bi, i: (0,) * nd)
    blk = (g, tb, w)
    dblk = (None, g, tb, w)
    out = jax.ShapeDtypeStruct((b, t, w), BF16)
    return pl.pallas_call(
        functools.partial(_rwkv_scan_kernel, n_chunks=tb // CHUNK),
        grid=(b // g, nb),
        in_specs=[
            pl.BlockSpec(blk, fwd), pl.BlockSpec(blk, fwd), pl.BlockSpec(blk, fwd),
            pl.BlockSpec(blk, bwd), pl.BlockSpec(blk, bwd), pl.BlockSpec(blk, bwd),
            pl.BlockSpec(dblk, fwd_d), pl.BlockSpec(dblk, fwd_d), pl.BlockSpec(dblk, fwd_d),
            pl.BlockSpec(dblk, bwd_d), pl.BlockSpec(dblk, bwd_d), pl.BlockSpec(dblk, bwd_d),
            pl.BlockSpec(masks.shape, const(4)),
            pl.BlockSpec(mcum.shape, const(3)),
            pl.BlockSpec(bd16.shape, const(2)),
            pl.BlockSpec(bd32.shape, const(2)),
        ],
        out_specs=[pl.BlockSpec(blk, fwd), pl.BlockSpec(blk, bwd)],
        out_shape=[out, out],
        scratch_shapes=[pltpu.VMEM((2, g, CHUNK, w), F32)],
        compiler_params=_cparams(("parallel", "arbitrary")),
        name="rwkv_scan",
    )(r, v, nkk, r, v, nkk, lw, kd, bd, lw, kd, bd, masks, mcum, bd16, bd32)


def _attn_kernel(slope_ref, q_ref, k_ref, v_ref, lq1_ref, lk1_ref, lq2_ref, lk2_ref, sg_ref, o_ref,
                 m_ref, l_ref, acc_ref, *, qk_dim, lambda_init):
    h = pl.program_id(1)
    i = pl.program_id(2)
    j = pl.program_id(3)
    nk = pl.num_programs(3)
    tq = q_ref.shape[0]
    tk = k_ref.shape[0]

    @pl.when(j == 0)
    def _():
        m_ref[...] = jnp.full(m_ref.shape, -jnp.inf, F32)
        l_ref[...] = jnp.zeros_like(l_ref)
        acc_ref[...] = jnp.zeros_like(acc_ref)

    q = q_ref[...]
    k = k_ref[...]
    v = v_ref[...]
    lane = lax.broadcasted_iota(jnp.int32, q.shape, 1)
    zero = jnp.zeros_like(q)
    qi = lax.broadcasted_iota(jnp.int32, (tq, tk), 0) + i * tq
    kj = lax.broadcasted_iota(jnp.int32, (tq, tk), 1) + j * tk
    bias = slope_ref[h] * jnp.abs(qi - kj).astype(F32)
    for c in range(2):
        qc = jnp.where((lane < qk_dim) == (c == 0), q, zero)
        s = lax.dot_general(qc, k, (((1,), (1,)), ((), ())), preferred_element_type=F32) - bias
        m_old = m_ref[c]
        m_new = jnp.maximum(m_old, jnp.max(s, axis=-1, keepdims=True))
        alpha = jnp.exp(m_old - m_new)
        p = jnp.exp(s - m_new)
        l_ref[c] = alpha * l_ref[c] + jnp.sum(p, axis=-1, keepdims=True)
        acc_ref[c] = alpha * acc_ref[c] + _dot(p.astype(BF16), v)
        m_ref[c] = m_new

    @pl.when(j == nk - 1)
    def _():
        lam = (jnp.exp(jnp.sum(lq1_ref[...] * lk1_ref[...], axis=-1, keepdims=True))
               - jnp.exp(jnp.sum(lq2_ref[...] * lk2_ref[...], axis=-1, keepdims=True)) + lambda_init)
        o = acc_ref[0] / l_ref[0] - lam * (acc_ref[1] / l_ref[1])
        ms = jnp.mean(o * o, axis=-1, keepdims=True)
        o_ref[...] = o * lax.rsqrt(ms + NORM_EPS) * (sg_ref[...] * (1.0 - lambda_init))


def _attention(slopes, q3, k3, v3, lq1, lk1, lq2, lk2, subln_g, heads, v_dim, lambda_init, tq, tk):
    b, t, _ = q3.shape
    const = lambda bi, h, i, j, s: (0, 0)
    grid_spec = pltpu.PrefetchScalarGridSpec(
        num_scalar_prefetch=1,
        grid=(b, heads, t // tq, t // tk),
        in_specs=[
            pl.BlockSpec((None, tq, v_dim), lambda bi, h, i, j, s: (bi, i, h)),
            pl.BlockSpec((None, tk, v_dim), lambda bi, h, i, j, s: (bi, j, h)),
            pl.BlockSpec((None, tk, v_dim), lambda bi, h, i, j, s: (bi, j, h)),
            pl.BlockSpec(lq1.shape, const),
            pl.BlockSpec(lk1.shape, const),
            pl.BlockSpec(lq2.shape, const),
            pl.BlockSpec(lk2.shape, const),
            pl.BlockSpec(subln_g.shape, const),
        ],
        out_specs=pl.BlockSpec((None, tq, v_dim), lambda bi, h, i, j, s: (bi, i, h)),
        scratch_shapes=[
            pltpu.VMEM((2, tq, 1), F32),
            pltpu.VMEM((2, tq, 1), F32),
            pltpu.VMEM((2, tq, v_dim), F32),
        ],
    )
    return pl.pallas_call(
        functools.partial(_attn_kernel, qk_dim=v_dim // 2, lambda_init=lambda_init),
        grid_spec=grid_spec,
        out_shape=jax.ShapeDtypeStruct((b, t, heads * v_dim), F32),
        compiler_params=_cparams(("parallel", "parallel", "parallel", "arbitrary")),
        name="attn",
    )(slopes, q3, k3, v3, lq1, lk1, lq2, lk2, subln_g)


EXP_ZERO_ARG = 104.0
MAX_FIXED_SHIFT = 40.0
N_AUG = 6
ATTN_BLOCK = 512
ATTN_GROUP_SIZES = (8, 4, 2, 1)
ATTN_KEY_SPLIT = 1
ATTN_V_PAD = 16


def _attn_fixed_kernel(w_ref, slope_ref, mb_ref, q_ref, k_ref, vt_ref, dbias_ref, qaug_ref, kaug_ref,
                       lq1_ref, lk1_ref, lq2_ref, lk2_ref, sg_ref, o_ref, acc_ref, *, qk_dim, lambda_init):
    h = pl.program_id(1)
    i = pl.program_id(2)
    tb = q_ref.shape[0]
    nk = k_ref.shape[0] // tb
    v_dim = 2 * qk_dim
    slope = slope_ref[h]
    mb = mb_ref[0]
    w = w_ref[h]

    q = q_ref[...]
    lane = lax.broadcasted_iota(jnp.int32, q.shape, 1)
    lane_row = lax.broadcasted_iota(jnp.int32, (1, q.shape[1]), 1)
    is_c0 = lane < qk_dim
    datas = (is_c0, jnp.logical_not(is_c0))

    def blocks(items, qvar, kvar, diag):
        sub = tb // ATTN_KEY_SPLIT
        qas = [jnp.where(datas[c], q, qaug_ref[qvar, c]) for c in range(2)]
        chains = []
        for n, (j, delta) in enumerate(items):
            kblk = k_ref[pl.ds(pl.multiple_of(j * tb, tb), tb), :]
            vtblk = vt_ref[j]
            for c in range(2):
                base = qk_dim * (1 - c)
                dyn = jnp.where(lane_row == base + 4, -mb,
                                jnp.where(lane_row == base + 5, -(slope * tb) * delta, 0.0)).astype(BF16)
                ka = jnp.where(datas[c], kblk, kaug_ref[kvar, c] + dyn)
                for u in range(ATTN_KEY_SPLIT):
                    rows = slice(u * sub, (u + 1) * sub)
                    chains.append((c, u, qas[c], ka[rows], vtblk[:, rows]))
        sts = [lax.dot_general(ka, qa, (((1,), (1,)), ((), ())), preferred_element_type=F32)
               for _, _, qa, ka, _ in chains]
        if diag:
            sts = [st - dbias_ref[u * sub:(u + 1) * sub, :] for st, (_, u, _, _, _) in zip(sts, chains)]
        pts = [jnp.exp(st).astype(BF16) for st in sts]
        for c in range(2):
            upd = None
            for pt, (cc, _, _, _, vsub) in zip(pts, chains):
                if cc == c:
                    part = _dot(vsub, pt)
                    upd = part if upd is None else upd + part
            if diag:
                acc_ref[c] = upd
            else:
                acc_ref[c] += upd

    def span(first, count, qvar, kvar, dist):
        for per in ATTN_GROUP_SIZES:
            def group(g, carry, first=first, per=per):
                j0 = first + g * per
                blocks([(j0 + n, dist(j0 + n)) for n in range(per)], qvar, kvar, False)
                return carry

            groups = count // per
            lax.fori_loop(0, groups, group, 0)
            first = first + groups * per
            count = count - groups * per

    lo = jnp.maximum(i - w, 0)
    hi = jnp.minimum(i + w, nk - 1)
    blocks([(i, jnp.zeros((), F32))], 2, 0, True)
    span(lo, i - lo, 0, 0, lambda j: (i - j).astype(F32))
    span(i + 1, hi - i, 1, 1, lambda j: (j - i).astype(F32))

    lam = (jnp.exp(jnp.sum(lq1_ref[...] * lk1_ref[...], axis=-1, keepdims=True))
           - jnp.exp(jnp.sum(lq2_ref[...] * lk2_ref[...], axis=-1, keepdims=True)) + lambda_init)
    a0 = acc_ref[0]
    a1 = acc_ref[1]
    inv0 = 1.0 / a0[v_dim:v_dim + 1]
    inv1 = lam / a1[v_dim:v_dim + 1]
    ot = a0[:v_dim] * inv0 - a1[:v_dim] * inv1
    ms = jnp.mean(ot * ot, axis=0, keepdims=True)
    o_ref[...] = (ot * lax.rsqrt(ms + NORM_EPS) * (sg_ref[...] * (1.0 - lambda_init))).T.astype(o_ref.dtype)


def _attn_aug_tables(slopes, tb, qk_dim):
    heads = len(slopes)
    lanes = 2 * qk_dim
    pos = np.arange(tb)
    qaug = np.zeros((heads, 3, 2, tb, lanes), np.float32)
    kaug = np.zeros((heads, 2, 2, tb, lanes), np.float32)
    for hd, sl in enumerate(slopes):
        for var in range(3):
            rr = pos if var != 1 else tb - 1 - pos
            hi = (rr // 16) * (16.0 * sl)
            lo = (rr % 16) * sl
            for c in range(2):
                base = qk_dim * (1 - c)
                if var == 2:
                    qaug[hd, var, c, :, base + 4] = 1.0
                    continue
                qaug[hd, var, c, :, base + 0] = -hi
                qaug[hd, var, c, :, base + 1] = -lo
                qaug[hd, var, c, :, base + 2:base + N_AUG] = 1.0
                kaug[hd, var, c, :, base + 0:base + 2] = 1.0
                kaug[hd, var, c, :, base + 2] = hi
                kaug[hd, var, c, :, base + 3] = lo
    return jnp.asarray(qaug, BF16), jnp.asarray(kaug, BF16)


def _attention_fixed(mb, q3, k3, vt, lq1, lk1, lq2, lk2, subln_g, heads, v_dim, lambda_init, tb):
    b, t, _ = q3.shape
    qk_dim = v_dim // 2
    slopes = [2.0 ** (-8.0 * (hd + 1) / heads) for hd in range(heads)]
    widths = [int(math.ceil((EXP_ZERO_ARG / s - 1.0) / tb)) for s in slopes]
    qaug, kaug = _attn_aug_tables(slopes, tb, qk_dim)
    pos = np.arange(tb)
    dist = np.abs(pos[:, None] - pos[None, :]).astype(np.float32)
    dbias = jnp.asarray(np.stack([dist * s for s in slopes]), F32)
    nb = t // tb
    vrows = v_dim + ATTN_V_PAD
    assert vt.shape == (b, heads, nb, vrows, tb)
    const = lambda bi, h, i, *_: (0, 0)
    grid_spec = pltpu.PrefetchScalarGridSpec(
        num_scalar_prefetch=3,
        grid=(b, heads, t // tb),
        in_specs=[
            pl.BlockSpec((None, tb, v_dim), lambda bi, h, i, *_: (bi, i, h)),
            pl.BlockSpec((None, t, v_dim), lambda bi, h, i, *_: (bi, 0, h)),
            pl.BlockSpec((None, None, nb, vrows, tb), lambda bi, h, i, *_: (bi, h, 0, 0, 0)),
            pl.BlockSpec((None, tb, tb), lambda bi, h, i, *_: (h, 0, 0)),
            pl.BlockSpec((None, 3, 2, tb, v_dim), lambda bi, h, i, *_: (h, 0, 0, 0, 0)),
            pl.BlockSpec((None, 2, 2, tb, v_dim), lambda bi, h, i, *_: (h, 0, 0, 0, 0)),
            pl.BlockSpec(lq1.shape, const),
            pl.BlockSpec(lk1.shape, const),
            pl.BlockSpec(lq2.shape, const),
            pl.BlockSpec(lk2.shape, const),
            pl.BlockSpec((v_dim, 1), const),
        ],
        out_specs=pl.BlockSpec((None, tb, v_dim), lambda bi, h, i, *_: (bi, i, h)),
        scratch_shapes=[pltpu.VMEM((2, vrows, tb), F32)],
    )
    return pl.pallas_call(
        functools.partial(_attn_fixed_kernel, qk_dim=qk_dim, lambda_init=lambda_init),
        grid_spec=grid_spec,
        out_shape=jax.ShapeDtypeStruct((b, t, heads * v_dim), BF16),
        compiler_params=_cparams(("parallel", "parallel", "parallel")),
        name="attn_fixed",
    )(jnp.asarray(widths, jnp.int32), jnp.asarray(slopes, F32), mb.reshape(1).astype(F32),
      q3, k3, vt, dbias, qaug, kaug, lq1, lk1, lq2, lk2, subln_g.reshape(v_dim, 1))


def _post_kernel(yf_ref, yb_ref, bonus_ref, g_ref, oda_ref, x_ref, lng_ref, lnb_ref, gmean_ref, wo_rw_ref, wo_da_ref,
                 gffn_ref, wr_ref, br_ref, h_ref, m_ref, idx_ref, gate_ref, cnt_ref, *, gn_eps):
    @pl.when(pl.program_id(0) == 0)
    def _():
        cnt_ref[...] = jnp.zeros_like(cnt_ref)

    gmean = gmean_ref[...]
    y = yf_ref[...].astype(F32) + yb_ref[...].astype(F32)
    mean = _dot_hilo_rhs(y, gmean)
    yc = y - mean
    var = _dot_hilo_rhs(yc * yc, gmean)
    yn = yc * lax.rsqrt(var + gn_eps) * lng_ref[...] + lnb_ref[...]
    o_rw = (yn + bonus_ref[...]) * g_ref[...]
    h = x_ref[...] + _dot(o_rw.astype(BF16), wo_rw_ref[...]) + _dot(oda_ref[...].astype(BF16), wo_da_ref[...])
    h_ref[...] = h
    ms = jnp.mean(h * h, axis=-1, keepdims=True)
    m = h * lax.rsqrt(ms + NORM_EPS) * gffn_ref[...]
    _store_rows(m_ref, m)

    logits = _dot_hilo(m, wr_ref[...]) + br_ref[...]
    ne = logits.shape[1]
    lane = lax.broadcasted_iota(jnp.int32, logits.shape, 1).astype(F32)
    kcol = lax.broadcasted_iota(jnp.int32, (logits.shape[0], TOP_K), 1)
    idx_all = jnp.zeros((logits.shape[0], TOP_K), F32)
    val_all = jnp.zeros((logits.shape[0], TOP_K), F32)
    work = logits
    chosen = jnp.zeros(logits.shape, F32)
    for kk in range(TOP_K):
        mx = jnp.max(work, axis=-1, keepdims=True)
        sel = jnp.min(jnp.where(work == mx, lane, float(ne)), axis=-1, keepdims=True)
        idx_all = jnp.where(kcol == kk, sel, idx_all)
        val_all = jnp.where(kcol == kk, mx, val_all)
        hit = lane == sel
        chosen = jnp.where(hit, 1.0, chosen)
        work = jnp.where(hit, -jnp.inf, work)
    e = jnp.exp(val_all - jnp.max(val_all, axis=-1, keepdims=True))
    idx_ref[...] = idx_all.astype(jnp.int32)
    gate_ref[...] = e / jnp.sum(e, axis=-1, keepdims=True)
    cnt_ref[...] += jnp.sum(chosen, axis=0, keepdims=True)


def _post(yf, yb, bonus, g, oda, x2, lng, lnb, gmean, wo_rw, wo_da, gffn, wr, br, gn_eps, tm):
    n, c = yf.shape
    d = x2.shape[1]
    ne = wr.shape[1]
    full = lambda i: (0, 0)
    row = lambda i: (i, 0)
    return pl.pallas_call(
        functools.partial(_post_kernel, gn_eps=gn_eps),
        grid=(n // tm,),
        in_specs=[
            pl.BlockSpec((tm, c), row),
            pl.BlockSpec((tm, c), row),
            pl.BlockSpec((tm, c), row),
            pl.BlockSpec((tm, c), row),
            pl.BlockSpec((tm, c), row),
            pl.BlockSpec((tm, d), row),
            pl.BlockSpec((1, c), full),
            pl.BlockSpec((1, c), full),
            pl.BlockSpec((c, c), full),
            pl.BlockSpec((c, d), full),
            pl.BlockSpec((c, d), full),
            pl.BlockSpec((1, d), full),
            pl.BlockSpec((d, ne), full),
            pl.BlockSpec((1, ne), full),
        ],
        out_specs=[
            pl.BlockSpec((tm, d), row),
            pl.BlockSpec((tm * d // V7X_LANES, V7X_LANES), row),
            pl.BlockSpec((tm, TOP_K), row),
            pl.BlockSpec((tm, TOP_K), row),
            pl.BlockSpec((1, ne), full),
        ],
        out_shape=[
            jax.ShapeDtypeStruct((n, d), F32),
            jax.ShapeDtypeStruct((n * d // V7X_LANES, V7X_LANES), F32),
            jax.ShapeDtypeStruct((n, TOP_K), jnp.int32),
            jax.ShapeDtypeStruct((n, TOP_K), F32),
            jax.ShapeDtypeStruct((1, ne), F32),
        ],
        compiler_params=_cparams(("arbitrary",)),
        name="post",
    )(yf, yb, bonus, g, oda, x2, lng, lnb, gmean, wo_rw, wo_da, gffn, wr, br)


def _rank_kernel(idx_ref, ltri_ref, offs_ref, slot_ref, carry_ref):
    @pl.when(pl.program_id(0) == 0)
    def _():
        carry_ref[...] = offs_ref[...]

    idx = idx_ref[...]
    tr = idx.shape[0]
    lane = lax.broadcasted_iota(jnp.int32, (tr, N_EXPERTS), 1)
    hots = [jnp.where(lane == idx[:, kk:kk + 1], 1.0, 0.0) for kk in range(TOP_K)]
    mask = hots[0]
    for kk in range(1, TOP_K):
        mask = mask + hots[kk]
    before = _dot(ltri_ref[...], mask.astype(BF16)) + carry_ref[...]
    kcol = lax.broadcasted_iota(jnp.int32, (tr, TOP_K), 1)
    slot = jnp.zeros((tr, TOP_K), F32)
    for kk in range(TOP_K):
        rk = jnp.sum(hots[kk] * before, axis=-1, keepdims=True)
        slot = jnp.where(kcol == kk, rk, slot)
    slot_ref[...] = slot.astype(jnp.int32)
    carry_ref[...] = carry_ref[...] + jnp.sum(mask, axis=0, keepdims=True)


def _moe_rank(idx, offs_row, tr):
    n = idx.shape[0]
    ltri = jnp.asarray(np.tril(np.ones((tr, tr), np.float32), -1), BF16)
    return pl.pallas_call(
        _rank_kernel,
        grid=(n // tr,),
        in_specs=[pl.BlockSpec((tr, TOP_K), lambda i: (i, 0)), pl.BlockSpec((tr, tr), lambda i: (0, 0)),
                  pl.BlockSpec((1, N_EXPERTS), lambda i: (0, 0))],
        out_specs=pl.BlockSpec((tr, TOP_K), lambda i: (i, 0)),
        out_shape=jax.ShapeDtypeStruct((n, TOP_K), jnp.int32),
        scratch_shapes=[pltpu.VMEM((1, N_EXPERTS), F32)],
        compiler_params=_cparams(("arbitrary",)),
        name="moe_rank",
    )(idx, ltri, offs_row)


ROW_UNROLL = 8


def _row(r, c):
    return pl.ds(pl.multiple_of(r * c, c), c)


def _dispatch_kernel(slot_ref, m_ref, xs_ref, sem, *, c):
    td = m_ref.shape[0] // c

    def issue(g, carry):
        for u in range(ROW_UNROLL):
            t = g * ROW_UNROLL + u
            for kk in range(TOP_K):
                dest = slot_ref[t * TOP_K + kk]
                pltpu.make_async_copy(m_ref.at[_row(t, c), :], xs_ref.at[_row(dest, c), :], sem).start(
                    priority=kk % 2)
        return carry

    lax.fori_loop(0, td // ROW_UNROLL, issue, 0)
    for kk in range(TOP_K):
        pltpu.make_async_copy(m_ref, xs_ref.at[pl.ds(0, td * c), :], sem).wait()


def _dispatch(slot_flat, mrows, c, p_rows, td):
    n = mrows.shape[0] // c
    assert td % ROW_UNROLL == 0
    return pl.pallas_call(
        functools.partial(_dispatch_kernel, c=c),
        grid=(n // td,),
        in_specs=[
            pl.BlockSpec((td * TOP_K,), lambda i: (i,), memory_space=pltpu.SMEM),
            pl.BlockSpec((td * c, V7X_LANES), lambda i: (i, 0)),
        ],
        out_specs=pl.BlockSpec(memory_space=pl.ANY),
        scratch_shapes=[pltpu.SemaphoreType.DMA(())],
        out_shape=jax.ShapeDtypeStruct((p_rows * c, V7X_LANES), F32),
        compiler_params=_cparams(("arbitrary",), has_side_effects=True),
        name="moe_dispatch",
    )(slot_flat, mrows)


def _ffn_kernel(be_ref, nu_ref, x_ref, w1_ref, b1_ref, w2_ref, b2_ref, y_ref, w1s_ref, w2s_ref, *, d_ff):
    i = pl.program_id(0)

    @pl.when((i == 0) | (be_ref[i] != be_ref[jnp.maximum(i - 1, 0)]))
    def _():
        w1s_ref[...] = w1_ref[...].astype(BF16)
        w2s_ref[...] = w2_ref[...].astype(BF16)

    @pl.when(i < nu_ref[0])
    def _():
        x = _load_rows(x_ref, w1_ref.shape[0] // V7X_LANES).astype(BF16)
        hcat = _dot(x, w1s_ref[...]) + b1_ref[...]
        glu = jnp.minimum(hcat[:, :d_ff], SWIGLU_LIMIT)
        lin = jnp.clip(hcat[:, d_ff:], -SWIGLU_LIMIT, SWIGLU_LIMIT)
        act = glu * jax.nn.sigmoid(SWIGLU_ALPHA * glu) * (lin + 1.0)
        _store_rows(y_ref, _dot(act.astype(BF16), w2s_ref[...]) + b2_ref[...])


def _moe_ffn(blk_e, n_used, xrows, w1, b1, w2, b2, bm):
    ne, d, f2 = w1.shape
    d_ff = f2 // 2
    c = d // V7X_LANES
    p_rows = xrows.shape[0] // c
    blk = (bm * c, V7X_LANES)
    grid_spec = pltpu.PrefetchScalarGridSpec(
        num_scalar_prefetch=2,
        grid=(p_rows // bm,),
        in_specs=[
            pl.BlockSpec(blk, lambda i, be, nu: (jnp.minimum(i, nu[0] - 1), 0)),
            pl.BlockSpec((None, d, f2), lambda i, be, nu: (be[i], 0, 0)),
            pl.BlockSpec((None, 1, f2), lambda i, be, nu: (be[i], 0, 0)),
            pl.BlockSpec((None, d_ff, d), lambda i, be, nu: (be[i], 0, 0)),
            pl.BlockSpec((None, 1, d), lambda i, be, nu: (be[i], 0, 0)),
        ],
        out_specs=pl.BlockSpec(blk, lambda i, be, nu: (i, 0)),
        scratch_shapes=[pltpu.VMEM((d, f2), BF16), pltpu.VMEM((d_ff, d), BF16)],
    )
    return pl.pallas_call(
        functools.partial(_ffn_kernel, d_ff=d_ff),
        grid_spec=grid_spec,
        out_shape=jax.ShapeDtypeStruct(xrows.shape, F32),
        compiler_params=_cparams(("arbitrary",)),
        name="moe_ffn",
    )(blk_e, n_used, xrows, w1, b1.reshape(ne, 1, f2), w2, b2.reshape(ne, 1, d))


def _combine_kernel(slot_ref, next_slot_ref, h_ref, gate_ref, p_ref, wp_ref, gp_ref, wg_ref, y_ref, o_ref,
                    buf_ref, h2_ref, sems):
    i = pl.program_id(0)
    tc, d = h_ref.shape
    c = d // V7X_LANES

    def start_gather(slots, half):
        def issue(g, carry):
            for u in range(ROW_UNROLL):
                t = g * ROW_UNROLL + u
                for kk in range(TOP_K):
                    src = slots[t * TOP_K + kk]
                    pltpu.make_async_copy(y_ref.at[_row(src, c), :], buf_ref.at[half, kk, _row(t, c), :],
                                          sems.at[half]).start(priority=kk % 2)
            return carry

        lax.fori_loop(0, tc // ROW_UNROLL, issue, 0)

    cur = i % 2

    @pl.when(i == 0)
    def _():
        start_gather(slot_ref, 0)

    @pl.when(i + 1 < pl.num_programs(0))
    def _():
        start_gather(next_slot_ref, 1 - cur)

    for kk in range(TOP_K):
        pltpu.make_async_copy(y_ref.at[pl.ds(0, tc * c), :], buf_ref.at[cur, kk], sems.at[cur]).wait()

    gate = gate_ref[...]
    for j in range(c):
        cols = slice(j * V7X_LANES, (j + 1) * V7X_LANES)
        acc = h_ref[:, cols]
        for kk in range(TOP_K):
            acc = acc + gate[:, kk:kk + 1] * buf_ref[cur, kk, pl.ds(j, tc, stride=c), :]
        h2_ref[:, cols] = acc

    h2 = h2_ref[...]
    e = _dot(p_ref[...].astype(BF16), wp_ref[...])
    ms = jnp.mean(e * e, axis=-1, keepdims=True)
    e = e * lax.rsqrt(ms + NORM_EPS) * gp_ref[...]
    o_ref[...] = h2 + e * jax.nn.sigmoid(_dot(h2.astype(BF16), wg_ref[...]))


def _combine_ple(slot_flat, h, gates, yrows, p2, wp_bf, gp, wg_bf, tc):
    n, d = h.shape
    pd = p2.shape[1]
    assert tc % ROW_UNROLL == 0
    last = n // tc - 1
    full = lambda i: (0, 0)
    row = lambda i: (i, 0)
    return pl.pallas_call(
        _combine_kernel,
        grid=(n // tc,),
        in_specs=[
            pl.BlockSpec((tc * TOP_K,), lambda i: (i,), memory_space=pltpu.SMEM),
            pl.BlockSpec((tc * TOP_K,), lambda i: (jnp.minimum(i + 1, last),), memory_space=pltpu.SMEM),
            pl.BlockSpec((tc, d), row),
            pl.BlockSpec((tc, TOP_K), row),
            pl.BlockSpec((tc, pd), row),
            pl.BlockSpec((pd, d), full),
            pl.BlockSpec((1, d), full),
            pl.BlockSpec((d, d), full),
            pl.BlockSpec(memory_space=pl.ANY),
        ],
        out_specs=pl.BlockSpec((tc, d), row),
        scratch_shapes=[pltpu.VMEM((2, TOP_K, tc * d // V7X_LANES, V7X_LANES), F32),
                        pltpu.VMEM((tc, d), F32),
                        pltpu.SemaphoreType.DMA((2,))],
        out_shape=jax.ShapeDtypeStruct((n, d), F32),
        compiler_params=_cparams(("arbitrary",)),
        name="moe_combine_ple",
    )(slot_flat, slot_flat, h, gates, p2, wp_bf, gp, wg_bf, yrows)


def _tile(n, want):
    t = min(n, want)
    assert n % t == 0, (n, t)
    return t


def _layer(h3, p3, lambda_init, norm_mix_g, w_in, rw_mu, rw_w0, rw_w2, rw_a0, rw_a2, rw_g2, rw_k_k, rw_k_a, rw_r_k,
           rw_ln_g, rw_ln_b, da_q_norm_g, da_k_norm_g, da_lq1, da_lk1, da_lq2, da_lk2, da_subln_g, w_out,
           norm_ffn_g, w_router, b_router, w1, b1, w2, b2, w_ple, ple_norm_g, w_ple_gate):
    b, t, d = h3.shape
    n = b * t
    c_rw = rw_k_k.shape[0]
    lora = rw_w2.shape[1]
    rw_cols = rw_mu.shape[0]
    rw_heads = rw_r_k.shape[0]
    c_da = w_in.shape[1] - rw_cols
    assert c_da % 3 == 0
    c_da //= 3
    qk_dim = da_q_norm_g.shape[0]
    v_dim = da_subln_g.shape[0]
    da_heads = c_da // v_dim
    assert rw_heads * RW_HEAD == c_rw and c_rw % HEAD_GROUP == 0 and v_dim == V7X_LANES and 2 * qk_dim == v_dim
    assert 2 * lora == V7X_LANES and w_router.shape[1] == N_EXPERTS

    x2 = h3.reshape(n, d)
    row = lambda a: a.reshape(1, -1).astype(F32)

    gq_t = row(jnp.tile(da_q_norm_g, c_da // qk_dim))
    gk_t = row(jnp.tile(da_k_norm_g, c_da // qk_dim))
    gm_qk = _group_matrix(c_da, qk_dim, 1.0 / qk_dim)
    ta = _tile(t, ATTN_BLOCK)
    zrw, q, k, vt = _inproj(x2, row(norm_mix_g), w_in.astype(BF16), gq_t, gk_t, gm_qk, rw_cols, c_da,
                            1.0 / math.sqrt(qk_dim), _tile(ta, ROW_TILE), b, da_heads, ta)

    zeros = jnp.zeros((lora, c_rw), F32)
    w2bd = jnp.concatenate([jnp.concatenate([rw_w2[0], zeros], axis=1),
                            jnp.concatenate([zeros, rw_w2[1]], axis=1)], axis=0).astype(BF16)
    a2bd = jnp.concatenate([jnp.concatenate([rw_a2[0], zeros], axis=1),
                            jnp.concatenate([zeros, rw_a2[1]], axis=1)], axis=0).astype(BF16)
    gsum = _group_matrix(c_rw, RW_HEAD, 1.0)
    r, vv, nkk, lw, kd, bd, g, bonus = _rwkv_prep(
        zrw.reshape(b, t, rw_cols), row(rw_mu), w2bd, row(rw_w0), a2bd, row(rw_a0), rw_g2.astype(BF16),
        row(rw_k_k), row(rw_k_a), row(rw_r_k), gsum, c_rw, lora, _tile(t, ROW_TILE))
    yf, yb = _rwkv_scan(r, vv, nkk, lw, kd, bd, _tile(t, SCAN_TILE))

    slopes = 2.0 ** (-8.0 * jnp.arange(1, da_heads + 1, dtype=F32) / da_heads)
    q3 = q.reshape(b, t, c_da)
    k3 = k.reshape(b, t, c_da)
    small = (row(da_lq1), row(da_lk1), row(da_lq2), row(da_lk2), row(da_subln_g))

    def attention_running_max():
        v3 = vt[:, :, :, :v_dim, :].transpose(0, 2, 4, 1, 3).reshape(b, t, c_da)
        return _attention(slopes, q3, k3, v3, *small, da_heads, v_dim, lambda_init, ta, ta)

    score_bound = 1.01 * math.sqrt(qk_dim) * jnp.max(jnp.abs(da_q_norm_g)) * jnp.max(jnp.abs(da_k_norm_g))
    mb = jnp.ceil(2.0 * score_bound.astype(F32)) * 0.5
    o_da = lax.cond(
        mb <= MAX_FIXED_SHIFT,
        lambda: _attention_fixed(mb, q3, k3, vt, *small, da_heads, v_dim, lambda_init, ta),
        lambda: attention_running_max().astype(BF16))

    gmean = _group_matrix(c_rw, RW_HEAD, 1.0 / RW_HEAD)
    w_out_bf = w_out.astype(BF16)
    h1, m, idx, gates, counts = _post(
        yf.reshape(n, c_rw), yb.reshape(n, c_rw), bonus.reshape(n, c_rw), g.reshape(n, c_rw), o_da.reshape(n, c_da), x2,
        row(rw_ln_g), row(rw_ln_b), gmean, w_out_bf[:c_rw], w_out_bf[c_rw:], row(norm_ffn_g),
        w_router.astype(F32), row(b_router), RW_HEAD * 1e-5, _tile(n, WIDE_TILE))

    bm = MOE_BLOCK_ROWS
    counts = counts.reshape(N_EXPERTS).astype(jnp.int32)
    padded = (counts + bm - 1) // bm * bm
    pend = jnp.cumsum(padded)
    offs = (pend - padded).astype(jnp.int32)
    slot = _moe_rank(idx, offs.astype(F32).reshape(1, N_EXPERTS), _tile(n, ROW_TILE))
    n_blk = (n * TOP_K) // bm + N_EXPERTS
    blk_start = jnp.arange(n_blk, dtype=jnp.int32) * bm
    blk_e = jnp.minimum(jnp.sum(blk_start[:, None] >= pend[None, :], axis=1), N_EXPERTS - 1).astype(jnp.int32)
    n_used = (pend[-1] // bm).astype(jnp.int32).reshape(1)
    slot_flat = slot.reshape(n * TOP_K)
    xs = _dispatch(slot_flat, m, d // V7X_LANES, n_blk * bm, _tile(n, DISPATCH_TILE))
    ys = _moe_ffn(blk_e, n_used, xs, w1, b1, w2, b2, bm)
    out = _combine_ple(slot_flat, h1, gates, ys, p3.reshape(n, -1), w_ple.astype(BF16), row(ple_norm_g),
                       w_ple_gate.astype(BF16), _tile(n, COMBINE_TILE))
    return out.reshape(b, t, d)


def kernel(x, p, norm_mix_g, w_in, rw_mu, rw_w0, rw_w2, rw_a0, rw_a2, rw_g2, rw_k_k, rw_k_a, rw_r_k, rw_ln_g,
           rw_ln_b, da_q_norm_g, da_k_norm_g, da_lq1, da_lk1, da_lq2, da_lk2, da_subln_g, w_out, norm_ffn_g,
           w_router, b_router, w1, b1, w2, b2, w_ple, ple_norm_g, w_ple_gate):
    h = x.astype(F32)
    params = (norm_mix_g, w_in, rw_mu, rw_w0, rw_w2, rw_a0, rw_a2, rw_g2, rw_k_k, rw_k_a, rw_r_k, rw_ln_g, rw_ln_b,
              da_q_norm_g, da_k_norm_g, da_lq1, da_lk1, da_lq2, da_lk2, da_subln_g, w_out, norm_ffn_g,
              w_router, b_router, w1, b1, w2, b2, w_ple, ple_norm_g, w_ple_gate)
    for i in range(p.shape[0]):
        lambda_init = 0.8 - 0.6 * math.exp(-0.3 * i)
        h = _layer(h, p[i], lambda_init, *(a[i] for a in params))
    return h.astype(x.dtype)
```

```python
import functools
import math

import jax
import jax.numpy as jnp
import numpy as np
from jax import lax
from jax.experimental import pallas as pl
from jax.experimental.pallas import tpu as pltpu

F32 = jnp.float32
BF16 = jnp.bfloat16

V7X_LANES = 128
V7X_VMEM_BYTES = 64 * 1024 * 1024
VMEM_LIMIT = V7X_VMEM_BYTES - 12 * 1024 * 1024

ROW_TILE = 512
WIDE_TILE = 1024
DISPATCH_TILE = 2048
COMBINE_TILE = 256
SCAN_TILE = 512

NORM_EPS = 1e-6
RW_HEAD = 64
CHUNK = 64
HEAD_GROUP = 256
N_EXPERTS = 32
TOP_K = 4
MOE_BLOCK_ROWS = 512
SWIGLU_ALPHA = 1.702
SWIGLU_LIMIT = 7.0


def _cparams(sem, **kw):
    return pltpu.CompilerParams(dimension_semantics=sem, vmem_limit_bytes=VMEM_LIMIT, **kw)


def _dot(a, b):
    return jnp.dot(a, b, preferred_element_type=F32)


def _dot_hilo_lhs(g_bf16, x):
    hi, lo = _split2(x)
    return _dot(g_bf16, hi) + _dot(g_bf16, lo)


def _split2(x):
    hi = x.astype(BF16)
    return hi, (x - hi.astype(F32)).astype(BF16)


def _dot_hilo_rhs(x, g_bf16):
    hi, lo = _split2(x)
    out = _dot(jnp.concatenate([hi, lo], axis=0), g_bf16)
    return out[:x.shape[0]] + out[x.shape[0]:]


def _dot_hilo(a, b):
    ah, al = _split2(a)
    bh, bl = _split2(b)
    return _dot(ah, bh) + (_dot(ah, bl) + _dot(al, bh))


def _store_rows(ref, x2):
    rows, d = x2.shape
    c = d // V7X_LANES
    for j in range(c):
        ref[pl.ds(j, rows, stride=c), :] = x2[:, j * V7X_LANES:(j + 1) * V7X_LANES]


def _load_rows(ref, c):
    rows = ref.shape[0] // c
    return jnp.concatenate([ref[pl.ds(j, rows, stride=c), :] for j in range(c)], axis=1)


def _group_matrix(width, group, value):
    g = np.arange(width) // group
    return jnp.asarray(np.where(g[:, None] == g[None, :], value, 0.0), BF16)


def _inproj_kernel(x_ref, g_ref, w_ref, gq_ref, gk_ref, gm_ref, zrw_ref, q_ref, k_ref, vt_ref, *, rw_cols, c_da, scale):
    x = x_ref[...]
    ms = jnp.mean(x * x, axis=-1, keepdims=True)
    u = (x * lax.rsqrt(ms + NORM_EPS) * g_ref[...]).astype(BF16)
    z = _dot(u, w_ref[...])
    zrw_ref[...] = z[:, :rw_cols]
    zq = z[:, rw_cols:rw_cols + c_da]
    zk = z[:, rw_cols + c_da:rw_cols + 2 * c_da]
    zv = z[:, rw_cols + 2 * c_da:]
    tm = zq.shape[0]
    msqk = _dot(jnp.concatenate([zq * zq, zk * zk], axis=0).astype(BF16), gm_ref[...])
    q_ref[...] = (zq * lax.rsqrt(msqk[:tm] + NORM_EPS) * (gq_ref[...] * scale)).astype(BF16)
    k_ref[...] = (zk * lax.rsqrt(msqk[tm:] + NORM_EPS) * gk_ref[...]).astype(BF16)
    heads, vrows, _ = vt_ref.shape
    v_dim = c_da // heads
    pad_row = lax.broadcasted_iota(jnp.int32, (vrows - v_dim, tm), 0)
    pad = jnp.where(pad_row == 0, 1.0, 0.0).astype(BF16)
    for hd in range(heads):
        vt_ref[hd, 0:v_dim, :] = zv[:, hd * v_dim:(hd + 1) * v_dim].T.astype(BF16)
        vt_ref[hd, v_dim:, :] = pad


def _inproj(x2, g, w_bf, gq_t, gk_t, gm, rw_cols, c_da, scale, tm, b, heads, tb):
    n, d = x2.shape
    cols = w_bf.shape[1]
    t = n // b
    tps = t // tm
    per = tb // tm
    vrows = c_da // heads + ATTN_V_PAD
    full = lambda i: (0, 0)
    row = lambda i: (i, 0)
    return pl.pallas_call(
        functools.partial(_inproj_kernel, rw_cols=rw_cols, c_da=c_da, scale=scale),
        grid=(n // tm,),
        in_specs=[
            pl.BlockSpec((tm, d), row),
            pl.BlockSpec((1, d), full),
            pl.BlockSpec((d, cols), full),
            pl.BlockSpec((1, c_da), full),
            pl.BlockSpec((1, c_da), full),
            pl.BlockSpec((c_da, c_da), full),
        ],
        out_specs=[
            pl.BlockSpec((tm, rw_cols), row),
            pl.BlockSpec((tm, c_da), row),
            pl.BlockSpec((tm, c_da), row),
            pl.BlockSpec((None, heads, None, vrows, tm),
                         lambda i: (i // tps, 0, (i % tps) // per, 0, (i % tps) % per)),
        ],
        out_shape=[
            jax.ShapeDtypeStruct((n, rw_cols), F32),
            jax.ShapeDtypeStruct((n, c_da), BF16),
            jax.ShapeDtypeStruct((n, c_da), BF16),
            jax.ShapeDtypeStruct((b, heads, t // tb, vrows, tb), BF16),
        ],
        compiler_params=_cparams(("parallel",)),
        name="inproj",
    )(x2, g, w_bf, gq_t, gk_t, gm)


def _rwkv_prep_kernel(z_ref, zp_ref, zn_ref, mu_ref, w2_ref, w0_ref, a2_ref, a0_ref, g2_ref, kk_ref, ka_ref,
                      rk_ref, gsum_ref,
                      r_ref, v_ref, nkk_ref, lw_ref, kd_ref, bd_ref, g_ref, bonus_ref, *, c_rw, lora):
    i = pl.program_id(1)
    nt = pl.num_programs(1)
    z = z_ref[...]
    tt = z.shape[0]
    rows = lax.broadcasted_iota(jnp.int32, z.shape, 0)
    prev_row = jnp.where(i == 0, 0.0, zp_ref[7:8, :])
    next_row = jnp.where(i == nt - 1, 0.0, zn_ref[0:1, :])
    zprev = jnp.where(rows == 0, prev_row, pltpu.roll(z, 1, 0))
    znext = jnp.where(rows == tt - 1, next_row, pltpu.roll(z, tt - 1, 0))
    z = z + (0.5 * (zprev + znext) - z) * mu_ref[...]

    r = z[:, 0:c_rw]
    k = z[:, c_rw:2 * c_rw]
    v = z[:, 2 * c_rw:3 * c_rw]
    o = 3 * c_rw
    wd = z[:, o:o + 2 * lora]
    ad = z[:, o + 2 * lora:o + 4 * lora]
    gd = z[:, o + 4 * lora:]

    xw = w0_ref[...] + _dot(jnp.tanh(wd).astype(BF16), w2_ref[...])
    xa = a0_ref[...] + _dot(ad.astype(BF16), a2_ref[...])
    lw = -math.exp(-0.5) * jax.nn.sigmoid(xw)
    rate = jax.nn.sigmoid(xa)
    g = _dot(jax.nn.sigmoid(gd).astype(BF16), g2_ref[...])

    gsum = gsum_ref[...]
    kkr = k * kk_ref[...]
    ka = ka_ref[...]
    kd0 = k * (1.0 + (rate[:, :c_rw] - 1.0) * ka)
    kd1 = k * (1.0 + (rate[:, c_rw:] - 1.0) * ka)
    sums = _dot_hilo_rhs(jnp.concatenate([kkr * kkr, r * (0.5 * (kd0 + kd1)) * rk_ref[...]], axis=0), gsum)
    kk = kkr * lax.rsqrt(sums[:tt] + 1e-12)
    bonus = sums[tt:] * v

    r_ref[...] = r.astype(BF16)
    v_ref[...] = v.astype(BF16)
    nkk_ref[...] = (-kk).astype(BF16)
    lw_ref[0] = lw[:, :c_rw]
    lw_ref[1] = lw[:, c_rw:]
    kd_ref[0] = kd0.astype(BF16)
    kd_ref[1] = kd1.astype(BF16)
    bd_ref[0] = (kk * rate[:, :c_rw]).astype(BF16)
    bd_ref[1] = (kk * rate[:, c_rw:]).astype(BF16)
    g_ref[...] = g.astype(BF16)
    bonus_ref[...] = bonus.astype(BF16)


def _rwkv_prep(zrw3, mu, w2bd, w0f, a2bd, a0f, g2_bf, k_k, k_a, r_kf, gsum, c_rw, lora, tt):
    b, t, cols = zrw3.shape
    nt = t // tt
    hb = tt // 8
    nhb = t // 8
    full = lambda bi, i: (0, 0)
    tile = lambda bi, i: (bi, i, 0)
    dtile = lambda bi, i: (0, bi, i, 0)
    one = jax.ShapeDtypeStruct((b, t, c_rw), F32)
    two = jax.ShapeDtypeStruct((2, b, t, c_rw), F32)
    one16 = jax.ShapeDtypeStruct((b, t, c_rw), BF16)
    two16 = jax.ShapeDtypeStruct((2, b, t, c_rw), BF16)
    return pl.pallas_call(
        functools.partial(_rwkv_prep_kernel, c_rw=c_rw, lora=lora),
        grid=(b, nt),
        in_specs=[
            pl.BlockSpec((None, tt, cols), tile),
            pl.BlockSpec((None, 8, cols), lambda bi, i: (bi, jnp.maximum(i * hb - 1, 0), 0)),
            pl.BlockSpec((None, 8, cols), lambda bi, i: (bi, jnp.minimum((i + 1) * hb, nhb - 1), 0)),
            pl.BlockSpec((1, cols), full),
            pl.BlockSpec(w2bd.shape, full),
            pl.BlockSpec((1, 2 * c_rw), full),
            pl.BlockSpec(a2bd.shape, full),
            pl.BlockSpec((1, 2 * c_rw), full),
            pl.BlockSpec(g2_bf.shape, full),
            pl.BlockSpec((1, c_rw), full),
            pl.BlockSpec((1, c_rw), full),
            pl.BlockSpec((1, c_rw), full),
            pl.BlockSpec((c_rw, c_rw), full),
        ],
        out_specs=[
            pl.BlockSpec((None, tt, c_rw), tile),
            pl.BlockSpec((None, tt, c_rw), tile),
            pl.BlockSpec((None, tt, c_rw), tile),
            pl.BlockSpec((2, None, tt, c_rw), dtile),
            pl.BlockSpec((2, None, tt, c_rw), dtile),
            pl.BlockSpec((2, None, tt, c_rw), dtile),
            pl.BlockSpec((None, tt, c_rw), tile),
            pl.BlockSpec((None, tt, c_rw), tile),
        ],
        out_shape=[one16, one16, one16, two, two16, two16, one16, one16],
        compiler_params=_cparams(("parallel", "parallel")),
        name="rwkv_prep",
    )(zrw3, zrw3, zrw3, mu, w2bd, w0f, a2bd, a0f, g2_bf, k_k, k_a, r_kf, gsum)


M_STRICT, M_INCL, M_EYE, M_PAIR, M_LEVEL = 0, 1, 2, 3, 4
N_LEVELS = 5
SCAN_SEQS = 2


def _scan_tables(w):
    c = CHUNK
    masks = []
    mcums = []
    for d in range(2):
        t = np.broadcast_to(np.arange(c)[:, None], (c, w))
        s = np.broadcast_to((np.arange(w) % RW_HEAD)[None, :], (c, w))
        tt = np.broadcast_to(np.arange(c)[:, None], (c, c))
        ss = np.broadcast_to(np.arange(c)[None, :], (c, c))
        if d == 1:
            t, s, tt, ss = c - 1 - t, c - 1 - s, c - 1 - tt, c - 1 - ss
        rows = [s < t, s <= t, s == t, (t // 2 == s // 2) & (t > s)]
        sz = 2
        while sz < c:
            rows.append((t // (2 * sz) == s // (2 * sz)) & ((t // sz) % 2 == 1) & ((s // sz) % 2 == 0))
            sz *= 2
        assert len(rows) == M_LEVEL + N_LEVELS
        masks.append(np.stack(rows).astype(np.float32))
        mcums.append((ss <= tt).astype(np.float32))
    return jnp.asarray(np.stack(masks), F32), jnp.asarray(np.stack(mcums), BF16)


def _block_diag(y, bd16):
    g = y.shape[1] // RW_HEAD
    return jnp.concatenate([y.astype(BF16)] * g, axis=0) * bd16


def _bmm(x, y, bd16):
    outs = []
    for s in range(0, y.shape[1], HEAD_GROUP):
        outs.append(_dot(x[:, s:s + HEAD_GROUP].astype(BF16), _block_diag(y[:, s:s + HEAD_GROUP], bd16)))
    return jnp.concatenate(outs, axis=1)


def _bmm_nt(x, y, bd16):
    outs = []
    for s in range(0, y.shape[1], HEAD_GROUP):
        bd = _block_diag(y[:, s:s + HEAD_GROUP], bd16)
        outs.append(lax.dot_general(x[:, s:s + HEAD_GROUP].astype(BF16), bd, (((1,), (1,)), ((), ())),
                                    preferred_element_type=F32))
    return jnp.concatenate(outs, axis=1)


def _bmm_tn(x, y, bd32):
    outs = []
    for s in range(0, y.shape[1], HEAD_GROUP):
        xt = x[:, s:s + HEAD_GROUP].T.astype(BF16)
        full = _dot(xt, y[:, s:s + HEAD_GROUP].astype(BF16)) * bd32
        o = full[0:RW_HEAD]
        for j in range(1, HEAD_GROUP // RW_HEAD):
            o = o + full[j * RW_HEAD:(j + 1) * RW_HEAD]
        outs.append(o)
    return jnp.concatenate(outs, axis=1)


def _scan_chunks(r, lw, k, v, a, b, state, masks, mcum, bd16, bd32):
    c = r[0].shape[0]

    def each(f, *cols):
        return [f(*xs) for xs in zip(*cols)]

    cum = each(_dot_hilo_lhs, mcum, lw)
    tot = each(lambda x: jnp.sum(x, axis=0, keepdims=True), lw)
    e_neg = each(lambda x: jnp.exp(-x), cum)
    e_tail = each(lambda t_, x: jnp.exp(t_ - x), tot, cum)
    rt = each(lambda x, cm: x * jnp.exp(cm), r, cum)
    at = each(lambda x, cm, l_: x * jnp.exp(cm - l_), a, cum, lw)
    bt = each(jnp.multiply, b, e_neg)
    kt = each(jnp.multiply, k, e_neg)
    bh = each(jnp.multiply, b, e_tail)
    kh = each(jnp.multiply, k, e_tail)

    ar = each(lambda x, y_: jnp.concatenate([x, y_], axis=0), at, rt)
    pb = each(lambda x, y_: _bmm_nt(x, y_, bd16), ar, bt)
    pk = each(lambda x, y_: _bmm_nt(x, y_, bd16), ar, kt)
    a_ab = each(lambda x, m: x[:c] * m[M_STRICT], pb, masks)
    a_rb = each(lambda x, m: x[c:] * m[M_INCL], pb, masks)
    a_ak = each(lambda x, m: x[:c] * m[M_STRICT], pk, masks)
    a_rk = each(lambda x, m: x[c:] * m[M_INCL], pk, masks)

    tinv = each(lambda x, m: m[M_EYE] + x * m[M_PAIR], a_ab, masks)
    for lvl in range(N_LEVELS):
        x = each(lambda aa, m, ti: _bmm(aa * m[M_LEVEL + lvl], ti, bd16), a_ab, masks, tinv)
        tinv = each(lambda ti, x_: ti + _bmm(ti, x_, bd16), tinv, x)

    bmm = lambda p, q: _bmm(p, q, bd16)
    stack = lambda p, q: jnp.concatenate([p, q], axis=0)
    att = each(bmm, tinv, at)
    akv = each(lambda p, p2, q: _bmm(stack(p, p2), q, bd16), a_ak, a_rk, v)
    vt = each(lambda ti, x_: _bmm(ti, x_[:c], bd16), tinv, akv)
    rh = each(lambda x, p, q: x + _bmm(p, q, bd16), rt, a_rb, att)
    y_in = each(lambda p, q, x_: _bmm(p, q, bd16) + x_[c:], a_rb, vt, akv)
    mc = each(lambda m, t_, p, q: m[M_EYE] * jnp.exp(t_) + _bmm_tn(p, q, bd32), masks, tot, bh, att)
    nc = each(lambda p, p2, q, q2: _bmm_tn(stack(p, p2), stack(q, q2), bd32), bh, kh, vt, v)
    onst = each(lambda p, p2, q: _bmm(stack(p, p2), q, bd16), rh, mc, state)
    y = each(lambda x_, yi: x_[:c] + yi, onst, y_in)
    new_state = each(lambda x_, n_: x_[c:] + n_, onst, nc)
    return y, new_state


def _rwkv_scan_kernel(rf_ref, vf_ref, af_ref, rb_ref, vb_ref, ab_ref, lwf_ref, kf_ref, bf_ref, lwb_ref, kb_ref,
                      bb_ref, mask_ref, mcum_ref, bd16_ref, bd32_ref, yf_ref, yb_ref, state_ref, *, n_chunks):
    @pl.when(pl.program_id(1) == 0)
    def _():
        state_ref[...] = jnp.zeros_like(state_ref)

    dirs = ((0, rf_ref, lwf_ref, kf_ref, vf_ref, af_ref, bf_ref, yf_ref),
            (1, rb_ref, lwb_ref, kb_ref, vb_ref, ab_ref, bb_ref, yb_ref))
    n_seq = rf_ref.shape[0]

    def body(j, carry):
        cols = [[] for _ in range(9)]
        where = []
        for d, r_ref, lw_ref, k_ref, v_ref, a_ref, b_ref, y_ref in dirs:
            cj = j if d == 0 else n_chunks - 1 - j
            sl = pl.ds(pl.multiple_of(cj * CHUNK, CHUNK), CHUNK)
            for g in range(n_seq):
                vals = (r_ref[g, sl, :], lw_ref[g, sl, :], k_ref[g, sl, :], v_ref[g, sl, :], a_ref[g, sl, :],
                        b_ref[g, sl, :], state_ref[d, g], mask_ref.at[d], mcum_ref[d])
                for col, val in zip(cols, vals):
                    col.append(val)
                where.append((y_ref, d, g, sl))
        ys, sts = _scan_chunks(*cols, bd16_ref[...], bd32_ref[...])
        for (y_ref, d, g, sl), y, st in zip(where, ys, sts):
            y_ref[g, sl, :] = y.astype(y_ref.dtype)
            state_ref[d, g] = st
        return carry

    lax.fori_loop(0, n_chunks, body, 0)


def _rwkv_scan(r, v, nkk, lw, kd, bd, tb):
    b, t, w = r.shape
    nb = t // tb
    g = SCAN_SEQS if b % SCAN_SEQS == 0 else 1
    masks, mcum = _scan_tables(w)
    bd32 = _group_matrix(HEAD_GROUP, RW_HEAD, 1.0).astype(F32)
    bd16 = bd32.astype(BF16)
    fwd = lambda bi, i: (bi, i, 0)
    bwd = lambda bi, i: (bi, nb - 1 - i, 0)
    fwd_d = lambda bi, i: (0, bi, i, 0)
    bwd_d = lambda bi, i: (1, bi, nb - 1 - i, 0)
    const = lambda nd: (lambda bi, i: (0,) * nd)
    blk = (g, tb, w)
    dblk = (None, g, tb, w)
    out = jax.ShapeDtypeStruct((b, t, w), BF16)
    return pl.pallas_call(
        functools.partial(_rwkv_scan_kernel, n_chunks=tb // CHUNK),
        grid=(b // g, nb),
        in_specs=[
            pl.BlockSpec(blk, fwd), pl.BlockSpec(blk, fwd), pl.BlockSpec(blk, fwd),
            pl.BlockSpec(blk, bwd), pl.BlockSpec(blk, bwd), pl.BlockSpec(blk, bwd),
            pl.BlockSpec(dblk, fwd_d), pl.BlockSpec(dblk, fwd_d), pl.BlockSpec(dblk, fwd_d),
            pl.BlockSpec(dblk, bwd_d), pl.BlockSpec(dblk, bwd_d), pl.BlockSpec(dblk, bwd_d),
            pl.BlockSpec(masks.shape, const(4)),
            pl.BlockSpec(mcum.shape, const(3)),
            pl.BlockSpec(bd16.shape, const(2)),
            pl.BlockSpec(bd32.shape, const(2)),
        ],
        out_specs=[pl.BlockSpec(blk, fwd), pl.BlockSpec(blk, bwd)],
        out_shape=[out, out],
        scratch_shapes=[pltpu.VMEM((2, g, CHUNK, w), F32)],
        compiler_params=_cparams(("parallel", "arbitrary")),
        name="rwkv_scan",
    )(r, v, nkk, r, v, nkk, lw, kd, bd, lw, kd, bd, masks, mcum, bd16, bd32)


def _attn_kernel(slope_ref, q_ref, k_ref, v_ref, lq1_ref, lk1_ref, lq2_ref, lk2_ref, sg_ref, o_ref,
                 m_ref, l_ref, acc_ref, *, qk_dim, lambda_init):
    h = pl.program_id(1)
    i = pl.program_id(2)
    j = pl.program_id(3)
    nk = pl.num_programs(3)
    tq = q_ref.shape[0]
    tk = k_ref.shape[0]

    @pl.when(j == 0)
    def _():
        m_ref[...] = jnp.full(m_ref.shape, -jnp.inf, F32)
        l_ref[...] = jnp.zeros_like(l_ref)
        acc_ref[...] = jnp.zeros_like(acc_ref)

    q = q_ref[...]
    k = k_ref[...]
    v = v_ref[...]
    lane = lax.broadcasted_iota(jnp.int32, q.shape, 1)
    zero = jnp.zeros_like(q)
    qi = lax.broadcasted_iota(jnp.int32, (tq, tk), 0) + i * tq
    kj = lax.broadcasted_iota(jnp.int32, (tq, tk), 1) + j * tk
    bias = slope_ref[h] * jnp.abs(qi - kj).astype(F32)
    for c in range(2):
        qc = jnp.where((lane < qk_dim) == (c == 0), q, zero)
        s = lax.dot_general(qc, k, (((1,), (1,)), ((), ())), preferred_element_type=F32) - bias
        m_old = m_ref[c]
        m_new = jnp.maximum(m_old, jnp.max(s, axis=-1, keepdims=True))
        alpha = jnp.exp(m_old - m_new)
        p = jnp.exp(s - m_new)
        l_ref[c] = alpha * l_ref[c] + jnp.sum(p, axis=-1, keepdims=True)
        acc_ref[c] = alpha * acc_ref[c] + _dot(p.astype(BF16), v)
        m_ref[c] = m_new

    @pl.when(j == nk - 1)
    def _():
        lam = (jnp.exp(jnp.sum(lq1_ref[...] * lk1_ref[...], axis=-1, keepdims=True))
               - jnp.exp(jnp.sum(lq2_ref[...] * lk2_ref[...], axis=-1, keepdims=True)) + lambda_init)
        o = acc_ref[0] / l_ref[0] - lam * (acc_ref[1] / l_ref[1])
        ms = jnp.mean(o * o, axis=-1, keepdims=True)
        o_ref[...] = o * lax.rsqrt(ms + NORM_EPS) * (sg_ref[...] * (1.0 - lambda_init))


def _attention(slopes, q3, k3, v3, lq1, lk1, lq2, lk2, subln_g, heads, v_dim, lambda_init, tq, tk):
    b, t, _ = q3.shape
    const = lambda bi, h, i, j, s: (0, 0)
    grid_spec = pltpu.PrefetchScalarGridSpec(
        num_scalar_prefetch=1,
        grid=(b, heads, t // tq, t // tk),
        in_specs=[
            pl.BlockSpec((None, tq, v_dim), lambda bi, h, i, j, s: (bi, i, h)),
            pl.BlockSpec((None, tk, v_dim), lambda bi, h, i, j, s: (bi, j, h)),
            pl.BlockSpec((None, tk, v_dim), lambda bi, h, i, j, s: (bi, j, h)),
            pl.BlockSpec(lq1.shape, const),
            pl.BlockSpec(lk1.shape, const),
            pl.BlockSpec(lq2.shape, const),
            pl.BlockSpec(lk2.shape, const),
            pl.BlockSpec(subln_g.shape, const),
        ],
        out_specs=pl.BlockSpec((None, tq, v_dim), lambda bi, h, i, j, s: (bi, i, h)),
        scratch_shapes=[
            pltpu.VMEM((2, tq, 1), F32),
            pltpu.VMEM((2, tq, 1), F32),
            pltpu.VMEM((2, tq, v_dim), F32),
        ],
    )
    return pl.pallas_call(
        functools.partial(_attn_kernel, qk_dim=v_dim // 2, lambda_init=lambda_init),
        grid_spec=grid_spec,
        out_shape=jax.ShapeDtypeStruct((b, t, heads * v_dim), F32),
        compiler_params=_cparams(("parallel", "parallel", "parallel", "arbitrary")),
        name="attn",
    )(slopes, q3, k3, v3, lq1, lk1, lq2, lk2, subln_g)


EXP_ZERO_ARG = 104.0
MAX_FIXED_SHIFT = 40.0
N_AUG = 6
ATTN_BLOCK = 512
ATTN_GROUP_SIZES = (8, 4, 2, 1)
ATTN_KEY_SPLIT = 1
ATTN_V_PAD = 16


def _attn_fixed_kernel(w_ref, slope_ref, mb_ref, q_ref, k_ref, vt_ref, dbias_ref, qaug_ref, kaug_ref,
                       lq1_ref, lk1_ref, lq2_ref, lk2_ref, sg_ref, o_ref, acc_ref, *, qk_dim, lambda_init):
    h = pl.program_id(1)
    i = pl.program_id(2)
    tb = q_ref.shape[0]
    nk = k_ref.shape[0] // tb
    v_dim = 2 * qk_dim
    slope = slope_ref[h]
    mb = mb_ref[0]
    w = w_ref[h]

    q = q_ref[...]
    lane = lax.broadcasted_iota(jnp.int32, q.shape, 1)
    lane_row = lax.broadcasted_iota(jnp.int32, (1, q.shape[1]), 1)
    is_c0 = lane < qk_dim
    datas = (is_c0, jnp.logical_not(is_c0))

    def blocks(items, qvar, kvar, diag):
        sub = tb // ATTN_KEY_SPLIT
        qas = [jnp.where(datas[c], q, qaug_ref[qvar, c]) for c in range(2)]
        chains = []
        for n, (j, delta) in enumerate(items):
            kblk = k_ref[pl.ds(pl.multiple_of(j * tb, tb), tb), :]
            vtblk = vt_ref[j]
            for c in range(2):
                base = qk_dim * (1 - c)
                dyn = jnp.where(lane_row == base + 4, -mb,
                                jnp.where(lane_row == base + 5, -(slope * tb) * delta, 0.0)).astype(BF16)
                ka = jnp.where(datas[c], kblk, kaug_ref[kvar, c] + dyn)
                for u in range(ATTN_KEY_SPLIT):
                    rows = slice(u * sub, (u + 1) * sub)
                    chains.append((c, u, qas[c], ka[rows], vtblk[:, rows]))
        sts = [lax.dot_general(ka, qa, (((1,), (1,)), ((), ())), preferred_element_type=F32)
               for _, _, qa, ka, _ in chains]
        if diag:
            sts = [st - dbias_ref[u * sub:(u + 1) * sub, :] for st, (_, u, _, _, _) in zip(sts, chains)]
        pts = [jnp.exp(st).astype(BF16) for st in sts]
        for c in range(2):
            upd = None
            for pt, (cc, _, _, _, vsub) in zip(pts, chains):
                if cc == c:
                    part = _dot(vsub, pt)
                    upd = part if upd is None else upd + part
            if diag:
                acc_ref[c] = upd
            else:
                acc_ref[c] += upd

    def span(first, count, qvar, kvar, dist):
        for per in ATTN_GROUP_SIZES:
            def group(g, carry, first=first, per=per):
                j0 = first + g * per
                blocks([(j0 + n, dist(j0 + n)) for n in range(per)], qvar, kvar, False)
                return carry

            groups = count // per
            lax.fori_loop(0, groups, group, 0)
            first = first + groups * per
            count = count - groups * per

    lo = jnp.maximum(i - w, 0)
    hi = jnp.minimum(i + w, nk - 1)
    blocks([(i, jnp.zeros((), F32))], 2, 0, True)
    span(lo, i - lo, 0, 0, lambda j: (i - j).astype(F32))
    span(i + 1, hi - i, 1, 1, lambda j: (j - i).astype(F32))

    lam = (jnp.exp(jnp.sum(lq1_ref[...] * lk1_ref[...], axis=-1, keepdims=True))
           - jnp.exp(jnp.sum(lq2_ref[...] * lk2_ref[...], axis=-1, keepdims=True)) + lambda_init)
    a0 = acc_ref[0]
    a1 = acc_ref[1]
    inv0 = 1.0 / a0[v_dim:v_dim + 1]
    inv1 = lam / a1[v_dim:v_dim + 1]
    ot = a0[:v_dim] * inv0 - a1[:v_dim] * inv1
    ms = jnp.mean(ot * ot, axis=0, keepdims=True)
    o_ref[...] = (ot * lax.rsqrt(ms + NORM_EPS) * (sg_ref[...] * (1.0 - lambda_init))).T.astype(o_ref.dtype)


def _attn_aug_tables(slopes, tb, qk_dim):
    heads = len(slopes)
    lanes = 2 * qk_dim
    pos = np.arange(tb)
    qaug = np.zeros((heads, 3, 2, tb, lanes), np.float32)
    kaug = np.zeros((heads, 2, 2, tb, lanes), np.float32)
    for hd, sl in enumerate(slopes):
        for var in range(3):
            rr = pos if var != 1 else tb - 1 - pos
            hi = (rr // 16) * (16.0 * sl)
            lo = (rr % 16) * sl
            for c in range(2):
                base = qk_dim * (1 - c)
                if var == 2:
                    qaug[hd, var, c, :, base + 4] = 1.0
                    continue
                qaug[hd, var, c, :, base + 0] = -hi
                qaug[hd, var, c, :, base + 1] = -lo
                qaug[hd, var, c, :, base + 2:base + N_AUG] = 1.0
                kaug[hd, var, c, :, base + 0:base + 2] = 1.0
                kaug[hd, var, c, :, base + 2] = hi
                kaug[hd, var, c, :, base + 3] = lo
    return jnp.asarray(qaug, BF16), jnp.asarray(kaug, BF16)


def _attention_fixed(mb, q3, k3, vt, lq1, lk1, lq2, lk2, subln_g, heads, v_dim, lambda_init, tb):
    b, t, _ = q3.shape
    qk_dim = v_dim // 2
    slopes = [2.0 ** (-8.0 * (hd + 1) / heads) for hd in range(heads)]
    widths = [int(math.ceil((EXP_ZERO_ARG / s - 1.0) / tb)) for s in slopes]
    qaug, kaug = _attn_aug_tables(slopes, tb, qk_dim)
    pos = np.arange(tb)
    dist = np.abs(pos[:, None] - pos[None, :]).astype(np.float32)
    dbias = jnp.asarray(np.stack([dist * s for s in slopes]), F32)
    nb = t // tb
    vrows = v_dim + ATTN_V_PAD
    assert vt.shape == (b, heads, nb, vrows, tb)
    const = lambda bi, h, i, *_: (0, 0)
    grid_spec = pltpu.PrefetchScalarGridSpec(
        num_scalar_prefetch=3,
        grid=(b, heads, t // tb),
        in_specs=[
            pl.BlockSpec((None, tb, v_dim), lambda bi, h, i, *_: (bi, i, h)),
            pl.BlockSpec((None, t, v_dim), lambda bi, h, i, *_: (bi, 0, h)),
            pl.BlockSpec((None, None, nb, vrows, tb), lambda bi, h, i, *_: (bi, h, 0, 0, 0)),
            pl.BlockSpec((None, tb, tb), lambda bi, h, i, *_: (h, 0, 0)),
            pl.BlockSpec((None, 3, 2, tb, v_dim), lambda bi, h, i, *_: (h, 0, 0, 0, 0)),
            pl.BlockSpec((None, 2, 2, tb, v_dim), lambda bi, h, i, *_: (h, 0, 0, 0, 0)),
            pl.BlockSpec(lq1.shape, const),
            pl.BlockSpec(lk1.shape, const),
            pl.BlockSpec(lq2.shape, const),
            pl.BlockSpec(lk2.shape, const),
            pl.BlockSpec((v_dim, 1), const),
        ],
        out_specs=pl.BlockSpec((None, tb, v_dim), lambda bi, h, i, *_: (bi, i, h)),
        scratch_shapes=[pltpu.VMEM((2, vrows, tb), F32)],
    )
    return pl.pallas_call(
        functools.partial(_attn_fixed_kernel, qk_dim=qk_dim, lambda_init=lambda_init),
        grid_spec=grid_spec,
        out_shape=jax.ShapeDtypeStruct((b, t, heads * v_dim), BF16),
        compiler_params=_cparams(("parallel", "parallel", "parallel")),
        name="attn_fixed",
    )(jnp.asarray(widths, jnp.int32), jnp.asarray(slopes, F32), mb.reshape(1).astype(F32),
      q3, k3, vt, dbias, qaug, kaug, lq1, lk1, lq2, lk2, subln_g.reshape(v_dim, 1))


def _post_kernel(yf_ref, yb_ref, bonus_ref, g_ref, oda_ref, x_ref, lng_ref, lnb_ref, gmean_ref, wo_rw_ref, wo_da_ref,
                 gffn_ref, wr_ref, br_ref, h_ref, m_ref, idx_ref, gate_ref, cnt_ref, *, gn_eps):
    @pl.when(pl.program_id(0) == 0)
    def _():
        cnt_ref[...] = jnp.zeros_like(cnt_ref)

    gmean = gmean_ref[...]
    y = yf_ref[...].astype(F32) + yb_ref[...].astype(F32)
    mean = _dot_hilo_rhs(y, gmean)
    yc = y - mean
    var = _dot_hilo_rhs(yc * yc, gmean)
    yn = yc * lax.rsqrt(var + gn_eps) * lng_ref[...] + lnb_ref[...]
    o_rw = (yn + bonus_ref[...]) * g_ref[...]
    h = x_ref[...] + _dot(o_rw.astype(BF16), wo_rw_ref[...]) + _dot(oda_ref[...].astype(BF16), wo_da_ref[...])
    h_ref[...] = h
    ms = jnp.mean(h * h, axis=-1, keepdims=True)
    m = h * lax.rsqrt(ms + NORM_EPS) * gffn_ref[...]
    _store_rows(m_ref, m)

    logits = _dot_hilo(m, wr_ref[...]) + br_ref[...]
    ne = logits.shape[1]
    lane = lax.broadcasted_iota(jnp.int32, logits.shape, 1).astype(F32)
    kcol = lax.broadcasted_iota(jnp.int32, (logits.shape[0], TOP_K), 1)
    idx_all = jnp.zeros((logits.shape[0], TOP_K), F32)
    val_all = jnp.zeros((logits.shape[0], TOP_K), F32)
    work = logits
    chosen = jnp.zeros(logits.shape, F32)
    for kk in range(TOP_K):
        mx = jnp.max(work, axis=-1, keepdims=True)
        sel = jnp.min(jnp.where(work == mx, lane, float(ne)), axis=-1, keepdims=True)
        idx_all = jnp.where(kcol == kk, sel, idx_all)
        val_all = jnp.where(kcol == kk, mx, val_all)
        hit = lane == sel
        chosen = jnp.where(hit, 1.0, chosen)
        work = jnp.where(hit, -jnp.inf, work)
    e = jnp.exp(val_all - jnp.max(val_all, axis=-1, keepdims=True))
    idx_ref[...] = idx_all.astype(jnp.int32)
    gate_ref[...] = e / jnp.sum(e, axis=-1, keepdims=True)
    cnt_ref[...] += jnp.sum(chosen, axis=0, keepdims=True)


def _post(yf, yb, bonus, g, oda, x2, lng, lnb, gmean, wo_rw, wo_da, gffn, wr, br, gn_eps, tm):
    n, c = yf.shape
    d = x2.shape[1]
    ne = wr.shape[1]
    full = lambda i: (0, 0)
    row = lambda i: (i, 0)
    return pl.pallas_call(
        functools.partial(_post_kernel, gn_eps=gn_eps),
        grid=(n // tm,),
        in_specs=[
            pl.BlockSpec((tm, c), row),
            pl.BlockSpec((tm, c), row),
            pl.BlockSpec((tm, c), row),
            pl.BlockSpec((tm, c), row),
            pl.BlockSpec((tm, c), row),
            pl.BlockSpec((tm, d), row),
            pl.BlockSpec((1, c), full),
            pl.BlockSpec((1, c), full),
            pl.BlockSpec((c, c), full),
            pl.BlockSpec((c, d), full),
            pl.BlockSpec((c, d), full),
            pl.BlockSpec((1, d), full),
            pl.BlockSpec((d, ne), full),
            pl.BlockSpec((1, ne), full),
        ],
        out_specs=[
            pl.BlockSpec((tm, d), row),
            pl.BlockSpec((tm * d // V7X_LANES, V7X_LANES), row),
            pl.BlockSpec((tm, TOP_K), row),
            pl.BlockSpec((tm, TOP_K), row),
            pl.BlockSpec((1, ne), full),
        ],
        out_shape=[
            jax.ShapeDtypeStruct((n, d), F32),
            jax.ShapeDtypeStruct((n * d // V7X_LANES, V7X_LANES), F32),
            jax.ShapeDtypeStruct((n, TOP_K), jnp.int32),
            jax.ShapeDtypeStruct((n, TOP_K), F32),
            jax.ShapeDtypeStruct((1, ne), F32),
        ],
        compiler_params=_cparams(("arbitrary",)),
        name="post",
    )(yf, yb, bonus, g, oda, x2, lng, lnb, gmean, wo_rw, wo_da, gffn, wr, br)


def _rank_kernel(idx_ref, ltri_ref, offs_ref, slot_ref, carry_ref):
    @pl.when(pl.program_id(0) == 0)
    def _():
        carry_ref[...] = offs_ref[...]

    idx = idx_ref[...]
    tr = idx.shape[0]
    lane = lax.broadcasted_iota(jnp.int32, (tr, N_EXPERTS), 1)
    hots = [jnp.where(lane == idx[:, kk:kk + 1], 1.0, 0.0) for kk in range(TOP_K)]
    mask = hots[0]
    for kk in range(1, TOP_K):
        mask = mask + hots[kk]
    before = _dot(ltri_ref[...], mask.astype(BF16)) + carry_ref[...]
    kcol = lax.broadcasted_iota(jnp.int32, (tr, TOP_K), 1)
    slot = jnp.zeros((tr, TOP_K), F32)
    for kk in range(TOP_K):
        rk = jnp.sum(hots[kk] * before, axis=-1, keepdims=True)
        slot = jnp.where(kcol == kk, rk, slot)
    slot_ref[...] = slot.astype(jnp.int32)
    carry_ref[...] = carry_ref[...] + jnp.sum(mask, axis=0, keepdims=True)


def _moe_rank(idx, offs_row, tr):
    n = idx.shape[0]
    ltri = jnp.asarray(np.tril(np.ones((tr, tr), np.float32), -1), BF16)
    return pl.pallas_call(
        _rank_kernel,
        grid=(n // tr,),
        in_specs=[pl.BlockSpec((tr, TOP_K), lambda i: (i, 0)), pl.BlockSpec((tr, tr), lambda i: (0, 0)),
                  pl.BlockSpec((1, N_EXPERTS), lambda i: (0, 0))],
        out_specs=pl.BlockSpec((tr, TOP_K), lambda i: (i, 0)),
        out_shape=jax.ShapeDtypeStruct((n, TOP_K), jnp.int32),
        scratch_shapes=[pltpu.VMEM((1, N_EXPERTS), F32)],
        compiler_params=_cparams(("arbitrary",)),
        name="moe_rank",
    )(idx, ltri, offs_row)


ROW_UNROLL = 8


def _row(r, c):
    return pl.ds(pl.multiple_of(r * c, c), c)


def _dispatch_kernel(slot_ref, m_ref, xs_ref, sem, *, c):
    td = m_ref.shape[0] // c

    def issue(g, carry):
        for u in range(ROW_UNROLL):
            t = g * ROW_UNROLL + u
            for kk in range(TOP_K):
                dest = slot_ref[t * TOP_K + kk]
                pltpu.make_async_copy(m_ref.at[_row(t, c), :], xs_ref.at[_row(dest, c), :], sem).start(
                    priority=kk % 2)
        return carry

    lax.fori_loop(0, td // ROW_UNROLL, issue, 0)
    for kk in range(TOP_K):
        pltpu.make_async_copy(m_ref, xs_ref.at[pl.ds(0, td * c), :], sem).wait()


def _dispatch(slot_flat, mrows, c, p_rows, td):
    n = mrows.shape[0] // c
    assert td % ROW_UNROLL == 0
    return pl.pallas_call(
        functools.partial(_dispatch_kernel, c=c),
        grid=(n // td,),
        in_specs=[
            pl.BlockSpec((td * TOP_K,), lambda i: (i,), memory_space=pltpu.SMEM),
            pl.BlockSpec((td * c, V7X_LANES), lambda i: (i, 0)),
        ],
        out_specs=pl.BlockSpec(memory_space=pl.ANY),
        scratch_shapes=[pltpu.SemaphoreType.DMA(())],
        out_shape=jax.ShapeDtypeStruct((p_rows * c, V7X_LANES), F32),
        compiler_params=_cparams(("arbitrary",), has_side_effects=True),
        name="moe_dispatch",
    )(slot_flat, mrows)


def _ffn_kernel(be_ref, nu_ref, chg_ref, par_ref, nxt_ref, x_ref, w1_ref, b1_ref, w2_ref, b2_ref, y_ref,
                w1f_ref, w2f_ref, w1s_ref, w2s_ref, sems, *, d_ff):
    i = pl.program_id(0)

    def weight_copies(e, slot):
        return (pltpu.make_async_copy(w1_ref.at[e], w1f_ref.at[slot], sems.at[0, slot]),
                pltpu.make_async_copy(w2_ref.at[e], w2f_ref.at[slot], sems.at[1, slot]))

    @pl.when(i == 0)
    def _():
        for cp in weight_copies(be_ref[0], 0):
            cp.start()

    @pl.when(chg_ref[i] == 1)
    def _():
        slot = par_ref[i]
        for cp in weight_copies(be_ref[i], slot):
            cp.wait()
        w1s_ref[...] = w1f_ref[slot].astype(BF16)
        w2s_ref[...] = w2f_ref[slot].astype(BF16)

        @pl.when(nxt_ref[i] >= 0)
        def _():
            for cp in weight_copies(nxt_ref[i], 1 - slot):
                cp.start()

    @pl.when(i < nu_ref[0])
    def _():
        x = _load_rows(x_ref, w1s_ref.shape[0] // V7X_LANES).astype(BF16)
        hcat = _dot(x, w1s_ref[...]) + b1_ref[...]
        glu = jnp.minimum(hcat[:, :d_ff], SWIGLU_LIMIT)
        lin = jnp.clip(hcat[:, d_ff:], -SWIGLU_LIMIT, SWIGLU_LIMIT)
        act = glu * jax.nn.sigmoid(SWIGLU_ALPHA * glu) * (lin + 1.0)
        _store_rows(y_ref, _dot(act.astype(BF16), w2s_ref[...]) + b2_ref[...])


def _moe_ffn(blk_e, n_used, xrows, w1, b1, w2, b2, bm):
    ne, d, f2 = w1.shape
    d_ff = f2 // 2
    c = d // V7X_LANES
    p_rows = xrows.shape[0] // c
    blk = (bm * c, V7X_LANES)
    n_blk = p_rows // bm
    pos = jnp.arange(n_blk, dtype=jnp.int32)
    prev_e = jnp.concatenate([jnp.full((1,), -1, jnp.int32), blk_e[:-1]])
    chg = ((pos < n_used[0]) & (blk_e != prev_e)).astype(jnp.int32)
    par = ((jnp.cumsum(chg) - 1) % 2).astype(jnp.int32)
    later = jnp.where(chg == 1, pos, n_blk)
    nextpos = jnp.concatenate([lax.cummin(later, reverse=True)[1:], jnp.full((1,), n_blk, jnp.int32)])
    nxt = jnp.where(nextpos < n_blk, blk_e[jnp.minimum(nextpos, n_blk - 1)], -1).astype(jnp.int32)
    grid_spec = pltpu.PrefetchScalarGridSpec(
        num_scalar_prefetch=5,
        grid=(n_blk,),
        in_specs=[
            pl.BlockSpec(blk, lambda i, be, nu, *_: (jnp.minimum(i, nu[0] - 1), 0)),
            pl.BlockSpec(memory_space=pl.ANY),
            pl.BlockSpec((None, 1, f2), lambda i, be, nu, *_: (be[i], 0, 0)),
            pl.BlockSpec(memory_space=pl.ANY),
            pl.BlockSpec((None, 1, d), lambda i, be, nu, *_: (be[i], 0, 0)),
        ],
        out_specs=pl.BlockSpec(blk, lambda i, be, nu, *_: (i, 0)),
        scratch_shapes=[pltpu.VMEM((2, d, f2), F32), pltpu.VMEM((2, d_ff, d), F32),
                        pltpu.VMEM((d, f2), BF16), pltpu.VMEM((d_ff, d), BF16),
                        pltpu.SemaphoreType.DMA((2, 2))],
    )
    return pl.pallas_call(
        functools.partial(_ffn_kernel, d_ff=d_ff),
        grid_spec=grid_spec,
        out_shape=jax.ShapeDtypeStruct(xrows.shape, F32),
        compiler_params=_cparams(("arbitrary",)),
        name="moe_ffn",
    )(blk_e, n_used, chg, par, nxt, xrows, w1, b1.reshape(ne, 1, f2), w2, b2.reshape(ne, 1, d))


def _combine_kernel(slot_ref, next_slot_ref, h_ref, gate_ref, p_ref, wp_ref, gp_ref, wg_ref, y_ref, o_ref,
                    buf_ref, h2_ref, sems):
    i = pl.program_id(0)
    tc, d = h_ref.shape
    c = d // V7X_LANES

    def start_gather(slots, half):
        def issue(g, carry):
            for u in range(ROW_UNROLL):
                t = g * ROW_UNROLL + u
                for kk in range(TOP_K):
                    src = slots[t * TOP_K + kk]
                    pltpu.make_async_copy(y_ref.at[_row(src, c), :], buf_ref.at[half, kk, _row(t, c), :],
                                          sems.at[half]).start(priority=kk % 2)
            return carry

        lax.fori_loop(0, tc // ROW_UNROLL, issue, 0)

    cur = i % 2

    @pl.when(i == 0)
    def _():
        start_gather(slot_ref, 0)

    @pl.when(i + 1 < pl.num_programs(0))
    def _():
        start_gather(next_slot_ref, 1 - cur)

    for kk in range(TOP_K):
        pltpu.make_async_copy(y_ref.at[pl.ds(0, tc * c), :], buf_ref.at[cur, kk], sems.at[cur]).wait()

    gate = gate_ref[...]
    for j in range(c):
        cols = slice(j * V7X_LANES, (j + 1) * V7X_LANES)
        acc = h_ref[:, cols]
        for kk in range(TOP_K):
            acc = acc + gate[:, kk:kk + 1] * buf_ref[cur, kk, pl.ds(j, tc, stride=c), :]
        h2_ref[:, cols] = acc

    h2 = h2_ref[...]
    e = _dot(p_ref[...].astype(BF16), wp_ref[...])
    ms = jnp.mean(e * e, axis=-1, keepdims=True)
    e = e * lax.rsqrt(ms + NORM_EPS) * gp_ref[...]
    o_ref[...] = h2 + e * jax.nn.sigmoid(_dot(h2.astype(BF16), wg_ref[...]))


def _combine_ple(slot_flat, h, gates, yrows, p2, wp_bf, gp, wg_bf, tc):
    n, d = h.shape
    pd = p2.shape[1]
    assert tc % ROW_UNROLL == 0
    last = n // tc - 1
    full = lambda i: (0, 0)
    row = lambda i: (i, 0)
    return pl.pallas_call(
        _combine_kernel,
        grid=(n // tc,),
        in_specs=[
            pl.BlockSpec((tc * TOP_K,), lambda i: (i,), memory_space=pltpu.SMEM),
            pl.BlockSpec((tc * TOP_K,), lambda i: (jnp.minimum(i + 1, last),), memory_space=pltpu.SMEM),
            pl.BlockSpec((tc, d), row),
            pl.BlockSpec((tc, TOP_K), row),
            pl.BlockSpec((tc, pd), row),
            pl.BlockSpec((pd, d), full),
            pl.BlockSpec((1, d), full),
            pl.BlockSpec((d, d), full),
            pl.BlockSpec(memory_space=pl.ANY),
        ],
        out_specs=pl.BlockSpec((tc, d), row),
        scratch_shapes=[pltpu.VMEM((2, TOP_K, tc * d // V7X_LANES, V7X_LANES), F32),
                        pltpu.VMEM((tc, d), F32),
                        pltpu.SemaphoreType.DMA((2,))],
        out_shape=jax.ShapeDtypeStruct((n, d), F32),
        compiler_params=_cparams(("arbitrary",)),
        name="moe_combine_ple",
    )(slot_flat, slot_flat, h, gates, p2, wp_bf, gp, wg_bf, yrows)


def _tile(n, want):
    t = min(n, want)
    assert n % t == 0, (n, t)
    return t


def _layer(h3, p3, lambda_init, norm_mix_g, w_in, rw_mu, rw_w0, rw_w2, rw_a0, rw_a2, rw_g2, rw_k_k, rw_k_a, rw_r_k,
           rw_ln_g, rw_ln_b, da_q_norm_g, da_k_norm_g, da_lq1, da_lk1, da_lq2, da_lk2, da_subln_g, w_out,
           norm_ffn_g, w_router, b_router, w1, b1, w2, b2, w_ple, ple_norm_g, w_ple_gate):
    b, t, d = h3.shape
    n = b * t
    c_rw = rw_k_k.shape[0]
    lora = rw_w2.shape[1]
    rw_cols = rw_mu.shape[0]
    rw_heads = rw_r_k.shape[0]
    c_da = w_in.shape[1] - rw_cols
    assert c_da % 3 == 0
    c_da //= 3
    qk_dim = da_q_norm_g.shape[0]
    v_dim = da_subln_g.shape[0]
    da_heads = c_da // v_dim
    assert rw_heads * RW_HEAD == c_rw and c_rw % HEAD_GROUP == 0 and v_dim == V7X_LANES and 2 * qk_dim == v_dim
    assert 2 * lora == V7X_LANES and w_router.shape[1] == N_EXPERTS

    x2 = h3.reshape(n, d)
    row = lambda a: a.reshape(1, -1).astype(F32)

    gq_t = row(jnp.tile(da_q_norm_g, c_da // qk_dim))
    gk_t = row(jnp.tile(da_k_norm_g, c_da // qk_dim))
    gm_qk = _group_matrix(c_da, qk_dim, 1.0 / qk_dim)
    ta = _tile(t, ATTN_BLOCK)
    zrw, q, k, vt = _inproj(x2, row(norm_mix_g), w_in.astype(BF16), gq_t, gk_t, gm_qk, rw_cols, c_da,
                            1.0 / math.sqrt(qk_dim), _tile(ta, ROW_TILE), b, da_heads, ta)

    zeros = jnp.zeros((lora, c_rw), F32)
    w2bd = jnp.concatenate([jnp.concatenate([rw_w2[0], zeros], axis=1),
                            jnp.concatenate([zeros, rw_w2[1]], axis=1)], axis=0).astype(BF16)
    a2bd = jnp.concatenate([jnp.concatenate([rw_a2[0], zeros], axis=1),
                            jnp.concatenate([zeros, rw_a2[1]], axis=1)], axis=0).astype(BF16)
    gsum = _group_matrix(c_rw, RW_HEAD, 1.0)
    r, vv, nkk, lw, kd, bd, g, bonus = _rwkv_prep(
        zrw.reshape(b, t, rw_cols), row(rw_mu), w2bd, row(rw_w0), a2bd, row(rw_a0), rw_g2.astype(BF16),
        row(rw_k_k), row(rw_k_a), row(rw_r_k), gsum, c_rw, lora, _tile(t, ROW_TILE))
    yf, yb = _rwkv_scan(r, vv, nkk, lw, kd, bd, _tile(t, SCAN_TILE))

    slopes = 2.0 ** (-8.0 * jnp.arange(1, da_heads + 1, dtype=F32) / da_heads)
    q3 = q.reshape(b, t, c_da)
    k3 = k.reshape(b, t, c_da)
    small = (row(da_lq1), row(da_lk1), row(da_lq2), row(da_lk2), row(da_subln_g))

    def attention_running_max():
        v3 = vt[:, :, :, :v_dim, :].transpose(0, 2, 4, 1, 3).reshape(b, t, c_da)
        return _attention(slopes, q3, k3, v3, *small, da_heads, v_dim, lambda_init, ta, ta)

    score_bound = 1.01 * math.sqrt(qk_dim) * jnp.max(jnp.abs(da_q_norm_g)) * jnp.max(jnp.abs(da_k_norm_g))
    mb = jnp.ceil(2.0 * score_bound.astype(F32)) * 0.5
    o_da = lax.cond(
        mb <= MAX_FIXED_SHIFT,
        lambda: _attention_fixed(mb, q3, k3, vt, *small, da_heads, v_dim, lambda_init, ta),
        lambda: attention_running_max().astype(BF16))

    gmean = _group_matrix(c_rw, RW_HEAD, 1.0 / RW_HEAD)
    w_out_bf = w_out.astype(BF16)
    h1, m, idx, gates, counts = _post(
        yf.reshape(n, c_rw), yb.reshape(n, c_rw), bonus.reshape(n, c_rw), g.reshape(n, c_rw), o_da.reshape(n, c_da), x2,
        row(rw_ln_g), row(rw_ln_b), gmean, w_out_bf[:c_rw], w_out_bf[c_rw:], row(norm_ffn_g),
        w_router.astype(F32), row(b_router), RW_HEAD * 1e-5, _tile(n, WIDE_TILE))

    bm = MOE_BLOCK_ROWS
    counts = counts.reshape(N_EXPERTS).astype(jnp.int32)
    padded = (counts + bm - 1) // bm * bm
    pend = jnp.cumsum(padded)
    offs = (pend - padded).astype(jnp.int32)
    slot = _moe_rank(idx, offs.astype(F32).reshape(1, N_EXPERTS), _tile(n, ROW_TILE))
    n_blk = (n * TOP_K) // bm + N_EXPERTS
    blk_start = jnp.arange(n_blk, dtype=jnp.int32) * bm
    blk_e = jnp.minimum(jnp.sum(blk_start[:, None] >= pend[None, :], axis=1), N_EXPERTS - 1).astype(jnp.int32)
    n_used = (pend[-1] // bm).astype(jnp.int32).reshape(1)
    slot_flat = slot.reshape(n * TOP_K)
    xs = _dispatch(slot_flat, m, d // V7X_LANES, n_blk * bm, _tile(n, DISPATCH_TILE))
    ys = _moe_ffn(blk_e, n_used, xs, w1, b1, w2, b2, bm)
    out = _combine_ple(slot_flat, h1, gates, ys, p3.reshape(n, -1), w_ple.astype(BF16), row(ple_norm_g),
                       w_ple_gate.astype(BF16), _tile(n, COMBINE_TILE))
    return out.reshape(b, t, d)


def kernel(x, p, norm_mix_g, w_in, rw_mu, rw_w0, rw_w2, rw_a0, rw_a2, rw_g2, rw_k_k, rw_k_a, rw_r_k, rw_ln_g,
           rw_ln_b, da_q_norm_g, da_k_norm_g, da_lq1, da_lk1, da_lq2, da_lk2, da_subln_g, w_out, norm_ffn_g,
           w_router, b_router, w1, b1, w2, b2, w_ple, ple_norm_g, w_ple_gate):
    h = x.astype(F32)
    params = (norm_mix_g, w_in, rw_mu, rw_w0, rw_w2, rw_a0, rw_a2, rw_g2, rw_k_k, rw_k_a, rw_r_k, rw_ln_g, rw_ln_b,
              da_q_norm_g, da_k_norm_g, da_lq1, da_lk1, da_lq2, da_lk2, da_subln_g, w_out, norm_ffn_g,
              w_router, b_router, w1, b1, w2, b2, w_ple, ple_norm_g, w_ple_gate)
    for i in range(p.shape[0]):
        lambda_init = 0.8 - 0.6 * math.exp(-0.3 * i)
        h = _layer(h, p[i], lambda_init, *(a[i] for a in params))
    return h.astype(x.dtype)
```

```python
import functools
import math

import jax
import jax.numpy as jnp
import numpy as np
from jax import lax
from jax.experimental import pallas as pl
from jax.experimental.pallas import tpu as pltpu

F32 = jnp.float32
BF16 = jnp.bfloat16

V7X_LANES = 128
V7X_VMEM_BYTES = 64 * 1024 * 1024
VMEM_LIMIT = V7X_VMEM_BYTES - 12 * 1024 * 1024

ROW_TILE = 512
WIDE_TILE = 1024
DISPATCH_TILE = 2048
COMBINE_TILE = 256
SCAN_TILE = 512

NORM_EPS = 1e-6
RW_HEAD = 64
CHUNK = 64
HEAD_GROUP = 256
N_EXPERTS = 32
TOP_K = 4
MOE_BLOCK_ROWS = 512
SWIGLU_ALPHA = 1.702
SWIGLU_LIMIT = 7.0


def _cparams(sem, **kw):
    return pltpu.CompilerParams(dimension_semantics=sem, vmem_limit_bytes=VMEM_LIMIT, **kw)


def _dot(a, b):
    return jnp.dot(a, b, preferred_element_type=F32)


def _dot_hilo_lhs(g_bf16, x):
    hi, lo = _split2(x)
    return _dot(g_bf16, hi) + _dot(g_bf16, lo)


def _split2(x):
    hi = x.astype(BF16)
    return hi, (x - hi.astype(F32)).astype(BF16)


def _dot_hilo_rhs(x, g_bf16):
    hi, lo = _split2(x)
    out = _dot(jnp.concatenate([hi, lo], axis=0), g_bf16)
    return out[:x.shape[0]] + out[x.shape[0]:]


def _dot_hilo(a, b):
    ah, al = _split2(a)
    bh, bl = _split2(b)
    return _dot(ah, bh) + (_dot(ah, bl) + _dot(al, bh))


def _store_rows(ref, x2):
    rows, d = x2.shape
    c = d // V7X_LANES
    for j in range(c):
        ref[pl.ds(j, rows, stride=c), :] = x2[:, j * V7X_LANES:(j + 1) * V7X_LANES]


def _load_rows(ref, c):
    rows = ref.shape[0] // c
    return jnp.concatenate([ref[pl.ds(j, rows, stride=c), :] for j in range(c)], axis=1)


def _group_matrix(width, group, value):
    g = np.arange(width) // group
    return jnp.asarray(np.where(g[:, None] == g[None, :], value, 0.0), BF16)


def _inproj_kernel(x_ref, g_ref, w_ref, gq_ref, gk_ref, gm_ref, zrw_ref, q_ref, k_ref, vt_ref, *, rw_cols, c_da, scale):
    x = x_ref[...]
    ms = jnp.mean(x * x, axis=-1, keepdims=True)
    u = (x * lax.rsqrt(ms + NORM_EPS) * g_ref[...]).astype(BF16)
    z = _dot(u, w_ref[...])
    zrw_ref[...] = z[:, :rw_cols]
    zq = z[:, rw_cols:rw_cols + c_da]
    zk = z[:, rw_cols + c_da:rw_cols + 2 * c_da]
    zv = z[:, rw_cols + 2 * c_da:]
    tm = zq.shape[0]
    msqk = _dot(jnp.concatenate([zq * zq, zk * zk], axis=0).astype(BF16), gm_ref[...])
    q_ref[...] = (zq * lax.rsqrt(msqk[:tm] + NORM_EPS) * (gq_ref[...] * scale)).astype(BF16)
    k_ref[...] = (zk * lax.rsqrt(msqk[tm:] + NORM_EPS) * gk_ref[...]).astype(BF16)
    heads, vrows, _ = vt_ref.shape
    v_dim = c_da // heads
    pad_row = lax.broadcasted_iota(jnp.int32, (vrows - v_dim, tm), 0)
    pad = jnp.where(pad_row == 0, 1.0, 0.0).astype(BF16)
    for hd in range(heads):
        vt_ref[hd, 0:v_dim, :] = zv[:, hd * v_dim:(hd + 1) * v_dim].T.astype(BF16)
        vt_ref[hd, v_dim:, :] = pad


def _inproj(x2, g, w_bf, gq_t, gk_t, gm, rw_cols, c_da, scale, tm, b, heads, tb):
    n, d = x2.shape
    cols = w_bf.shape[1]
    t = n // b
    tps = t // tm
    per = tb // tm
    vrows = c_da // heads + ATTN_V_PAD
    full = lambda i: (0, 0)
    row = lambda i: (i, 0)
    return pl.pallas_call(
        functools.partial(_inproj_kernel, rw_cols=rw_cols, c_da=c_da, scale=scale),
        grid=(n // tm,),
        in_specs=[
            pl.BlockSpec((tm, d), row),
            pl.BlockSpec((1, d), full),
            pl.BlockSpec((d, cols), full),
            pl.BlockSpec((1, c_da), full),
            pl.BlockSpec((1, c_da), full),
            pl.BlockSpec((c_da, c_da), full),
        ],
        out_specs=[
            pl.BlockSpec((tm, rw_cols), row),
            pl.BlockSpec((tm, c_da), row),
            pl.BlockSpec((tm, c_da), row),
            pl.BlockSpec((None, heads, None, vrows, tm),
                         lambda i: (i // tps, 0, (i % tps) // per, 0, (i % tps) % per)),
        ],
        out_shape=[
            jax.ShapeDtypeStruct((n, rw_cols), F32),
            jax.ShapeDtypeStruct((n, c_da), BF16),
            jax.ShapeDtypeStruct((n, c_da), BF16),
            jax.ShapeDtypeStruct((b, heads, t // tb, vrows, tb), BF16),
        ],
        compiler_params=_cparams(("parallel",)),
        name="inproj",
    )(x2, g, w_bf, gq_t, gk_t, gm)


def _rwkv_prep_kernel(z_ref, zp_ref, zn_ref, mu_ref, w2_ref, w0_ref, a2_ref, a0_ref, g2_ref, kk_ref, ka_ref,
                      rk_ref, gsum_ref,
                      r_ref, v_ref, nkk_ref, lw_ref, kd_ref, bd_ref, g_ref, bonus_ref, *, c_rw, lora):
    i = pl.program_id(1)
    nt = pl.num_programs(1)
    z = z_ref[...]
    tt = z.shape[0]
    rows = lax.broadcasted_iota(jnp.int32, z.shape, 0)
    prev_row = jnp.where(i == 0, 0.0, zp_ref[7:8, :])
    next_row = jnp.where(i == nt - 1, 0.0, zn_ref[0:1, :])
    zprev = jnp.where(rows == 0, prev_row, pltpu.roll(z, 1, 0))
    znext = jnp.where(rows == tt - 1, next_row, pltpu.roll(z, tt - 1, 0))
    z = z + (0.5 * (zprev + znext) - z) * mu_ref[...]

    r = z[:, 0:c_rw]
    k = z[:, c_rw:2 * c_rw]
    v = z[:, 2 * c_rw:3 * c_rw]
    o = 3 * c_rw
    wd = z[:, o:o + 2 * lora]
    ad = z[:, o + 2 * lora:o + 4 * lora]
    gd = z[:, o + 4 * lora:]

    xw = w0_ref[...] + _dot(jnp.tanh(wd).astype(BF16), w2_ref[...])
    xa = a0_ref[...] + _dot(ad.astype(BF16), a2_ref[...])
    lw = -math.exp(-0.5) * jax.nn.sigmoid(xw)
    rate = jax.nn.sigmoid(xa)
    g = _dot(jax.nn.sigmoid(gd).astype(BF16), g2_ref[...])

    gsum = gsum_ref[...]
    kkr = k * kk_ref[...]
    ka = ka_ref[...]
    kd0 = k * (1.0 + (rate[:, :c_rw] - 1.0) * ka)
    kd1 = k * (1.0 + (rate[:, c_rw:] - 1.0) * ka)
    sums = _dot_hilo_rhs(jnp.concatenate([kkr * kkr, r * (0.5 * (kd0 + kd1)) * rk_ref[...]], axis=0), gsum)
    kk = kkr * lax.rsqrt(sums[:tt] + 1e-12)
    bonus = sums[tt:] * v

    r_ref[...] = r.astype(BF16)
    v_ref[...] = v.astype(BF16)
    nkk_ref[...] = (-kk).astype(BF16)
    lw_ref[0] = lw[:, :c_rw]
    lw_ref[1] = lw[:, c_rw:]
    kd_ref[0] = kd0.astype(BF16)
    kd_ref[1] = kd1.astype(BF16)
    bd_ref[0] = (kk * rate[:, :c_rw]).astype(BF16)
    bd_ref[1] = (kk * rate[:, c_rw:]).astype(BF16)
    g_ref[...] = g.astype(BF16)
    bonus_ref[...] = bonus.astype(BF16)


def _rwkv_prep(zrw3, mu, w2bd, w0f, a2bd, a0f, g2_bf, k_k, k_a, r_kf, gsum, c_rw, lora, tt):
    b, t, cols = zrw3.shape
    nt = t // tt
    hb = tt // 8
    nhb = t // 8
    full = lambda bi, i: (0, 0)
    tile = lambda bi, i: (bi, i, 0)
    dtile = lambda bi, i: (0, bi, i, 0)
    one = jax.ShapeDtypeStruct((b, t, c_rw), F32)
    two = jax.ShapeDtypeStruct((2, b, t, c_rw), F32)
    one16 = jax.ShapeDtypeStruct((b, t, c_rw), BF16)
    two16 = jax.ShapeDtypeStruct((2, b, t, c_rw), BF16)
    return pl.pallas_call(
        functools.partial(_rwkv_prep_kernel, c_rw=c_rw, lora=lora),
        grid=(b, nt),
        in_specs=[
            pl.BlockSpec((None, tt, cols), tile),
            pl.BlockSpec((None, 8, cols), lambda bi, i: (bi, jnp.maximum(i * hb - 1, 0), 0)),
            pl.BlockSpec((None, 8, cols), lambda bi, i: (bi, jnp.minimum((i + 1) * hb, nhb - 1), 0)),
            pl.BlockSpec((1, cols), full),
            pl.BlockSpec(w2bd.shape, full),
            pl.BlockSpec((1, 2 * c_rw), full),
            pl.BlockSpec(a2bd.shape, full),
            pl.BlockSpec((1, 2 * c_rw), full),
            pl.BlockSpec(g2_bf.shape, full),
            pl.BlockSpec((1, c_rw), full),
            pl.BlockSpec((1, c_rw), full),
            pl.BlockSpec((1, c_rw), full),
            pl.BlockSpec((c_rw, c_rw), full),
        ],
        out_specs=[
            pl.BlockSpec((None, tt, c_rw), tile),
            pl.BlockSpec((None, tt, c_rw), tile),
            pl.BlockSpec((None, tt, c_rw), tile),
            pl.BlockSpec((2, None, tt, c_rw), dtile),
            pl.BlockSpec((2, None, tt, c_rw), dtile),
            pl.BlockSpec((2, None, tt, c_rw), dtile),
            pl.BlockSpec((None, tt, c_rw), tile),
            pl.BlockSpec((None, tt, c_rw), tile),
        ],
        out_shape=[one16, one16, one16, two, two16, two16, one16, one16],
        compiler_params=_cparams(("parallel", "parallel")),
        name="rwkv_prep",
    )(zrw3, zrw3, zrw3, mu, w2bd, w0f, a2bd, a0f, g2_bf, k_k, k_a, r_kf, gsum)


M_STRICT, M_INCL, M_EYE, M_PAIR, M_LEVEL = 0, 1, 2, 3, 4
N_LEVELS = 5
SCAN_SEQS = 2


def _scan_tables(w):
    c = CHUNK
    masks = []
    mcums = []
    for d in range(2):
        t = np.broadcast_to(np.arange(c)[:, None], (c, w))
        s = np.broadcast_to((np.arange(w) % RW_HEAD)[None, :], (c, w))
        tt = np.broadcast_to(np.arange(c)[:, None], (c, c))
        ss = np.broadcast_to(np.arange(c)[None, :], (c, c))
        if d == 1:
            t, s, tt, ss = c - 1 - t, c - 1 - s, c - 1 - tt, c - 1 - ss
        rows = [s < t, s <= t, s == t, (t // 2 == s // 2) & (t > s)]
        sz = 2
        while sz < c:
            rows.append((t // (2 * sz) == s // (2 * sz)) & ((t // sz) % 2 == 1) & ((s // sz) % 2 == 0))
            sz *= 2
        assert len(rows) == M_LEVEL + N_LEVELS
        masks.append(np.stack(rows).astype(np.float32))
        mcums.append((ss <= tt).astype(np.float32))
    return jnp.asarray(np.stack(masks), F32), jnp.asarray(np.stack(mcums), BF16)


def _block_diag(y, bd16):
    g = y.shape[1] // RW_HEAD
    return jnp.concatenate([y.astype(BF16)] * g, axis=0) * bd16


def _bmm(x, y, bd16):
    outs = []
    for s in range(0, y.shape[1], HEAD_GROUP):
        outs.append(_dot(x[:, s:s + HEAD_GROUP].astype(BF16), _block_diag(y[:, s:s + HEAD_GROUP], bd16)))
    return jnp.concatenate(outs, axis=1)


def _bmm_nt(x, y, bd16):
    outs = []
    for s in range(0, y.shape[1], HEAD_GROUP):
        bd = _block_diag(y[:, s:s + HEAD_GROUP], bd16)
        outs.append(lax.dot_general(x[:, s:s + HEAD_GROUP].astype(BF16), bd, (((1,), (1,)), ((), ())),
                                    preferred_element_type=F32))
    return jnp.concatenate(outs, axis=1)


def _bmm_tn(x, y, bd32):
    outs = []
    for s in range(0, y.shape[1], HEAD_GROUP):
        xt = x[:, s:s + HEAD_GROUP].T.astype(BF16)
        full = _dot(xt, y[:, s:s + HEAD_GROUP].astype(BF16)) * bd32
        o = full[0:RW_HEAD]
        for j in range(1, HEAD_GROUP // RW_HEAD):
            o = o + full[j * RW_HEAD:(j + 1) * RW_HEAD]
        outs.append(o)
    return jnp.concatenate(outs, axis=1)


def _scan_chunks(r, lw, k, v, a, b, state, masks, mcum, bd16, bd32):
    c = r[0].shape[0]

    def each(f, *cols):
        return [f(*xs) for xs in zip(*cols)]

    cum = each(_dot_hilo_lhs, mcum, lw)
    tot = each(lambda x: jnp.sum(x, axis=0, keepdims=True), lw)
    e_neg = each(lambda x: jnp.exp(-x), cum)
    e_tail = each(lambda t_, x: jnp.exp(t_ - x), tot, cum)
    rt = each(lambda x, cm: x * jnp.exp(cm), r, cum)
    at = each(lambda x, cm, l_: x * jnp.exp(cm - l_), a, cum, lw)
    bt = each(jnp.multiply, b, e_neg)
    kt = each(jnp.multiply, k, e_neg)
    bh = each(jnp.multiply, b, e_tail)
    kh = each(jnp.multiply, k, e_tail)

    ar = each(lambda x, y_: jnp.concatenate([x, y_], axis=0), at, rt)
    pb = each(lambda x, y_: _bmm_nt(x, y_, bd16), ar, bt)
    pk = each(lambda x, y_: _bmm_nt(x, y_, bd16), ar, kt)
    a_ab = each(lambda x, m: x[:c] * m[M_STRICT], pb, masks)
    a_rb = each(lambda x, m: x[c:] * m[M_INCL], pb, masks)
    a_ak = each(lambda x, m: x[:c] * m[M_STRICT], pk, masks)
    a_rk = each(lambda x, m: x[c:] * m[M_INCL], pk, masks)

    tinv = each(lambda x, m: m[M_EYE] + x * m[M_PAIR], a_ab, masks)
    for lvl in range(N_LEVELS):
        x = each(lambda aa, m, ti: _bmm(aa * m[M_LEVEL + lvl], ti, bd16), a_ab, masks, tinv)
        tinv = each(lambda ti, x_: ti + _bmm(ti, x_, bd16), tinv, x)

    bmm = lambda p, q: _bmm(p, q, bd16)
    stack = lambda p, q: jnp.concatenate([p, q], axis=0)
    att = each(bmm, tinv, at)
    akv = each(lambda p, p2, q: _bmm(stack(p, p2), q, bd16), a_ak, a_rk, v)
    vt = each(lambda ti, x_: _bmm(ti, x_[:c], bd16), tinv, akv)
    rh = each(lambda x, p, q: x + _bmm(p, q, bd16), rt, a_rb, att)
    y_in = each(lambda p, q, x_: _bmm(p, q, bd16) + x_[c:], a_rb, vt, akv)
    mc = each(lambda m, t_, p, q: m[M_EYE] * jnp.exp(t_) + _bmm_tn(p, q, bd32), masks, tot, bh, att)
    nc = each(lambda p, p2, q, q2: _bmm_tn(stack(p, p2), stack(q, q2), bd32), bh, kh, vt, v)
    onst = each(lambda p, p2, q: _bmm(stack(p, p2), q, bd16), rh, mc, state)
    y = each(lambda x_, yi: x_[:c] + yi, onst, y_in)
    new_state = each(lambda x_, n_: x_[c:] + n_, onst, nc)
    return y, new_state


def _rwkv_scan_kernel(rf_ref, vf_ref, af_ref, rb_ref, vb_ref, ab_ref, lwf_ref, kf_ref, bf_ref, lwb_ref, kb_ref,
                      bb_ref, mask_ref, mcum_ref, bd16_ref, bd32_ref, yf_ref, yb_ref, state_ref, *, n_chunks):
    @pl.when(pl.program_id(1) == 0)
    def _():
        state_ref[...] = jnp.zeros_like(state_ref)

    dirs = ((0, rf_ref, lwf_ref, kf_ref, vf_ref, af_ref, bf_ref, yf_ref),
            (1, rb_ref, lwb_ref, kb_ref, vb_ref, ab_ref, bb_ref, yb_ref))
    n_seq = rf_ref.shape[0]

    def body(j, carry):
        cols = [[] for _ in range(9)]
        where = []
        for d, r_ref, lw_ref, k_ref, v_ref, a_ref, b_ref, y_ref in dirs:
            cj = j if d == 0 else n_chunks - 1 - j
            sl = pl.ds(pl.multiple_of(cj * CHUNK, CHUNK), CHUNK)
            for g in range(n_seq):
                vals = (r_ref[g, sl, :], lw_ref[g, sl, :], k_ref[g, sl, :], v_ref[g, sl, :], a_ref[g, sl, :],
                        b_ref[g, sl, :], state_ref[d, g], mask_ref.at[d], mcum_ref[d])
                for col, val in zip(cols, vals):
                    col.append(val)
                where.append((y_ref, d, g, sl))
        ys, sts = _scan_chunks(*cols, bd16_ref[...], bd32_ref[...])
        for (y_ref, d, g, sl), y, st in zip(where, ys, sts):
            y_ref[g, sl, :] = y.astype(y_ref.dtype)
            state_ref[d, g] = st
        return carry

    lax.fori_loop(0, n_chunks, body, 0)


def _rwkv_scan(r, v, nkk, lw, kd, bd, tb):
    b, t, w = r.shape
    nb = t // tb
    g = SCAN_SEQS if b % SCAN_SEQS == 0 else 1
    masks, mcum = _scan_tables(w)
    bd32 = _group_matrix(HEAD_GROUP, RW_HEAD, 1.0).astype(F32)
    bd16 = bd32.astype(BF16)
    fwd = lambda bi, i: (bi, i, 0)
    bwd = lambda bi, i: (bi, nb - 1 - i, 0)
    fwd_d = lambda bi, i: (0, bi, i, 0)
    bwd_d = lambda bi, i: (1, bi, nb - 1 - i, 0)
    const = lambda nd: (lambda bi, i: (0,) * nd)
    blk = (g, tb, w)
    dblk = (None, g, tb, w)
    out = jax.ShapeDtypeStruct((b, t, w), BF16)
    return pl.pallas_call(
        functools.partial(_rwkv_scan_kernel, n_chunks=tb // CHUNK),
        grid=(b // g, nb),
        in_specs=[
            pl.BlockSpec(blk, fwd), pl.BlockSpec(blk, fwd), pl.BlockSpec(blk, fwd),
            pl.BlockSpec(blk, bwd), pl.BlockSpec(blk, bwd), pl.BlockSpec(blk, bwd),
            pl.BlockSpec(dblk, fwd_d), pl.BlockSpec(dblk, fwd_d), pl.BlockSpec(dblk, fwd_d),
            pl.BlockSpec(dblk, bwd_d), pl.BlockSpec(dblk, bwd_d), pl.BlockSpec(dblk, bwd_d),
            pl.BlockSpec(masks.shape, const(4)),
            pl.BlockSpec(mcum.shape, const(3)),
            pl.BlockSpec(bd16.shape, const(2)),
            pl.BlockSpec(bd32.shape, const(2)),
        ],
        out_specs=[pl.BlockSpec(blk, fwd), pl.BlockSpec(blk, bwd)],
        out_shape=[out, out],
        scratch_shapes=[pltpu.VMEM((2, g, CHUNK, w), F32)],
        compiler_params=_cparams(("parallel", "arbitrary")),
        name="rwkv_scan",
    )(r, v, nkk, r, v, nkk, lw, kd, bd, lw, kd, bd, masks, mcum, bd16, bd32)


def _attn_kernel(slope_ref, q_ref, k_ref, v_ref, lq1_ref, lk1_ref, lq2_ref, lk2_ref, sg_ref, o_ref,
                 m_ref, l_ref, acc_ref, *, qk_dim, lambda_init):
    h = pl.program_id(1)
    i = pl.program_id(2)
    j = pl.program_id(3)
    nk = pl.num_programs(3)
    tq = q_ref.shape[0]
    tk = k_ref.shape[0]

    @pl.when(j == 0)
    def _():
        m_ref[...] = jnp.full(m_ref.shape, -jnp.inf, F32)
        l_ref[...] = jnp.zeros_like(l_ref)
        acc_ref[...] = jnp.zeros_like(acc_ref)

    q = q_ref[...]
    k = k_ref[...]
    v = v_ref[...]
    lane = lax.broadcasted_iota(jnp.int32, q.shape, 1)
    zero = jnp.zeros_like(q)
    qi = lax.broadcasted_iota(jnp.int32, (tq, tk), 0) + i * tq
    kj = lax.broadcasted_iota(jnp.int32, (tq, tk), 1) + j * tk
    bias = slope_ref[h] * jnp.abs(qi - kj).astype(F32)
    for c in range(2):
        qc = jnp.where((lane < qk_dim) == (c == 0), q, zero)
        s = lax.dot_general(qc, k, (((1,), (1,)), ((), ())), preferred_element_type=F32) - bias
        m_old = m_ref[c]
        m_new = jnp.maximum(m_old, jnp.max(s, axis=-1, keepdims=True))
        alpha = jnp.exp(m_old - m_new)
        p = jnp.exp(s - m_new)
        l_ref[c] = alpha * l_ref[c] + jnp.sum(p, axis=-1, keepdims=True)
        acc_ref[c] = alpha * acc_ref[c] + _dot(p.astype(BF16), v)
        m_ref[c] = m_new

    @pl.when(j == nk - 1)
    def _():
        lam = (jnp.exp(jnp.sum(lq1_ref[...] * lk1_ref[...], axis=-1, keepdims=True))
               - jnp.exp(jnp.sum(lq2_ref[...] * lk2_ref[...], axis=-1, keepdims=True)) + lambda_init)
        o = acc_ref[0] / l_ref[0] - lam * (acc_ref[1] / l_ref[1])
        ms = jnp.mean(o * o, axis=-1, keepdims=True)
        o_ref[...] = o * lax.rsqrt(ms + NORM_EPS) * (sg_ref[...] * (1.0 - lambda_init))


def _attention(slopes, q3, k3, v3, lq1, lk1, lq2, lk2, subln_g, heads, v_dim, lambda_init, tq, tk):
    b, t, _ = q3.shape
    const = lambda bi, h, i, j, s: (0, 0)
    grid_spec = pltpu.PrefetchScalarGridSpec(
        num_scalar_prefetch=1,
        grid=(b, heads, t // tq, t // tk),
        in_specs=[
            pl.BlockSpec((None, tq, v_dim), lambda bi, h, i, j, s: (bi, i, h)),
            pl.BlockSpec((None, tk, v_dim), lambda bi, h, i, j, s: (bi, j, h)),
            pl.BlockSpec((None, tk, v_dim), lambda bi, h, i, j, s: (bi, j, h)),
            pl.BlockSpec(lq1.shape, const),
            pl.BlockSpec(lk1.shape, const),
            pl.BlockSpec(lq2.shape, const),
            pl.BlockSpec(lk2.shape, const),
            pl.BlockSpec(subln_g.shape, const),
        ],
        out_specs=pl.BlockSpec((None, tq, v_dim), lambda bi, h, i, j, s: (bi, i, h)),
        scratch_shapes=[
            pltpu.VMEM((2, tq, 1), F32),
            pltpu.VMEM((2, tq, 1), F32),
            pltpu.VMEM((2, tq, v_dim), F32),
        ],
    )
    return pl.pallas_call(
        functools.partial(_attn_kernel, qk_dim=v_dim // 2, lambda_init=lambda_init),
        grid_spec=grid_spec,
        out_shape=jax.ShapeDtypeStruct((b, t, heads * v_dim), F32),
        compiler_params=_cparams(("parallel", "parallel", "parallel", "arbitrary")),
        name="attn",
    )(slopes, q3, k3, v3, lq1, lk1, lq2, lk2, subln_g)


EXP_ZERO_ARG = 104.0
MAX_FIXED_SHIFT = 40.0
N_AUG = 6
ATTN_BLOCK = 512
ATTN_GROUP_SIZES = (8, 4, 2, 1)
ATTN_KEY_SPLIT = 1
ATTN_V_PAD = 16


def _attn_fixed_kernel(w_ref, slope_ref, mb_ref, q_ref, k_ref, vt_ref, dbias_ref, qaug_ref, kaug_ref,
                       lq1_ref, lk1_ref, lq2_ref, lk2_ref, sg_ref, o_ref, acc_ref, *, qk_dim, lambda_init):
    h = pl.program_id(1)
    i = pl.program_id(2)
    tb = q_ref.shape[0]
    nk = k_ref.shape[0] // tb
    v_dim = 2 * qk_dim
    slope = slope_ref[h]
    mb = mb_ref[0]
    w = w_ref[h]

    q = q_ref[...]
    lane = lax.broadcasted_iota(jnp.int32, q.shape, 1)
    lane_row = lax.broadcasted_iota(jnp.int32, (1, q.shape[1]), 1)
    is_c0 = lane < qk_dim
    datas = (is_c0, jnp.logical_not(is_c0))

    def blocks(items, qvar, kvar, diag):
        sub = tb // ATTN_KEY_SPLIT
        qas = [jnp.where(datas[c], q, qaug_ref[qvar, c]) for c in range(2)]
        chains = []
        for n, (j, delta) in enumerate(items):
            kblk = k_ref[pl.ds(pl.multiple_of(j * tb, tb), tb), :]
            vtblk = vt_ref[j]
            for c in range(2):
                base = qk_dim * (1 - c)
                dyn = jnp.where(lane_row == base + 4, -mb,
                                jnp.where(lane_row == base + 5, -(slope * tb) * delta, 0.0)).astype(BF16)
                ka = jnp.where(datas[c], kblk, kaug_ref[kvar, c] + dyn)
                for u in range(ATTN_KEY_SPLIT):
                    rows = slice(u * sub, (u + 1) * sub)
                    chains.append((c, u, qas[c], ka[rows], vtblk[:, rows]))
        sts = [lax.dot_general(ka, qa, (((1,), (1,)), ((), ())), preferred_element_type=F32)
               for _, _, qa, ka, _ in chains]
        if diag:
            sts = [st - dbias_ref[u * sub:(u + 1) * sub, :] for st, (_, u, _, _, _) in zip(sts, chains)]
        pts = [jnp.exp(st).astype(BF16) for st in sts]
        for c in range(2):
            upd = None
            for pt, (cc, _, _, _, vsub) in zip(pts, chains):
                if cc == c:
                    part = _dot(vsub, pt)
                    upd = part if upd is None else upd + part
            if diag:
                acc_ref[c] = upd
            else:
                acc_ref[c] += upd

    def span(first, count, qvar, kvar, dist):
        for per in ATTN_GROUP_SIZES:
            def group(g, carry, first=first, per=per):
                j0 = first + g * per
                blocks([(j0 + n, dist(j0 + n)) for n in range(per)], qvar, kvar, False)
                return carry

            groups = count // per
            lax.fori_loop(0, groups, group, 0)
            first = first + groups * per
            count = count - groups * per

    lo = jnp.maximum(i - w, 0)
    hi = jnp.minimum(i + w, nk - 1)
    blocks([(i, jnp.zeros((), F32))], 2, 0, True)
    span(lo, i - lo, 0, 0, lambda j: (i - j).astype(F32))
    span(i + 1, hi - i, 1, 1, lambda j: (j - i).astype(F32))

    lam = (jnp.exp(jnp.sum(lq1_ref[...] * lk1_ref[...], axis=-1, keepdims=True))
           - jnp.exp(jnp.sum(lq2_ref[...] * lk2_ref[...], axis=-1, keepdims=True)) + lambda_init)
    a0 = acc_ref[0]
    a1 = acc_ref[1]
    inv0 = 1.0 / a0[v_dim:v_dim + 1]
    inv1 = lam / a1[v_dim:v_dim + 1]
    ot = a0[:v_dim] * inv0 - a1[:v_dim] * inv1
    ms = jnp.mean(ot * ot, axis=0, keepdims=True)
    o_ref[...] = (ot * lax.rsqrt(ms + NORM_EPS) * (sg_ref[...] * (1.0 - lambda_init))).T.astype(o_ref.dtype)


def _attn_aug_tables(slopes, tb, qk_dim):
    heads = len(slopes)
    lanes = 2 * qk_dim
    pos = np.arange(tb)
    qaug = np.zeros((heads, 3, 2, tb, lanes), np.float32)
    kaug = np.zeros((heads, 2, 2, tb, lanes), np.float32)
    for hd, sl in enumerate(slopes):
        for var in range(3):
            rr = pos if var != 1 else tb - 1 - pos
            hi = (rr // 16) * (16.0 * sl)
            lo = (rr % 16) * sl
            for c in range(2):
                base = qk_dim * (1 - c)
                if var == 2:
                    qaug[hd, var, c, :, base + 4] = 1.0
                    continue
                qaug[hd, var, c, :, base + 0] = -hi
                qaug[hd, var, c, :, base + 1] = -lo
                qaug[hd, var, c, :, base + 2:base + N_AUG] = 1.0
                kaug[hd, var, c, :, base + 0:base + 2] = 1.0
                kaug[hd, var, c, :, base + 2] = hi
                kaug[hd, var, c, :, base + 3] = lo
    return jnp.asarray(qaug, BF16), jnp.asarray(kaug, BF16)


def _attention_fixed(mb, q3, k3, vt, lq1, lk1, lq2, lk2, subln_g, heads, v_dim, lambda_init, tb):
    b, t, _ = q3.shape
    qk_dim = v_dim // 2
    slopes = [2.0 ** (-8.0 * (hd + 1) / heads) for hd in range(heads)]
    widths = [int(math.ceil((EXP_ZERO_ARG / s - 1.0) / tb)) for s in slopes]
    qaug, kaug = _attn_aug_tables(slopes, tb, qk_dim)
    pos = np.arange(tb)
    dist = np.abs(pos[:, None] - pos[None, :]).astype(np.float32)
    dbias = jnp.asarray(np.stack([dist * s for s in slopes]), F32)
    nb = t // tb
    vrows = v_dim + ATTN_V_PAD
    assert vt.shape == (b, heads, nb, vrows, tb)
    const = lambda bi, h, i, *_: (0, 0)
    grid_spec = pltpu.PrefetchScalarGridSpec(
        num_scalar_prefetch=3,
        grid=(b, heads, t // tb),
        in_specs=[
            pl.BlockSpec((None, tb, v_dim), lambda bi, h, i, *_: (bi, i, h)),
            pl.BlockSpec((None, t, v_dim), lambda bi, h, i, *_: (bi, 0, h)),
            pl.BlockSpec((None, None, nb, vrows, tb), lambda bi, h, i, *_: (bi, h, 0, 0, 0)),
            pl.BlockSpec((None, tb, tb), lambda bi, h, i, *_: (h, 0, 0)),
            pl.BlockSpec((None, 3, 2, tb, v_dim), lambda bi, h, i, *_: (h, 0, 0, 0, 0)),
            pl.BlockSpec((None, 2, 2, tb, v_dim), lambda bi, h, i, *_: (h, 0, 0, 0, 0)),
            pl.BlockSpec(lq1.shape, const),
            pl.BlockSpec(lk1.shape, const),
            pl.BlockSpec(lq2.shape, const),
            pl.BlockSpec(lk2.shape, const),
            pl.BlockSpec((v_dim, 1), const),
        ],
        out_specs=pl.BlockSpec((None, tb, v_dim), lambda bi, h, i, *_: (bi, i, h)),
        scratch_shapes=[pltpu.VMEM((2, vrows, tb), F32)],
    )
    return pl.pallas_call(
        functools.partial(_attn_fixed_kernel, qk_dim=qk_dim, lambda_init=lambda_init),
        grid_spec=grid_spec,
        out_shape=jax.ShapeDtypeStruct((b, t, heads * v_dim), BF16),
        compiler_params=_cparams(("parallel", "parallel", "parallel")),
        name="attn_fixed",
    )(jnp.asarray(widths, jnp.int32), jnp.asarray(slopes, F32), mb.reshape(1).astype(F32),
      q3, k3, vt, dbias, qaug, kaug, lq1, lk1, lq2, lk2, subln_g.reshape(v_dim, 1))


def _post_kernel(yf_ref, yb_ref, bonus_ref, g_ref, oda_ref, x_ref, lng_ref, lnb_ref, gmean_ref, wo_rw_ref, wo_da_ref,
                 gffn_ref, wr_ref, br_ref, h_ref, m_ref, idx_ref, gate_ref, cnt_ref, *, gn_eps):
    @pl.when(pl.program_id(0) == 0)
    def _():
        cnt_ref[...] = jnp.zeros_like(cnt_ref)

    gmean = gmean_ref[...]
    y = yf_ref[...].astype(F32) + yb_ref[...].astype(F32)
    mean = _dot_hilo_rhs(y, gmean)
    yc = y - mean
    var = _dot_hilo_rhs(yc * yc, gmean)
    yn = yc * lax.rsqrt(var + gn_eps) * lng_ref[...] + lnb_ref[...]
    o_rw = (yn + bonus_ref[...]) * g_ref[...]
    h = x_ref[...] + _dot(o_rw.astype(BF16), wo_rw_ref[...]) + _dot(oda_ref[...].astype(BF16), wo_da_ref[...])
    h_ref[...] = h
    ms = jnp.mean(h * h, axis=-1, keepdims=True)
    m = h * lax.rsqrt(ms + NORM_EPS) * gffn_ref[...]
    _store_rows(m_ref, m)

    logits = _dot_hilo(m, wr_ref[...]) + br_ref[...]
    ne = logits.shape[1]
    lane = lax.broadcasted_iota(jnp.int32, logits.shape, 1).astype(F32)
    kcol = lax.broadcasted_iota(jnp.int32, (logits.shape[0], TOP_K), 1)
    idx_all = jnp.zeros((logits.shape[0], TOP_K), F32)
    val_all = jnp.zeros((logits.shape[0], TOP_K), F32)
    work = logits
    chosen = jnp.zeros(logits.shape, F32)
    for kk in range(TOP_K):
        mx = jnp.max(work, axis=-1, keepdims=True)
        sel = jnp.min(jnp.where(work == mx, lane, float(ne)), axis=-1, keepdims=True)
        idx_all = jnp.where(kcol == kk, sel, idx_all)
        val_all = jnp.where(kcol == kk, mx, val_all)
        hit = lane == sel
        chosen = jnp.where(hit, 1.0, chosen)
        work = jnp.where(hit, -jnp.inf, work)
    e = jnp.exp(val_all - jnp.max(val_all, axis=-1, keepdims=True))
    idx_ref[...] = idx_all.astype(jnp.int32)
    gate_ref[...] = e / jnp.sum(e, axis=-1, keepdims=True)
    cnt_ref[...] += jnp.sum(chosen, axis=0, keepdims=True)


def _post(yf, yb, bonus, g, oda, x2, lng, lnb, gmean, wo_rw, wo_da, gffn, wr, br, gn_eps, tm):
    n, c = yf.shape
    d = x2.shape[1]
    ne = wr.shape[1]
    full = lambda i: (0, 0)
    row = lambda i: (i, 0)
    return pl.pallas_call(
        functools.partial(_post_kernel, gn_eps=gn_eps),
        grid=(n // tm,),
        in_specs=[
            pl.BlockSpec((tm, c), row),
            pl.BlockSpec((tm, c), row),
            pl.BlockSpec((tm, c), row),
            pl.BlockSpec((tm, c), row),
            pl.BlockSpec((tm, c), row),
            pl.BlockSpec((tm, d), row),
            pl.BlockSpec((1, c), full),
            pl.BlockSpec((1, c), full),
            pl.BlockSpec((c, c), full),
            pl.BlockSpec((c, d), full),
            pl.BlockSpec((c, d), full),
            pl.BlockSpec((1, d), full),
            pl.BlockSpec((d, ne), full),
            pl.BlockSpec((1, ne), full),
        ],
        out_specs=[
            pl.BlockSpec((tm, d), row),
            pl.BlockSpec((tm * d // V7X_LANES, V7X_LANES), row),
            pl.BlockSpec((tm, TOP_K), row),
            pl.BlockSpec((tm, TOP_K), row),
            pl.BlockSpec((1, ne), full),
        ],
        out_shape=[
            jax.ShapeDtypeStruct((n, d), F32),
            jax.ShapeDtypeStruct((n * d // V7X_LANES, V7X_LANES), F32),
            jax.ShapeDtypeStruct((n, TOP_K), jnp.int32),
            jax.ShapeDtypeStruct((n, TOP_K), F32),
            jax.ShapeDtypeStruct((1, ne), F32),
        ],
        compiler_params=_cparams(("arbitrary",)),
        name="post",
    )(yf, yb, bonus, g, oda, x2, lng, lnb, gmean, wo_rw, wo_da, gffn, wr, br)


def _rank_kernel(idx_ref, ltri_ref, offs_ref, slot_ref, carry_ref):
    @pl.when(pl.program_id(0) == 0)
    def _():
        carry_ref[...] = offs_ref[...]

    idx = idx_ref[...]
    tr = idx.shape[0]
    lane = lax.broadcasted_iota(jnp.int32, (tr, N_EXPERTS), 1)
    hots = [jnp.where(lane == idx[:, kk:kk + 1], 1.0, 0.0) for kk in range(TOP_K)]
    mask = hots[0]
    for kk in range(1, TOP_K):
        mask = mask + hots[kk]
    before = _dot(ltri_ref[...], mask.astype(BF16)) + carry_ref[...]
    kcol = lax.broadcasted_iota(jnp.int32, (tr, TOP_K), 1)
    slot = jnp.zeros((tr, TOP_K), F32)
    for kk in range(TOP_K):
        rk = jnp.sum(hots[kk] * before, axis=-1, keepdims=True)
        slot = jnp.where(kcol == kk, rk, slot)
    slot_ref[...] = slot.astype(jnp.int32)
    carry_ref[...] = carry_ref[...] + jnp.sum(mask, axis=0, keepdims=True)


def _moe_rank(idx, offs_row, tr):
    n = idx.shape[0]
    ltri = jnp.asarray(np.tril(np.ones((tr, tr), np.float32), -1), BF16)
    return pl.pallas_call(
        _rank_kernel,
        grid=(n // tr,),
        in_specs=[pl.BlockSpec((tr, TOP_K), lambda i: (i, 0)), pl.BlockSpec((tr, tr), lambda i: (0, 0)),
                  pl.BlockSpec((1, N_EXPERTS), lambda i: (0, 0))],
        out_specs=pl.BlockSpec((tr, TOP_K), lambda i: (i, 0)),
        out_shape=jax.ShapeDtypeStruct((n, TOP_K), jnp.int32),
        scratch_shapes=[pltpu.VMEM((1, N_EXPERTS), F32)],
        compiler_params=_cparams(("arbitrary",)),
        name="moe_rank",
    )(idx, ltri, offs_row)


ROW_UNROLL = 8


def _row(r, c):
    return pl.ds(pl.multiple_of(r * c, c), c)


PLE_ROWS = 256


def _dispatch_kernel(slot_ref, m_ref, p_ref, wp_ref, gp_ref, xs_ref, e_ref, sem, *, c):
    td = m_ref.shape[0] // c

    def issue(g, carry):
        for u in range(ROW_UNROLL):
            t = g * ROW_UNROLL + u
            for kk in range(TOP_K):
                dest = slot_ref[t * TOP_K + kk]
                pltpu.make_async_copy(m_ref.at[_row(t, c), :], xs_ref.at[_row(dest, c), :], sem).start(
                    priority=kk % 2)
        return carry

    lax.fori_loop(0, td // ROW_UNROLL, issue, 0)

    rows = min(PLE_ROWS, td)
    for s in range(0, td, rows):
        e = _dot(p_ref[s:s + rows, :].astype(BF16), wp_ref[...])
        ms = jnp.mean(e * e, axis=-1, keepdims=True)
        e_ref[s:s + rows, :] = (e * lax.rsqrt(ms + NORM_EPS) * gp_ref[...]).astype(e_ref.dtype)

    for kk in range(TOP_K):
        pltpu.make_async_copy(m_ref, xs_ref.at[pl.ds(0, td * c), :], sem).wait()


def _dispatch(slot_flat, mrows, c, p_rows, p2, wp_bf, gp, td):
    n = mrows.shape[0] // c
    pd, d = wp_bf.shape
    assert td % ROW_UNROLL == 0
    full = lambda i: (0, 0)
    return pl.pallas_call(
        functools.partial(_dispatch_kernel, c=c),
        grid=(n // td,),
        in_specs=[
            pl.BlockSpec((td * TOP_K,), lambda i: (i,), memory_space=pltpu.SMEM),
            pl.BlockSpec((td * c, V7X_LANES), lambda i: (i, 0)),
            pl.BlockSpec((td, pd), lambda i: (i, 0)),
            pl.BlockSpec((pd, d), full),
            pl.BlockSpec((1, d), full),
        ],
        out_specs=[pl.BlockSpec(memory_space=pl.ANY), pl.BlockSpec((td, d), lambda i: (i, 0))],
        scratch_shapes=[pltpu.SemaphoreType.DMA(())],
        out_shape=[jax.ShapeDtypeStruct((p_rows * c, V7X_LANES), F32), jax.ShapeDtypeStruct((n, d), BF16)],
        compiler_params=_cparams(("arbitrary",), has_side_effects=True),
        name="moe_dispatch",
    )(slot_flat, mrows, p2, wp_bf, gp)


def _ffn_kernel(be_ref, nu_ref, chg_ref, par_ref, nxt_ref, x_ref, w1_ref, b1_ref, w2_ref, b2_ref, y_ref,
                w1f_ref, w2f_ref, w1s_ref, w2s_ref, sems, *, d_ff):
    i = pl.program_id(0)

    def weight_copies(e, slot):
        return (pltpu.make_async_copy(w1_ref.at[e], w1f_ref.at[slot], sems.at[0, slot]),
                pltpu.make_async_copy(w2_ref.at[e], w2f_ref.at[slot], sems.at[1, slot]))

    @pl.when(i == 0)
    def _():
        for cp in weight_copies(be_ref[0], 0):
            cp.start()

    @pl.when(chg_ref[i] == 1)
    def _():
        slot = par_ref[i]
        for cp in weight_copies(be_ref[i], slot):
            cp.wait()
        w1s_ref[...] = w1f_ref[slot].astype(BF16)
        w2s_ref[...] = w2f_ref[slot].astype(BF16)

        @pl.when(nxt_ref[i] >= 0)
        def _():
            for cp in weight_copies(nxt_ref[i], 1 - slot):
                cp.start()

    @pl.when(i < nu_ref[0])
    def _():
        x = _load_rows(x_ref, w1s_ref.shape[0] // V7X_LANES).astype(BF16)
        hcat = _dot(x, w1s_ref[...]) + b1_ref[...]
        glu = jnp.minimum(hcat[:, :d_ff], SWIGLU_LIMIT)
        lin = jnp.clip(hcat[:, d_ff:], -SWIGLU_LIMIT, SWIGLU_LIMIT)
        act = glu * jax.nn.sigmoid(SWIGLU_ALPHA * glu) * (lin + 1.0)
        _store_rows(y_ref, _dot(act.astype(BF16), w2s_ref[...]) + b2_ref[...])


def _moe_ffn(blk_e, n_used, xrows, w1, b1, w2, b2, bm):
    ne, d, f2 = w1.shape
    d_ff = f2 // 2
    c = d // V7X_LANES
    p_rows = xrows.shape[0] // c
    blk = (bm * c, V7X_LANES)
    n_blk = p_rows // bm
    pos = jnp.arange(n_blk, dtype=jnp.int32)
    prev_e = jnp.concatenate([jnp.full((1,), -1, jnp.int32), blk_e[:-1]])
    chg = ((pos < n_used[0]) & (blk_e != prev_e)).astype(jnp.int32)
    par = ((jnp.cumsum(chg) - 1) % 2).astype(jnp.int32)
    later = jnp.where(chg == 1, pos, n_blk)
    nextpos = jnp.concatenate([lax.cummin(later, reverse=True)[1:], jnp.full((1,), n_blk, jnp.int32)])
    nxt = jnp.where(nextpos < n_blk, blk_e[jnp.minimum(nextpos, n_blk - 1)], -1).astype(jnp.int32)
    grid_spec = pltpu.PrefetchScalarGridSpec(
        num_scalar_prefetch=5,
        grid=(n_blk,),
        in_specs=[
            pl.BlockSpec(blk, lambda i, be, nu, *_: (jnp.minimum(i, nu[0] - 1), 0)),
            pl.BlockSpec(memory_space=pl.ANY),
            pl.BlockSpec((None, 1, f2), lambda i, be, nu, *_: (be[i], 0, 0)),
            pl.BlockSpec(memory_space=pl.ANY),
            pl.BlockSpec((None, 1, d), lambda i, be, nu, *_: (be[i], 0, 0)),
        ],
        out_specs=pl.BlockSpec(blk, lambda i, be, nu, *_: (i, 0)),
        scratch_shapes=[pltpu.VMEM((2, d, f2), F32), pltpu.VMEM((2, d_ff, d), F32),
                        pltpu.VMEM((d, f2), BF16), pltpu.VMEM((d_ff, d), BF16),
                        pltpu.SemaphoreType.DMA((2, 2))],
    )
    return pl.pallas_call(
        functools.partial(_ffn_kernel, d_ff=d_ff),
        grid_spec=grid_spec,
        out_shape=jax.ShapeDtypeStruct(xrows.shape, F32),
        compiler_params=_cparams(("arbitrary",)),
        name="moe_ffn",
    )(blk_e, n_used, chg, par, nxt, xrows, w1, b1.reshape(ne, 1, f2), w2, b2.reshape(ne, 1, d))


def _combine_kernel(slot_ref, next_slot_ref, h_ref, gate_ref, e_ref, wg_ref, y_ref, o_ref,
                    buf_ref, h2_ref, sems):
    i = pl.program_id(0)
    tc, d = h_ref.shape
    c = d // V7X_LANES

    def start_gather(slots, half):
        def issue(g, carry):
            for u in range(ROW_UNROLL):
                t = g * ROW_UNROLL + u
                for kk in range(TOP_K):
                    src = slots[t * TOP_K + kk]
                    pltpu.make_async_copy(y_ref.at[_row(src, c), :], buf_ref.at[half, kk, _row(t, c), :],
                                          sems.at[half]).start(priority=kk % 2)
            return carry

        lax.fori_loop(0, tc // ROW_UNROLL, issue, 0)

    cur = i % 2

    @pl.when(i == 0)
    def _():
        start_gather(slot_ref, 0)

    @pl.when(i + 1 < pl.num_programs(0))
    def _():
        start_gather(next_slot_ref, 1 - cur)

    for kk in range(TOP_K):
        pltpu.make_async_copy(y_ref.at[pl.ds(0, tc * c), :], buf_ref.at[cur, kk], sems.at[cur]).wait()

    gate = gate_ref[...]
    for j in range(c):
        cols = slice(j * V7X_LANES, (j + 1) * V7X_LANES)
        acc = h_ref[:, cols]
        for kk in range(TOP_K):
            acc = acc + gate[:, kk:kk + 1] * buf_ref[cur, kk, pl.ds(j, tc, stride=c), :]
        h2_ref[:, cols] = acc

    h2 = h2_ref[...]
    o_ref[...] = h2 + e_ref[...] * jax.nn.sigmoid(_dot(h2.astype(BF16), wg_ref[...]))


def _combine_ple(slot_flat, h, gates, yrows, e_bf, wg_bf, tc):
    n, d = h.shape
    assert tc % ROW_UNROLL == 0
    last = n // tc - 1
    full = lambda i: (0, 0)
    row = lambda i: (i, 0)
    return pl.pallas_call(
        _combine_kernel,
        grid=(n // tc,),
        in_specs=[
            pl.BlockSpec((tc * TOP_K,), lambda i: (i,), memory_space=pltpu.SMEM),
            pl.BlockSpec((tc * TOP_K,), lambda i: (jnp.minimum(i + 1, last),), memory_space=pltpu.SMEM),
            pl.BlockSpec((tc, d), row),
            pl.BlockSpec((tc, TOP_K), row),
            pl.BlockSpec((tc, d), row),
            pl.BlockSpec((d, d), full),
            pl.BlockSpec(memory_space=pl.ANY),
        ],
        out_specs=pl.BlockSpec((tc, d), row),
        scratch_shapes=[pltpu.VMEM((2, TOP_K, tc * d // V7X_LANES, V7X_LANES), F32),
                        pltpu.VMEM((tc, d), F32),
                        pltpu.SemaphoreType.DMA((2,))],
        out_shape=jax.ShapeDtypeStruct((n, d), F32),
        compiler_params=_cparams(("arbitrary",)),
        name="moe_combine_ple",
    )(slot_flat, slot_flat, h, gates, e_bf, wg_bf, yrows)


def _tile(n, want):
    t = min(n, want)
    assert n % t == 0, (n, t)
    return t


def _layer(h3, p3, lambda_init, norm_mix_g, w_in, rw_mu, rw_w0, rw_w2, rw_a0, rw_a2, rw_g2, rw_k_k, rw_k_a, rw_r_k,
           rw_ln_g, rw_ln_b, da_q_norm_g, da_k_norm_g, da_lq1, da_lk1, da_lq2, da_lk2, da_subln_g, w_out,
           norm_ffn_g, w_router, b_router, w1, b1, w2, b2, w_ple, ple_norm_g, w_ple_gate):
    b, t, d = h3.shape
    n = b * t
    c_rw = rw_k_k.shape[0]
    lora = rw_w2.shape[1]
    rw_cols = rw_mu.shape[0]
    rw_heads = rw_r_k.shape[0]
    c_da = w_in.shape[1] - rw_cols
    assert c_da % 3 == 0
    c_da //= 3
    qk_dim = da_q_norm_g.shape[0]
    v_dim = da_subln_g.shape[0]
    da_heads = c_da // v_dim
    assert rw_heads * RW_HEAD == c_rw and c_rw % HEAD_GROUP == 0 and v_dim == V7X_LANES and 2 * qk_dim == v_dim
    assert 2 * lora == V7X_LANES and w_router.shape[1] == N_EXPERTS

    x2 = h3.reshape(n, d)
    row = lambda a: a.reshape(1, -1).astype(F32)

    gq_t = row(jnp.tile(da_q_norm_g, c_da // qk_dim))
    gk_t = row(jnp.tile(da_k_norm_g, c_da // qk_dim))
    gm_qk = _group_matrix(c_da, qk_dim, 1.0 / qk_dim)
    ta = _tile(t, ATTN_BLOCK)
    zrw, q, k, vt = _inproj(x2, row(norm_mix_g), w_in.astype(BF16), gq_t, gk_t, gm_qk, rw_cols, c_da,
                            1.0 / math.sqrt(qk_dim), _tile(ta, ROW_TILE), b, da_heads, ta)

    zeros = jnp.zeros((lora, c_rw), F32)
    w2bd = jnp.concatenate([jnp.concatenate([rw_w2[0], zeros], axis=1),
                            jnp.concatenate([zeros, rw_w2[1]], axis=1)], axis=0).astype(BF16)
    a2bd = jnp.concatenate([jnp.concatenate([rw_a2[0], zeros], axis=1),
                            jnp.concatenate([zeros, rw_a2[1]], axis=1)], axis=0).astype(BF16)
    gsum = _group_matrix(c_rw, RW_HEAD, 1.0)
    r, vv, nkk, lw, kd, bd, g, bonus = _rwkv_prep(
        zrw.reshape(b, t, rw_cols), row(rw_mu), w2bd, row(rw_w0), a2bd, row(rw_a0), rw_g2.astype(BF16),
        row(rw_k_k), row(rw_k_a), row(rw_r_k), gsum, c_rw, lora, _tile(t, ROW_TILE))
    yf, yb = _rwkv_scan(r, vv, nkk, lw, kd, bd, _tile(t, SCAN_TILE))

    slopes = 2.0 ** (-8.0 * jnp.arange(1, da_heads + 1, dtype=F32) / da_heads)
    q3 = q.reshape(b, t, c_da)
    k3 = k.reshape(b, t, c_da)
    small = (row(da_lq1), row(da_lk1), row(da_lq2), row(da_lk2), row(da_subln_g))

    def attention_running_max():
        v3 = vt[:, :, :, :v_dim, :].transpose(0, 2, 4, 1, 3).reshape(b, t, c_da)
        return _attention(slopes, q3, k3, v3, *small, da_heads, v_dim, lambda_init, ta, ta)

    score_bound = 1.01 * math.sqrt(qk_dim) * jnp.max(jnp.abs(da_q_norm_g)) * jnp.max(jnp.abs(da_k_norm_g))
    mb = jnp.ceil(2.0 * score_bound.astype(F32)) * 0.5
    o_da = lax.cond(
        mb <= MAX_FIXED_SHIFT,
        lambda: _attention_fixed(mb, q3, k3, vt, *small, da_heads, v_dim, lambda_init, ta),
        lambda: attention_running_max().astype(BF16))

    gmean = _group_matrix(c_rw, RW_HEAD, 1.0 / RW_HEAD)
    w_out_bf = w_out.astype(BF16)
    h1, m, idx, gates, counts = _post(
        yf.reshape(n, c_rw), yb.reshape(n, c_rw), bonus.reshape(n, c_rw), g.reshape(n, c_rw), o_da.reshape(n, c_da), x2,
        row(rw_ln_g), row(rw_ln_b), gmean, w_out_bf[:c_rw], w_out_bf[c_rw:], row(norm_ffn_g),
        w_router.astype(F32), row(b_router), RW_HEAD * 1e-5, _tile(n, WIDE_TILE))

    bm = MOE_BLOCK_ROWS
    counts = counts.reshape(N_EXPERTS).astype(jnp.int32)
    padded = (counts + bm - 1) // bm * bm
    pend = jnp.cumsum(padded)
    offs = (pend - padded).astype(jnp.int32)
    slot = _moe_rank(idx, offs.astype(F32).reshape(1, N_EXPERTS), _tile(n, ROW_TILE))
    n_blk = (n * TOP_K) // bm + N_EXPERTS
    blk_start = jnp.arange(n_blk, dtype=jnp.int32) * bm
    blk_e = jnp.minimum(jnp.sum(blk_start[:, None] >= pend[None, :], axis=1), N_EXPERTS - 1).astype(jnp.int32)
    n_used = (pend[-1] // bm).astype(jnp.int32).reshape(1)
    slot_flat = slot.reshape(n * TOP_K)
    xs, e_ple = _dispatch(slot_flat, m, d // V7X_LANES, n_blk * bm, p3.reshape(n, -1), w_ple.astype(BF16),
                          row(ple_norm_g), _tile(n, DISPATCH_TILE))
    ys = _moe_ffn(blk_e, n_used, xs, w1, b1, w2, b2, bm)
    out = _combine_ple(slot_flat, h1, gates, ys, e_ple, w_ple_gate.astype(BF16), _tile(n, COMBINE_TILE))
    return out.reshape(b, t, d)


def kernel(x, p, norm_mix_g, w_in, rw_mu, rw_w0, rw_w2, rw_a0, rw_a2, rw_g2, rw_k_k, rw_k_a, rw_r_k, rw_ln_g,
           rw_ln_b, da_q_norm_g, da_k_norm_g, da_lq1, da_lk1, da_lq2, da_lk2, da_subln_g, w_out, norm_ffn_g,
           w_router, b_router, w1, b1, w2, b2, w_ple, ple_norm_g, w_ple_gate):
    h = x.astype(F32)
    params = (norm_mix_g, w_in, rw_mu, rw_w0, rw_w2, rw_a0, rw_a2, rw_g2, rw_k_k, rw_k_a, rw_r_k, rw_ln_g, rw_ln_b,
              da_q_norm_g, da_k_norm_g, da_lq1, da_lk1, da_lq2, da_lk2, da_subln_g, w_out, norm_ffn_g,
              w_router, b_router, w1, b1, w2, b2, w_ple, ple_norm_g, w_ple_gate)
    for i in range(p.shape[0]):
        lambda_init = 0.8 - 0.6 * math.exp(-0.3 * i)
        h = _layer(h, p[i], lambda_init, *(a[i] for a in params))
    return h.astype(x.dtype)
```

```python
import functools
import math

import jax
import jax.numpy as jnp
import numpy as np
from jax import lax
from jax.experimental import pallas as pl
from jax.experimental.pallas import tpu as pltpu

F32 = jnp.float32
BF16 = jnp.bfloat16

V7X_LANES = 128
V7X_VMEM_BYTES = 64 * 1024 * 1024
VMEM_LIMIT = V7X_VMEM_BYTES - 12 * 1024 * 1024

ROW_TILE = 512
WIDE_TILE = 1024
DISPATCH_TILE = 2048
COMBINE_TILE = 256
SCAN_TILE = 512

NORM_EPS = 1e-6
RW_HEAD = 64
CHUNK = 64
HEAD_GROUP = 256
N_EXPERTS = 32
TOP_K = 4
MOE_BLOCK_ROWS = 512
SWIGLU_ALPHA = 1.702
SWIGLU_LIMIT = 7.0


def _cparams(sem, **kw):
    return pltpu.CompilerParams(dimension_semantics=sem, vmem_limit_bytes=VMEM_LIMIT, **kw)


def _dot(a, b):
    return jnp.dot(a, b, preferred_element_type=F32)


def _dot_hilo_lhs(g_bf16, x):
    hi, lo = _split2(x)
    return _dot(g_bf16, hi) + _dot(g_bf16, lo)


def _split2(x):
    hi = x.astype(BF16)
    return hi, (x - hi.astype(F32)).astype(BF16)


def _dot_hilo_rhs(x, g_bf16):
    hi, lo = _split2(x)
    out = _dot(jnp.concatenate([hi, lo], axis=0), g_bf16)
    return out[:x.shape[0]] + out[x.shape[0]:]


def _dot_hilo(a, b):
    ah, al = _split2(a)
    bh, bl = _split2(b)
    return _dot(ah, bh) + (_dot(ah, bl) + _dot(al, bh))


def _store_rows(ref, x2):
    rows, d = x2.shape
    c = d // V7X_LANES
    for j in range(c):
        ref[pl.ds(j, rows, stride=c), :] = x2[:, j * V7X_LANES:(j + 1) * V7X_LANES]


def _load_rows(ref, c):
    rows = ref.shape[0] // c
    return jnp.concatenate([ref[pl.ds(j, rows, stride=c), :] for j in range(c)], axis=1)


def _group_matrix(width, group, value):
    g = np.arange(width) // group
    return jnp.asarray(np.where(g[:, None] == g[None, :], value, 0.0), BF16)


def _inproj_kernel(x_ref, g_ref, w_ref, gq_ref, gk_ref, gm_ref, zrw_ref, q_ref, k_ref, vt_ref, *, rw_cols, c_da, scale):
    x = x_ref[...]
    ms = jnp.mean(x * x, axis=-1, keepdims=True)
    u = (x * lax.rsqrt(ms + NORM_EPS) * g_ref[...]).astype(BF16)
    z = _dot(u, w_ref[...])
    zrw_ref[...] = z[:, :rw_cols]
    zq = z[:, rw_cols:rw_cols + c_da]
    zk = z[:, rw_cols + c_da:rw_cols + 2 * c_da]
    zv = z[:, rw_cols + 2 * c_da:]
    tm = zq.shape[0]
    msqk = _dot(jnp.concatenate([zq * zq, zk * zk], axis=0).astype(BF16), gm_ref[...])
    q_ref[...] = (zq * lax.rsqrt(msqk[:tm] + NORM_EPS) * (gq_ref[...] * scale)).astype(BF16)
    k_ref[...] = (zk * lax.rsqrt(msqk[tm:] + NORM_EPS) * gk_ref[...]).astype(BF16)
    heads, vrows, _ = vt_ref.shape
    v_dim = c_da // heads
    pad_row = lax.broadcasted_iota(jnp.int32, (vrows - v_dim, tm), 0)
    pad = jnp.where(pad_row == 0, 1.0, 0.0).astype(BF16)
    for hd in range(heads):
        vt_ref[hd, 0:v_dim, :] = zv[:, hd * v_dim:(hd + 1) * v_dim].T.astype(BF16)
        vt_ref[hd, v_dim:, :] = pad


def _inproj(x2, g, w_bf, gq_t, gk_t, gm, rw_cols, c_da, scale, tm, b, heads, tb):
    n, d = x2.shape
    cols = w_bf.shape[1]
    t = n // b
    tps = t // tm
    per = tb // tm
    vrows = c_da // heads + ATTN_V_PAD
    full = lambda i: (0, 0)
    row = lambda i: (i, 0)
    return pl.pallas_call(
        functools.partial(_inproj_kernel, rw_cols=rw_cols, c_da=c_da, scale=scale),
        grid=(n // tm,),
        in_specs=[
            pl.BlockSpec((tm, d), row),
            pl.BlockSpec((1, d), full),
            pl.BlockSpec((d, cols), full),
            pl.BlockSpec((1, c_da), full),
            pl.BlockSpec((1, c_da), full),
            pl.BlockSpec((c_da, c_da), full),
        ],
        out_specs=[
            pl.BlockSpec((tm, rw_cols), row),
            pl.BlockSpec((tm, c_da), row),
            pl.BlockSpec((tm, c_da), row),
            pl.BlockSpec((None, heads, None, vrows, tm),
                         lambda i: (i // tps, 0, (i % tps) // per, 0, (i % tps) % per)),
        ],
        out_shape=[
            jax.ShapeDtypeStruct((n, rw_cols), F32),
            jax.ShapeDtypeStruct((n, c_da), BF16),
            jax.ShapeDtypeStruct((n, c_da), BF16),
            jax.ShapeDtypeStruct((b, heads, t // tb, vrows, tb), BF16),
        ],
        compiler_params=_cparams(("parallel",)),
        name="inproj",
    )(x2, g, w_bf, gq_t, gk_t, gm)


def _rwkv_prep_kernel(z_ref, zp_ref, zn_ref, mu_ref, w2_ref, w0_ref, a2_ref, a0_ref, g2_ref, kk_ref, ka_ref,
                      rk_ref, gsum_ref,
                      r_ref, v_ref, nkk_ref, lw_ref, kd_ref, bd_ref, g_ref, bonus_ref, *, c_rw, lora):
    i = pl.program_id(1)
    nt = pl.num_programs(1)
    z = z_ref[...]
    tt = z.shape[0]
    rows = lax.broadcasted_iota(jnp.int32, z.shape, 0)
    prev_row = jnp.where(i == 0, 0.0, zp_ref[7:8, :])
    next_row = jnp.where(i == nt - 1, 0.0, zn_ref[0:1, :])
    zprev = jnp.where(rows == 0, prev_row, pltpu.roll(z, 1, 0))
    znext = jnp.where(rows == tt - 1, next_row, pltpu.roll(z, tt - 1, 0))
    z = z + (0.5 * (zprev + znext) - z) * mu_ref[...]

    r = z[:, 0:c_rw]
    k = z[:, c_rw:2 * c_rw]
    v = z[:, 2 * c_rw:3 * c_rw]
    o = 3 * c_rw
    wd = z[:, o:o + 2 * lora]
    ad = z[:, o + 2 * lora:o + 4 * lora]
    gd = z[:, o + 4 * lora:]

    xw = w0_ref[...] + _dot(jnp.tanh(wd).astype(BF16), w2_ref[...])
    xa = a0_ref[...] + _dot(ad.astype(BF16), a2_ref[...])
    lw = -math.exp(-0.5) * jax.nn.sigmoid(xw)
    rate = jax.nn.sigmoid(xa)
    g = _dot(jax.nn.sigmoid(gd).astype(BF16), g2_ref[...])

    gsum = gsum_ref[...]
    kkr = k * kk_ref[...]
    ka = ka_ref[...]
    kd0 = k * (1.0 + (rate[:, :c_rw] - 1.0) * ka)
    kd1 = k * (1.0 + (rate[:, c_rw:] - 1.0) * ka)
    sums = _dot_hilo_rhs(jnp.concatenate([kkr * kkr, r * (0.5 * (kd0 + kd1)) * rk_ref[...]], axis=0), gsum)
    kk = kkr * lax.rsqrt(sums[:tt] + 1e-12)
    bonus = sums[tt:] * v

    r_ref[...] = r.astype(BF16)
    v_ref[...] = v.astype(BF16)
    nkk_ref[...] = (-kk).astype(BF16)
    lw_ref[0] = lw[:, :c_rw]
    lw_ref[1] = lw[:, c_rw:]
    kd_ref[0] = kd0.astype(BF16)
    kd_ref[1] = kd1.astype(BF16)
    bd_ref[0] = (kk * rate[:, :c_rw]).astype(BF16)
    bd_ref[1] = (kk * rate[:, c_rw:]).astype(BF16)
    g_ref[...] = g.astype(BF16)
    bonus_ref[...] = bonus.astype(BF16)


def _rwkv_prep(zrw3, mu, w2bd, w0f, a2bd, a0f, g2_bf, k_k, k_a, r_kf, gsum, c_rw, lora, tt):
    b, t, cols = zrw3.shape
    nt = t // tt
    hb = tt // 8
    nhb = t // 8
    full = lambda bi, i: (0, 0)
    tile = lambda bi, i: (bi, i, 0)
    dtile = lambda bi, i: (0, bi, i, 0)
    one = jax.ShapeDtypeStruct((b, t, c_rw), F32)
    two = jax.ShapeDtypeStruct((2, b, t, c_rw), F32)
    one16 = jax.ShapeDtypeStruct((b, t, c_rw), BF16)
    two16 = jax.ShapeDtypeStruct((2, b, t, c_rw), BF16)
    return pl.pallas_call(
        functools.partial(_rwkv_prep_kernel, c_rw=c_rw, lora=lora),
        grid=(b, nt),
        in_specs=[
            pl.BlockSpec((None, tt, cols), tile),
            pl.BlockSpec((None, 8, cols), lambda bi, i: (bi, jnp.maximum(i * hb - 1, 0), 0)),
            pl.BlockSpec((None, 8, cols), lambda bi, i: (bi, jnp.minimum((i + 1) * hb, nhb - 1), 0)),
            pl.BlockSpec((1, cols), full),
            pl.BlockSpec(w2bd.shape, full),
            pl.BlockSpec((1, 2 * c_rw), full),
            pl.BlockSpec(a2bd.shape, full),
            pl.BlockSpec((1, 2 * c_rw), full),
            pl.BlockSpec(g2_bf.shape, full),
            pl.BlockSpec((1, c_rw), full),
            pl.BlockSpec((1, c_rw), full),
            pl.BlockSpec((1, c_rw), full),
            pl.BlockSpec((c_rw, c_rw), full),
        ],
        out_specs=[
            pl.BlockSpec((None, tt, c_rw), tile),
            pl.BlockSpec((None, tt, c_rw), tile),
            pl.BlockSpec((None, tt, c_rw), tile),
            pl.BlockSpec((2, None, tt, c_rw), dtile),
            pl.BlockSpec((2, None, tt, c_rw), dtile),
            pl.BlockSpec((2, None, tt, c_rw), dtile),
            pl.BlockSpec((None, tt, c_rw), tile),
            pl.BlockSpec((None, tt, c_rw), tile),
        ],
        out_shape=[one16, one16, one16, two, two16, two16, one16, one16],
        compiler_params=_cparams(("parallel", "parallel")),
        name="rwkv_prep",
    )(zrw3, zrw3, zrw3, mu, w2bd, w0f, a2bd, a0f, g2_bf, k_k, k_a, r_kf, gsum)


M_STRICT, M_INCL, M_EYE, M_PAIR, M_LEVEL = 0, 1, 2, 3, 4
N_LEVELS = 5
SCAN_SEQS = 2


def _scan_tables(w):
    c = CHUNK
    masks = []
    mcums = []
    for d in range(2):
        t = np.broadcast_to(np.arange(c)[:, None], (c, w))
        s = np.broadcast_to((np.arange(w) % RW_HEAD)[None, :], (c, w))
        tt = np.broadcast_to(np.arange(c)[:, None], (c, c))
        ss = np.broadcast_to(np.arange(c)[None, :], (c, c))
        if d == 1:
            t, s, tt, ss = c - 1 - t, c - 1 - s, c - 1 - tt, c - 1 - ss
        rows = [s < t, s <= t, s == t, (t // 2 == s // 2) & (t > s)]
        sz = 2
        while sz < c:
            rows.append((t // (2 * sz) == s // (2 * sz)) & ((t // sz) % 2 == 1) & ((s // sz) % 2 == 0))
            sz *= 2
        assert len(rows) == M_LEVEL + N_LEVELS
        masks.append(np.stack(rows).astype(np.float32))
        mcums.append((ss <= tt).astype(np.float32))
    return jnp.asarray(np.stack(masks), F32), jnp.asarray(np.stack(mcums), BF16)


def _block_diag(y, bd16):
    g = y.shape[1] // RW_HEAD
    return jnp.concatenate([y.astype(BF16)] * g, axis=0) * bd16


def _bmm(x, y, bd16):
    outs = []
    for s in range(0, y.shape[1], HEAD_GROUP):
        outs.append(_dot(x[:, s:s + HEAD_GROUP].astype(BF16), _block_diag(y[:, s:s + HEAD_GROUP], bd16)))
    return jnp.concatenate(outs, axis=1)


def _bmm_nt(x, y, bd16):
    outs = []
    for s in range(0, y.shape[1], HEAD_GROUP):
        bd = _block_diag(y[:, s:s + HEAD_GROUP], bd16)
        outs.append(lax.dot_general(x[:, s:s + HEAD_GROUP].astype(BF16), bd, (((1,), (1,)), ((), ())),
                                    preferred_element_type=F32))
    return jnp.concatenate(outs, axis=1)


def _bmm_tn(x, y, bd32):
    outs = []
    for s in range(0, y.shape[1], HEAD_GROUP):
        xt = x[:, s:s + HEAD_GROUP].T.astype(BF16)
        full = _dot(xt, y[:, s:s + HEAD_GROUP].astype(BF16)) * bd32
        o = full[0:RW_HEAD]
        for j in range(1, HEAD_GROUP // RW_HEAD):
            o = o + full[j * RW_HEAD:(j + 1) * RW_HEAD]
        outs.append(o)
    return jnp.concatenate(outs, axis=1)


def _scan_chunks(r, lw, k, v, a, b, state, masks, mcum, bd16, bd32):
    c = r[0].shape[0]

    def each(f, *cols):
        return [f(*xs) for xs in zip(*cols)]

    cum = each(_dot_hilo_lhs, mcum, lw)
    tot = each(lambda x: jnp.sum(x, axis=0, keepdims=True), lw)
    e_neg = each(lambda x: jnp.exp(-x), cum)
    e_tail = each(lambda t_, x: jnp.exp(t_ - x), tot, cum)
    rt = each(lambda x, cm: x * jnp.exp(cm), r, cum)
    at = each(lambda x, cm, l_: x * jnp.exp(cm - l_), a, cum, lw)
    bt = each(jnp.multiply, b, e_neg)
    kt = each(jnp.multiply, k, e_neg)
    bh = each(jnp.multiply, b, e_tail)
    kh = each(jnp.multiply, k, e_tail)

    ar = each(lambda x, y_: jnp.concatenate([x, y_], axis=0), at, rt)
    pb = each(lambda x, y_: _bmm_nt(x, y_, bd16), ar, bt)
    pk = each(lambda x, y_: _bmm_nt(x, y_, bd16), ar, kt)
    a_ab = each(lambda x, m: x[:c] * m[M_STRICT], pb, masks)
    a_rb = each(lambda x, m: x[c:] * m[M_INCL], pb, masks)
    a_ak = each(lambda x, m: x[:c] * m[M_STRICT], pk, masks)
    a_rk = each(lambda x, m: x[c:] * m[M_INCL], pk, masks)

    tinv = each(lambda x, m: m[M_EYE] + x * m[M_PAIR], a_ab, masks)
    for lvl in range(N_LEVELS):
        x = each(lambda aa, m, ti: _bmm(aa * m[M_LEVEL + lvl], ti, bd16), a_ab, masks, tinv)
        tinv = each(lambda ti, x_: ti + _bmm(ti, x_, bd16), tinv, x)

    bmm = lambda p, q: _bmm(p, q, bd16)
    stack = lambda p, q: jnp.concatenate([p, q], axis=0)
    att = each(bmm, tinv, at)
    akv = each(lambda p, p2, q: _bmm(stack(p, p2), q, bd16), a_ak, a_rk, v)
    vt = each(lambda ti, x_: _bmm(ti, x_[:c], bd16), tinv, akv)
    rh = each(lambda x, p, q: x + _bmm(p, q, bd16), rt, a_rb, att)
    y_in = each(lambda p, q, x_: _bmm(p, q, bd16) + x_[c:], a_rb, vt, akv)
    mc = each(lambda m, t_, p, q: m[M_EYE] * jnp.exp(t_) + _bmm_tn(p, q, bd32), masks, tot, bh, att)
    nc = each(lambda p, p2, q, q2: _bmm_tn(stack(p, p2), stack(q, q2), bd32), bh, kh, vt, v)
    onst = each(lambda p, p2, q: _bmm(stack(p, p2), q, bd16), rh, mc, state)
    y = each(lambda x_, yi: x_[:c] + yi, onst, y_in)
    new_state = each(lambda x_, n_: x_[c:] + n_, onst, nc)
    return y, new_state


def _rwkv_scan_kernel(rf_ref, vf_ref, af_ref, rb_ref, vb_ref, ab_ref, lwf_ref, kf_ref, bf_ref, lwb_ref, kb_ref,
                      bb_ref, mask_ref, mcum_ref, bd16_ref, bd32_ref, yf_ref, yb_ref, state_ref, *, n_chunks):
    @pl.when(pl.program_id(1) == 0)
    def _():
        state_ref[...] = jnp.zeros_like(state_ref)

    dirs = ((0, rf_ref, lwf_ref, kf_ref, vf_ref, af_ref, bf_ref, yf_ref),
            (1, rb_ref, lwb_ref, kb_ref, vb_ref, ab_ref, bb_ref, yb_ref))
    n_seq = rf_ref.shape[0]

    def body(j, carry):
        cols = [[] for _ in range(9)]
        where = []
        for d, r_ref, lw_ref, k_ref, v_ref, a_ref, b_ref, y_ref in dirs:
            cj = j if d == 0 else n_chunks - 1 - j
            sl = pl.ds(pl.multiple_of(cj * CHUNK, CHUNK), CHUNK)
            for g in range(n_seq):
                vals = (r_ref[g, sl, :], lw_ref[g, sl, :], k_ref[g, sl, :], v_ref[g, sl, :], a_ref[g, sl, :],
                        b_ref[g, sl, :], state_ref[d, g], mask_ref.at[d], mcum_ref[d])
                for col, val in zip(cols, vals):
                    col.append(val)
                where.append((y_ref, d, g, sl))
        ys, sts = _scan_chunks(*cols, bd16_ref[...], bd32_ref[...])
        for (y_ref, d, g, sl), y, st in zip(where, ys, sts):
            y_ref[g, sl, :] = y.astype(y_ref.dtype)
            state_ref[d, g] = st
        return carry

    lax.fori_loop(0, n_chunks, body, 0)


def _rwkv_scan(r, v, nkk, lw, kd, bd, tb):
    b, t, w = r.shape
    nb = t // tb
    g = SCAN_SEQS if b % SCAN_SEQS == 0 else 1
    masks, mcum = _scan_tables(w)
    bd32 = _group_matrix(HEAD_GROUP, RW_HEAD, 1.0).astype(F32)
    bd16 = bd32.astype(BF16)
    fwd = lambda bi, i: (bi, i, 0)
    bwd = lambda bi, i: (bi, nb - 1 - i, 0)
    fwd_d = lambda bi, i: (0, bi, i, 0)
    bwd_d = lambda bi, i: (1, bi, nb - 1 - i, 0)
    const = lambda nd: (lambda bi, i: (0,) * nd)
    blk = (g, tb, w)
    dblk = (None, g, tb, w)
    out = jax.ShapeDtypeStruct((b, t, w), BF16)
    return pl.pallas_call(
        functools.partial(_rwkv_scan_kernel, n_chunks=tb // CHUNK),
        grid=(b // g, nb),
        in_specs=[
            pl.BlockSpec(blk, fwd), pl.BlockSpec(blk, fwd), pl.BlockSpec(blk, fwd),
            pl.BlockSpec(blk, bwd), pl.BlockSpec(blk, bwd), pl.BlockSpec(blk, bwd),
            pl.BlockSpec(dblk, fwd_d), pl.BlockSpec(dblk, fwd_d), pl.BlockSpec(dblk, fwd_d),
            pl.BlockSpec(dblk, bwd_d), pl.BlockSpec(dblk, bwd_d), pl.BlockSpec(dblk, bwd_d),
            pl.BlockSpec(masks.shape, const(4)),
            pl.BlockSpec(mcum.shape, const(3)),
            pl.BlockSpec(bd16.shape, const(2)),
            pl.BlockSpec(bd32.shape, const(2)),
        ],
        out_specs=[pl.BlockSpec(blk, fwd), pl.BlockSpec(blk, bwd)],
        out_shape=[out, out],
        scratch_shapes=[pltpu.VMEM((2, g, CHUNK, w), F32)],
        compiler_params=_cparams(("parallel", "arbitrary")),
        name="rwkv_scan",
    )(r, v, nkk, r, v, nkk, lw, kd, bd, lw, kd, bd, masks, mcum, bd16, bd32)


def _attn_kernel(slope_ref, q_ref, k_ref, v_ref, lq1_ref, lk1_ref, lq2_ref, lk2_ref, sg_ref, o_ref,
                 m_ref, l_ref, acc_ref, *, qk_dim, lambda_init):
    h = pl.program_id(1)
    i = pl.program_id(2)
    j = pl.program_id(3)
    nk = pl.num_programs(3)
    tq = q_ref.shape[0]
    tk = k_ref.shape[0]

    @pl.when(j == 0)
    def _():
        m_ref[...] = jnp.full(m_ref.shape, -jnp.inf, F32)
        l_ref[...] = jnp.zeros_like(l_ref)
        acc_ref[...] = jnp.zeros_like(acc_ref)

    q = q_ref[...]
    k = k_ref[...]
    v = v_ref[...]
    lane = lax.broadcasted_iota(jnp.int32, q.shape, 1)
    zero = jnp.zeros_like(q)
    qi = lax.broadcasted_iota(jnp.int32, (tq, tk), 0) + i * tq
    kj = lax.broadcasted_iota(jnp.int32, (tq, tk), 1) + j * tk
    bias = slope_ref[h] * jnp.abs(qi - kj).astype(F32)
    for c in range(2):
        qc = jnp.where((lane < qk_dim) == (c == 0), q, zero)
        s = lax.dot_general(qc, k, (((1,), (1,)), ((), ())), preferred_element_type=F32) - bias
        m_old = m_ref[c]
        m_new = jnp.maximum(m_old, jnp.max(s, axis=-1, keepdims=True))
        alpha = jnp.exp(m_old - m_new)
        p = jnp.exp(s - m_new)
        l_ref[c] = alpha * l_ref[c] + jnp.sum(p, axis=-1, keepdims=True)
        acc_ref[c] = alpha * acc_ref[c] + _dot(p.astype(BF16), v)
        m_ref[c] = m_new

    @pl.when(j == nk - 1)
    def _():
        lam = (jnp.exp(jnp.sum(lq1_ref[...] * lk1_ref[...], axis=-1, keepdims=True))
               - jnp.exp(jnp.sum(lq2_ref[...] * lk2_ref[...], axis=-1, keepdims=True)) + lambda_init)
        o = acc_ref[0] / l_ref[0] - lam * (acc_ref[1] / l_ref[1])
        ms = jnp.mean(o * o, axis=-1, keepdims=True)
        o_ref[...] = o * lax.rsqrt(ms + NORM_EPS) * (sg_ref[...] * (1.0 - lambda_init))


def _attention(slopes, q3, k3, v3, lq1, lk1, lq2, lk2, subln_g, heads, v_dim, lambda_init, tq, tk):
    b, t, _ = q3.shape
    const = lambda bi, h, i, j, s: (0, 0)
    grid_spec = pltpu.PrefetchScalarGridSpec(
        num_scalar_prefetch=1,
        grid=(b, heads, t // tq, t // tk),
        in_specs=[
            pl.BlockSpec((None, tq, v_dim), lambda bi, h, i, j, s: (bi, i, h)),
            pl.BlockSpec((None, tk, v_dim), lambda bi, h, i, j, s: (bi, j, h)),
            pl.BlockSpec((None, tk, v_dim), lambda bi, h, i, j, s: (bi, j, h)),
            pl.BlockSpec(lq1.shape, const),
            pl.BlockSpec(lk1.shape, const),
            pl.BlockSpec(lq2.shape, const),
            pl.BlockSpec(lk2.shape, const),
            pl.BlockSpec(subln_g.shape, const),
        ],
        out_specs=pl.BlockSpec((None, tq, v_dim), lambda bi, h, i, j, s: (bi, i, h)),
        scratch_shapes=[
            pltpu.VMEM((2, tq, 1), F32),
            pltpu.VMEM((2, tq, 1), F32),
            pltpu.VMEM((2, tq, v_dim), F32),
        ],
    )
    return pl.pallas_call(
        functools.partial(_attn_kernel, qk_dim=v_dim // 2, lambda_init=lambda_init),
        grid_spec=grid_spec,
        out_shape=jax.ShapeDtypeStruct((b, t, heads * v_dim), F32),
        compiler_params=_cparams(("parallel", "parallel", "parallel", "arbitrary")),
        name="attn",
    )(slopes, q3, k3, v3, lq1, lk1, lq2, lk2, subln_g)


EXP_ZERO_ARG = 104.0
MAX_FIXED_SHIFT = 40.0
N_AUG = 6
ATTN_BLOCK = 512
ATTN_GROUP_SIZES = (8, 4, 2, 1)
ATTN_KEY_SPLIT = 1
ATTN_V_PAD = 16


def _attn_fixed_kernel(w_ref, slope_ref, mb_ref, q_ref, k_ref, vt_ref, dbias_ref, qaug_ref, kaug_ref,
                       lq1_ref, lk1_ref, lq2_ref, lk2_ref, sg_ref, o_ref, acc_ref, *, qk_dim, lambda_init):
    h = pl.program_id(1)
    i = pl.program_id(2)
    tb = q_ref.shape[0]
    nk = k_ref.shape[0] // tb
    v_dim = 2 * qk_dim
    slope = slope_ref[h]
    mb = mb_ref[0]
    w = w_ref[h]

    q = q_ref[...]
    lane = lax.broadcasted_iota(jnp.int32, q.shape, 1)
    lane_row = lax.broadcasted_iota(jnp.int32, (1, q.shape[1]), 1)
    is_c0 = lane < qk_dim
    datas = (is_c0, jnp.logical_not(is_c0))

    def blocks(items, qvar, kvar, diag):
        sub = tb // ATTN_KEY_SPLIT
        qas = [jnp.where(datas[c], q, qaug_ref[qvar, c]) for c in range(2)]
        chains = []
        for n, (j, delta) in enumerate(items):
            kblk = k_ref[pl.ds(pl.multiple_of(j * tb, tb), tb), :]
            vtblk = vt_ref[j]
            for c in range(2):
                base = qk_dim * (1 - c)
                dyn = jnp.where(lane_row == base + 4, -mb,
                                jnp.where(lane_row == base + 5, -(slope * tb) * delta, 0.0)).astype(BF16)
                ka = jnp.where(datas[c], kblk, kaug_ref[kvar, c] + dyn)
                for u in range(ATTN_KEY_SPLIT):
                    rows = slice(u * sub, (u + 1) * sub)
                    chains.append((c, u, qas[c], ka[rows], vtblk[:, rows]))
        sts = [lax.dot_general(ka, qa, (((1,), (1,)), ((), ())), preferred_element_type=F32)
               for _, _, qa, ka, _ in chains]
        if diag:
            sts = [st - dbias_ref[u * sub:(u + 1) * sub, :] for st, (_, u, _, _, _) in zip(sts, chains)]
        pts = [jnp.exp(st).astype(BF16) for st in sts]
        for c in range(2):
            upd = None
            for pt, (cc, _, _, _, vsub) in zip(pts, chains):
                if cc == c:
                    part = _dot(vsub, pt)
                    upd = part if upd is None else upd + part
            if diag:
                acc_ref[c] = upd
            else:
                acc_ref[c] += upd

    def span(first, count, qvar, kvar, dist):
        for per in ATTN_GROUP_SIZES:
            def group(g, carry, first=first, per=per):
                j0 = first + g * per
                blocks([(j0 + n, dist(j0 + n)) for n in range(per)], qvar, kvar, False)
                return carry

            groups = count // per
            lax.fori_loop(0, groups, group, 0)
            first = first + groups * per
            count = count - groups * per

    lo = jnp.maximum(i - w, 0)
    hi = jnp.minimum(i + w, nk - 1)
    blocks([(i, jnp.zeros((), F32))], 2, 0, True)
    span(lo, i - lo, 0, 0, lambda j: (i - j).astype(F32))
    span(i + 1, hi - i, 1, 1, lambda j: (j - i).astype(F32))

    lam = (jnp.exp(jnp.sum(lq1_ref[...] * lk1_ref[...], axis=-1, keepdims=True))
           - jnp.exp(jnp.sum(lq2_ref[...] * lk2_ref[...], axis=-1, keepdims=True)) + lambda_init)
    a0 = acc_ref[0]
    a1 = acc_ref[1]
    inv0 = 1.0 / a0[v_dim:v_dim + 1]
    inv1 = lam / a1[v_dim:v_dim + 1]
    ot = a0[:v_dim] * inv0 - a1[:v_dim] * inv1
    ms = jnp.mean(ot * ot, axis=0, keepdims=True)
    o_ref[...] = (ot * lax.rsqrt(ms + NORM_EPS) * (sg_ref[...] * (1.0 - lambda_init))).T.astype(o_ref.dtype)


def _attn_aug_tables(slopes, tb, qk_dim):
    heads = len(slopes)
    lanes = 2 * qk_dim
    pos = np.arange(tb)
    qaug = np.zeros((heads, 3, 2, tb, lanes), np.float32)
    kaug = np.zeros((heads, 2, 2, tb, lanes), np.float32)
    for hd, sl in enumerate(slopes):
        for var in range(3):
            rr = pos if var != 1 else tb - 1 - pos
            hi = (rr // 16) * (16.0 * sl)
            lo = (rr % 16) * sl
            for c in range(2):
                base = qk_dim * (1 - c)
                if var == 2:
                    qaug[hd, var, c, :, base + 4] = 1.0
                    continue
                qaug[hd, var, c, :, base + 0] = -hi
                qaug[hd, var, c, :, base + 1] = -lo
                qaug[hd, var, c, :, base + 2:base + N_AUG] = 1.0
                kaug[hd, var, c, :, base + 0:base + 2] = 1.0
                kaug[hd, var, c, :, base + 2] = hi
                kaug[hd, var, c, :, base + 3] = lo
    return jnp.asarray(qaug, BF16), jnp.asarray(kaug, BF16)


def _attention_fixed(mb, q3, k3, vt, lq1, lk1, lq2, lk2, subln_g, heads, v_dim, lambda_init, tb):
    b, t, _ = q3.shape
    qk_dim = v_dim // 2
    slopes = [2.0 ** (-8.0 * (hd + 1) / heads) for hd in range(heads)]
    widths = [int(math.ceil((EXP_ZERO_ARG / s - 1.0) / tb)) for s in slopes]
    qaug, kaug = _attn_aug_tables(slopes, tb, qk_dim)
    pos = np.arange(tb)
    dist = np.abs(pos[:, None] - pos[None, :]).astype(np.float32)
    dbias = jnp.asarray(np.stack([dist * s for s in slopes]), F32)
    nb = t // tb
    vrows = v_dim + ATTN_V_PAD
    assert vt.shape == (b, heads, nb, vrows, tb)
    const = lambda bi, h, i, *_: (0, 0)
    grid_spec = pltpu.PrefetchScalarGridSpec(
        num_scalar_prefetch=3,
        grid=(b, heads, t // tb),
        in_specs=[
            pl.BlockSpec((None, tb, v_dim), lambda bi, h, i, *_: (bi, i, h)),
            pl.BlockSpec((None, t, v_dim), lambda bi, h, i, *_: (bi, 0, h)),
            pl.BlockSpec((None, None, nb, vrows, tb), lambda bi, h, i, *_: (bi, h, 0, 0, 0)),
            pl.BlockSpec((None, tb, tb), lambda bi, h, i, *_: (h, 0, 0)),
            pl.BlockSpec((None, 3, 2, tb, v_dim), lambda bi, h, i, *_: (h, 0, 0, 0, 0)),
            pl.BlockSpec((None, 2, 2, tb, v_dim), lambda bi, h, i, *_: (h, 0, 0, 0, 0)),
            pl.BlockSpec(lq1.shape, const),
            pl.BlockSpec(lk1.shape, const),
            pl.BlockSpec(lq2.shape, const),
            pl.BlockSpec(lk2.shape, const),
            pl.BlockSpec((v_dim, 1), const),
        ],
        out_specs=pl.BlockSpec((None, tb, v_dim), lambda bi, h, i, *_: (bi, i, h)),
        scratch_shapes=[pltpu.VMEM((2, vrows, tb), F32)],
    )
    return pl.pallas_call(
        functools.partial(_attn_fixed_kernel, qk_dim=qk_dim, lambda_init=lambda_init),
        grid_spec=grid_spec,
        out_shape=jax.ShapeDtypeStruct((b, t, heads * v_dim), BF16),
        compiler_params=_cparams(("parallel", "parallel", "parallel")),
        name="attn_fixed",
    )(jnp.asarray(widths, jnp.int32), jnp.asarray(slopes, F32), mb.reshape(1).astype(F32),
      q3, k3, vt, dbias, qaug, kaug, lq1, lk1, lq2, lk2, subln_g.reshape(v_dim, 1))


def _post_kernel(yf_ref, yb_ref, bonus_ref, g_ref, oda_ref, x_ref, lng_ref, lnb_ref, gmean_ref, wo_rw_ref, wo_da_ref,
                 gffn_ref, wr_ref, br_ref, h_ref, m_ref, idx_ref, gate_ref, cnt_ref, *, gn_eps):
    @pl.when(pl.program_id(0) == 0)
    def _():
        cnt_ref[...] = jnp.zeros_like(cnt_ref)

    gmean = gmean_ref[...]
    y = yf_ref[...].astype(F32) + yb_ref[...].astype(F32)
    mean = _dot_hilo_rhs(y, gmean)
    yc = y - mean
    var = _dot_hilo_rhs(yc * yc, gmean)
    yn = yc * lax.rsqrt(var + gn_eps) * lng_ref[...] + lnb_ref[...]
    o_rw = (yn + bonus_ref[...]) * g_ref[...]
    h = x_ref[...] + _dot(o_rw.astype(BF16), wo_rw_ref[...]) + _dot(oda_ref[...].astype(BF16), wo_da_ref[...])
    h_ref[...] = h
    ms = jnp.mean(h * h, axis=-1, keepdims=True)
    m = h * lax.rsqrt(ms + NORM_EPS) * gffn_ref[...]
    _store_rows(m_ref, m)

    logits = _dot_hilo(m, wr_ref[...]) + br_ref[...]
    ne = logits.shape[1]
    lane = lax.broadcasted_iota(jnp.int32, logits.shape, 1).astype(F32)
    kcol = lax.broadcasted_iota(jnp.int32, (logits.shape[0], TOP_K), 1)
    idx_all = jnp.zeros((logits.shape[0], TOP_K), F32)
    val_all = jnp.zeros((logits.shape[0], TOP_K), F32)
    work = logits
    chosen = jnp.zeros(logits.shape, F32)
    for kk in range(TOP_K):
        mx = jnp.max(work, axis=-1, keepdims=True)
        sel = jnp.min(jnp.where(work == mx, lane, float(ne)), axis=-1, keepdims=True)
        idx_all = jnp.where(kcol == kk, sel, idx_all)
        val_all = jnp.where(kcol == kk, mx, val_all)
        hit = lane == sel
        chosen = jnp.where(hit, 1.0, chosen)
        work = jnp.where(hit, -jnp.inf, work)
    e = jnp.exp(val_all - jnp.max(val_all, axis=-1, keepdims=True))
    idx_ref[...] = idx_all.astype(jnp.int32)
    gate_ref[...] = e / jnp.sum(e, axis=-1, keepdims=True)
    cnt_ref[...] += jnp.sum(chosen, axis=0, keepdims=True)


def _post(yf, yb, bonus, g, oda, x2, lng, lnb, gmean, wo_rw, wo_da, gffn, wr, br, gn_eps, tm):
    n, c = yf.shape
    d = x2.shape[1]
    ne = wr.shape[1]
    full = lambda i: (0, 0)
    row = lambda i: (i, 0)
    return pl.pallas_call(
        functools.partial(_post_kernel, gn_eps=gn_eps),
        grid=(n // tm,),
        in_specs=[
            pl.BlockSpec((tm, c), row),
            pl.BlockSpec((tm, c), row),
            pl.BlockSpec((tm, c), row),
            pl.BlockSpec((tm, c), row),
            pl.BlockSpec((tm, c), row),
            pl.BlockSpec((tm, d), row),
            pl.BlockSpec((1, c), full),
            pl.BlockSpec((1, c), full),
            pl.BlockSpec((c, c), full),
            pl.BlockSpec((c, d), full),
            pl.BlockSpec((c, d), full),
            pl.BlockSpec((1, d), full),
            pl.BlockSpec((d, ne), full),
            pl.BlockSpec((1, ne), full),
        ],
        out_specs=[
            pl.BlockSpec((tm, d), row),
            pl.BlockSpec((tm * d // V7X_LANES, V7X_LANES), row),
            pl.BlockSpec((tm, TOP_K), row),
            pl.BlockSpec((tm, TOP_K), row),
            pl.BlockSpec((1, ne), full),
        ],
        out_shape=[
            jax.ShapeDtypeStruct((n, d), F32),
            jax.ShapeDtypeStruct((n * d // V7X_LANES, V7X_LANES), F32),
            jax.ShapeDtypeStruct((n, TOP_K), jnp.int32),
            jax.ShapeDtypeStruct((n, TOP_K), F32),
            jax.ShapeDtypeStruct((1, ne), F32),
        ],
        compiler_params=_cparams(("arbitrary",)),
        name="post",
    )(yf, yb, bonus, g, oda, x2, lng, lnb, gmean, wo_rw, wo_da, gffn, wr, br)


def _rank_kernel(idx_ref, ltri_ref, offs_ref, slot_ref, carry_ref):
    @pl.when(pl.program_id(0) == 0)
    def _():
        carry_ref[...] = offs_ref[...]

    idx = idx_ref[...]
    tr = idx.shape[0]
    lane = lax.broadcasted_iota(jnp.int32, (tr, N_EXPERTS), 1)
    hots = [jnp.where(lane == idx[:, kk:kk + 1], 1.0, 0.0) for kk in range(TOP_K)]
    mask = hots[0]
    for kk in range(1, TOP_K):
        mask = mask + hots[kk]
    before = _dot(ltri_ref[...], mask.astype(BF16)) + carry_ref[...]
    kcol = lax.broadcasted_iota(jnp.int32, (tr, TOP_K), 1)
    slot = jnp.zeros((tr, TOP_K), F32)
    for kk in range(TOP_K):
        rk = jnp.sum(hots[kk] * before, axis=-1, keepdims=True)
        slot = jnp.where(kcol == kk, rk, slot)
    slot_ref[...] = slot.astype(jnp.int32)
    carry_ref[...] = carry_ref[...] + jnp.sum(mask, axis=0, keepdims=True)


def _moe_rank(idx, offs_row, tr):
    n = idx.shape[0]
    ltri = jnp.asarray(np.tril(np.ones((tr, tr), np.float32), -1), BF16)
    return pl.pallas_call(
        _rank_kernel,
        grid=(n // tr,),
        in_specs=[pl.BlockSpec((tr, TOP_K), lambda i: (i, 0)), pl.BlockSpec((tr, tr), lambda i: (0, 0)),
                  pl.BlockSpec((1, N_EXPERTS), lambda i: (0, 0))],
        out_specs=pl.BlockSpec((tr, TOP_K), lambda i: (i, 0)),
        out_shape=jax.ShapeDtypeStruct((n, TOP_K), jnp.int32),
        scratch_shapes=[pltpu.VMEM((1, N_EXPERTS), F32)],
        compiler_params=_cparams(("arbitrary",)),
        name="moe_rank",
    )(idx, ltri, offs_row)


ROW_UNROLL = 8


def _row(r, c):
    return pl.ds(pl.multiple_of(r * c, c), c)


def _dispatch_kernel(slot_ref, m_ref, xs_ref, buf_ref, load_sems, scat_sems, *, c, td):
    i = pl.program_id(0)
    n_steps = pl.num_programs(0)
    rows = td * c

    def load(tile, slot):
        return pltpu.make_async_copy(m_ref.at[pl.ds(pl.multiple_of(tile * rows, rows), rows), :],
                                     buf_ref.at[slot], load_sems.at[slot])

    def wait_scatters(half):
        for kk in range(TOP_K):
            pltpu.make_async_copy(buf_ref.at[0], xs_ref.at[pl.ds(0, rows), :], scat_sems.at[half]).wait()

    @pl.when(i == 0)
    def _():
        load(0, 0).start()

        @pl.when(n_steps > 1)
        def _():
            load(1, 1).start()

    slot = i % 3
    half = i % 2
    load(i, slot).wait()

    def issue(g, carry):
        for u in range(ROW_UNROLL):
            t = g * ROW_UNROLL + u
            for kk in range(TOP_K):
                dest = slot_ref[t * TOP_K + kk]
                pltpu.make_async_copy(buf_ref.at[slot, _row(t, c), :], xs_ref.at[_row(dest, c), :],
                                      scat_sems.at[half]).start(priority=kk % 2)
        return carry

    lax.fori_loop(0, td // ROW_UNROLL, issue, 0)

    @pl.when(i > 0)
    def _():
        wait_scatters(1 - half)

    @pl.when(i + 2 < n_steps)
    def _():
        load(i + 2, (i + 2) % 3).start()

    @pl.when(i == n_steps - 1)
    def _():
        wait_scatters(half)


def _dispatch(slot_flat, mrows, c, p_rows, td):
    n = mrows.shape[0] // c
    assert td % ROW_UNROLL == 0
    return pl.pallas_call(
        functools.partial(_dispatch_kernel, c=c, td=td),
        grid=(n // td,),
        in_specs=[
            pl.BlockSpec((td * TOP_K,), lambda i: (i,), memory_space=pltpu.SMEM),
            pl.BlockSpec(memory_space=pl.ANY),
        ],
        out_specs=pl.BlockSpec(memory_space=pl.ANY),
        scratch_shapes=[pltpu.VMEM((3, td * c, V7X_LANES), F32), pltpu.SemaphoreType.DMA((3,)),
                        pltpu.SemaphoreType.DMA((2,))],
        out_shape=jax.ShapeDtypeStruct((p_rows * c, V7X_LANES), F32),
        compiler_params=_cparams(("arbitrary",), has_side_effects=True),
        name="moe_dispatch",
    )(slot_flat, mrows)


def _ffn_kernel(be_ref, nu_ref, chg_ref, par_ref, nxt_ref, x_ref, w1_ref, b1_ref, w2_ref, b2_ref, y_ref,
                w1f_ref, w2f_ref, w1s_ref, w2s_ref, sems, *, d_ff):
    i = pl.program_id(0)

    def weight_copies(e, slot):
        return (pltpu.make_async_copy(w1_ref.at[e], w1f_ref.at[slot], sems.at[0, slot]),
                pltpu.make_async_copy(w2_ref.at[e], w2f_ref.at[slot], sems.at[1, slot]))

    @pl.when(i == 0)
    def _():
        for cp in weight_copies(be_ref[0], 0):
            cp.start()

    @pl.when(chg_ref[i] == 1)
    def _():
        slot = par_ref[i]
        for cp in weight_copies(be_ref[i], slot):
            cp.wait()
        w1s_ref[...] = w1f_ref[slot].astype(BF16)
        w2s_ref[...] = w2f_ref[slot].astype(BF16)

        @pl.when(nxt_ref[i] >= 0)
        def _():
            for cp in weight_copies(nxt_ref[i], 1 - slot):
                cp.start()

    @pl.when(i < nu_ref[0])
    def _():
        x = _load_rows(x_ref, w1s_ref.shape[0] // V7X_LANES).astype(BF16)
        hcat = _dot(x, w1s_ref[...]) + b1_ref[...]
        glu = jnp.minimum(hcat[:, :d_ff], SWIGLU_LIMIT)
        lin = jnp.clip(hcat[:, d_ff:], -SWIGLU_LIMIT, SWIGLU_LIMIT)
        act = glu * jax.nn.sigmoid(SWIGLU_ALPHA * glu) * (lin + 1.0)
        _store_rows(y_ref, _dot(act.astype(BF16), w2s_ref[...]) + b2_ref[...])


def _moe_ffn(blk_e, n_used, xrows, w1, b1, w2, b2, bm):
    ne, d, f2 = w1.shape
    d_ff = f2 // 2
    c = d // V7X_LANES
    p_rows = xrows.shape[0] // c
    blk = (bm * c, V7X_LANES)
    n_blk = p_rows // bm
    pos = jnp.arange(n_blk, dtype=jnp.int32)
    prev_e = jnp.concatenate([jnp.full((1,), -1, jnp.int32), blk_e[:-1]])
    chg = ((pos < n_used[0]) & (blk_e != prev_e)).astype(jnp.int32)
    par = ((jnp.cumsum(chg) - 1) % 2).astype(jnp.int32)
    later = jnp.where(chg == 1, pos, n_blk)
    nextpos = jnp.concatenate([lax.cummin(later, reverse=True)[1:], jnp.full((1,), n_blk, jnp.int32)])
    nxt = jnp.where(nextpos < n_blk, blk_e[jnp.minimum(nextpos, n_blk - 1)], -1).astype(jnp.int32)
    grid_spec = pltpu.PrefetchScalarGridSpec(
        num_scalar_prefetch=5,
        grid=(n_blk,),
        in_specs=[
            pl.BlockSpec(blk, lambda i, be, nu, *_: (jnp.minimum(i, nu[0] - 1), 0)),
            pl.BlockSpec(memory_space=pl.ANY),
            pl.BlockSpec((None, 1, f2), lambda i, be, nu, *_: (be[i], 0, 0)),
            pl.BlockSpec(memory_space=pl.ANY),
            pl.BlockSpec((None, 1, d), lambda i, be, nu, *_: (be[i], 0, 0)),
        ],
        out_specs=pl.BlockSpec(blk, lambda i, be, nu, *_: (i, 0)),
        scratch_shapes=[pltpu.VMEM((2, d, f2), F32), pltpu.VMEM((2, d_ff, d), F32),
                        pltpu.VMEM((d, f2), BF16), pltpu.VMEM((d_ff, d), BF16),
                        pltpu.SemaphoreType.DMA((2, 2))],
    )
    return pl.pallas_call(
        functools.partial(_ffn_kernel, d_ff=d_ff),
        grid_spec=grid_spec,
        out_shape=jax.ShapeDtypeStruct(xrows.shape, F32),
        compiler_params=_cparams(("arbitrary",)),
        name="moe_ffn",
    )(blk_e, n_used, chg, par, nxt, xrows, w1, b1.reshape(ne, 1, f2), w2, b2.reshape(ne, 1, d))


def _combine_kernel(slot_ref, next_slot_ref, h_ref, gate_ref, p_ref, wp_ref, gp_ref, wg_ref, y_ref, o_ref,
                    buf_ref, h2_ref, sems):
    i = pl.program_id(0)
    tc, d = h_ref.shape
    c = d // V7X_LANES

    def start_gather(slots, half):
        def issue(g, carry):
            for u in range(ROW_UNROLL):
                t = g * ROW_UNROLL + u
                for kk in range(TOP_K):
                    src = slots[t * TOP_K + kk]
                    pltpu.make_async_copy(y_ref.at[_row(src, c), :], buf_ref.at[half, kk, _row(t, c), :],
                                          sems.at[half]).start(priority=kk % 2)
            return carry

        lax.fori_loop(0, tc // ROW_UNROLL, issue, 0)

    cur = i % 2

    @pl.when(i == 0)
    def _():
        start_gather(slot_ref, 0)

    @pl.when(i + 1 < pl.num_programs(0))
    def _():
        start_gather(next_slot_ref, 1 - cur)

    for kk in range(TOP_K):
        pltpu.make_async_copy(y_ref.at[pl.ds(0, tc * c), :], buf_ref.at[cur, kk], sems.at[cur]).wait()

    gate = gate_ref[...]
    for j in range(c):
        cols = slice(j * V7X_LANES, (j + 1) * V7X_LANES)
        acc = h_ref[:, cols]
        for kk in range(TOP_K):
            acc = acc + gate[:, kk:kk + 1] * buf_ref[cur, kk, pl.ds(j, tc, stride=c), :]
        h2_ref[:, cols] = acc

    h2 = h2_ref[...]
    e = _dot(p_ref[...].astype(BF16), wp_ref[...])
    ms = jnp.mean(e * e, axis=-1, keepdims=True)
    e = e * lax.rsqrt(ms + NORM_EPS) * gp_ref[...]
    o_ref[...] = h2 + e * jax.nn.sigmoid(_dot(h2.astype(BF16), wg_ref[...]))


def _combine_ple(slot_flat, h, gates, yrows, p2, wp_bf, gp, wg_bf, tc):
    n, d = h.shape
    pd = p2.shape[1]
    assert tc % ROW_UNROLL == 0
    last = n // tc - 1
    full = lambda i: (0, 0)
    row = lambda i: (i, 0)
    return pl.pallas_call(
        _combine_kernel,
        grid=(n // tc,),
        in_specs=[
            pl.BlockSpec((tc * TOP_K,), lambda i: (i,), memory_space=pltpu.SMEM),
            pl.BlockSpec((tc * TOP_K,), lambda i: (jnp.minimum(i + 1, last),), memory_space=pltpu.SMEM),
            pl.BlockSpec((tc, d), row),
            pl.BlockSpec((tc, TOP_K), row),
            pl.BlockSpec((tc, pd), row),
            pl.BlockSpec((pd, d), full),
            pl.BlockSpec((1, d), full),
            pl.BlockSpec((d, d), full),
            pl.BlockSpec(memory_space=pl.ANY),
        ],
        out_specs=pl.BlockSpec((tc, d), row),
        scratch_shapes=[pltpu.VMEM((2, TOP_K, tc * d // V7X_LANES, V7X_LANES), F32),
                        pltpu.VMEM((tc, d), F32),
                        pltpu.SemaphoreType.DMA((2,))],
        out_shape=jax.ShapeDtypeStruct((n, d), F32),
        compiler_params=_cparams(("arbitrary",)),
        name="moe_combine_ple",
    )(slot_flat, slot_flat, h, gates, p2, wp_bf, gp, wg_bf, yrows)


def _tile(n, want):
    t = min(n, want)
    assert n % t == 0, (n, t)
    return t


def _layer(h3, p3, lambda_init, norm_mix_g, w_in, rw_mu, rw_w0, rw_w2, rw_a0, rw_a2, rw_g2, rw_k_k, rw_k_a, rw_r_k,
           rw_ln_g, rw_ln_b, da_q_norm_g, da_k_norm_g, da_lq1, da_lk1, da_lq2, da_lk2, da_subln_g, w_out,
           norm_ffn_g, w_router, b_router, w1, b1, w2, b2, w_ple, ple_norm_g, w_ple_gate):
    b, t, d = h3.shape
    n = b * t
    c_rw = rw_k_k.shape[0]
    lora = rw_w2.shape[1]
    rw_cols = rw_mu.shape[0]
    rw_heads = rw_r_k.shape[0]
    c_da = w_in.shape[1] - rw_cols
    assert c_da % 3 == 0
    c_da //= 3
    qk_dim = da_q_norm_g.shape[0]
    v_dim = da_subln_g.shape[0]
    da_heads = c_da // v_dim
    assert rw_heads * RW_HEAD == c_rw and c_rw % HEAD_GROUP == 0 and v_dim == V7X_LANES and 2 * qk_dim == v_dim
    assert 2 * lora == V7X_LANES and w_router.shape[1] == N_EXPERTS

    x2 = h3.reshape(n, d)
    row = lambda a: a.reshape(1, -1).astype(F32)

    gq_t = row(jnp.tile(da_q_norm_g, c_da // qk_dim))
    gk_t = row(jnp.tile(da_k_norm_g, c_da // qk_dim))
    gm_qk = _group_matrix(c_da, qk_dim, 1.0 / qk_dim)
    ta = _tile(t, ATTN_BLOCK)
    zrw, q, k, vt = _inproj(x2, row(norm_mix_g), w_in.astype(BF16), gq_t, gk_t, gm_qk, rw_cols, c_da,
                            1.0 / math.sqrt(qk_dim), _tile(ta, ROW_TILE), b, da_heads, ta)

    zeros = jnp.zeros((lora, c_rw), F32)
    w2bd = jnp.concatenate([jnp.concatenate([rw_w2[0], zeros], axis=1),
                            jnp.concatenate([zeros, rw_w2[1]], axis=1)], axis=0).astype(BF16)
    a2bd = jnp.concatenate([jnp.concatenate([rw_a2[0], zeros], axis=1),
                            jnp.concatenate([zeros, rw_a2[1]], axis=1)], axis=0).astype(BF16)
    gsum = _group_matrix(c_rw, RW_HEAD, 1.0)
    r, vv, nkk, lw, kd, bd, g, bonus = _rwkv_prep(
        zrw.reshape(b, t, rw_cols), row(rw_mu), w2bd, row(rw_w0), a2bd, row(rw_a0), rw_g2.astype(BF16),
        row(rw_k_k), row(rw_k_a), row(rw_r_k), gsum, c_rw, lora, _tile(t, ROW_TILE))
    yf, yb = _rwkv_scan(r, vv, nkk, lw, kd, bd, _tile(t, SCAN_TILE))

    slopes = 2.0 ** (-8.0 * jnp.arange(1, da_heads + 1, dtype=F32) / da_heads)
    q3 = q.reshape(b, t, c_da)
    k3 = k.reshape(b, t, c_da)
    small = (row(da_lq1), row(da_lk1), row(da_lq2), row(da_lk2), row(da_subln_g))

    def attention_running_max():
        v3 = vt[:, :, :, :v_dim, :].transpose(0, 2, 4, 1, 3).reshape(b, t, c_da)
        return _attention(slopes, q3, k3, v3, *small, da_heads, v_dim, lambda_init, ta, ta)

    score_bound = 1.01 * math.sqrt(qk_dim) * jnp.max(jnp.abs(da_q_norm_g)) * jnp.max(jnp.abs(da_k_norm_g))
    mb = jnp.ceil(2.0 * score_bound.astype(F32)) * 0.5
    o_da = lax.cond(
        mb <= MAX_FIXED_SHIFT,
        lambda: _attention_fixed(mb, q3, k3, vt, *small, da_heads, v_dim, lambda_init, ta),
        lambda: attention_running_max().astype(BF16))

    gmean = _group_matrix(c_rw, RW_HEAD, 1.0 / RW_HEAD)
    w_out_bf = w_out.astype(BF16)
    h1, m, idx, gates, counts = _post(
        yf.reshape(n, c_rw), yb.reshape(n, c_rw), bonus.reshape(n, c_rw), g.reshape(n, c_rw), o_da.reshape(n, c_da), x2,
        row(rw_ln_g), row(rw_ln_b), gmean, w_out_bf[:c_rw], w_out_bf[c_rw:], row(norm_ffn_g),
        w_router.astype(F32), row(b_router), RW_HEAD * 1e-5, _tile(n, WIDE_TILE))

    bm = MOE_BLOCK_ROWS
    counts = counts.reshape(N_EXPERTS).astype(jnp.int32)
    padded = (counts + bm - 1) // bm * bm
    pend = jnp.cumsum(padded)
    offs = (pend - padded).astype(jnp.int32)
    slot = _moe_rank(idx, offs.astype(F32).reshape(1, N_EXPERTS), _tile(n, ROW_TILE))
    n_blk = (n * TOP_K) // bm + N_EXPERTS
    blk_start = jnp.arange(n_blk, dtype=jnp.int32) * bm
    blk_e = jnp.minimum(jnp.sum(blk_start[:, None] >= pend[None, :], axis=1), N_EXPERTS - 1).astype(jnp.int32)
    n_used = (pend[-1] // bm).astype(jnp.int32).reshape(1)
    slot_flat = slot.reshape(n * TOP_K)
    xs = _dispatch(slot_flat, m, d // V7X_LANES, n_blk * bm, _tile(n, DISPATCH_TILE))
    ys = _moe_ffn(blk_e, n_used, xs, w1, b1, w2, b2, bm)
    out = _combine_ple(slot_flat, h1, gates, ys, p3.reshape(n, -1), w_ple.astype(BF16), row(ple_norm_g),
                       w_ple_gate.astype(BF16), _tile(n, COMBINE_TILE))
    return out.reshape(b, t, d)


def kernel(x, p, norm_mix_g, w_in, rw_mu, rw_w0, rw_w2, rw_a0, rw_a2, rw_g2, rw_k_k, rw_k_a, rw_r_k, rw_ln_g,
           rw_ln_b, da_q_norm_g, da_k_norm_g, da_lq1, da_lk1, da_lq2, da_lk2, da_subln_g, w_out, norm_ffn_g,
           w_router, b_router, w1, b1, w2, b2, w_ple, ple_norm_g, w_ple_gate):
    h = x.astype(F32)
    params = (norm_mix_g, w_in, rw_mu, rw_w0, rw_w2, rw_a0, rw_a2, rw_g2, rw_k_k, rw_k_a, rw_r_k, rw_ln_g, rw_ln_b,
              da_q_norm_g, da_k_norm_g, da_lq1, da_lk1, da_lq2, da_lk2, da_subln_g, w_out, norm_ffn_g,
              w_router, b_router, w1, b1, w2, b2, w_ple, ple_norm_g, w_ple_gate)
    for i in range(p.shape[0]):
        lambda_init = 0.8 - 0.6 * math.exp(-0.3 * i)
        h = _layer(h, p[i], lambda_init, *(a[i] for a in params))
    return h.astype(x.dtype)
```
